```python
import math
import jax, jax.numpy as jnp
from jax import lax
import numpy as np

D_MODEL = 1024
BATCH = 8
SEQ = 4096
DEPTH = 4

CHUNK = 64
N_A_LAYERS = DEPTH // 2
N_B_LAYERS = DEPTH - N_A_LAYERS
POOL_WINDOWS = (2, 4, 8, 16)
N_POOL_GROUPS = len(POOL_WINDOWS)
POOL_GROUP_DIM = D_MODEL // N_POOL_GROUPS
D_FF = -(-8 * D_MODEL // (3 * 256)) * 256
N_HEADS = 8
QK_NOPE_DIM = 128
QK_ROPE_DIM = 64
QK_HEAD_DIM = QK_NOPE_DIM + QK_ROPE_DIM
V_HEAD_DIM = 128
Q_LORA_RANK = 256
KV_LORA_RANK = 512
ROPE_THETA = 10000.0
Q_BLOCK = 128
EPS = 1e-6

kernel_name = "yoco_pool_mla_hybrid"


def rms_norm(x, g):
    xf = x.astype(jnp.float32)
    y = xf * lax.rsqrt(jnp.mean(xf * xf, axis=-1, keepdims=True) + EPS)
    return (y * g.astype(jnp.float32)).astype(x.dtype)


def swiglu(h, w_gate, w_up, w_down):
    return (jax.nn.silu(h @ w_gate) * (h @ w_up)) @ w_down


def pool_mix(h, w_pool, b_pool, scale):
    B, S, D = h.shape
    hf = h.astype(jnp.float32)
    cs0 = jnp.pad(jnp.cumsum(hf, axis=1), ((0, 0), (1, 0), (0, 0)))
    t1 = jnp.arange(1, S + 1)
    groups = []
    for g, w in enumerate(POOL_WINDOWS):
        sl = slice(g * POOL_GROUP_DIM, (g + 1) * POOL_GROUP_DIM)
        c = cs0[:, :, sl]
        lag = jnp.pad(c, ((0, 0), (w, 0), (0, 0)))[:, : S + 1]
        win = (c - lag)[:, 1:]
        cnt = jnp.minimum(t1, w).astype(jnp.float32)[None, :, None]
        groups.append(win / cnt - hf[:, :, sl])
    d = jnp.stack(groups, axis=2).astype(h.dtype)
    y = jnp.einsum('bsgc,gcd->bsgd', d, w_pool) + b_pool
    return y.reshape(B, S, D) * scale


def rope(x, positions):
    R = x.shape[-1]
    half = R // 2
    inv = ROPE_THETA ** (-jnp.arange(half, dtype=jnp.float32) * 2.0 / R)
    ang = positions.astype(jnp.float32)[..., None] * inv
    cos = jnp.cos(ang)[:, :, None, :]
    sin = jnp.sin(ang)[:, :, None, :]
    xf = x.astype(jnp.float32)
    x1, x2 = xf[..., :half], xf[..., half:]
    return jnp.concatenate([x1 * cos - x2 * sin, x2 * cos + x1 * sin], axis=-1).astype(x.dtype)


def shared_kv(h, positions, w_dkv, g_kv_latent, w_uk, w_uv, g_k):
    B, S, _ = h.shape
    ckv = h @ w_dkv
    c = rms_norm(ckv[..., :KV_LORA_RANK], g_kv_latent)
    k_pe = ckv[..., KV_LORA_RANK:]
    k_nope = (c @ w_uk).reshape(B, S, N_HEADS, QK_NOPE_DIM)
    v = (c @ w_uv).reshape(B, S, N_HEADS, V_HEAD_DIM)
    k = jnp.concatenate([k_nope, jnp.broadcast_to(k_pe[:, :, None, :], (B, S, N_HEADS, QK_ROPE_DIM))], axis=-1)
    k = rms_norm(k, g_k)
    k = jnp.concatenate([k[..., :QK_NOPE_DIM], rope(k[..., QK_NOPE_DIM:], positions)], axis=-1)
    return k, v


def mla_queries(h, positions, w_dq, g_q_latent, w_uq, g_q):
    B, S, _ = h.shape
    cq = rms_norm(h @ w_dq, g_q_latent)
    q = (cq @ w_uq).reshape(B, S, N_HEADS, QK_HEAD_DIM)
    q = rms_norm(q, g_q)
    return jnp.concatenate([q[..., :QK_NOPE_DIM], rope(q[..., QK_NOPE_DIM:], positions)], axis=-1)


def chunk_causal_attention(q, k, v):
    S = q.shape[1]
    scale = 1.0 / math.sqrt(QK_HEAD_DIM)
    outs = []
    for i in range(S // Q_BLOCK):
        start, end = i * Q_BLOCK, (i + 1) * Q_BLOCK
        s = jnp.einsum('bqhd,bkhd->bhqk', q[:, start:end], k[:, :end]).astype(jnp.float32) * scale
        qc = (start + jnp.arange(Q_BLOCK)) // CHUNK
        kc = jnp.arange(end) // CHUNK
        mask = kc[None, :] <= qc[:, None]
        s = jnp.where(mask[None, None], s, jnp.float32(-1e30))
        p = jax.nn.softmax(s, axis=-1).astype(v.dtype)
        outs.append(jnp.einsum('bhqk,bkhd->bqhd', p, v[:, :end]))
    return jnp.concatenate(outs, axis=1)


def _fwd_setup_inputs(seed: int = 0) -> dict:
    key = jax.random.key(seed)
    ks = iter(jax.random.split(key, 32))
    f32 = jnp.float32

    def w(shape, fan_in):
        return jax.random.normal(next(ks), shape, f32) * fan_in ** -0.5

    def gain(shape):
        return 1.0 + 0.02 * jax.random.normal(next(ks), shape, f32)

    x = jax.random.normal(next(ks), (BATCH, SEQ, D_MODEL), f32)
    positions = jnp.broadcast_to(jnp.arange(SEQ, dtype=jnp.int32)[None, :], (BATCH, SEQ)).astype(jnp.int32)
    return {
        "x": x,
        "positions": positions,
        "ln_mix_a": gain((N_A_LAYERS, D_MODEL)),
        "w_pool": w((N_A_LAYERS, N_POOL_GROUPS, POOL_GROUP_DIM, POOL_GROUP_DIM), POOL_GROUP_DIM),
        "b_pool": 0.02 * jax.random.normal(next(ks), (N_A_LAYERS, N_POOL_GROUPS, POOL_GROUP_DIM), f32),
        "pool_scale": gain((N_A_LAYERS, D_MODEL)),
        "ln_ffn": gain((DEPTH, D_MODEL)),
        "w_gate": w((DEPTH, D_MODEL, D_FF), D_MODEL),
        "w_up": w((DEPTH, D_MODEL, D_FF), D_MODEL),
        "w_down": w((DEPTH, D_FF, D_MODEL), D_FF),
        "ln_kv": gain((D_MODEL,)),
        "w_dkv": w((D_MODEL, KV_LORA_RANK + QK_ROPE_DIM), D_MODEL),
        "g_kv_latent": gain((KV_LORA_RANK,)),
        "w_uk": w((KV_LORA_RANK, N_HEADS * QK_NOPE_DIM), KV_LORA_RANK),
        "w_uv": w((KV_LORA_RANK, N_HEADS * V_HEAD_DIM), KV_LORA_RANK),
        "g_k": gain((QK_HEAD_DIM,)),
        "ln_mix_b": gain((N_B_LAYERS, D_MODEL)),
        "w_dq": w((N_B_LAYERS, D_MODEL, Q_LORA_RANK), D_MODEL),
        "g_q_latent": gain((N_B_LAYERS, Q_LORA_RANK)),
        "w_uq": w((N_B_LAYERS, Q_LORA_RANK, N_HEADS * QK_HEAD_DIM), Q_LORA_RANK),
        "g_q": gain((N_B_LAYERS, QK_HEAD_DIM)),
        "w_o": w((N_B_LAYERS, N_HEADS * V_HEAD_DIM, D_MODEL), N_HEADS * V_HEAD_DIM),
    }


def _fwd_reference(x, positions, ln_mix_a, w_pool, b_pool, pool_scale, ln_ffn, w_gate, w_up, w_down,
              ln_kv, w_dkv, g_kv_latent, w_uk, w_uv, g_k,
              ln_mix_b, w_dq, g_q_latent, w_uq, g_q, w_o):
    B, S, D = x.shape
    k_sh, v_sh = None, None
    for l in range(DEPTH):
        if l < N_A_LAYERS:
            x = x + pool_mix(rms_norm(x, ln_mix_a[l]), w_pool[l], b_pool[l], pool_scale[l])
        else:
            j = l - N_A_LAYERS
            q = mla_queries(rms_norm(x, ln_mix_b[j]), positions, w_dq[j], g_q_latent[j], w_uq[j], g_q[j])
            o = chunk_causal_attention(q, k_sh, v_sh)
            x = x + o.reshape(B, S, N_HEADS * V_HEAD_DIM) @ w_o[j]
        x = x + swiglu(rms_norm(x, ln_ffn[l]), w_gate[l], w_up[l], w_down[l])
        if l == N_A_LAYERS - 1:
            k_sh, v_sh = shared_kv(rms_norm(x, ln_kv), positions, w_dkv, g_kv_latent, w_uk, w_uv, g_k)
    return x


import jax as _jax
import jax.numpy as _jnp

TWIN_FORMAT = 'train_step'
FWD_PARAMS = ['x', 'positions', 'ln_mix_a', 'w_pool', 'b_pool', 'pool_scale', 'ln_ffn', 'w_gate', 'w_up', 'w_down', 'ln_kv', 'w_dkv', 'g_kv_latent', 'w_uk', 'w_uv', 'g_k', 'ln_mix_b', 'w_dq', 'g_q_latent', 'w_uq', 'g_q', 'w_o']
TWIN_WEIGHTS = ['ln_mix_a', 'w_pool', 'b_pool', 'pool_scale', 'ln_ffn', 'w_gate', 'w_up', 'w_down', 'ln_kv', 'w_dkv', 'g_kv_latent', 'w_uk', 'w_uv', 'g_k', 'ln_mix_b', 'w_dq', 'g_q_latent', 'w_uq', 'g_q', 'w_o']
TWIN_DIFF_INPUT = 'x'
TWIN_INPUTS = ['x', 'positions', 'ln_mix_a', 'w_pool', 'b_pool', 'pool_scale', 'ln_ffn', 'w_gate', 'w_up', 'w_down', 'ln_kv', 'w_dkv', 'g_kv_latent', 'w_uk', 'w_uv', 'g_k', 'ln_mix_b', 'w_dq', 'g_q_latent', 'w_uq', 'g_q', 'w_o', 'loss_target', 'm_ln_mix_a', 'm_w_pool', 'm_b_pool', 'm_pool_scale', 'm_ln_ffn', 'm_w_gate', 'm_w_up', 'm_w_down', 'm_ln_kv', 'm_w_dkv', 'm_g_kv_latent', 'm_w_uk', 'm_w_uv', 'm_g_k', 'm_ln_mix_b', 'm_w_dq', 'm_g_q_latent', 'm_w_uq', 'm_g_q', 'm_w_o', 'v_ln_mix_a', 'v_w_pool', 'v_b_pool', 'v_pool_scale', 'v_ln_ffn', 'v_w_gate', 'v_w_up', 'v_w_down', 'v_ln_kv', 'v_w_dkv', 'v_g_kv_latent', 'v_w_uk', 'v_w_uv', 'v_g_k', 'v_ln_mix_b', 'v_w_dq', 'v_g_q_latent', 'v_w_uq', 'v_g_q', 'v_w_o']
TWIN_OUTPUTS = ['loss', 'grad_x', 'grad_ln_mix_a', 'grad_w_pool', 'grad_b_pool', 'grad_pool_scale', 'grad_ln_ffn', 'grad_w_gate', 'grad_w_up', 'grad_w_down', 'grad_ln_kv', 'grad_w_dkv', 'grad_g_kv_latent', 'grad_w_uk', 'grad_w_uv', 'grad_g_k', 'grad_ln_mix_b', 'grad_w_dq', 'grad_g_q_latent', 'grad_w_uq', 'grad_g_q', 'grad_w_o', 'delta_ln_mix_a', 'delta_w_pool', 'delta_b_pool', 'delta_pool_scale', 'delta_ln_ffn', 'delta_w_gate', 'delta_w_up', 'delta_w_down', 'delta_ln_kv', 'delta_w_dkv', 'delta_g_kv_latent', 'delta_w_uk', 'delta_w_uv', 'delta_g_k', 'delta_ln_mix_b', 'delta_w_dq', 'delta_g_q_latent', 'delta_w_uq', 'delta_g_q', 'delta_w_o', 'new_m_ln_mix_a', 'new_m_w_pool', 'new_m_b_pool', 'new_m_pool_scale', 'new_m_ln_ffn', 'new_m_w_gate', 'new_m_w_up', 'new_m_w_down', 'new_m_ln_kv', 'new_m_w_dkv', 'new_m_g_kv_latent', 'new_m_w_uk', 'new_m_w_uv', 'new_m_g_k', 'new_m_ln_mix_b', 'new_m_w_dq', 'new_m_g_q_latent', 'new_m_w_uq', 'new_m_g_q', 'new_m_w_o', 'new_v_ln_mix_a', 'new_v_w_pool', 'new_v_b_pool', 'new_v_pool_scale', 'new_v_ln_ffn', 'new_v_w_gate', 'new_v_w_up', 'new_v_w_down', 'new_v_ln_kv', 'new_v_w_dkv', 'new_v_g_kv_latent', 'new_v_w_uk', 'new_v_w_uv', 'new_v_g_k', 'new_v_ln_mix_b', 'new_v_w_dq', 'new_v_g_q_latent', 'new_v_w_uq', 'new_v_g_q', 'new_v_w_o']
TWIN_LEAF_KINDS = {'loss': 'loss', 'grad_x': 'grad_x', 'grad_ln_mix_a': 'grad_w', 'grad_w_pool': 'grad_w', 'grad_b_pool': 'grad_w', 'grad_pool_scale': 'grad_w', 'grad_ln_ffn': 'grad_w', 'grad_w_gate': 'grad_w', 'grad_w_up': 'grad_w', 'grad_w_down': 'grad_w', 'grad_ln_kv': 'grad_w', 'grad_w_dkv': 'grad_w', 'grad_g_kv_latent': 'grad_w', 'grad_w_uk': 'grad_w', 'grad_w_uv': 'grad_w', 'grad_g_k': 'grad_w', 'grad_ln_mix_b': 'grad_w', 'grad_w_dq': 'grad_w', 'grad_g_q_latent': 'grad_w', 'grad_w_uq': 'grad_w', 'grad_g_q': 'grad_w', 'grad_w_o': 'grad_w', 'delta_ln_mix_a': 'delta_w', 'delta_w_pool': 'delta_w', 'delta_b_pool': 'delta_w', 'delta_pool_scale': 'delta_w', 'delta_ln_ffn': 'delta_w', 'delta_w_gate': 'delta_w', 'delta_w_up': 'delta_w', 'delta_w_down': 'delta_w', 'delta_ln_kv': 'delta_w', 'delta_w_dkv': 'delta_w', 'delta_g_kv_latent': 'delta_w', 'delta_w_uk': 'delta_w', 'delta_w_uv': 'delta_w', 'delta_g_k': 'delta_w', 'delta_ln_mix_b': 'delta_w', 'delta_w_dq': 'delta_w', 'delta_g_q_latent': 'delta_w', 'delta_w_uq': 'delta_w', 'delta_g_q': 'delta_w', 'delta_w_o': 'delta_w', 'new_m_ln_mix_a': 'new_m', 'new_m_w_pool': 'new_m', 'new_m_b_pool': 'new_m', 'new_m_pool_scale': 'new_m', 'new_m_ln_ffn': 'new_m', 'new_m_w_gate': 'new_m', 'new_m_w_up': 'new_m', 'new_m_w_down': 'new_m', 'new_m_ln_kv': 'new_m', 'new_m_w_dkv': 'new_m', 'new_m_g_kv_latent': 'new_m', 'new_m_w_uk': 'new_m', 'new_m_w_uv': 'new_m', 'new_m_g_k': 'new_m', 'new_m_ln_mix_b': 'new_m', 'new_m_w_dq': 'new_m', 'new_m_g_q_latent': 'new_m', 'new_m_w_uq': 'new_m', 'new_m_g_q': 'new_m', 'new_m_w_o': 'new_m', 'new_v_ln_mix_a': 'new_v', 'new_v_w_pool': 'new_v', 'new_v_b_pool': 'new_v', 'new_v_pool_scale': 'new_v', 'new_v_ln_ffn': 'new_v', 'new_v_w_gate': 'new_v', 'new_v_w_up': 'new_v', 'new_v_w_down': 'new_v', 'new_v_ln_kv': 'new_v', 'new_v_w_dkv': 'new_v', 'new_v_g_kv_latent': 'new_v', 'new_v_w_uk': 'new_v', 'new_v_w_uv': 'new_v', 'new_v_g_k': 'new_v', 'new_v_ln_mix_b': 'new_v', 'new_v_w_dq': 'new_v', 'new_v_g_q_latent': 'new_v', 'new_v_w_uq': 'new_v', 'new_v_g_q': 'new_v', 'new_v_w_o': 'new_v'}


def _forward(args):
    return _fwd_reference(*[args[k] for k in FWD_PARAMS])


def _output_shape():
    def fwd():
        inp = _fwd_setup_inputs(0)
        return _fwd_reference(*[inp[k] for k in FWD_PARAMS])
    out = _jax.eval_shape(fwd)
    return out.shape, out.dtype

N_MICROBATCH = 1
ADAM_LR = 0.001
ADAM_B1 = 0.9
ADAM_B2 = 0.999
ADAM_EPS = 1e-08
ADAM_WD = 0.01
ADAM_STEP = 10
PER_EXAMPLE_BATCH_AXIS = {'x': 0, 'positions': 0, 'loss_target': 0}
SHARED_INPUTS = []
_WEIGHT_DTYPES = {'ln_mix_a': _jnp.float32, 'w_pool': _jnp.float32, 'b_pool': _jnp.float32, 'pool_scale': _jnp.float32, 'ln_ffn': _jnp.float32, 'w_gate': _jnp.float32, 'w_up': _jnp.float32, 'w_down': _jnp.float32, 'ln_kv': _jnp.float32, 'w_dkv': _jnp.float32, 'g_kv_latent': _jnp.float32, 'w_uk': _jnp.float32, 'w_uv': _jnp.float32, 'g_k': _jnp.float32, 'ln_mix_b': _jnp.float32, 'w_dq': _jnp.float32, 'g_q_latent': _jnp.float32, 'w_uq': _jnp.float32, 'g_q': _jnp.float32, 'w_o': _jnp.float32}
MOMENT_SCALE = {'ln_mix_a': 2.671904e+01, 'w_pool': 3.214069e+00, 'b_pool': 6.939708e+00, 'pool_scale': 2.673091e+01, 'ln_ffn': 2.476446e+01, 'w_gate': 2.539687e-01, 'w_up': 3.273579e-01, 'w_down': 5.231990e-01, 'ln_kv': 1.838361e-01, 'w_dkv': 2.108926e-01, 'g_kv_latent': 7.778017e-01, 'w_uk': 1.214892e-01, 'w_uv': 1.606226e-01, 'g_k': 1.714365e+00, 'ln_mix_b': 8.370847e-02, 'w_dq': 1.711736e-01, 'g_q_latent': 1.695597e-01, 'w_uq': 7.007408e-02, 'g_q': 8.723972e-01, 'w_o': 1.143199e-01}


def _to_microbatches(a, axis):
    t = _jnp.moveaxis(a, axis, 0)
    t = t.reshape((N_MICROBATCH, t.shape[0] // N_MICROBATCH) + t.shape[1:])
    return _jnp.moveaxis(t, 1, axis + 1)


def setup_inputs(seed: int = 0) -> dict:
    inp = _fwd_setup_inputs(seed)
    key = _jax.random.fold_in(_jax.random.key(seed), 7919)
    shape, _ = _output_shape()
    out = dict(inp)
    out["loss_target"] = _jax.random.normal(_jax.random.fold_in(key, 0), shape, _jnp.float32)
    for i, name in enumerate(TWIN_WEIGHTS):
        w = inp[name].astype(_jnp.float32)
        if MOMENT_SCALE is None:
            s = _jnp.sqrt(_jnp.mean(_jnp.square(w)) + 1e-30)
        else:
            s = MOMENT_SCALE[name]
        km, kv = _jax.random.split(_jax.random.fold_in(key, i + 1))
        out[name] = w
        out["m_" + name] = s * _jax.random.normal(km, w.shape, _jnp.float32)
        out["v_" + name] = (s * s) * _jax.random.uniform(kv, w.shape, _jnp.float32, 0.5, 1.5)
    if N_MICROBATCH > 1:
        for name, axis in PER_EXAMPLE_BATCH_AXIS.items():
            out[name] = _to_microbatches(out[name], axis)
    return {'x': out['x'], 'positions': out['positions'], 'ln_mix_a': out['ln_mix_a'], 'w_pool': out['w_pool'], 'b_pool': out['b_pool'], 'pool_scale': out['pool_scale'], 'ln_ffn': out['ln_ffn'], 'w_gate': out['w_gate'], 'w_up': out['w_up'], 'w_down': out['w_down'], 'ln_kv': out['ln_kv'], 'w_dkv': out['w_dkv'], 'g_kv_latent': out['g_kv_latent'], 'w_uk': out['w_uk'], 'w_uv': out['w_uv'], 'g_k': out['g_k'], 'ln_mix_b': out['ln_mix_b'], 'w_dq': out['w_dq'], 'g_q_latent': out['g_q_latent'], 'w_uq': out['w_uq'], 'g_q': out['g_q'], 'w_o': out['w_o'], 'loss_target': out['loss_target'], 'm_ln_mix_a': out['m_ln_mix_a'], 'm_w_pool': out['m_w_pool'], 'm_b_pool': out['m_b_pool'], 'm_pool_scale': out['m_pool_scale'], 'm_ln_ffn': out['m_ln_ffn'], 'm_w_gate': out['m_w_gate'], 'm_w_up': out['m_w_up'], 'm_w_down': out['m_w_down'], 'm_ln_kv': out['m_ln_kv'], 'm_w_dkv': out['m_w_dkv'], 'm_g_kv_latent': out['m_g_kv_latent'], 'm_w_uk': out['m_w_uk'], 'm_w_uv': out['m_w_uv'], 'm_g_k': out['m_g_k'], 'm_ln_mix_b': out['m_ln_mix_b'], 'm_w_dq': out['m_w_dq'], 'm_g_q_latent': out['m_g_q_latent'], 'm_w_uq': out['m_w_uq'], 'm_g_q': out['m_g_q'], 'm_w_o': out['m_w_o'], 'v_ln_mix_a': out['v_ln_mix_a'], 'v_w_pool': out['v_w_pool'], 'v_b_pool': out['v_b_pool'], 'v_pool_scale': out['v_pool_scale'], 'v_ln_ffn': out['v_ln_ffn'], 'v_w_gate': out['v_w_gate'], 'v_w_up': out['v_w_up'], 'v_w_down': out['v_w_down'], 'v_ln_kv': out['v_ln_kv'], 'v_w_dkv': out['v_w_dkv'], 'v_g_kv_latent': out['v_g_kv_latent'], 'v_w_uk': out['v_w_uk'], 'v_w_uv': out['v_w_uv'], 'v_g_k': out['v_g_k'], 'v_ln_mix_b': out['v_ln_mix_b'], 'v_w_dq': out['v_w_dq'], 'v_g_q_latent': out['v_g_q_latent'], 'v_w_uq': out['v_w_uq'], 'v_g_q': out['v_g_q'], 'v_w_o': out['v_w_o']}


def _loss(weights, diff, rest, loss_target):
    with _jax.named_scope("forward"):
        args = {**rest, TWIN_DIFF_INPUT: diff, **{k: w.astype(_WEIGHT_DTYPES[k]) for k, w in weights.items()}}
        y = _forward(args)
    with _jax.named_scope("loss_head"):
        err = _jnp.square(y.astype(_jnp.float32) - loss_target)
        return 0.5 * _jnp.sum(_jnp.mean(err, axis=-1)) if err.ndim else 0.5 * err


def _adamw(w, g, m, v):
    m = ADAM_B1 * m + (1.0 - ADAM_B1) * g
    v = ADAM_B2 * v + (1.0 - ADAM_B2) * _jnp.square(g)
    m_hat = m / (1.0 - ADAM_B1 ** ADAM_STEP)
    v_hat = v / (1.0 - ADAM_B2 ** ADAM_STEP)
    delta = -ADAM_LR * (m_hat / (_jnp.sqrt(v_hat) + ADAM_EPS) + ADAM_WD * w)
    return delta, m, v


def reference(x, positions, ln_mix_a, w_pool, b_pool, pool_scale, ln_ffn, w_gate, w_up, w_down, ln_kv, w_dkv, g_kv_latent, w_uk, w_uv, g_k, ln_mix_b, w_dq, g_q_latent, w_uq, g_q, w_o, loss_target, m_ln_mix_a, m_w_pool, m_b_pool, m_pool_scale, m_ln_ffn, m_w_gate, m_w_up, m_w_down, m_ln_kv, m_w_dkv, m_g_kv_latent, m_w_uk, m_w_uv, m_g_k, m_ln_mix_b, m_w_dq, m_g_q_latent, m_w_uq, m_g_q, m_w_o, v_ln_mix_a, v_w_pool, v_b_pool, v_pool_scale, v_ln_ffn, v_w_gate, v_w_up, v_w_down, v_ln_kv, v_w_dkv, v_g_kv_latent, v_w_uk, v_w_uv, v_g_k, v_ln_mix_b, v_w_dq, v_g_q_latent, v_w_uq, v_g_q, v_w_o):
    given = dict(x=x, positions=positions, ln_mix_a=ln_mix_a, w_pool=w_pool, b_pool=b_pool, pool_scale=pool_scale, ln_ffn=ln_ffn, w_gate=w_gate, w_up=w_up, w_down=w_down, ln_kv=ln_kv, w_dkv=w_dkv, g_kv_latent=g_kv_latent, w_uk=w_uk, w_uv=w_uv, g_k=g_k, ln_mix_b=ln_mix_b, w_dq=w_dq, g_q_latent=g_q_latent, w_uq=w_uq, g_q=g_q, w_o=w_o, loss_target=loss_target, m_ln_mix_a=m_ln_mix_a, m_w_pool=m_w_pool, m_b_pool=m_b_pool, m_pool_scale=m_pool_scale, m_ln_ffn=m_ln_ffn, m_w_gate=m_w_gate, m_w_up=m_w_up, m_w_down=m_w_down, m_ln_kv=m_ln_kv, m_w_dkv=m_w_dkv, m_g_kv_latent=m_g_kv_latent, m_w_uk=m_w_uk, m_w_uv=m_w_uv, m_g_k=m_g_k, m_ln_mix_b=m_ln_mix_b, m_w_dq=m_w_dq, m_g_q_latent=m_g_q_latent, m_w_uq=m_w_uq, m_g_q=m_g_q, m_w_o=m_w_o, v_ln_mix_a=v_ln_mix_a, v_w_pool=v_w_pool, v_b_pool=v_b_pool, v_pool_scale=v_pool_scale, v_ln_ffn=v_ln_ffn, v_w_gate=v_w_gate, v_w_up=v_w_up, v_w_down=v_w_down, v_ln_kv=v_ln_kv, v_w_dkv=v_w_dkv, v_g_kv_latent=v_g_kv_latent, v_w_uk=v_w_uk, v_w_uv=v_w_uv, v_g_k=v_g_k, v_ln_mix_b=v_ln_mix_b, v_w_dq=v_w_dq, v_g_q_latent=v_g_q_latent, v_w_uq=v_w_uq, v_g_q=v_g_q, v_w_o=v_w_o)
    weights = {n: given[n] for n in TWIN_WEIGHTS}
    shared = {n: given[n] for n in SHARED_INPUTS}
    per_example = {n: given[n] for n in ['x', 'positions']}
    grad_fn = _jax.value_and_grad(_loss, argnums=(0, 1))

    def one_microbatch(ex, loss_target):
        ex = dict(ex)
        diff = ex.pop(TWIN_DIFF_INPUT)
        return grad_fn(weights, diff, {**shared, **ex}, loss_target)

    if N_MICROBATCH == 1:
        loss, (grad_w, grad_x) = one_microbatch(per_example, given["loss_target"])
    else:
        def body(carry, xs):
            loss_sum, grad_sum = carry
            l_k, (gw_k, gx_k) = one_microbatch(xs[0], xs[1])
            with _jax.named_scope("update"):
                return (loss_sum + l_k, _jax.tree.map(_jnp.add, grad_sum, gw_k)), gx_k

        init = (_jnp.zeros((), _jnp.float32), _jax.tree.map(_jnp.zeros_like, weights))
        (loss, grad_w), grad_x = _jax.lax.scan(body, init, (per_example, given["loss_target"]))
    with _jax.named_scope("update"):
        delta_w, new_m, new_v = {}, {}, {}
        for n in TWIN_WEIGHTS:
            delta_w[n], new_m[n], new_v[n] = _adamw(weights[n], grad_w[n], given["m_" + n], given["v_" + n])
    return (loss, grad_x, *[grad_w[n] for n in TWIN_WEIGHTS], *[delta_w[n] for n in TWIN_WEIGHTS],
            *[new_m[n] for n in TWIN_WEIGHTS], *[new_v[n] for n in TWIN_WEIGHTS])
```

```python
import functools
import math

import jax
import jax.numpy as jnp
from jax import lax
from jax.experimental import pallas as pl
from jax.experimental.pallas import tpu as pltpu

F32 = jnp.float32
BF16 = jnp.bfloat16
MXU_DTYPE = BF16
WIRE_DTYPE = BF16

D_MODEL = 1024
N_A = 2
N_B = 2
DEPTH = 4
POOL_WINDOWS = (2, 4, 8, 16)
N_GROUPS = 4
GROUP_DIM = 256
POOL_HALO = 16
N_HEADS = 8
NOPE = 128
ROPE = 64
QK_DIM = 192
HEAD_PAD = 256
V_DIM = 128
Q_LORA = 256
KV_LORA = 512
CKV_PAD = 640
ROPE_THETA = 10000.0
CHUNK = 64
EPS = 1e-6
N_SHARD = 4
FF_SHARD = 704
LANES = 128
SUBLANES = 8
ADAM_LR, ADAM_B1, ADAM_B2, ADAM_EPS, ADAM_WD, ADAM_STEP = 0.001, 0.9, 0.999, 1e-08, 0.01, 10
MESH = pl.DeviceIdType.MESH
ANY = pl.BlockSpec(memory_space=pl.ANY)
VMEM_SPEC = pl.BlockSpec(memory_space=pltpu.VMEM)


def _tile(n, pref):
    if n <= pref:
        return n
    t = pref - pref % SUBLANES
    while n % t:
        t -= SUBLANES
    return t


def _fold8(v):
    r, n = v.shape
    return v.reshape(r // SUBLANES, SUBLANES, n).sum(axis=0)


def _dot(a, b, dims):
    return lax.dot_general(a.astype(MXU_DTYPE), b.astype(MXU_DTYPE), (dims, ((), ())),
                           preferred_element_type=F32)


def _nn(a, b):
    return _dot(a, b, ((1,), (0,)))


def _nt(a, b):
    return _dot(a, b, ((1,), (1,)))


def _tn(a, b):
    return _dot(a, b, ((0,), (0,)))


def _mm(a, b, *, tb=False, resid=None, out_dtype=F32, name, acols=None):
    m = a.shape[0]
    k = a.shape[1] if acols is None else acols[1]
    n = b.shape[0] if tb else b.shape[1]
    tm, tn = _tile(m, 512), _tile(n, 1024)
    ablk = 0 if acols is None else acols[0]

    def body(*refs):
        if resid is None:
            a_ref, b_ref, o_ref = refs
        else:
            a_ref, b_ref, r_ref, o_ref = refs
        acc = _nt(a_ref[...], b_ref[...]) if tb else _nn(a_ref[...], b_ref[...])
        if resid is not None:
            acc = r_ref[...] + acc
        o_ref[...] = acc.astype(o_ref.dtype)

    in_specs = [pl.BlockSpec((tm, k), lambda i, j: (i, ablk)),
                pl.BlockSpec((tn, k), lambda i, j: (j, 0)) if tb else pl.BlockSpec((k, tn), lambda i, j: (0, j))]
    args = [a, b]
    if resid is not None:
        in_specs.append(pl.BlockSpec((tm, tn), lambda i, j: (i, j)))
        args.append(resid)
    return pl.pallas_call(
        body, name=name, grid=(m // tm, n // tn), in_specs=in_specs,
        out_specs=pl.BlockSpec((tm, tn), lambda i, j: (i, j)),
        out_shape=jax.ShapeDtypeStruct((m, n), out_dtype))(*args)


def _mm_tn(a, b, *, name, acols=None, out_dtype=F32):
    m = a.shape[0]
    k1 = a.shape[1] if acols is None else acols[1]
    n = b.shape[1]
    tm, tn = _tile(m, 512), _tile(n, 1024)
    ablk = 0 if acols is None else acols[0]
    nm = m // tm

    def body(a_ref, b_ref, o_ref, acc):
        i = pl.program_id(1)

        @pl.when(i == 0)
        def _():
            acc[...] = jnp.zeros_like(acc)

        acc[...] += _tn(a_ref[...], b_ref[...])

        @pl.when(i == nm - 1)
        def _():
            o_ref[...] = acc[...].astype(o_ref.dtype)

    return pl.pallas_call(
        body, name=name, grid=(n // tn, nm),
        in_specs=[pl.BlockSpec((tm, k1), lambda j, i: (i, ablk)), pl.BlockSpec((tm, tn), lambda j, i: (i, j))],
        out_specs=pl.BlockSpec((k1, tn), lambda j, i: (0, j)),
        out_shape=jax.ShapeDtypeStruct((k1, n), out_dtype),
        scratch_shapes=[pltpu.VMEM((k1, tn), F32)])(a, b)


def _rms_fwd(x, g, *, n, n_valid=None, name):
    out_dtype = MXU_DTYPE
    rows = x.shape[0]
    tm = _tile(rows, 512)
    inv_n = 1.0 / (n_valid or n)

    def body(x_ref, g_ref, o_ref):
        xv = x_ref[...]
        r = lax.rsqrt(jnp.sum(xv * xv, axis=-1, keepdims=True) * inv_n + EPS)
        o_ref[...] = (xv * r * g_ref[...]).astype(o_ref.dtype)

    return pl.pallas_call(
        body, name=name, grid=(rows // tm,),
        in_specs=[pl.BlockSpec((tm, n), lambda i: (i, 0)), pl.BlockSpec((1, n), lambda i: (0, 0))],
        out_specs=pl.BlockSpec((tm, n), lambda i: (i, 0)),
        out_shape=jax.ShapeDtypeStruct((rows, n), out_dtype))(x, g)


def _rms_bwd_math(xv, gv, dyv, inv_n):
    r = lax.rsqrt(jnp.sum(xv * xv, axis=-1, keepdims=True) * inv_n + EPS)
    xh = xv * r
    gy = dyv * gv
    dx = r * (gy - xh * (jnp.sum(gy * xh, axis=-1, keepdims=True) * inv_n))
    return dx, dyv * xh


def _rms_bwd(x, g, dy, *, n, dx_in=None, name):
    rows = x.shape[0]
    tm = _tile(rows, 512)
    inv_n = 1.0 / n

    def body(*refs):
        if dx_in is None:
            x_ref, g_ref, dy_ref, dx_ref, dg_ref = refs
        else:
            x_ref, g_ref, dy_ref, din_ref, dx_ref, dg_ref = refs
        dx, dgc = _rms_bwd_math(x_ref[...], g_ref[...], dy_ref[...], inv_n)
        if dx_in is not None:
            dx = din_ref[...] + dx
        dx_ref[...] = dx

        @pl.when(pl.program_id(0) == 0)
        def _():
            dg_ref[...] = jnp.zeros_like(dg_ref)

        dg_ref[...] += _fold8(dgc)

    row_spec = pl.BlockSpec((tm, n), lambda i: (i, 0))
    in_specs = [row_spec, pl.BlockSpec((1, n), lambda i: (0, 0)), row_spec]
    args = [x, g, dy]
    if dx_in is not None:
        in_specs.append(row_spec)
        args.append(dx_in)
    return pl.pallas_call(
        body, name=name, grid=(rows // tm,), in_specs=in_specs,
        out_specs=[row_spec, pl.BlockSpec((SUBLANES, n), lambda i: (0, 0))],
        out_shape=[jax.ShapeDtypeStruct((rows, n), F32), jax.ShapeDtypeStruct((SUBLANES, n), F32)])(*args)


def _pool_counts(t0, tm, w):
    t = t0 + lax.broadcasted_iota(jnp.int32, (tm, 1), 0)
    return jnp.minimum(t + 1, w).astype(F32)


def _rms_pool_fwd(x, g, *, name):
    s, d = x.shape
    tm = _tile(s, 512)
    hb = tm // POOL_HALO

    def body(x_ref, halo_ref, g_ref, o_ref):
        i = pl.program_id(0)
        gv = g_ref[...]

        def norm(v):
            return v * lax.rsqrt(jnp.mean(v * v, axis=-1, keepdims=True) + EPS) * gv

        h = norm(x_ref[...])
        halo = norm(halo_ref[...]) * (i > 0).astype(F32)
        hh = jnp.concatenate([halo, h], axis=0)
        rows = tm + POOL_HALO
        for gi, w in enumerate(POOL_WINDOWS):
            cols = slice(gi * GROUP_DIM, (gi + 1) * GROUP_DIM)
            acc = hh[:, cols]
            k = 1
            while k < w:
                acc = acc + pltpu.roll(acc, k, 0)
                k *= 2
            win = acc[POOL_HALO:rows]
            o_ref[:, cols] = (win / _pool_counts(i * tm, tm, w) - h[:, cols]).astype(o_ref.dtype)

    return pl.pallas_call(
        body, name=name, grid=(s // tm,),
        in_specs=[pl.BlockSpec((tm, d), lambda i: (i, 0)),
                  pl.BlockSpec((POOL_HALO, d), lambda i: (jnp.maximum(i * hb - 1, 0), 0)),
                  pl.BlockSpec((1, d), lambda i: (0, 0))],
        out_specs=pl.BlockSpec((tm, d), lambda i: (i, 0)),
        out_shape=jax.ShapeDtypeStruct((s, d), MXU_DTYPE))(x, x, g)


def _rms_pool_bwd(x, g, dd, dx_in, *, name):
    s, d = x.shape
    tm = _tile(s, 512)
    hb = tm // POOL_HALO
    nt = s // tm

    def body(x_ref, g_ref, dd_ref, halo_ref, din_ref, dx_ref, dg_ref):
        i = pl.program_id(0)
        ddv = dd_ref[...]
        halo = halo_ref[...] * (i < nt - 1).astype(F32)
        rows = tm + POOL_HALO
        parts = []
        for gi, w in enumerate(POOL_WINDOWS):
            cols = slice(gi * GROUP_DIM, (gi + 1) * GROUP_DIM)
            acc = jnp.concatenate([ddv[:, cols] / _pool_counts(i * tm, tm, w), halo[:, cols] * (1.0 / w)], axis=0)
            k = 1
            while k < w:
                acc = acc + pltpu.roll(acc, rows - k, 0)
                k *= 2
            parts.append(acc[0:tm] - ddv[:, cols])
        dh = jnp.concatenate(parts, axis=1)
        dx, dgc = _rms_bwd_math(x_ref[...], g_ref[...], dh, 1.0 / d)
        dx_ref[...] = din_ref[...] + dx

        @pl.when(i == 0)
        def _():
            dg_ref[...] = jnp.zeros_like(dg_ref)

        dg_ref[...] += _fold8(dgc)

    row_spec = pl.BlockSpec((tm, d), lambda i: (i, 0))
    return pl.pallas_call(
        body, name=name, grid=(nt,),
        in_specs=[row_spec, pl.BlockSpec((1, d), lambda i: (0, 0)), row_spec,
                  pl.BlockSpec((POOL_HALO, d), lambda i: (jnp.minimum((i + 1) * hb, s // POOL_HALO - 1), 0)),
                  row_spec],
        out_specs=[row_spec, pl.BlockSpec((SUBLANES, d), lambda i: (0, 0))],
        out_shape=[jax.ShapeDtypeStruct((s, d), F32), jax.ShapeDtypeStruct((SUBLANES, d), F32)])(x, g, dd, dd, dx_in)


def _pool_mm_fwd(dpool, w, b, scale, x, *, name):
    s, d = x.shape
    tm = _tile(s, 512)

    def body(d_ref, w_ref, b_ref, s_ref, x_ref, o_ref):
        for gi in range(N_GROUPS):
            cols = slice(gi * GROUP_DIM, (gi + 1) * GROUP_DIM)
            y = _nn(d_ref[:, cols], w_ref[gi]) + b_ref[:, cols]
            o_ref[:, cols] = x_ref[:, cols] + y * s_ref[:, cols]

    row_spec = pl.BlockSpec((tm, d), lambda i: (i, 0))
    vec_spec = pl.BlockSpec((1, d), lambda i: (0, 0))
    return pl.pallas_call(
        body, name=name, grid=(s // tm,),
        in_specs=[row_spec, pl.BlockSpec((N_GROUPS, GROUP_DIM, GROUP_DIM), lambda i: (0, 0, 0)), vec_spec, vec_spec, row_spec],
        out_specs=row_spec, out_shape=jax.ShapeDtypeStruct((s, d), F32))(dpool, w, b, scale, x)


def _pool_mm_bwd(dpool, w, b, scale, dx, *, name):
    s, d = dx.shape
    tm = _tile(s, 512)

    def body(d_ref, w_ref, b_ref, s_ref, dx_ref, dd_ref, dw_ref, db_ref, ds_ref):
        @pl.when(pl.program_id(0) == 0)
        def _():
            dw_ref[...] = jnp.zeros_like(dw_ref)
            db_ref[...] = jnp.zeros_like(db_ref)
            ds_ref[...] = jnp.zeros_like(ds_ref)

        for gi in range(N_GROUPS):
            cols = slice(gi * GROUP_DIM, (gi + 1) * GROUP_DIM)
            dg = d_ref[:, cols]
            y = _nn(dg, w_ref[gi]) + b_ref[:, cols]
            dxg = dx_ref[:, cols]
            dy = dxg * s_ref[:, cols]
            ds_ref[:, cols] += _fold8(dxg * y)
            db_ref[:, cols] += _fold8(dy)
            dw_ref[gi] += _tn(dg, dy)
            dd_ref[:, cols] = _nt(dy, w_ref[gi])

    row_spec = pl.BlockSpec((tm, d), lambda i: (i, 0))
    vec_spec = pl.BlockSpec((1, d), lambda i: (0, 0))
    w_spec = pl.BlockSpec((N_GROUPS, GROUP_DIM, GROUP_DIM), lambda i: (0, 0, 0))
    part_spec = pl.BlockSpec((SUBLANES, d), lambda i: (0, 0))
    return pl.pallas_call(
        body, name=name, grid=(s // tm,),
        in_specs=[row_spec, w_spec, vec_spec, vec_spec, row_spec],
        out_specs=[row_spec, w_spec, part_spec, part_spec],
        out_shape=[jax.ShapeDtypeStruct((s, d), F32), jax.ShapeDtypeStruct((N_GROUPS, GROUP_DIM, GROUP_DIM), F32),
                   jax.ShapeDtypeStruct((SUBLANES, d), F32), jax.ShapeDtypeStruct((SUBLANES, d), F32)])(dpool, w, b, scale, dx)


def _ffn_up(hf, wg, wu, layer, *, name):
    s, d = hf.shape
    tm = _tile(s, 512)

    def body(h_ref, wg_ref, wu_ref, a_ref, b_ref, u_ref):
        hv = h_ref[...]
        a = _nn(hv, wg_ref[...])
        b = _nn(hv, wu_ref[...])
        a_ref[...] = a
        b_ref[...] = b
        u_ref[...] = (a * (1.0 / (1.0 + jnp.exp(-a))) * b).astype(u_ref.dtype)

    w_spec = pl.BlockSpec((None, None, d, FF_SHARD), lambda j, i: (j, layer, 0, 0))
    h_spec = pl.BlockSpec((None, tm, FF_SHARD), lambda j, i: (j, i, 0))
    hid = (N_SHARD, s, FF_SHARD)
    return pl.pallas_call(
        body, name=name, grid=(N_SHARD, s // tm),
        in_specs=[pl.BlockSpec((tm, d), lambda j, i: (i, 0)), w_spec, w_spec],
        out_specs=[h_spec, h_spec, h_spec],
        out_shape=[jax.ShapeDtypeStruct(hid, F32), jax.ShapeDtypeStruct(hid, F32), jax.ShapeDtypeStruct(hid, MXU_DTYPE)])(hf, wg, wu)


def _ffn_down(u, wd, layer, x, *, name):
    s, d = x.shape
    tm = _tile(s, 1024)

    def body(u_ref, w_ref, x_ref, o_ref):
        j = pl.program_id(1)

        @pl.when(j == 0)
        def _():
            o_ref[...] = x_ref[...]

        o_ref[...] += _nn(u_ref[...], w_ref[...])

    return pl.pallas_call(
        body, name=name, grid=(s // tm, N_SHARD),
        in_specs=[pl.BlockSpec((None, tm, FF_SHARD), lambda i, j: (j, i, 0)),
                  pl.BlockSpec((None, None, FF_SHARD, d), lambda i, j: (j, layer, 0, 0)),
                  pl.BlockSpec((tm, d), lambda i, j: (i, 0))],
        out_specs=pl.BlockSpec((tm, d), lambda i, j: (i, 0)),
        out_shape=jax.ShapeDtypeStruct((s, d), F32))(u, wd, x)


def _ffn_bwd_hidden(dy, wd, layer, a, b, *, name):
    s, d = dy.shape
    tm = _tile(s, 512)

    def body(dy_ref, w_ref, a_ref, b_ref, da_ref, db_ref):
        du = _nt(dy_ref[...], w_ref[...])
        av, bv = a_ref[...], b_ref[...]
        sg = 1.0 / (1.0 + jnp.exp(-av))
        da_ref[...] = (du * bv * (sg * (1.0 + av * (1.0 - sg)))).astype(da_ref.dtype)
        db_ref[...] = (du * (av * sg)).astype(db_ref.dtype)

    h_spec = pl.BlockSpec((None, tm, FF_SHARD), lambda j, i: (j, i, 0))
    hid = jax.ShapeDtypeStruct((N_SHARD, s, FF_SHARD), MXU_DTYPE)
    return pl.pallas_call(
        body, name=name, grid=(N_SHARD, s // tm),
        in_specs=[pl.BlockSpec((tm, d), lambda j, i: (i, 0)),
                  pl.BlockSpec((None, None, FF_SHARD, d), lambda j, i: (j, layer, 0, 0)), h_spec, h_spec],
        out_specs=[h_spec, h_spec], out_shape=[hid, hid])(dy, wd, a, b)


def _ffn_bwd_dwd(u, dy, layer, dwd_all, *, name):
    s, d = dy.shape
    tm = _tile(s, 512)
    nm = s // tm

    def body(u_ref, dy_ref, _, o_ref, acc):
        i = pl.program_id(1)

        @pl.when(i == 0)
        def _():
            acc[...] = jnp.zeros_like(acc)

        acc[...] += _tn(u_ref[...], dy_ref[...])

        @pl.when(i == nm - 1)
        def _():
            o_ref[...] = acc[...].astype(o_ref.dtype)

    return pl.pallas_call(
        body, name=name, grid=(N_SHARD, nm),
        in_specs=[pl.BlockSpec((None, tm, FF_SHARD), lambda j, i: (j, i, 0)), pl.BlockSpec((tm, d), lambda j, i: (i, 0)), ANY],
        out_specs=pl.BlockSpec((None, None, FF_SHARD, d), lambda j, i: (j, layer, 0, 0)),
        out_shape=jax.ShapeDtypeStruct(dwd_all.shape, dwd_all.dtype),
        scratch_shapes=[pltpu.VMEM((FF_SHARD, d), F32)],
        input_output_aliases={2: 0})(u, dy, dwd_all)


def _ffn_bwd_dwgu(hf, da, db, layer, dwg_all, dwu_all, *, name):
    s, d = hf.shape
    tm = _tile(s, 512)
    nm = s // tm

    def body(h_ref, da_ref, db_ref, _g, _u, og_ref, ou_ref, accg, accu):
        i = pl.program_id(1)

        @pl.when(i == 0)
        def _():
            accg[...] = jnp.zeros_like(accg)
            accu[...] = jnp.zeros_like(accu)

        hv = h_ref[...]
        accg[...] += _tn(hv, da_ref[...])
        accu[...] += _tn(hv, db_ref[...])

        @pl.when(i == nm - 1)
        def _():
            og_ref[...] = accg[...].astype(og_ref.dtype)
            ou_ref[...] = accu[...].astype(ou_ref.dtype)

    h_spec = pl.BlockSpec((None, tm, FF_SHARD), lambda j, i: (j, i, 0))
    w_spec = pl.BlockSpec((None, None, d, FF_SHARD), lambda j, i: (j, layer, 0, 0))
    return pl.pallas_call(
        body, name=name, grid=(N_SHARD, nm),
        in_specs=[pl.BlockSpec((tm, d), lambda j, i: (i, 0)), h_spec, h_spec, ANY, ANY],
        out_specs=[w_spec, w_spec],
        out_shape=[jax.ShapeDtypeStruct(dwg_all.shape, dwg_all.dtype), jax.ShapeDtypeStruct(dwu_all.shape, dwu_all.dtype)],
        scratch_shapes=[pltpu.VMEM((d, FF_SHARD), F32), pltpu.VMEM((d, FF_SHARD), F32)],
        input_output_aliases={3: 0, 4: 1})(hf, da, db, dwg_all, dwu_all)


def _ffn_bwd_dh(da, db, wg, wu, layer, *, name):
    s = da.shape[1]
    d = wg.shape[2]
    tm = _tile(s, 1024)

    def body(da_ref, db_ref, wg_ref, wu_ref, o_ref):
        j = pl.program_id(1)

        @pl.when(j == 0)
        def _():
            o_ref[...] = jnp.zeros_like(o_ref)

        o_ref[...] += _nt(da_ref[...], wg_ref[...]) + _nt(db_ref[...], wu_ref[...])

    h_spec = pl.BlockSpec((None, tm, FF_SHARD), lambda i, j: (j, i, 0))
    w_spec = pl.BlockSpec((None, None, d, FF_SHARD), lambda i, j: (j, layer, 0, 0))
    return pl.pallas_call(
        body, name=name, grid=(s // tm, N_SHARD),
        in_specs=[h_spec, h_spec, w_spec, w_spec],
        out_specs=pl.BlockSpec((tm, d), lambda i, j: (i, 0)),
        out_shape=jax.ShapeDtypeStruct((s, d), F32))(da, db, wg, wu)


def _rope_tables(pos, inv, *, name):
    s = pos.shape[0]
    tm = _tile(s, 512)
    half = ROPE // 2

    def body(p_ref, i_ref, c_ref, s_ref):
        ang = p_ref[...] * i_ref[...]
        lane = lax.broadcasted_iota(jnp.int32, ang.shape, 1)
        live = lane < ROPE
        c_ref[...] = jnp.where(live, jnp.cos(ang), 0.0)
        sn = jnp.sin(ang)
        s_ref[...] = jnp.where(live, jnp.where(lane < half, -sn, sn), 0.0)

    out = jax.ShapeDtypeStruct((s, LANES), F32)
    return pl.pallas_call(
        body, name=name, grid=(s // tm,),
        in_specs=[pl.BlockSpec((tm, 1), lambda i: (i, 0)), pl.BlockSpec((1, LANES), lambda i: (0, 0))],
        out_specs=[pl.BlockSpec((tm, LANES), lambda i: (i, 0))] * 2, out_shape=[out, out])(pos, inv)


def _swap_halves(v):
    half = ROPE // 2
    lane = lax.broadcasted_iota(jnp.int32, v.shape, 1)
    return jnp.where(lane < half, pltpu.roll(v, LANES - half, 1), pltpu.roll(v, half, 1))


def _head_norm_rope_fwd(raw, g, cos, sin, *, name):
    s = raw.shape[0]
    tm = _tile(s, 256)
    width = N_HEADS * HEAD_PAD

    def body(x_ref, g_ref, c_ref, s_ref, o_ref):
        cv, sv = c_ref[...], s_ref[...]
        for h in range(N_HEADS):
            lo = h * HEAD_PAD
            xa = x_ref[:, lo:lo + NOPE]
            xb = x_ref[:, lo + NOPE:lo + HEAD_PAD]
            ms = (jnp.sum(xa * xa, axis=-1, keepdims=True) + jnp.sum(xb * xb, axis=-1, keepdims=True)) * (1.0 / QK_DIM)
            r = lax.rsqrt(ms + EPS)
            o_ref[:, lo:lo + NOPE] = (xa * r * g_ref[:, 0:NOPE]).astype(o_ref.dtype)
            yb = xb * r * g_ref[:, NOPE:HEAD_PAD]
            o_ref[:, lo + NOPE:lo + HEAD_PAD] = (yb * cv + _swap_halves(yb) * sv).astype(o_ref.dtype)

    row_spec = pl.BlockSpec((tm, width), lambda i: (i, 0))
    tab_spec = pl.BlockSpec((tm, LANES), lambda i: (i, 0))
    return pl.pallas_call(
        body, name=name, grid=(s // tm,),
        in_specs=[row_spec, pl.BlockSpec((1, HEAD_PAD), lambda i: (0, 0)), tab_spec, tab_spec],
        out_specs=row_spec, out_shape=jax.ShapeDtypeStruct((s, width), MXU_DTYPE))(raw, g, cos, sin)


def _head_norm_rope_bwd(raw, g, cos, sin, dout, *, name):
    s = raw.shape[0]
    tm = _tile(s, 256)
    width = N_HEADS * HEAD_PAD

    def body(x_ref, g_ref, c_ref, s_ref, do_ref, dx_ref, dg_ref):
        @pl.when(pl.program_id(0) == 0)
        def _():
            dg_ref[...] = jnp.zeros_like(dg_ref)

        cv, sv = c_ref[...], s_ref[...]
        ga, gb = g_ref[:, 0:NOPE], g_ref[:, NOPE:HEAD_PAD]
        for h in range(N_HEADS):
            lo = h * HEAD_PAD
            xa = x_ref[:, lo:lo + NOPE]
            xb = x_ref[:, lo + NOPE:lo + HEAD_PAD]
            dya = do_ref[:, lo:lo + NOPE]
            dob = do_ref[:, lo + NOPE:lo + HEAD_PAD]
            dyb = dob * cv + _swap_halves(dob * sv)
            ms = (jnp.sum(xa * xa, axis=-1, keepdims=True) + jnp.sum(xb * xb, axis=-1, keepdims=True)) * (1.0 / QK_DIM)
            r = lax.rsqrt(ms + EPS)
            xha, xhb = xa * r, xb * r
            gya, gyb = dya * ga, dyb * gb
            dot = (jnp.sum(gya * xha, axis=-1, keepdims=True) + jnp.sum(gyb * xhb, axis=-1, keepdims=True)) * (1.0 / QK_DIM)
            dx_ref[:, lo:lo + NOPE] = r * (gya - xha * dot)
            dx_ref[:, lo + NOPE:lo + HEAD_PAD] = r * (gyb - xhb * dot)
            dg_ref[:, 0:NOPE] += _fold8(dya * xha)
            dg_ref[:, NOPE:HEAD_PAD] += _fold8(dyb * xhb)

    row_spec = pl.BlockSpec((tm, width), lambda i: (i, 0))
    tab_spec = pl.BlockSpec((tm, LANES), lambda i: (i, 0))
    return pl.pallas_call(
        body, name=name, grid=(s // tm,),
        in_specs=[row_spec, pl.BlockSpec((1, HEAD_PAD), lambda i: (0, 0)), tab_spec, tab_spec, row_spec],
        out_specs=[row_spec, pl.BlockSpec((SUBLANES, HEAD_PAD), lambda i: (0, 0))],
        out_shape=[jax.ShapeDtypeStruct((s, width), F32), jax.ShapeDtypeStruct((SUBLANES, HEAD_PAD), F32)])(raw, g, cos, sin, dout)


def _k_assemble(kn, ckv, *, name):
    s = kn.shape[0]
    tm = _tile(s, 512)
    width = N_HEADS * HEAD_PAD

    def body(kn_ref, pe_ref, o_ref):
        pe = pe_ref[...]
        for h in range(N_HEADS):
            o_ref[:, h * HEAD_PAD:h * HEAD_PAD + NOPE] = kn_ref[:, h * NOPE:(h + 1) * NOPE]
            o_ref[:, h * HEAD_PAD + NOPE:(h + 1) * HEAD_PAD] = pe

    return pl.pallas_call(
        body, name=name, grid=(s // tm,),
        in_specs=[pl.BlockSpec((tm, N_HEADS * NOPE), lambda i: (i, 0)),
                  pl.BlockSpec((tm, LANES), lambda i: (i, KV_LORA // LANES))],
        out_specs=pl.BlockSpec((tm, width), lambda i: (i, 0)),
        out_shape=jax.ShapeDtypeStruct((s, width), F32))(kn, ckv)


def _k_disassemble(dk_raw, *, name):
    s = dk_raw.shape[0]
    tm = _tile(s, 512)
    width = N_HEADS * HEAD_PAD

    def body(dk_ref, dkn_ref, dpe_ref):
        pe = dk_ref[:, NOPE:HEAD_PAD]
        for h in range(N_HEADS):
            dkn_ref[:, h * NOPE:(h + 1) * NOPE] = dk_ref[:, h * HEAD_PAD:h * HEAD_PAD + NOPE]
            if h:
                pe = pe + dk_ref[:, h * HEAD_PAD + NOPE:(h + 1) * HEAD_PAD]
        dpe_ref[...] = pe

    return pl.pallas_call(
        body, name=name, grid=(s // tm,),
        in_specs=[pl.BlockSpec((tm, width), lambda i: (i, 0))],
        out_specs=[pl.BlockSpec((tm, N_HEADS * NOPE), lambda i: (i, 0)), pl.BlockSpec((tm, LANES), lambda i: (i, 0))],
        out_shape=[jax.ShapeDtypeStruct((s, N_HEADS * NOPE), F32), jax.ShapeDtypeStruct((s, LANES), F32)])(dk_raw)


ATTN_SCALE = 1.0 / math.sqrt(QK_DIM)
MASKED = -1e30


def _attn_scores(q, k, qi, kj, tq, tk):
    sc = _nt(q, k) * ATTN_SCALE
    row = qi * tq + lax.broadcasted_iota(jnp.int32, (tq, tk), 0)
    col = kj * tk + lax.broadcasted_iota(jnp.int32, (tq, tk), 1)
    return jnp.where(col // CHUNK <= row // CHUNK, sc, MASKED)


def _attn_fwd(q, k, v, *, name):
    s = q.shape[0]
    tq = tk = _tile(s, 512)
    nq, nk = s // tq, s // tk

    def last_k(qi):
        return ((qi + 1) * tq - 1) // tk

    def body(q_ref, k_ref, v_ref, o_ref, lse_ref, m_sc, l_sc, acc):
        qi, kj = pl.program_id(1), pl.program_id(2)

        @pl.when(kj == 0)
        def _():
            m_sc[...] = jnp.full_like(m_sc, MASKED)
            l_sc[...] = jnp.zeros_like(l_sc)
            acc[...] = jnp.zeros_like(acc)

        @pl.when(kj <= last_k(qi))
        def _():
            sc = _attn_scores(q_ref[...], k_ref[...], qi, kj, tq, tk)
            m_prev = m_sc[...]
            m_new = jnp.maximum(m_prev, jnp.max(sc, axis=-1, keepdims=True))
            alpha = jnp.exp(m_prev - m_new)
            p = jnp.exp(sc - m_new)
            l_sc[...] = alpha * l_sc[...] + jnp.sum(p, axis=-1, keepdims=True)
            acc[...] = alpha * acc[...] + _nn(p, v_ref[...])
            m_sc[...] = m_new

        @pl.when(kj == nk - 1)
        def _():
            o_ref[...] = acc[...] / l_sc[...]
            lse_ref[...] = m_sc[...] + jnp.log(l_sc[...])

    return pl.pallas_call(
        body, name=name, grid=(N_HEADS, nq, nk),
        in_specs=[pl.BlockSpec((tq, HEAD_PAD), lambda h, i, j: (i, h)),
                  pl.BlockSpec((tk, HEAD_PAD), lambda h, i, j: (jnp.minimum(j, last_k(i)), h)),
                  pl.BlockSpec((tk, V_DIM), lambda h, i, j: (jnp.minimum(j, last_k(i)), h))],
        out_specs=[pl.BlockSpec((tq, V_DIM), lambda h, i, j: (i, h)), pl.BlockSpec((None, tq, 1), lambda h, i, j: (h, i, 0))],
        out_shape=[jax.ShapeDtypeStruct((s, N_HEADS * V_DIM), F32), jax.ShapeDtypeStruct((N_HEADS, s, 1), F32)],
        scratch_shapes=[pltpu.VMEM((tq, 1), F32), pltpu.VMEM((tq, 1), F32), pltpu.VMEM((tq, V_DIM), F32)])(q, k, v)


def _attn_delta(o, do, *, name):
    s = o.shape[0]
    tm = _tile(s, 512)

    def body(o_ref, do_ref, d_ref):
        d_ref[...] = jnp.sum(o_ref[...] * do_ref[...], axis=-1, keepdims=True)

    blk = pl.BlockSpec((tm, V_DIM), lambda h, i: (i, h))
    return pl.pallas_call(
        body, name=name, grid=(N_HEADS, s // tm), in_specs=[blk, blk],
        out_specs=pl.BlockSpec((None, tm, 1), lambda h, i: (h, i, 0)),
        out_shape=jax.ShapeDtypeStruct((N_HEADS, s, 1), F32))(o, do)


def _attn_bwd_dq(q, k, v, do, lse, delta, *, name):
    s = q.shape[0]
    tq = tk = _tile(s, 512)
    nq, nk = s // tq, s // tk

    def last_k(qi):
        return ((qi + 1) * tq - 1) // tk

    def body(q_ref, k_ref, v_ref, do_ref, lse_ref, dl_ref, dq_ref, acc):
        qi, kj = pl.program_id(1), pl.program_id(2)

        @pl.when(kj == 0)
        def _():
            acc[...] = jnp.zeros_like(acc)

        @pl.when(kj <= last_k(qi))
        def _():
            kv = k_ref[...]
            p = jnp.exp(_attn_scores(q_ref[...], kv, qi, kj, tq, tk) - lse_ref[...])
            dp = _nt(do_ref[...], v_ref[...])
            acc[...] += _nn(p * (dp - dl_ref[...]), kv)

        @pl.when(kj == nk - 1)
        def _():
            dq_ref[...] = acc[...] * ATTN_SCALE

    def kmap(h, i, j):
        return (jnp.minimum(j, last_k(i)), h)

    row1 = pl.BlockSpec((None, tq, 1), lambda h, i, j: (h, i, 0))
    return pl.pallas_call(
        body, name=name, grid=(N_HEADS, nq, nk),
        in_specs=[pl.BlockSpec((tq, HEAD_PAD), lambda h, i, j: (i, h)), pl.BlockSpec((tk, HEAD_PAD), kmap),
                  pl.BlockSpec((tk, V_DIM), kmap), pl.BlockSpec((tq, V_DIM), lambda h, i, j: (i, h)), row1, row1],
        out_specs=pl.BlockSpec((tq, HEAD_PAD), lambda h, i, j: (i, h)),
        out_shape=jax.ShapeDtypeStruct((s, N_HEADS * HEAD_PAD), F32),
        scratch_shapes=[pltpu.VMEM((tq, HEAD_PAD), F32)])(q, k, v, do, lse, delta)


def _attn_bwd_dkv(q, k, v, do, lse, delta, dk_in, dv_in, *, name):
    s = q.shape[0]
    tq = tk = _tile(s, 512)
    nq, nk = s // tq, s // tk
    has_in = dk_in is not None

    def first_q(kj):
        return (kj * tk) // tq

    def body(*refs):
        if has_in:
            q_ref, k_ref, v_ref, do_ref, lse_ref, dl_ref, dki_ref, dvi_ref, dk_ref, dv_ref, acck, accv = refs
        else:
            q_ref, k_ref, v_ref, do_ref, lse_ref, dl_ref, dk_ref, dv_ref, acck, accv = refs
        kj, qi = pl.program_id(1), pl.program_id(2)

        @pl.when(qi == 0)
        def _():
            acck[...] = jnp.zeros_like(acck)
            accv[...] = jnp.zeros_like(accv)

        @pl.when(qi >= first_q(kj))
        def _():
            qv, dov = q_ref[...], do_ref[...]
            p = jnp.exp(_attn_scores(qv, k_ref[...], qi, kj, tq, tk) - lse_ref[...])
            accv[...] += _tn(p, dov)
            dp = _nt(dov, v_ref[...])
            acck[...] += _tn(p * (dp - dl_ref[...]), qv)

        @pl.when(qi == nq - 1)
        def _():
            dk = acck[...] * ATTN_SCALE
            dv = accv[...]
            if has_in:
                dk = dki_ref[...] + dk
                dv = dvi_ref[...] + dv
            dk_ref[...] = dk
            dv_ref[...] = dv

    def qmap(h, j, i):
        return (jnp.maximum(i, first_q(j)), h)

    row1 = pl.BlockSpec((None, tq, 1), lambda h, j, i: (h, jnp.maximum(i, first_q(j)), 0))
    k_spec = pl.BlockSpec((tk, HEAD_PAD), lambda h, j, i: (j, h))
    v_spec = pl.BlockSpec((tk, V_DIM), lambda h, j, i: (j, h))
    in_specs = [pl.BlockSpec((tq, HEAD_PAD), qmap), k_spec, v_spec, pl.BlockSpec((tq, V_DIM), qmap), row1, row1]
    args = [q, k, v, do, lse, delta]
    if has_in:
        in_specs += [k_spec, v_spec]
        args += [dk_in, dv_in]
    return pl.pallas_call(
        body, name=name, grid=(N_HEADS, nk, nq), in_specs=in_specs, out_specs=[k_spec, v_spec],
        out_shape=[jax.ShapeDtypeStruct((s, N_HEADS * HEAD_PAD), F32), jax.ShapeDtypeStruct((s, N_HEADS * V_DIM), F32)],
        scratch_shapes=[pltpu.VMEM((tk, HEAD_PAD), F32), pltpu.VMEM((tk, V_DIM), F32)])(*args)


def _loss_head(y, target, *, name):
    s, d = y.shape
    tm = _tile(s, 512)

    def body(y_ref, t_ref, dy_ref, l_ref):
        @pl.when(pl.program_id(0) == 0)
        def _():
            l_ref[...] = jnp.zeros_like(l_ref)

        err = y_ref[...] - t_ref[...]
        dy_ref[...] = err * (1.0 / d)
        sq = _fold8(err * err)
        part = sq[:, 0:LANES]
        for cb in range(1, d // LANES):
            part = part + sq[:, cb * LANES:(cb + 1) * LANES]
        l_ref[...] += part * (0.5 / d)

    row_spec = pl.BlockSpec((tm, d), lambda i: (i, 0))
    return pl.pallas_call(
        body, name=name, grid=(s // tm,), in_specs=[row_spec, row_spec],
        out_specs=[row_spec, pl.BlockSpec((SUBLANES, LANES), lambda i: (0, 0))],
        out_shape=[jax.ShapeDtypeStruct((s, d), F32), jax.ShapeDtypeStruct((SUBLANES, LANES), F32)])(y, target)


def _adamw(w, m, v, g_parts, *, name):
    rows, cols = w.shape
    tm = _tile(rows, 512)
    n_parts = len(g_parts)
    c1 = 1.0 - ADAM_B1 ** ADAM_STEP
    c2 = 1.0 - ADAM_B2 ** ADAM_STEP

    def body(*refs):
        w_ref, m_ref, v_ref = refs[:3]
        g_refs = refs[3:3 + n_parts]
        g_out, d_out, m_out, v_out = refs[3 + n_parts:]
        g = g_refs[0][...]
        for r in g_refs[1:]:
            g = g + r[...]
        mn = ADAM_B1 * m_ref[...] + (1.0 - ADAM_B1) * g
        vn = ADAM_B2 * v_ref[...] + (1.0 - ADAM_B2) * (g * g)
        g_out[...] = g
        m_out[...] = mn
        v_out[...] = vn
        d_out[...] = -ADAM_LR * ((mn / c1) / (jnp.sqrt(vn / c2) + ADAM_EPS) + ADAM_WD * w_ref[...])

    spec = pl.BlockSpec((tm, cols), lambda i: (i, 0))
    out = jax.ShapeDtypeStruct((rows, cols), F32)
    return pl.pallas_call(
        body, name=name, grid=(rows // tm,), in_specs=[spec] * (3 + n_parts),
        out_specs=[spec] * 4, out_shape=[out] * 4)(w, m, v, *g_parts)


def _sum_slots(parts, *, name):
    _, rows, cols = parts.shape
    tm = _tile(rows, 512)

    def body(p_ref, o_ref):
        acc = p_ref[0].astype(F32)
        for k in range(1, N_SHARD):
            acc = acc + p_ref[k].astype(F32)
        o_ref[...] = acc

    return pl.pallas_call(
        body, name=name, grid=(rows // tm,),
        in_specs=[pl.BlockSpec((N_SHARD, tm, cols), lambda i: (0, i, 0))],
        out_specs=pl.BlockSpec((tm, cols), lambda i: (i, 0)),
        out_shape=jax.ShapeDtypeStruct((rows, cols), F32))(parts)


def _mesh_pos():
    return lax.axis_index("x"), lax.axis_index("y"), lax.axis_index("c")


CHIP_FLIPS = ((1, 0), (0, 1), (1, 1))


def _all_gather_chips(shards, *, name):
    n = len(shards)

    def body(*refs):
        ins, outs = refs[:n], refs[n:2 * n]
        send_sems, recv_sems, local_sems = refs[2 * n:]
        x, y, c = _mesh_pos()
        mine = 2 * x + y
        copies = []
        for a in range(n):
            local = pltpu.make_async_copy(ins[a], outs[a].at[mine], local_sems.at[a])
            local.start()
            copies.append(local)
            for k, (fx, fy) in enumerate(CHIP_FLIPS):
                px, py = x ^ fx, y ^ fy
                send = pltpu.make_async_remote_copy(
                    src_ref=ins[a], dst_ref=outs[a].at[mine], send_sem=send_sems.at[a, k], recv_sem=recv_sems.at[a, k],
                    device_id=(px, py, c), device_id_type=MESH)
                send.start()
                arrive = pltpu.make_async_remote_copy(
                    src_ref=ins[a], dst_ref=outs[a].at[2 * px + py], send_sem=send_sems.at[a, k], recv_sem=recv_sems.at[a, k],
                    device_id=(px, py, c), device_id_type=MESH)
                copies.append((send, arrive))
        for cp in copies:
            if isinstance(cp, tuple):
                cp[0].wait_send()
                cp[1].wait_recv()
            else:
                cp.wait()

    return pl.pallas_call(
        body, name=name, in_specs=[ANY] * n, out_specs=[ANY] * n,
        out_shape=[jax.ShapeDtypeStruct((N_SHARD,) + a.shape, a.dtype) for a in shards],
        scratch_shapes=[pltpu.SemaphoreType.DMA((n, 3)), pltpu.SemaphoreType.DMA((n, 3)), pltpu.SemaphoreType.DMA((n,))])(*shards)


def _scatter_to_chips(fulls, *, name):
    n = len(fulls)

    def body(*refs):
        ins, outs = refs[:n], refs[n:2 * n]
        send_sems, recv_sems, local_sems = refs[2 * n:]
        x, y, c = _mesh_pos()
        mine = 2 * x + y
        copies = []
        for a in range(n):
            local = pltpu.make_async_copy(ins[a].at[mine], outs[a].at[mine], local_sems.at[a])
            local.start()
            copies.append(local)
            for k, (fx, fy) in enumerate(CHIP_FLIPS):
                px, py = x ^ fx, y ^ fy
                peer = 2 * px + py
                send = pltpu.make_async_remote_copy(
                    src_ref=ins[a].at[peer], dst_ref=outs[a].at[mine], send_sem=send_sems.at[a, k], recv_sem=recv_sems.at[a, k],
                    device_id=(px, py, c), device_id_type=MESH)
                send.start()
                arrive = pltpu.make_async_remote_copy(
                    src_ref=ins[a].at[peer], dst_ref=outs[a].at[peer], send_sem=send_sems.at[a, k], recv_sem=recv_sems.at[a, k],
                    device_id=(px, py, c), device_id_type=MESH)
                copies.append((send, arrive))
        for cp in copies:
            if isinstance(cp, tuple):
                cp[0].wait_send()
                cp[1].wait_recv()
            else:
                cp.wait()

    return pl.pallas_call(
        body, name=name, in_specs=[ANY] * n, out_specs=[ANY] * n,
        out_shape=[jax.ShapeDtypeStruct(a.shape, a.dtype) for a in fulls],
        scratch_shapes=[pltpu.SemaphoreType.DMA((n, 3)), pltpu.SemaphoreType.DMA((n, 3)), pltpu.SemaphoreType.DMA((n,))])(*fulls)


def _swap_with_sibling(arrays, *, name):
    n = len(arrays)

    def body(*refs):
        ins, outs = refs[:n], refs[n:2 * n]
        send_sems, recv_sems = refs[2 * n:]
        x, y, c = _mesh_pos()
        copies = []
        for a in range(n):
            cp = pltpu.make_async_remote_copy(
                src_ref=ins[a], dst_ref=outs[a], send_sem=send_sems.at[a], recv_sem=recv_sems.at[a],
                device_id=(x, y, 1 - c), device_id_type=MESH)
            cp.start()
            copies.append(cp)
        for cp in copies:
            cp.wait()

    return pl.pallas_call(
        body, name=name, in_specs=[ANY] * n, out_specs=[ANY] * n,
        out_shape=[jax.ShapeDtypeStruct(a.shape, a.dtype) for a in arrays],
        scratch_shapes=[pltpu.SemaphoreType.DMA((n,)), pltpu.SemaphoreType.DMA((n,))])(*arrays)


N_DEV = 8


def _all_reduce_small(vec, *, name):
    rows = vec.shape[0]

    def body(v_ref, o_ref, land, send_sems, recv_sems):
        x, y, c = _mesh_pos()
        me = 4 * x + 2 * y + c
        land[me] = v_ref[...]
        copies = []
        for k in range(1, N_DEV):
            fx, fy, fc = (k >> 2) & 1, (k >> 1) & 1, k & 1
            px, py, pc = x ^ fx, y ^ fy, c ^ fc
            send = pltpu.make_async_remote_copy(
                src_ref=v_ref, dst_ref=land.at[me], send_sem=send_sems.at[k - 1], recv_sem=recv_sems.at[k - 1],
                device_id=(px, py, pc), device_id_type=MESH)
            send.start()
            arrive = pltpu.make_async_remote_copy(
                src_ref=v_ref, dst_ref=land.at[4 * px + 2 * py + pc], send_sem=send_sems.at[k - 1], recv_sem=recv_sems.at[k - 1],
                device_id=(px, py, pc), device_id_type=MESH)
            copies.append((send, arrive))
        for send, arrive in copies:
            send.wait_send()
            arrive.wait_recv()
        acc = land[0]
        for k in range(1, N_DEV):
            acc = acc + land[k]
        o_ref[...] = acc

    return pl.pallas_call(
        body, name=name, in_specs=[VMEM_SPEC], out_specs=VMEM_SPEC,
        out_shape=jax.ShapeDtypeStruct(vec.shape, F32),
        scratch_shapes=[pltpu.VMEM((N_DEV, rows, LANES), F32), pltpu.SemaphoreType.DMA((N_DEV - 1,)),
                        pltpu.SemaphoreType.DMA((N_DEV - 1,))])(vec)


PACK_UNIT = SUBLANES * LANES * 2


def _padded(n):
    return -(-n // PACK_UNIT) * PACK_UNIT


def _pack(arrays, dtype, lead=0):
    parts = []
    for a in arrays:
        lead_shape = a.shape[:lead]
        flat = a.astype(dtype).reshape(lead_shape + (-1,))
        n = flat.shape[-1]
        flat = jnp.pad(flat, [(0, 0)] * lead + [(0, _padded(n) - n)])
        parts.append(flat.reshape(lead_shape + (-1, LANES)))
    return jnp.concatenate(parts, axis=lead)


def _unpack(buf, shapes, lead=0):
    out, row = [], 0
    for shp in shapes:
        n = math.prod(shp)
        rows = _padded(n) // LANES
        part = lax.slice_in_dim(buf, row, row + rows, axis=lead)
        lead_shape = part.shape[:lead]
        part = part.reshape(lead_shape + (-1,))
        part = lax.slice_in_dim(part, 0, n, axis=lead)
        out.append(part.reshape(lead_shape + tuple(shp)))
        row += rows
    return out


def kernel(x, positions, ln_mix_a, w_pool, b_pool, pool_scale, ln_ffn, w_gate, w_up, w_down, ln_kv, w_dkv, g_kv_latent, w_uk, w_uv, g_k, ln_mix_b, w_dq, g_q_latent, w_uq, g_q, w_o, loss_target, m_ln_mix_a, m_w_pool, m_b_pool, m_pool_scale, m_ln_ffn, m_w_gate, m_w_up, m_w_down, m_ln_kv, m_w_dkv, m_g_kv_latent, m_w_uk, m_w_uv, m_g_k, m_ln_mix_b, m_w_dq, m_g_q_latent, m_w_uq, m_g_q, m_w_o, v_ln_mix_a, v_w_pool, v_b_pool, v_pool_scale, v_ln_ffn, v_w_gate, v_w_up, v_w_down, v_ln_kv, v_w_dkv, v_g_kv_latent, v_w_uk, v_w_uv, v_g_k, v_ln_mix_b, v_w_dq, v_g_q_latent, v_w_uq, v_g_q, v_w_o):
    weights = dict(ln_mix_a=ln_mix_a, w_pool=w_pool, b_pool=b_pool, pool_scale=pool_scale, ln_ffn=ln_ffn, w_gate=w_gate,
                   w_up=w_up, w_down=w_down, ln_kv=ln_kv, w_dkv=w_dkv, g_kv_latent=g_kv_latent, w_uk=w_uk, w_uv=w_uv, g_k=g_k,
                   ln_mix_b=ln_mix_b, w_dq=w_dq, g_q_latent=g_q_latent, w_uq=w_uq, g_q=g_q, w_o=w_o)
    mom_m = dict(ln_mix_a=m_ln_mix_a, w_pool=m_w_pool, b_pool=m_b_pool, pool_scale=m_pool_scale, ln_ffn=m_ln_ffn,
                 w_gate=m_w_gate, w_up=m_w_up, w_down=m_w_down, ln_kv=m_ln_kv, w_dkv=m_w_dkv, g_kv_latent=m_g_kv_latent,
                 w_uk=m_w_uk, w_uv=m_w_uv, g_k=m_g_k, ln_mix_b=m_ln_mix_b, w_dq=m_w_dq, g_q_latent=m_g_q_latent,
                 w_uq=m_w_uq, g_q=m_g_q, w_o=m_w_o)
    mom_v = dict(ln_mix_a=v_ln_mix_a, w_pool=v_w_pool, b_pool=v_b_pool, pool_scale=v_pool_scale, ln_ffn=v_ln_ffn,
                 w_gate=v_w_gate, w_up=v_w_up, w_down=v_w_down, ln_kv=v_ln_kv, w_dkv=v_w_dkv, g_kv_latent=v_g_kv_latent,
                 w_uk=v_w_uk, w_uv=v_w_uv, g_k=v_g_k, ln_mix_b=v_ln_mix_b, w_dq=v_w_dq, g_q_latent=v_g_q_latent,
                 w_uq=v_w_uq, g_q=v_g_q, w_o=v_w_o)
    order = list(weights)
    s = x.shape[1]
    d = D_MODEL
    xs = x.reshape(s, d)
    target = loss_target.reshape(s, d)
    my_chip = 2 * lax.axis_index("x") + lax.axis_index("y")

    mat_names = ("w_pool", "w_dkv", "w_uk", "w_uv", "w_dq", "w_uq", "w_o")
    vec_names = ("ln_mix_a", "b_pool", "pool_scale")
    mat_shapes = [weights[n].shape for n in mat_names]
    vec_shapes = [weights[n].shape for n in vec_names]
    mats_local = _pack([weights[n] for n in mat_names], WIRE_DTYPE)
    vecs_local = _pack([weights[n] for n in vec_names], F32)
    wg_all, wu_all, wd_all, mats_all, vecs_all = _all_gather_chips(
        [w_gate.astype(WIRE_DTYPE), w_up.astype(WIRE_DTYPE), w_down.astype(WIRE_DTYPE), mats_local, vecs_local], name="gather_weights")
    g_pool, g_dkv, g_uk, g_uv, g_dq, g_uq, g_o = _unpack(mats_all, mat_shapes, lead=1)
    g_lna, g_bp, g_ps = _unpack(vecs_all, vec_shapes, lead=1)

    wpool_f = g_pool.transpose(1, 2, 0, 3, 4).reshape(N_A, N_GROUPS, GROUP_DIM, GROUP_DIM)
    bpool_f = g_bp.transpose(1, 2, 0, 3).reshape(N_A, 1, d)
    pscale_f = g_ps.transpose(1, 0, 2).reshape(N_A, 1, d)
    lna_f = g_lna.transpose(1, 0, 2).reshape(N_A, 1, d)
    wdkv_f = jnp.pad(g_dkv.reshape(d, KV_LORA + ROPE), ((0, 0), (0, CKV_PAD - KV_LORA - ROPE)))
    wuk_f = g_uk.transpose(1, 0, 2).reshape(KV_LORA, N_HEADS * NOPE)
    wuv_f = g_uv.transpose(1, 0, 2).reshape(KV_LORA, N_HEADS * V_DIM)
    wdq_f = g_dq.transpose(1, 0, 2, 3).reshape(N_B, d, Q_LORA)
    wuq_f = jnp.pad(g_uq.transpose(1, 2, 0, 3).reshape(N_B, Q_LORA, N_HEADS, QK_DIM),
                    ((0, 0), (0, 0), (0, 0), (0, HEAD_PAD - QK_DIM))).reshape(N_B, Q_LORA, N_HEADS * HEAD_PAD)
    wo_f = g_o.transpose(1, 0, 2, 3).reshape(N_B, d, d)

    def head_gain(g):
        return jnp.pad(g.reshape(1, QK_DIM), ((0, 0), (0, HEAD_PAD - QK_DIM)))

    inv = ROPE_THETA ** (-jnp.arange(ROPE // 2, dtype=F32) * 2.0 / ROPE)
    inv_lanes = jnp.concatenate([inv, inv, jnp.zeros((LANES - ROPE,), F32)]).reshape(1, LANES)
    cos_t, sin_t = _rope_tables(positions.reshape(s, 1).astype(F32), inv_lanes, name="rope_tables")

    def ffn_fwd(xin, layer):
        hf = _rms_fwd(xin, ln_ffn[layer].reshape(1, d), n=d, name="ffn_norm")
        a, b, u = _ffn_up(hf, wg_all, wu_all, layer, name="ffn_up")
        return _ffn_down(u, wd_all, layer, xin, name="ffn_down"), (xin, hf, a, b, u)

    saved_a, saved_b, saved_f = [], [], []
    cur = xs
    for l in range(N_A):
        dpool = _rms_pool_fwd(cur, lna_f[l], name="pool_fwd")
        x1 = _pool_mm_fwd(dpool, wpool_f[l], bpool_f[l], pscale_f[l], cur, name="pool_mm")
        saved_a.append((cur, dpool))
        cur, sf = ffn_fwd(x1, l)
        saved_f.append(sf)

    x_kv = cur
    hk = _rms_fwd(x_kv, ln_kv.reshape(1, d), n=d, name="kv_norm")
    ckv = _mm(hk, wdkv_f, name="kv_down")
    c_lat = _rms_fwd(ckv, g_kv_latent.reshape(1, KV_LORA), n=KV_LORA, name="kv_latent_norm")
    kn_raw = _mm(c_lat, wuk_f, name="k_up")
    v_all = _mm(c_lat, wuv_f, out_dtype=MXU_DTYPE, name="v_up")
    k_raw = _k_assemble(kn_raw, ckv, name="k_assemble")
    gk_pad = head_gain(g_k)
    k_cat = _head_norm_rope_fwd(k_raw, gk_pad, cos_t, sin_t, name="k_norm_rope")

    for j in range(N_B):
        l = N_A + j
        hq = _rms_fwd(cur, ln_mix_b[j].reshape(1, d), n=d, name="q_norm")
        cq_raw = _mm(hq, wdq_f[j], name="q_down")
        cq = _rms_fwd(cq_raw, g_q_latent[j].reshape(1, Q_LORA), n=Q_LORA, name="q_latent_norm")
        q_raw = _mm(cq, wuq_f[j], name="q_up")
        gq_pad = head_gain(g_q[j])
        q_cat = _head_norm_rope_fwd(q_raw, gq_pad, cos_t, sin_t, name="q_norm_rope")
        o, lse = _attn_fwd(q_cat, k_cat, v_all, name="attn_fwd")
        x1 = _mm(o, wo_f[j], resid=cur, name="attn_out")
        saved_b.append((cur, hq, cq_raw, cq, q_raw, gq_pad, q_cat, o, lse))
        cur, sf = ffn_fwd(x1, l)
        saved_f.append(sf)

    dy, loss_part = _loss_head(cur, target, name="loss_head")

    ff_w = (N_SHARD, DEPTH, d, FF_SHARD)
    dwg_all = jnp.zeros(ff_w, WIRE_DTYPE)
    dwu_all = jnp.zeros(ff_w, WIRE_DTYPE)
    dwd_all = jnp.zeros((N_SHARD, DEPTH, FF_SHARD, d), WIRE_DTYPE)
    grads = {}
    d_ln_ffn = [None] * DEPTH

    def ffn_bwd(dyv, layer):
        nonlocal dwg_all, dwu_all, dwd_all
        xin, hf, a, b, u = saved_f[layer]
        da, db = _ffn_bwd_hidden(dyv, wd_all, layer, a, b, name="ffn_bwd_hidden")
        dwd_all = _ffn_bwd_dwd(u, dyv, layer, dwd_all, name="ffn_bwd_dwd")
        dwg_all, dwu_all = _ffn_bwd_dwgu(hf, da, db, layer, dwg_all, dwu_all, name="ffn_bwd_dwgu")
        dhf = _ffn_bwd_dh(da, db, wg_all, wu_all, layer, name="ffn_bwd_dh")
        dx, dg = _rms_bwd(xin, ln_ffn[layer].reshape(1, d), dhf, n=d, dx_in=dyv, name="ffn_norm_bwd")
        d_ln_ffn[layer] = dg.sum(axis=0)
        return dx

    dk_acc = dv_acc = None
    d_ln_mix_b, d_w_dq, d_g_q_latent, d_w_uq, d_g_q, d_w_o = ([None] * N_B for _ in range(6))
    dcur = dy
    for j in reversed(range(N_B)):
        l = N_A + j
        xin, hq, cq_raw, cq, q_raw, gq_pad, q_cat, o, lse = saved_b[j]
        dx1 = ffn_bwd(dcur, l)
        do = _mm(dx1, wo_f[j], tb=True, name="attn_out_bwd")
        d_w_o[j] = _mm_tn(o, dx1, name="attn_out_dw")
        delta = _attn_delta(o, do, name="attn_delta")
        dq_cat = _attn_bwd_dq(q_cat, k_cat, v_all, do, lse, delta, name="attn_bwd_dq")
        dk_acc, dv_acc = _attn_bwd_dkv(q_cat, k_cat, v_all, do, lse, delta, dk_acc, dv_acc, name="attn_bwd_dkv")
        dq_raw, dgq = _head_norm_rope_bwd(q_raw, gq_pad, cos_t, sin_t, dq_cat, name="q_norm_rope_bwd")
        d_g_q[j] = dgq.sum(axis=0)[:QK_DIM]
        dcq = _mm(dq_raw, wuq_f[j], tb=True, name="q_up_bwd")
        d_w_uq[j] = _mm_tn(cq, dq_raw, name="q_up_dw").reshape(Q_LORA, N_HEADS, HEAD_PAD)[:, :, :QK_DIM].reshape(Q_LORA, N_HEADS * QK_DIM)
        dcq_raw, dgl = _rms_bwd(cq_raw, g_q_latent[j].reshape(1, Q_LORA), dcq, n=Q_LORA, name="q_latent_norm_bwd")
        d_g_q_latent[j] = dgl.sum(axis=0)
        dhq = _mm(dcq_raw, wdq_f[j], tb=True, name="q_down_bwd")
        d_w_dq[j] = _mm_tn(hq, dcq_raw, name="q_down_dw")
        dcur, dgm = _rms_bwd(xin, ln_mix_b[j].reshape(1, d), dhq, n=d, dx_in=dx1, name="q_norm_bwd")
        d_ln_mix_b[j] = dgm.sum(axis=0)

    dk_raw, dgk = _head_norm_rope_bwd(k_raw, gk_pad, cos_t, sin_t, dk_acc, name="k_norm_rope_bwd")
    grads["g_k"] = dgk.sum(axis=0)[:QK_DIM]
    dc = _mm(dv_acc, wuv_f, tb=True, name="v_up_bwd")
    grads["w_uv"] = _mm_tn(c_lat, dv_acc, name="v_up_dw")
    dkn, dpe = _k_disassemble(dk_raw, name="k_disassemble")
    dc = _mm(dkn, wuk_f, tb=True, resid=dc, name="k_up_bwd")
    grads["w_uk"] = _mm_tn(c_lat, dkn, name="k_up_dw")
    dc_raw, dgl = _rms_bwd(ckv, g_kv_latent.reshape(1, KV_LORA), dc, n=KV_LORA, name="kv_latent_norm_bwd")
    grads["g_kv_latent"] = dgl.sum(axis=0)
    dckv = jnp.concatenate([dc_raw, dpe], axis=1)
    dhk = _mm(dckv, wdkv_f, tb=True, name="kv_down_bwd")
    grads["w_dkv"] = _mm_tn(hk, dckv, name="kv_down_dw")[:, :KV_LORA + ROPE]
    dcur, dg = _rms_bwd(x_kv, ln_kv.reshape(1, d), dhk, n=d, dx_in=dcur, name="kv_norm_bwd")
    grads["ln_kv"] = dg.sum(axis=0)

    d_ln_mix_a, d_w_pool, d_b_pool, d_pool_scale = ([None] * N_A for _ in range(4))
    for l in reversed(range(N_A)):
        xin, dpool = saved_a[l]
        dx1 = ffn_bwd(dcur, l)
        dd, dwp, dbp, dsp = _pool_mm_bwd(dpool, wpool_f[l], bpool_f[l], pscale_f[l], dx1, name="pool_mm_bwd")
        d_w_pool[l], d_b_pool[l], d_pool_scale[l] = dwp, dbp.sum(axis=0), dsp.sum(axis=0)
        dcur, dg = _rms_pool_bwd(xin, lna_f[l], dd, dx1, name="pool_bwd")
        d_ln_mix_a[l] = dg.sum(axis=0)
    grad_x = dcur.reshape(1, s, d)

    gm = {
        "w_pool": jnp.stack(d_w_pool).reshape(N_A, N_GROUPS, N_SHARD, GROUP_DIM // N_SHARD, GROUP_DIM).transpose(2, 0, 1, 3, 4),
        "w_dkv": grads["w_dkv"].reshape(N_SHARD, d // N_SHARD, KV_LORA + ROPE),
        "w_uk": grads["w_uk"].reshape(KV_LORA, N_SHARD, -1).transpose(1, 0, 2),
        "w_uv": grads["w_uv"].reshape(KV_LORA, N_SHARD, -1).transpose(1, 0, 2),
        "w_dq": jnp.stack(d_w_dq).reshape(N_B, N_SHARD, d // N_SHARD, Q_LORA).transpose(1, 0, 2, 3),
        "w_uq": jnp.stack(d_w_uq).reshape(N_B, Q_LORA, N_SHARD, -1).transpose(2, 0, 1, 3),
        "w_o": jnp.stack(d_w_o).reshape(N_B, N_SHARD, d // N_SHARD, d).transpose(1, 0, 2, 3),
    }
    mats_grad = _pack([gm[n] for n in mat_names], WIRE_DTYPE, lead=1)
    big = [dwg_all.reshape(N_SHARD, DEPTH * d, FF_SHARD), dwu_all.reshape(N_SHARD, DEPTH * d, FF_SHARD),
           dwd_all.reshape(N_SHARD, DEPTH * FF_SHARD, d), mats_grad]
    landed = _scatter_to_chips(big, name="scatter_grads")
    chip_sums = [_sum_slots(p, name="sum_chips") for p in landed]
    sib_sums = _swap_with_sibling(chip_sums, name="swap_sibling")

    vec_full = {
        "ln_mix_a": jnp.stack(d_ln_mix_a), "b_pool": jnp.stack(d_b_pool).reshape(N_A, N_GROUPS, GROUP_DIM),
        "pool_scale": jnp.stack(d_pool_scale), "ln_ffn": jnp.stack(d_ln_ffn), "ln_kv": grads["ln_kv"],
        "g_kv_latent": grads["g_kv_latent"], "g_k": grads["g_k"], "ln_mix_b": jnp.stack(d_ln_mix_b),
        "g_q_latent": jnp.stack(d_g_q_latent), "g_q": jnp.stack(d_g_q),
    }
    small_names = list(vec_full)
    small_shapes = [vec_full[n].shape for n in small_names] + [(SUBLANES * LANES,)]
    small = _all_reduce_small(_pack([vec_full[n] for n in small_names] + [loss_part.reshape(-1)], F32), name="all_reduce_small")
    small_sum = _unpack(small, small_shapes)
    loss = jnp.sum(small_sum[-1])
    vec_grad = dict(zip(small_names, small_sum[:-1]))
    vec_grad["ln_mix_a"] = lax.dynamic_slice_in_dim(vec_grad["ln_mix_a"], my_chip * (d // N_SHARD), d // N_SHARD, axis=1)
    vec_grad["pool_scale"] = lax.dynamic_slice_in_dim(vec_grad["pool_scale"], my_chip * (d // N_SHARD), d // N_SHARD, axis=1)
    vec_grad["b_pool"] = lax.dynamic_slice_in_dim(vec_grad["b_pool"], my_chip * (GROUP_DIM // N_SHARD), GROUP_DIM // N_SHARD, axis=2)

    out_g, out_d, out_m, out_v = {}, {}, {}, {}
    for idx, (nm, rows, cols) in enumerate((("w_gate", DEPTH * d, FF_SHARD), ("w_up", DEPTH * d, FF_SHARD), ("w_down", DEPTH * FF_SHARD, d))):
        res = _adamw(weights[nm].reshape(rows, cols), mom_m[nm].reshape(rows, cols), mom_v[nm].reshape(rows, cols),
                     [chip_sums[idx], sib_sums[idx]], name="adamw_ffn")
        shp = weights[nm].shape
        out_g[nm], out_d[nm], out_m[nm], out_v[nm] = (r.reshape(shp) for r in res)

    rest = list(mat_names) + small_names
    rest_shapes = [weights[n].shape for n in rest]
    vec_rows = _pack([vec_grad[n].reshape(weights[n].shape) for n in small_names], F32)
    zeros_vec = jnp.zeros_like(vec_rows)
    g_own = jnp.concatenate([chip_sums[3], vec_rows], axis=0)
    g_sib = jnp.concatenate([sib_sums[3], zeros_vec], axis=0)
    res = _adamw(_pack([weights[n] for n in rest], F32), _pack([mom_m[n] for n in rest], F32),
                 _pack([mom_v[n] for n in rest], F32), [g_own, g_sib], name="adamw_small")
    for tgt, buf in zip((out_g, out_d, out_m, out_v), res):
        for n, arr in zip(rest, _unpack(buf, rest_shapes)):
            tgt[n] = arr

    return (loss, grad_x, *[out_g[n] for n in order], *[out_d[n] for n in order],
            *[out_m[n] for n in order], *[out_v[n] for n in order])
```

```python
import functools
import math

import jax
import jax.numpy as jnp
from jax import lax
from jax.experimental import pallas as pl
from jax.experimental.pallas import tpu as pltpu

F32 = jnp.float32
BF16 = jnp.bfloat16
MXU_DTYPE = BF16
WIRE_DTYPE = BF16

D_MODEL = 1024
N_A = 2
N_B = 2
DEPTH = 4
POOL_WINDOWS = (2, 4, 8, 16)
N_GROUPS = 4
GROUP_DIM = 256
POOL_HALO = 16
N_HEADS = 8
NOPE = 128
ROPE = 64
QK_DIM = 192
HEAD_PAD = 256
V_DIM = 128
Q_LORA = 256
KV_LORA = 512
CKV_PAD = 640
ROPE_THETA = 10000.0
CHUNK = 64
EPS = 1e-6
N_SHARD = 4
FF_SHARD = 704
LANES = 128
SUBLANES = 8
ADAM_LR, ADAM_B1, ADAM_B2, ADAM_EPS, ADAM_WD, ADAM_STEP = 0.001, 0.9, 0.999, 1e-08, 0.01, 10
MESH = pl.DeviceIdType.MESH
ANY = pl.BlockSpec(memory_space=pl.ANY)
VMEM_SPEC = pl.BlockSpec(memory_space=pltpu.VMEM)


def _tile(n, pref):
    if n <= pref:
        return n
    t = pref - pref % SUBLANES
    while n % t:
        t -= SUBLANES
    return t


def _fold8(v):
    r, n = v.shape
    return v.reshape(r // SUBLANES, SUBLANES, n).sum(axis=0)


def _dot(a, b, dims):
    return lax.dot_general(a.astype(MXU_DTYPE), b.astype(MXU_DTYPE), (dims, ((), ())),
                           preferred_element_type=F32)


def _nn(a, b):
    return _dot(a, b, ((1,), (0,)))


def _nt(a, b):
    return _dot(a, b, ((1,), (1,)))


def _tn(a, b):
    return _dot(a, b, ((0,), (0,)))


def _mm(a, b, *, ta=False, tb=False, resid=None, out_dtype=F32, name):
    assert not (ta and tb)
    m, k = (a.shape[1], a.shape[0]) if ta else a.shape
    n = b.shape[0] if tb else b.shape[1]
    tm, tn = _tile(m, 512), _tile(n, 1024)

    def body(*refs):
        if resid is None:
            a_ref, b_ref, o_ref = refs
        else:
            a_ref, b_ref, r_ref, o_ref = refs
        acc = (_tn if ta else _nt if tb else _nn)(a_ref[...], b_ref[...])
        if resid is not None:
            acc = r_ref[...] + acc
        o_ref[...] = acc.astype(o_ref.dtype)

    in_specs = [pl.BlockSpec((k, tm), lambda i, j: (0, i)) if ta else pl.BlockSpec((tm, k), lambda i, j: (i, 0)),
                pl.BlockSpec((tn, k), lambda i, j: (j, 0)) if tb else pl.BlockSpec((k, tn), lambda i, j: (0, j))]
    args = [a, b]
    if resid is not None:
        in_specs.append(pl.BlockSpec((tm, tn), lambda i, j: (i, j)))
        args.append(resid)
    return pl.pallas_call(
        body, name=name, grid=(m // tm, n // tn), in_specs=in_specs,
        out_specs=pl.BlockSpec((tm, tn), lambda i, j: (i, j)),
        out_shape=jax.ShapeDtypeStruct((m, n), out_dtype))(*args)


def _mm_tn(a, b, *, name, at=False, out_dtype=F32):
    m = b.shape[0]
    k1 = a.shape[0] if at else a.shape[1]
    n = b.shape[1]
    tm, tn = _tile(m, 512), _tile(n, 1024)
    nm = m // tm

    def body(a_ref, b_ref, o_ref, acc):
        i = pl.program_id(1)

        @pl.when(i == 0)
        def _():
            acc[...] = jnp.zeros_like(acc)

        acc[...] += (_nn if at else _tn)(a_ref[...], b_ref[...])

        @pl.when(i == nm - 1)
        def _():
            o_ref[...] = acc[...].astype(o_ref.dtype)

    return pl.pallas_call(
        body, name=name, grid=(n // tn, nm),
        in_specs=[pl.BlockSpec((k1, tm), lambda j, i: (0, i)) if at else pl.BlockSpec((tm, k1), lambda j, i: (i, 0)),
                  pl.BlockSpec((tm, tn), lambda j, i: (i, j))],
        out_specs=pl.BlockSpec((k1, tn), lambda j, i: (0, j)),
        out_shape=jax.ShapeDtypeStruct((k1, n), out_dtype),
        scratch_shapes=[pltpu.VMEM((k1, tn), F32)])(a, b)


def _rms_fwd(x, g, *, n, n_valid=None, name):
    out_dtype = MXU_DTYPE
    rows = x.shape[0]
    tm = _tile(rows, 512)
    inv_n = 1.0 / (n_valid or n)

    def body(x_ref, g_ref, o_ref):
        xv = x_ref[...]
        r = lax.rsqrt(jnp.sum(xv * xv, axis=-1, keepdims=True) * inv_n + EPS)
        o_ref[...] = (xv * r * g_ref[...]).astype(o_ref.dtype)

    return pl.pallas_call(
        body, name=name, grid=(rows // tm,),
        in_specs=[pl.BlockSpec((tm, n), lambda i: (i, 0)), pl.BlockSpec((1, n), lambda i: (0, 0))],
        out_specs=pl.BlockSpec((tm, n), lambda i: (i, 0)),
        out_shape=jax.ShapeDtypeStruct((rows, n), out_dtype))(x, g)


def _rms_bwd_math(xv, gv, dyv, inv_n):
    r = lax.rsqrt(jnp.sum(xv * xv, axis=-1, keepdims=True) * inv_n + EPS)
    xh = xv * r
    gy = dyv * gv
    dx = r * (gy - xh * (jnp.sum(gy * xh, axis=-1, keepdims=True) * inv_n))
    return dx, dyv * xh


def _rms_bwd(x, g, dy, *, n, dx_in=None, name):
    rows = x.shape[0]
    tm = _tile(rows, 512)
    inv_n = 1.0 / n

    def body(*refs):
        if dx_in is None:
            x_ref, g_ref, dy_ref, dx_ref, dg_ref = refs
        else:
            x_ref, g_ref, dy_ref, din_ref, dx_ref, dg_ref = refs
        dx, dgc = _rms_bwd_math(x_ref[...], g_ref[...], dy_ref[...], inv_n)
        if dx_in is not None:
            dx = din_ref[...] + dx
        dx_ref[...] = dx

        @pl.when(pl.program_id(0) == 0)
        def _():
            dg_ref[...] = jnp.zeros_like(dg_ref)

        dg_ref[...] += _fold8(dgc)

    row_spec = pl.BlockSpec((tm, n), lambda i: (i, 0))
    in_specs = [row_spec, pl.BlockSpec((1, n), lambda i: (0, 0)), row_spec]
    args = [x, g, dy]
    if dx_in is not None:
        in_specs.append(row_spec)
        args.append(dx_in)
    return pl.pallas_call(
        body, name=name, grid=(rows // tm,), in_specs=in_specs,
        out_specs=[row_spec, pl.BlockSpec((SUBLANES, n), lambda i: (0, 0))],
        out_shape=[jax.ShapeDtypeStruct((rows, n), F32), jax.ShapeDtypeStruct((SUBLANES, n), F32)])(*args)


def _pool_counts(t0, tm, w):
    t = t0 + lax.broadcasted_iota(jnp.int32, (tm, 1), 0)
    return jnp.minimum(t + 1, w).astype(F32)


def _rms_pool_fwd(x, g, *, name):
    s, d = x.shape
    tm = _tile(s, 512)
    hb = tm // POOL_HALO

    def body(x_ref, halo_ref, g_ref, o_ref):
        i = pl.program_id(0)
        gv = g_ref[...]

        def norm(v):
            return v * lax.rsqrt(jnp.mean(v * v, axis=-1, keepdims=True) + EPS) * gv

        h = norm(x_ref[...])
        halo = norm(halo_ref[...]) * (i > 0).astype(F32)
        hh = jnp.concatenate([halo, h], axis=0)
        rows = tm + POOL_HALO
        for gi, w in enumerate(POOL_WINDOWS):
            cols = slice(gi * GROUP_DIM, (gi + 1) * GROUP_DIM)
            acc = hh[:, cols]
            k = 1
            while k < w:
                acc = acc + pltpu.roll(acc, k, 0)
                k *= 2
            win = acc[POOL_HALO:rows]
            o_ref[:, cols] = (win / _pool_counts(i * tm, tm, w) - h[:, cols]).astype(o_ref.dtype)

    return pl.pallas_call(
        body, name=name, grid=(s // tm,),
        in_specs=[pl.BlockSpec((tm, d), lambda i: (i, 0)),
                  pl.BlockSpec((POOL_HALO, d), lambda i: (jnp.maximum(i * hb - 1, 0), 0)),
                  pl.BlockSpec((1, d), lambda i: (0, 0))],
        out_specs=pl.BlockSpec((tm, d), lambda i: (i, 0)),
        out_shape=jax.ShapeDtypeStruct((s, d), MXU_DTYPE))(x, x, g)


def _rms_pool_bwd(x, g, dd, dx_in, *, name):
    s, d = x.shape
    tm = _tile(s, 512)
    hb = tm // POOL_HALO
    nt = s // tm

    def body(x_ref, g_ref, dd_ref, halo_ref, din_ref, dx_ref, dg_ref):
        i = pl.program_id(0)
        ddv = dd_ref[...]
        halo = halo_ref[...] * (i < nt - 1).astype(F32)
        rows = tm + POOL_HALO
        parts = []
        for gi, w in enumerate(POOL_WINDOWS):
            cols = slice(gi * GROUP_DIM, (gi + 1) * GROUP_DIM)
            acc = jnp.concatenate([ddv[:, cols] / _pool_counts(i * tm, tm, w), halo[:, cols] * (1.0 / w)], axis=0)
            k = 1
            while k < w:
                acc = acc + pltpu.roll(acc, rows - k, 0)
                k *= 2
            parts.append(acc[0:tm] - ddv[:, cols])
        dh = jnp.concatenate(parts, axis=1)
        dx, dgc = _rms_bwd_math(x_ref[...], g_ref[...], dh, 1.0 / d)
        dx_ref[...] = din_ref[...] + dx

        @pl.when(i == 0)
        def _():
            dg_ref[...] = jnp.zeros_like(dg_ref)

        dg_ref[...] += _fold8(dgc)

    row_spec = pl.BlockSpec((tm, d), lambda i: (i, 0))
    return pl.pallas_call(
        body, name=name, grid=(nt,),
        in_specs=[row_spec, pl.BlockSpec((1, d), lambda i: (0, 0)), row_spec,
                  pl.BlockSpec((POOL_HALO, d), lambda i: (jnp.minimum((i + 1) * hb, s // POOL_HALO - 1), 0)),
                  row_spec],
        out_specs=[row_spec, pl.BlockSpec((SUBLANES, d), lambda i: (0, 0))],
        out_shape=[jax.ShapeDtypeStruct((s, d), F32), jax.ShapeDtypeStruct((SUBLANES, d), F32)])(x, g, dd, dd, dx_in)


def _pool_mm_fwd(dpool, w, b, scale, x, *, name):
    s, d = x.shape
    tm = _tile(s, 512)

    def body(d_ref, w_ref, b_ref, s_ref, x_ref, o_ref):
        for gi in range(N_GROUPS):
            cols = slice(gi * GROUP_DIM, (gi + 1) * GROUP_DIM)
            y = _nn(d_ref[:, cols], w_ref[gi]) + b_ref[:, cols]
            o_ref[:, cols] = x_ref[:, cols] + y * s_ref[:, cols]

    row_spec = pl.BlockSpec((tm, d), lambda i: (i, 0))
    vec_spec = pl.BlockSpec((1, d), lambda i: (0, 0))
    return pl.pallas_call(
        body, name=name, grid=(s // tm,),
        in_specs=[row_spec, pl.BlockSpec((N_GROUPS, GROUP_DIM, GROUP_DIM), lambda i: (0, 0, 0)), vec_spec, vec_spec, row_spec],
        out_specs=row_spec, out_shape=jax.ShapeDtypeStruct((s, d), F32))(dpool, w, b, scale, x)


def _pool_mm_bwd(dpool, w, b, scale, dx, *, name):
    s, d = dx.shape
    tm = _tile(s, 512)

    def body(d_ref, w_ref, b_ref, s_ref, dx_ref, dd_ref, dw_ref, db_ref, ds_ref):
        @pl.when(pl.program_id(0) == 0)
        def _():
            dw_ref[...] = jnp.zeros_like(dw_ref)
            db_ref[...] = jnp.zeros_like(db_ref)
            ds_ref[...] = jnp.zeros_like(ds_ref)

        for gi in range(N_GROUPS):
            cols = slice(gi * GROUP_DIM, (gi + 1) * GROUP_DIM)
            dg = d_ref[:, cols]
            y = _nn(dg, w_ref[gi]) + b_ref[:, cols]
            dxg = dx_ref[:, cols]
            dy = dxg * s_ref[:, cols]
            ds_ref[:, cols] += _fold8(dxg * y)
            db_ref[:, cols] += _fold8(dy)
            dw_ref[gi] += _tn(dg, dy)
            dd_ref[:, cols] = _nt(dy, w_ref[gi])

    row_spec = pl.BlockSpec((tm, d), lambda i: (i, 0))
    vec_spec = pl.BlockSpec((1, d), lambda i: (0, 0))
    w_spec = pl.BlockSpec((N_GROUPS, GROUP_DIM, GROUP_DIM), lambda i: (0, 0, 0))
    part_spec = pl.BlockSpec((SUBLANES, d), lambda i: (0, 0))
    return pl.pallas_call(
        body, name=name, grid=(s // tm,),
        in_specs=[row_spec, w_spec, vec_spec, vec_spec, row_spec],
        out_specs=[row_spec, w_spec, part_spec, part_spec],
        out_shape=[jax.ShapeDtypeStruct((s, d), F32), jax.ShapeDtypeStruct((N_GROUPS, GROUP_DIM, GROUP_DIM), F32),
                   jax.ShapeDtypeStruct((SUBLANES, d), F32), jax.ShapeDtypeStruct((SUBLANES, d), F32)])(dpool, w, b, scale, dx)


def _ffn_up(hf, wg, wu, layer, *, name):
    s, d = hf.shape
    tm = _tile(s, 512)

    def body(h_ref, wg_ref, wu_ref, a_ref, b_ref, u_ref):
        hv = h_ref[...]
        a = _nn(hv, wg_ref[...])
        b = _nn(hv, wu_ref[...])
        a_ref[...] = a
        b_ref[...] = b
        u_ref[...] = (a * (1.0 / (1.0 + jnp.exp(-a))) * b).astype(u_ref.dtype)

    w_spec = pl.BlockSpec((None, None, d, FF_SHARD), lambda j, i: (j, layer, 0, 0))
    h_spec = pl.BlockSpec((None, tm, FF_SHARD), lambda j, i: (j, i, 0))
    hid = (N_SHARD, s, FF_SHARD)
    return pl.pallas_call(
        body, name=name, grid=(N_SHARD, s // tm),
        in_specs=[pl.BlockSpec((tm, d), lambda j, i: (i, 0)), w_spec, w_spec],
        out_specs=[h_spec, h_spec, h_spec],
        out_shape=[jax.ShapeDtypeStruct(hid, F32), jax.ShapeDtypeStruct(hid, F32), jax.ShapeDtypeStruct(hid, MXU_DTYPE)])(hf, wg, wu)


def _ffn_down(u, wd, layer, x, *, name):
    s, d = x.shape
    tm = _tile(s, 1024)

    def body(u_ref, w_ref, x_ref, o_ref):
        j = pl.program_id(1)

        @pl.when(j == 0)
        def _():
            o_ref[...] = x_ref[...]

        o_ref[...] += _nn(u_ref[...], w_ref[...])

    return pl.pallas_call(
        body, name=name, grid=(s // tm, N_SHARD),
        in_specs=[pl.BlockSpec((None, tm, FF_SHARD), lambda i, j: (j, i, 0)),
                  pl.BlockSpec((None, None, FF_SHARD, d), lambda i, j: (j, layer, 0, 0)),
                  pl.BlockSpec((tm, d), lambda i, j: (i, 0))],
        out_specs=pl.BlockSpec((tm, d), lambda i, j: (i, 0)),
        out_shape=jax.ShapeDtypeStruct((s, d), F32))(u, wd, x)


def _ffn_bwd_hidden(dy, wd, layer, a, b, *, name):
    s, d = dy.shape
    tm = _tile(s, 512)

    def body(dy_ref, w_ref, a_ref, b_ref, da_ref, db_ref):
        du = _nt(dy_ref[...], w_ref[...])
        av, bv = a_ref[...], b_ref[...]
        sg = 1.0 / (1.0 + jnp.exp(-av))
        da_ref[...] = (du * bv * (sg * (1.0 + av * (1.0 - sg)))).astype(da_ref.dtype)
        db_ref[...] = (du * (av * sg)).astype(db_ref.dtype)

    h_spec = pl.BlockSpec((None, tm, FF_SHARD), lambda j, i: (j, i, 0))
    hid = jax.ShapeDtypeStruct((N_SHARD, s, FF_SHARD), MXU_DTYPE)
    return pl.pallas_call(
        body, name=name, grid=(N_SHARD, s // tm),
        in_specs=[pl.BlockSpec((tm, d), lambda j, i: (i, 0)),
                  pl.BlockSpec((None, None, FF_SHARD, d), lambda j, i: (j, layer, 0, 0)), h_spec, h_spec],
        out_specs=[h_spec, h_spec], out_shape=[hid, hid])(dy, wd, a, b)


def _ffn_bwd_dwd(u, dy, layer, dwd_all, *, name):
    s, d = dy.shape
    tm = _tile(s, 512)
    nm = s // tm

    def body(u_ref, dy_ref, _, o_ref, acc):
        i = pl.program_id(1)

        @pl.when(i == 0)
        def _():
            acc[...] = jnp.zeros_like(acc)

        acc[...] += _tn(u_ref[...], dy_ref[...])

        @pl.when(i == nm - 1)
        def _():
            o_ref[...] = acc[...].astype(o_ref.dtype)

    return pl.pallas_call(
        body, name=name, grid=(N_SHARD, nm),
        in_specs=[pl.BlockSpec((None, tm, FF_SHARD), lambda j, i: (j, i, 0)), pl.BlockSpec((tm, d), lambda j, i: (i, 0)), ANY],
        out_specs=pl.BlockSpec((None, None, FF_SHARD, d), lambda j, i: (j, layer, 0, 0)),
        out_shape=jax.ShapeDtypeStruct(dwd_all.shape, dwd_all.dtype),
        scratch_shapes=[pltpu.VMEM((FF_SHARD, d), F32)],
        input_output_aliases={2: 0})(u, dy, dwd_all)


def _ffn_bwd_dwgu(hf, da, db, layer, dwg_all, dwu_all, *, name):
    s, d = hf.shape
    tm = _tile(s, 512)
    nm = s // tm

    def body(h_ref, da_ref, db_ref, _g, _u, og_ref, ou_ref, accg, accu):
        i = pl.program_id(1)

        @pl.when(i == 0)
        def _():
            accg[...] = jnp.zeros_like(accg)
            accu[...] = jnp.zeros_like(accu)

        hv = h_ref[...]
        accg[...] += _tn(hv, da_ref[...])
        accu[...] += _tn(hv, db_ref[...])

        @pl.when(i == nm - 1)
        def _():
            og_ref[...] = accg[...].astype(og_ref.dtype)
            ou_ref[...] = accu[...].astype(ou_ref.dtype)

    h_spec = pl.BlockSpec((None, tm, FF_SHARD), lambda j, i: (j, i, 0))
    w_spec = pl.BlockSpec((None, None, d, FF_SHARD), lambda j, i: (j, layer, 0, 0))
    return pl.pallas_call(
        body, name=name, grid=(N_SHARD, nm),
        in_specs=[pl.BlockSpec((tm, d), lambda j, i: (i, 0)), h_spec, h_spec, ANY, ANY],
        out_specs=[w_spec, w_spec],
        out_shape=[jax.ShapeDtypeStruct(dwg_all.shape, dwg_all.dtype), jax.ShapeDtypeStruct(dwu_all.shape, dwu_all.dtype)],
        scratch_shapes=[pltpu.VMEM((d, FF_SHARD), F32), pltpu.VMEM((d, FF_SHARD), F32)],
        input_output_aliases={3: 0, 4: 1})(hf, da, db, dwg_all, dwu_all)


def _ffn_bwd_dh(da, db, wg, wu, layer, *, name):
    s = da.shape[1]
    d = wg.shape[2]
    tm = _tile(s, 1024)

    def body(da_ref, db_ref, wg_ref, wu_ref, o_ref):
        j = pl.program_id(1)

        @pl.when(j == 0)
        def _():
            o_ref[...] = jnp.zeros_like(o_ref)

        o_ref[...] += _nt(da_ref[...], wg_ref[...]) + _nt(db_ref[...], wu_ref[...])

    h_spec = pl.BlockSpec((None, tm, FF_SHARD), lambda i, j: (j, i, 0))
    w_spec = pl.BlockSpec((None, None, d, FF_SHARD), lambda i, j: (j, layer, 0, 0))
    return pl.pallas_call(
        body, name=name, grid=(s // tm, N_SHARD),
        in_specs=[h_spec, h_spec, w_spec, w_spec],
        out_specs=pl.BlockSpec((tm, d), lambda i, j: (i, 0)),
        out_shape=jax.ShapeDtypeStruct((s, d), F32))(da, db, wg, wu)


def _rope_tables(pos, inv, *, name):
    s = pos.shape[0]
    tm = _tile(s, 512)
    half = ROPE // 2

    def body(p_ref, i_ref, c_ref, s_ref):
        ang = p_ref[...] * i_ref[...]
        lane = lax.broadcasted_iota(jnp.int32, ang.shape, 1)
        live = lane < ROPE
        c_ref[...] = jnp.where(live, jnp.cos(ang), 0.0)
        sn = jnp.sin(ang)
        s_ref[...] = jnp.where(live, jnp.where(lane < half, -sn, sn), 0.0)

    out = jax.ShapeDtypeStruct((s, LANES), F32)
    return pl.pallas_call(
        body, name=name, grid=(s // tm,),
        in_specs=[pl.BlockSpec((tm, 1), lambda i: (i, 0)), pl.BlockSpec((1, LANES), lambda i: (0, 0))],
        out_specs=[pl.BlockSpec((tm, LANES), lambda i: (i, 0))] * 2, out_shape=[out, out])(pos, inv)


def _swap_halves(v):
    half = ROPE // 2
    lane = lax.broadcasted_iota(jnp.int32, v.shape, 1)
    return jnp.where(lane < half, pltpu.roll(v, LANES - half, 1), pltpu.roll(v, half, 1))


def _head_norm_rope_fwd(raw, g, cos, sin, *, name):
    s = raw.shape[0]
    tm = _tile(s, 256)
    width = N_HEADS * HEAD_PAD

    def body(x_ref, g_ref, c_ref, s_ref, o_ref):
        cv, sv = c_ref[...], s_ref[...]
        for h in range(N_HEADS):
            lo = h * HEAD_PAD
            xa = x_ref[:, lo:lo + NOPE]
            xb = x_ref[:, lo + NOPE:lo + HEAD_PAD]
            ms = (jnp.sum(xa * xa, axis=-1, keepdims=True) + jnp.sum(xb * xb, axis=-1, keepdims=True)) * (1.0 / QK_DIM)
            r = lax.rsqrt(ms + EPS)
            o_ref[:, lo:lo + NOPE] = (xa * r * g_ref[:, 0:NOPE]).astype(o_ref.dtype)
            yb = xb * r * g_ref[:, NOPE:HEAD_PAD]
            o_ref[:, lo + NOPE:lo + HEAD_PAD] = (yb * cv + _swap_halves(yb) * sv).astype(o_ref.dtype)

    row_spec = pl.BlockSpec((tm, width), lambda i: (i, 0))
    tab_spec = pl.BlockSpec((tm, LANES), lambda i: (i, 0))
    return pl.pallas_call(
        body, name=name, grid=(s // tm,),
        in_specs=[row_spec, pl.BlockSpec((1, HEAD_PAD), lambda i: (0, 0)), tab_spec, tab_spec],
        out_specs=row_spec, out_shape=jax.ShapeDtypeStruct((s, width), MXU_DTYPE))(raw, g, cos, sin)


def _head_norm_rope_bwd(raw, g, cos, sin, dout, *, name):
    s = raw.shape[0]
    tm = _tile(s, 256)
    width = N_HEADS * HEAD_PAD

    def body(x_ref, g_ref, c_ref, s_ref, do_ref, dx_ref, dg_ref):
        @pl.when(pl.program_id(0) == 0)
        def _():
            dg_ref[...] = jnp.zeros_like(dg_ref)

        cv, sv = c_ref[...], s_ref[...]
        ga, gb = g_ref[:, 0:NOPE], g_ref[:, NOPE:HEAD_PAD]
        for h in range(N_HEADS):
            lo = h * HEAD_PAD
            xa = x_ref[:, lo:lo + NOPE]
            xb = x_ref[:, lo + NOPE:lo + HEAD_PAD]
            dya = do_ref[:, lo:lo + NOPE]
            dob = do_ref[:, lo + NOPE:lo + HEAD_PAD]
            dyb = dob * cv + _swap_halves(dob * sv)
            ms = (jnp.sum(xa * xa, axis=-1, keepdims=True) + jnp.sum(xb * xb, axis=-1, keepdims=True)) * (1.0 / QK_DIM)
            r = lax.rsqrt(ms + EPS)
            xha, xhb = xa * r, xb * r
            gya, gyb = dya * ga, dyb * gb
            dot = (jnp.sum(gya * xha, axis=-1, keepdims=True) + jnp.sum(gyb * xhb, axis=-1, keepdims=True)) * (1.0 / QK_DIM)
            dx_ref[:, lo:lo + NOPE] = r * (gya - xha * dot)
            dx_ref[:, lo + NOPE:lo + HEAD_PAD] = r * (gyb - xhb * dot)
            dg_ref[:, 0:NOPE] += _fold8(dya * xha)
            dg_ref[:, NOPE:HEAD_PAD] += _fold8(dyb * xhb)

    row_spec = pl.BlockSpec((tm, width), lambda i: (i, 0))
    tab_spec = pl.BlockSpec((tm, LANES), lambda i: (i, 0))
    return pl.pallas_call(
        body, name=name, grid=(s // tm,),
        in_specs=[row_spec, pl.BlockSpec((1, HEAD_PAD), lambda i: (0, 0)), tab_spec, tab_spec, row_spec],
        out_specs=[row_spec, pl.BlockSpec((SUBLANES, HEAD_PAD), lambda i: (0, 0))],
        out_shape=[jax.ShapeDtypeStruct((s, width), F32), jax.ShapeDtypeStruct((SUBLANES, HEAD_PAD), F32)])(raw, g, cos, sin, dout)


def _k_assemble(kn, ckv, *, name):
    s = kn.shape[0]
    tm = _tile(s, 512)
    width = N_HEADS * HEAD_PAD

    def body(kn_ref, pe_ref, o_ref):
        pe = pe_ref[...]
        for h in range(N_HEADS):
            o_ref[:, h * HEAD_PAD:h * HEAD_PAD + NOPE] = kn_ref[:, h * NOPE:(h + 1) * NOPE]
            o_ref[:, h * HEAD_PAD + NOPE:(h + 1) * HEAD_PAD] = pe

    return pl.pallas_call(
        body, name=name, grid=(s // tm,),
        in_specs=[pl.BlockSpec((tm, N_HEADS * NOPE), lambda i: (i, 0)),
                  pl.BlockSpec((tm, LANES), lambda i: (i, KV_LORA // LANES))],
        out_specs=pl.BlockSpec((tm, width), lambda i: (i, 0)),
        out_shape=jax.ShapeDtypeStruct((s, width), F32))(kn, ckv)


def _k_disassemble(dk_raw, *, name):
    s = dk_raw.shape[0]
    tm = _tile(s, 512)
    width = N_HEADS * HEAD_PAD

    def body(dk_ref, dkn_ref, dpe_ref):
        pe = dk_ref[:, NOPE:HEAD_PAD]
        for h in range(N_HEADS):
            dkn_ref[:, h * NOPE:(h + 1) * NOPE] = dk_ref[:, h * HEAD_PAD:h * HEAD_PAD + NOPE]
            if h:
                pe = pe + dk_ref[:, h * HEAD_PAD + NOPE:(h + 1) * HEAD_PAD]
        dpe_ref[...] = pe

    return pl.pallas_call(
        body, name=name, grid=(s // tm,),
        in_specs=[pl.BlockSpec((tm, width), lambda i: (i, 0))],
        out_specs=[pl.BlockSpec((tm, N_HEADS * NOPE), lambda i: (i, 0)), pl.BlockSpec((tm, LANES), lambda i: (i, 0))],
        out_shape=[jax.ShapeDtypeStruct((s, N_HEADS * NOPE), F32), jax.ShapeDtypeStruct((s, LANES), F32)])(dk_raw)


ATTN_SCALE = 1.0 / math.sqrt(QK_DIM)
MASKED = -1e30


ATTN_TILE = 512


def _chunk_mask(q0, k0, shape, q_axis):
    qpos = q0 + lax.broadcasted_iota(jnp.int32, shape, q_axis)
    kpos = k0 + lax.broadcasted_iota(jnp.int32, shape, 1 - q_axis)
    return kpos // CHUNK <= qpos // CHUNK


def _attn_fwd(q, k, vt, *, name):
    s = q.shape[0]
    t = _tile(s, ATTN_TILE)
    n = s // t

    def body(q_ref, k_ref, vt_ref, o_ref, lse_ref, m_sc, l_sc, acc):
        qi, kj = pl.program_id(1), pl.program_id(2)

        @pl.when(kj == 0)
        def _():
            m_sc[...] = jnp.full_like(m_sc, MASKED)
            l_sc[...] = jnp.zeros_like(l_sc)
            acc[...] = jnp.zeros_like(acc)

        def step(masked):
            st = _nt(k_ref[...], q_ref[...]) * ATTN_SCALE
            if masked:
                st = jnp.where(_chunk_mask(qi * t, kj * t, (t, t), 1), st, MASKED)
            m_prev = m_sc[...]
            m_new = jnp.maximum(m_prev, jnp.max(st, axis=0, keepdims=True))
            alpha = jnp.exp(m_prev - m_new)
            pt = jnp.exp(st - m_new)
            l_sc[...] = alpha * l_sc[...] + jnp.sum(pt, axis=0, keepdims=True)
            acc[...] = alpha * acc[...] + _nn(vt_ref[...], pt)
            m_sc[...] = m_new

        @pl.when(kj < qi)
        def _():
            step(False)

        @pl.when(kj == qi)
        def _():
            step(True)

        @pl.when(kj == n - 1)
        def _():
            o_ref[...] = acc[...] / l_sc[...]
            lse_ref[...] = m_sc[...] + jnp.log(l_sc[...])

    return pl.pallas_call(
        body, name=name, grid=(N_HEADS, n, n),
        in_specs=[pl.BlockSpec((t, HEAD_PAD), lambda h, i, j: (i, h)),
                  pl.BlockSpec((t, HEAD_PAD), lambda h, i, j: (jnp.minimum(j, i), h)),
                  pl.BlockSpec((V_DIM, t), lambda h, i, j: (h, jnp.minimum(j, i)))],
        out_specs=[pl.BlockSpec((V_DIM, t), lambda h, i, j: (h, i)), pl.BlockSpec((None, 1, t), lambda h, i, j: (h, 0, i))],
        out_shape=[jax.ShapeDtypeStruct((N_HEADS * V_DIM, s), F32), jax.ShapeDtypeStruct((N_HEADS, 1, s), F32)],
        scratch_shapes=[pltpu.VMEM((1, t), F32), pltpu.VMEM((1, t), F32), pltpu.VMEM((V_DIM, t), F32)])(q, k, vt)


def _attn_delta(ot, dot, *, name):
    s = ot.shape[1]
    t = _tile(s, 1024)

    def body(o_ref, do_ref, d_ref):
        d_ref[...] = jnp.sum(o_ref[...] * do_ref[...], axis=0, keepdims=True)

    blk = pl.BlockSpec((V_DIM, t), lambda h, i: (h, i))
    return pl.pallas_call(
        body, name=name, grid=(N_HEADS, s // t), in_specs=[blk, blk],
        out_specs=pl.BlockSpec((None, 1, t), lambda h, i: (h, 0, i)),
        out_shape=jax.ShapeDtypeStruct((N_HEADS, 1, s), F32))(ot, dot)


def _attn_bwd_dq(q, k, v, do, lse_col, delta_col, *, name):
    s = q.shape[0]
    t = _tile(s, ATTN_TILE)
    n = s // t

    def body(q_ref, k_ref, v_ref, do_ref, lse_ref, dl_ref, dq_ref, acc):
        qi, kj = pl.program_id(1), pl.program_id(2)

        @pl.when(kj == 0)
        def _():
            acc[...] = jnp.zeros_like(acc)

        def step(masked):
            kv = k_ref[...]
            sc = _nt(q_ref[...], kv) * ATTN_SCALE
            if masked:
                sc = jnp.where(_chunk_mask(qi * t, kj * t, (t, t), 0), sc, MASKED)
            p = jnp.exp(sc - lse_ref[...])
            dp = _nt(do_ref[...], v_ref[...])
            acc[...] += _nn(p * (dp - dl_ref[...]), kv)

        @pl.when(kj < qi)
        def _():
            step(False)

        @pl.when(kj == qi)
        def _():
            step(True)

        @pl.when(kj == n - 1)
        def _():
            dq_ref[...] = acc[...] * ATTN_SCALE

    def kmap(h, i, j):
        return (jnp.minimum(j, i), h)

    col = pl.BlockSpec((None, t, 1), lambda h, i, j: (h, i, 0))
    return pl.pallas_call(
        body, name=name, grid=(N_HEADS, n, n),
        in_specs=[pl.BlockSpec((t, HEAD_PAD), lambda h, i, j: (i, h)), pl.BlockSpec((t, HEAD_PAD), kmap),
                  pl.BlockSpec((t, V_DIM), kmap), pl.BlockSpec((t, V_DIM), lambda h, i, j: (i, h)), col, col],
        out_specs=pl.BlockSpec((t, HEAD_PAD), lambda h, i, j: (i, h)),
        out_shape=jax.ShapeDtypeStruct((s, N_HEADS * HEAD_PAD), F32),
        scratch_shapes=[pltpu.VMEM((t, HEAD_PAD), F32)])(q, k, v, do, lse_col, delta_col)


def _attn_bwd_dkv(q, k, v, do, lse_row, delta_row, dk_in, dv_in, *, name):
    s = q.shape[0]
    t = _tile(s, ATTN_TILE)
    n = s // t
    has_in = dk_in is not None

    def body(*refs):
        if has_in:
            q_ref, k_ref, v_ref, do_ref, lse_ref, dl_ref, dki_ref, dvi_ref, dk_ref, dv_ref, acck, accv = refs
        else:
            q_ref, k_ref, v_ref, do_ref, lse_ref, dl_ref, dk_ref, dv_ref, acck, accv = refs
        kj, qi = pl.program_id(1), pl.program_id(2)

        @pl.when(qi == 0)
        def _():
            acck[...] = jnp.zeros_like(acck)
            accv[...] = jnp.zeros_like(accv)

        def step(masked):
            qv, dov = q_ref[...], do_ref[...]
            st = _nt(k_ref[...], qv) * ATTN_SCALE
            if masked:
                st = jnp.where(_chunk_mask(qi * t, kj * t, (t, t), 1), st, MASKED)
            pt = jnp.exp(st - lse_ref[...])
            accv[...] += _nn(pt, dov)
            dpt = _nt(v_ref[...], dov)
            acck[...] += _nn(pt * (dpt - dl_ref[...]), qv)

        @pl.when(qi > kj)
        def _():
            step(False)

        @pl.when(qi == kj)
        def _():
            step(True)

        @pl.when(qi == n - 1)
        def _():
            dk = acck[...] * ATTN_SCALE
            dv = accv[...]
            if has_in:
                dk = dki_ref[...] + dk
                dv = dvi_ref[...] + dv
            dk_ref[...] = dk
            dv_ref[...] = dv

    def qmap(h, j, i):
        return (jnp.maximum(i, j), h)

    row = pl.BlockSpec((None, 1, t), lambda h, j, i: (h, 0, jnp.maximum(i, j)))
    k_spec = pl.BlockSpec((t, HEAD_PAD), lambda h, j, i: (j, h))
    v_spec = pl.BlockSpec((t, V_DIM), lambda h, j, i: (j, h))
    in_specs = [pl.BlockSpec((t, HEAD_PAD), qmap), k_spec, v_spec, pl.BlockSpec((t, V_DIM), qmap), row, row]
    args = [q, k, v, do, lse_row, delta_row]
    if has_in:
        in_specs += [k_spec, v_spec]
        args += [dk_in, dv_in]
    return pl.pallas_call(
        body, name=name, grid=(N_HEADS, n, n), in_specs=in_specs, out_specs=[k_spec, v_spec],
        out_shape=[jax.ShapeDtypeStruct((s, N_HEADS * HEAD_PAD), F32), jax.ShapeDtypeStruct((s, N_HEADS * V_DIM), F32)],
        scratch_shapes=[pltpu.VMEM((t, HEAD_PAD), F32), pltpu.VMEM((t, V_DIM), F32)])(*args)


def _loss_head(y, target, *, name):
    s, d = y.shape
    tm = _tile(s, 512)

    def body(y_ref, t_ref, dy_ref, l_ref):
        @pl.when(pl.program_id(0) == 0)
        def _():
            l_ref[...] = jnp.zeros_like(l_ref)

        err = y_ref[...] - t_ref[...]
        dy_ref[...] = err * (1.0 / d)
        sq = _fold8(err * err)
        part = sq[:, 0:LANES]
        for cb in range(1, d // LANES):
            part = part + sq[:, cb * LANES:(cb + 1) * LANES]
        l_ref[...] += part * (0.5 / d)

    row_spec = pl.BlockSpec((tm, d), lambda i: (i, 0))
    return pl.pallas_call(
        body, name=name, grid=(s // tm,), in_specs=[row_spec, row_spec],
        out_specs=[row_spec, pl.BlockSpec((SUBLANES, LANES), lambda i: (0, 0))],
        out_shape=[jax.ShapeDtypeStruct((s, d), F32), jax.ShapeDtypeStruct((SUBLANES, LANES), F32)])(y, target)


ADAMW_ROWS = 512


def _adamw(w, m, v, g_parts, *, name):
    rows, cols = w.shape
    tm = _tile(rows, ADAMW_ROWS)
    n_parts = len(g_parts)
    c1 = 1.0 - ADAM_B1 ** ADAM_STEP
    c2 = 1.0 - ADAM_B2 ** ADAM_STEP

    def body(*refs):
        w_ref, m_ref, v_ref = refs[:3]
        g_refs = refs[3:3 + n_parts]
        g_out, d_out, m_out, v_out = refs[3 + n_parts:]
        g = g_refs[0][...]
        for r in g_refs[1:]:
            g = g + r[...]
        mn = ADAM_B1 * m_ref[...] + (1.0 - ADAM_B1) * g
        vn = ADAM_B2 * v_ref[...] + (1.0 - ADAM_B2) * (g * g)
        g_out[...] = g
        m_out[...] = mn
        v_out[...] = vn
        d_out[...] = -ADAM_LR * ((mn / c1) / (jnp.sqrt(vn / c2) + ADAM_EPS) + ADAM_WD * w_ref[...])

    spec = pl.BlockSpec((tm, cols), lambda i: (i, 0))
    out = jax.ShapeDtypeStruct((rows, cols), F32)
    return pl.pallas_call(
        body, name=name, grid=(rows // tm,), in_specs=[spec] * (3 + n_parts),
        out_specs=[spec] * 4, out_shape=[out] * 4)(w, m, v, *g_parts)


def _sum_slots(parts, *, name):
    _, rows, cols = parts.shape
    tm = _tile(rows, 512)

    def body(p_ref, o_ref):
        acc = p_ref[0].astype(F32)
        for k in range(1, N_SHARD):
            acc = acc + p_ref[k].astype(F32)
        o_ref[...] = acc

    return pl.pallas_call(
        body, name=name, grid=(rows // tm,),
        in_specs=[pl.BlockSpec((N_SHARD, tm, cols), lambda i: (0, i, 0))],
        out_specs=pl.BlockSpec((tm, cols), lambda i: (i, 0)),
        out_shape=jax.ShapeDtypeStruct((rows, cols), F32))(parts)


def _mesh_pos():
    return lax.axis_index("x"), lax.axis_index("y"), lax.axis_index("c")


CHIP_FLIPS = ((1, 0), (0, 1), (1, 1))


def _all_gather_chips(shards, *, name):
    n = len(shards)

    def body(*refs):
        ins, outs = refs[:n], refs[n:2 * n]
        send_sems, recv_sems, local_sems = refs[2 * n:]
        x, y, c = _mesh_pos()
        mine = 2 * x + y
        copies = []
        for a in range(n):
            local = pltpu.make_async_copy(ins[a], outs[a].at[mine], local_sems.at[a])
            local.start()
            copies.append(local)
            for k, (fx, fy) in enumerate(CHIP_FLIPS):
                px, py = x ^ fx, y ^ fy
                send = pltpu.make_async_remote_copy(
                    src_ref=ins[a], dst_ref=outs[a].at[mine], send_sem=send_sems.at[a, k], recv_sem=recv_sems.at[a, k],
                    device_id=(px, py, c), device_id_type=MESH)
                send.start()
                arrive = pltpu.make_async_remote_copy(
                    src_ref=ins[a], dst_ref=outs[a].at[2 * px + py], send_sem=send_sems.at[a, k], recv_sem=recv_sems.at[a, k],
                    device_id=(px, py, c), device_id_type=MESH)
                copies.append((send, arrive))
        for cp in copies:
            if isinstance(cp, tuple):
                cp[0].wait_send()
                cp[1].wait_recv()
            else:
                cp.wait()

    return pl.pallas_call(
        body, name=name, in_specs=[ANY] * n, out_specs=[ANY] * n,
        out_shape=[jax.ShapeDtypeStruct((N_SHARD,) + a.shape, a.dtype) for a in shards],
        scratch_shapes=[pltpu.SemaphoreType.DMA((n, 3)), pltpu.SemaphoreType.DMA((n, 3)), pltpu.SemaphoreType.DMA((n,))])(*shards)


def _scatter_to_chips(fulls, *, name):
    n = len(fulls)

    def body(*refs):
        ins, outs = refs[:n], refs[n:2 * n]
        send_sems, recv_sems, local_sems = refs[2 * n:]
        x, y, c = _mesh_pos()
        mine = 2 * x + y
        copies = []
        for a in range(n):
            local = pltpu.make_async_copy(ins[a].at[mine], outs[a].at[mine], local_sems.at[a])
            local.start()
            copies.append(local)
            for k, (fx, fy) in enumerate(CHIP_FLIPS):
                px, py = x ^ fx, y ^ fy
                peer = 2 * px + py
                send = pltpu.make_async_remote_copy(
                    src_ref=ins[a].at[peer], dst_ref=outs[a].at[mine], send_sem=send_sems.at[a, k], recv_sem=recv_sems.at[a, k],
                    device_id=(px, py, c), device_id_type=MESH)
                send.start()
                arrive = pltpu.make_async_remote_copy(
                    src_ref=ins[a].at[peer], dst_ref=outs[a].at[peer], send_sem=send_sems.at[a, k], recv_sem=recv_sems.at[a, k],
                    device_id=(px, py, c), device_id_type=MESH)
                copies.append((send, arrive))
        for cp in copies:
            if isinstance(cp, tuple):
                cp[0].wait_send()
                cp[1].wait_recv()
            else:
                cp.wait()

    return pl.pallas_call(
        body, name=name, in_specs=[ANY] * n, out_specs=[ANY] * n,
        out_shape=[jax.ShapeDtypeStruct(a.shape, a.dtype) for a in fulls],
        scratch_shapes=[pltpu.SemaphoreType.DMA((n, 3)), pltpu.SemaphoreType.DMA((n, 3)), pltpu.SemaphoreType.DMA((n,))])(*fulls)


def _swap_with_sibling(arrays, *, name):
    n = len(arrays)

    def body(*refs):
        ins, outs = refs[:n], refs[n:2 * n]
        send_sems, recv_sems = refs[2 * n:]
        x, y, c = _mesh_pos()
        copies = []
        for a in range(n):
            cp = pltpu.make_async_remote_copy(
                src_ref=ins[a], dst_ref=outs[a], send_sem=send_sems.at[a], recv_sem=recv_sems.at[a],
                device_id=(x, y, 1 - c), device_id_type=MESH)
            cp.start()
            copies.append(cp)
        for cp in copies:
            cp.wait()

    return pl.pallas_call(
        body, name=name, in_specs=[ANY] * n, out_specs=[ANY] * n,
        out_shape=[jax.ShapeDtypeStruct(a.shape, a.dtype) for a in arrays],
        scratch_shapes=[pltpu.SemaphoreType.DMA((n,)), pltpu.SemaphoreType.DMA((n,))])(*arrays)


N_DEV = 8


def _all_reduce_small(vec, *, name):
    rows = vec.shape[0]

    def body(v_ref, o_ref, land, send_sems, recv_sems):
        x, y, c = _mesh_pos()
        me = 4 * x + 2 * y + c
        land[me] = v_ref[...]
        copies = []
        for k in range(1, N_DEV):
            fx, fy, fc = (k >> 2) & 1, (k >> 1) & 1, k & 1
            px, py, pc = x ^ fx, y ^ fy, c ^ fc
            send = pltpu.make_async_remote_copy(
                src_ref=v_ref, dst_ref=land.at[me], send_sem=send_sems.at[k - 1], recv_sem=recv_sems.at[k - 1],
                device_id=(px, py, pc), device_id_type=MESH)
            send.start()
            arrive = pltpu.make_async_remote_copy(
                src_ref=v_ref, dst_ref=land.at[4 * px + 2 * py + pc], send_sem=send_sems.at[k - 1], recv_sem=recv_sems.at[k - 1],
                device_id=(px, py, pc), device_id_type=MESH)
            copies.append((send, arrive))
        for send, arrive in copies:
            send.wait_send()
            arrive.wait_recv()
        acc = land[0]
        for k in range(1, N_DEV):
            acc = acc + land[k]
        o_ref[...] = acc

    return pl.pallas_call(
        body, name=name, in_specs=[VMEM_SPEC], out_specs=VMEM_SPEC,
        out_shape=jax.ShapeDtypeStruct(vec.shape, F32),
        scratch_shapes=[pltpu.VMEM((N_DEV, rows, LANES), F32), pltpu.SemaphoreType.DMA((N_DEV - 1,)),
                        pltpu.SemaphoreType.DMA((N_DEV - 1,))])(vec)


PACK_UNIT = SUBLANES * LANES * 2


def _padded(n):
    return -(-n // PACK_UNIT) * PACK_UNIT


def _pack(arrays, dtype, lead=0):
    parts = []
    for a in arrays:
        lead_shape = a.shape[:lead]
        flat = a.astype(dtype).reshape(lead_shape + (-1,))
        n = flat.shape[-1]
        flat = jnp.pad(flat, [(0, 0)] * lead + [(0, _padded(n) - n)])
        parts.append(flat.reshape(lead_shape + (-1, LANES)))
    return jnp.concatenate(parts, axis=lead)


def _unpack(buf, shapes, lead=0):
    out, row = [], 0
    for shp in shapes:
        n = math.prod(shp)
        rows = _padded(n) // LANES
        part = lax.slice_in_dim(buf, row, row + rows, axis=lead)
        lead_shape = part.shape[:lead]
        part = part.reshape(lead_shape + (-1,))
        part = lax.slice_in_dim(part, 0, n, axis=lead)
        out.append(part.reshape(lead_shape + tuple(shp)))
        row += rows
    return out


def kernel(x, positions, ln_mix_a, w_pool, b_pool, pool_scale, ln_ffn, w_gate, w_up, w_down, ln_kv, w_dkv, g_kv_latent, w_uk, w_uv, g_k, ln_mix_b, w_dq, g_q_latent, w_uq, g_q, w_o, loss_target, m_ln_mix_a, m_w_pool, m_b_pool, m_pool_scale, m_ln_ffn, m_w_gate, m_w_up, m_w_down, m_ln_kv, m_w_dkv, m_g_kv_latent, m_w_uk, m_w_uv, m_g_k, m_ln_mix_b, m_w_dq, m_g_q_latent, m_w_uq, m_g_q, m_w_o, v_ln_mix_a, v_w_pool, v_b_pool, v_pool_scale, v_ln_ffn, v_w_gate, v_w_up, v_w_down, v_ln_kv, v_w_dkv, v_g_kv_latent, v_w_uk, v_w_uv, v_g_k, v_ln_mix_b, v_w_dq, v_g_q_latent, v_w_uq, v_g_q, v_w_o):
    weights = dict(ln_mix_a=ln_mix_a, w_pool=w_pool, b_pool=b_pool, pool_scale=pool_scale, ln_ffn=ln_ffn, w_gate=w_gate,
                   w_up=w_up, w_down=w_down, ln_kv=ln_kv, w_dkv=w_dkv, g_kv_latent=g_kv_latent, w_uk=w_uk, w_uv=w_uv, g_k=g_k,
                   ln_mix_b=ln_mix_b, w_dq=w_dq, g_q_latent=g_q_latent, w_uq=w_uq, g_q=g_q, w_o=w_o)
    mom_m = dict(ln_mix_a=m_ln_mix_a, w_pool=m_w_pool, b_pool=m_b_pool, pool_scale=m_pool_scale, ln_ffn=m_ln_ffn,
                 w_gate=m_w_gate, w_up=m_w_up, w_down=m_w_down, ln_kv=m_ln_kv, w_dkv=m_w_dkv, g_kv_latent=m_g_kv_latent,
                 w_uk=m_w_uk, w_uv=m_w_uv, g_k=m_g_k, ln_mix_b=m_ln_mix_b, w_dq=m_w_dq, g_q_latent=m_g_q_latent,
                 w_uq=m_w_uq, g_q=m_g_q, w_o=m_w_o)
    mom_v = dict(ln_mix_a=v_ln_mix_a, w_pool=v_w_pool, b_pool=v_b_pool, pool_scale=v_pool_scale, ln_ffn=v_ln_ffn,
                 w_gate=v_w_gate, w_up=v_w_up, w_down=v_w_down, ln_kv=v_ln_kv, w_dkv=v_w_dkv, g_kv_latent=v_g_kv_latent,
                 w_uk=v_w_uk, w_uv=v_w_uv, g_k=v_g_k, ln_mix_b=v_ln_mix_b, w_dq=v_w_dq, g_q_latent=v_g_q_latent,
                 w_uq=v_w_uq, g_q=v_g_q, w_o=v_w_o)
    order = list(weights)
    s = x.shape[1]
    d = D_MODEL
    xs = x.reshape(s, d)
    target = loss_target.reshape(s, d)
    my_chip = 2 * lax.axis_index("x") + lax.axis_index("y")

    mat_names = ("w_pool", "w_dkv", "w_uk", "w_uv", "w_dq", "w_uq", "w_o")
    vec_names = ("ln_mix_a", "b_pool", "pool_scale")
    mat_shapes = [weights[n].shape for n in mat_names]
    vec_shapes = [weights[n].shape for n in vec_names]
    mats_local = _pack([weights[n] for n in mat_names], WIRE_DTYPE)
    vecs_local = _pack([weights[n] for n in vec_names], F32)
    wg_all, wu_all, wd_all, mats_all, vecs_all = _all_gather_chips(
        [w_gate.astype(WIRE_DTYPE), w_up.astype(WIRE_DTYPE), w_down.astype(WIRE_DTYPE), mats_local, vecs_local], name="gather_weights")
    g_pool, g_dkv, g_uk, g_uv, g_dq, g_uq, g_o = _unpack(mats_all, mat_shapes, lead=1)
    g_lna, g_bp, g_ps = _unpack(vecs_all, vec_shapes, lead=1)

    wpool_f = g_pool.transpose(1, 2, 0, 3, 4).reshape(N_A, N_GROUPS, GROUP_DIM, GROUP_DIM)
    bpool_f = g_bp.transpose(1, 2, 0, 3).reshape(N_A, 1, d)
    pscale_f = g_ps.transpose(1, 0, 2).reshape(N_A, 1, d)
    lna_f = g_lna.transpose(1, 0, 2).reshape(N_A, 1, d)
    wdkv_f = jnp.pad(g_dkv.reshape(d, KV_LORA + ROPE), ((0, 0), (0, CKV_PAD - KV_LORA - ROPE)))
    wuk_f = g_uk.transpose(1, 0, 2).reshape(KV_LORA, N_HEADS * NOPE)
    wuv_f = g_uv.transpose(1, 0, 2).reshape(KV_LORA, N_HEADS * V_DIM)
    wdq_f = g_dq.transpose(1, 0, 2, 3).reshape(N_B, d, Q_LORA)
    wuq_f = jnp.pad(g_uq.transpose(1, 2, 0, 3).reshape(N_B, Q_LORA, N_HEADS, QK_DIM),
                    ((0, 0), (0, 0), (0, 0), (0, HEAD_PAD - QK_DIM))).reshape(N_B, Q_LORA, N_HEADS * HEAD_PAD)
    wo_f = g_o.transpose(1, 0, 2, 3).reshape(N_B, d, d)

    def head_gain(g):
        return jnp.pad(g.reshape(1, QK_DIM), ((0, 0), (0, HEAD_PAD - QK_DIM)))

    inv = ROPE_THETA ** (-jnp.arange(ROPE // 2, dtype=F32) * 2.0 / ROPE)
    inv_lanes = jnp.concatenate([inv, inv, jnp.zeros((LANES - ROPE,), F32)]).reshape(1, LANES)
    cos_t, sin_t = _rope_tables(positions.reshape(s, 1).astype(F32), inv_lanes, name="rope_tables")

    def ffn_fwd(xin, layer):
        hf = _rms_fwd(xin, ln_ffn[layer].reshape(1, d), n=d, name="ffn_norm")
        a, b, u = _ffn_up(hf, wg_all, wu_all, layer, name="ffn_up")
        return _ffn_down(u, wd_all, layer, xin, name="ffn_down"), (xin, hf, a, b, u)

    saved_a, saved_b, saved_f = [], [], []
    cur = xs
    for l in range(N_A):
        dpool = _rms_pool_fwd(cur, lna_f[l], name="pool_fwd")
        x1 = _pool_mm_fwd(dpool, wpool_f[l], bpool_f[l], pscale_f[l], cur, name="pool_mm")
        saved_a.append((cur, dpool))
        cur, sf = ffn_fwd(x1, l)
        saved_f.append(sf)

    x_kv = cur
    hk = _rms_fwd(x_kv, ln_kv.reshape(1, d), n=d, name="kv_norm")
    ckv = _mm(hk, wdkv_f, name="kv_down")
    c_lat = _rms_fwd(ckv, g_kv_latent.reshape(1, KV_LORA), n=KV_LORA, name="kv_latent_norm")
    kn_raw = _mm(c_lat, wuk_f, name="k_up")
    v_all = _mm(c_lat, wuv_f, out_dtype=MXU_DTYPE, name="v_up")
    vt_all = _mm(wuv_f.T, c_lat, tb=True, out_dtype=MXU_DTYPE, name="v_up_t")
    k_raw = _k_assemble(kn_raw, ckv, name="k_assemble")
    gk_pad = head_gain(g_k)
    k_cat = _head_norm_rope_fwd(k_raw, gk_pad, cos_t, sin_t, name="k_norm_rope")

    for j in range(N_B):
        l = N_A + j
        hq = _rms_fwd(cur, ln_mix_b[j].reshape(1, d), n=d, name="q_norm")
        cq_raw = _mm(hq, wdq_f[j], name="q_down")
        cq = _rms_fwd(cq_raw, g_q_latent[j].reshape(1, Q_LORA), n=Q_LORA, name="q_latent_norm")
        q_raw = _mm(cq, wuq_f[j], name="q_up")
        gq_pad = head_gain(g_q[j])
        q_cat = _head_norm_rope_fwd(q_raw, gq_pad, cos_t, sin_t, name="q_norm_rope")
        ot, lse = _attn_fwd(q_cat, k_cat, vt_all, name="attn_fwd")
        x1 = _mm(ot, wo_f[j], ta=True, resid=cur, name="attn_out")
        saved_b.append((cur, hq, cq_raw, cq, q_raw, gq_pad, q_cat, ot, lse))
        cur, sf = ffn_fwd(x1, l)
        saved_f.append(sf)

    dy, loss_part = _loss_head(cur, target, name="loss_head")

    ff_w = (N_SHARD, DEPTH, d, FF_SHARD)
    dwg_all = jnp.zeros(ff_w, WIRE_DTYPE)
    dwu_all = jnp.zeros(ff_w, WIRE_DTYPE)
    dwd_all = jnp.zeros((N_SHARD, DEPTH, FF_SHARD, d), WIRE_DTYPE)
    grads = {}
    d_ln_ffn = [None] * DEPTH

    def ffn_bwd(dyv, layer):
        nonlocal dwg_all, dwu_all, dwd_all
        xin, hf, a, b, u = saved_f[layer]
        da, db = _ffn_bwd_hidden(dyv, wd_all, layer, a, b, name="ffn_bwd_hidden")
        dwd_all = _ffn_bwd_dwd(u, dyv, layer, dwd_all, name="ffn_bwd_dwd")
        dwg_all, dwu_all = _ffn_bwd_dwgu(hf, da, db, layer, dwg_all, dwu_all, name="ffn_bwd_dwgu")
        dhf = _ffn_bwd_dh(da, db, wg_all, wu_all, layer, name="ffn_bwd_dh")
        dx, dg = _rms_bwd(xin, ln_ffn[layer].reshape(1, d), dhf, n=d, dx_in=dyv, name="ffn_norm_bwd")
        d_ln_ffn[layer] = dg.sum(axis=0)
        return dx

    dk_acc = dv_acc = None
    d_ln_mix_b, d_w_dq, d_g_q_latent, d_w_uq, d_g_q, d_w_o = ([None] * N_B for _ in range(6))
    dcur = dy
    for j in reversed(range(N_B)):
        l = N_A + j
        xin, hq, cq_raw, cq, q_raw, gq_pad, q_cat, ot, lse = saved_b[j]
        dx1 = ffn_bwd(dcur, l)
        do = _mm(dx1, wo_f[j], tb=True, out_dtype=MXU_DTYPE, name="attn_out_bwd")
        dot = _mm(wo_f[j], dx1, tb=True, name="attn_out_bwd_t")
        d_w_o[j] = _mm_tn(ot, dx1, at=True, name="attn_out_dw")
        delta = _attn_delta(ot, dot, name="attn_delta")
        lse_col, delta_col = lse.reshape(N_HEADS, s, 1), delta.reshape(N_HEADS, s, 1)
        dq_cat = _attn_bwd_dq(q_cat, k_cat, v_all, do, lse_col, delta_col, name="attn_bwd_dq")
        dk_acc, dv_acc = _attn_bwd_dkv(q_cat, k_cat, v_all, do, lse, delta, dk_acc, dv_acc, name="attn_bwd_dkv")
        dq_raw, dgq = _head_norm_rope_bwd(q_raw, gq_pad, cos_t, sin_t, dq_cat, name="q_norm_rope_bwd")
        d_g_q[j] = dgq.sum(axis=0)[:QK_DIM]
        dcq = _mm(dq_raw, wuq_f[j], tb=True, name="q_up_bwd")
        d_w_uq[j] = _mm_tn(cq, dq_raw, name="q_up_dw").reshape(Q_LORA, N_HEADS, HEAD_PAD)[:, :, :QK_DIM].reshape(Q_LORA, N_HEADS * QK_DIM)
        dcq_raw, dgl = _rms_bwd(cq_raw, g_q_latent[j].reshape(1, Q_LORA), dcq, n=Q_LORA, name="q_latent_norm_bwd")
        d_g_q_latent[j] = dgl.sum(axis=0)
        dhq = _mm(dcq_raw, wdq_f[j], tb=True, name="q_down_bwd")
        d_w_dq[j] = _mm_tn(hq, dcq_raw, name="q_down_dw")
        dcur, dgm = _rms_bwd(xin, ln_mix_b[j].reshape(1, d), dhq, n=d, dx_in=dx1, name="q_norm_bwd")
        d_ln_mix_b[j] = dgm.sum(axis=0)

    dk_raw, dgk = _head_norm_rope_bwd(k_raw, gk_pad, cos_t, sin_t, dk_acc, name="k_norm_rope_bwd")
    grads["g_k"] = dgk.sum(axis=0)[:QK_DIM]
    dc = _mm(dv_acc, wuv_f, tb=True, name="v_up_bwd")
    grads["w_uv"] = _mm_tn(c_lat, dv_acc, name="v_up_dw")
    dkn, dpe = _k_disassemble(dk_raw, name="k_disassemble")
    dc = _mm(dkn, wuk_f, tb=True, resid=dc, name="k_up_bwd")
    grads["w_uk"] = _mm_tn(c_lat, dkn, name="k_up_dw")
    dc_raw, dgl = _rms_bwd(ckv, g_kv_latent.reshape(1, KV_LORA), dc, n=KV_LORA, name="kv_latent_norm_bwd")
    grads["g_kv_latent"] = dgl.sum(axis=0)
    dckv = jnp.concatenate([dc_raw, dpe], axis=1)
    dhk = _mm(dckv, wdkv_f, tb=True, name="kv_down_bwd")
    grads["w_dkv"] = _mm_tn(hk, dckv, name="kv_down_dw")[:, :KV_LORA + ROPE]
    dcur, dg = _rms_bwd(x_kv, ln_kv.reshape(1, d), dhk, n=d, dx_in=dcur, name="kv_norm_bwd")
    grads["ln_kv"] = dg.sum(axis=0)

    d_ln_mix_a, d_w_pool, d_b_pool, d_pool_scale = ([None] * N_A for _ in range(4))
    for l in reversed(range(N_A)):
        xin, dpool = saved_a[l]
        dx1 = ffn_bwd(dcur, l)
        dd, dwp, dbp, dsp = _pool_mm_bwd(dpool, wpool_f[l], bpool_f[l], pscale_f[l], dx1, name="pool_mm_bwd")
        d_w_pool[l], d_b_pool[l], d_pool_scale[l] = dwp, dbp.sum(axis=0), dsp.sum(axis=0)
        dcur, dg = _rms_pool_bwd(xin, lna_f[l], dd, dx1, name="pool_bwd")
        d_ln_mix_a[l] = dg.sum(axis=0)
    grad_x = dcur.reshape(1, s, d)

    gm = {
        "w_pool": jnp.stack(d_w_pool).reshape(N_A, N_GROUPS, N_SHARD, GROUP_DIM // N_SHARD, GROUP_DIM).transpose(2, 0, 1, 3, 4),
        "w_dkv": grads["w_dkv"].reshape(N_SHARD, d // N_SHARD, KV_LORA + ROPE),
        "w_uk": grads["w_uk"].reshape(KV_LORA, N_SHARD, -1).transpose(1, 0, 2),
        "w_uv": grads["w_uv"].reshape(KV_LORA, N_SHARD, -1).transpose(1, 0, 2),
        "w_dq": jnp.stack(d_w_dq).reshape(N_B, N_SHARD, d // N_SHARD, Q_LORA).transpose(1, 0, 2, 3),
        "w_uq": jnp.stack(d_w_uq).reshape(N_B, Q_LORA, N_SHARD, -1).transpose(2, 0, 1, 3),
        "w_o": jnp.stack(d_w_o).reshape(N_B, N_SHARD, d // N_SHARD, d).transpose(1, 0, 2, 3),
    }
    mats_grad = _pack([gm[n] for n in mat_names], WIRE_DTYPE, lead=1)
    big = [dwg_all.reshape(N_SHARD, DEPTH * d, FF_SHARD), dwu_all.reshape(N_SHARD, DEPTH * d, FF_SHARD),
           dwd_all.reshape(N_SHARD, DEPTH * FF_SHARD, d), mats_grad]
    landed = _scatter_to_chips(big, name="scatter_grads")
    chip_sums = [_sum_slots(p, name="sum_chips") for p in landed]
    sib_sums = _swap_with_sibling(chip_sums, name="swap_sibling")

    vec_full = {
        "ln_mix_a": jnp.stack(d_ln_mix_a), "b_pool": jnp.stack(d_b_pool).reshape(N_A, N_GROUPS, GROUP_DIM),
        "pool_scale": jnp.stack(d_pool_scale), "ln_ffn": jnp.stack(d_ln_ffn), "ln_kv": grads["ln_kv"],
        "g_kv_latent": grads["g_kv_latent"], "g_k": grads["g_k"], "ln_mix_b": jnp.stack(d_ln_mix_b),
        "g_q_latent": jnp.stack(d_g_q_latent), "g_q": jnp.stack(d_g_q),
    }
    small_names = list(vec_full)
    small_shapes = [vec_full[n].shape for n in small_names] + [(SUBLANES * LANES,)]
    small = _all_reduce_small(_pack([vec_full[n] for n in small_names] + [loss_part.reshape(-1)], F32), name="all_reduce_small")
    small_sum = _unpack(small, small_shapes)
    loss = jnp.sum(small_sum[-1])
    vec_grad = dict(zip(small_names, small_sum[:-1]))
    vec_grad["ln_mix_a"] = lax.dynamic_slice_in_dim(vec_grad["ln_mix_a"], my_chip * (d // N_SHARD), d // N_SHARD, axis=1)
    vec_grad["pool_scale"] = lax.dynamic_slice_in_dim(vec_grad["pool_scale"], my_chip * (d // N_SHARD), d // N_SHARD, axis=1)
    vec_grad["b_pool"] = lax.dynamic_slice_in_dim(vec_grad["b_pool"], my_chip * (GROUP_DIM // N_SHARD), GROUP_DIM // N_SHARD, axis=2)

    out_g, out_d, out_m, out_v = {}, {}, {}, {}
    for idx, (nm, rows, cols) in enumerate((("w_gate", DEPTH * d, FF_SHARD), ("w_up", DEPTH * d, FF_SHARD), ("w_down", DEPTH * FF_SHARD, d))):
        res = _adamw(weights[nm].reshape(rows, cols), mom_m[nm].reshape(rows, cols), mom_v[nm].reshape(rows, cols),
                     [chip_sums[idx], sib_sums[idx]], name="adamw_ffn")
        shp = weights[nm].shape
        out_g[nm], out_d[nm], out_m[nm], out_v[nm] = (r.reshape(shp) for r in res)

    rest = list(mat_names) + small_names
    rest_shapes = [weights[n].shape for n in rest]
    vec_rows = _pack([vec_grad[n].reshape(weights[n].shape) for n in small_names], F32)
    fill = jnp.zeros(((-(chip_sums[3].shape[0] + vec_rows.shape[0])) % ADAMW_ROWS, LANES), F32)
    g_own = jnp.concatenate([chip_sums[3], vec_rows, fill], axis=0)
    g_sib = jnp.concatenate([sib_sums[3], jnp.zeros_like(vec_rows), fill], axis=0)
    res = _adamw(*[jnp.concatenate([_pack([src[n] for n in rest], F32), fill], axis=0) for src in (weights, mom_m, mom_v)],
                 [g_own, g_sib], name="adamw_small")
    for tgt, buf in zip((out_g, out_d, out_m, out_v), res):
        for n, arr in zip(rest, _unpack(buf, rest_shapes)):
            tgt[n] = arr

    return (loss, grad_x, *[out_g[n] for n in order], *[out_d[n] for n in order],
            *[out_m[n] for n in order], *[out_v[n] for n in order])
```

```python
import math
from typing import Any, NamedTuple

import jax
import jax.numpy as jnp
from jax import lax
from jax.experimental import pallas as pl
from jax.experimental.pallas import tpu as pltpu

F32 = jnp.float32
BF16 = jnp.bfloat16
MXU_DTYPE = BF16
WIRE_DTYPE = BF16

D_MODEL = 1024
N_A = 2
N_B = 2
DEPTH = 4
POOL_WINDOWS = (2, 4, 8, 16)
N_GROUPS = 4
GROUP_DIM = 256
POOL_HALO = 16
N_HEADS = 8
NOPE = 128
ROPE = 64
QK_DIM = 192
HEAD_PAD = 256
V_DIM = 128
Q_LORA = 256
KV_LORA = 512
CKV_PAD = 640
ROPE_THETA = 10000.0
CHUNK = 64
EPS = 1e-6
N_SHARD = 4
FF_SHARD = 704
LANES = 128
SUBLANES = 8
ADAM_LR, ADAM_B1, ADAM_B2, ADAM_EPS, ADAM_WD, ADAM_STEP = 0.001, 0.9, 0.999, 1e-08, 0.01, 10
MESH = pl.DeviceIdType.MESH
ANY = pl.BlockSpec(memory_space=pl.ANY)
VMEM_SPEC = pl.BlockSpec(memory_space=pltpu.VMEM)


def _tile(n, pref):
    if n <= pref:
        return n
    t = pref - pref % SUBLANES
    while n % t:
        t -= SUBLANES
    return t


def _fold8(v):
    r, n = v.shape
    return v.reshape(r // SUBLANES, SUBLANES, n).sum(axis=0)


def _dot(a, b, dims):
    return lax.dot_general(a.astype(MXU_DTYPE), b.astype(MXU_DTYPE), (dims, ((), ())),
                           preferred_element_type=F32)


def _nn(a, b):
    return _dot(a, b, ((1,), (0,)))


def _nt(a, b):
    return _dot(a, b, ((1,), (1,)))


def _tn(a, b):
    return _dot(a, b, ((0,), (0,)))


def _mm(a, b, *, ta=False, tb=False, resid=None, out_dtype=F32, name):
    assert not (ta and tb)
    m, k = (a.shape[1], a.shape[0]) if ta else a.shape
    n = b.shape[0] if tb else b.shape[1]
    tm, tn = _tile(m, 512), _tile(n, 1024)

    def body(*refs):
        if resid is None:
            a_ref, b_ref, o_ref = refs
        else:
            a_ref, b_ref, r_ref, o_ref = refs
        acc = (_tn if ta else _nt if tb else _nn)(a_ref[...], b_ref[...])
        if resid is not None:
            acc = r_ref[...] + acc
        o_ref[...] = acc.astype(o_ref.dtype)

    in_specs = [pl.BlockSpec((k, tm), lambda i, j: (0, i)) if ta else pl.BlockSpec((tm, k), lambda i, j: (i, 0)),
                pl.BlockSpec((tn, k), lambda i, j: (j, 0)) if tb else pl.BlockSpec((k, tn), lambda i, j: (0, j))]
    args = [a, b]
    if resid is not None:
        in_specs.append(pl.BlockSpec((tm, tn), lambda i, j: (i, j)))
        args.append(resid)
    return pl.pallas_call(
        body, name=name, grid=(m // tm, n // tn), in_specs=in_specs,
        out_specs=pl.BlockSpec((tm, tn), lambda i, j: (i, j)),
        out_shape=jax.ShapeDtypeStruct((m, n), out_dtype))(*args)


def _mm_tn(a, b, *, name, at=False, out_dtype=F32):
    m = b.shape[0]
    k1 = a.shape[0] if at else a.shape[1]
    n = b.shape[1]
    tm, tn = _tile(m, 512), _tile(n, 1024)
    nm = m // tm

    def body(a_ref, b_ref, o_ref, acc):
        i = pl.program_id(1)

        @pl.when(i == 0)
        def _():
            acc[...] = jnp.zeros_like(acc)

        acc[...] += (_nn if at else _tn)(a_ref[...], b_ref[...])

        @pl.when(i == nm - 1)
        def _():
            o_ref[...] = acc[...].astype(o_ref.dtype)

    return pl.pallas_call(
        body, name=name, grid=(n // tn, nm),
        in_specs=[pl.BlockSpec((k1, tm), lambda j, i: (0, i)) if at else pl.BlockSpec((tm, k1), lambda j, i: (i, 0)),
                  pl.BlockSpec((tm, tn), lambda j, i: (i, j))],
        out_specs=pl.BlockSpec((k1, tn), lambda j, i: (0, j)),
        out_shape=jax.ShapeDtypeStruct((k1, n), out_dtype),
        scratch_shapes=[pltpu.VMEM((k1, tn), F32)])(a, b)


def _rms_fwd(x, g, *, n, n_valid=None, name):
    out_dtype = MXU_DTYPE
    rows = x.shape[0]
    tm = _tile(rows, 512)
    inv_n = 1.0 / (n_valid or n)

    def body(x_ref, g_ref, o_ref):
        xv = x_ref[...]
        r = lax.rsqrt(jnp.sum(xv * xv, axis=-1, keepdims=True) * inv_n + EPS)
        o_ref[...] = (xv * r * g_ref[...]).astype(o_ref.dtype)

    return pl.pallas_call(
        body, name=name, grid=(rows // tm,),
        in_specs=[pl.BlockSpec((tm, n), lambda i: (i, 0)), pl.BlockSpec((1, n), lambda i: (0, 0))],
        out_specs=pl.BlockSpec((tm, n), lambda i: (i, 0)),
        out_shape=jax.ShapeDtypeStruct((rows, n), out_dtype))(x, g)


def _rms_bwd_math(xv, gv, dyv, inv_n):
    r = lax.rsqrt(jnp.sum(xv * xv, axis=-1, keepdims=True) * inv_n + EPS)
    xh = xv * r
    gy = dyv * gv
    dx = r * (gy - xh * (jnp.sum(gy * xh, axis=-1, keepdims=True) * inv_n))
    return dx, dyv * xh


def _rms_bwd(x, g, dy, *, n, dx_in=None, name):
    rows = x.shape[0]
    tm = _tile(rows, 512)
    inv_n = 1.0 / n

    def body(*refs):
        if dx_in is None:
            x_ref, g_ref, dy_ref, dx_ref, dg_ref = refs
        else:
            x_ref, g_ref, dy_ref, din_ref, dx_ref, dg_ref = refs
        dx, dgc = _rms_bwd_math(x_ref[...], g_ref[...], dy_ref[...], inv_n)
        if dx_in is not None:
            dx = din_ref[...] + dx
        dx_ref[...] = dx

        @pl.when(pl.program_id(0) == 0)
        def _():
            dg_ref[...] = jnp.zeros_like(dg_ref)

        dg_ref[...] += _fold8(dgc)

    row_spec = pl.BlockSpec((tm, n), lambda i: (i, 0))
    in_specs = [row_spec, pl.BlockSpec((1, n), lambda i: (0, 0)), row_spec]
    args = [x, g, dy]
    if dx_in is not None:
        in_specs.append(row_spec)
        args.append(dx_in)
    return pl.pallas_call(
        body, name=name, grid=(rows // tm,), in_specs=in_specs,
        out_specs=[row_spec, pl.BlockSpec((SUBLANES, n), lambda i: (0, 0))],
        out_shape=[jax.ShapeDtypeStruct((rows, n), F32), jax.ShapeDtypeStruct((SUBLANES, n), F32)])(*args)


def _pool_counts(t0, tm, w):
    t = t0 + lax.broadcasted_iota(jnp.int32, (tm, 1), 0)
    return jnp.minimum(t + 1, w).astype(F32)


def _rms_pool_fwd(x, g, *, name):
    s, d = x.shape
    tm = _tile(s, 512)
    hb = tm // POOL_HALO

    def body(x_ref, halo_ref, g_ref, o_ref):
        i = pl.program_id(0)
        gv = g_ref[...]

        def norm(v):
            return v * lax.rsqrt(jnp.mean(v * v, axis=-1, keepdims=True) + EPS) * gv

        h = norm(x_ref[...])
        halo = norm(halo_ref[...]) * (i > 0).astype(F32)
        hh = jnp.concatenate([halo, h], axis=0)
        rows = tm + POOL_HALO
        for gi, w in enumerate(POOL_WINDOWS):
            cols = slice(gi * GROUP_DIM, (gi + 1) * GROUP_DIM)
            acc = hh[:, cols]
            k = 1
            while k < w:
                acc = acc + pltpu.roll(acc, k, 0)
                k *= 2
            win = acc[POOL_HALO:rows]
            o_ref[:, cols] = (win / _pool_counts(i * tm, tm, w) - h[:, cols]).astype(o_ref.dtype)

    return pl.pallas_call(
        body, name=name, grid=(s // tm,),
        in_specs=[pl.BlockSpec((tm, d), lambda i: (i, 0)),
                  pl.BlockSpec((POOL_HALO, d), lambda i: (jnp.maximum(i * hb - 1, 0), 0)),
                  pl.BlockSpec((1, d), lambda i: (0, 0))],
        out_specs=pl.BlockSpec((tm, d), lambda i: (i, 0)),
        out_shape=jax.ShapeDtypeStruct((s, d), MXU_DTYPE))(x, x, g)


def _rms_pool_bwd(x, g, dd, dx_in, *, name):
    s, d = x.shape
    tm = _tile(s, 512)
    hb = tm // POOL_HALO
    nt = s // tm

    def body(x_ref, g_ref, dd_ref, halo_ref, din_ref, dx_ref, dg_ref):
        i = pl.program_id(0)
        ddv = dd_ref[...]
        halo = halo_ref[...] * (i < nt - 1).astype(F32)
        rows = tm + POOL_HALO
        parts = []
        for gi, w in enumerate(POOL_WINDOWS):
            cols = slice(gi * GROUP_DIM, (gi + 1) * GROUP_DIM)
            acc = jnp.concatenate([ddv[:, cols] / _pool_counts(i * tm, tm, w), halo[:, cols] * (1.0 / w)], axis=0)
            k = 1
            while k < w:
                acc = acc + pltpu.roll(acc, rows - k, 0)
                k *= 2
            parts.append(acc[0:tm] - ddv[:, cols])
        dh = jnp.concatenate(parts, axis=1)
        dx, dgc = _rms_bwd_math(x_ref[...], g_ref[...], dh, 1.0 / d)
        dx_ref[...] = din_ref[...] + dx

        @pl.when(i == 0)
        def _():
            dg_ref[...] = jnp.zeros_like(dg_ref)

        dg_ref[...] += _fold8(dgc)

    row_spec = pl.BlockSpec((tm, d), lambda i: (i, 0))
    return pl.pallas_call(
        body, name=name, grid=(nt,),
        in_specs=[row_spec, pl.BlockSpec((1, d), lambda i: (0, 0)), row_spec,
                  pl.BlockSpec((POOL_HALO, d), lambda i: (jnp.minimum((i + 1) * hb, s // POOL_HALO - 1), 0)),
                  row_spec],
        out_specs=[row_spec, pl.BlockSpec((SUBLANES, d), lambda i: (0, 0))],
        out_shape=[jax.ShapeDtypeStruct((s, d), F32), jax.ShapeDtypeStruct((SUBLANES, d), F32)])(x, g, dd, dd, dx_in)


def _pool_mm_fwd(dpool, w, b, scale, x, *, name):
    s, d = x.shape
    tm = _tile(s, 512)

    def body(d_ref, w_ref, b_ref, s_ref, x_ref, o_ref):
        for gi in range(N_GROUPS):
            cols = slice(gi * GROUP_DIM, (gi + 1) * GROUP_DIM)
            y = _nn(d_ref[:, cols], w_ref[gi]) + b_ref[:, cols]
            o_ref[:, cols] = x_ref[:, cols] + y * s_ref[:, cols]

    row_spec = pl.BlockSpec((tm, d), lambda i: (i, 0))
    vec_spec = pl.BlockSpec((1, d), lambda i: (0, 0))
    return pl.pallas_call(
        body, name=name, grid=(s // tm,),
        in_specs=[row_spec, pl.BlockSpec((N_GROUPS, GROUP_DIM, GROUP_DIM), lambda i: (0, 0, 0)), vec_spec, vec_spec, row_spec],
        out_specs=row_spec, out_shape=jax.ShapeDtypeStruct((s, d), F32))(dpool, w, b, scale, x)


def _pool_mm_bwd(dpool, w, b, scale, dx, *, name):
    s, d = dx.shape
    tm = _tile(s, 512)

    def body(d_ref, w_ref, b_ref, s_ref, dx_ref, dd_ref, dw_ref, db_ref, ds_ref):
        @pl.when(pl.program_id(0) == 0)
        def _():
            dw_ref[...] = jnp.zeros_like(dw_ref)
            db_ref[...] = jnp.zeros_like(db_ref)
            ds_ref[...] = jnp.zeros_like(ds_ref)

        for gi in range(N_GROUPS):
            cols = slice(gi * GROUP_DIM, (gi + 1) * GROUP_DIM)
            dg = d_ref[:, cols]
            y = _nn(dg, w_ref[gi]) + b_ref[:, cols]
            dxg = dx_ref[:, cols]
            dy = dxg * s_ref[:, cols]
            ds_ref[:, cols] += _fold8(dxg * y)
            db_ref[:, cols] += _fold8(dy)
            dw_ref[gi] += _tn(dg, dy)
            dd_ref[:, cols] = _nt(dy, w_ref[gi])

    row_spec = pl.BlockSpec((tm, d), lambda i: (i, 0))
    vec_spec = pl.BlockSpec((1, d), lambda i: (0, 0))
    w_spec = pl.BlockSpec((N_GROUPS, GROUP_DIM, GROUP_DIM), lambda i: (0, 0, 0))
    part_spec = pl.BlockSpec((SUBLANES, d), lambda i: (0, 0))
    return pl.pallas_call(
        body, name=name, grid=(s // tm,),
        in_specs=[row_spec, w_spec, vec_spec, vec_spec, row_spec],
        out_specs=[row_spec, w_spec, part_spec, part_spec],
        out_shape=[jax.ShapeDtypeStruct((s, d), F32), jax.ShapeDtypeStruct((N_GROUPS, GROUP_DIM, GROUP_DIM), F32),
                   jax.ShapeDtypeStruct((SUBLANES, d), F32), jax.ShapeDtypeStruct((SUBLANES, d), F32)])(dpool, w, b, scale, dx)


def _ffn_up(hf, wg, wu, *, name):
    s, d = hf.shape
    tm = _tile(s, 512)

    def body(h_ref, wg_ref, wu_ref, a_ref, b_ref, u_ref):
        hv = h_ref[...]
        a = _nn(hv, wg_ref[...])
        b = _nn(hv, wu_ref[...])
        a_ref[...] = a
        b_ref[...] = b
        u_ref[...] = (a * (1.0 / (1.0 + jnp.exp(-a))) * b).astype(u_ref.dtype)

    w_spec = pl.BlockSpec((None, d, FF_SHARD), lambda j, i: (j, 0, 0))
    h_spec = pl.BlockSpec((None, tm, FF_SHARD), lambda j, i: (j, i, 0))
    hid = (N_SHARD, s, FF_SHARD)
    return pl.pallas_call(
        body, name=name, grid=(N_SHARD, s // tm),
        in_specs=[pl.BlockSpec((tm, d), lambda j, i: (i, 0)), w_spec, w_spec],
        out_specs=[h_spec, h_spec, h_spec],
        out_shape=[jax.ShapeDtypeStruct(hid, F32), jax.ShapeDtypeStruct(hid, F32), jax.ShapeDtypeStruct(hid, MXU_DTYPE)])(hf, wg, wu)


def _ffn_down(u, wd, x, *, name):
    s, d = x.shape
    tm = _tile(s, 1024)

    def body(u_ref, w_ref, x_ref, o_ref):
        j = pl.program_id(1)

        @pl.when(j == 0)
        def _():
            o_ref[...] = x_ref[...]

        o_ref[...] += _nn(u_ref[...], w_ref[...])

    return pl.pallas_call(
        body, name=name, grid=(s // tm, N_SHARD),
        in_specs=[pl.BlockSpec((None, tm, FF_SHARD), lambda i, j: (j, i, 0)),
                  pl.BlockSpec((None, FF_SHARD, d), lambda i, j: (j, 0, 0)),
                  pl.BlockSpec((tm, d), lambda i, j: (i, 0))],
        out_specs=pl.BlockSpec((tm, d), lambda i, j: (i, 0)),
        out_shape=jax.ShapeDtypeStruct((s, d), F32))(u, wd, x)


def _ffn_bwd_hidden(dy, wd, a, b, *, name):
    s, d = dy.shape
    tm = _tile(s, 512)

    def body(dy_ref, w_ref, a_ref, b_ref, da_ref, db_ref):
        du = _nt(dy_ref[...], w_ref[...])
        av, bv = a_ref[...], b_ref[...]
        sg = 1.0 / (1.0 + jnp.exp(-av))
        da_ref[...] = (du * bv * (sg * (1.0 + av * (1.0 - sg)))).astype(da_ref.dtype)
        db_ref[...] = (du * (av * sg)).astype(db_ref.dtype)

    h_spec = pl.BlockSpec((None, tm, FF_SHARD), lambda j, i: (j, i, 0))
    hid = jax.ShapeDtypeStruct((N_SHARD, s, FF_SHARD), MXU_DTYPE)
    return pl.pallas_call(
        body, name=name, grid=(N_SHARD, s // tm),
        in_specs=[pl.BlockSpec((tm, d), lambda j, i: (i, 0)),
                  pl.BlockSpec((None, FF_SHARD, d), lambda j, i: (j, 0, 0)), h_spec, h_spec],
        out_specs=[h_spec, h_spec], out_shape=[hid, hid])(dy, wd, a, b)


def _ffn_bwd_dwd(u, dy, *, name):
    s, d = dy.shape
    tm = _tile(s, 512)
    nm = s // tm

    def body(u_ref, dy_ref, o_ref, acc):
        i = pl.program_id(1)

        @pl.when(i == 0)
        def _():
            acc[...] = jnp.zeros_like(acc)

        acc[...] += _tn(u_ref[...], dy_ref[...])

        @pl.when(i == nm - 1)
        def _():
            o_ref[...] = acc[...].astype(o_ref.dtype)

    return pl.pallas_call(
        body, name=name, grid=(N_SHARD, nm),
        in_specs=[pl.BlockSpec((None, tm, FF_SHARD), lambda j, i: (j, i, 0)), pl.BlockSpec((tm, d), lambda j, i: (i, 0))],
        out_specs=pl.BlockSpec((None, FF_SHARD, d), lambda j, i: (j, 0, 0)),
        out_shape=jax.ShapeDtypeStruct((N_SHARD, FF_SHARD, d), WIRE_DTYPE),
        scratch_shapes=[pltpu.VMEM((FF_SHARD, d), F32)])(u, dy)


def _ffn_bwd_dwgu(hf, da, db, *, name):
    s, d = hf.shape
    tm = _tile(s, 512)
    nm = s // tm

    def body(h_ref, da_ref, db_ref, og_ref, ou_ref, accg, accu):
        i = pl.program_id(1)

        @pl.when(i == 0)
        def _():
            accg[...] = jnp.zeros_like(accg)
            accu[...] = jnp.zeros_like(accu)

        hv = h_ref[...]
        accg[...] += _tn(hv, da_ref[...])
        accu[...] += _tn(hv, db_ref[...])

        @pl.when(i == nm - 1)
        def _():
            og_ref[...] = accg[...].astype(og_ref.dtype)
            ou_ref[...] = accu[...].astype(ou_ref.dtype)

    h_spec = pl.BlockSpec((None, tm, FF_SHARD), lambda j, i: (j, i, 0))
    w_spec = pl.BlockSpec((None, d, FF_SHARD), lambda j, i: (j, 0, 0))
    grad = jax.ShapeDtypeStruct((N_SHARD, d, FF_SHARD), WIRE_DTYPE)
    return pl.pallas_call(
        body, name=name, grid=(N_SHARD, nm),
        in_specs=[pl.BlockSpec((tm, d), lambda j, i: (i, 0)), h_spec, h_spec],
        out_specs=[w_spec, w_spec], out_shape=[grad, grad],
        scratch_shapes=[pltpu.VMEM((d, FF_SHARD), F32), pltpu.VMEM((d, FF_SHARD), F32)])(hf, da, db)


def _ffn_bwd_dh(da, db, wg, wu, *, name):
    s = da.shape[1]
    d = wg.shape[1]
    tm = _tile(s, 1024)

    def body(da_ref, db_ref, wg_ref, wu_ref, o_ref):
        j = pl.program_id(1)

        @pl.when(j == 0)
        def _():
            o_ref[...] = jnp.zeros_like(o_ref)

        o_ref[...] += _nt(da_ref[...], wg_ref[...]) + _nt(db_ref[...], wu_ref[...])

    h_spec = pl.BlockSpec((None, tm, FF_SHARD), lambda i, j: (j, i, 0))
    w_spec = pl.BlockSpec((None, d, FF_SHARD), lambda i, j: (j, 0, 0))
    return pl.pallas_call(
        body, name=name, grid=(s // tm, N_SHARD),
        in_specs=[h_spec, h_spec, w_spec, w_spec],
        out_specs=pl.BlockSpec((tm, d), lambda i, j: (i, 0)),
        out_shape=jax.ShapeDtypeStruct((s, d), F32))(da, db, wg, wu)


def _rope_tables(pos, inv, *, name):
    s = pos.shape[0]
    tm = _tile(s, 512)
    half = ROPE // 2

    def body(p_ref, i_ref, c_ref, s_ref):
        ang = p_ref[...] * i_ref[...]
        lane = lax.broadcasted_iota(jnp.int32, ang.shape, 1)
        live = lane < ROPE
        c_ref[...] = jnp.where(live, jnp.cos(ang), 0.0)
        sn = jnp.sin(ang)
        s_ref[...] = jnp.where(live, jnp.where(lane < half, -sn, sn), 0.0)

    out = jax.ShapeDtypeStruct((s, LANES), F32)
    return pl.pallas_call(
        body, name=name, grid=(s // tm,),
        in_specs=[pl.BlockSpec((tm, 1), lambda i: (i, 0)), pl.BlockSpec((1, LANES), lambda i: (0, 0))],
        out_specs=[pl.BlockSpec((tm, LANES), lambda i: (i, 0))] * 2, out_shape=[out, out])(pos, inv)


def _swap_halves(v):
    half = ROPE // 2
    lane = lax.broadcasted_iota(jnp.int32, v.shape, 1)
    return jnp.where(lane < half, pltpu.roll(v, LANES - half, 1), pltpu.roll(v, half, 1))


def _head_norm_rope_fwd(raw, g, cos, sin, *, name):
    s = raw.shape[0]
    tm = _tile(s, 256)
    width = N_HEADS * HEAD_PAD

    def body(x_ref, g_ref, c_ref, s_ref, o_ref):
        cv, sv = c_ref[...], s_ref[...]
        for h in range(N_HEADS):
            lo = h * HEAD_PAD
            xa = x_ref[:, lo:lo + NOPE]
            xb = x_ref[:, lo + NOPE:lo + HEAD_PAD]
            ms = (jnp.sum(xa * xa, axis=-1, keepdims=True) + jnp.sum(xb * xb, axis=-1, keepdims=True)) * (1.0 / QK_DIM)
            r = lax.rsqrt(ms + EPS)
            o_ref[:, lo:lo + NOPE] = (xa * r * g_ref[:, 0:NOPE]).astype(o_ref.dtype)
            yb = xb * r * g_ref[:, NOPE:HEAD_PAD]
            o_ref[:, lo + NOPE:lo + HEAD_PAD] = (yb * cv + _swap_halves(yb) * sv).astype(o_ref.dtype)

    row_spec = pl.BlockSpec((tm, width), lambda i: (i, 0))
    tab_spec = pl.BlockSpec((tm, LANES), lambda i: (i, 0))
    return pl.pallas_call(
        body, name=name, grid=(s // tm,),
        in_specs=[row_spec, pl.BlockSpec((1, HEAD_PAD), lambda i: (0, 0)), tab_spec, tab_spec],
        out_specs=row_spec, out_shape=jax.ShapeDtypeStruct((s, width), MXU_DTYPE))(raw, g, cos, sin)


def _head_norm_rope_bwd(raw, g, cos, sin, dout, *, name):
    s = raw.shape[0]
    tm = _tile(s, 256)
    width = N_HEADS * HEAD_PAD

    def body(x_ref, g_ref, c_ref, s_ref, do_ref, dx_ref, dg_ref):
        @pl.when(pl.program_id(0) == 0)
        def _():
            dg_ref[...] = jnp.zeros_like(dg_ref)

        cv, sv = c_ref[...], s_ref[...]
        ga, gb = g_ref[:, 0:NOPE], g_ref[:, NOPE:HEAD_PAD]
        for h in range(N_HEADS):
            lo = h * HEAD_PAD
            xa = x_ref[:, lo:lo + NOPE]
            xb = x_ref[:, lo + NOPE:lo + HEAD_PAD]
            dya = do_ref[:, lo:lo + NOPE]
            dob = do_ref[:, lo + NOPE:lo + HEAD_PAD]
            dyb = dob * cv + _swap_halves(dob * sv)
            ms = (jnp.sum(xa * xa, axis=-1, keepdims=True) + jnp.sum(xb * xb, axis=-1, keepdims=True)) * (1.0 / QK_DIM)
            r = lax.rsqrt(ms + EPS)
            xha, xhb = xa * r, xb * r
            gya, gyb = dya * ga, dyb * gb
            dot = (jnp.sum(gya * xha, axis=-1, keepdims=True) + jnp.sum(gyb * xhb, axis=-1, keepdims=True)) * (1.0 / QK_DIM)
            dx_ref[:, lo:lo + NOPE] = r * (gya - xha * dot)
            dx_ref[:, lo + NOPE:lo + HEAD_PAD] = r * (gyb - xhb * dot)
            dg_ref[:, 0:NOPE] += _fold8(dya * xha)
            dg_ref[:, NOPE:HEAD_PAD] += _fold8(dyb * xhb)

    row_spec = pl.BlockSpec((tm, width), lambda i: (i, 0))
    tab_spec = pl.BlockSpec((tm, LANES), lambda i: (i, 0))
    return pl.pallas_call(
        body, name=name, grid=(s // tm,),
        in_specs=[row_spec, pl.BlockSpec((1, HEAD_PAD), lambda i: (0, 0)), tab_spec, tab_spec, row_spec],
        out_specs=[row_spec, pl.BlockSpec((SUBLANES, HEAD_PAD), lambda i: (0, 0))],
        out_shape=[jax.ShapeDtypeStruct((s, width), F32), jax.ShapeDtypeStruct((SUBLANES, HEAD_PAD), F32)])(raw, g, cos, sin, dout)


def _k_assemble(kn, ckv, *, name):
    s = kn.shape[0]
    tm = _tile(s, 512)
    width = N_HEADS * HEAD_PAD

    def body(kn_ref, pe_ref, o_ref):
        pe = pe_ref[...]
        for h in range(N_HEADS):
            o_ref[:, h * HEAD_PAD:h * HEAD_PAD + NOPE] = kn_ref[:, h * NOPE:(h + 1) * NOPE]
            o_ref[:, h * HEAD_PAD + NOPE:(h + 1) * HEAD_PAD] = pe

    return pl.pallas_call(
        body, name=name, grid=(s // tm,),
        in_specs=[pl.BlockSpec((tm, N_HEADS * NOPE), lambda i: (i, 0)),
                  pl.BlockSpec((tm, LANES), lambda i: (i, KV_LORA // LANES))],
        out_specs=pl.BlockSpec((tm, width), lambda i: (i, 0)),
        out_shape=jax.ShapeDtypeStruct((s, width), F32))(kn, ckv)


def _k_disassemble(dk_raw, *, name):
    s = dk_raw.shape[0]
    tm = _tile(s, 512)
    width = N_HEADS * HEAD_PAD

    def body(dk_ref, dkn_ref, dpe_ref):
        pe = dk_ref[:, NOPE:HEAD_PAD]
        for h in range(N_HEADS):
            dkn_ref[:, h * NOPE:(h + 1) * NOPE] = dk_ref[:, h * HEAD_PAD:h * HEAD_PAD + NOPE]
            if h:
                pe = pe + dk_ref[:, h * HEAD_PAD + NOPE:(h + 1) * HEAD_PAD]
        dpe_ref[...] = pe

    return pl.pallas_call(
        body, name=name, grid=(s // tm,),
        in_specs=[pl.BlockSpec((tm, width), lambda i: (i, 0))],
        out_specs=[pl.BlockSpec((tm, N_HEADS * NOPE), lambda i: (i, 0)), pl.BlockSpec((tm, LANES), lambda i: (i, 0))],
        out_shape=[jax.ShapeDtypeStruct((s, N_HEADS * NOPE), F32), jax.ShapeDtypeStruct((s, LANES), F32)])(dk_raw)


ATTN_SCALE = 1.0 / math.sqrt(QK_DIM)
MASKED = -1e30


ATTN_TILE = 512


def _chunk_mask(q0, k0, shape, q_axis):
    qpos = q0 + lax.broadcasted_iota(jnp.int32, shape, q_axis)
    kpos = k0 + lax.broadcasted_iota(jnp.int32, shape, 1 - q_axis)
    return kpos // CHUNK <= qpos // CHUNK


def _attn_fwd(q, k, vt, *, name):
    s = q.shape[0]
    t = _tile(s, ATTN_TILE)
    n = s // t

    def body(q_ref, k_ref, vt_ref, o_ref, lse_ref, m_sc, l_sc, acc):
        qi, kj = pl.program_id(1), pl.program_id(2)

        @pl.when(kj == 0)
        def _():
            m_sc[...] = jnp.full_like(m_sc, MASKED)
            l_sc[...] = jnp.zeros_like(l_sc)
            acc[...] = jnp.zeros_like(acc)

        def step(masked):
            st = _nt(k_ref[...], q_ref[...]) * ATTN_SCALE
            if masked:
                st = jnp.where(_chunk_mask(qi * t, kj * t, (t, t), 1), st, MASKED)
            m_prev = m_sc[...]
            m_new = jnp.maximum(m_prev, jnp.max(st, axis=0, keepdims=True))
            alpha = jnp.exp(m_prev - m_new)
            pt = jnp.exp(st - m_new)
            l_sc[...] = alpha * l_sc[...] + jnp.sum(pt, axis=0, keepdims=True)
            acc[...] = alpha * acc[...] + _nn(vt_ref[...], pt)
            m_sc[...] = m_new

        @pl.when(kj < qi)
        def _():
            step(False)

        @pl.when(kj == qi)
        def _():
            step(True)

        @pl.when(kj == n - 1)
        def _():
            o_ref[...] = acc[...] / l_sc[...]
            lse_ref[...] = m_sc[...] + jnp.log(l_sc[...])

    return pl.pallas_call(
        body, name=name, grid=(N_HEADS, n, n),
        in_specs=[pl.BlockSpec((t, HEAD_PAD), lambda h, i, j: (i, h)),
                  pl.BlockSpec((t, HEAD_PAD), lambda h, i, j: (jnp.minimum(j, i), h)),
                  pl.BlockSpec((V_DIM, t), lambda h, i, j: (h, jnp.minimum(j, i)))],
        out_specs=[pl.BlockSpec((V_DIM, t), lambda h, i, j: (h, i)), pl.BlockSpec((None, 1, t), lambda h, i, j: (h, 0, i))],
        out_shape=[jax.ShapeDtypeStruct((N_HEADS * V_DIM, s), F32), jax.ShapeDtypeStruct((N_HEADS, 1, s), F32)],
        scratch_shapes=[pltpu.VMEM((1, t), F32), pltpu.VMEM((1, t), F32), pltpu.VMEM((V_DIM, t), F32)])(q, k, vt)


def _attn_delta(ot, dot, *, name):
    s = ot.shape[1]
    t = _tile(s, 1024)

    def body(o_ref, do_ref, d_ref):
        d_ref[...] = jnp.sum(o_ref[...] * do_ref[...], axis=0, keepdims=True)

    blk = pl.BlockSpec((V_DIM, t), lambda h, i: (h, i))
    return pl.pallas_call(
        body, name=name, grid=(N_HEADS, s // t), in_specs=[blk, blk],
        out_specs=pl.BlockSpec((None, 1, t), lambda h, i: (h, 0, i)),
        out_shape=jax.ShapeDtypeStruct((N_HEADS, 1, s), F32))(ot, dot)


def _attn_bwd_dq(q, k, v, do, lse_col, delta_col, *, name):
    s = q.shape[0]
    t = _tile(s, ATTN_TILE)
    n = s // t

    def body(q_ref, k_ref, v_ref, do_ref, lse_ref, dl_ref, dq_ref, acc):
        qi, kj = pl.program_id(1), pl.program_id(2)

        @pl.when(kj == 0)
        def _():
            acc[...] = jnp.zeros_like(acc)

        def step(masked):
            kv = k_ref[...]
            sc = _nt(q_ref[...], kv) * ATTN_SCALE
            if masked:
                sc = jnp.where(_chunk_mask(qi * t, kj * t, (t, t), 0), sc, MASKED)
            p = jnp.exp(sc - lse_ref[...])
            dp = _nt(do_ref[...], v_ref[...])
            acc[...] += _nn(p * (dp - dl_ref[...]), kv)

        @pl.when(kj < qi)
        def _():
            step(False)

        @pl.when(kj == qi)
        def _():
            step(True)

        @pl.when(kj == n - 1)
        def _():
            dq_ref[...] = acc[...] * ATTN_SCALE

    def kmap(h, i, j):
        return (jnp.minimum(j, i), h)

    col = pl.BlockSpec((None, t, 1), lambda h, i, j: (h, i, 0))
    return pl.pallas_call(
        body, name=name, grid=(N_HEADS, n, n),
        in_specs=[pl.BlockSpec((t, HEAD_PAD), lambda h, i, j: (i, h)), pl.BlockSpec((t, HEAD_PAD), kmap),
                  pl.BlockSpec((t, V_DIM), kmap), pl.BlockSpec((t, V_DIM), lambda h, i, j: (i, h)), col, col],
        out_specs=pl.BlockSpec((t, HEAD_PAD), lambda h, i, j: (i, h)),
        out_shape=jax.ShapeDtypeStruct((s, N_HEADS * HEAD_PAD), F32),
        scratch_shapes=[pltpu.VMEM((t, HEAD_PAD), F32)])(q, k, v, do, lse_col, delta_col)


def _attn_bwd_dkv(q, k, v, do, lse_row, delta_row, dk_in, dv_in, *, name):
    s = q.shape[0]
    t = _tile(s, ATTN_TILE)
    n = s // t
    has_in = dk_in is not None

    def body(*refs):
        if has_in:
            q_ref, k_ref, v_ref, do_ref, lse_ref, dl_ref, dki_ref, dvi_ref, dk_ref, dv_ref, acck, accv = refs
        else:
            q_ref, k_ref, v_ref, do_ref, lse_ref, dl_ref, dk_ref, dv_ref, acck, accv = refs
        kj, qi = pl.program_id(1), pl.program_id(2)

        @pl.when(qi == 0)
        def _():
            acck[...] = jnp.zeros_like(acck)
            accv[...] = jnp.zeros_like(accv)

        def step(masked):
            qv, dov = q_ref[...], do_ref[...]
            st = _nt(k_ref[...], qv) * ATTN_SCALE
            if masked:
                st = jnp.where(_chunk_mask(qi * t, kj * t, (t, t), 1), st, MASKED)
            pt = jnp.exp(st - lse_ref[...])
            accv[...] += _nn(pt, dov)
            dpt = _nt(v_ref[...], dov)
            acck[...] += _nn(pt * (dpt - dl_ref[...]), qv)

        @pl.when(qi > kj)
        def _():
            step(False)

        @pl.when(qi == kj)
        def _():
            step(True)

        @pl.when(qi == n - 1)
        def _():
            dk = acck[...] * ATTN_SCALE
            dv = accv[...]
            if has_in:
                dk = dki_ref[...] + dk
                dv = dvi_ref[...] + dv
            dk_ref[...] = dk
            dv_ref[...] = dv

    def qmap(h, j, i):
        return (jnp.maximum(i, j), h)

    row = pl.BlockSpec((None, 1, t), lambda h, j, i: (h, 0, jnp.maximum(i, j)))
    k_spec = pl.BlockSpec((t, HEAD_PAD), lambda h, j, i: (j, h))
    v_spec = pl.BlockSpec((t, V_DIM), lambda h, j, i: (j, h))
    in_specs = [pl.BlockSpec((t, HEAD_PAD), qmap), k_spec, v_spec, pl.BlockSpec((t, V_DIM), qmap), row, row]
    args = [q, k, v, do, lse_row, delta_row]
    if has_in:
        in_specs += [k_spec, v_spec]
        args += [dk_in, dv_in]
    return pl.pallas_call(
        body, name=name, grid=(N_HEADS, n, n), in_specs=in_specs, out_specs=[k_spec, v_spec],
        out_shape=[jax.ShapeDtypeStruct((s, N_HEADS * HEAD_PAD), F32), jax.ShapeDtypeStruct((s, N_HEADS * V_DIM), F32)],
        scratch_shapes=[pltpu.VMEM((t, HEAD_PAD), F32), pltpu.VMEM((t, V_DIM), F32)])(*args)


def _loss_head(y, target, *, name):
    s, d = y.shape
    tm = _tile(s, 512)

    def body(y_ref, t_ref, dy_ref, l_ref):
        @pl.when(pl.program_id(0) == 0)
        def _():
            l_ref[...] = jnp.zeros_like(l_ref)

        err = y_ref[...] - t_ref[...]
        dy_ref[...] = err * (1.0 / d)
        sq = _fold8(err * err)
        part = sq[:, 0:LANES]
        for cb in range(1, d // LANES):
            part = part + sq[:, cb * LANES:(cb + 1) * LANES]
        l_ref[...] += part * (0.5 / d)

    row_spec = pl.BlockSpec((tm, d), lambda i: (i, 0))
    return pl.pallas_call(
        body, name=name, grid=(s // tm,), in_specs=[row_spec, row_spec],
        out_specs=[row_spec, pl.BlockSpec((SUBLANES, LANES), lambda i: (0, 0))],
        out_shape=[jax.ShapeDtypeStruct((s, d), F32), jax.ShapeDtypeStruct((SUBLANES, LANES), F32)])(y, target)


ADAMW_ROWS = 512


def _adamw(w, m, v, g_parts, *, name):
    rows, cols = w.shape
    tm = _tile(rows, ADAMW_ROWS)
    n_parts = len(g_parts)
    c1 = 1.0 - ADAM_B1 ** ADAM_STEP
    c2 = 1.0 - ADAM_B2 ** ADAM_STEP

    def body(*refs):
        w_ref, m_ref, v_ref = refs[:3]
        g_refs = refs[3:3 + n_parts]
        g_out, d_out, m_out, v_out = refs[3 + n_parts:]
        g = g_refs[0][...]
        for r in g_refs[1:]:
            g = g + r[...]
        mn = ADAM_B1 * m_ref[...] + (1.0 - ADAM_B1) * g
        vn = ADAM_B2 * v_ref[...] + (1.0 - ADAM_B2) * (g * g)
        g_out[...] = g
        m_out[...] = mn
        v_out[...] = vn
        d_out[...] = -ADAM_LR * ((mn / c1) / (jnp.sqrt(vn / c2) + ADAM_EPS) + ADAM_WD * w_ref[...])

    spec = pl.BlockSpec((tm, cols), lambda i: (i, 0))
    out = jax.ShapeDtypeStruct((rows, cols), F32)
    return pl.pallas_call(
        body, name=name, grid=(rows // tm,), in_specs=[spec] * (3 + n_parts),
        out_specs=[spec] * 4, out_shape=[out] * 4)(w, m, v, *g_parts)


def _sum_slots(parts, *, name):
    _, rows, cols = parts.shape
    tm = _tile(rows, 512)

    def body(p_ref, o_ref):
        acc = p_ref[0].astype(F32)
        for k in range(1, N_SHARD):
            acc = acc + p_ref[k].astype(F32)
        o_ref[...] = acc

    return pl.pallas_call(
        body, name=name, grid=(rows // tm,),
        in_specs=[pl.BlockSpec((N_SHARD, tm, cols), lambda i: (0, i, 0))],
        out_specs=pl.BlockSpec((tm, cols), lambda i: (i, 0)),
        out_shape=jax.ShapeDtypeStruct((rows, cols), F32))(parts)


def _mesh_pos():
    return lax.axis_index("x"), lax.axis_index("y"), lax.axis_index("c")


CHIP_FLIPS = ((1, 0), (0, 1), (1, 1))


class Exchange(NamedTuple):
    kind: str
    srcs: tuple
    lands: tuple
    layer: Any = None


HBM_SPEC = pl.BlockSpec(memory_space=pltpu.HBM)
SEM_SPEC = pl.BlockSpec(memory_space=pltpu.SEMAPHORE)
DATAFLOW = pltpu.SideEffectType.DATAFLOW_SIDE_EFFECTING


def _exchange_copies(ex, src_refs, land_refs, send_sems, recv_sems):
    x, y, c = _mesh_pos()
    mine = 2 * x + y

    def slot(ref, chip):
        return ref.at[chip] if ex.layer is None else ref.at[chip, ex.layer]

    pairs = []
    for a, (src, land) in enumerate(zip(src_refs, land_refs)):
        for k, (fx, fy) in enumerate(CHIP_FLIPS):
            px, py = x ^ fx, y ^ fy
            peer = 2 * px + py
            src_part = src if ex.kind == "gather" else src.at[peer]
            pair = a * len(CHIP_FLIPS) + k
            common = dict(src_ref=src_part, send_sem=send_sems.at[pair], recv_sem=recv_sems.at[pair],
                          device_id=(px, py, c), device_id_type=MESH)
            pairs.append((pltpu.make_async_remote_copy(dst_ref=slot(land, mine), **common),
                          pltpu.make_async_remote_copy(dst_ref=slot(land, peer), **common)))
    return pairs


def _exchange_start(exchanges, *, name):
    srcs = [s for ex in exchanges for s in ex.srcs]
    lands = [b for ex in exchanges for b in ex.lands]
    n_arr, n_ex = len(srcs) + len(lands), len(exchanges)

    def body(*refs):
        src_refs, land_refs = refs[:len(srcs)], refs[len(srcs):n_arr]
        sems, token = refs[n_arr:n_arr + 2 * n_ex], refs[-1]
        at = 0
        for e, ex in enumerate(exchanges):
            n = len(ex.srcs)
            for send, _ in _exchange_copies(ex, src_refs[at:at + n], land_refs[at:at + n], sems[2 * e], sems[2 * e + 1]):
                send.start()
            at += n
        token[...] = jnp.zeros_like(token)

    sem_shapes = [pltpu.SemaphoreType.DMA((len(ex.srcs) * len(CHIP_FLIPS),)) for ex in exchanges for _ in range(2)]
    out = pl.pallas_call(
        body, name=name,
        out_shape=sem_shapes + [pltpu.HBM(a.shape, a.dtype) for a in srcs + lands] + [jax.ShapeDtypeStruct((SUBLANES, LANES), F32)],
        in_specs=[HBM_SPEC] * n_arr, out_specs=[SEM_SPEC] * (2 * n_ex) + [HBM_SPEC] * n_arr + [VMEM_SPEC],
        input_output_aliases={i: 2 * n_ex + i for i in range(n_arr)},
        compiler_params=pltpu.CompilerParams(has_side_effects=DATAFLOW),
    )(*[pltpu.with_memory_space_constraint(a, pltpu.HBM) for a in srcs + lands])
    sems, thru = out[:2 * n_ex], out[2 * n_ex:-1]
    pending, at = [], 0
    for e, ex in enumerate(exchanges):
        n = len(ex.srcs)
        pending.append((ex._replace(srcs=tuple(thru[at:at + n]), lands=tuple(thru[len(srcs) + at:len(srcs) + at + n])),
                        sems[2 * e], sems[2 * e + 1]))
        at += n
    return pending


def _exchange_wait(pending, after, *, name):
    ex, send_sems, recv_sems = pending
    n = len(ex.srcs)

    def body(*refs):
        src_refs, land_refs = refs[:n], refs[n:2 * n]
        for send, arrive in _exchange_copies(ex, src_refs, land_refs, refs[2 * n], refs[2 * n + 1]):
            send.wait_send()
            arrive.wait_recv()

    arrays = list(ex.srcs) + list(ex.lands)
    out = pl.pallas_call(
        body, name=name, out_shape=[pltpu.HBM(a.shape, a.dtype) for a in arrays],
        in_specs=[HBM_SPEC] * (2 * n) + [SEM_SPEC, SEM_SPEC, ANY], out_specs=[HBM_SPEC] * (2 * n),
        input_output_aliases={i: i for i in range(2 * n)},
        compiler_params=pltpu.CompilerParams(has_side_effects=DATAFLOW),
    )(*arrays, send_sems, recv_sems, after)
    return out[n:]


def _swap_with_sibling(arrays, *, name):
    n = len(arrays)

    def body(*refs):
        ins, outs = refs[:n], refs[n:2 * n]
        send_sems, recv_sems = refs[2 * n:]
        x, y, c = _mesh_pos()
        copies = []
        for a in range(n):
            cp = pltpu.make_async_remote_copy(
                src_ref=ins[a], dst_ref=outs[a], send_sem=send_sems.at[a], recv_sem=recv_sems.at[a],
                device_id=(x, y, 1 - c), device_id_type=MESH)
            cp.start()
            copies.append(cp)
        for cp in copies:
            cp.wait()

    return pl.pallas_call(
        body, name=name, in_specs=[ANY] * n, out_specs=[ANY] * n,
        out_shape=[jax.ShapeDtypeStruct(a.shape, a.dtype) for a in arrays],
        scratch_shapes=[pltpu.SemaphoreType.DMA((n,)), pltpu.SemaphoreType.DMA((n,))])(*arrays)


N_DEV = 8


def _all_reduce_small(vec, *, name):
    rows = vec.shape[0]

    def body(v_ref, o_ref, land, send_sems, recv_sems):
        x, y, c = _mesh_pos()
        me = 4 * x + 2 * y + c
        land[me] = v_ref[...]
        copies = []
        for k in range(1, N_DEV):
            fx, fy, fc = (k >> 2) & 1, (k >> 1) & 1, k & 1
            px, py, pc = x ^ fx, y ^ fy, c ^ fc
            send = pltpu.make_async_remote_copy(
                src_ref=v_ref, dst_ref=land.at[me], send_sem=send_sems.at[k - 1], recv_sem=recv_sems.at[k - 1],
                device_id=(px, py, pc), device_id_type=MESH)
            send.start()
            arrive = pltpu.make_async_remote_copy(
                src_ref=v_ref, dst_ref=land.at[4 * px + 2 * py + pc], send_sem=send_sems.at[k - 1], recv_sem=recv_sems.at[k - 1],
                device_id=(px, py, pc), device_id_type=MESH)
            copies.append((send, arrive))
        for send, arrive in copies:
            send.wait_send()
            arrive.wait_recv()
        acc = land[0]
        for k in range(1, N_DEV):
            acc = acc + land[k]
        o_ref[...] = acc

    return pl.pallas_call(
        body, name=name, in_specs=[VMEM_SPEC], out_specs=VMEM_SPEC,
        out_shape=jax.ShapeDtypeStruct(vec.shape, F32),
        scratch_shapes=[pltpu.VMEM((N_DEV, rows, LANES), F32), pltpu.SemaphoreType.DMA((N_DEV - 1,)),
                        pltpu.SemaphoreType.DMA((N_DEV - 1,))])(vec)


PACK_UNIT = SUBLANES * LANES * 2


def _padded(n):
    return -(-n // PACK_UNIT) * PACK_UNIT


def _pack(arrays, dtype, lead=0):
    parts = []
    for a in arrays:
        lead_shape = a.shape[:lead]
        flat = a.astype(dtype).reshape(lead_shape + (-1,))
        n = flat.shape[-1]
        flat = jnp.pad(flat, [(0, 0)] * lead + [(0, _padded(n) - n)])
        parts.append(flat.reshape(lead_shape + (-1, LANES)))
    return jnp.concatenate(parts, axis=lead)


def _unpack(buf, shapes, lead=0):
    out, row = [], 0
    for shp in shapes:
        n = math.prod(shp)
        rows = _padded(n) // LANES
        part = lax.slice_in_dim(buf, row, row + rows, axis=lead)
        lead_shape = part.shape[:lead]
        part = part.reshape(lead_shape + (-1,))
        part = lax.slice_in_dim(part, 0, n, axis=lead)
        out.append(part.reshape(lead_shape + tuple(shp)))
        row += rows
    return out


def kernel(x, positions, ln_mix_a, w_pool, b_pool, pool_scale, ln_ffn, w_gate, w_up, w_down, ln_kv, w_dkv, g_kv_latent, w_uk, w_uv, g_k, ln_mix_b, w_dq, g_q_latent, w_uq, g_q, w_o, loss_target, m_ln_mix_a, m_w_pool, m_b_pool, m_pool_scale, m_ln_ffn, m_w_gate, m_w_up, m_w_down, m_ln_kv, m_w_dkv, m_g_kv_latent, m_w_uk, m_w_uv, m_g_k, m_ln_mix_b, m_w_dq, m_g_q_latent, m_w_uq, m_g_q, m_w_o, v_ln_mix_a, v_w_pool, v_b_pool, v_pool_scale, v_ln_ffn, v_w_gate, v_w_up, v_w_down, v_ln_kv, v_w_dkv, v_g_kv_latent, v_w_uk, v_w_uv, v_g_k, v_ln_mix_b, v_w_dq, v_g_q_latent, v_w_uq, v_g_q, v_w_o):
    weights = dict(ln_mix_a=ln_mix_a, w_pool=w_pool, b_pool=b_pool, pool_scale=pool_scale, ln_ffn=ln_ffn, w_gate=w_gate,
                   w_up=w_up, w_down=w_down, ln_kv=ln_kv, w_dkv=w_dkv, g_kv_latent=g_kv_latent, w_uk=w_uk, w_uv=w_uv, g_k=g_k,
                   ln_mix_b=ln_mix_b, w_dq=w_dq, g_q_latent=g_q_latent, w_uq=w_uq, g_q=g_q, w_o=w_o)
    mom_m = dict(ln_mix_a=m_ln_mix_a, w_pool=m_w_pool, b_pool=m_b_pool, pool_scale=m_pool_scale, ln_ffn=m_ln_ffn,
                 w_gate=m_w_gate, w_up=m_w_up, w_down=m_w_down, ln_kv=m_ln_kv, w_dkv=m_w_dkv, g_kv_latent=m_g_kv_latent,
                 w_uk=m_w_uk, w_uv=m_w_uv, g_k=m_g_k, ln_mix_b=m_ln_mix_b, w_dq=m_w_dq, g_q_latent=m_g_q_latent,
                 w_uq=m_w_uq, g_q=m_g_q, w_o=m_w_o)
    mom_v = dict(ln_mix_a=v_ln_mix_a, w_pool=v_w_pool, b_pool=v_b_pool, pool_scale=v_pool_scale, ln_ffn=v_ln_ffn,
                 w_gate=v_w_gate, w_up=v_w_up, w_down=v_w_down, ln_kv=v_ln_kv, w_dkv=v_w_dkv, g_kv_latent=v_g_kv_latent,
                 w_uk=v_w_uk, w_uv=v_w_uv, g_k=v_g_k, ln_mix_b=v_ln_mix_b, w_dq=v_w_dq, g_q_latent=v_g_q_latent,
                 w_uq=v_w_uq, g_q=v_g_q, w_o=v_w_o)
    order = list(weights)
    s = x.shape[1]
    d = D_MODEL
    xs = x.reshape(s, d)
    target = loss_target.reshape(s, d)
    my_chip = 2 * lax.axis_index("x") + lax.axis_index("y")

    mat_names = ("w_pool", "w_dkv", "w_uk", "w_uv", "w_dq", "w_uq", "w_o")
    vec_names = ("ln_mix_a", "b_pool", "pool_scale")
    mat_shapes = [weights[n].shape for n in mat_names]
    vec_shapes = [weights[n].shape for n in vec_names]
    mats_local = _pack([weights[n] for n in mat_names], WIRE_DTYPE)
    vecs_local = _pack([weights[n] for n in vec_names], F32)

    def landing(shard):
        return lax.dynamic_update_slice_in_dim(lax.empty((N_SHARD,) + shard.shape, shard.dtype), shard[None], my_chip, axis=0)

    gathers = [Exchange("gather", (mats_local, vecs_local), (landing(mats_local), landing(vecs_local)))]
    for l in range(DEPTH):
        shards = tuple(w[l].astype(WIRE_DTYPE) for w in (w_gate, w_up, w_down))
        gathers.append(Exchange("gather", shards, tuple(landing(sh) for sh in shards)))
    gathering = _exchange_start(gathers, name="gather_start")

    inv = ROPE_THETA ** (-jnp.arange(ROPE // 2, dtype=F32) * 2.0 / ROPE)
    inv_lanes = jnp.concatenate([inv, inv, jnp.zeros((LANES - ROPE,), F32)]).reshape(1, LANES)
    cos_t, sin_t = _rope_tables(positions.reshape(s, 1).astype(F32), inv_lanes, name="rope_tables")

    mats_all, vecs_all = _exchange_wait(gathering[0], cos_t, name="gather_wait_small")
    g_pool, g_dkv, g_uk, g_uv, g_dq, g_uq, g_o = _unpack(mats_all, mat_shapes, lead=1)
    g_lna, g_bp, g_ps = _unpack(vecs_all, vec_shapes, lead=1)

    wpool_f = g_pool.transpose(1, 2, 0, 3, 4).reshape(N_A, N_GROUPS, GROUP_DIM, GROUP_DIM)
    bpool_f = g_bp.transpose(1, 2, 0, 3).reshape(N_A, 1, d)
    pscale_f = g_ps.transpose(1, 0, 2).reshape(N_A, 1, d)
    lna_f = g_lna.transpose(1, 0, 2).reshape(N_A, 1, d)
    wdkv_f = jnp.pad(g_dkv.reshape(d, KV_LORA + ROPE), ((0, 0), (0, CKV_PAD - KV_LORA - ROPE)))
    wuk_f = g_uk.transpose(1, 0, 2).reshape(KV_LORA, N_HEADS * NOPE)
    wuv_f = g_uv.transpose(1, 0, 2).reshape(KV_LORA, N_HEADS * V_DIM)
    wdq_f = g_dq.transpose(1, 0, 2, 3).reshape(N_B, d, Q_LORA)
    wuq_f = jnp.pad(g_uq.transpose(1, 2, 0, 3).reshape(N_B, Q_LORA, N_HEADS, QK_DIM),
                    ((0, 0), (0, 0), (0, 0), (0, HEAD_PAD - QK_DIM))).reshape(N_B, Q_LORA, N_HEADS * HEAD_PAD)
    wo_f = g_o.transpose(1, 0, 2, 3).reshape(N_B, d, d)

    def head_gain(g):
        return jnp.pad(g.reshape(1, QK_DIM), ((0, 0), (0, HEAD_PAD - QK_DIM)))

    ffn_w = [None] * DEPTH

    def ffn_fwd(xin, layer):
        hf = _rms_fwd(xin, ln_ffn[layer].reshape(1, d), n=d, name="ffn_norm")
        ffn_w[layer] = wg, wu, wd = _exchange_wait(gathering[1 + layer], hf, name=f"gather_wait_{layer}")
        a, b, u = _ffn_up(hf, wg, wu, name="ffn_up")
        return _ffn_down(u, wd, xin, name="ffn_down"), (xin, hf, a, b, u)

    saved_a, saved_b, saved_f = [], [], []
    cur = xs
    for l in range(N_A):
        dpool = _rms_pool_fwd(cur, lna_f[l], name="pool_fwd")
        x1 = _pool_mm_fwd(dpool, wpool_f[l], bpool_f[l], pscale_f[l], cur, name="pool_mm")
        saved_a.append((cur, dpool))
        cur, sf = ffn_fwd(x1, l)
        saved_f.append(sf)

    x_kv = cur
    hk = _rms_fwd(x_kv, ln_kv.reshape(1, d), n=d, name="kv_norm")
    ckv = _mm(hk, wdkv_f, name="kv_down")
    c_lat = _rms_fwd(ckv, g_kv_latent.reshape(1, KV_LORA), n=KV_LORA, name="kv_latent_norm")
    kn_raw = _mm(c_lat, wuk_f, name="k_up")
    v_all = _mm(c_lat, wuv_f, out_dtype=MXU_DTYPE, name="v_up")
    vt_all = _mm(wuv_f.T, c_lat, tb=True, out_dtype=MXU_DTYPE, name="v_up_t")
    k_raw = _k_assemble(kn_raw, ckv, name="k_assemble")
    gk_pad = head_gain(g_k)
    k_cat = _head_norm_rope_fwd(k_raw, gk_pad, cos_t, sin_t, name="k_norm_rope")

    for j in range(N_B):
        l = N_A + j
        hq = _rms_fwd(cur, ln_mix_b[j].reshape(1, d), n=d, name="q_norm")
        cq_raw = _mm(hq, wdq_f[j], name="q_down")
        cq = _rms_fwd(cq_raw, g_q_latent[j].reshape(1, Q_LORA), n=Q_LORA, name="q_latent_norm")
        q_raw = _mm(cq, wuq_f[j], name="q_up")
        gq_pad = head_gain(g_q[j])
        q_cat = _head_norm_rope_fwd(q_raw, gq_pad, cos_t, sin_t, name="q_norm_rope")
        ot, lse = _attn_fwd(q_cat, k_cat, vt_all, name="attn_fwd")
        x1 = _mm(ot, wo_f[j], ta=True, resid=cur, name="attn_out")
        saved_b.append((cur, hq, cq_raw, cq, q_raw, gq_pad, q_cat, ot, lse))
        cur, sf = ffn_fwd(x1, l)
        saved_f.append(sf)

    dy, loss_part = _loss_head(cur, target, name="loss_head")

    ffn_landed = (lax.empty((N_SHARD, DEPTH, d, FF_SHARD), WIRE_DTYPE), lax.empty((N_SHARD, DEPTH, d, FF_SHARD), WIRE_DTYPE),
                  lax.empty((N_SHARD, DEPTH, FF_SHARD, d), WIRE_DTYPE))
    scattering = None
    grads = {}
    d_ln_ffn = [None] * DEPTH

    def own_part(full):
        return lax.dynamic_index_in_dim(full, my_chip, axis=0, keepdims=True)

    def ffn_bwd(dyv, layer):
        nonlocal ffn_landed, scattering
        xin, hf, a, b, u = saved_f[layer]
        wg, wu, wd = ffn_w[layer]
        da, db = _ffn_bwd_hidden(dyv, wd, a, b, name="ffn_bwd_hidden")
        dwd = _ffn_bwd_dwd(u, dyv, name="ffn_bwd_dwd")
        dwg, dwu = _ffn_bwd_dwgu(hf, da, db, name="ffn_bwd_dwgu")
        dhf = _ffn_bwd_dh(da, db, wg, wu, name="ffn_bwd_dh")
        dx, dg = _rms_bwd(xin, ln_ffn[layer].reshape(1, d), dhf, n=d, dx_in=dyv, name="ffn_norm_bwd")
        d_ln_ffn[layer] = dg.sum(axis=0)
        if scattering is not None:
            ffn_landed = _exchange_wait(scattering, dx, name=f"scatter_wait_{layer + 1}")
        ffn_landed = tuple(lax.dynamic_update_slice(buf, own_part(g)[:, None], (my_chip, layer, 0, 0))
                           for buf, g in zip(ffn_landed, (dwg, dwu, dwd)))
        scattering = _exchange_start([Exchange("scatter", (dwg, dwu, dwd), ffn_landed, layer)], name=f"scatter_start_{layer}")[0]
        return dx

    dk_acc = dv_acc = None
    d_ln_mix_b, d_w_dq, d_g_q_latent, d_w_uq, d_g_q, d_w_o = ([None] * N_B for _ in range(6))
    dcur = dy
    for j in reversed(range(N_B)):
        l = N_A + j
        xin, hq, cq_raw, cq, q_raw, gq_pad, q_cat, ot, lse = saved_b[j]
        dx1 = ffn_bwd(dcur, l)
        do = _mm(dx1, wo_f[j], tb=True, out_dtype=MXU_DTYPE, name="attn_out_bwd")
        dot = _mm(wo_f[j], dx1, tb=True, name="attn_out_bwd_t")
        d_w_o[j] = _mm_tn(ot, dx1, at=True, name="attn_out_dw")
        delta = _attn_delta(ot, dot, name="attn_delta")
        lse_col, delta_col = lse.reshape(N_HEADS, s, 1), delta.reshape(N_HEADS, s, 1)
        dq_cat = _attn_bwd_dq(q_cat, k_cat, v_all, do, lse_col, delta_col, name="attn_bwd_dq")
        dk_acc, dv_acc = _attn_bwd_dkv(q_cat, k_cat, v_all, do, lse, delta, dk_acc, dv_acc, name="attn_bwd_dkv")
        dq_raw, dgq = _head_norm_rope_bwd(q_raw, gq_pad, cos_t, sin_t, dq_cat, name="q_norm_rope_bwd")
        d_g_q[j] = dgq.sum(axis=0)[:QK_DIM]
        dcq = _mm(dq_raw, wuq_f[j], tb=True, name="q_up_bwd")
        d_w_uq[j] = _mm_tn(cq, dq_raw, name="q_up_dw").reshape(Q_LORA, N_HEADS, HEAD_PAD)[:, :, :QK_DIM].reshape(Q_LORA, N_HEADS * QK_DIM)
        dcq_raw, dgl = _rms_bwd(cq_raw, g_q_latent[j].reshape(1, Q_LORA), dcq, n=Q_LORA, name="q_latent_norm_bwd")
        d_g_q_latent[j] = dgl.sum(axis=0)
        dhq = _mm(dcq_raw, wdq_f[j], tb=True, name="q_down_bwd")
        d_w_dq[j] = _mm_tn(hq, dcq_raw, name="q_down_dw")
        dcur, dgm = _rms_bwd(xin, ln_mix_b[j].reshape(1, d), dhq, n=d, dx_in=dx1, name="q_norm_bwd")
        d_ln_mix_b[j] = dgm.sum(axis=0)

    dk_raw, dgk = _head_norm_rope_bwd(k_raw, gk_pad, cos_t, sin_t, dk_acc, name="k_norm_rope_bwd")
    grads["g_k"] = dgk.sum(axis=0)[:QK_DIM]
    dc = _mm(dv_acc, wuv_f, tb=True, name="v_up_bwd")
    grads["w_uv"] = _mm_tn(c_lat, dv_acc, name="v_up_dw")
    dkn, dpe = _k_disassemble(dk_raw, name="k_disassemble")
    dc = _mm(dkn, wuk_f, tb=True, resid=dc, name="k_up_bwd")
    grads["w_uk"] = _mm_tn(c_lat, dkn, name="k_up_dw")
    dc_raw, dgl = _rms_bwd(ckv, g_kv_latent.reshape(1, KV_LORA), dc, n=KV_LORA, name="kv_latent_norm_bwd")
    grads["g_kv_latent"] = dgl.sum(axis=0)
    dckv = jnp.concatenate([dc_raw, dpe], axis=1)
    dhk = _mm(dckv, wdkv_f, tb=True, name="kv_down_bwd")
    grads["w_dkv"] = _mm_tn(hk, dckv, name="kv_down_dw")[:, :KV_LORA + ROPE]
    dcur, dg = _rms_bwd(x_kv, ln_kv.reshape(1, d), dhk, n=d, dx_in=dcur, name="kv_norm_bwd")
    grads["ln_kv"] = dg.sum(axis=0)

    d_ln_mix_a, d_w_pool, d_b_pool, d_pool_scale = ([None] * N_A for _ in range(4))
    for l in reversed(range(N_A)):
        xin, dpool = saved_a[l]
        dx1 = ffn_bwd(dcur, l)
        dd, dwp, dbp, dsp = _pool_mm_bwd(dpool, wpool_f[l], bpool_f[l], pscale_f[l], dx1, name="pool_mm_bwd")
        d_w_pool[l], d_b_pool[l], d_pool_scale[l] = dwp, dbp.sum(axis=0), dsp.sum(axis=0)
        dcur, dg = _rms_pool_bwd(xin, lna_f[l], dd, dx1, name="pool_bwd")
        d_ln_mix_a[l] = dg.sum(axis=0)
    grad_x = dcur.reshape(1, s, d)

    gm = {
        "w_pool": jnp.stack(d_w_pool).reshape(N_A, N_GROUPS, N_SHARD, GROUP_DIM // N_SHARD, GROUP_DIM).transpose(2, 0, 1, 3, 4),
        "w_dkv": grads["w_dkv"].reshape(N_SHARD, d // N_SHARD, KV_LORA + ROPE),
        "w_uk": grads["w_uk"].reshape(KV_LORA, N_SHARD, -1).transpose(1, 0, 2),
        "w_uv": grads["w_uv"].reshape(KV_LORA, N_SHARD, -1).transpose(1, 0, 2),
        "w_dq": jnp.stack(d_w_dq).reshape(N_B, N_SHARD, d // N_SHARD, Q_LORA).transpose(1, 0, 2, 3),
        "w_uq": jnp.stack(d_w_uq).reshape(N_B, Q_LORA, N_SHARD, -1).transpose(2, 0, 1, 3),
        "w_o": jnp.stack(d_w_o).reshape(N_B, N_SHARD, d // N_SHARD, d).transpose(1, 0, 2, 3),
    }
    mats_grad = _pack([gm[n] for n in mat_names], WIRE_DTYPE, lead=1)
    mats_landing = lax.dynamic_update_slice_in_dim(lax.empty(mats_grad.shape, WIRE_DTYPE), own_part(mats_grad), my_chip, axis=0)
    mats_scatter = _exchange_start([Exchange("scatter", (mats_grad,), (mats_landing,))], name="scatter_start_small")[0]
    ffn_landed = _exchange_wait(scattering, dcur, name="scatter_wait_0")
    (mats_landed,) = _exchange_wait(mats_scatter, ffn_landed[0], name="scatter_wait_small")
    landed = [ffn_landed[0].reshape(N_SHARD, DEPTH * d, FF_SHARD), ffn_landed[1].reshape(N_SHARD, DEPTH * d, FF_SHARD),
              ffn_landed[2].reshape(N_SHARD, DEPTH * FF_SHARD, d), mats_landed]
    chip_sums = [_sum_slots(p, name="sum_chips") for p in landed]
    sib_sums = _swap_with_sibling(chip_sums, name="swap_sibling")

    vec_full = {
        "ln_mix_a": jnp.stack(d_ln_mix_a), "b_pool": jnp.stack(d_b_pool).reshape(N_A, N_GROUPS, GROUP_DIM),
        "pool_scale": jnp.stack(d_pool_scale), "ln_ffn": jnp.stack(d_ln_ffn), "ln_kv": grads["ln_kv"],
        "g_kv_latent": grads["g_kv_latent"], "g_k": grads["g_k"], "ln_mix_b": jnp.stack(d_ln_mix_b),
        "g_q_latent": jnp.stack(d_g_q_latent), "g_q": jnp.stack(d_g_q),
    }
    small_names = list(vec_full)
    small_shapes = [vec_full[n].shape for n in small_names] + [(SUBLANES * LANES,)]
    small = _all_reduce_small(_pack([vec_full[n] for n in small_names] + [loss_part.reshape(-1)], F32), name="all_reduce_small")
    small_sum = _unpack(small, small_shapes)
    loss = jnp.sum(small_sum[-1])
    vec_grad = dict(zip(small_names, small_sum[:-1]))
    vec_grad["ln_mix_a"] = lax.dynamic_slice_in_dim(vec_grad["ln_mix_a"], my_chip * (d // N_SHARD), d // N_SHARD, axis=1)
    vec_grad["pool_scale"] = lax.dynamic_slice_in_dim(vec_grad["pool_scale"], my_chip * (d // N_SHARD), d // N_SHARD, axis=1)
    vec_grad["b_pool"] = lax.dynamic_slice_in_dim(vec_grad["b_pool"], my_chip * (GROUP_DIM // N_SHARD), GROUP_DIM // N_SHARD, axis=2)

    out_g, out_d, out_m, out_v = {}, {}, {}, {}
    for idx, (nm, rows, cols) in enumerate((("w_gate", DEPTH * d, FF_SHARD), ("w_up", DEPTH * d, FF_SHARD), ("w_down", DEPTH * FF_SHARD, d))):
        res = _adamw(weights[nm].reshape(rows, cols), mom_m[nm].reshape(rows, cols), mom_v[nm].reshape(rows, cols),
                     [chip_sums[idx], sib_sums[idx]], name="adamw_ffn")
        shp = weights[nm].shape
        out_g[nm], out_d[nm], out_m[nm], out_v[nm] = (r.reshape(shp) for r in res)

    rest = list(mat_names) + small_names
    rest_shapes = [weights[n].shape for n in rest]
    vec_rows = _pack([vec_grad[n].reshape(weights[n].shape) for n in small_names], F32)
    fill = jnp.zeros(((-(chip_sums[3].shape[0] + vec_rows.shape[0])) % ADAMW_ROWS, LANES), F32)
    g_own = jnp.concatenate([chip_sums[3], vec_rows, fill], axis=0)
    g_sib = jnp.concatenate([sib_sums[3], jnp.zeros_like(vec_rows), fill], axis=0)
    res = _adamw(*[jnp.concatenate([_pack([src[n] for n in rest], F32), fill], axis=0) for src in (weights, mom_m, mom_v)],
                 [g_own, g_sib], name="adamw_small")
    for tgt, buf in zip((out_g, out_d, out_m, out_v), res):
        for n, arr in zip(rest, _unpack(buf, rest_shapes)):
            tgt[n] = arr

    return (loss, grad_x, *[out_g[n] for n in order], *[out_d[n] for n in order],
            *[out_m[n] for n in order], *[out_v[n] for n in order])
```

```python
import math
from typing import Any, NamedTuple

import jax
import jax.numpy as jnp
from jax import lax
from jax.experimental import pallas as pl
from jax.experimental.pallas import tpu as pltpu

F32 = jnp.float32
BF16 = jnp.bfloat16
MXU_DTYPE = BF16
WIRE_DTYPE = BF16

D_MODEL = 1024
N_A = 2
N_B = 2
DEPTH = 4
POOL_WINDOWS = (2, 4, 8, 16)
N_GROUPS = 4
GROUP_DIM = 256
POOL_HALO = 16
N_HEADS = 8
NOPE = 128
ROPE = 64
QK_DIM = 192
HEAD_PAD = 256
V_DIM = 128
Q_LORA = 256
KV_LORA = 512
CKV_PAD = 640
ROPE_THETA = 10000.0
CHUNK = 64
EPS = 1e-6
N_SHARD = 4
FF_SHARD = 704
LANES = 128
SUBLANES = 8
ADAM_LR, ADAM_B1, ADAM_B2, ADAM_EPS, ADAM_WD, ADAM_STEP = 0.001, 0.9, 0.999, 1e-08, 0.01, 10
MESH = pl.DeviceIdType.MESH
ANY = pl.BlockSpec(memory_space=pl.ANY)
VMEM_SPEC = pl.BlockSpec(memory_space=pltpu.VMEM)


def _tile(n, pref):
    if n <= pref:
        return n
    t = pref - pref % SUBLANES
    while n % t:
        t -= SUBLANES
    return t


def _fold8(v):
    r, n = v.shape
    return v.reshape(r // SUBLANES, SUBLANES, n).sum(axis=0)


def _dot(a, b, dims):
    return lax.dot_general(a.astype(MXU_DTYPE), b.astype(MXU_DTYPE), (dims, ((), ())),
                           preferred_element_type=F32)


def _nn(a, b):
    return _dot(a, b, ((1,), (0,)))


def _nt(a, b):
    return _dot(a, b, ((1,), (1,)))


def _tn(a, b):
    return _dot(a, b, ((0,), (0,)))


def _mm(a, b, *, ta=False, tb=False, resid=None, out_dtype=F32, name):
    assert not (ta and tb)
    m, k = (a.shape[1], a.shape[0]) if ta else a.shape
    n = b.shape[0] if tb else b.shape[1]
    tm, tn = _tile(m, 512), _tile(n, 1024)

    def body(*refs):
        if resid is None:
            a_ref, b_ref, o_ref = refs
        else:
            a_ref, b_ref, r_ref, o_ref = refs
        acc = (_tn if ta else _nt if tb else _nn)(a_ref[...], b_ref[...])
        if resid is not None:
            acc = r_ref[...] + acc
        o_ref[...] = acc.astype(o_ref.dtype)

    in_specs = [pl.BlockSpec((k, tm), lambda i, j: (0, i)) if ta else pl.BlockSpec((tm, k), lambda i, j: (i, 0)),
                pl.BlockSpec((tn, k), lambda i, j: (j, 0)) if tb else pl.BlockSpec((k, tn), lambda i, j: (0, j))]
    args = [a, b]
    if resid is not None:
        in_specs.append(pl.BlockSpec((tm, tn), lambda i, j: (i, j)))
        args.append(resid)
    return pl.pallas_call(
        body, name=name, grid=(m // tm, n // tn), in_specs=in_specs,
        out_specs=pl.BlockSpec((tm, tn), lambda i, j: (i, j)),
        out_shape=jax.ShapeDtypeStruct((m, n), out_dtype))(*args)


def _mm_tn(a, b, *, name, at=False, out_dtype=F32):
    m = b.shape[0]
    k1 = a.shape[0] if at else a.shape[1]
    n = b.shape[1]
    tm, tn = _tile(m, 512), _tile(n, 1024)
    nm = m // tm

    def body(a_ref, b_ref, o_ref, acc):
        i = pl.program_id(1)

        @pl.when(i == 0)
        def _():
            acc[...] = jnp.zeros_like(acc)

        acc[...] += (_nn if at else _tn)(a_ref[...], b_ref[...])

        @pl.when(i == nm - 1)
        def _():
            o_ref[...] = acc[...].astype(o_ref.dtype)

    return pl.pallas_call(
        body, name=name, grid=(n // tn, nm),
        in_specs=[pl.BlockSpec((k1, tm), lambda j, i: (0, i)) if at else pl.BlockSpec((tm, k1), lambda j, i: (i, 0)),
                  pl.BlockSpec((tm, tn), lambda j, i: (i, j))],
        out_specs=pl.BlockSpec((k1, tn), lambda j, i: (0, j)),
        out_shape=jax.ShapeDtypeStruct((k1, n), out_dtype),
        scratch_shapes=[pltpu.VMEM((k1, tn), F32)])(a, b)


def _rms_fwd(x, g, *, n, n_valid=None, name):
    out_dtype = MXU_DTYPE
    rows = x.shape[0]
    tm = _tile(rows, 512)
    inv_n = 1.0 / (n_valid or n)

    def body(x_ref, g_ref, o_ref):
        xv = x_ref[...]
        r = lax.rsqrt(jnp.sum(xv * xv, axis=-1, keepdims=True) * inv_n + EPS)
        o_ref[...] = (xv * r * g_ref[...]).astype(o_ref.dtype)

    return pl.pallas_call(
        body, name=name, grid=(rows // tm,),
        in_specs=[pl.BlockSpec((tm, n), lambda i: (i, 0)), pl.BlockSpec((1, n), lambda i: (0, 0))],
        out_specs=pl.BlockSpec((tm, n), lambda i: (i, 0)),
        out_shape=jax.ShapeDtypeStruct((rows, n), out_dtype))(x, g)


def _rms_bwd_math(xv, gv, dyv, inv_n):
    r = lax.rsqrt(jnp.sum(xv * xv, axis=-1, keepdims=True) * inv_n + EPS)
    xh = xv * r
    gy = dyv * gv
    dx = r * (gy - xh * (jnp.sum(gy * xh, axis=-1, keepdims=True) * inv_n))
    return dx, dyv * xh


def _rms_bwd(x, g, dy, *, n, dx_in=None, after=None, name):
    rows = x.shape[0]
    tm = _tile(rows, 512)
    inv_n = 1.0 / n

    def body(*refs):
        if after is not None:
            refs = refs[:-3] + refs[-2:]
        if dx_in is None:
            x_ref, g_ref, dy_ref, dx_ref, dg_ref = refs
        else:
            x_ref, g_ref, dy_ref, din_ref, dx_ref, dg_ref = refs
        dx, dgc = _rms_bwd_math(x_ref[...], g_ref[...], dy_ref[...], inv_n)
        if dx_in is not None:
            dx = din_ref[...] + dx
        dx_ref[...] = dx

        @pl.when(pl.program_id(0) == 0)
        def _():
            dg_ref[...] = jnp.zeros_like(dg_ref)

        dg_ref[...] += _fold8(dgc)

    row_spec = pl.BlockSpec((tm, n), lambda i: (i, 0))
    in_specs = [row_spec, pl.BlockSpec((1, n), lambda i: (0, 0)), row_spec]
    args = [x, g, dy]
    if dx_in is not None:
        in_specs.append(row_spec)
        args.append(dx_in)
    if after is not None:
        in_specs.append(ANY)
        args.append(after)
    return pl.pallas_call(
        body, name=name, grid=(rows // tm,), in_specs=in_specs,
        out_specs=[row_spec, pl.BlockSpec((SUBLANES, n), lambda i: (0, 0))],
        out_shape=[jax.ShapeDtypeStruct((rows, n), F32), jax.ShapeDtypeStruct((SUBLANES, n), F32)])(*args)


def _pool_counts(t0, tm, w):
    t = t0 + lax.broadcasted_iota(jnp.int32, (tm, 1), 0)
    return jnp.minimum(t + 1, w).astype(F32)


def _rms_pool_fwd(x, g, *, name):
    s, d = x.shape
    tm = _tile(s, 512)
    hb = tm // POOL_HALO

    def body(x_ref, halo_ref, g_ref, o_ref):
        i = pl.program_id(0)
        gv = g_ref[...]

        def norm(v):
            return v * lax.rsqrt(jnp.mean(v * v, axis=-1, keepdims=True) + EPS) * gv

        h = norm(x_ref[...])
        halo = norm(halo_ref[...]) * (i > 0).astype(F32)
        hh = jnp.concatenate([halo, h], axis=0)
        rows = tm + POOL_HALO
        for gi, w in enumerate(POOL_WINDOWS):
            cols = slice(gi * GROUP_DIM, (gi + 1) * GROUP_DIM)
            acc = hh[:, cols]
            k = 1
            while k < w:
                acc = acc + pltpu.roll(acc, k, 0)
                k *= 2
            win = acc[POOL_HALO:rows]
            o_ref[:, cols] = (win / _pool_counts(i * tm, tm, w) - h[:, cols]).astype(o_ref.dtype)

    return pl.pallas_call(
        body, name=name, grid=(s // tm,),
        in_specs=[pl.BlockSpec((tm, d), lambda i: (i, 0)),
                  pl.BlockSpec((POOL_HALO, d), lambda i: (jnp.maximum(i * hb - 1, 0), 0)),
                  pl.BlockSpec((1, d), lambda i: (0, 0))],
        out_specs=pl.BlockSpec((tm, d), lambda i: (i, 0)),
        out_shape=jax.ShapeDtypeStruct((s, d), MXU_DTYPE))(x, x, g)


def _rms_pool_bwd(x, g, dd, dx_in, *, name):
    s, d = x.shape
    tm = _tile(s, 512)
    hb = tm // POOL_HALO
    nt = s // tm

    def body(x_ref, g_ref, dd_ref, halo_ref, din_ref, dx_ref, dg_ref):
        i = pl.program_id(0)
        ddv = dd_ref[...]
        halo = halo_ref[...] * (i < nt - 1).astype(F32)
        rows = tm + POOL_HALO
        parts = []
        for gi, w in enumerate(POOL_WINDOWS):
            cols = slice(gi * GROUP_DIM, (gi + 1) * GROUP_DIM)
            acc = jnp.concatenate([ddv[:, cols] / _pool_counts(i * tm, tm, w), halo[:, cols] * (1.0 / w)], axis=0)
            k = 1
            while k < w:
                acc = acc + pltpu.roll(acc, rows - k, 0)
                k *= 2
            parts.append(acc[0:tm] - ddv[:, cols])
        dh = jnp.concatenate(parts, axis=1)
        dx, dgc = _rms_bwd_math(x_ref[...], g_ref[...], dh, 1.0 / d)
        dx_ref[...] = din_ref[...] + dx

        @pl.when(i == 0)
        def _():
            dg_ref[...] = jnp.zeros_like(dg_ref)

        dg_ref[...] += _fold8(dgc)

    row_spec = pl.BlockSpec((tm, d), lambda i: (i, 0))
    return pl.pallas_call(
        body, name=name, grid=(nt,),
        in_specs=[row_spec, pl.BlockSpec((1, d), lambda i: (0, 0)), row_spec,
                  pl.BlockSpec((POOL_HALO, d), lambda i: (jnp.minimum((i + 1) * hb, s // POOL_HALO - 1), 0)),
                  row_spec],
        out_specs=[row_spec, pl.BlockSpec((SUBLANES, d), lambda i: (0, 0))],
        out_shape=[jax.ShapeDtypeStruct((s, d), F32), jax.ShapeDtypeStruct((SUBLANES, d), F32)])(x, g, dd, dd, dx_in)


def _pool_mm_fwd(dpool, w, b, scale, x, *, name):
    s, d = x.shape
    tm = _tile(s, 512)

    def body(d_ref, w_ref, b_ref, s_ref, x_ref, o_ref):
        for gi in range(N_GROUPS):
            cols = slice(gi * GROUP_DIM, (gi + 1) * GROUP_DIM)
            y = _nn(d_ref[:, cols], w_ref[gi]) + b_ref[:, cols]
            o_ref[:, cols] = x_ref[:, cols] + y * s_ref[:, cols]

    row_spec = pl.BlockSpec((tm, d), lambda i: (i, 0))
    vec_spec = pl.BlockSpec((1, d), lambda i: (0, 0))
    return pl.pallas_call(
        body, name=name, grid=(s // tm,),
        in_specs=[row_spec, pl.BlockSpec((N_GROUPS, GROUP_DIM, GROUP_DIM), lambda i: (0, 0, 0)), vec_spec, vec_spec, row_spec],
        out_specs=row_spec, out_shape=jax.ShapeDtypeStruct((s, d), F32))(dpool, w, b, scale, x)


def _pool_mm_bwd(dpool, w, b, scale, dx, *, name):
    s, d = dx.shape
    tm = _tile(s, 512)

    def body(d_ref, w_ref, b_ref, s_ref, dx_ref, dd_ref, dw_ref, db_ref, ds_ref):
        @pl.when(pl.program_id(0) == 0)
        def _():
            dw_ref[...] = jnp.zeros_like(dw_ref)
            db_ref[...] = jnp.zeros_like(db_ref)
            ds_ref[...] = jnp.zeros_like(ds_ref)

        for gi in range(N_GROUPS):
            cols = slice(gi * GROUP_DIM, (gi + 1) * GROUP_DIM)
            dg = d_ref[:, cols]
            y = _nn(dg, w_ref[gi]) + b_ref[:, cols]
            dxg = dx_ref[:, cols]
            dy = dxg * s_ref[:, cols]
            ds_ref[:, cols] += _fold8(dxg * y)
            db_ref[:, cols] += _fold8(dy)
            dw_ref[gi] += _tn(dg, dy)
            dd_ref[:, cols] = _nt(dy, w_ref[gi])

    row_spec = pl.BlockSpec((tm, d), lambda i: (i, 0))
    vec_spec = pl.BlockSpec((1, d), lambda i: (0, 0))
    w_spec = pl.BlockSpec((N_GROUPS, GROUP_DIM, GROUP_DIM), lambda i: (0, 0, 0))
    part_spec = pl.BlockSpec((SUBLANES, d), lambda i: (0, 0))
    return pl.pallas_call(
        body, name=name, grid=(s // tm,),
        in_specs=[row_spec, w_spec, vec_spec, vec_spec, row_spec],
        out_specs=[row_spec, w_spec, part_spec, part_spec],
        out_shape=[jax.ShapeDtypeStruct((s, d), F32), jax.ShapeDtypeStruct((N_GROUPS, GROUP_DIM, GROUP_DIM), F32),
                   jax.ShapeDtypeStruct((SUBLANES, d), F32), jax.ShapeDtypeStruct((SUBLANES, d), F32)])(dpool, w, b, scale, dx)


def _ffn_up(hf, wg, wu, *, name):
    s, d = hf.shape
    tm = _tile(s, 512)

    def body(h_ref, wg_ref, wu_ref, a_ref, b_ref, u_ref):
        hv = h_ref[...]
        a = _nn(hv, wg_ref[...])
        b = _nn(hv, wu_ref[...])
        a_ref[...] = a
        b_ref[...] = b
        u_ref[...] = (a * (1.0 / (1.0 + jnp.exp(-a))) * b).astype(u_ref.dtype)

    w_spec = pl.BlockSpec((None, d, FF_SHARD), lambda j, i: (j, 0, 0))
    h_spec = pl.BlockSpec((None, tm, FF_SHARD), lambda j, i: (j, i, 0))
    hid = (N_SHARD, s, FF_SHARD)
    return pl.pallas_call(
        body, name=name, grid=(N_SHARD, s // tm),
        in_specs=[pl.BlockSpec((tm, d), lambda j, i: (i, 0)), w_spec, w_spec],
        out_specs=[h_spec, h_spec, h_spec],
        out_shape=[jax.ShapeDtypeStruct(hid, F32), jax.ShapeDtypeStruct(hid, F32), jax.ShapeDtypeStruct(hid, MXU_DTYPE)])(hf, wg, wu)


def _ffn_down(u, wd, x, *, name):
    s, d = x.shape
    tm = _tile(s, 1024)

    def body(u_ref, w_ref, x_ref, o_ref):
        j = pl.program_id(1)

        @pl.when(j == 0)
        def _():
            o_ref[...] = x_ref[...]

        o_ref[...] += _nn(u_ref[...], w_ref[...])

    return pl.pallas_call(
        body, name=name, grid=(s // tm, N_SHARD),
        in_specs=[pl.BlockSpec((None, tm, FF_SHARD), lambda i, j: (j, i, 0)),
                  pl.BlockSpec((None, FF_SHARD, d), lambda i, j: (j, 0, 0)),
                  pl.BlockSpec((tm, d), lambda i, j: (i, 0))],
        out_specs=pl.BlockSpec((tm, d), lambda i, j: (i, 0)),
        out_shape=jax.ShapeDtypeStruct((s, d), F32))(u, wd, x)


def _ffn_bwd_hidden(dy, wd, a, b, *, name):
    s, d = dy.shape
    tm = _tile(s, 512)

    def body(dy_ref, w_ref, a_ref, b_ref, da_ref, db_ref):
        du = _nt(dy_ref[...], w_ref[...])
        av, bv = a_ref[...], b_ref[...]
        sg = 1.0 / (1.0 + jnp.exp(-av))
        da_ref[...] = (du * bv * (sg * (1.0 + av * (1.0 - sg)))).astype(da_ref.dtype)
        db_ref[...] = (du * (av * sg)).astype(db_ref.dtype)

    h_spec = pl.BlockSpec((None, tm, FF_SHARD), lambda j, i: (j, i, 0))
    hid = jax.ShapeDtypeStruct((N_SHARD, s, FF_SHARD), MXU_DTYPE)
    return pl.pallas_call(
        body, name=name, grid=(N_SHARD, s // tm),
        in_specs=[pl.BlockSpec((tm, d), lambda j, i: (i, 0)),
                  pl.BlockSpec((None, FF_SHARD, d), lambda j, i: (j, 0, 0)), h_spec, h_spec],
        out_specs=[h_spec, h_spec], out_shape=[hid, hid])(dy, wd, a, b)


def _ffn_bwd_dwd(u, dy, *, name):
    s, d = dy.shape
    tm = _tile(s, 512)
    nm = s // tm

    def body(u_ref, dy_ref, o_ref, acc):
        i = pl.program_id(1)

        @pl.when(i == 0)
        def _():
            acc[...] = jnp.zeros_like(acc)

        acc[...] += _tn(u_ref[...], dy_ref[...])

        @pl.when(i == nm - 1)
        def _():
            o_ref[...] = acc[...].astype(o_ref.dtype)

    return pl.pallas_call(
        body, name=name, grid=(N_SHARD, nm),
        in_specs=[pl.BlockSpec((None, tm, FF_SHARD), lambda j, i: (j, i, 0)), pl.BlockSpec((tm, d), lambda j, i: (i, 0))],
        out_specs=pl.BlockSpec((None, FF_SHARD, d), lambda j, i: (j, 0, 0)),
        out_shape=jax.ShapeDtypeStruct((N_SHARD, FF_SHARD, d), WIRE_DTYPE),
        scratch_shapes=[pltpu.VMEM((FF_SHARD, d), F32)])(u, dy)


def _ffn_bwd_dwgu(hf, da, db, *, name):
    s, d = hf.shape
    tm = _tile(s, 512)
    nm = s // tm

    def body(h_ref, da_ref, db_ref, og_ref, ou_ref, accg, accu):
        i = pl.program_id(1)

        @pl.when(i == 0)
        def _():
            accg[...] = jnp.zeros_like(accg)
            accu[...] = jnp.zeros_like(accu)

        hv = h_ref[...]
        accg[...] += _tn(hv, da_ref[...])
        accu[...] += _tn(hv, db_ref[...])

        @pl.when(i == nm - 1)
        def _():
            og_ref[...] = accg[...].astype(og_ref.dtype)
            ou_ref[...] = accu[...].astype(ou_ref.dtype)

    h_spec = pl.BlockSpec((None, tm, FF_SHARD), lambda j, i: (j, i, 0))
    w_spec = pl.BlockSpec((None, d, FF_SHARD), lambda j, i: (j, 0, 0))
    grad = jax.ShapeDtypeStruct((N_SHARD, d, FF_SHARD), WIRE_DTYPE)
    return pl.pallas_call(
        body, name=name, grid=(N_SHARD, nm),
        in_specs=[pl.BlockSpec((tm, d), lambda j, i: (i, 0)), h_spec, h_spec],
        out_specs=[w_spec, w_spec], out_shape=[grad, grad],
        scratch_shapes=[pltpu.VMEM((d, FF_SHARD), F32), pltpu.VMEM((d, FF_SHARD), F32)])(hf, da, db)


def _ffn_bwd_dh(da, db, wg, wu, *, name):
    s = da.shape[1]
    d = wg.shape[1]
    tm = _tile(s, 1024)

    def body(da_ref, db_ref, wg_ref, wu_ref, o_ref):
        j = pl.program_id(1)

        @pl.when(j == 0)
        def _():
            o_ref[...] = jnp.zeros_like(o_ref)

        o_ref[...] += _nt(da_ref[...], wg_ref[...]) + _nt(db_ref[...], wu_ref[...])

    h_spec = pl.BlockSpec((None, tm, FF_SHARD), lambda i, j: (j, i, 0))
    w_spec = pl.BlockSpec((None, d, FF_SHARD), lambda i, j: (j, 0, 0))
    return pl.pallas_call(
        body, name=name, grid=(s // tm, N_SHARD),
        in_specs=[h_spec, h_spec, w_spec, w_spec],
        out_specs=pl.BlockSpec((tm, d), lambda i, j: (i, 0)),
        out_shape=jax.ShapeDtypeStruct((s, d), F32))(da, db, wg, wu)


def _rope_tables(pos, inv, *, name):
    s = pos.shape[0]
    tm = _tile(s, 512)
    half = ROPE // 2

    def body(p_ref, i_ref, c_ref, s_ref):
        ang = p_ref[...] * i_ref[...]
        lane = lax.broadcasted_iota(jnp.int32, ang.shape, 1)
        live = lane < ROPE
        c_ref[...] = jnp.where(live, jnp.cos(ang), 0.0)
        sn = jnp.sin(ang)
        s_ref[...] = jnp.where(live, jnp.where(lane < half, -sn, sn), 0.0)

    out = jax.ShapeDtypeStruct((s, LANES), F32)
    return pl.pallas_call(
        body, name=name, grid=(s // tm,),
        in_specs=[pl.BlockSpec((tm, 1), lambda i: (i, 0)), pl.BlockSpec((1, LANES), lambda i: (0, 0))],
        out_specs=[pl.BlockSpec((tm, LANES), lambda i: (i, 0))] * 2, out_shape=[out, out])(pos, inv)


def _swap_halves(v):
    half = ROPE // 2
    lane = lax.broadcasted_iota(jnp.int32, v.shape, 1)
    return jnp.where(lane < half, pltpu.roll(v, LANES - half, 1), pltpu.roll(v, half, 1))


def _head_norm_rope_fwd(raw, g, cos, sin, *, name):
    s = raw.shape[0]
    tm = _tile(s, 256)
    width = N_HEADS * HEAD_PAD

    def body(x_ref, g_ref, c_ref, s_ref, o_ref):
        cv, sv = c_ref[...], s_ref[...]
        for h in range(N_HEADS):
            lo = h * HEAD_PAD
            xa = x_ref[:, lo:lo + NOPE]
            xb = x_ref[:, lo + NOPE:lo + HEAD_PAD]
            ms = (jnp.sum(xa * xa, axis=-1, keepdims=True) + jnp.sum(xb * xb, axis=-1, keepdims=True)) * (1.0 / QK_DIM)
            r = lax.rsqrt(ms + EPS)
            o_ref[:, lo:lo + NOPE] = (xa * r * g_ref[:, 0:NOPE]).astype(o_ref.dtype)
            yb = xb * r * g_ref[:, NOPE:HEAD_PAD]
            o_ref[:, lo + NOPE:lo + HEAD_PAD] = (yb * cv + _swap_halves(yb) * sv).astype(o_ref.dtype)

    row_spec = pl.BlockSpec((tm, width), lambda i: (i, 0))
    tab_spec = pl.BlockSpec((tm, LANES), lambda i: (i, 0))
    return pl.pallas_call(
        body, name=name, grid=(s // tm,),
        in_specs=[row_spec, pl.BlockSpec((1, HEAD_PAD), lambda i: (0, 0)), tab_spec, tab_spec],
        out_specs=row_spec, out_shape=jax.ShapeDtypeStruct((s, width), MXU_DTYPE))(raw, g, cos, sin)


def _head_norm_rope_bwd(raw, g, cos, sin, dout, *, name):
    s = raw.shape[0]
    tm = _tile(s, 256)
    width = N_HEADS * HEAD_PAD

    def body(x_ref, g_ref, c_ref, s_ref, do_ref, dx_ref, dg_ref):
        @pl.when(pl.program_id(0) == 0)
        def _():
            dg_ref[...] = jnp.zeros_like(dg_ref)

        cv, sv = c_ref[...], s_ref[...]
        ga, gb = g_ref[:, 0:NOPE], g_ref[:, NOPE:HEAD_PAD]
        for h in range(N_HEADS):
            lo = h * HEAD_PAD
            xa = x_ref[:, lo:lo + NOPE]
            xb = x_ref[:, lo + NOPE:lo + HEAD_PAD]
            dya = do_ref[:, lo:lo + NOPE]
            dob = do_ref[:, lo + NOPE:lo + HEAD_PAD]
            dyb = dob * cv + _swap_halves(dob * sv)
            ms = (jnp.sum(xa * xa, axis=-1, keepdims=True) + jnp.sum(xb * xb, axis=-1, keepdims=True)) * (1.0 / QK_DIM)
            r = lax.rsqrt(ms + EPS)
            xha, xhb = xa * r, xb * r
            gya, gyb = dya * ga, dyb * gb
            dot = (jnp.sum(gya * xha, axis=-1, keepdims=True) + jnp.sum(gyb * xhb, axis=-1, keepdims=True)) * (1.0 / QK_DIM)
            dx_ref[:, lo:lo + NOPE] = r * (gya - xha * dot)
            dx_ref[:, lo + NOPE:lo + HEAD_PAD] = r * (gyb - xhb * dot)
            dg_ref[:, 0:NOPE] += _fold8(dya * xha)
            dg_ref[:, NOPE:HEAD_PAD] += _fold8(dyb * xhb)

    row_spec = pl.BlockSpec((tm, width), lambda i: (i, 0))
    tab_spec = pl.BlockSpec((tm, LANES), lambda i: (i, 0))
    return pl.pallas_call(
        body, name=name, grid=(s // tm,),
        in_specs=[row_spec, pl.BlockSpec((1, HEAD_PAD), lambda i: (0, 0)), tab_spec, tab_spec, row_spec],
        out_specs=[row_spec, pl.BlockSpec((SUBLANES, HEAD_PAD), lambda i: (0, 0))],
        out_shape=[jax.ShapeDtypeStruct((s, width), F32), jax.ShapeDtypeStruct((SUBLANES, HEAD_PAD), F32)])(raw, g, cos, sin, dout)


def _k_assemble(kn, ckv, *, name):
    s = kn.shape[0]
    tm = _tile(s, 512)
    width = N_HEADS * HEAD_PAD

    def body(kn_ref, pe_ref, o_ref):
        pe = pe_ref[...]
        for h in range(N_HEADS):
            o_ref[:, h * HEAD_PAD:h * HEAD_PAD + NOPE] = kn_ref[:, h * NOPE:(h + 1) * NOPE]
            o_ref[:, h * HEAD_PAD + NOPE:(h + 1) * HEAD_PAD] = pe

    return pl.pallas_call(
        body, name=name, grid=(s // tm,),
        in_specs=[pl.BlockSpec((tm, N_HEADS * NOPE), lambda i: (i, 0)),
                  pl.BlockSpec((tm, LANES), lambda i: (i, KV_LORA // LANES))],
        out_specs=pl.BlockSpec((tm, width), lambda i: (i, 0)),
        out_shape=jax.ShapeDtypeStruct((s, width), F32))(kn, ckv)


def _k_disassemble(dk_raw, *, name):
    s = dk_raw.shape[0]
    tm = _tile(s, 512)
    width = N_HEADS * HEAD_PAD

    def body(dk_ref, dkn_ref, dpe_ref):
        pe = dk_ref[:, NOPE:HEAD_PAD]
        for h in range(N_HEADS):
            dkn_ref[:, h * NOPE:(h + 1) * NOPE] = dk_ref[:, h * HEAD_PAD:h * HEAD_PAD + NOPE]
            if h:
                pe = pe + dk_ref[:, h * HEAD_PAD + NOPE:(h + 1) * HEAD_PAD]
        dpe_ref[...] = pe

    return pl.pallas_call(
        body, name=name, grid=(s // tm,),
        in_specs=[pl.BlockSpec((tm, width), lambda i: (i, 0))],
        out_specs=[pl.BlockSpec((tm, N_HEADS * NOPE), lambda i: (i, 0)), pl.BlockSpec((tm, LANES), lambda i: (i, 0))],
        out_shape=[jax.ShapeDtypeStruct((s, N_HEADS * NOPE), F32), jax.ShapeDtypeStruct((s, LANES), F32)])(dk_raw)


ATTN_SCALE = 1.0 / math.sqrt(QK_DIM)
MASKED = -1e30


ATTN_TILE = 512


def _chunk_mask(q0, k0, shape, q_axis):
    qpos = q0 + lax.broadcasted_iota(jnp.int32, shape, q_axis)
    kpos = k0 + lax.broadcasted_iota(jnp.int32, shape, 1 - q_axis)
    return kpos // CHUNK <= qpos // CHUNK


def _attn_fwd(q, k, vt, *, name):
    s = q.shape[0]
    t = _tile(s, ATTN_TILE)
    n = s // t

    def body(q_ref, k_ref, vt_ref, o_ref, lse_ref, m_sc, l_sc, acc):
        qi, kj = pl.program_id(1), pl.program_id(2)

        @pl.when(kj == 0)
        def _():
            m_sc[...] = jnp.full_like(m_sc, MASKED)
            l_sc[...] = jnp.zeros_like(l_sc)
            acc[...] = jnp.zeros_like(acc)

        def step(masked):
            st = _nt(k_ref[...], q_ref[...]) * ATTN_SCALE
            if masked:
                st = jnp.where(_chunk_mask(qi * t, kj * t, (t, t), 1), st, MASKED)
            m_prev = m_sc[...]
            m_new = jnp.maximum(m_prev, jnp.max(st, axis=0, keepdims=True))
            alpha = jnp.exp(m_prev - m_new)
            pt = jnp.exp(st - m_new)
            l_sc[...] = alpha * l_sc[...] + jnp.sum(pt, axis=0, keepdims=True)
            acc[...] = alpha * acc[...] + _nn(vt_ref[...], pt)
            m_sc[...] = m_new

        @pl.when(kj < qi)
        def _():
            step(False)

        @pl.when(kj == qi)
        def _():
            step(True)

        @pl.when(kj == n - 1)
        def _():
            o_ref[...] = acc[...] / l_sc[...]
            lse_ref[...] = m_sc[...] + jnp.log(l_sc[...])

    return pl.pallas_call(
        body, name=name, grid=(N_HEADS, n, n),
        in_specs=[pl.BlockSpec((t, HEAD_PAD), lambda h, i, j: (i, h)),
                  pl.BlockSpec((t, HEAD_PAD), lambda h, i, j: (jnp.minimum(j, i), h)),
                  pl.BlockSpec((V_DIM, t), lambda h, i, j: (h, jnp.minimum(j, i)))],
        out_specs=[pl.BlockSpec((V_DIM, t), lambda h, i, j: (h, i)), pl.BlockSpec((None, 1, t), lambda h, i, j: (h, 0, i))],
        out_shape=[jax.ShapeDtypeStruct((N_HEADS * V_DIM, s), F32), jax.ShapeDtypeStruct((N_HEADS, 1, s), F32)],
        scratch_shapes=[pltpu.VMEM((1, t), F32), pltpu.VMEM((1, t), F32), pltpu.VMEM((V_DIM, t), F32)])(q, k, vt)


def _attn_delta(ot, dot, *, name):
    s = ot.shape[1]
    t = _tile(s, 1024)

    def body(o_ref, do_ref, d_ref):
        d_ref[...] = jnp.sum(o_ref[...] * do_ref[...], axis=0, keepdims=True)

    blk = pl.BlockSpec((V_DIM, t), lambda h, i: (h, i))
    return pl.pallas_call(
        body, name=name, grid=(N_HEADS, s // t), in_specs=[blk, blk],
        out_specs=pl.BlockSpec((None, 1, t), lambda h, i: (h, 0, i)),
        out_shape=jax.ShapeDtypeStruct((N_HEADS, 1, s), F32))(ot, dot)


def _attn_bwd_dq(q, k, v, do, lse_col, delta_col, *, name):
    s = q.shape[0]
    t = _tile(s, ATTN_TILE)
    n = s // t

    def body(q_ref, k_ref, v_ref, do_ref, lse_ref, dl_ref, dq_ref, acc):
        qi, kj = pl.program_id(1), pl.program_id(2)

        @pl.when(kj == 0)
        def _():
            acc[...] = jnp.zeros_like(acc)

        def step(masked):
            kv = k_ref[...]
            sc = _nt(q_ref[...], kv) * ATTN_SCALE
            if masked:
                sc = jnp.where(_chunk_mask(qi * t, kj * t, (t, t), 0), sc, MASKED)
            p = jnp.exp(sc - lse_ref[...])
            dp = _nt(do_ref[...], v_ref[...])
            acc[...] += _nn(p * (dp - dl_ref[...]), kv)

        @pl.when(kj < qi)
        def _():
            step(False)

        @pl.when(kj == qi)
        def _():
            step(True)

        @pl.when(kj == n - 1)
        def _():
            dq_ref[...] = acc[...] * ATTN_SCALE

    def kmap(h, i, j):
        return (jnp.minimum(j, i), h)

    col = pl.BlockSpec((None, t, 1), lambda h, i, j: (h, i, 0))
    return pl.pallas_call(
        body, name=name, grid=(N_HEADS, n, n),
        in_specs=[pl.BlockSpec((t, HEAD_PAD), lambda h, i, j: (i, h)), pl.BlockSpec((t, HEAD_PAD), kmap),
                  pl.BlockSpec((t, V_DIM), kmap), pl.BlockSpec((t, V_DIM), lambda h, i, j: (i, h)), col, col],
        out_specs=pl.BlockSpec((t, HEAD_PAD), lambda h, i, j: (i, h)),
        out_shape=jax.ShapeDtypeStruct((s, N_HEADS * HEAD_PAD), F32),
        scratch_shapes=[pltpu.VMEM((t, HEAD_PAD), F32)])(q, k, v, do, lse_col, delta_col)


def _attn_bwd_dkv(q, k, v, do, lse_row, delta_row, dk_in, dv_in, *, name):
    s = q.shape[0]
    t = _tile(s, ATTN_TILE)
    n = s // t
    has_in = dk_in is not None

    def body(*refs):
        if has_in:
            q_ref, k_ref, v_ref, do_ref, lse_ref, dl_ref, dki_ref, dvi_ref, dk_ref, dv_ref, acck, accv = refs
        else:
            q_ref, k_ref, v_ref, do_ref, lse_ref, dl_ref, dk_ref, dv_ref, acck, accv = refs
        kj, qi = pl.program_id(1), pl.program_id(2)

        @pl.when(qi == 0)
        def _():
            acck[...] = jnp.zeros_like(acck)
            accv[...] = jnp.zeros_like(accv)

        def step(masked):
            qv, dov = q_ref[...], do_ref[...]
            st = _nt(k_ref[...], qv) * ATTN_SCALE
            if masked:
                st = jnp.where(_chunk_mask(qi * t, kj * t, (t, t), 1), st, MASKED)
            pt = jnp.exp(st - lse_ref[...])
            accv[...] += _nn(pt, dov)
            dpt = _nt(v_ref[...], dov)
            acck[...] += _nn(pt * (dpt - dl_ref[...]), qv)

        @pl.when(qi > kj)
        def _():
            step(False)

        @pl.when(qi == kj)
        def _():
            step(True)

        @pl.when(qi == n - 1)
        def _():
            dk = acck[...] * ATTN_SCALE
            dv = accv[...]
            if has_in:
                dk = dki_ref[...] + dk
                dv = dvi_ref[...] + dv
            dk_ref[...] = dk
            dv_ref[...] = dv

    def qmap(h, j, i):
        return (jnp.maximum(i, j), h)

    row = pl.BlockSpec((None, 1, t), lambda h, j, i: (h, 0, jnp.maximum(i, j)))
    k_spec = pl.BlockSpec((t, HEAD_PAD), lambda h, j, i: (j, h))
    v_spec = pl.BlockSpec((t, V_DIM), lambda h, j, i: (j, h))
    in_specs = [pl.BlockSpec((t, HEAD_PAD), qmap), k_spec, v_spec, pl.BlockSpec((t, V_DIM), qmap), row, row]
    args = [q, k, v, do, lse_row, delta_row]
    if has_in:
        in_specs += [k_spec, v_spec]
        args += [dk_in, dv_in]
    return pl.pallas_call(
        body, name=name, grid=(N_HEADS, n, n), in_specs=in_specs, out_specs=[k_spec, v_spec],
        out_shape=[jax.ShapeDtypeStruct((s, N_HEADS * HEAD_PAD), F32), jax.ShapeDtypeStruct((s, N_HEADS * V_DIM), F32)],
        scratch_shapes=[pltpu.VMEM((t, HEAD_PAD), F32), pltpu.VMEM((t, V_DIM), F32)])(*args)


def _loss_head(y, target, *, name):
    s, d = y.shape
    tm = _tile(s, 512)

    def body(y_ref, t_ref, dy_ref, l_ref):
        @pl.when(pl.program_id(0) == 0)
        def _():
            l_ref[...] = jnp.zeros_like(l_ref)

        err = y_ref[...] - t_ref[...]
        dy_ref[...] = err * (1.0 / d)
        sq = _fold8(err * err)
        part = sq[:, 0:LANES]
        for cb in range(1, d // LANES):
            part = part + sq[:, cb * LANES:(cb + 1) * LANES]
        l_ref[...] += part * (0.5 / d)

    row_spec = pl.BlockSpec((tm, d), lambda i: (i, 0))
    return pl.pallas_call(
        body, name=name, grid=(s // tm,), in_specs=[row_spec, row_spec],
        out_specs=[row_spec, pl.BlockSpec((SUBLANES, LANES), lambda i: (0, 0))],
        out_shape=[jax.ShapeDtypeStruct((s, d), F32), jax.ShapeDtypeStruct((SUBLANES, LANES), F32)])(y, target)


ADAMW_ROWS = 512


def _adamw(w, m, v, g_parts, *, name):
    rows, cols = w.shape
    tm = _tile(rows, ADAMW_ROWS)
    n_parts = len(g_parts)
    c1 = 1.0 - ADAM_B1 ** ADAM_STEP
    c2 = 1.0 - ADAM_B2 ** ADAM_STEP

    def body(*refs):
        w_ref, m_ref, v_ref = refs[:3]
        g_refs = refs[3:3 + n_parts]
        g_out, d_out, m_out, v_out = refs[3 + n_parts:]
        g = g_refs[0][...]
        for r in g_refs[1:]:
            g = g + r[...]
        mn = ADAM_B1 * m_ref[...] + (1.0 - ADAM_B1) * g
        vn = ADAM_B2 * v_ref[...] + (1.0 - ADAM_B2) * (g * g)
        g_out[...] = g
        m_out[...] = mn
        v_out[...] = vn
        d_out[...] = -ADAM_LR * ((mn / c1) / (jnp.sqrt(vn / c2) + ADAM_EPS) + ADAM_WD * w_ref[...])

    spec = pl.BlockSpec((tm, cols), lambda i: (i, 0))
    out = jax.ShapeDtypeStruct((rows, cols), F32)
    return pl.pallas_call(
        body, name=name, grid=(rows // tm,), in_specs=[spec] * (3 + n_parts),
        out_specs=[spec] * 4, out_shape=[out] * 4)(w, m, v, *g_parts)


def _sum_slots(parts, *, name):
    _, rows, cols = parts.shape
    tm = _tile(rows, 512)

    def body(p_ref, o_ref):
        acc = p_ref[0].astype(F32)
        for k in range(1, N_SHARD):
            acc = acc + p_ref[k].astype(F32)
        o_ref[...] = acc

    return pl.pallas_call(
        body, name=name, grid=(rows // tm,),
        in_specs=[pl.BlockSpec((N_SHARD, tm, cols), lambda i: (0, i, 0))],
        out_specs=pl.BlockSpec((tm, cols), lambda i: (i, 0)),
        out_shape=jax.ShapeDtypeStruct((rows, cols), F32))(parts)


def _mesh_pos():
    return lax.axis_index("x"), lax.axis_index("y"), lax.axis_index("c")


CHIP_FLIPS = ((1, 0), (0, 1), (1, 1))


class Exchange(NamedTuple):
    kind: str
    srcs: tuple
    lands: tuple
    layer: Any = None


HBM_SPEC = pl.BlockSpec(memory_space=pltpu.HBM)
SEM_SPEC = pl.BlockSpec(memory_space=pltpu.SEMAPHORE)
DATAFLOW = pltpu.SideEffectType.DATAFLOW_SIDE_EFFECTING


def _exchange_copies(ex, src_refs, land_refs, send_sems, recv_sems):
    x, y, c = _mesh_pos()
    mine = 2 * x + y

    def slot(ref, chip):
        return ref.at[chip] if ex.layer is None else ref.at[chip, ex.layer]

    pairs = []
    for a, (src, land) in enumerate(zip(src_refs, land_refs)):
        for k, (fx, fy) in enumerate(CHIP_FLIPS):
            px, py = x ^ fx, y ^ fy
            peer = 2 * px + py
            src_part = src if ex.kind == "gather" else src.at[peer]
            pair = a * len(CHIP_FLIPS) + k
            common = dict(src_ref=src_part, send_sem=send_sems.at[pair], recv_sem=recv_sems.at[pair],
                          device_id=(px, py, c), device_id_type=MESH)
            pairs.append((pltpu.make_async_remote_copy(dst_ref=slot(land, mine), **common),
                          pltpu.make_async_remote_copy(dst_ref=slot(land, peer), **common)))
    return pairs


def _exchange_start(exchanges, *, name):
    srcs = [s for ex in exchanges for s in ex.srcs]
    lands = [b for ex in exchanges for b in ex.lands]
    n_arr, n_ex = len(srcs) + len(lands), len(exchanges)

    def body(*refs):
        src_refs, land_refs = refs[:len(srcs)], refs[len(srcs):n_arr]
        sems, token = refs[n_arr:n_arr + 2 * n_ex], refs[-1]
        at = 0
        for e, ex in enumerate(exchanges):
            n = len(ex.srcs)
            for send, _ in _exchange_copies(ex, src_refs[at:at + n], land_refs[at:at + n], sems[2 * e], sems[2 * e + 1]):
                send.start()
            at += n
        token[...] = jnp.zeros_like(token)

    sem_shapes = [pltpu.SemaphoreType.DMA((len(ex.srcs) * len(CHIP_FLIPS),)) for ex in exchanges for _ in range(2)]
    out = pl.pallas_call(
        body, name=name,
        out_shape=sem_shapes + [pltpu.HBM(a.shape, a.dtype) for a in srcs + lands] + [jax.ShapeDtypeStruct((SUBLANES, LANES), F32)],
        in_specs=[HBM_SPEC] * n_arr, out_specs=[SEM_SPEC] * (2 * n_ex) + [HBM_SPEC] * n_arr + [VMEM_SPEC],
        input_output_aliases={i: 2 * n_ex + i for i in range(n_arr)},
        compiler_params=pltpu.CompilerParams(has_side_effects=DATAFLOW),
    )(*[pltpu.with_memory_space_constraint(a, pltpu.HBM) for a in srcs + lands])
    sems, thru = out[:2 * n_ex], out[2 * n_ex:-1]
    pending, at = [], 0
    for e, ex in enumerate(exchanges):
        n = len(ex.srcs)
        pending.append((ex._replace(srcs=tuple(thru[at:at + n]), lands=tuple(thru[len(srcs) + at:len(srcs) + at + n])),
                        sems[2 * e], sems[2 * e + 1]))
        at += n
    return pending, out[-1]


def _exchange_wait(pending, after, *, name):
    ex, send_sems, recv_sems = pending
    n = len(ex.srcs)

    def body(*refs):
        src_refs, land_refs = refs[:n], refs[n:2 * n]
        for send, arrive in _exchange_copies(ex, src_refs, land_refs, refs[2 * n], refs[2 * n + 1]):
            send.wait_send()
            arrive.wait_recv()

    arrays = list(ex.srcs) + list(ex.lands)
    out = pl.pallas_call(
        body, name=name, out_shape=[pltpu.HBM(a.shape, a.dtype) for a in arrays],
        in_specs=[HBM_SPEC] * (2 * n) + [SEM_SPEC, SEM_SPEC, ANY], out_specs=[HBM_SPEC] * (2 * n),
        input_output_aliases={i: i for i in range(2 * n)},
        compiler_params=pltpu.CompilerParams(has_side_effects=DATAFLOW),
    )(*arrays, send_sems, recv_sems, after)
    return out[n:]


def _swap_with_sibling(arrays, *, name):
    n = len(arrays)

    def body(*refs):
        ins, outs = refs[:n], refs[n:2 * n]
        send_sems, recv_sems = refs[2 * n:]
        x, y, c = _mesh_pos()
        copies = []
        for a in range(n):
            cp = pltpu.make_async_remote_copy(
                src_ref=ins[a], dst_ref=outs[a], send_sem=send_sems.at[a], recv_sem=recv_sems.at[a],
                device_id=(x, y, 1 - c), device_id_type=MESH)
            cp.start()
            copies.append(cp)
        for cp in copies:
            cp.wait()

    return pl.pallas_call(
        body, name=name, in_specs=[ANY] * n, out_specs=[ANY] * n,
        out_shape=[jax.ShapeDtypeStruct(a.shape, a.dtype) for a in arrays],
        scratch_shapes=[pltpu.SemaphoreType.DMA((n,)), pltpu.SemaphoreType.DMA((n,))])(*arrays)


N_DEV = 8


def _all_reduce_small(vec, *, name):
    rows = vec.shape[0]

    def body(v_ref, o_ref, land, send_sems, recv_sems):
        x, y, c = _mesh_pos()
        me = 4 * x + 2 * y + c
        land[me] = v_ref[...]
        copies = []
        for k in range(1, N_DEV):
            fx, fy, fc = (k >> 2) & 1, (k >> 1) & 1, k & 1
            px, py, pc = x ^ fx, y ^ fy, c ^ fc
            send = pltpu.make_async_remote_copy(
                src_ref=v_ref, dst_ref=land.at[me], send_sem=send_sems.at[k - 1], recv_sem=recv_sems.at[k - 1],
                device_id=(px, py, pc), device_id_type=MESH)
            send.start()
            arrive = pltpu.make_async_remote_copy(
                src_ref=v_ref, dst_ref=land.at[4 * px + 2 * py + pc], send_sem=send_sems.at[k - 1], recv_sem=recv_sems.at[k - 1],
                device_id=(px, py, pc), device_id_type=MESH)
            copies.append((send, arrive))
        for send, arrive in copies:
            send.wait_send()
            arrive.wait_recv()
        acc = land[0]
        for k in range(1, N_DEV):
            acc = acc + land[k]
        o_ref[...] = acc

    return pl.pallas_call(
        body, name=name, in_specs=[VMEM_SPEC], out_specs=VMEM_SPEC,
        out_shape=jax.ShapeDtypeStruct(vec.shape, F32),
        scratch_shapes=[pltpu.VMEM((N_DEV, rows, LANES), F32), pltpu.SemaphoreType.DMA((N_DEV - 1,)),
                        pltpu.SemaphoreType.DMA((N_DEV - 1,))])(vec)


PACK_UNIT = SUBLANES * LANES * 2


def _padded(n):
    return -(-n // PACK_UNIT) * PACK_UNIT


def _pack(arrays, dtype, lead=0):
    parts = []
    for a in arrays:
        lead_shape = a.shape[:lead]
        flat = a.astype(dtype).reshape(lead_shape + (-1,))
        n = flat.shape[-1]
        flat = jnp.pad(flat, [(0, 0)] * lead + [(0, _padded(n) - n)])
        parts.append(flat.reshape(lead_shape + (-1, LANES)))
    return jnp.concatenate(parts, axis=lead)


def _unpack(buf, shapes, lead=0):
    out, row = [], 0
    for shp in shapes:
        n = math.prod(shp)
        rows = _padded(n) // LANES
        part = lax.slice_in_dim(buf, row, row + rows, axis=lead)
        lead_shape = part.shape[:lead]
        part = part.reshape(lead_shape + (-1,))
        part = lax.slice_in_dim(part, 0, n, axis=lead)
        out.append(part.reshape(lead_shape + tuple(shp)))
        row += rows
    return out


def kernel(x, positions, ln_mix_a, w_pool, b_pool, pool_scale, ln_ffn, w_gate, w_up, w_down, ln_kv, w_dkv, g_kv_latent, w_uk, w_uv, g_k, ln_mix_b, w_dq, g_q_latent, w_uq, g_q, w_o, loss_target, m_ln_mix_a, m_w_pool, m_b_pool, m_pool_scale, m_ln_ffn, m_w_gate, m_w_up, m_w_down, m_ln_kv, m_w_dkv, m_g_kv_latent, m_w_uk, m_w_uv, m_g_k, m_ln_mix_b, m_w_dq, m_g_q_latent, m_w_uq, m_g_q, m_w_o, v_ln_mix_a, v_w_pool, v_b_pool, v_pool_scale, v_ln_ffn, v_w_gate, v_w_up, v_w_down, v_ln_kv, v_w_dkv, v_g_kv_latent, v_w_uk, v_w_uv, v_g_k, v_ln_mix_b, v_w_dq, v_g_q_latent, v_w_uq, v_g_q, v_w_o):
    weights = dict(ln_mix_a=ln_mix_a, w_pool=w_pool, b_pool=b_pool, pool_scale=pool_scale, ln_ffn=ln_ffn, w_gate=w_gate,
                   w_up=w_up, w_down=w_down, ln_kv=ln_kv, w_dkv=w_dkv, g_kv_latent=g_kv_latent, w_uk=w_uk, w_uv=w_uv, g_k=g_k,
                   ln_mix_b=ln_mix_b, w_dq=w_dq, g_q_latent=g_q_latent, w_uq=w_uq, g_q=g_q, w_o=w_o)
    mom_m = dict(ln_mix_a=m_ln_mix_a, w_pool=m_w_pool, b_pool=m_b_pool, pool_scale=m_pool_scale, ln_ffn=m_ln_ffn,
                 w_gate=m_w_gate, w_up=m_w_up, w_down=m_w_down, ln_kv=m_ln_kv, w_dkv=m_w_dkv, g_kv_latent=m_g_kv_latent,
                 w_uk=m_w_uk, w_uv=m_w_uv, g_k=m_g_k, ln_mix_b=m_ln_mix_b, w_dq=m_w_dq, g_q_latent=m_g_q_latent,
                 w_uq=m_w_uq, g_q=m_g_q, w_o=m_w_o)
    mom_v = dict(ln_mix_a=v_ln_mix_a, w_pool=v_w_pool, b_pool=v_b_pool, pool_scale=v_pool_scale, ln_ffn=v_ln_ffn,
                 w_gate=v_w_gate, w_up=v_w_up, w_down=v_w_down, ln_kv=v_ln_kv, w_dkv=v_w_dkv, g_kv_latent=v_g_kv_latent,
                 w_uk=v_w_uk, w_uv=v_w_uv, g_k=v_g_k, ln_mix_b=v_ln_mix_b, w_dq=v_w_dq, g_q_latent=v_g_q_latent,
                 w_uq=v_w_uq, g_q=v_g_q, w_o=v_w_o)
    order = list(weights)
    s = x.shape[1]
    d = D_MODEL
    xs = x.reshape(s, d)
    target = loss_target.reshape(s, d)
    my_chip = 2 * lax.axis_index("x") + lax.axis_index("y")

    mat_names = ("w_pool", "w_dkv", "w_uk", "w_uv", "w_dq", "w_uq", "w_o")
    vec_names = ("ln_mix_a", "b_pool", "pool_scale")
    mat_shapes = [weights[n].shape for n in mat_names]
    vec_shapes = [weights[n].shape for n in vec_names]
    mats_local = _pack([weights[n] for n in mat_names], WIRE_DTYPE)
    vecs_local = _pack([weights[n] for n in vec_names], F32)

    def landing(shard):
        return lax.dynamic_update_slice_in_dim(lax.empty((N_SHARD,) + shard.shape, shard.dtype), shard[None], my_chip, axis=0)

    gathers = [Exchange("gather", (mats_local, vecs_local), (landing(mats_local), landing(vecs_local)))]
    for l in range(DEPTH):
        shards = tuple(w[l].astype(WIRE_DTYPE) for w in (w_gate, w_up, w_down))
        gathers.append(Exchange("gather", shards, tuple(landing(sh) for sh in shards)))
    gathering, _ = _exchange_start(gathers, name="gather_start")

    inv = ROPE_THETA ** (-jnp.arange(ROPE // 2, dtype=F32) * 2.0 / ROPE)
    inv_lanes = jnp.concatenate([inv, inv, jnp.zeros((LANES - ROPE,), F32)]).reshape(1, LANES)
    cos_t, sin_t = _rope_tables(positions.reshape(s, 1).astype(F32), inv_lanes, name="rope_tables")

    mats_all, vecs_all = _exchange_wait(gathering[0], cos_t, name="gather_wait_small")
    g_pool, g_dkv, g_uk, g_uv, g_dq, g_uq, g_o = _unpack(mats_all, mat_shapes, lead=1)
    g_lna, g_bp, g_ps = _unpack(vecs_all, vec_shapes, lead=1)

    wpool_f = g_pool.transpose(1, 2, 0, 3, 4).reshape(N_A, N_GROUPS, GROUP_DIM, GROUP_DIM)
    bpool_f = g_bp.transpose(1, 2, 0, 3).reshape(N_A, 1, d)
    pscale_f = g_ps.transpose(1, 0, 2).reshape(N_A, 1, d)
    lna_f = g_lna.transpose(1, 0, 2).reshape(N_A, 1, d)
    wdkv_f = jnp.pad(g_dkv.reshape(d, KV_LORA + ROPE), ((0, 0), (0, CKV_PAD - KV_LORA - ROPE)))
    wuk_f = g_uk.transpose(1, 0, 2).reshape(KV_LORA, N_HEADS * NOPE)
    wuv_f = g_uv.transpose(1, 0, 2).reshape(KV_LORA, N_HEADS * V_DIM)
    wdq_f = g_dq.transpose(1, 0, 2, 3).reshape(N_B, d, Q_LORA)
    wuq_f = jnp.pad(g_uq.transpose(1, 2, 0, 3).reshape(N_B, Q_LORA, N_HEADS, QK_DIM),
                    ((0, 0), (0, 0), (0, 0), (0, HEAD_PAD - QK_DIM))).reshape(N_B, Q_LORA, N_HEADS * HEAD_PAD)
    wo_f = g_o.transpose(1, 0, 2, 3).reshape(N_B, d, d)

    def head_gain(g):
        return jnp.pad(g.reshape(1, QK_DIM), ((0, 0), (0, HEAD_PAD - QK_DIM)))

    ffn_w = [None] * DEPTH

    def ffn_fwd(xin, layer):
        hf = _rms_fwd(xin, ln_ffn[layer].reshape(1, d), n=d, name="ffn_norm")
        ffn_w[layer] = wg, wu, wd = _exchange_wait(gathering[1 + layer], hf, name=f"gather_wait_{layer}")
        a, b, u = _ffn_up(hf, wg, wu, name="ffn_up")
        return _ffn_down(u, wd, xin, name="ffn_down"), (xin, hf, a, b, u)

    saved_a, saved_b, saved_f = [], [], []
    cur = xs
    for l in range(N_A):
        dpool = _rms_pool_fwd(cur, lna_f[l], name="pool_fwd")
        x1 = _pool_mm_fwd(dpool, wpool_f[l], bpool_f[l], pscale_f[l], cur, name="pool_mm")
        saved_a.append((cur, dpool))
        cur, sf = ffn_fwd(x1, l)
        saved_f.append(sf)

    x_kv = cur
    hk = _rms_fwd(x_kv, ln_kv.reshape(1, d), n=d, name="kv_norm")
    ckv = _mm(hk, wdkv_f, name="kv_down")
    c_lat = _rms_fwd(ckv, g_kv_latent.reshape(1, KV_LORA), n=KV_LORA, name="kv_latent_norm")
    kn_raw = _mm(c_lat, wuk_f, name="k_up")
    v_all = _mm(c_lat, wuv_f, out_dtype=MXU_DTYPE, name="v_up")
    vt_all = _mm(wuv_f.T, c_lat, tb=True, out_dtype=MXU_DTYPE, name="v_up_t")
    k_raw = _k_assemble(kn_raw, ckv, name="k_assemble")
    gk_pad = head_gain(g_k)
    k_cat = _head_norm_rope_fwd(k_raw, gk_pad, cos_t, sin_t, name="k_norm_rope")

    for j in range(N_B):
        l = N_A + j
        hq = _rms_fwd(cur, ln_mix_b[j].reshape(1, d), n=d, name="q_norm")
        cq_raw = _mm(hq, wdq_f[j], name="q_down")
        cq = _rms_fwd(cq_raw, g_q_latent[j].reshape(1, Q_LORA), n=Q_LORA, name="q_latent_norm")
        q_raw = _mm(cq, wuq_f[j], name="q_up")
        gq_pad = head_gain(g_q[j])
        q_cat = _head_norm_rope_fwd(q_raw, gq_pad, cos_t, sin_t, name="q_norm_rope")
        ot, lse = _attn_fwd(q_cat, k_cat, vt_all, name="attn_fwd")
        x1 = _mm(ot, wo_f[j], ta=True, resid=cur, name="attn_out")
        saved_b.append((cur, hq, cq_raw, cq, q_raw, gq_pad, q_cat, ot, lse))
        cur, sf = ffn_fwd(x1, l)
        saved_f.append(sf)

    dy, loss_part = _loss_head(cur, target, name="loss_head")

    ffn_landed = (lax.empty((N_SHARD, DEPTH, d, FF_SHARD), WIRE_DTYPE), lax.empty((N_SHARD, DEPTH, d, FF_SHARD), WIRE_DTYPE),
                  lax.empty((N_SHARD, DEPTH, FF_SHARD, d), WIRE_DTYPE))
    scattering = None
    grads = {}
    d_ln_ffn = [None] * DEPTH

    def own_part(full):
        return lax.dynamic_index_in_dim(full, my_chip, axis=0, keepdims=True)

    def ffn_bwd(dyv, layer):
        nonlocal ffn_landed, scattering
        xin, hf, a, b, u = saved_f[layer]
        wg, wu, wd = ffn_w[layer]
        da, db = _ffn_bwd_hidden(dyv, wd, a, b, name="ffn_bwd_hidden")
        dwd = _ffn_bwd_dwd(u, dyv, name="ffn_bwd_dwd")
        dwg, dwu = _ffn_bwd_dwgu(hf, da, db, name="ffn_bwd_dwgu")
        if scattering is not None:
            ffn_landed = _exchange_wait(scattering, dwg, name=f"scatter_wait_{layer + 1}")
        ffn_landed = tuple(lax.dynamic_update_slice(buf, own_part(g)[:, None], (my_chip, layer, 0, 0))
                           for buf, g in zip(ffn_landed, (dwg, dwu, dwd)))
        (scattering,), started = _exchange_start([Exchange("scatter", (dwg, dwu, dwd), ffn_landed, layer)],
                                                 name=f"scatter_start_{layer}")
        dhf = _ffn_bwd_dh(da, db, wg, wu, name="ffn_bwd_dh")
        dx, dg = _rms_bwd(xin, ln_ffn[layer].reshape(1, d), dhf, n=d, dx_in=dyv, after=started, name="ffn_norm_bwd")
        d_ln_ffn[layer] = dg.sum(axis=0)
        return dx

    dk_acc = dv_acc = None
    d_ln_mix_b, d_w_dq, d_g_q_latent, d_w_uq, d_g_q, d_w_o = ([None] * N_B for _ in range(6))
    dcur = dy
    for j in reversed(range(N_B)):
        l = N_A + j
        xin, hq, cq_raw, cq, q_raw, gq_pad, q_cat, ot, lse = saved_b[j]
        dx1 = ffn_bwd(dcur, l)
        do = _mm(dx1, wo_f[j], tb=True, out_dtype=MXU_DTYPE, name="attn_out_bwd")
        dot = _mm(wo_f[j], dx1, tb=True, name="attn_out_bwd_t")
        d_w_o[j] = _mm_tn(ot, dx1, at=True, name="attn_out_dw")
        delta = _attn_delta(ot, dot, name="attn_delta")
        lse_col, delta_col = lse.reshape(N_HEADS, s, 1), delta.reshape(N_HEADS, s, 1)
        dq_cat = _attn_bwd_dq(q_cat, k_cat, v_all, do, lse_col, delta_col, name="attn_bwd_dq")
        dk_acc, dv_acc = _attn_bwd_dkv(q_cat, k_cat, v_all, do, lse, delta, dk_acc, dv_acc, name="attn_bwd_dkv")
        dq_raw, dgq = _head_norm_rope_bwd(q_raw, gq_pad, cos_t, sin_t, dq_cat, name="q_norm_rope_bwd")
        d_g_q[j] = dgq.sum(axis=0)[:QK_DIM]
        dcq = _mm(dq_raw, wuq_f[j], tb=True, name="q_up_bwd")
        d_w_uq[j] = _mm_tn(cq, dq_raw, name="q_up_dw").reshape(Q_LORA, N_HEADS, HEAD_PAD)[:, :, :QK_DIM].reshape(Q_LORA, N_HEADS * QK_DIM)
        dcq_raw, dgl = _rms_bwd(cq_raw, g_q_latent[j].reshape(1, Q_LORA), dcq, n=Q_LORA, name="q_latent_norm_bwd")
        d_g_q_latent[j] = dgl.sum(axis=0)
        dhq = _mm(dcq_raw, wdq_f[j], tb=True, name="q_down_bwd")
        d_w_dq[j] = _mm_tn(hq, dcq_raw, name="q_down_dw")
        dcur, dgm = _rms_bwd(xin, ln_mix_b[j].reshape(1, d), dhq, n=d, dx_in=dx1, name="q_norm_bwd")
        d_ln_mix_b[j] = dgm.sum(axis=0)

    dk_raw, dgk = _head_norm_rope_bwd(k_raw, gk_pad, cos_t, sin_t, dk_acc, name="k_norm_rope_bwd")
    grads["g_k"] = dgk.sum(axis=0)[:QK_DIM]
    dc = _mm(dv_acc, wuv_f, tb=True, name="v_up_bwd")
    grads["w_uv"] = _mm_tn(c_lat, dv_acc, name="v_up_dw")
    dkn, dpe = _k_disassemble(dk_raw, name="k_disassemble")
    dc = _mm(dkn, wuk_f, tb=True, resid=dc, name="k_up_bwd")
    grads["w_uk"] = _mm_tn(c_lat, dkn, name="k_up_dw")
    dc_raw, dgl = _rms_bwd(ckv, g_kv_latent.reshape(1, KV_LORA), dc, n=KV_LORA, name="kv_latent_norm_bwd")
    grads["g_kv_latent"] = dgl.sum(axis=0)
    dckv = jnp.concatenate([dc_raw, dpe], axis=1)
    dhk = _mm(dckv, wdkv_f, tb=True, name="kv_down_bwd")
    grads["w_dkv"] = _mm_tn(hk, dckv, name="kv_down_dw")[:, :KV_LORA + ROPE]
    dcur, dg = _rms_bwd(x_kv, ln_kv.reshape(1, d), dhk, n=d, dx_in=dcur, name="kv_norm_bwd")
    grads["ln_kv"] = dg.sum(axis=0)

    d_ln_mix_a, d_w_pool, d_b_pool, d_pool_scale = ([None] * N_A for _ in range(4))
    for l in reversed(range(N_A)):
        xin, dpool = saved_a[l]
        dx1 = ffn_bwd(dcur, l)
        dd, dwp, dbp, dsp = _pool_mm_bwd(dpool, wpool_f[l], bpool_f[l], pscale_f[l], dx1, name="pool_mm_bwd")
        d_w_pool[l], d_b_pool[l], d_pool_scale[l] = dwp, dbp.sum(axis=0), dsp.sum(axis=0)
        dcur, dg = _rms_pool_bwd(xin, lna_f[l], dd, dx1, name="pool_bwd")
        d_ln_mix_a[l] = dg.sum(axis=0)
    grad_x = dcur.reshape(1, s, d)

    gm = {
        "w_pool": jnp.stack(d_w_pool).reshape(N_A, N_GROUPS, N_SHARD, GROUP_DIM // N_SHARD, GROUP_DIM).transpose(2, 0, 1, 3, 4),
        "w_dkv": grads["w_dkv"].reshape(N_SHARD, d // N_SHARD, KV_LORA + ROPE),
        "w_uk": grads["w_uk"].reshape(KV_LORA, N_SHARD, -1).transpose(1, 0, 2),
        "w_uv": grads["w_uv"].reshape(KV_LORA, N_SHARD, -1).transpose(1, 0, 2),
        "w_dq": jnp.stack(d_w_dq).reshape(N_B, N_SHARD, d // N_SHARD, Q_LORA).transpose(1, 0, 2, 3),
        "w_uq": jnp.stack(d_w_uq).reshape(N_B, Q_LORA, N_SHARD, -1).transpose(2, 0, 1, 3),
        "w_o": jnp.stack(d_w_o).reshape(N_B, N_SHARD, d // N_SHARD, d).transpose(1, 0, 2, 3),
    }
    mats_grad = _pack([gm[n] for n in mat_names], WIRE_DTYPE, lead=1)
    mats_landing = lax.dynamic_update_slice_in_dim(lax.empty(mats_grad.shape, WIRE_DTYPE), own_part(mats_grad), my_chip, axis=0)
    (mats_scatter,), _ = _exchange_start([Exchange("scatter", (mats_grad,), (mats_landing,))], name="scatter_start_small")
    ffn_landed = _exchange_wait(scattering, dcur, name="scatter_wait_0")
    (mats_landed,) = _exchange_wait(mats_scatter, ffn_landed[0], name="scatter_wait_small")
    landed = [ffn_landed[0].reshape(N_SHARD, DEPTH * d, FF_SHARD), ffn_landed[1].reshape(N_SHARD, DEPTH * d, FF_SHARD),
              ffn_landed[2].reshape(N_SHARD, DEPTH * FF_SHARD, d), mats_landed]
    chip_sums = [_sum_slots(p, name="sum_chips") for p in landed]
    sib_sums = _swap_with_sibling(chip_sums, name="swap_sibling")

    vec_full = {
        "ln_mix_a": jnp.stack(d_ln_mix_a), "b_pool": jnp.stack(d_b_pool).reshape(N_A, N_GROUPS, GROUP_DIM),
        "pool_scale": jnp.stack(d_pool_scale), "ln_ffn": jnp.stack(d_ln_ffn), "ln_kv": grads["ln_kv"],
        "g_kv_latent": grads["g_kv_latent"], "g_k": grads["g_k"], "ln_mix_b": jnp.stack(d_ln_mix_b),
        "g_q_latent": jnp.stack(d_g_q_latent), "g_q": jnp.stack(d_g_q),
    }
    small_names = list(vec_full)
    small_shapes = [vec_full[n].shape for n in small_names] + [(SUBLANES * LANES,)]
    small = _all_reduce_small(_pack([vec_full[n] for n in small_names] + [loss_part.reshape(-1)], F32), name="all_reduce_small")
    small_sum = _unpack(small, small_shapes)
    loss = jnp.sum(small_sum[-1])
    vec_grad = dict(zip(small_names, small_sum[:-1]))
    vec_grad["ln_mix_a"] = lax.dynamic_slice_in_dim(vec_grad["ln_mix_a"], my_chip * (d // N_SHARD), d // N_SHARD, axis=1)
    vec_grad["pool_scale"] = lax.dynamic_slice_in_dim(vec_grad["pool_scale"], my_chip * (d // N_SHARD), d // N_SHARD, axis=1)
    vec_grad["b_pool"] = lax.dynamic_slice_in_dim(vec_grad["b_pool"], my_chip * (GROUP_DIM // N_SHARD), GROUP_DIM // N_SHARD, axis=2)

    out_g, out_d, out_m, out_v = {}, {}, {}, {}
    for idx, (nm, rows, cols) in enumerate((("w_gate", DEPTH * d, FF_SHARD), ("w_up", DEPTH * d, FF_SHARD), ("w_down", DEPTH * FF_SHARD, d))):
        res = _adamw(weights[nm].reshape(rows, cols), mom_m[nm].reshape(rows, cols), mom_v[nm].reshape(rows, cols),
                     [chip_sums[idx], sib_sums[idx]], name="adamw_ffn")
        shp = weights[nm].shape
        out_g[nm], out_d[nm], out_m[nm], out_v[nm] = (r.reshape(shp) for r in res)

    rest = list(mat_names) + small_names
    rest_shapes = [weights[n].shape for n in rest]
    vec_rows = _pack([vec_grad[n].reshape(weights[n].shape) for n in small_names], F32)
    fill = jnp.zeros(((-(chip_sums[3].shape[0] + vec_rows.shape[0])) % ADAMW_ROWS, LANES), F32)
    g_own = jnp.concatenate([chip_sums[3], vec_rows, fill], axis=0)
    g_sib = jnp.concatenate([sib_sums[3], jnp.zeros_like(vec_rows), fill], axis=0)
    res = _adamw(*[jnp.concatenate([_pack([src[n] for n in rest], F32), fill], axis=0) for src in (weights, mom_m, mom_v)],
                 [g_own, g_sib], name="adamw_small")
    for tgt, buf in zip((out_g, out_d, out_m, out_v), res):
        for n, arr in zip(rest, _unpack(buf, rest_shapes)):
            tgt[n] = arr

    return (loss, grad_x, *[out_g[n] for n in order], *[out_d[n] for n in order],
            *[out_m[n] for n in order], *[out_v[n] for n in order])
```

```python
import math
from typing import Any, NamedTuple

import jax
import jax.numpy as jnp
from jax import lax
from jax.experimental import pallas as pl
from jax.experimental.pallas import tpu as pltpu

F32 = jnp.float32
BF16 = jnp.bfloat16
MXU_DTYPE = BF16
WIRE_DTYPE = BF16

D_MODEL = 1024
N_A = 2
N_B = 2
DEPTH = 4
POOL_WINDOWS = (2, 4, 8, 16)
N_GROUPS = 4
GROUP_DIM = 256
POOL_HALO = 16
N_HEADS = 8
NOPE = 128
ROPE = 64
QK_DIM = 192
HEAD_PAD = 256
V_DIM = 128
Q_LORA = 256
KV_LORA = 512
CKV_PAD = 640
ROPE_THETA = 10000.0
CHUNK = 64
EPS = 1e-6
N_SHARD = 4
FF_SHARD = 704
LANES = 128
SUBLANES = 8
ADAM_LR, ADAM_B1, ADAM_B2, ADAM_EPS, ADAM_WD, ADAM_STEP = 0.001, 0.9, 0.999, 1e-08, 0.01, 10
MESH = pl.DeviceIdType.MESH
ANY = pl.BlockSpec(memory_space=pl.ANY)
VMEM_SPEC = pl.BlockSpec(memory_space=pltpu.VMEM)


def _tile(n, pref):
    if n <= pref:
        return n
    t = pref - pref % SUBLANES
    while n % t:
        t -= SUBLANES
    return t


def _fold8(v):
    r, n = v.shape
    return v.reshape(r // SUBLANES, SUBLANES, n).sum(axis=0)


def _dot(a, b, dims):
    return lax.dot_general(a.astype(MXU_DTYPE), b.astype(MXU_DTYPE), (dims, ((), ())),
                           preferred_element_type=F32)


def _nn(a, b):
    return _dot(a, b, ((1,), (0,)))


def _nt(a, b):
    return _dot(a, b, ((1,), (1,)))


def _tn(a, b):
    return _dot(a, b, ((0,), (0,)))


def _mm(a, b, *, ta=False, tb=False, resid=None, out_dtype=F32, name):
    assert not (ta and tb)
    m, k = (a.shape[1], a.shape[0]) if ta else a.shape
    n = b.shape[0] if tb else b.shape[1]
    tm, tn = _tile(m, 512), _tile(n, 1024)

    def body(*refs):
        if resid is None:
            a_ref, b_ref, o_ref = refs
        else:
            a_ref, b_ref, r_ref, o_ref = refs
        acc = (_tn if ta else _nt if tb else _nn)(a_ref[...], b_ref[...])
        if resid is not None:
            acc = r_ref[...] + acc
        o_ref[...] = acc.astype(o_ref.dtype)

    in_specs = [pl.BlockSpec((k, tm), lambda i, j: (0, i)) if ta else pl.BlockSpec((tm, k), lambda i, j: (i, 0)),
                pl.BlockSpec((tn, k), lambda i, j: (j, 0)) if tb else pl.BlockSpec((k, tn), lambda i, j: (0, j))]
    args = [a, b]
    if resid is not None:
        in_specs.append(pl.BlockSpec((tm, tn), lambda i, j: (i, j)))
        args.append(resid)
    return pl.pallas_call(
        body, name=name, grid=(m // tm, n // tn), in_specs=in_specs,
        out_specs=pl.BlockSpec((tm, tn), lambda i, j: (i, j)),
        out_shape=jax.ShapeDtypeStruct((m, n), out_dtype))(*args)


def _mm_tn(a, b, *, name, at=False, out_dtype=F32):
    m = b.shape[0]
    k1 = a.shape[0] if at else a.shape[1]
    n = b.shape[1]
    tm, tn = _tile(m, 512), _tile(n, 1024)
    nm = m // tm

    def body(a_ref, b_ref, o_ref, acc):
        i = pl.program_id(1)

        @pl.when(i == 0)
        def _():
            acc[...] = jnp.zeros_like(acc)

        acc[...] += (_nn if at else _tn)(a_ref[...], b_ref[...])

        @pl.when(i == nm - 1)
        def _():
            o_ref[...] = acc[...].astype(o_ref.dtype)

    return pl.pallas_call(
        body, name=name, grid=(n // tn, nm),
        in_specs=[pl.BlockSpec((k1, tm), lambda j, i: (0, i)) if at else pl.BlockSpec((tm, k1), lambda j, i: (i, 0)),
                  pl.BlockSpec((tm, tn), lambda j, i: (i, j))],
        out_specs=pl.BlockSpec((k1, tn), lambda j, i: (0, j)),
        out_shape=jax.ShapeDtypeStruct((k1, n), out_dtype),
        scratch_shapes=[pltpu.VMEM((k1, tn), F32)])(a, b)


def _rms_fwd(x, g, *, n, n_valid=None, name):
    out_dtype = MXU_DTYPE
    rows = x.shape[0]
    tm = _tile(rows, 512)
    inv_n = 1.0 / (n_valid or n)

    def body(x_ref, g_ref, o_ref):
        xv = x_ref[...]
        r = lax.rsqrt(jnp.sum(xv * xv, axis=-1, keepdims=True) * inv_n + EPS)
        o_ref[...] = (xv * r * g_ref[...]).astype(o_ref.dtype)

    return pl.pallas_call(
        body, name=name, grid=(rows // tm,),
        in_specs=[pl.BlockSpec((tm, n), lambda i: (i, 0)), pl.BlockSpec((1, n), lambda i: (0, 0))],
        out_specs=pl.BlockSpec((tm, n), lambda i: (i, 0)),
        out_shape=jax.ShapeDtypeStruct((rows, n), out_dtype))(x, g)


def _rms_bwd_math(xv, gv, dyv, inv_n):
    r = lax.rsqrt(jnp.sum(xv * xv, axis=-1, keepdims=True) * inv_n + EPS)
    xh = xv * r
    gy = dyv * gv
    dx = r * (gy - xh * (jnp.sum(gy * xh, axis=-1, keepdims=True) * inv_n))
    return dx, dyv * xh


def _rms_bwd(x, g, dy, *, n, dx_in=None, after=None, name):
    rows = x.shape[0]
    tm = _tile(rows, 512)
    inv_n = 1.0 / n

    def body(*refs):
        if after is not None:
            refs = refs[:-3] + refs[-2:]
        if dx_in is None:
            x_ref, g_ref, dy_ref, dx_ref, dg_ref = refs
        else:
            x_ref, g_ref, dy_ref, din_ref, dx_ref, dg_ref = refs
        dx, dgc = _rms_bwd_math(x_ref[...], g_ref[...], dy_ref[...], inv_n)
        if dx_in is not None:
            dx = din_ref[...] + dx
        dx_ref[...] = dx

        @pl.when(pl.program_id(0) == 0)
        def _():
            dg_ref[...] = jnp.zeros_like(dg_ref)

        dg_ref[...] += _fold8(dgc)

    row_spec = pl.BlockSpec((tm, n), lambda i: (i, 0))
    in_specs = [row_spec, pl.BlockSpec((1, n), lambda i: (0, 0)), row_spec]
    args = [x, g, dy]
    if dx_in is not None:
        in_specs.append(row_spec)
        args.append(dx_in)
    if after is not None:
        in_specs.append(ANY)
        args.append(after)
    return pl.pallas_call(
        body, name=name, grid=(rows // tm,), in_specs=in_specs,
        out_specs=[row_spec, pl.BlockSpec((SUBLANES, n), lambda i: (0, 0))],
        out_shape=[jax.ShapeDtypeStruct((rows, n), F32), jax.ShapeDtypeStruct((SUBLANES, n), F32)])(*args)


def _pool_counts(t0, tm, w):
    t = t0 + lax.broadcasted_iota(jnp.int32, (tm, 1), 0)
    return jnp.minimum(t + 1, w).astype(F32)


def _rms_pool_fwd(x, g, *, name):
    s, d = x.shape
    tm = _tile(s, 512)
    hb = tm // POOL_HALO

    def body(x_ref, halo_ref, g_ref, o_ref):
        i = pl.program_id(0)
        gv = g_ref[...]

        def norm(v):
            return v * lax.rsqrt(jnp.mean(v * v, axis=-1, keepdims=True) + EPS) * gv

        h = norm(x_ref[...])
        halo = norm(halo_ref[...]) * (i > 0).astype(F32)
        hh = jnp.concatenate([halo, h], axis=0)
        rows = tm + POOL_HALO
        for gi, w in enumerate(POOL_WINDOWS):
            cols = slice(gi * GROUP_DIM, (gi + 1) * GROUP_DIM)
            acc = hh[:, cols]
            k = 1
            while k < w:
                acc = acc + pltpu.roll(acc, k, 0)
                k *= 2
            win = acc[POOL_HALO:rows]
            o_ref[:, cols] = (win / _pool_counts(i * tm, tm, w) - h[:, cols]).astype(o_ref.dtype)

    return pl.pallas_call(
        body, name=name, grid=(s // tm,),
        in_specs=[pl.BlockSpec((tm, d), lambda i: (i, 0)),
                  pl.BlockSpec((POOL_HALO, d), lambda i: (jnp.maximum(i * hb - 1, 0), 0)),
                  pl.BlockSpec((1, d), lambda i: (0, 0))],
        out_specs=pl.BlockSpec((tm, d), lambda i: (i, 0)),
        out_shape=jax.ShapeDtypeStruct((s, d), MXU_DTYPE))(x, x, g)


def _rms_pool_bwd(x, g, dd, dx_in, *, name):
    s, d = x.shape
    tm = _tile(s, 512)
    hb = tm // POOL_HALO
    nt = s // tm

    def body(x_ref, g_ref, dd_ref, halo_ref, din_ref, dx_ref, dg_ref):
        i = pl.program_id(0)
        ddv = dd_ref[...]
        halo = halo_ref[...] * (i < nt - 1).astype(F32)
        rows = tm + POOL_HALO
        parts = []
        for gi, w in enumerate(POOL_WINDOWS):
            cols = slice(gi * GROUP_DIM, (gi + 1) * GROUP_DIM)
            acc = jnp.concatenate([ddv[:, cols] / _pool_counts(i * tm, tm, w), halo[:, cols] * (1.0 / w)], axis=0)
            k = 1
            while k < w:
                acc = acc + pltpu.roll(acc, rows - k, 0)
                k *= 2
            parts.append(acc[0:tm] - ddv[:, cols])
        dh = jnp.concatenate(parts, axis=1)
        dx, dgc = _rms_bwd_math(x_ref[...], g_ref[...], dh, 1.0 / d)
        dx_ref[...] = din_ref[...] + dx

        @pl.when(i == 0)
        def _():
            dg_ref[...] = jnp.zeros_like(dg_ref)

        dg_ref[...] += _fold8(dgc)

    row_spec = pl.BlockSpec((tm, d), lambda i: (i, 0))
    return pl.pallas_call(
        body, name=name, grid=(nt,),
        in_specs=[row_spec, pl.BlockSpec((1, d), lambda i: (0, 0)), row_spec,
                  pl.BlockSpec((POOL_HALO, d), lambda i: (jnp.minimum((i + 1) * hb, s // POOL_HALO - 1), 0)),
                  row_spec],
        out_specs=[row_spec, pl.BlockSpec((SUBLANES, d), lambda i: (0, 0))],
        out_shape=[jax.ShapeDtypeStruct((s, d), F32), jax.ShapeDtypeStruct((SUBLANES, d), F32)])(x, g, dd, dd, dx_in)


def _pool_mm_fwd(dpool, w, b, scale, x, *, name):
    s, d = x.shape
    tm = _tile(s, 512)

    def body(d_ref, w_ref, b_ref, s_ref, x_ref, o_ref):
        for gi in range(N_GROUPS):
            cols = slice(gi * GROUP_DIM, (gi + 1) * GROUP_DIM)
            y = _nn(d_ref[:, cols], w_ref[gi]) + b_ref[:, cols]
            o_ref[:, cols] = x_ref[:, cols] + y * s_ref[:, cols]

    row_spec = pl.BlockSpec((tm, d), lambda i: (i, 0))
    vec_spec = pl.BlockSpec((1, d), lambda i: (0, 0))
    return pl.pallas_call(
        body, name=name, grid=(s // tm,),
        in_specs=[row_spec, pl.BlockSpec((N_GROUPS, GROUP_DIM, GROUP_DIM), lambda i: (0, 0, 0)), vec_spec, vec_spec, row_spec],
        out_specs=row_spec, out_shape=jax.ShapeDtypeStruct((s, d), F32))(dpool, w, b, scale, x)


def _pool_mm_bwd(dpool, w, b, scale, dx, *, name):
    s, d = dx.shape
    tm = _tile(s, 512)

    def body(d_ref, w_ref, b_ref, s_ref, dx_ref, dd_ref, dw_ref, db_ref, ds_ref):
        @pl.when(pl.program_id(0) == 0)
        def _():
            dw_ref[...] = jnp.zeros_like(dw_ref)
            db_ref[...] = jnp.zeros_like(db_ref)
            ds_ref[...] = jnp.zeros_like(ds_ref)

        for gi in range(N_GROUPS):
            cols = slice(gi * GROUP_DIM, (gi + 1) * GROUP_DIM)
            dg = d_ref[:, cols]
            y = _nn(dg, w_ref[gi]) + b_ref[:, cols]
            dxg = dx_ref[:, cols]
            dy = dxg * s_ref[:, cols]
            ds_ref[:, cols] += _fold8(dxg * y)
            db_ref[:, cols] += _fold8(dy)
            dw_ref[gi] += _tn(dg, dy)
            dd_ref[:, cols] = _nt(dy, w_ref[gi])

    row_spec = pl.BlockSpec((tm, d), lambda i: (i, 0))
    vec_spec = pl.BlockSpec((1, d), lambda i: (0, 0))
    w_spec = pl.BlockSpec((N_GROUPS, GROUP_DIM, GROUP_DIM), lambda i: (0, 0, 0))
    part_spec = pl.BlockSpec((SUBLANES, d), lambda i: (0, 0))
    return pl.pallas_call(
        body, name=name, grid=(s // tm,),
        in_specs=[row_spec, w_spec, vec_spec, vec_spec, row_spec],
        out_specs=[row_spec, w_spec, part_spec, part_spec],
        out_shape=[jax.ShapeDtypeStruct((s, d), F32), jax.ShapeDtypeStruct((N_GROUPS, GROUP_DIM, GROUP_DIM), F32),
                   jax.ShapeDtypeStruct((SUBLANES, d), F32), jax.ShapeDtypeStruct((SUBLANES, d), F32)])(dpool, w, b, scale, dx)


def _ffn_up(hf, wg, wu, *, name):
    s, d = hf.shape
    tm = _tile(s, 512)

    def body(h_ref, wg_ref, wu_ref, a_ref, b_ref, u_ref):
        hv = h_ref[...]
        a = _nn(hv, wg_ref[...])
        b = _nn(hv, wu_ref[...])
        a_ref[...] = a
        b_ref[...] = b
        u_ref[...] = (a * (1.0 / (1.0 + jnp.exp(-a))) * b).astype(u_ref.dtype)

    w_spec = pl.BlockSpec((None, d, FF_SHARD), lambda j, i: (j, 0, 0))
    h_spec = pl.BlockSpec((None, tm, FF_SHARD), lambda j, i: (j, i, 0))
    hid = (N_SHARD, s, FF_SHARD)
    return pl.pallas_call(
        body, name=name, grid=(N_SHARD, s // tm),
        in_specs=[pl.BlockSpec((tm, d), lambda j, i: (i, 0)), w_spec, w_spec],
        out_specs=[h_spec, h_spec, h_spec],
        out_shape=[jax.ShapeDtypeStruct(hid, F32), jax.ShapeDtypeStruct(hid, F32), jax.ShapeDtypeStruct(hid, MXU_DTYPE)])(hf, wg, wu)


def _ffn_down(u, wd, x, *, name):
    s, d = x.shape
    tm = _tile(s, 1024)

    def body(u_ref, w_ref, x_ref, o_ref):
        j = pl.program_id(1)

        @pl.when(j == 0)
        def _():
            o_ref[...] = x_ref[...]

        o_ref[...] += _nn(u_ref[...], w_ref[...])

    return pl.pallas_call(
        body, name=name, grid=(s // tm, N_SHARD),
        in_specs=[pl.BlockSpec((None, tm, FF_SHARD), lambda i, j: (j, i, 0)),
                  pl.BlockSpec((None, FF_SHARD, d), lambda i, j: (j, 0, 0)),
                  pl.BlockSpec((tm, d), lambda i, j: (i, 0))],
        out_specs=pl.BlockSpec((tm, d), lambda i, j: (i, 0)),
        out_shape=jax.ShapeDtypeStruct((s, d), F32))(u, wd, x)


def _ffn_bwd_hidden(dy, wd, a, b, *, name):
    s, d = dy.shape
    tm = _tile(s, 512)

    def body(dy_ref, w_ref, a_ref, b_ref, da_ref, db_ref):
        du = _nt(dy_ref[...], w_ref[...])
        av, bv = a_ref[...], b_ref[...]
        sg = 1.0 / (1.0 + jnp.exp(-av))
        da_ref[...] = (du * bv * (sg * (1.0 + av * (1.0 - sg)))).astype(da_ref.dtype)
        db_ref[...] = (du * (av * sg)).astype(db_ref.dtype)

    h_spec = pl.BlockSpec((None, tm, FF_SHARD), lambda j, i: (j, i, 0))
    hid = jax.ShapeDtypeStruct((N_SHARD, s, FF_SHARD), MXU_DTYPE)
    return pl.pallas_call(
        body, name=name, grid=(N_SHARD, s // tm),
        in_specs=[pl.BlockSpec((tm, d), lambda j, i: (i, 0)),
                  pl.BlockSpec((None, FF_SHARD, d), lambda j, i: (j, 0, 0)), h_spec, h_spec],
        out_specs=[h_spec, h_spec], out_shape=[hid, hid])(dy, wd, a, b)


def _ffn_bwd_dwd(u, dy, *, name):
    s, d = dy.shape
    tm = _tile(s, 512)
    nm = s // tm

    def body(u_ref, dy_ref, o_ref, acc):
        i = pl.program_id(1)

        @pl.when(i == 0)
        def _():
            acc[...] = jnp.zeros_like(acc)

        acc[...] += _tn(u_ref[...], dy_ref[...])

        @pl.when(i == nm - 1)
        def _():
            o_ref[...] = acc[...].astype(o_ref.dtype)

    return pl.pallas_call(
        body, name=name, grid=(N_SHARD, nm),
        in_specs=[pl.BlockSpec((None, tm, FF_SHARD), lambda j, i: (j, i, 0)), pl.BlockSpec((tm, d), lambda j, i: (i, 0))],
        out_specs=pl.BlockSpec((None, FF_SHARD, d), lambda j, i: (j, 0, 0)),
        out_shape=jax.ShapeDtypeStruct((N_SHARD, FF_SHARD, d), WIRE_DTYPE),
        scratch_shapes=[pltpu.VMEM((FF_SHARD, d), F32)])(u, dy)


def _ffn_bwd_dwgu(hf, da, db, *, name):
    s, d = hf.shape
    tm = _tile(s, 512)
    nm = s // tm

    def body(h_ref, da_ref, db_ref, og_ref, ou_ref, accg, accu):
        i = pl.program_id(1)

        @pl.when(i == 0)
        def _():
            accg[...] = jnp.zeros_like(accg)
            accu[...] = jnp.zeros_like(accu)

        hv = h_ref[...]
        accg[...] += _tn(hv, da_ref[...])
        accu[...] += _tn(hv, db_ref[...])

        @pl.when(i == nm - 1)
        def _():
            og_ref[...] = accg[...].astype(og_ref.dtype)
            ou_ref[...] = accu[...].astype(ou_ref.dtype)

    h_spec = pl.BlockSpec((None, tm, FF_SHARD), lambda j, i: (j, i, 0))
    w_spec = pl.BlockSpec((None, d, FF_SHARD), lambda j, i: (j, 0, 0))
    grad = jax.ShapeDtypeStruct((N_SHARD, d, FF_SHARD), WIRE_DTYPE)
    return pl.pallas_call(
        body, name=name, grid=(N_SHARD, nm),
        in_specs=[pl.BlockSpec((tm, d), lambda j, i: (i, 0)), h_spec, h_spec],
        out_specs=[w_spec, w_spec], out_shape=[grad, grad],
        scratch_shapes=[pltpu.VMEM((d, FF_SHARD), F32), pltpu.VMEM((d, FF_SHARD), F32)])(hf, da, db)


def _ffn_bwd_dh(da, db, wg, wu, *, name):
    s = da.shape[1]
    d = wg.shape[1]
    tm = _tile(s, 1024)

    def body(da_ref, db_ref, wg_ref, wu_ref, o_ref):
        j = pl.program_id(1)

        @pl.when(j == 0)
        def _():
            o_ref[...] = jnp.zeros_like(o_ref)

        o_ref[...] += _nt(da_ref[...], wg_ref[...]) + _nt(db_ref[...], wu_ref[...])

    h_spec = pl.BlockSpec((None, tm, FF_SHARD), lambda i, j: (j, i, 0))
    w_spec = pl.BlockSpec((None, d, FF_SHARD), lambda i, j: (j, 0, 0))
    return pl.pallas_call(
        body, name=name, grid=(s // tm, N_SHARD),
        in_specs=[h_spec, h_spec, w_spec, w_spec],
        out_specs=pl.BlockSpec((tm, d), lambda i, j: (i, 0)),
        out_shape=jax.ShapeDtypeStruct((s, d), F32))(da, db, wg, wu)


def _rope_tables(pos, inv, *, name):
    s = pos.shape[0]
    tm = _tile(s, 512)
    half = ROPE // 2

    def body(p_ref, i_ref, c_ref, s_ref):
        ang = p_ref[...] * i_ref[...]
        lane = lax.broadcasted_iota(jnp.int32, ang.shape, 1)
        live = lane < ROPE
        c_ref[...] = jnp.where(live, jnp.cos(ang), 0.0)
        sn = jnp.sin(ang)
        s_ref[...] = jnp.where(live, jnp.where(lane < half, -sn, sn), 0.0)

    out = jax.ShapeDtypeStruct((s, LANES), F32)
    return pl.pallas_call(
        body, name=name, grid=(s // tm,),
        in_specs=[pl.BlockSpec((tm, 1), lambda i: (i, 0)), pl.BlockSpec((1, LANES), lambda i: (0, 0))],
        out_specs=[pl.BlockSpec((tm, LANES), lambda i: (i, 0))] * 2, out_shape=[out, out])(pos, inv)


def _swap_halves(v):
    half = ROPE // 2
    lane = lax.broadcasted_iota(jnp.int32, v.shape, 1)
    return jnp.where(lane < half, pltpu.roll(v, LANES - half, 1), pltpu.roll(v, half, 1))


def _head_norm_rope_fwd(raw, g, cos, sin, *, name):
    s = raw.shape[0]
    tm = _tile(s, 256)
    width = N_HEADS * HEAD_PAD

    def body(x_ref, g_ref, c_ref, s_ref, o_ref):
        cv, sv = c_ref[...], s_ref[...]
        for h in range(N_HEADS):
            lo = h * HEAD_PAD
            xa = x_ref[:, lo:lo + NOPE]
            xb = x_ref[:, lo + NOPE:lo + HEAD_PAD]
            ms = (jnp.sum(xa * xa, axis=-1, keepdims=True) + jnp.sum(xb * xb, axis=-1, keepdims=True)) * (1.0 / QK_DIM)
            r = lax.rsqrt(ms + EPS)
            o_ref[:, lo:lo + NOPE] = (xa * r * g_ref[:, 0:NOPE]).astype(o_ref.dtype)
            yb = xb * r * g_ref[:, NOPE:HEAD_PAD]
            o_ref[:, lo + NOPE:lo + HEAD_PAD] = (yb * cv + _swap_halves(yb) * sv).astype(o_ref.dtype)

    row_spec = pl.BlockSpec((tm, width), lambda i: (i, 0))
    tab_spec = pl.BlockSpec((tm, LANES), lambda i: (i, 0))
    return pl.pallas_call(
        body, name=name, grid=(s // tm,),
        in_specs=[row_spec, pl.BlockSpec((1, HEAD_PAD), lambda i: (0, 0)), tab_spec, tab_spec],
        out_specs=row_spec, out_shape=jax.ShapeDtypeStruct((s, width), MXU_DTYPE))(raw, g, cos, sin)


def _head_norm_rope_bwd(raw, g, cos, sin, dout, *, name):
    s = raw.shape[0]
    tm = _tile(s, 256)
    width = N_HEADS * HEAD_PAD

    def body(x_ref, g_ref, c_ref, s_ref, do_ref, dx_ref, dg_ref):
        @pl.when(pl.program_id(0) == 0)
        def _():
            dg_ref[...] = jnp.zeros_like(dg_ref)

        cv, sv = c_ref[...], s_ref[...]
        ga, gb = g_ref[:, 0:NOPE], g_ref[:, NOPE:HEAD_PAD]
        for h in range(N_HEADS):
            lo = h * HEAD_PAD
            xa = x_ref[:, lo:lo + NOPE]
            xb = x_ref[:, lo + NOPE:lo + HEAD_PAD]
            dya = do_ref[:, lo:lo + NOPE]
            dob = do_ref[:, lo + NOPE:lo + HEAD_PAD]
            dyb = dob * cv + _swap_halves(dob * sv)
            ms = (jnp.sum(xa * xa, axis=-1, keepdims=True) + jnp.sum(xb * xb, axis=-1, keepdims=True)) * (1.0 / QK_DIM)
            r = lax.rsqrt(ms + EPS)
            xha, xhb = xa * r, xb * r
            gya, gyb = dya * ga, dyb * gb
            dot = (jnp.sum(gya * xha, axis=-1, keepdims=True) + jnp.sum(gyb * xhb, axis=-1, keepdims=True)) * (1.0 / QK_DIM)
            dx_ref[:, lo:lo + NOPE] = r * (gya - xha * dot)
            dx_ref[:, lo + NOPE:lo + HEAD_PAD] = r * (gyb - xhb * dot)
            dg_ref[:, 0:NOPE] += _fold8(dya * xha)
            dg_ref[:, NOPE:HEAD_PAD] += _fold8(dyb * xhb)

    row_spec = pl.BlockSpec((tm, width), lambda i: (i, 0))
    tab_spec = pl.BlockSpec((tm, LANES), lambda i: (i, 0))
    return pl.pallas_call(
        body, name=name, grid=(s // tm,),
        in_specs=[row_spec, pl.BlockSpec((1, HEAD_PAD), lambda i: (0, 0)), tab_spec, tab_spec, row_spec],
        out_specs=[row_spec, pl.BlockSpec((SUBLANES, HEAD_PAD), lambda i: (0, 0))],
        out_shape=[jax.ShapeDtypeStruct((s, width), F32), jax.ShapeDtypeStruct((SUBLANES, HEAD_PAD), F32)])(raw, g, cos, sin, dout)


def _k_assemble(kn, ckv, *, name):
    s = kn.shape[0]
    tm = _tile(s, 512)
    width = N_HEADS * HEAD_PAD

    def body(kn_ref, pe_ref, o_ref):
        pe = pe_ref[...]
        for h in range(N_HEADS):
            o_ref[:, h * HEAD_PAD:h * HEAD_PAD + NOPE] = kn_ref[:, h * NOPE:(h + 1) * NOPE]
            o_ref[:, h * HEAD_PAD + NOPE:(h + 1) * HEAD_PAD] = pe

    return pl.pallas_call(
        body, name=name, grid=(s // tm,),
        in_specs=[pl.BlockSpec((tm, N_HEADS * NOPE), lambda i: (i, 0)),
                  pl.BlockSpec((tm, LANES), lambda i: (i, KV_LORA // LANES))],
        out_specs=pl.BlockSpec((tm, width), lambda i: (i, 0)),
        out_shape=jax.ShapeDtypeStruct((s, width), F32))(kn, ckv)


def _k_disassemble(dk_raw, *, name):
    s = dk_raw.shape[0]
    tm = _tile(s, 512)
    width = N_HEADS * HEAD_PAD

    def body(dk_ref, dkn_ref, dpe_ref):
        pe = dk_ref[:, NOPE:HEAD_PAD]
        for h in range(N_HEADS):
            dkn_ref[:, h * NOPE:(h + 1) * NOPE] = dk_ref[:, h * HEAD_PAD:h * HEAD_PAD + NOPE]
            if h:
                pe = pe + dk_ref[:, h * HEAD_PAD + NOPE:(h + 1) * HEAD_PAD]
        dpe_ref[...] = pe

    return pl.pallas_call(
        body, name=name, grid=(s // tm,),
        in_specs=[pl.BlockSpec((tm, width), lambda i: (i, 0))],
        out_specs=[pl.BlockSpec((tm, N_HEADS * NOPE), lambda i: (i, 0)), pl.BlockSpec((tm, LANES), lambda i: (i, 0))],
        out_shape=[jax.ShapeDtypeStruct((s, N_HEADS * NOPE), F32), jax.ShapeDtypeStruct((s, LANES), F32)])(dk_raw)


ATTN_SCALE = 1.0 / math.sqrt(QK_DIM)
MASKED = -1e30


ATTN_TILE = 512


def _chunk_mask(q0, k0, shape, q_axis):
    qpos = q0 + lax.broadcasted_iota(jnp.int32, shape, q_axis)
    kpos = k0 + lax.broadcasted_iota(jnp.int32, shape, 1 - q_axis)
    return kpos // CHUNK <= qpos // CHUNK


LOG2E = math.log2(math.e)
SCORE_LOG2 = ATTN_SCALE * LOG2E


def _causal_pairs(n, by_key):
    if by_key:
        pairs = [(i, j) for j in range(n) for i in range(j, n)]
    else:
        pairs = [(i, j) for i in range(n) for j in range(i + 1)]
    return jnp.asarray([p[0] for p in pairs], jnp.int32), jnp.asarray([p[1] for p in pairs], jnp.int32)


def _attn_fwd(q, k, vt, *, name):
    s = q.shape[0]
    t = _tile(s, ATTN_TILE)
    n = s // t
    qi_tab, kj_tab = _causal_pairs(n, by_key=False)

    def body(qi_ref, kj_ref, q_ref, k_ref, vt_ref, o_ref, lse_ref, m_sc, l_sc, acc):
        pair = pl.program_id(1)
        qi, kj = qi_ref[pair], kj_ref[pair]

        @pl.when(kj == 0)
        def _():
            m_sc[...] = jnp.full_like(m_sc, MASKED)
            l_sc[...] = jnp.zeros_like(l_sc)
            acc[...] = jnp.zeros_like(acc)

        def step(masked):
            st = _nt(k_ref[...], q_ref[...])
            if masked:
                st = jnp.where(_chunk_mask(qi * t, kj * t, (t, t), 1), st, MASKED)
            m_prev = m_sc[...]
            m_new = jnp.maximum(m_prev, jnp.max(st, axis=0, keepdims=True) * SCORE_LOG2)
            alpha = jnp.exp2(m_prev - m_new)
            pt = jnp.exp2(st * SCORE_LOG2 - m_new)
            l_sc[...] = alpha * l_sc[...] + jnp.sum(pt, axis=0, keepdims=True)
            acc[...] = alpha * acc[...] + _nn(vt_ref[...], pt)
            m_sc[...] = m_new

        @pl.when(kj < qi)
        def _():
            step(False)

        @pl.when(kj == qi)
        def _():
            step(True)
            o_ref[...] = acc[...] / l_sc[...]
            lse_ref[...] = m_sc[...] + jnp.log(l_sc[...]) * LOG2E

    return pl.pallas_call(
        body, name=name,
        grid_spec=pltpu.PrefetchScalarGridSpec(
            num_scalar_prefetch=2, grid=(N_HEADS, int(qi_tab.shape[0])),
            in_specs=[pl.BlockSpec((t, HEAD_PAD), lambda h, p, qi, kj: (qi[p], h)),
                      pl.BlockSpec((t, HEAD_PAD), lambda h, p, qi, kj: (kj[p], h)),
                      pl.BlockSpec((V_DIM, t), lambda h, p, qi, kj: (h, kj[p]))],
            out_specs=[pl.BlockSpec((V_DIM, t), lambda h, p, qi, kj: (h, qi[p])),
                       pl.BlockSpec((None, 1, t), lambda h, p, qi, kj: (h, 0, qi[p]))],
            scratch_shapes=[pltpu.VMEM((1, t), F32), pltpu.VMEM((1, t), F32), pltpu.VMEM((V_DIM, t), F32)]),
        out_shape=[jax.ShapeDtypeStruct((N_HEADS * V_DIM, s), F32), jax.ShapeDtypeStruct((N_HEADS, 1, s), F32)])(qi_tab, kj_tab, q, k, vt)


def _attn_delta(ot, dot, *, name):
    s = ot.shape[1]
    t = _tile(s, 1024)

    def body(o_ref, do_ref, d_ref):
        d_ref[...] = jnp.sum(o_ref[...] * do_ref[...], axis=0, keepdims=True)

    blk = pl.BlockSpec((V_DIM, t), lambda h, i: (h, i))
    return pl.pallas_call(
        body, name=name, grid=(N_HEADS, s // t), in_specs=[blk, blk],
        out_specs=pl.BlockSpec((None, 1, t), lambda h, i: (h, 0, i)),
        out_shape=jax.ShapeDtypeStruct((N_HEADS, 1, s), F32))(ot, dot)


def _attn_bwd_dq(q, k, v, do, lse_col, delta_col, *, name):
    s = q.shape[0]
    t = _tile(s, ATTN_TILE)
    n = s // t
    qi_tab, kj_tab = _causal_pairs(n, by_key=False)

    def body(qi_ref, kj_ref, q_ref, k_ref, v_ref, do_ref, lse_ref, dl_ref, dq_ref, acc):
        pair = pl.program_id(1)
        qi, kj = qi_ref[pair], kj_ref[pair]

        @pl.when(kj == 0)
        def _():
            acc[...] = jnp.zeros_like(acc)

        def step(masked):
            kv = k_ref[...]
            sc = _nt(q_ref[...], kv)
            if masked:
                sc = jnp.where(_chunk_mask(qi * t, kj * t, (t, t), 0), sc, MASKED)
            p = jnp.exp2(sc * SCORE_LOG2 - lse_ref[...])
            dp = _nt(do_ref[...], v_ref[...])
            acc[...] += _nn(p * (dp - dl_ref[...]), kv)

        @pl.when(kj < qi)
        def _():
            step(False)

        @pl.when(kj == qi)
        def _():
            step(True)
            dq_ref[...] = acc[...] * ATTN_SCALE

    col = pl.BlockSpec((None, t, 1), lambda h, p, qi, kj: (h, qi[p], 0))
    return pl.pallas_call(
        body, name=name,
        grid_spec=pltpu.PrefetchScalarGridSpec(
            num_scalar_prefetch=2, grid=(N_HEADS, int(qi_tab.shape[0])),
            in_specs=[pl.BlockSpec((t, HEAD_PAD), lambda h, p, qi, kj: (qi[p], h)),
                      pl.BlockSpec((t, HEAD_PAD), lambda h, p, qi, kj: (kj[p], h)),
                      pl.BlockSpec((t, V_DIM), lambda h, p, qi, kj: (kj[p], h)),
                      pl.BlockSpec((t, V_DIM), lambda h, p, qi, kj: (qi[p], h)), col, col],
            out_specs=pl.BlockSpec((t, HEAD_PAD), lambda h, p, qi, kj: (qi[p], h)),
            scratch_shapes=[pltpu.VMEM((t, HEAD_PAD), F32)]),
        out_shape=jax.ShapeDtypeStruct((s, N_HEADS * HEAD_PAD), F32))(qi_tab, kj_tab, q, k, v, do, lse_col, delta_col)


def _attn_bwd_dkv(q, k, v, do, lse_row, delta_row, dk_in, dv_in, *, name):
    s = q.shape[0]
    t = _tile(s, ATTN_TILE)
    n = s // t
    has_in = dk_in is not None
    qi_tab, kj_tab = _causal_pairs(n, by_key=True)

    def body(qi_ref, kj_ref, *refs):
        if has_in:
            q_ref, k_ref, v_ref, do_ref, lse_ref, dl_ref, dki_ref, dvi_ref, dk_ref, dv_ref, acck, accv = refs
        else:
            q_ref, k_ref, v_ref, do_ref, lse_ref, dl_ref, dk_ref, dv_ref, acck, accv = refs
        pair = pl.program_id(1)
        qi, kj = qi_ref[pair], kj_ref[pair]

        def step(masked):
            qv, dov = q_ref[...], do_ref[...]
            st = _nt(k_ref[...], qv)
            if masked:
                st = jnp.where(_chunk_mask(qi * t, kj * t, (t, t), 1), st, MASKED)
            pt = jnp.exp2(st * SCORE_LOG2 - lse_ref[...])
            accv[...] += _nn(pt, dov)
            dpt = _nt(v_ref[...], dov)
            acck[...] += _nn(pt * (dpt - dl_ref[...]), qv)

        @pl.when(qi == kj)
        def _():
            acck[...] = jnp.zeros_like(acck)
            accv[...] = jnp.zeros_like(accv)
            step(True)

        @pl.when(qi > kj)
        def _():
            step(False)

        @pl.when(qi == n - 1)
        def _():
            dk = acck[...] * ATTN_SCALE
            dv = accv[...]
            if has_in:
                dk = dki_ref[...] + dk
                dv = dvi_ref[...] + dv
            dk_ref[...] = dk
            dv_ref[...] = dv

    row = pl.BlockSpec((None, 1, t), lambda h, p, qi, kj: (h, 0, qi[p]))
    k_spec = pl.BlockSpec((t, HEAD_PAD), lambda h, p, qi, kj: (kj[p], h))
    v_spec = pl.BlockSpec((t, V_DIM), lambda h, p, qi, kj: (kj[p], h))
    in_specs = [pl.BlockSpec((t, HEAD_PAD), lambda h, p, qi, kj: (qi[p], h)), k_spec, v_spec,
                pl.BlockSpec((t, V_DIM), lambda h, p, qi, kj: (qi[p], h)), row, row]
    args = [q, k, v, do, lse_row, delta_row]
    if has_in:
        in_specs += [k_spec, v_spec]
        args += [dk_in, dv_in]
    return pl.pallas_call(
        body, name=name,
        grid_spec=pltpu.PrefetchScalarGridSpec(
            num_scalar_prefetch=2, grid=(N_HEADS, int(qi_tab.shape[0])), in_specs=in_specs, out_specs=[k_spec, v_spec],
            scratch_shapes=[pltpu.VMEM((t, HEAD_PAD), F32), pltpu.VMEM((t, V_DIM), F32)]),
        out_shape=[jax.ShapeDtypeStruct((s, N_HEADS * HEAD_PAD), F32), jax.ShapeDtypeStruct((s, N_HEADS * V_DIM), F32)])(qi_tab, kj_tab, *args)


def _loss_head(y, target, *, name):
    s, d = y.shape
    tm = _tile(s, 512)

    def body(y_ref, t_ref, dy_ref, l_ref):
        @pl.when(pl.program_id(0) == 0)
        def _():
            l_ref[...] = jnp.zeros_like(l_ref)

        err = y_ref[...] - t_ref[...]
        dy_ref[...] = err * (1.0 / d)
        sq = _fold8(err * err)
        part = sq[:, 0:LANES]
        for cb in range(1, d // LANES):
            part = part + sq[:, cb * LANES:(cb + 1) * LANES]
        l_ref[...] += part * (0.5 / d)

    row_spec = pl.BlockSpec((tm, d), lambda i: (i, 0))
    return pl.pallas_call(
        body, name=name, grid=(s // tm,), in_specs=[row_spec, row_spec],
        out_specs=[row_spec, pl.BlockSpec((SUBLANES, LANES), lambda i: (0, 0))],
        out_shape=[jax.ShapeDtypeStruct((s, d), F32), jax.ShapeDtypeStruct((SUBLANES, LANES), F32)])(y, target)


ADAMW_ROWS = 512


def _adamw(w, m, v, g_parts, *, name):
    rows, cols = w.shape
    tm = _tile(rows, ADAMW_ROWS)
    n_parts = len(g_parts)
    c1 = 1.0 - ADAM_B1 ** ADAM_STEP
    c2 = 1.0 - ADAM_B2 ** ADAM_STEP

    def body(*refs):
        w_ref, m_ref, v_ref = refs[:3]
        g_refs = refs[3:3 + n_parts]
        g_out, d_out, m_out, v_out = refs[3 + n_parts:]
        g = g_refs[0][...]
        for r in g_refs[1:]:
            g = g + r[...]
        mn = ADAM_B1 * m_ref[...] + (1.0 - ADAM_B1) * g
        vn = ADAM_B2 * v_ref[...] + (1.0 - ADAM_B2) * (g * g)
        g_out[...] = g
        m_out[...] = mn
        v_out[...] = vn
        d_out[...] = -ADAM_LR * ((mn / c1) / (jnp.sqrt(vn / c2) + ADAM_EPS) + ADAM_WD * w_ref[...])

    spec = pl.BlockSpec((tm, cols), lambda i: (i, 0))
    out = jax.ShapeDtypeStruct((rows, cols), F32)
    return pl.pallas_call(
        body, name=name, grid=(rows // tm,), in_specs=[spec] * (3 + n_parts),
        out_specs=[spec] * 4, out_shape=[out] * 4)(w, m, v, *g_parts)


def _sum_slots(parts, *, name):
    _, rows, cols = parts.shape
    tm = _tile(rows, 512)

    def body(p_ref, o_ref):
        acc = p_ref[0].astype(F32)
        for k in range(1, N_SHARD):
            acc = acc + p_ref[k].astype(F32)
        o_ref[...] = acc

    return pl.pallas_call(
        body, name=name, grid=(rows // tm,),
        in_specs=[pl.BlockSpec((N_SHARD, tm, cols), lambda i: (0, i, 0))],
        out_specs=pl.BlockSpec((tm, cols), lambda i: (i, 0)),
        out_shape=jax.ShapeDtypeStruct((rows, cols), F32))(parts)


def _mesh_pos():
    return lax.axis_index("x"), lax.axis_index("y"), lax.axis_index("c")


CHIP_FLIPS = ((1, 0), (0, 1), (1, 1))


class Exchange(NamedTuple):
    kind: str
    srcs: tuple
    lands: tuple
    layer: Any = None


HBM_SPEC = pl.BlockSpec(memory_space=pltpu.HBM)
SEM_SPEC = pl.BlockSpec(memory_space=pltpu.SEMAPHORE)
DATAFLOW = pltpu.SideEffectType.DATAFLOW_SIDE_EFFECTING


def _exchange_copies(ex, src_refs, land_refs, send_sems, recv_sems):
    x, y, c = _mesh_pos()
    mine = 2 * x + y

    def slot(ref, chip):
        return ref.at[chip] if ex.layer is None else ref.at[chip, ex.layer]

    pairs = []
    for a, (src, land) in enumerate(zip(src_refs, land_refs)):
        for k, (fx, fy) in enumerate(CHIP_FLIPS):
            px, py = x ^ fx, y ^ fy
            peer = 2 * px + py
            src_part = src if ex.kind == "gather" else src.at[peer]
            pair = a * len(CHIP_FLIPS) + k
            common = dict(src_ref=src_part, send_sem=send_sems.at[pair], recv_sem=recv_sems.at[pair],
                          device_id=(px, py, c), device_id_type=MESH)
            pairs.append((pltpu.make_async_remote_copy(dst_ref=slot(land, mine), **common),
                          pltpu.make_async_remote_copy(dst_ref=slot(land, peer), **common)))
    return pairs


def _exchange_start(exchanges, *, name):
    srcs = [s for ex in exchanges for s in ex.srcs]
    lands = [b for ex in exchanges for b in ex.lands]
    n_arr, n_ex = len(srcs) + len(lands), len(exchanges)

    def body(*refs):
        src_refs, land_refs = refs[:len(srcs)], refs[len(srcs):n_arr]
        sems, token = refs[n_arr:n_arr + 2 * n_ex], refs[-1]
        at = 0
        for e, ex in enumerate(exchanges):
            n = len(ex.srcs)
            for send, _ in _exchange_copies(ex, src_refs[at:at + n], land_refs[at:at + n], sems[2 * e], sems[2 * e + 1]):
                send.start()
            at += n
        token[...] = jnp.zeros_like(token)

    sem_shapes = [pltpu.SemaphoreType.DMA((len(ex.srcs) * len(CHIP_FLIPS),)) for ex in exchanges for _ in range(2)]
    out = pl.pallas_call(
        body, name=name,
        out_shape=sem_shapes + [pltpu.HBM(a.shape, a.dtype) for a in srcs + lands] + [jax.ShapeDtypeStruct((SUBLANES, LANES), F32)],
        in_specs=[HBM_SPEC] * n_arr, out_specs=[SEM_SPEC] * (2 * n_ex) + [HBM_SPEC] * n_arr + [VMEM_SPEC],
        input_output_aliases={i: 2 * n_ex + i for i in range(n_arr)},
        compiler_params=pltpu.CompilerParams(has_side_effects=DATAFLOW),
    )(*[pltpu.with_memory_space_constraint(a, pltpu.HBM) for a in srcs + lands])
    sems, thru = out[:2 * n_ex], out[2 * n_ex:-1]
    pending, at = [], 0
    for e, ex in enumerate(exchanges):
        n = len(ex.srcs)
        pending.append((ex._replace(srcs=tuple(thru[at:at + n]), lands=tuple(thru[len(srcs) + at:len(srcs) + at + n])),
                        sems[2 * e], sems[2 * e + 1]))
        at += n
    return pending, out[-1]


def _exchange_wait(pending, after, *, name):
    ex, send_sems, recv_sems = pending
    n = len(ex.srcs)

    def body(*refs):
        src_refs, land_refs = refs[:n], refs[n:2 * n]
        for send, arrive in _exchange_copies(ex, src_refs, land_refs, refs[2 * n], refs[2 * n + 1]):
            send.wait_send()
            arrive.wait_recv()

    arrays = list(ex.srcs) + list(ex.lands)
    out = pl.pallas_call(
        body, name=name, out_shape=[pltpu.HBM(a.shape, a.dtype) for a in arrays],
        in_specs=[HBM_SPEC] * (2 * n) + [SEM_SPEC, SEM_SPEC, ANY], out_specs=[HBM_SPEC] * (2 * n),
        input_output_aliases={i: i for i in range(2 * n)},
        compiler_params=pltpu.CompilerParams(has_side_effects=DATAFLOW),
    )(*arrays, send_sems, recv_sems, after)
    return out[n:]


def _swap_with_sibling(arrays, *, name):
    n = len(arrays)

    def body(*refs):
        ins, outs = refs[:n], refs[n:2 * n]
        send_sems, recv_sems = refs[2 * n:]
        x, y, c = _mesh_pos()
        copies = []
        for a in range(n):
            cp = pltpu.make_async_remote_copy(
                src_ref=ins[a], dst_ref=outs[a], send_sem=send_sems.at[a], recv_sem=recv_sems.at[a],
                device_id=(x, y, 1 - c), device_id_type=MESH)
            cp.start()
            copies.append(cp)
        for cp in copies:
            cp.wait()

    return pl.pallas_call(
        body, name=name, in_specs=[ANY] * n, out_specs=[ANY] * n,
        out_shape=[jax.ShapeDtypeStruct(a.shape, a.dtype) for a in arrays],
        scratch_shapes=[pltpu.SemaphoreType.DMA((n,)), pltpu.SemaphoreType.DMA((n,))])(*arrays)


N_DEV = 8


def _all_reduce_small(vec, *, name):
    rows = vec.shape[0]

    def body(v_ref, o_ref, land, send_sems, recv_sems):
        x, y, c = _mesh_pos()
        me = 4 * x + 2 * y + c
        land[me] = v_ref[...]
        copies = []
        for k in range(1, N_DEV):
            fx, fy, fc = (k >> 2) & 1, (k >> 1) & 1, k & 1
            px, py, pc = x ^ fx, y ^ fy, c ^ fc
            send = pltpu.make_async_remote_copy(
                src_ref=v_ref, dst_ref=land.at[me], send_sem=send_sems.at[k - 1], recv_sem=recv_sems.at[k - 1],
                device_id=(px, py, pc), device_id_type=MESH)
            send.start()
            arrive = pltpu.make_async_remote_copy(
                src_ref=v_ref, dst_ref=land.at[4 * px + 2 * py + pc], send_sem=send_sems.at[k - 1], recv_sem=recv_sems.at[k - 1],
                device_id=(px, py, pc), device_id_type=MESH)
            copies.append((send, arrive))
        for send, arrive in copies:
            send.wait_send()
            arrive.wait_recv()
        acc = land[0]
        for k in range(1, N_DEV):
            acc = acc + land[k]
        o_ref[...] = acc

    return pl.pallas_call(
        body, name=name, in_specs=[VMEM_SPEC], out_specs=VMEM_SPEC,
        out_shape=jax.ShapeDtypeStruct(vec.shape, F32),
        scratch_shapes=[pltpu.VMEM((N_DEV, rows, LANES), F32), pltpu.SemaphoreType.DMA((N_DEV - 1,)),
                        pltpu.SemaphoreType.DMA((N_DEV - 1,))])(vec)


PACK_UNIT = SUBLANES * LANES * 2


def _padded(n):
    return -(-n // PACK_UNIT) * PACK_UNIT


def _pack(arrays, dtype, lead=0):
    parts = []
    for a in arrays:
        lead_shape = a.shape[:lead]
        flat = a.astype(dtype).reshape(lead_shape + (-1,))
        n = flat.shape[-1]
        flat = jnp.pad(flat, [(0, 0)] * lead + [(0, _padded(n) - n)])
        parts.append(flat.reshape(lead_shape + (-1, LANES)))
    return jnp.concatenate(parts, axis=lead)


def _unpack(buf, shapes, lead=0):
    out, row = [], 0
    for shp in shapes:
        n = math.prod(shp)
        rows = _padded(n) // LANES
        part = lax.slice_in_dim(buf, row, row + rows, axis=lead)
        lead_shape = part.shape[:lead]
        part = part.reshape(lead_shape + (-1,))
        part = lax.slice_in_dim(part, 0, n, axis=lead)
        out.append(part.reshape(lead_shape + tuple(shp)))
        row += rows
    return out


def kernel(x, positions, ln_mix_a, w_pool, b_pool, pool_scale, ln_ffn, w_gate, w_up, w_down, ln_kv, w_dkv, g_kv_latent, w_uk, w_uv, g_k, ln_mix_b, w_dq, g_q_latent, w_uq, g_q, w_o, loss_target, m_ln_mix_a, m_w_pool, m_b_pool, m_pool_scale, m_ln_ffn, m_w_gate, m_w_up, m_w_down, m_ln_kv, m_w_dkv, m_g_kv_latent, m_w_uk, m_w_uv, m_g_k, m_ln_mix_b, m_w_dq, m_g_q_latent, m_w_uq, m_g_q, m_w_o, v_ln_mix_a, v_w_pool, v_b_pool, v_pool_scale, v_ln_ffn, v_w_gate, v_w_up, v_w_down, v_ln_kv, v_w_dkv, v_g_kv_latent, v_w_uk, v_w_uv, v_g_k, v_ln_mix_b, v_w_dq, v_g_q_latent, v_w_uq, v_g_q, v_w_o):
    weights = dict(ln_mix_a=ln_mix_a, w_pool=w_pool, b_pool=b_pool, pool_scale=pool_scale, ln_ffn=ln_ffn, w_gate=w_gate,
                   w_up=w_up, w_down=w_down, ln_kv=ln_kv, w_dkv=w_dkv, g_kv_latent=g_kv_latent, w_uk=w_uk, w_uv=w_uv, g_k=g_k,
                   ln_mix_b=ln_mix_b, w_dq=w_dq, g_q_latent=g_q_latent, w_uq=w_uq, g_q=g_q, w_o=w_o)
    mom_m = dict(ln_mix_a=m_ln_mix_a, w_pool=m_w_pool, b_pool=m_b_pool, pool_scale=m_pool_scale, ln_ffn=m_ln_ffn,
                 w_gate=m_w_gate, w_up=m_w_up, w_down=m_w_down, ln_kv=m_ln_kv, w_dkv=m_w_dkv, g_kv_latent=m_g_kv_latent,
                 w_uk=m_w_uk, w_uv=m_w_uv, g_k=m_g_k, ln_mix_b=m_ln_mix_b, w_dq=m_w_dq, g_q_latent=m_g_q_latent,
                 w_uq=m_w_uq, g_q=m_g_q, w_o=m_w_o)
    mom_v = dict(ln_mix_a=v_ln_mix_a, w_pool=v_w_pool, b_pool=v_b_pool, pool_scale=v_pool_scale, ln_ffn=v_ln_ffn,
                 w_gate=v_w_gate, w_up=v_w_up, w_down=v_w_down, ln_kv=v_ln_kv, w_dkv=v_w_dkv, g_kv_latent=v_g_kv_latent,
                 w_uk=v_w_uk, w_uv=v_w_uv, g_k=v_g_k, ln_mix_b=v_ln_mix_b, w_dq=v_w_dq, g_q_latent=v_g_q_latent,
                 w_uq=v_w_uq, g_q=v_g_q, w_o=v_w_o)
    order = list(weights)
    s = x.shape[1]
    d = D_MODEL
    xs = x.reshape(s, d)
    target = loss_target.reshape(s, d)
    my_chip = 2 * lax.axis_index("x") + lax.axis_index("y")

    mat_names = ("w_pool", "w_dkv", "w_uk", "w_uv", "w_dq", "w_uq", "w_o")
    vec_names = ("ln_mix_a", "b_pool", "pool_scale")
    mat_shapes = [weights[n].shape for n in mat_names]
    vec_shapes = [weights[n].shape for n in vec_names]
    mats_local = _pack([weights[n] for n in mat_names], WIRE_DTYPE)
    vecs_local = _pack([weights[n] for n in vec_names], F32)

    def landing(shard):
        return lax.dynamic_update_slice_in_dim(lax.empty((N_SHARD,) + shard.shape, shard.dtype), shard[None], my_chip, axis=0)

    gathers = [Exchange("gather", (mats_local, vecs_local), (landing(mats_local), landing(vecs_local)))]
    for l in range(DEPTH):
        shards = tuple(w[l].astype(WIRE_DTYPE) for w in (w_gate, w_up, w_down))
        gathers.append(Exchange("gather", shards, tuple(landing(sh) for sh in shards)))
    gathering, _ = _exchange_start(gathers, name="gather_start")

    inv = ROPE_THETA ** (-jnp.arange(ROPE // 2, dtype=F32) * 2.0 / ROPE)
    inv_lanes = jnp.concatenate([inv, inv, jnp.zeros((LANES - ROPE,), F32)]).reshape(1, LANES)
    cos_t, sin_t = _rope_tables(positions.reshape(s, 1).astype(F32), inv_lanes, name="rope_tables")

    mats_all, vecs_all = _exchange_wait(gathering[0], cos_t, name="gather_wait_small")
    g_pool, g_dkv, g_uk, g_uv, g_dq, g_uq, g_o = _unpack(mats_all, mat_shapes, lead=1)
    g_lna, g_bp, g_ps = _unpack(vecs_all, vec_shapes, lead=1)

    wpool_f = g_pool.transpose(1, 2, 0, 3, 4).reshape(N_A, N_GROUPS, GROUP_DIM, GROUP_DIM)
    bpool_f = g_bp.transpose(1, 2, 0, 3).reshape(N_A, 1, d)
    pscale_f = g_ps.transpose(1, 0, 2).reshape(N_A, 1, d)
    lna_f = g_lna.transpose(1, 0, 2).reshape(N_A, 1, d)
    wdkv_f = jnp.pad(g_dkv.reshape(d, KV_LORA + ROPE), ((0, 0), (0, CKV_PAD - KV_LORA - ROPE)))
    wuk_f = g_uk.transpose(1, 0, 2).reshape(KV_LORA, N_HEADS * NOPE)
    wuv_f = g_uv.transpose(1, 0, 2).reshape(KV_LORA, N_HEADS * V_DIM)
    wdq_f = g_dq.transpose(1, 0, 2, 3).reshape(N_B, d, Q_LORA)
    wuq_f = jnp.pad(g_uq.transpose(1, 2, 0, 3).reshape(N_B, Q_LORA, N_HEADS, QK_DIM),
                    ((0, 0), (0, 0), (0, 0), (0, HEAD_PAD - QK_DIM))).reshape(N_B, Q_LORA, N_HEADS * HEAD_PAD)
    wo_f = g_o.transpose(1, 0, 2, 3).reshape(N_B, d, d)

    def head_gain(g):
        return jnp.pad(g.reshape(1, QK_DIM), ((0, 0), (0, HEAD_PAD - QK_DIM)))

    ffn_w = [None] * DEPTH

    def ffn_fwd(xin, layer):
        hf = _rms_fwd(xin, ln_ffn[layer].reshape(1, d), n=d, name="ffn_norm")
        ffn_w[layer] = wg, wu, wd = _exchange_wait(gathering[1 + layer], hf, name=f"gather_wait_{layer}")
        a, b, u = _ffn_up(hf, wg, wu, name="ffn_up")
        return _ffn_down(u, wd, xin, name="ffn_down"), (xin, hf, a, b, u)

    saved_a, saved_b, saved_f = [], [], []
    cur = xs
    for l in range(N_A):
        dpool = _rms_pool_fwd(cur, lna_f[l], name="pool_fwd")
        x1 = _pool_mm_fwd(dpool, wpool_f[l], bpool_f[l], pscale_f[l], cur, name="pool_mm")
        saved_a.append((cur, dpool))
        cur, sf = ffn_fwd(x1, l)
        saved_f.append(sf)

    x_kv = cur
    hk = _rms_fwd(x_kv, ln_kv.reshape(1, d), n=d, name="kv_norm")
    ckv = _mm(hk, wdkv_f, name="kv_down")
    c_lat = _rms_fwd(ckv, g_kv_latent.reshape(1, KV_LORA), n=KV_LORA, name="kv_latent_norm")
    kn_raw = _mm(c_lat, wuk_f, name="k_up")
    v_all = _mm(c_lat, wuv_f, out_dtype=MXU_DTYPE, name="v_up")
    vt_all = _mm(wuv_f.T, c_lat, tb=True, out_dtype=MXU_DTYPE, name="v_up_t")
    k_raw = _k_assemble(kn_raw, ckv, name="k_assemble")
    gk_pad = head_gain(g_k)
    k_cat = _head_norm_rope_fwd(k_raw, gk_pad, cos_t, sin_t, name="k_norm_rope")

    for j in range(N_B):
        l = N_A + j
        hq = _rms_fwd(cur, ln_mix_b[j].reshape(1, d), n=d, name="q_norm")
        cq_raw = _mm(hq, wdq_f[j], name="q_down")
        cq = _rms_fwd(cq_raw, g_q_latent[j].reshape(1, Q_LORA), n=Q_LORA, name="q_latent_norm")
        q_raw = _mm(cq, wuq_f[j], name="q_up")
        gq_pad = head_gain(g_q[j])
        q_cat = _head_norm_rope_fwd(q_raw, gq_pad, cos_t, sin_t, name="q_norm_rope")
        ot, lse = _attn_fwd(q_cat, k_cat, vt_all, name="attn_fwd")
        x1 = _mm(ot, wo_f[j], ta=True, resid=cur, name="attn_out")
        saved_b.append((cur, hq, cq_raw, cq, q_raw, gq_pad, q_cat, ot, lse))
        cur, sf = ffn_fwd(x1, l)
        saved_f.append(sf)

    dy, loss_part = _loss_head(cur, target, name="loss_head")

    ffn_landed = (lax.empty((N_SHARD, DEPTH, d, FF_SHARD), WIRE_DTYPE), lax.empty((N_SHARD, DEPTH, d, FF_SHARD), WIRE_DTYPE),
                  lax.empty((N_SHARD, DEPTH, FF_SHARD, d), WIRE_DTYPE))
    scattering = None
    grads = {}
    d_ln_ffn = [None] * DEPTH

    def own_part(full):
        return lax.dynamic_index_in_dim(full, my_chip, axis=0, keepdims=True)

    def ffn_bwd(dyv, layer):
        nonlocal ffn_landed, scattering
        xin, hf, a, b, u = saved_f[layer]
        wg, wu, wd = ffn_w[layer]
        da, db = _ffn_bwd_hidden(dyv, wd, a, b, name="ffn_bwd_hidden")
        dwd = _ffn_bwd_dwd(u, dyv, name="ffn_bwd_dwd")
        dwg, dwu = _ffn_bwd_dwgu(hf, da, db, name="ffn_bwd_dwgu")
        if scattering is not None:
            ffn_landed = _exchange_wait(scattering, dwg, name=f"scatter_wait_{layer + 1}")
        ffn_landed = tuple(lax.dynamic_update_slice(buf, own_part(g)[:, None], (my_chip, layer, 0, 0))
                           for buf, g in zip(ffn_landed, (dwg, dwu, dwd)))
        (scattering,), started = _exchange_start([Exchange("scatter", (dwg, dwu, dwd), ffn_landed, layer)],
                                                 name=f"scatter_start_{layer}")
        dhf = _ffn_bwd_dh(da, db, wg, wu, name="ffn_bwd_dh")
        dx, dg = _rms_bwd(xin, ln_ffn[layer].reshape(1, d), dhf, n=d, dx_in=dyv, after=started, name="ffn_norm_bwd")
        d_ln_ffn[layer] = dg.sum(axis=0)
        return dx

    dk_acc = dv_acc = None
    d_ln_mix_b, d_w_dq, d_g_q_latent, d_w_uq, d_g_q, d_w_o = ([None] * N_B for _ in range(6))
    dcur = dy
    for j in reversed(range(N_B)):
        l = N_A + j
        xin, hq, cq_raw, cq, q_raw, gq_pad, q_cat, ot, lse = saved_b[j]
        dx1 = ffn_bwd(dcur, l)
        do = _mm(dx1, wo_f[j], tb=True, out_dtype=MXU_DTYPE, name="attn_out_bwd")
        dot = _mm(wo_f[j], dx1, tb=True, name="attn_out_bwd_t")
        d_w_o[j] = _mm_tn(ot, dx1, at=True, name="attn_out_dw")
        delta = _attn_delta(ot, dot, name="attn_delta")
        lse_col, delta_col = lse.reshape(N_HEADS, s, 1), delta.reshape(N_HEADS, s, 1)
        dq_cat = _attn_bwd_dq(q_cat, k_cat, v_all, do, lse_col, delta_col, name="attn_bwd_dq")
        dk_acc, dv_acc = _attn_bwd_dkv(q_cat, k_cat, v_all, do, lse, delta, dk_acc, dv_acc, name="attn_bwd_dkv")
        dq_raw, dgq = _head_norm_rope_bwd(q_raw, gq_pad, cos_t, sin_t, dq_cat, name="q_norm_rope_bwd")
        d_g_q[j] = dgq.sum(axis=0)[:QK_DIM]
        dcq = _mm(dq_raw, wuq_f[j], tb=True, name="q_up_bwd")
        d_w_uq[j] = _mm_tn(cq, dq_raw, name="q_up_dw").reshape(Q_LORA, N_HEADS, HEAD_PAD)[:, :, :QK_DIM].reshape(Q_LORA, N_HEADS * QK_DIM)
        dcq_raw, dgl = _rms_bwd(cq_raw, g_q_latent[j].reshape(1, Q_LORA), dcq, n=Q_LORA, name="q_latent_norm_bwd")
        d_g_q_latent[j] = dgl.sum(axis=0)
        dhq = _mm(dcq_raw, wdq_f[j], tb=True, name="q_down_bwd")
        d_w_dq[j] = _mm_tn(hq, dcq_raw, name="q_down_dw")
        dcur, dgm = _rms_bwd(xin, ln_mix_b[j].reshape(1, d), dhq, n=d, dx_in=dx1, name="q_norm_bwd")
        d_ln_mix_b[j] = dgm.sum(axis=0)

    dk_raw, dgk = _head_norm_rope_bwd(k_raw, gk_pad, cos_t, sin_t, dk_acc, name="k_norm_rope_bwd")
    grads["g_k"] = dgk.sum(axis=0)[:QK_DIM]
    dc = _mm(dv_acc, wuv_f, tb=True, name="v_up_bwd")
    grads["w_uv"] = _mm_tn(c_lat, dv_acc, name="v_up_dw")
    dkn, dpe = _k_disassemble(dk_raw, name="k_disassemble")
    dc = _mm(dkn, wuk_f, tb=True, resid=dc, name="k_up_bwd")
    grads["w_uk"] = _mm_tn(c_lat, dkn, name="k_up_dw")
    dc_raw, dgl = _rms_bwd(ckv, g_kv_latent.reshape(1, KV_LORA), dc, n=KV_LORA, name="kv_latent_norm_bwd")
    grads["g_kv_latent"] = dgl.sum(axis=0)
    dckv = jnp.concatenate([dc_raw, dpe], axis=1)
    dhk = _mm(dckv, wdkv_f, tb=True, name="kv_down_bwd")
    grads["w_dkv"] = _mm_tn(hk, dckv, name="kv_down_dw")[:, :KV_LORA + ROPE]
    dcur, dg = _rms_bwd(x_kv, ln_kv.reshape(1, d), dhk, n=d, dx_in=dcur, name="kv_norm_bwd")
    grads["ln_kv"] = dg.sum(axis=0)

    d_ln_mix_a, d_w_pool, d_b_pool, d_pool_scale = ([None] * N_A for _ in range(4))
    for l in reversed(range(N_A)):
        xin, dpool = saved_a[l]
        dx1 = ffn_bwd(dcur, l)
        dd, dwp, dbp, dsp = _pool_mm_bwd(dpool, wpool_f[l], bpool_f[l], pscale_f[l], dx1, name="pool_mm_bwd")
        d_w_pool[l], d_b_pool[l], d_pool_scale[l] = dwp, dbp.sum(axis=0), dsp.sum(axis=0)
        dcur, dg = _rms_pool_bwd(xin, lna_f[l], dd, dx1, name="pool_bwd")
        d_ln_mix_a[l] = dg.sum(axis=0)
    grad_x = dcur.reshape(1, s, d)

    gm = {
        "w_pool": jnp.stack(d_w_pool).reshape(N_A, N_GROUPS, N_SHARD, GROUP_DIM // N_SHARD, GROUP_DIM).transpose(2, 0, 1, 3, 4),
        "w_dkv": grads["w_dkv"].reshape(N_SHARD, d // N_SHARD, KV_LORA + ROPE),
        "w_uk": grads["w_uk"].reshape(KV_LORA, N_SHARD, -1).transpose(1, 0, 2),
        "w_uv": grads["w_uv"].reshape(KV_LORA, N_SHARD, -1).transpose(1, 0, 2),
        "w_dq": jnp.stack(d_w_dq).reshape(N_B, N_SHARD, d // N_SHARD, Q_LORA).transpose(1, 0, 2, 3),
        "w_uq": jnp.stack(d_w_uq).reshape(N_B, Q_LORA, N_SHARD, -1).transpose(2, 0, 1, 3),
        "w_o": jnp.stack(d_w_o).reshape(N_B, N_SHARD, d // N_SHARD, d).transpose(1, 0, 2, 3),
    }
    mats_grad = _pack([gm[n] for n in mat_names], WIRE_DTYPE, lead=1)
    mats_landing = lax.dynamic_update_slice_in_dim(lax.empty(mats_grad.shape, WIRE_DTYPE), own_part(mats_grad), my_chip, axis=0)
    (mats_scatter,), _ = _exchange_start([Exchange("scatter", (mats_grad,), (mats_landing,))], name="scatter_start_small")
    ffn_landed = _exchange_wait(scattering, dcur, name="scatter_wait_0")
    (mats_landed,) = _exchange_wait(mats_scatter, ffn_landed[0], name="scatter_wait_small")
    landed = [ffn_landed[0].reshape(N_SHARD, DEPTH * d, FF_SHARD), ffn_landed[1].reshape(N_SHARD, DEPTH * d, FF_SHARD),
              ffn_landed[2].reshape(N_SHARD, DEPTH * FF_SHARD, d), mats_landed]
    chip_sums = [_sum_slots(p, name="sum_chips") for p in landed]
    sib_sums = _swap_with_sibling(chip_sums, name="swap_sibling")

    vec_full = {
        "ln_mix_a": jnp.stack(d_ln_mix_a), "b_pool": jnp.stack(d_b_pool).reshape(N_A, N_GROUPS, GROUP_DIM),
        "pool_scale": jnp.stack(d_pool_scale), "ln_ffn": jnp.stack(d_ln_ffn), "ln_kv": grads["ln_kv"],
        "g_kv_latent": grads["g_kv_latent"], "g_k": grads["g_k"], "ln_mix_b": jnp.stack(d_ln_mix_b),
        "g_q_latent": jnp.stack(d_g_q_latent), "g_q": jnp.stack(d_g_q),
    }
    small_names = list(vec_full)
    small_shapes = [vec_full[n].shape for n in small_names] + [(SUBLANES * LANES,)]
    small = _all_reduce_small(_pack([vec_full[n] for n in small_names] + [loss_part.reshape(-1)], F32), name="all_reduce_small")
    small_sum = _unpack(small, small_shapes)
    loss = jnp.sum(small_sum[-1])
    vec_grad = dict(zip(small_names, small_sum[:-1]))
    vec_grad["ln_mix_a"] = lax.dynamic_slice_in_dim(vec_grad["ln_mix_a"], my_chip * (d // N_SHARD), d // N_SHARD, axis=1)
    vec_grad["pool_scale"] = lax.dynamic_slice_in_dim(vec_grad["pool_scale"], my_chip * (d // N_SHARD), d // N_SHARD, axis=1)
    vec_grad["b_pool"] = lax.dynamic_slice_in_dim(vec_grad["b_pool"], my_chip * (GROUP_DIM // N_SHARD), GROUP_DIM // N_SHARD, axis=2)

    out_g, out_d, out_m, out_v = {}, {}, {}, {}
    for idx, (nm, rows, cols) in enumerate((("w_gate", DEPTH * d, FF_SHARD), ("w_up", DEPTH * d, FF_SHARD), ("w_down", DEPTH * FF_SHARD, d))):
        res = _adamw(weights[nm].reshape(rows, cols), mom_m[nm].reshape(rows, cols), mom_v[nm].reshape(rows, cols),
                     [chip_sums[idx], sib_sums[idx]], name="adamw_ffn")
        shp = weights[nm].shape
        out_g[nm], out_d[nm], out_m[nm], out_v[nm] = (r.reshape(shp) for r in res)

    rest = list(mat_names) + small_names
    rest_shapes = [weights[n].shape for n in rest]
    vec_rows = _pack([vec_grad[n].reshape(weights[n].shape) for n in small_names], F32)
    fill = jnp.zeros(((-(chip_sums[3].shape[0] + vec_rows.shape[0])) % ADAMW_ROWS, LANES), F32)
    g_own = jnp.concatenate([chip_sums[3], vec_rows, fill], axis=0)
    g_sib = jnp.concatenate([sib_sums[3], jnp.zeros_like(vec_rows), fill], axis=0)
    res = _adamw(*[jnp.concatenate([_pack([src[n] for n in rest], F32), fill], axis=0) for src in (weights, mom_m, mom_v)],
                 [g_own, g_sib], name="adamw_small")
    for tgt, buf in zip((out_g, out_d, out_m, out_v), res):
        for n, arr in zip(rest, _unpack(buf, rest_shapes)):
            tgt[n] = arr

    return (loss, grad_x, *[out_g[n] for n in order], *[out_d[n] for n in order],
            *[out_m[n] for n in order], *[out_v[n] for n in order])
```

```python
import math
from typing import Any, NamedTuple

import jax
import jax.numpy as jnp
from jax import lax
from jax.experimental import pallas as pl
from jax.experimental.pallas import tpu as pltpu

F32 = jnp.float32
BF16 = jnp.bfloat16
MXU_DTYPE = BF16
WIRE_DTYPE = BF16
SAVED_DTYPE = BF16

D_MODEL = 1024
N_A = 2
N_B = 2
DEPTH = 4
POOL_WINDOWS = (2, 4, 8, 16)
N_GROUPS = 4
GROUP_DIM = 256
POOL_HALO = 16
N_HEADS = 8
NOPE = 128
ROPE = 64
QK_DIM = 192
HEAD_PAD = 256
V_DIM = 128
Q_LORA = 256
KV_LORA = 512
CKV_PAD = 640
ROPE_THETA = 10000.0
CHUNK = 64
EPS = 1e-6
N_SHARD = 4
FF_SHARD = 704
FFN_ROWS = 1024
FFN_GRAD_ROWS = 2048
LANES = 128
SUBLANES = 8
ADAM_LR, ADAM_B1, ADAM_B2, ADAM_EPS, ADAM_WD, ADAM_STEP = 0.001, 0.9, 0.999, 1e-08, 0.01, 10
MESH = pl.DeviceIdType.MESH
ANY = pl.BlockSpec(memory_space=pl.ANY)
VMEM_SPEC = pl.BlockSpec(memory_space=pltpu.VMEM)


def _tile(n, pref):
    if n <= pref:
        return n
    t = pref - pref % SUBLANES
    while n % t:
        t -= SUBLANES
    return t


def _fold8(v):
    r, n = v.shape
    return v.reshape(r // SUBLANES, SUBLANES, n).sum(axis=0)


def _dot(a, b, dims):
    return lax.dot_general(a.astype(MXU_DTYPE), b.astype(MXU_DTYPE), (dims, ((), ())),
                           preferred_element_type=F32)


def _nn(a, b):
    return _dot(a, b, ((1,), (0,)))


def _nt(a, b):
    return _dot(a, b, ((1,), (1,)))


def _tn(a, b):
    return _dot(a, b, ((0,), (0,)))


def _mm(a, b, *, ta=False, tb=False, resid=None, out_dtype=F32, name):
    assert not (ta and tb)
    m, k = (a.shape[1], a.shape[0]) if ta else a.shape
    n = b.shape[0] if tb else b.shape[1]
    tm, tn = _tile(m, 512), _tile(n, 1024)

    def body(*refs):
        if resid is None:
            a_ref, b_ref, o_ref = refs
        else:
            a_ref, b_ref, r_ref, o_ref = refs
        acc = (_tn if ta else _nt if tb else _nn)(a_ref[...], b_ref[...])
        if resid is not None:
            acc = r_ref[...] + acc
        o_ref[...] = acc.astype(o_ref.dtype)

    in_specs = [pl.BlockSpec((k, tm), lambda i, j: (0, i)) if ta else pl.BlockSpec((tm, k), lambda i, j: (i, 0)),
                pl.BlockSpec((tn, k), lambda i, j: (j, 0)) if tb else pl.BlockSpec((k, tn), lambda i, j: (0, j))]
    args = [a, b]
    if resid is not None:
        in_specs.append(pl.BlockSpec((tm, tn), lambda i, j: (i, j)))
        args.append(resid)
    return pl.pallas_call(
        body, name=name, grid=(m // tm, n // tn), in_specs=in_specs,
        out_specs=pl.BlockSpec((tm, tn), lambda i, j: (i, j)),
        out_shape=jax.ShapeDtypeStruct((m, n), out_dtype))(*args)


def _mm_tn(a, b, *, name, at=False, out_dtype=F32):
    m = b.shape[0]
    k1 = a.shape[0] if at else a.shape[1]
    n = b.shape[1]
    tm, tn = _tile(m, 512), _tile(n, 1024)
    nm = m // tm

    def body(a_ref, b_ref, o_ref, acc):
        i = pl.program_id(1)

        @pl.when(i == 0)
        def _():
            acc[...] = jnp.zeros_like(acc)

        acc[...] += (_nn if at else _tn)(a_ref[...], b_ref[...])

        @pl.when(i == nm - 1)
        def _():
            o_ref[...] = acc[...].astype(o_ref.dtype)

    return pl.pallas_call(
        body, name=name, grid=(n // tn, nm),
        in_specs=[pl.BlockSpec((k1, tm), lambda j, i: (0, i)) if at else pl.BlockSpec((tm, k1), lambda j, i: (i, 0)),
                  pl.BlockSpec((tm, tn), lambda j, i: (i, j))],
        out_specs=pl.BlockSpec((k1, tn), lambda j, i: (0, j)),
        out_shape=jax.ShapeDtypeStruct((k1, n), out_dtype),
        scratch_shapes=[pltpu.VMEM((k1, tn), F32)])(a, b)


def _rms_fwd(x, g, *, n, n_valid=None, name):
    out_dtype = MXU_DTYPE
    rows = x.shape[0]
    tm = _tile(rows, 512)
    inv_n = 1.0 / (n_valid or n)

    def body(x_ref, g_ref, o_ref):
        xv = x_ref[...]
        r = lax.rsqrt(jnp.sum(xv * xv, axis=-1, keepdims=True) * inv_n + EPS)
        o_ref[...] = (xv * r * g_ref[...]).astype(o_ref.dtype)

    return pl.pallas_call(
        body, name=name, grid=(rows // tm,),
        in_specs=[pl.BlockSpec((tm, n), lambda i: (i, 0)), pl.BlockSpec((1, n), lambda i: (0, 0))],
        out_specs=pl.BlockSpec((tm, n), lambda i: (i, 0)),
        out_shape=jax.ShapeDtypeStruct((rows, n), out_dtype))(x, g)


def _rms_bwd_math(xv, gv, dyv, inv_n):
    r = lax.rsqrt(jnp.sum(xv * xv, axis=-1, keepdims=True) * inv_n + EPS)
    xh = xv * r
    gy = dyv * gv
    dx = r * (gy - xh * (jnp.sum(gy * xh, axis=-1, keepdims=True) * inv_n))
    return dx, dyv * xh


def _rms_bwd(x, g, dy, *, n, dx_in=None, after=None, name):
    rows = x.shape[0]
    tm = _tile(rows, 512)
    inv_n = 1.0 / n

    def body(*refs):
        if after is not None:
            refs = refs[:-3] + refs[-2:]
        if dx_in is None:
            x_ref, g_ref, dy_ref, dx_ref, dg_ref = refs
        else:
            x_ref, g_ref, dy_ref, din_ref, dx_ref, dg_ref = refs
        dx, dgc = _rms_bwd_math(x_ref[...], g_ref[...], dy_ref[...], inv_n)
        if dx_in is not None:
            dx = din_ref[...] + dx
        dx_ref[...] = dx

        @pl.when(pl.program_id(0) == 0)
        def _():
            dg_ref[...] = jnp.zeros_like(dg_ref)

        dg_ref[...] += _fold8(dgc)

    row_spec = pl.BlockSpec((tm, n), lambda i: (i, 0))
    in_specs = [row_spec, pl.BlockSpec((1, n), lambda i: (0, 0)), row_spec]
    args = [x, g, dy]
    if dx_in is not None:
        in_specs.append(row_spec)
        args.append(dx_in)
    if after is not None:
        in_specs.append(ANY)
        args.append(after)
    return pl.pallas_call(
        body, name=name, grid=(rows // tm,), in_specs=in_specs,
        out_specs=[row_spec, pl.BlockSpec((SUBLANES, n), lambda i: (0, 0))],
        out_shape=[jax.ShapeDtypeStruct((rows, n), F32), jax.ShapeDtypeStruct((SUBLANES, n), F32)])(*args)


def _pool_counts(t0, tm, w):
    t = t0 + lax.broadcasted_iota(jnp.int32, (tm, 1), 0)
    return jnp.minimum(t + 1, w).astype(F32)


def _rms_pool_fwd(x, g, *, name):
    s, d = x.shape
    tm = _tile(s, 512)
    hb = tm // POOL_HALO

    def body(x_ref, halo_ref, g_ref, o_ref):
        i = pl.program_id(0)
        gv = g_ref[...]

        def norm(v):
            return v * lax.rsqrt(jnp.mean(v * v, axis=-1, keepdims=True) + EPS) * gv

        h = norm(x_ref[...])
        halo = norm(halo_ref[...]) * (i > 0).astype(F32)
        hh = jnp.concatenate([halo, h], axis=0)
        rows = tm + POOL_HALO
        for gi, w in enumerate(POOL_WINDOWS):
            cols = slice(gi * GROUP_DIM, (gi + 1) * GROUP_DIM)
            acc = hh[:, cols]
            k = 1
            while k < w:
                acc = acc + pltpu.roll(acc, k, 0)
                k *= 2
            win = acc[POOL_HALO:rows]
            o_ref[:, cols] = (win / _pool_counts(i * tm, tm, w) - h[:, cols]).astype(o_ref.dtype)

    return pl.pallas_call(
        body, name=name, grid=(s // tm,),
        in_specs=[pl.BlockSpec((tm, d), lambda i: (i, 0)),
                  pl.BlockSpec((POOL_HALO, d), lambda i: (jnp.maximum(i * hb - 1, 0), 0)),
                  pl.BlockSpec((1, d), lambda i: (0, 0))],
        out_specs=pl.BlockSpec((tm, d), lambda i: (i, 0)),
        out_shape=jax.ShapeDtypeStruct((s, d), MXU_DTYPE))(x, x, g)


def _rms_pool_bwd(x, g, dd, dx_in, *, name):
    s, d = x.shape
    tm = _tile(s, 512)
    hb = tm // POOL_HALO
    nt = s // tm

    def body(x_ref, g_ref, dd_ref, halo_ref, din_ref, dx_ref, dg_ref):
        i = pl.program_id(0)
        ddv = dd_ref[...]
        halo = halo_ref[...] * (i < nt - 1).astype(F32)
        rows = tm + POOL_HALO
        parts = []
        for gi, w in enumerate(POOL_WINDOWS):
            cols = slice(gi * GROUP_DIM, (gi + 1) * GROUP_DIM)
            acc = jnp.concatenate([ddv[:, cols] / _pool_counts(i * tm, tm, w), halo[:, cols] * (1.0 / w)], axis=0)
            k = 1
            while k < w:
                acc = acc + pltpu.roll(acc, rows - k, 0)
                k *= 2
            parts.append(acc[0:tm] - ddv[:, cols])
        dh = jnp.concatenate(parts, axis=1)
        dx, dgc = _rms_bwd_math(x_ref[...], g_ref[...], dh, 1.0 / d)
        dx_ref[...] = din_ref[...] + dx

        @pl.when(i == 0)
        def _():
            dg_ref[...] = jnp.zeros_like(dg_ref)

        dg_ref[...] += _fold8(dgc)

    row_spec = pl.BlockSpec((tm, d), lambda i: (i, 0))
    return pl.pallas_call(
        body, name=name, grid=(nt,),
        in_specs=[row_spec, pl.BlockSpec((1, d), lambda i: (0, 0)), row_spec,
                  pl.BlockSpec((POOL_HALO, d), lambda i: (jnp.minimum((i + 1) * hb, s // POOL_HALO - 1), 0)),
                  row_spec],
        out_specs=[row_spec, pl.BlockSpec((SUBLANES, d), lambda i: (0, 0))],
        out_shape=[jax.ShapeDtypeStruct((s, d), F32), jax.ShapeDtypeStruct((SUBLANES, d), F32)])(x, g, dd, dd, dx_in)


def _pool_mm_fwd(dpool, w, b, scale, x, *, name):
    s, d = x.shape
    tm = _tile(s, 512)

    def body(d_ref, w_ref, b_ref, s_ref, x_ref, o_ref):
        for gi in range(N_GROUPS):
            cols = slice(gi * GROUP_DIM, (gi + 1) * GROUP_DIM)
            y = _nn(d_ref[:, cols], w_ref[gi]) + b_ref[:, cols]
            o_ref[:, cols] = x_ref[:, cols] + y * s_ref[:, cols]

    row_spec = pl.BlockSpec((tm, d), lambda i: (i, 0))
    vec_spec = pl.BlockSpec((1, d), lambda i: (0, 0))
    return pl.pallas_call(
        body, name=name, grid=(s // tm,),
        in_specs=[row_spec, pl.BlockSpec((N_GROUPS, GROUP_DIM, GROUP_DIM), lambda i: (0, 0, 0)), vec_spec, vec_spec, row_spec],
        out_specs=row_spec, out_shape=jax.ShapeDtypeStruct((s, d), F32))(dpool, w, b, scale, x)


def _pool_mm_bwd(dpool, w, b, scale, dx, *, name):
    s, d = dx.shape
    tm = _tile(s, 512)

    def body(d_ref, w_ref, b_ref, s_ref, dx_ref, dd_ref, dw_ref, db_ref, ds_ref):
        @pl.when(pl.program_id(0) == 0)
        def _():
            dw_ref[...] = jnp.zeros_like(dw_ref)
            db_ref[...] = jnp.zeros_like(db_ref)
            ds_ref[...] = jnp.zeros_like(ds_ref)

        for gi in range(N_GROUPS):
            cols = slice(gi * GROUP_DIM, (gi + 1) * GROUP_DIM)
            dg = d_ref[:, cols]
            y = _nn(dg, w_ref[gi]) + b_ref[:, cols]
            dxg = dx_ref[:, cols]
            dy = dxg * s_ref[:, cols]
            ds_ref[:, cols] += _fold8(dxg * y)
            db_ref[:, cols] += _fold8(dy)
            dw_ref[gi] += _tn(dg, dy)
            dd_ref[:, cols] = _nt(dy, w_ref[gi])

    row_spec = pl.BlockSpec((tm, d), lambda i: (i, 0))
    vec_spec = pl.BlockSpec((1, d), lambda i: (0, 0))
    w_spec = pl.BlockSpec((N_GROUPS, GROUP_DIM, GROUP_DIM), lambda i: (0, 0, 0))
    part_spec = pl.BlockSpec((SUBLANES, d), lambda i: (0, 0))
    return pl.pallas_call(
        body, name=name, grid=(s // tm,),
        in_specs=[row_spec, w_spec, vec_spec, vec_spec, row_spec],
        out_specs=[row_spec, w_spec, part_spec, part_spec],
        out_shape=[jax.ShapeDtypeStruct((s, d), F32), jax.ShapeDtypeStruct((N_GROUPS, GROUP_DIM, GROUP_DIM), F32),
                   jax.ShapeDtypeStruct((SUBLANES, d), F32), jax.ShapeDtypeStruct((SUBLANES, d), F32)])(dpool, w, b, scale, dx)


def _ffn_up(hf, wg, wu, *, name):
    s, d = hf.shape
    tm = _tile(s, FFN_ROWS)

    def body(h_ref, wg_ref, wu_ref, a_ref, b_ref, u_ref):
        hv = h_ref[...]
        a = _nn(hv, wg_ref[...])
        b = _nn(hv, wu_ref[...])
        a_ref[...] = a.astype(a_ref.dtype)
        b_ref[...] = b.astype(b_ref.dtype)
        u_ref[...] = (a * (1.0 / (1.0 + jnp.exp(-a))) * b).astype(u_ref.dtype)

    w_spec = pl.BlockSpec((None, d, FF_SHARD), lambda j, i: (j, 0, 0))
    h_spec = pl.BlockSpec((None, tm, FF_SHARD), lambda j, i: (j, i, 0))
    hid = (N_SHARD, s, FF_SHARD)
    return pl.pallas_call(
        body, name=name, grid=(N_SHARD, s // tm),
        in_specs=[pl.BlockSpec((tm, d), lambda j, i: (i, 0)), w_spec, w_spec],
        out_specs=[h_spec, h_spec, h_spec],
        out_shape=[jax.ShapeDtypeStruct(hid, SAVED_DTYPE), jax.ShapeDtypeStruct(hid, SAVED_DTYPE),
                   jax.ShapeDtypeStruct(hid, MXU_DTYPE)])(hf, wg, wu)


def _ffn_down(u, wd, x, *, name):
    s, d = x.shape
    tm = _tile(s, 1024)

    def body(u_ref, w_ref, x_ref, o_ref):
        j = pl.program_id(1)

        @pl.when(j == 0)
        def _():
            o_ref[...] = x_ref[...]

        o_ref[...] += _nn(u_ref[...], w_ref[...])

    return pl.pallas_call(
        body, name=name, grid=(s // tm, N_SHARD),
        in_specs=[pl.BlockSpec((None, tm, FF_SHARD), lambda i, j: (j, i, 0)),
                  pl.BlockSpec((None, FF_SHARD, d), lambda i, j: (j, 0, 0)),
                  pl.BlockSpec((tm, d), lambda i, j: (i, 0))],
        out_specs=pl.BlockSpec((tm, d), lambda i, j: (i, 0)),
        out_shape=jax.ShapeDtypeStruct((s, d), F32))(u, wd, x)


def _ffn_bwd_hidden(dy, wd, a, b, *, name):
    s, d = dy.shape
    tm = _tile(s, FFN_ROWS)

    def body(dy_ref, w_ref, a_ref, b_ref, da_ref, db_ref):
        du = _nt(dy_ref[...], w_ref[...])
        av, bv = a_ref[...].astype(F32), b_ref[...].astype(F32)
        sg = 1.0 / (1.0 + jnp.exp(-av))
        da_ref[...] = (du * bv * (sg * (1.0 + av * (1.0 - sg)))).astype(da_ref.dtype)
        db_ref[...] = (du * (av * sg)).astype(db_ref.dtype)

    h_spec = pl.BlockSpec((None, tm, FF_SHARD), lambda j, i: (j, i, 0))
    hid = jax.ShapeDtypeStruct((N_SHARD, s, FF_SHARD), MXU_DTYPE)
    return pl.pallas_call(
        body, name=name, grid=(N_SHARD, s // tm),
        in_specs=[pl.BlockSpec((tm, d), lambda j, i: (i, 0)),
                  pl.BlockSpec((None, FF_SHARD, d), lambda j, i: (j, 0, 0)), h_spec, h_spec],
        out_specs=[h_spec, h_spec], out_shape=[hid, hid])(dy, wd, a, b)


def _ffn_bwd_dwd(u, dy, *, name):
    s, d = dy.shape
    tm = _tile(s, FFN_GRAD_ROWS)
    nm = s // tm

    def body(u_ref, dy_ref, o_ref, acc):
        i = pl.program_id(1)

        @pl.when(i == 0)
        def _():
            acc[...] = jnp.zeros_like(acc)

        acc[...] += _tn(u_ref[...], dy_ref[...])

        @pl.when(i == nm - 1)
        def _():
            o_ref[...] = acc[...].astype(o_ref.dtype)

    return pl.pallas_call(
        body, name=name, grid=(N_SHARD, nm),
        in_specs=[pl.BlockSpec((None, tm, FF_SHARD), lambda j, i: (j, i, 0)), pl.BlockSpec((tm, d), lambda j, i: (i, 0))],
        out_specs=pl.BlockSpec((None, FF_SHARD, d), lambda j, i: (j, 0, 0)),
        out_shape=jax.ShapeDtypeStruct((N_SHARD, FF_SHARD, d), WIRE_DTYPE),
        scratch_shapes=[pltpu.VMEM((FF_SHARD, d), F32)])(u, dy)


def _ffn_bwd_dwgu(hf, da, db, *, name):
    s, d = hf.shape
    tm = _tile(s, FFN_GRAD_ROWS)
    nm = s // tm

    def body(h_ref, da_ref, db_ref, og_ref, ou_ref, accg, accu):
        i = pl.program_id(1)

        @pl.when(i == 0)
        def _():
            accg[...] = jnp.zeros_like(accg)
            accu[...] = jnp.zeros_like(accu)

        hv = h_ref[...]
        accg[...] += _tn(hv, da_ref[...])
        accu[...] += _tn(hv, db_ref[...])

        @pl.when(i == nm - 1)
        def _():
            og_ref[...] = accg[...].astype(og_ref.dtype)
            ou_ref[...] = accu[...].astype(ou_ref.dtype)

    h_spec = pl.BlockSpec((None, tm, FF_SHARD), lambda j, i: (j, i, 0))
    w_spec = pl.BlockSpec((None, d, FF_SHARD), lambda j, i: (j, 0, 0))
    grad = jax.ShapeDtypeStruct((N_SHARD, d, FF_SHARD), WIRE_DTYPE)
    return pl.pallas_call(
        body, name=name, grid=(N_SHARD, nm),
        in_specs=[pl.BlockSpec((tm, d), lambda j, i: (i, 0)), h_spec, h_spec],
        out_specs=[w_spec, w_spec], out_shape=[grad, grad],
        scratch_shapes=[pltpu.VMEM((d, FF_SHARD), F32), pltpu.VMEM((d, FF_SHARD), F32)])(hf, da, db)


def _ffn_bwd_dh(da, db, wg, wu, *, name):
    s = da.shape[1]
    d = wg.shape[1]
    tm = _tile(s, 1024)

    def body(da_ref, db_ref, wg_ref, wu_ref, o_ref):
        j = pl.program_id(1)

        @pl.when(j == 0)
        def _():
            o_ref[...] = jnp.zeros_like(o_ref)

        o_ref[...] += _nt(da_ref[...], wg_ref[...]) + _nt(db_ref[...], wu_ref[...])

    h_spec = pl.BlockSpec((None, tm, FF_SHARD), lambda i, j: (j, i, 0))
    w_spec = pl.BlockSpec((None, d, FF_SHARD), lambda i, j: (j, 0, 0))
    return pl.pallas_call(
        body, name=name, grid=(s // tm, N_SHARD),
        in_specs=[h_spec, h_spec, w_spec, w_spec],
        out_specs=pl.BlockSpec((tm, d), lambda i, j: (i, 0)),
        out_shape=jax.ShapeDtypeStruct((s, d), F32))(da, db, wg, wu)


def _rope_tables(pos, inv, *, name):
    s = pos.shape[0]
    tm = _tile(s, 512)
    half = ROPE // 2

    def body(p_ref, i_ref, c_ref, s_ref):
        ang = p_ref[...] * i_ref[...]
        lane = lax.broadcasted_iota(jnp.int32, ang.shape, 1)
        live = lane < ROPE
        c_ref[...] = jnp.where(live, jnp.cos(ang), 0.0)
        sn = jnp.sin(ang)
        s_ref[...] = jnp.where(live, jnp.where(lane < half, -sn, sn), 0.0)

    out = jax.ShapeDtypeStruct((s, LANES), F32)
    return pl.pallas_call(
        body, name=name, grid=(s // tm,),
        in_specs=[pl.BlockSpec((tm, 1), lambda i: (i, 0)), pl.BlockSpec((1, LANES), lambda i: (0, 0))],
        out_specs=[pl.BlockSpec((tm, LANES), lambda i: (i, 0))] * 2, out_shape=[out, out])(pos, inv)


def _swap_halves(v):
    half = ROPE // 2
    lane = lax.broadcasted_iota(jnp.int32, v.shape, 1)
    return jnp.where(lane < half, pltpu.roll(v, LANES - half, 1), pltpu.roll(v, half, 1))


def _head_norm_rope_fwd(raw, g, cos, sin, *, name):
    s = raw.shape[0]
    tm = _tile(s, 256)
    width = N_HEADS * HEAD_PAD

    def body(x_ref, g_ref, c_ref, s_ref, o_ref):
        cv, sv = c_ref[...], s_ref[...]
        for h in range(N_HEADS):
            lo = h * HEAD_PAD
            xa = x_ref[:, lo:lo + NOPE]
            xb = x_ref[:, lo + NOPE:lo + HEAD_PAD]
            ms = (jnp.sum(xa * xa, axis=-1, keepdims=True) + jnp.sum(xb * xb, axis=-1, keepdims=True)) * (1.0 / QK_DIM)
            r = lax.rsqrt(ms + EPS)
            o_ref[:, lo:lo + NOPE] = (xa * r * g_ref[:, 0:NOPE]).astype(o_ref.dtype)
            yb = xb * r * g_ref[:, NOPE:HEAD_PAD]
            o_ref[:, lo + NOPE:lo + HEAD_PAD] = (yb * cv + _swap_halves(yb) * sv).astype(o_ref.dtype)

    row_spec = pl.BlockSpec((tm, width), lambda i: (i, 0))
    tab_spec = pl.BlockSpec((tm, LANES), lambda i: (i, 0))
    return pl.pallas_call(
        body, name=name, grid=(s // tm,),
        in_specs=[row_spec, pl.BlockSpec((1, HEAD_PAD), lambda i: (0, 0)), tab_spec, tab_spec],
        out_specs=row_spec, out_shape=jax.ShapeDtypeStruct((s, width), MXU_DTYPE))(raw, g, cos, sin)


def _head_norm_rope_bwd(raw, g, cos, sin, dout, *, name):
    s = raw.shape[0]
    tm = _tile(s, 256)
    width = N_HEADS * HEAD_PAD

    def body(x_ref, g_ref, c_ref, s_ref, do_ref, dx_ref, dg_ref):
        @pl.when(pl.program_id(0) == 0)
        def _():
            dg_ref[...] = jnp.zeros_like(dg_ref)

        cv, sv = c_ref[...], s_ref[...]
        ga, gb = g_ref[:, 0:NOPE], g_ref[:, NOPE:HEAD_PAD]
        for h in range(N_HEADS):
            lo = h * HEAD_PAD
            xa = x_ref[:, lo:lo + NOPE]
            xb = x_ref[:, lo + NOPE:lo + HEAD_PAD]
            dya = do_ref[:, lo:lo + NOPE]
            dob = do_ref[:, lo + NOPE:lo + HEAD_PAD]
            dyb = dob * cv + _swap_halves(dob * sv)
            ms = (jnp.sum(xa * xa, axis=-1, keepdims=True) + jnp.sum(xb * xb, axis=-1, keepdims=True)) * (1.0 / QK_DIM)
            r = lax.rsqrt(ms + EPS)
            xha, xhb = xa * r, xb * r
            gya, gyb = dya * ga, dyb * gb
            dot = (jnp.sum(gya * xha, axis=-1, keepdims=True) + jnp.sum(gyb * xhb, axis=-1, keepdims=True)) * (1.0 / QK_DIM)
            dx_ref[:, lo:lo + NOPE] = r * (gya - xha * dot)
            dx_ref[:, lo + NOPE:lo + HEAD_PAD] = r * (gyb - xhb * dot)
            dg_ref[:, 0:NOPE] += _fold8(dya * xha)
            dg_ref[:, NOPE:HEAD_PAD] += _fold8(dyb * xhb)

    row_spec = pl.BlockSpec((tm, width), lambda i: (i, 0))
    tab_spec = pl.BlockSpec((tm, LANES), lambda i: (i, 0))
    return pl.pallas_call(
        body, name=name, grid=(s // tm,),
        in_specs=[row_spec, pl.BlockSpec((1, HEAD_PAD), lambda i: (0, 0)), tab_spec, tab_spec, row_spec],
        out_specs=[row_spec, pl.BlockSpec((SUBLANES, HEAD_PAD), lambda i: (0, 0))],
        out_shape=[jax.ShapeDtypeStruct((s, width), F32), jax.ShapeDtypeStruct((SUBLANES, HEAD_PAD), F32)])(raw, g, cos, sin, dout)


def _k_assemble(kn, ckv, *, name):
    s = kn.shape[0]
    tm = _tile(s, 512)
    width = N_HEADS * HEAD_PAD

    def body(kn_ref, pe_ref, o_ref):
        pe = pe_ref[...]
        for h in range(N_HEADS):
            o_ref[:, h * HEAD_PAD:h * HEAD_PAD + NOPE] = kn_ref[:, h * NOPE:(h + 1) * NOPE]
            o_ref[:, h * HEAD_PAD + NOPE:(h + 1) * HEAD_PAD] = pe

    return pl.pallas_call(
        body, name=name, grid=(s // tm,),
        in_specs=[pl.BlockSpec((tm, N_HEADS * NOPE), lambda i: (i, 0)),
                  pl.BlockSpec((tm, LANES), lambda i: (i, KV_LORA // LANES))],
        out_specs=pl.BlockSpec((tm, width), lambda i: (i, 0)),
        out_shape=jax.ShapeDtypeStruct((s, width), F32))(kn, ckv)


def _k_disassemble(dk_raw, *, name):
    s = dk_raw.shape[0]
    tm = _tile(s, 512)
    width = N_HEADS * HEAD_PAD

    def body(dk_ref, dkn_ref, dpe_ref):
        pe = dk_ref[:, NOPE:HEAD_PAD]
        for h in range(N_HEADS):
            dkn_ref[:, h * NOPE:(h + 1) * NOPE] = dk_ref[:, h * HEAD_PAD:h * HEAD_PAD + NOPE]
            if h:
                pe = pe + dk_ref[:, h * HEAD_PAD + NOPE:(h + 1) * HEAD_PAD]
        dpe_ref[...] = pe

    return pl.pallas_call(
        body, name=name, grid=(s // tm,),
        in_specs=[pl.BlockSpec((tm, width), lambda i: (i, 0))],
        out_specs=[pl.BlockSpec((tm, N_HEADS * NOPE), lambda i: (i, 0)), pl.BlockSpec((tm, LANES), lambda i: (i, 0))],
        out_shape=[jax.ShapeDtypeStruct((s, N_HEADS * NOPE), F32), jax.ShapeDtypeStruct((s, LANES), F32)])(dk_raw)


ATTN_SCALE = 1.0 / math.sqrt(QK_DIM)
MASKED = -1e30


ATTN_TILE = 512


def _chunk_mask(q0, k0, shape, q_axis):
    qpos = q0 + lax.broadcasted_iota(jnp.int32, shape, q_axis)
    kpos = k0 + lax.broadcasted_iota(jnp.int32, shape, 1 - q_axis)
    return kpos // CHUNK <= qpos // CHUNK


LOG2E = math.log2(math.e)
SCORE_LOG2 = ATTN_SCALE * LOG2E


def _causal_pairs(n, by_key):
    if by_key:
        pairs = [(i, j) for j in range(n) for i in range(j, n)]
    else:
        pairs = [(i, j) for i in range(n) for j in range(i + 1)]
    return jnp.asarray([p[0] for p in pairs], jnp.int32), jnp.asarray([p[1] for p in pairs], jnp.int32)


def _attn_fwd(q, k, vt, *, name):
    s = q.shape[0]
    t = _tile(s, ATTN_TILE)
    n = s // t
    qi_tab, kj_tab = _causal_pairs(n, by_key=False)

    def body(qi_ref, kj_ref, q_ref, k_ref, vt_ref, o_ref, lse_ref, m_sc, l_sc, acc):
        pair = pl.program_id(1)
        qi, kj = qi_ref[pair], kj_ref[pair]

        @pl.when(kj == 0)
        def _():
            m_sc[...] = jnp.full_like(m_sc, MASKED)
            l_sc[...] = jnp.zeros_like(l_sc)
            acc[...] = jnp.zeros_like(acc)

        def step(masked):
            st = _nt(k_ref[...], q_ref[...])
            if masked:
                st = jnp.where(_chunk_mask(qi * t, kj * t, (t, t), 1), st, MASKED)
            m_prev = m_sc[...]
            m_new = jnp.maximum(m_prev, jnp.max(st, axis=0, keepdims=True) * SCORE_LOG2)
            alpha = jnp.exp2(m_prev - m_new)
            pt = jnp.exp2(st * SCORE_LOG2 - m_new)
            l_sc[...] = alpha * l_sc[...] + jnp.sum(pt, axis=0, keepdims=True)
            acc[...] = alpha * acc[...] + _nn(vt_ref[...], pt)
            m_sc[...] = m_new

        @pl.when(kj < qi)
        def _():
            step(False)

        @pl.when(kj == qi)
        def _():
            step(True)
            o_ref[...] = acc[...] / l_sc[...]
            lse_ref[...] = m_sc[...] + jnp.log(l_sc[...]) * LOG2E

    return pl.pallas_call(
        body, name=name,
        grid_spec=pltpu.PrefetchScalarGridSpec(
            num_scalar_prefetch=2, grid=(N_HEADS, int(qi_tab.shape[0])),
            in_specs=[pl.BlockSpec((t, HEAD_PAD), lambda h, p, qi, kj: (qi[p], h)),
                      pl.BlockSpec((t, HEAD_PAD), lambda h, p, qi, kj: (kj[p], h)),
                      pl.BlockSpec((V_DIM, t), lambda h, p, qi, kj: (h, kj[p]))],
            out_specs=[pl.BlockSpec((V_DIM, t), lambda h, p, qi, kj: (h, qi[p])),
                       pl.BlockSpec((None, 1, t), lambda h, p, qi, kj: (h, 0, qi[p]))],
            scratch_shapes=[pltpu.VMEM((1, t), F32), pltpu.VMEM((1, t), F32), pltpu.VMEM((V_DIM, t), F32)]),
        out_shape=[jax.ShapeDtypeStruct((N_HEADS * V_DIM, s), F32), jax.ShapeDtypeStruct((N_HEADS, 1, s), F32)])(qi_tab, kj_tab, q, k, vt)


def _attn_delta(ot, dot, *, name):
    s = ot.shape[1]
    t = _tile(s, 1024)

    def body(o_ref, do_ref, d_ref):
        d_ref[...] = jnp.sum(o_ref[...] * do_ref[...], axis=0, keepdims=True)

    blk = pl.BlockSpec((V_DIM, t), lambda h, i: (h, i))
    return pl.pallas_call(
        body, name=name, grid=(N_HEADS, s // t), in_specs=[blk, blk],
        out_specs=pl.BlockSpec((None, 1, t), lambda h, i: (h, 0, i)),
        out_shape=jax.ShapeDtypeStruct((N_HEADS, 1, s), F32))(ot, dot)


def _attn_bwd_dq(q, k, v, do, lse_col, delta_col, *, name):
    s = q.shape[0]
    t = _tile(s, ATTN_TILE)
    n = s // t
    qi_tab, kj_tab = _causal_pairs(n, by_key=False)

    def body(qi_ref, kj_ref, q_ref, k_ref, v_ref, do_ref, lse_ref, dl_ref, dq_ref, acc):
        pair = pl.program_id(1)
        qi, kj = qi_ref[pair], kj_ref[pair]

        @pl.when(kj == 0)
        def _():
            acc[...] = jnp.zeros_like(acc)

        def step(masked):
            kv = k_ref[...]
            sc = _nt(q_ref[...], kv)
            if masked:
                sc = jnp.where(_chunk_mask(qi * t, kj * t, (t, t), 0), sc, MASKED)
            p = jnp.exp2(sc * SCORE_LOG2 - lse_ref[...])
            dp = _nt(do_ref[...], v_ref[...])
            acc[...] += _nn(p * (dp - dl_ref[...]), kv)

        @pl.when(kj < qi)
        def _():
            step(False)

        @pl.when(kj == qi)
        def _():
            step(True)
            dq_ref[...] = acc[...] * ATTN_SCALE

    col = pl.BlockSpec((None, t, 1), lambda h, p, qi, kj: (h, qi[p], 0))
    return pl.pallas_call(
        body, name=name,
        grid_spec=pltpu.PrefetchScalarGridSpec(
            num_scalar_prefetch=2, grid=(N_HEADS, int(qi_tab.shape[0])),
            in_specs=[pl.BlockSpec((t, HEAD_PAD), lambda h, p, qi, kj: (qi[p], h)),
                      pl.BlockSpec((t, HEAD_PAD), lambda h, p, qi, kj: (kj[p], h)),
                      pl.BlockSpec((t, V_DIM), lambda h, p, qi, kj: (kj[p], h)),
                      pl.BlockSpec((t, V_DIM), lambda h, p, qi, kj: (qi[p], h)), col, col],
            out_specs=pl.BlockSpec((t, HEAD_PAD), lambda h, p, qi, kj: (qi[p], h)),
            scratch_shapes=[pltpu.VMEM((t, HEAD_PAD), F32)]),
        out_shape=jax.ShapeDtypeStruct((s, N_HEADS * HEAD_PAD), F32))(qi_tab, kj_tab, q, k, v, do, lse_col, delta_col)


def _attn_bwd_dkv(q, k, v, do, lse_row, delta_row, dk_in, dv_in, *, name):
    s = q.shape[0]
    t = _tile(s, ATTN_TILE)
    n = s // t
    has_in = dk_in is not None
    qi_tab, kj_tab = _causal_pairs(n, by_key=True)

    def body(qi_ref, kj_ref, *refs):
        if has_in:
            q_ref, k_ref, v_ref, do_ref, lse_ref, dl_ref, dki_ref, dvi_ref, dk_ref, dv_ref, acck, accv = refs
        else:
            q_ref, k_ref, v_ref, do_ref, lse_ref, dl_ref, dk_ref, dv_ref, acck, accv = refs
        pair = pl.program_id(1)
        qi, kj = qi_ref[pair], kj_ref[pair]

        def step(masked):
            qv, dov = q_ref[...], do_ref[...]
            st = _nt(k_ref[...], qv)
            if masked:
                st = jnp.where(_chunk_mask(qi * t, kj * t, (t, t), 1), st, MASKED)
            pt = jnp.exp2(st * SCORE_LOG2 - lse_ref[...])
            accv[...] += _nn(pt, dov)
            dpt = _nt(v_ref[...], dov)
            acck[...] += _nn(pt * (dpt - dl_ref[...]), qv)

        @pl.when(qi == kj)
        def _():
            acck[...] = jnp.zeros_like(acck)
            accv[...] = jnp.zeros_like(accv)
            step(True)

        @pl.when(qi > kj)
        def _():
            step(False)

        @pl.when(qi == n - 1)
        def _():
            dk = acck[...] * ATTN_SCALE
            dv = accv[...]
            if has_in:
                dk = dki_ref[...] + dk
                dv = dvi_ref[...] + dv
            dk_ref[...] = dk
            dv_ref[...] = dv

    row = pl.BlockSpec((None, 1, t), lambda h, p, qi, kj: (h, 0, qi[p]))
    k_spec = pl.BlockSpec((t, HEAD_PAD), lambda h, p, qi, kj: (kj[p], h))
    v_spec = pl.BlockSpec((t, V_DIM), lambda h, p, qi, kj: (kj[p], h))
    in_specs = [pl.BlockSpec((t, HEAD_PAD), lambda h, p, qi, kj: (qi[p], h)), k_spec, v_spec,
                pl.BlockSpec((t, V_DIM), lambda h, p, qi, kj: (qi[p], h)), row, row]
    args = [q, k, v, do, lse_row, delta_row]
    if has_in:
        in_specs += [k_spec, v_spec]
        args += [dk_in, dv_in]
    return pl.pallas_call(
        body, name=name,
        grid_spec=pltpu.PrefetchScalarGridSpec(
            num_scalar_prefetch=2, grid=(N_HEADS, int(qi_tab.shape[0])), in_specs=in_specs, out_specs=[k_spec, v_spec],
            scratch_shapes=[pltpu.VMEM((t, HEAD_PAD), F32), pltpu.VMEM((t, V_DIM), F32)]),
        out_shape=[jax.ShapeDtypeStruct((s, N_HEADS * HEAD_PAD), F32), jax.ShapeDtypeStruct((s, N_HEADS * V_DIM), F32)])(qi_tab, kj_tab, *args)


def _loss_head(y, target, *, name):
    s, d = y.shape
    tm = _tile(s, 512)

    def body(y_ref, t_ref, dy_ref, l_ref):
        @pl.when(pl.program_id(0) == 0)
        def _():
            l_ref[...] = jnp.zeros_like(l_ref)

        err = y_ref[...] - t_ref[...]
        dy_ref[...] = err * (1.0 / d)
        sq = _fold8(err * err)
        part = sq[:, 0:LANES]
        for cb in range(1, d // LANES):
            part = part + sq[:, cb * LANES:(cb + 1) * LANES]
        l_ref[...] += part * (0.5 / d)

    row_spec = pl.BlockSpec((tm, d), lambda i: (i, 0))
    return pl.pallas_call(
        body, name=name, grid=(s // tm,), in_specs=[row_spec, row_spec],
        out_specs=[row_spec, pl.BlockSpec((SUBLANES, LANES), lambda i: (0, 0))],
        out_shape=[jax.ShapeDtypeStruct((s, d), F32), jax.ShapeDtypeStruct((SUBLANES, LANES), F32)])(y, target)


ADAMW_ROWS = 512


def _adamw_math(w, m, v, g):
    mn = ADAM_B1 * m + (1.0 - ADAM_B1) * g
    vn = ADAM_B2 * v + (1.0 - ADAM_B2) * (g * g)
    m_hat = mn / (1.0 - ADAM_B1 ** ADAM_STEP)
    v_hat = vn / (1.0 - ADAM_B2 ** ADAM_STEP)
    return -ADAM_LR * (m_hat / (jnp.sqrt(v_hat) + ADAM_EPS) + ADAM_WD * w), mn, vn


def _adamw_vectors(ws, ms, vs, gs, *, name):
    n = len(ws)

    def body(*refs):
        ins, outs = refs[:4 * n], refs[4 * n:]
        for a in range(n):
            g = ins[3 * n + a][...]
            outs[a][...] = g
            outs[n + a][...], outs[2 * n + a][...], outs[3 * n + a][...] = _adamw_math(ins[a][...], ins[n + a][...], ins[2 * n + a][...], g)

    shapes = [jax.ShapeDtypeStruct(w.shape, F32) for w in ws]
    out = pl.pallas_call(body, name=name, in_specs=[VMEM_SPEC] * (4 * n), out_specs=[VMEM_SPEC] * (4 * n),
                         out_shape=shapes * 4)(*ws, *ms, *vs, *gs)
    return out[:n], out[n:2 * n], out[2 * n:3 * n], out[3 * n:]


def _adamw(w, m, v, g_parts, *, name):
    rows, cols = w.shape
    tm = _tile(rows, ADAMW_ROWS)
    n_parts = len(g_parts)

    def body(*refs):
        w_ref, m_ref, v_ref = refs[:3]
        g_refs = refs[3:3 + n_parts]
        g_out, d_out, m_out, v_out = refs[3 + n_parts:]
        g = g_refs[0][...]
        for r in g_refs[1:]:
            g = g + r[...]
        g_out[...] = g
        d_out[...], m_out[...], v_out[...] = _adamw_math(w_ref[...], m_ref[...], v_ref[...], g)

    spec = pl.BlockSpec((tm, cols), lambda i: (i, 0))
    out = jax.ShapeDtypeStruct((rows, cols), F32)
    return pl.pallas_call(
        body, name=name, grid=(rows // tm,), in_specs=[spec] * (3 + n_parts),
        out_specs=[spec] * 4, out_shape=[out] * 4)(w, m, v, *g_parts)


def _sum_slots(parts, *, name):
    _, rows, cols = parts.shape
    tm = _tile(rows, 512)

    def body(p_ref, o_ref):
        acc = p_ref[0].astype(F32)
        for k in range(1, N_SHARD):
            acc = acc + p_ref[k].astype(F32)
        o_ref[...] = acc

    return pl.pallas_call(
        body, name=name, grid=(rows // tm,),
        in_specs=[pl.BlockSpec((N_SHARD, tm, cols), lambda i: (0, i, 0))],
        out_specs=pl.BlockSpec((tm, cols), lambda i: (i, 0)),
        out_shape=jax.ShapeDtypeStruct((rows, cols), F32))(parts)


def _mesh_pos():
    return lax.axis_index("x"), lax.axis_index("y"), lax.axis_index("c")


CHIP_FLIPS = ((1, 0), (0, 1), (1, 1))


class Exchange(NamedTuple):
    kind: str
    srcs: tuple
    lands: tuple
    layer: Any = None


HBM_SPEC = pl.BlockSpec(memory_space=pltpu.HBM)
SEM_SPEC = pl.BlockSpec(memory_space=pltpu.SEMAPHORE)
DATAFLOW = pltpu.SideEffectType.DATAFLOW_SIDE_EFFECTING


def _exchange_copies(ex, src_refs, land_refs, send_sems, recv_sems):
    x, y, c = _mesh_pos()
    mine = 2 * x + y

    def slot(ref, chip):
        return ref.at[chip] if ex.layer is None else ref.at[chip, ex.layer]

    pairs = []
    for a, (src, land) in enumerate(zip(src_refs, land_refs)):
        for k, (fx, fy) in enumerate(CHIP_FLIPS):
            px, py = x ^ fx, y ^ fy
            peer = 2 * px + py
            src_part = src if ex.kind == "gather" else src.at[peer]
            pair = a * len(CHIP_FLIPS) + k
            common = dict(src_ref=src_part, send_sem=send_sems.at[pair], recv_sem=recv_sems.at[pair],
                          device_id=(px, py, c), device_id_type=MESH)
            pairs.append((pltpu.make_async_remote_copy(dst_ref=slot(land, mine), **common),
                          pltpu.make_async_remote_copy(dst_ref=slot(land, peer), **common)))
    return pairs


def _exchange_start(exchanges, *, name):
    srcs = [s for ex in exchanges for s in ex.srcs]
    lands = [b for ex in exchanges for b in ex.lands]
    n_arr, n_ex = len(srcs) + len(lands), len(exchanges)

    def body(*refs):
        src_refs, land_refs = refs[:len(srcs)], refs[len(srcs):n_arr]
        sems, token = refs[n_arr:n_arr + 2 * n_ex], refs[-1]
        at = 0
        for e, ex in enumerate(exchanges):
            n = len(ex.srcs)
            for send, _ in _exchange_copies(ex, src_refs[at:at + n], land_refs[at:at + n], sems[2 * e], sems[2 * e + 1]):
                send.start()
            at += n
        token[...] = jnp.zeros_like(token)

    sem_shapes = [pltpu.SemaphoreType.DMA((len(ex.srcs) * len(CHIP_FLIPS),)) for ex in exchanges for _ in range(2)]
    out = pl.pallas_call(
        body, name=name,
        out_shape=sem_shapes + [pltpu.HBM(a.shape, a.dtype) for a in srcs + lands] + [jax.ShapeDtypeStruct((SUBLANES, LANES), F32)],
        in_specs=[HBM_SPEC] * n_arr, out_specs=[SEM_SPEC] * (2 * n_ex) + [HBM_SPEC] * n_arr + [VMEM_SPEC],
        input_output_aliases={i: 2 * n_ex + i for i in range(n_arr)},
        compiler_params=pltpu.CompilerParams(has_side_effects=DATAFLOW),
    )(*[pltpu.with_memory_space_constraint(a, pltpu.HBM) for a in srcs + lands])
    sems, thru = out[:2 * n_ex], out[2 * n_ex:-1]
    pending, at = [], 0
    for e, ex in enumerate(exchanges):
        n = len(ex.srcs)
        pending.append((ex._replace(srcs=tuple(thru[at:at + n]), lands=tuple(thru[len(srcs) + at:len(srcs) + at + n])),
                        sems[2 * e], sems[2 * e + 1]))
        at += n
    return pending, out[-1]


def _exchange_wait(pending, after, *, name):
    ex, send_sems, recv_sems = pending
    n = len(ex.srcs)

    def body(*refs):
        src_refs, land_refs = refs[:n], refs[n:2 * n]
        for send, arrive in _exchange_copies(ex, src_refs, land_refs, refs[2 * n], refs[2 * n + 1]):
            send.wait_send()
            arrive.wait_recv()

    arrays = list(ex.srcs) + list(ex.lands)
    out = pl.pallas_call(
        body, name=name, out_shape=[pltpu.HBM(a.shape, a.dtype) for a in arrays],
        in_specs=[HBM_SPEC] * (2 * n) + [SEM_SPEC, SEM_SPEC, ANY], out_specs=[HBM_SPEC] * (2 * n),
        input_output_aliases={i: i for i in range(2 * n)},
        compiler_params=pltpu.CompilerParams(has_side_effects=DATAFLOW),
    )(*arrays, send_sems, recv_sems, after)
    return out[n:]


def _swap_with_sibling(arrays, *, name):
    n = len(arrays)

    def body(*refs):
        ins, outs = refs[:n], refs[n:2 * n]
        send_sems, recv_sems = refs[2 * n:]
        x, y, c = _mesh_pos()
        copies = []
        for a in range(n):
            cp = pltpu.make_async_remote_copy(
                src_ref=ins[a], dst_ref=outs[a], send_sem=send_sems.at[a], recv_sem=recv_sems.at[a],
                device_id=(x, y, 1 - c), device_id_type=MESH)
            cp.start()
            copies.append(cp)
        for cp in copies:
            cp.wait()

    return pl.pallas_call(
        body, name=name, in_specs=[ANY] * n, out_specs=[ANY] * n,
        out_shape=[jax.ShapeDtypeStruct(a.shape, a.dtype) for a in arrays],
        scratch_shapes=[pltpu.SemaphoreType.DMA((n,)), pltpu.SemaphoreType.DMA((n,))])(*arrays)


N_DEV = 8


def _all_reduce_small(vec, *, name):
    rows = vec.shape[0]

    def body(v_ref, o_ref, land, send_sems, recv_sems):
        x, y, c = _mesh_pos()
        me = 4 * x + 2 * y + c
        land[me] = v_ref[...]
        copies = []
        for k in range(1, N_DEV):
            fx, fy, fc = (k >> 2) & 1, (k >> 1) & 1, k & 1
            px, py, pc = x ^ fx, y ^ fy, c ^ fc
            send = pltpu.make_async_remote_copy(
                src_ref=v_ref, dst_ref=land.at[me], send_sem=send_sems.at[k - 1], recv_sem=recv_sems.at[k - 1],
                device_id=(px, py, pc), device_id_type=MESH)
            send.start()
            arrive = pltpu.make_async_remote_copy(
                src_ref=v_ref, dst_ref=land.at[4 * px + 2 * py + pc], send_sem=send_sems.at[k - 1], recv_sem=recv_sems.at[k - 1],
                device_id=(px, py, pc), device_id_type=MESH)
            copies.append((send, arrive))
        for send, arrive in copies:
            send.wait_send()
            arrive.wait_recv()
        acc = land[0]
        for k in range(1, N_DEV):
            acc = acc + land[k]
        o_ref[...] = acc

    return pl.pallas_call(
        body, name=name, in_specs=[VMEM_SPEC], out_specs=VMEM_SPEC,
        out_shape=jax.ShapeDtypeStruct(vec.shape, F32),
        scratch_shapes=[pltpu.VMEM((N_DEV, rows, LANES), F32), pltpu.SemaphoreType.DMA((N_DEV - 1,)),
                        pltpu.SemaphoreType.DMA((N_DEV - 1,))])(vec)


PACK_UNIT = SUBLANES * LANES * 2


def _padded(n):
    return -(-n // PACK_UNIT) * PACK_UNIT


def _pack(arrays, dtype, lead=0):
    parts = []
    for a in arrays:
        lead_shape = a.shape[:lead]
        flat = a.astype(dtype).reshape(lead_shape + (-1,))
        n = flat.shape[-1]
        flat = jnp.pad(flat, [(0, 0)] * lead + [(0, _padded(n) - n)])
        parts.append(flat.reshape(lead_shape + (-1, LANES)))
    return jnp.concatenate(parts, axis=lead)


def _unpack(buf, shapes, lead=0):
    out, row = [], 0
    for shp in shapes:
        n = math.prod(shp)
        rows = _padded(n) // LANES
        part = lax.slice_in_dim(buf, row, row + rows, axis=lead)
        lead_shape = part.shape[:lead]
        part = part.reshape(lead_shape + (-1,))
        part = lax.slice_in_dim(part, 0, n, axis=lead)
        out.append(part.reshape(lead_shape + tuple(shp)))
        row += rows
    return out


def kernel(x, positions, ln_mix_a, w_pool, b_pool, pool_scale, ln_ffn, w_gate, w_up, w_down, ln_kv, w_dkv, g_kv_latent, w_uk, w_uv, g_k, ln_mix_b, w_dq, g_q_latent, w_uq, g_q, w_o, loss_target, m_ln_mix_a, m_w_pool, m_b_pool, m_pool_scale, m_ln_ffn, m_w_gate, m_w_up, m_w_down, m_ln_kv, m_w_dkv, m_g_kv_latent, m_w_uk, m_w_uv, m_g_k, m_ln_mix_b, m_w_dq, m_g_q_latent, m_w_uq, m_g_q, m_w_o, v_ln_mix_a, v_w_pool, v_b_pool, v_pool_scale, v_ln_ffn, v_w_gate, v_w_up, v_w_down, v_ln_kv, v_w_dkv, v_g_kv_latent, v_w_uk, v_w_uv, v_g_k, v_ln_mix_b, v_w_dq, v_g_q_latent, v_w_uq, v_g_q, v_w_o):
    weights = dict(ln_mix_a=ln_mix_a, w_pool=w_pool, b_pool=b_pool, pool_scale=pool_scale, ln_ffn=ln_ffn, w_gate=w_gate,
                   w_up=w_up, w_down=w_down, ln_kv=ln_kv, w_dkv=w_dkv, g_kv_latent=g_kv_latent, w_uk=w_uk, w_uv=w_uv, g_k=g_k,
                   ln_mix_b=ln_mix_b, w_dq=w_dq, g_q_latent=g_q_latent, w_uq=w_uq, g_q=g_q, w_o=w_o)
    mom_m = dict(ln_mix_a=m_ln_mix_a, w_pool=m_w_pool, b_pool=m_b_pool, pool_scale=m_pool_scale, ln_ffn=m_ln_ffn,
                 w_gate=m_w_gate, w_up=m_w_up, w_down=m_w_down, ln_kv=m_ln_kv, w_dkv=m_w_dkv, g_kv_latent=m_g_kv_latent,
                 w_uk=m_w_uk, w_uv=m_w_uv, g_k=m_g_k, ln_mix_b=m_ln_mix_b, w_dq=m_w_dq, g_q_latent=m_g_q_latent,
                 w_uq=m_w_uq, g_q=m_g_q, w_o=m_w_o)
    mom_v = dict(ln_mix_a=v_ln_mix_a, w_pool=v_w_pool, b_pool=v_b_pool, pool_scale=v_pool_scale, ln_ffn=v_ln_ffn,
                 w_gate=v_w_gate, w_up=v_w_up, w_down=v_w_down, ln_kv=v_ln_kv, w_dkv=v_w_dkv, g_kv_latent=v_g_kv_latent,
                 w_uk=v_w_uk, w_uv=v_w_uv, g_k=v_g_k, ln_mix_b=v_ln_mix_b, w_dq=v_w_dq, g_q_latent=v_g_q_latent,
                 w_uq=v_w_uq, g_q=v_g_q, w_o=v_w_o)
    order = list(weights)
    s = x.shape[1]
    d = D_MODEL
    xs = x.reshape(s, d)
    target = loss_target.reshape(s, d)
    my_chip = 2 * lax.axis_index("x") + lax.axis_index("y")

    mat_names = ("w_pool", "w_dkv", "w_uk", "w_uv", "w_dq", "w_uq", "w_o")
    vec_names = ("ln_mix_a", "b_pool", "pool_scale")
    mat_shapes = [weights[n].shape for n in mat_names]
    vec_shapes = [weights[n].shape for n in vec_names]

    def rows_of(a, lead=0):
        return a.reshape(a.shape[:lead] + (-1, a.shape[-1]))

    mats_local = tuple(rows_of(weights[n].astype(WIRE_DTYPE)) for n in mat_names)
    vecs_local = _pack([weights[n] for n in vec_names], F32)

    def landing(shard):
        return lax.dynamic_update_slice_in_dim(lax.empty((N_SHARD,) + shard.shape, shard.dtype), shard[None], my_chip, axis=0)

    small_local = mats_local + (vecs_local,)
    gathers = [Exchange("gather", small_local, tuple(landing(sh) for sh in small_local))]
    for l in range(DEPTH):
        shards = tuple(w[l].astype(WIRE_DTYPE) for w in (w_gate, w_up, w_down))
        gathers.append(Exchange("gather", shards, tuple(landing(sh) for sh in shards)))
    gathering, _ = _exchange_start(gathers, name="gather_start")

    inv = ROPE_THETA ** (-jnp.arange(ROPE // 2, dtype=F32) * 2.0 / ROPE)
    inv_lanes = jnp.concatenate([inv, inv, jnp.zeros((LANES - ROPE,), F32)]).reshape(1, LANES)
    cos_t, sin_t = _rope_tables(positions.reshape(s, 1).astype(F32), inv_lanes, name="rope_tables")

    *mats_all, vecs_all = _exchange_wait(gathering[0], cos_t, name="gather_wait_small")
    g_pool, g_dkv, g_uk, g_uv, g_dq, g_uq, g_o = (a.reshape((N_SHARD,) + shp) for a, shp in zip(mats_all, mat_shapes))
    g_lna, g_bp, g_ps = _unpack(vecs_all, vec_shapes, lead=1)

    wpool_f = g_pool.transpose(1, 2, 0, 3, 4).reshape(N_A, N_GROUPS, GROUP_DIM, GROUP_DIM)
    bpool_f = g_bp.transpose(1, 2, 0, 3).reshape(N_A, 1, d)
    pscale_f = g_ps.transpose(1, 0, 2).reshape(N_A, 1, d)
    lna_f = g_lna.transpose(1, 0, 2).reshape(N_A, 1, d)
    wdkv_f = jnp.pad(g_dkv.reshape(d, KV_LORA + ROPE), ((0, 0), (0, CKV_PAD - KV_LORA - ROPE)))
    wuk_f = g_uk.transpose(1, 0, 2).reshape(KV_LORA, N_HEADS * NOPE)
    wuv_f = g_uv.transpose(1, 0, 2).reshape(KV_LORA, N_HEADS * V_DIM)
    wdq_f = g_dq.transpose(1, 0, 2, 3).reshape(N_B, d, Q_LORA)
    wuq_f = jnp.pad(g_uq.transpose(1, 2, 0, 3).reshape(N_B, Q_LORA, N_HEADS, QK_DIM),
                    ((0, 0), (0, 0), (0, 0), (0, HEAD_PAD - QK_DIM))).reshape(N_B, Q_LORA, N_HEADS * HEAD_PAD)
    wo_f = g_o.transpose(1, 0, 2, 3).reshape(N_B, d, d)

    def head_gain(g):
        return jnp.pad(g.reshape(1, QK_DIM), ((0, 0), (0, HEAD_PAD - QK_DIM)))

    ffn_w = [None] * DEPTH

    def ffn_fwd(xin, layer):
        hf = _rms_fwd(xin, ln_ffn[layer].reshape(1, d), n=d, name="ffn_norm")
        ffn_w[layer] = wg, wu, wd = _exchange_wait(gathering[1 + layer], hf, name=f"gather_wait_{layer}")
        a, b, u = _ffn_up(hf, wg, wu, name="ffn_up")
        return _ffn_down(u, wd, xin, name="ffn_down"), (xin, hf, a, b, u)

    saved_a, saved_b, saved_f = [], [], []
    cur = xs
    for l in range(N_A):
        dpool = _rms_pool_fwd(cur, lna_f[l], name="pool_fwd")
        x1 = _pool_mm_fwd(dpool, wpool_f[l], bpool_f[l], pscale_f[l], cur, name="pool_mm")
        saved_a.append((cur, dpool))
        cur, sf = ffn_fwd(x1, l)
        saved_f.append(sf)

    x_kv = cur
    hk = _rms_fwd(x_kv, ln_kv.reshape(1, d), n=d, name="kv_norm")
    ckv = _mm(hk, wdkv_f, name="kv_down")
    c_lat = _rms_fwd(ckv, g_kv_latent.reshape(1, KV_LORA), n=KV_LORA, name="kv_latent_norm")
    kn_raw = _mm(c_lat, wuk_f, name="k_up")
    v_all = _mm(c_lat, wuv_f, out_dtype=MXU_DTYPE, name="v_up")
    vt_all = _mm(wuv_f.T, c_lat, tb=True, out_dtype=MXU_DTYPE, name="v_up_t")
    k_raw = _k_assemble(kn_raw, ckv, name="k_assemble")
    gk_pad = head_gain(g_k)
    k_cat = _head_norm_rope_fwd(k_raw, gk_pad, cos_t, sin_t, name="k_norm_rope")

    for j in range(N_B):
        l = N_A + j
        hq = _rms_fwd(cur, ln_mix_b[j].reshape(1, d), n=d, name="q_norm")
        cq_raw = _mm(hq, wdq_f[j], name="q_down")
        cq = _rms_fwd(cq_raw, g_q_latent[j].reshape(1, Q_LORA), n=Q_LORA, name="q_latent_norm")
        q_raw = _mm(cq, wuq_f[j], name="q_up")
        gq_pad = head_gain(g_q[j])
        q_cat = _head_norm_rope_fwd(q_raw, gq_pad, cos_t, sin_t, name="q_norm_rope")
        ot, lse = _attn_fwd(q_cat, k_cat, vt_all, name="attn_fwd")
        x1 = _mm(ot, wo_f[j], ta=True, resid=cur, name="attn_out")
        saved_b.append((cur, hq, cq_raw, cq, q_raw, gq_pad, q_cat, ot, lse))
        cur, sf = ffn_fwd(x1, l)
        saved_f.append(sf)

    dy, loss_part = _loss_head(cur, target, name="loss_head")

    ffn_landed = (lax.empty((N_SHARD, DEPTH, d, FF_SHARD), WIRE_DTYPE), lax.empty((N_SHARD, DEPTH, d, FF_SHARD), WIRE_DTYPE),
                  lax.empty((N_SHARD, DEPTH, FF_SHARD, d), WIRE_DTYPE))
    scattering = None
    grads = {}
    d_ln_ffn = [None] * DEPTH

    def own_part(full):
        return lax.dynamic_index_in_dim(full, my_chip, axis=0, keepdims=True)

    def ffn_bwd(dyv, layer):
        nonlocal ffn_landed, scattering
        xin, hf, a, b, u = saved_f[layer]
        wg, wu, wd = ffn_w[layer]
        da, db = _ffn_bwd_hidden(dyv, wd, a, b, name="ffn_bwd_hidden")
        dwd = _ffn_bwd_dwd(u, dyv, name="ffn_bwd_dwd")
        dwg, dwu = _ffn_bwd_dwgu(hf, da, db, name="ffn_bwd_dwgu")
        if scattering is not None:
            ffn_landed = _exchange_wait(scattering, dwg, name=f"scatter_wait_{layer + 1}")
        ffn_landed = tuple(lax.dynamic_update_slice(buf, own_part(g)[:, None], (my_chip, layer, 0, 0))
                           for buf, g in zip(ffn_landed, (dwg, dwu, dwd)))
        (scattering,), started = _exchange_start([Exchange("scatter", (dwg, dwu, dwd), ffn_landed, layer)],
                                                 name=f"scatter_start_{layer}")
        dhf = _ffn_bwd_dh(da, db, wg, wu, name="ffn_bwd_dh")
        dx, dg = _rms_bwd(xin, ln_ffn[layer].reshape(1, d), dhf, n=d, dx_in=dyv, after=started, name="ffn_norm_bwd")
        d_ln_ffn[layer] = dg.sum(axis=0)
        return dx

    dk_acc = dv_acc = None
    d_ln_mix_b, d_w_dq, d_g_q_latent, d_w_uq, d_g_q, d_w_o = ([None] * N_B for _ in range(6))
    dcur = dy
    for j in reversed(range(N_B)):
        l = N_A + j
        xin, hq, cq_raw, cq, q_raw, gq_pad, q_cat, ot, lse = saved_b[j]
        dx1 = ffn_bwd(dcur, l)
        do = _mm(dx1, wo_f[j], tb=True, out_dtype=MXU_DTYPE, name="attn_out_bwd")
        dot = _mm(wo_f[j], dx1, tb=True, name="attn_out_bwd_t")
        d_w_o[j] = _mm_tn(ot, dx1, at=True, name="attn_out_dw")
        delta = _attn_delta(ot, dot, name="attn_delta")
        lse_col, delta_col = lse.reshape(N_HEADS, s, 1), delta.reshape(N_HEADS, s, 1)
        dq_cat = _attn_bwd_dq(q_cat, k_cat, v_all, do, lse_col, delta_col, name="attn_bwd_dq")
        dk_acc, dv_acc = _attn_bwd_dkv(q_cat, k_cat, v_all, do, lse, delta, dk_acc, dv_acc, name="attn_bwd_dkv")
        dq_raw, dgq = _head_norm_rope_bwd(q_raw, gq_pad, cos_t, sin_t, dq_cat, name="q_norm_rope_bwd")
        d_g_q[j] = dgq.sum(axis=0)[:QK_DIM]
        dcq = _mm(dq_raw, wuq_f[j], tb=True, name="q_up_bwd")
        d_w_uq[j] = _mm_tn(cq, dq_raw, name="q_up_dw").reshape(Q_LORA, N_HEADS, HEAD_PAD)[:, :, :QK_DIM].reshape(Q_LORA, N_HEADS * QK_DIM)
        dcq_raw, dgl = _rms_bwd(cq_raw, g_q_latent[j].reshape(1, Q_LORA), dcq, n=Q_LORA, name="q_latent_norm_bwd")
        d_g_q_latent[j] = dgl.sum(axis=0)
        dhq = _mm(dcq_raw, wdq_f[j], tb=True, name="q_down_bwd")
        d_w_dq[j] = _mm_tn(hq, dcq_raw, name="q_down_dw")
        dcur, dgm = _rms_bwd(xin, ln_mix_b[j].reshape(1, d), dhq, n=d, dx_in=dx1, name="q_norm_bwd")
        d_ln_mix_b[j] = dgm.sum(axis=0)

    dk_raw, dgk = _head_norm_rope_bwd(k_raw, gk_pad, cos_t, sin_t, dk_acc, name="k_norm_rope_bwd")
    grads["g_k"] = dgk.sum(axis=0)[:QK_DIM]
    dc = _mm(dv_acc, wuv_f, tb=True, name="v_up_bwd")
    grads["w_uv"] = _mm_tn(c_lat, dv_acc, name="v_up_dw")
    dkn, dpe = _k_disassemble(dk_raw, name="k_disassemble")
    dc = _mm(dkn, wuk_f, tb=True, resid=dc, name="k_up_bwd")
    grads["w_uk"] = _mm_tn(c_lat, dkn, name="k_up_dw")
    dc_raw, dgl = _rms_bwd(ckv, g_kv_latent.reshape(1, KV_LORA), dc, n=KV_LORA, name="kv_latent_norm_bwd")
    grads["g_kv_latent"] = dgl.sum(axis=0)
    dckv = jnp.concatenate([dc_raw, dpe], axis=1)
    dhk = _mm(dckv, wdkv_f, tb=True, name="kv_down_bwd")
    grads["w_dkv"] = _mm_tn(hk, dckv, name="kv_down_dw")[:, :KV_LORA + ROPE]
    dcur, dg = _rms_bwd(x_kv, ln_kv.reshape(1, d), dhk, n=d, dx_in=dcur, name="kv_norm_bwd")
    grads["ln_kv"] = dg.sum(axis=0)

    d_ln_mix_a, d_w_pool, d_b_pool, d_pool_scale = ([None] * N_A for _ in range(4))
    for l in reversed(range(N_A)):
        xin, dpool = saved_a[l]
        dx1 = ffn_bwd(dcur, l)
        dd, dwp, dbp, dsp = _pool_mm_bwd(dpool, wpool_f[l], bpool_f[l], pscale_f[l], dx1, name="pool_mm_bwd")
        d_w_pool[l], d_b_pool[l], d_pool_scale[l] = dwp, dbp.sum(axis=0), dsp.sum(axis=0)
        dcur, dg = _rms_pool_bwd(xin, lna_f[l], dd, dx1, name="pool_bwd")
        d_ln_mix_a[l] = dg.sum(axis=0)
    grad_x = dcur.reshape(1, s, d)

    gm = {
        "w_pool": jnp.stack(d_w_pool).reshape(N_A, N_GROUPS, N_SHARD, GROUP_DIM // N_SHARD, GROUP_DIM).transpose(2, 0, 1, 3, 4),
        "w_dkv": grads["w_dkv"].reshape(N_SHARD, d // N_SHARD, KV_LORA + ROPE),
        "w_uk": grads["w_uk"].reshape(KV_LORA, N_SHARD, -1).transpose(1, 0, 2),
        "w_uv": grads["w_uv"].reshape(KV_LORA, N_SHARD, -1).transpose(1, 0, 2),
        "w_dq": jnp.stack(d_w_dq).reshape(N_B, N_SHARD, d // N_SHARD, Q_LORA).transpose(1, 0, 2, 3),
        "w_uq": jnp.stack(d_w_uq).reshape(N_B, Q_LORA, N_SHARD, -1).transpose(2, 0, 1, 3),
        "w_o": jnp.stack(d_w_o).reshape(N_B, N_SHARD, d // N_SHARD, d).transpose(1, 0, 2, 3),
    }
    mats_grad = tuple(rows_of(gm[n].astype(WIRE_DTYPE), lead=1) for n in mat_names)
    mats_landing = tuple(lax.dynamic_update_slice_in_dim(lax.empty(g.shape, WIRE_DTYPE), own_part(g), my_chip, axis=0)
                         for g in mats_grad)
    (mats_scatter,), _ = _exchange_start([Exchange("scatter", mats_grad, mats_landing)], name="scatter_start_small")
    ffn_landed = _exchange_wait(scattering, dcur, name="scatter_wait_0")
    mats_landed = _exchange_wait(mats_scatter, ffn_landed[0], name="scatter_wait_small")
    landed = [ffn_landed[0].reshape(N_SHARD, DEPTH * d, FF_SHARD), ffn_landed[1].reshape(N_SHARD, DEPTH * d, FF_SHARD),
              ffn_landed[2].reshape(N_SHARD, DEPTH * FF_SHARD, d), *mats_landed]
    chip_sums = [_sum_slots(p, name="sum_chips") for p in landed]
    sib_sums = _swap_with_sibling(chip_sums, name="swap_sibling")

    vec_full = {
        "ln_mix_a": jnp.stack(d_ln_mix_a), "b_pool": jnp.stack(d_b_pool).reshape(N_A, N_GROUPS, GROUP_DIM),
        "pool_scale": jnp.stack(d_pool_scale), "ln_ffn": jnp.stack(d_ln_ffn), "ln_kv": grads["ln_kv"],
        "g_kv_latent": grads["g_kv_latent"], "g_k": grads["g_k"], "ln_mix_b": jnp.stack(d_ln_mix_b),
        "g_q_latent": jnp.stack(d_g_q_latent), "g_q": jnp.stack(d_g_q),
    }
    small_names = list(vec_full)
    small_shapes = [vec_full[n].shape for n in small_names] + [(SUBLANES * LANES,)]
    small = _all_reduce_small(_pack([vec_full[n] for n in small_names] + [loss_part.reshape(-1)], F32), name="all_reduce_small")
    small_sum = _unpack(small, small_shapes)
    loss = jnp.sum(small_sum[-1])
    vec_grad = dict(zip(small_names, small_sum[:-1]))
    vec_grad["ln_mix_a"] = lax.dynamic_slice_in_dim(vec_grad["ln_mix_a"], my_chip * (d // N_SHARD), d // N_SHARD, axis=1)
    vec_grad["pool_scale"] = lax.dynamic_slice_in_dim(vec_grad["pool_scale"], my_chip * (d // N_SHARD), d // N_SHARD, axis=1)
    vec_grad["b_pool"] = lax.dynamic_slice_in_dim(vec_grad["b_pool"], my_chip * (GROUP_DIM // N_SHARD), GROUP_DIM // N_SHARD, axis=2)

    out_g, out_d, out_m, out_v = {}, {}, {}, {}
    for idx, (nm, rows, cols) in enumerate((("w_gate", DEPTH * d, FF_SHARD), ("w_up", DEPTH * d, FF_SHARD), ("w_down", DEPTH * FF_SHARD, d))):
        res = _adamw(weights[nm].reshape(rows, cols), mom_m[nm].reshape(rows, cols), mom_v[nm].reshape(rows, cols),
                     [chip_sums[idx], sib_sums[idx]], name="adamw_ffn")
        shp = weights[nm].shape
        out_g[nm], out_d[nm], out_m[nm], out_v[nm] = (r.reshape(shp) for r in res)

    for idx, nm in enumerate(mat_names, start=3):
        res = _adamw(rows_of(weights[nm]), rows_of(mom_m[nm]), rows_of(mom_v[nm]), [chip_sums[idx], sib_sums[idx]], name="adamw_mat")
        shp = weights[nm].shape
        out_g[nm], out_d[nm], out_m[nm], out_v[nm] = (r.reshape(shp) for r in res)

    def as_rows(a):
        return a.reshape(1, -1) if a.ndim == 1 else rows_of(a)

    res = _adamw_vectors([as_rows(weights[n]) for n in small_names], [as_rows(mom_m[n]) for n in small_names],
                         [as_rows(mom_v[n]) for n in small_names],
                         [as_rows(vec_grad[n].reshape(weights[n].shape)) for n in small_names], name="adamw_vectors")
    for tgt, arrs in zip((out_g, out_d, out_m, out_v), res):
        for n, arr in zip(small_names, arrs):
            tgt[n] = arr.reshape(weights[n].shape)

    return (loss, grad_x, *[out_g[n] for n in order], *[out_d[n] for n in order],
            *[out_m[n] for n in order], *[out_v[n] for n in order])
```

```python
import math
from typing import Any, NamedTuple

import jax
import jax.numpy as jnp
from jax import lax
from jax.experimental import pallas as pl
from jax.experimental.pallas import tpu as pltpu

F32 = jnp.float32
BF16 = jnp.bfloat16
MXU_DTYPE = BF16
WIRE_DTYPE = BF16
SAVED_DTYPE = BF16

D_MODEL = 1024
N_A = 2
N_B = 2
DEPTH = 4
POOL_WINDOWS = (2, 4, 8, 16)
N_GROUPS = 4
GROUP_DIM = 256
POOL_HALO = 16
N_HEADS = 8
NOPE = 128
ROPE = 64
QK_DIM = 192
HEAD_PAD = 256
V_DIM = 128
Q_LORA = 256
KV_LORA = 512
CKV_PAD = 640
ROPE_THETA = 10000.0
CHUNK = 64
EPS = 1e-6
N_SHARD = 4
FF_SHARD = 704
FFN_ROWS = 1024
FFN_GRAD_ROWS = 2048
LANES = 128
SUBLANES = 8
ADAM_LR, ADAM_B1, ADAM_B2, ADAM_EPS, ADAM_WD, ADAM_STEP = 0.001, 0.9, 0.999, 1e-08, 0.01, 10
MESH = pl.DeviceIdType.MESH
ANY = pl.BlockSpec(memory_space=pl.ANY)
VMEM_SPEC = pl.BlockSpec(memory_space=pltpu.VMEM)


def _tile(n, pref):
    if n <= pref:
        return n
    t = pref - pref % SUBLANES
    while n % t:
        t -= SUBLANES
    return t


def _fold8(v):
    r, n = v.shape
    return v.reshape(r // SUBLANES, SUBLANES, n).sum(axis=0)


def _dot(a, b, dims):
    return lax.dot_general(a.astype(MXU_DTYPE), b.astype(MXU_DTYPE), (dims, ((), ())),
                           preferred_element_type=F32)


def _nn(a, b):
    return _dot(a, b, ((1,), (0,)))


def _nt(a, b):
    return _dot(a, b, ((1,), (1,)))


def _tn(a, b):
    return _dot(a, b, ((0,), (0,)))


def _mm(a, b, *, ta=False, tb=False, resid=None, out_dtype=F32, name):
    assert not (ta and tb)
    m, k = (a.shape[1], a.shape[0]) if ta else a.shape
    n = b.shape[0] if tb else b.shape[1]
    tm, tn = _tile(m, 512), _tile(n, 1024)

    def body(*refs):
        if resid is None:
            a_ref, b_ref, o_ref = refs
        else:
            a_ref, b_ref, r_ref, o_ref = refs
        acc = (_tn if ta else _nt if tb else _nn)(a_ref[...], b_ref[...])
        if resid is not None:
            acc = r_ref[...] + acc
        o_ref[...] = acc.astype(o_ref.dtype)

    in_specs = [pl.BlockSpec((k, tm), lambda i, j: (0, i)) if ta else pl.BlockSpec((tm, k), lambda i, j: (i, 0)),
                pl.BlockSpec((tn, k), lambda i, j: (j, 0)) if tb else pl.BlockSpec((k, tn), lambda i, j: (0, j))]
    args = [a, b]
    if resid is not None:
        in_specs.append(pl.BlockSpec((tm, tn), lambda i, j: (i, j)))
        args.append(resid)
    return pl.pallas_call(
        body, name=name, grid=(m // tm, n // tn), in_specs=in_specs,
        out_specs=pl.BlockSpec((tm, tn), lambda i, j: (i, j)),
        out_shape=jax.ShapeDtypeStruct((m, n), out_dtype))(*args)


def _mm_tn(a, b, *, name, at=False, out_dtype=F32):
    m = b.shape[0]
    k1 = a.shape[0] if at else a.shape[1]
    n = b.shape[1]
    tm, tn = _tile(m, 512), _tile(n, 1024)
    nm = m // tm

    def body(a_ref, b_ref, o_ref, acc):
        i = pl.program_id(1)

        @pl.when(i == 0)
        def _():
            acc[...] = jnp.zeros_like(acc)

        acc[...] += (_nn if at else _tn)(a_ref[...], b_ref[...])

        @pl.when(i == nm - 1)
        def _():
            o_ref[...] = acc[...].astype(o_ref.dtype)

    return pl.pallas_call(
        body, name=name, grid=(n // tn, nm),
        in_specs=[pl.BlockSpec((k1, tm), lambda j, i: (0, i)) if at else pl.BlockSpec((tm, k1), lambda j, i: (i, 0)),
                  pl.BlockSpec((tm, tn), lambda j, i: (i, j))],
        out_specs=pl.BlockSpec((k1, tn), lambda j, i: (0, j)),
        out_shape=jax.ShapeDtypeStruct((k1, n), out_dtype),
        scratch_shapes=[pltpu.VMEM((k1, tn), F32)])(a, b)


def _rms_fwd(x, g, *, n, n_valid=None, name):
    out_dtype = MXU_DTYPE
    rows = x.shape[0]
    tm = _tile(rows, 512)
    inv_n = 1.0 / (n_valid or n)

    def body(x_ref, g_ref, o_ref):
        xv = x_ref[...]
        r = lax.rsqrt(jnp.sum(xv * xv, axis=-1, keepdims=True) * inv_n + EPS)
        o_ref[...] = (xv * r * g_ref[...]).astype(o_ref.dtype)

    return pl.pallas_call(
        body, name=name, grid=(rows // tm,),
        in_specs=[pl.BlockSpec((tm, n), lambda i: (i, 0)), pl.BlockSpec((1, n), lambda i: (0, 0))],
        out_specs=pl.BlockSpec((tm, n), lambda i: (i, 0)),
        out_shape=jax.ShapeDtypeStruct((rows, n), out_dtype))(x, g)


def _rms_bwd_math(xv, gv, dyv, inv_n):
    r = lax.rsqrt(jnp.sum(xv * xv, axis=-1, keepdims=True) * inv_n + EPS)
    xh = xv * r
    gy = dyv * gv
    dx = r * (gy - xh * (jnp.sum(gy * xh, axis=-1, keepdims=True) * inv_n))
    return dx, dyv * xh


def _rms_bwd(x, g, dy, *, n, dx_in=None, after=None, name):
    rows = x.shape[0]
    tm = _tile(rows, 512)
    inv_n = 1.0 / n

    def body(*refs):
        if after is not None:
            refs = refs[:-3] + refs[-2:]
        if dx_in is None:
            x_ref, g_ref, dy_ref, dx_ref, dg_ref = refs
        else:
            x_ref, g_ref, dy_ref, din_ref, dx_ref, dg_ref = refs
        dx, dgc = _rms_bwd_math(x_ref[...], g_ref[...], dy_ref[...], inv_n)
        if dx_in is not None:
            dx = din_ref[...] + dx
        dx_ref[...] = dx

        @pl.when(pl.program_id(0) == 0)
        def _():
            dg_ref[...] = jnp.zeros_like(dg_ref)

        dg_ref[...] += _fold8(dgc)

    row_spec = pl.BlockSpec((tm, n), lambda i: (i, 0))
    in_specs = [row_spec, pl.BlockSpec((1, n), lambda i: (0, 0)), row_spec]
    args = [x, g, dy]
    if dx_in is not None:
        in_specs.append(row_spec)
        args.append(dx_in)
    if after is not None:
        in_specs.append(ANY)
        args.append(after)
    return pl.pallas_call(
        body, name=name, grid=(rows // tm,), in_specs=in_specs,
        out_specs=[row_spec, pl.BlockSpec((SUBLANES, n), lambda i: (0, 0))],
        out_shape=[jax.ShapeDtypeStruct((rows, n), F32), jax.ShapeDtypeStruct((SUBLANES, n), F32)])(*args)


def _pool_counts(t0, tm, w):
    t = t0 + lax.broadcasted_iota(jnp.int32, (tm, 1), 0)
    return jnp.minimum(t + 1, w).astype(F32)


def _rms_pool_fwd(x, g, *, name):
    s, d = x.shape
    tm = _tile(s, 512)
    hb = tm // POOL_HALO

    def body(x_ref, halo_ref, g_ref, o_ref):
        i = pl.program_id(0)
        gv = g_ref[...]

        def norm(v):
            return v * lax.rsqrt(jnp.mean(v * v, axis=-1, keepdims=True) + EPS) * gv

        h = norm(x_ref[...])
        halo = norm(halo_ref[...]) * (i > 0).astype(F32)
        hh = jnp.concatenate([halo, h], axis=0)
        rows = tm + POOL_HALO
        for gi, w in enumerate(POOL_WINDOWS):
            cols = slice(gi * GROUP_DIM, (gi + 1) * GROUP_DIM)
            acc = hh[:, cols]
            k = 1
            while k < w:
                acc = acc + pltpu.roll(acc, k, 0)
                k *= 2
            win = acc[POOL_HALO:rows]
            o_ref[:, cols] = (win / _pool_counts(i * tm, tm, w) - h[:, cols]).astype(o_ref.dtype)

    return pl.pallas_call(
        body, name=name, grid=(s // tm,),
        in_specs=[pl.BlockSpec((tm, d), lambda i: (i, 0)),
                  pl.BlockSpec((POOL_HALO, d), lambda i: (jnp.maximum(i * hb - 1, 0), 0)),
                  pl.BlockSpec((1, d), lambda i: (0, 0))],
        out_specs=pl.BlockSpec((tm, d), lambda i: (i, 0)),
        out_shape=jax.ShapeDtypeStruct((s, d), MXU_DTYPE))(x, x, g)


def _rms_pool_bwd(x, g, dd, dx_in, *, name):
    s, d = x.shape
    tm = _tile(s, 512)
    hb = tm // POOL_HALO
    nt = s // tm

    def body(x_ref, g_ref, dd_ref, halo_ref, din_ref, dx_ref, dg_ref):
        i = pl.program_id(0)
        ddv = dd_ref[...]
        halo = halo_ref[...] * (i < nt - 1).astype(F32)
        rows = tm + POOL_HALO
        parts = []
        for gi, w in enumerate(POOL_WINDOWS):
            cols = slice(gi * GROUP_DIM, (gi + 1) * GROUP_DIM)
            acc = jnp.concatenate([ddv[:, cols] / _pool_counts(i * tm, tm, w), halo[:, cols] * (1.0 / w)], axis=0)
            k = 1
            while k < w:
                acc = acc + pltpu.roll(acc, rows - k, 0)
                k *= 2
            parts.append(acc[0:tm] - ddv[:, cols])
        dh = jnp.concatenate(parts, axis=1)
        dx, dgc = _rms_bwd_math(x_ref[...], g_ref[...], dh, 1.0 / d)
        dx_ref[...] = din_ref[...] + dx

        @pl.when(i == 0)
        def _():
            dg_ref[...] = jnp.zeros_like(dg_ref)

        dg_ref[...] += _fold8(dgc)

    row_spec = pl.BlockSpec((tm, d), lambda i: (i, 0))
    return pl.pallas_call(
        body, name=name, grid=(nt,),
        in_specs=[row_spec, pl.BlockSpec((1, d), lambda i: (0, 0)), row_spec,
                  pl.BlockSpec((POOL_HALO, d), lambda i: (jnp.minimum((i + 1) * hb, s // POOL_HALO - 1), 0)),
                  row_spec],
        out_specs=[row_spec, pl.BlockSpec((SUBLANES, d), lambda i: (0, 0))],
        out_shape=[jax.ShapeDtypeStruct((s, d), F32), jax.ShapeDtypeStruct((SUBLANES, d), F32)])(x, g, dd, dd, dx_in)


def _pool_mm_fwd(dpool, w, b, scale, x, *, name):
    s, d = x.shape
    tm = _tile(s, 512)

    def body(d_ref, w_ref, b_ref, s_ref, x_ref, o_ref):
        for gi in range(N_GROUPS):
            cols = slice(gi * GROUP_DIM, (gi + 1) * GROUP_DIM)
            y = _nn(d_ref[:, cols], w_ref[gi]) + b_ref[:, cols]
            o_ref[:, cols] = x_ref[:, cols] + y * s_ref[:, cols]

    row_spec = pl.BlockSpec((tm, d), lambda i: (i, 0))
    vec_spec = pl.BlockSpec((1, d), lambda i: (0, 0))
    return pl.pallas_call(
        body, name=name, grid=(s // tm,),
        in_specs=[row_spec, pl.BlockSpec((N_GROUPS, GROUP_DIM, GROUP_DIM), lambda i: (0, 0, 0)), vec_spec, vec_spec, row_spec],
        out_specs=row_spec, out_shape=jax.ShapeDtypeStruct((s, d), F32))(dpool, w, b, scale, x)


def _pool_mm_bwd(dpool, w, b, scale, dx, *, name):
    s, d = dx.shape
    tm = _tile(s, 512)

    def body(d_ref, w_ref, b_ref, s_ref, dx_ref, dd_ref, dw_ref, db_ref, ds_ref):
        @pl.when(pl.program_id(0) == 0)
        def _():
            dw_ref[...] = jnp.zeros_like(dw_ref)
            db_ref[...] = jnp.zeros_like(db_ref)
            ds_ref[...] = jnp.zeros_like(ds_ref)

        for gi in range(N_GROUPS):
            cols = slice(gi * GROUP_DIM, (gi + 1) * GROUP_DIM)
            dg = d_ref[:, cols]
            y = _nn(dg, w_ref[gi]) + b_ref[:, cols]
            dxg = dx_ref[:, cols]
            dy = dxg * s_ref[:, cols]
            ds_ref[:, cols] += _fold8(dxg * y)
            db_ref[:, cols] += _fold8(dy)
            dw_ref[gi] += _tn(dg, dy)
            dd_ref[:, cols] = _nt(dy, w_ref[gi])

    row_spec = pl.BlockSpec((tm, d), lambda i: (i, 0))
    vec_spec = pl.BlockSpec((1, d), lambda i: (0, 0))
    w_spec = pl.BlockSpec((N_GROUPS, GROUP_DIM, GROUP_DIM), lambda i: (0, 0, 0))
    part_spec = pl.BlockSpec((SUBLANES, d), lambda i: (0, 0))
    return pl.pallas_call(
        body, name=name, grid=(s // tm,),
        in_specs=[row_spec, w_spec, vec_spec, vec_spec, row_spec],
        out_specs=[row_spec, w_spec, part_spec, part_spec],
        out_shape=[jax.ShapeDtypeStruct((s, d), F32), jax.ShapeDtypeStruct((N_GROUPS, GROUP_DIM, GROUP_DIM), F32),
                   jax.ShapeDtypeStruct((SUBLANES, d), F32), jax.ShapeDtypeStruct((SUBLANES, d), F32)])(dpool, w, b, scale, dx)


def _ffn_up(hf, wg, wu, *, name):
    s, d = hf.shape
    tm = _tile(s, FFN_ROWS)

    def body(h_ref, wg_ref, wu_ref, a_ref, b_ref, u_ref):
        hv = h_ref[...]
        a = _nn(hv, wg_ref[...])
        b = _nn(hv, wu_ref[...])
        a_ref[...] = a.astype(a_ref.dtype)
        b_ref[...] = b.astype(b_ref.dtype)
        u_ref[...] = (a * (1.0 / (1.0 + jnp.exp(-a))) * b).astype(u_ref.dtype)

    w_spec = pl.BlockSpec((None, d, FF_SHARD), lambda j, i: (j, 0, 0))
    h_spec = pl.BlockSpec((None, tm, FF_SHARD), lambda j, i: (j, i, 0))
    hid = (N_SHARD, s, FF_SHARD)
    return pl.pallas_call(
        body, name=name, grid=(N_SHARD, s // tm),
        in_specs=[pl.BlockSpec((tm, d), lambda j, i: (i, 0)), w_spec, w_spec],
        out_specs=[h_spec, h_spec, h_spec],
        out_shape=[jax.ShapeDtypeStruct(hid, SAVED_DTYPE), jax.ShapeDtypeStruct(hid, SAVED_DTYPE),
                   jax.ShapeDtypeStruct(hid, MXU_DTYPE)])(hf, wg, wu)


def _ffn_down(u, wd, x, *, name):
    s, d = x.shape
    tm = _tile(s, 1024)

    def body(u_ref, w_ref, x_ref, o_ref):
        j = pl.program_id(1)

        @pl.when(j == 0)
        def _():
            o_ref[...] = x_ref[...]

        o_ref[...] += _nn(u_ref[...], w_ref[...])

    return pl.pallas_call(
        body, name=name, grid=(s // tm, N_SHARD),
        in_specs=[pl.BlockSpec((None, tm, FF_SHARD), lambda i, j: (j, i, 0)),
                  pl.BlockSpec((None, FF_SHARD, d), lambda i, j: (j, 0, 0)),
                  pl.BlockSpec((tm, d), lambda i, j: (i, 0))],
        out_specs=pl.BlockSpec((tm, d), lambda i, j: (i, 0)),
        out_shape=jax.ShapeDtypeStruct((s, d), F32))(u, wd, x)


def _ffn_bwd_hidden(dy, wd, a, b, *, name):
    s, d = dy.shape
    tm = _tile(s, FFN_ROWS)

    def body(dy_ref, w_ref, a_ref, b_ref, da_ref, db_ref):
        du = _nt(dy_ref[...], w_ref[...])
        av, bv = a_ref[...].astype(F32), b_ref[...].astype(F32)
        sg = 1.0 / (1.0 + jnp.exp(-av))
        da_ref[...] = (du * bv * (sg * (1.0 + av * (1.0 - sg)))).astype(da_ref.dtype)
        db_ref[...] = (du * (av * sg)).astype(db_ref.dtype)

    h_spec = pl.BlockSpec((None, tm, FF_SHARD), lambda j, i: (j, i, 0))
    hid = jax.ShapeDtypeStruct((N_SHARD, s, FF_SHARD), MXU_DTYPE)
    return pl.pallas_call(
        body, name=name, grid=(N_SHARD, s // tm),
        in_specs=[pl.BlockSpec((tm, d), lambda j, i: (i, 0)),
                  pl.BlockSpec((None, FF_SHARD, d), lambda j, i: (j, 0, 0)), h_spec, h_spec],
        out_specs=[h_spec, h_spec], out_shape=[hid, hid])(dy, wd, a, b)


def _ffn_bwd_dwd(u, dy, *, name):
    s, d = dy.shape
    tm = _tile(s, FFN_GRAD_ROWS)
    nm = s // tm

    def body(u_ref, dy_ref, o_ref, acc):
        i = pl.program_id(1)

        @pl.when(i == 0)
        def _():
            acc[...] = jnp.zeros_like(acc)

        acc[...] += _tn(u_ref[...], dy_ref[...])

        @pl.when(i == nm - 1)
        def _():
            o_ref[...] = acc[...].astype(o_ref.dtype)

    return pl.pallas_call(
        body, name=name, grid=(N_SHARD, nm),
        in_specs=[pl.BlockSpec((None, tm, FF_SHARD), lambda j, i: (j, i, 0)), pl.BlockSpec((tm, d), lambda j, i: (i, 0))],
        out_specs=pl.BlockSpec((None, FF_SHARD, d), lambda j, i: (j, 0, 0)),
        out_shape=jax.ShapeDtypeStruct((N_SHARD, FF_SHARD, d), WIRE_DTYPE),
        scratch_shapes=[pltpu.VMEM((FF_SHARD, d), F32)])(u, dy)


def _ffn_bwd_dwgu(hf, da, db, *, name):
    s, d = hf.shape
    tm = _tile(s, FFN_GRAD_ROWS)
    nm = s // tm

    def body(h_ref, da_ref, db_ref, og_ref, ou_ref, accg, accu):
        i = pl.program_id(1)

        @pl.when(i == 0)
        def _():
            accg[...] = jnp.zeros_like(accg)
            accu[...] = jnp.zeros_like(accu)

        hv = h_ref[...]
        accg[...] += _tn(hv, da_ref[...])
        accu[...] += _tn(hv, db_ref[...])

        @pl.when(i == nm - 1)
        def _():
            og_ref[...] = accg[...].astype(og_ref.dtype)
            ou_ref[...] = accu[...].astype(ou_ref.dtype)

    h_spec = pl.BlockSpec((None, tm, FF_SHARD), lambda j, i: (j, i, 0))
    w_spec = pl.BlockSpec((None, d, FF_SHARD), lambda j, i: (j, 0, 0))
    grad = jax.ShapeDtypeStruct((N_SHARD, d, FF_SHARD), WIRE_DTYPE)
    return pl.pallas_call(
        body, name=name, grid=(N_SHARD, nm),
        in_specs=[pl.BlockSpec((tm, d), lambda j, i: (i, 0)), h_spec, h_spec],
        out_specs=[w_spec, w_spec], out_shape=[grad, grad],
        scratch_shapes=[pltpu.VMEM((d, FF_SHARD), F32), pltpu.VMEM((d, FF_SHARD), F32)])(hf, da, db)


def _ffn_bwd_dh(da, db, wg, wu, *, name):
    s = da.shape[1]
    d = wg.shape[1]
    tm = _tile(s, 1024)

    def body(da_ref, db_ref, wg_ref, wu_ref, o_ref):
        j = pl.program_id(1)

        @pl.when(j == 0)
        def _():
            o_ref[...] = jnp.zeros_like(o_ref)

        o_ref[...] += _nt(da_ref[...], wg_ref[...]) + _nt(db_ref[...], wu_ref[...])

    h_spec = pl.BlockSpec((None, tm, FF_SHARD), lambda i, j: (j, i, 0))
    w_spec = pl.BlockSpec((None, d, FF_SHARD), lambda i, j: (j, 0, 0))
    return pl.pallas_call(
        body, name=name, grid=(s // tm, N_SHARD),
        in_specs=[h_spec, h_spec, w_spec, w_spec],
        out_specs=pl.BlockSpec((tm, d), lambda i, j: (i, 0)),
        out_shape=jax.ShapeDtypeStruct((s, d), F32))(da, db, wg, wu)


def _rope_tables(pos, inv, *, name):
    s = pos.shape[0]
    tm = _tile(s, 512)
    half = ROPE // 2

    def body(p_ref, i_ref, c_ref, s_ref):
        ang = p_ref[...] * i_ref[...]
        lane = lax.broadcasted_iota(jnp.int32, ang.shape, 1)
        live = lane < ROPE
        c_ref[...] = jnp.where(live, jnp.cos(ang), 0.0)
        sn = jnp.sin(ang)
        s_ref[...] = jnp.where(live, jnp.where(lane < half, -sn, sn), 0.0)

    out = jax.ShapeDtypeStruct((s, LANES), F32)
    return pl.pallas_call(
        body, name=name, grid=(s // tm,),
        in_specs=[pl.BlockSpec((tm, 1), lambda i: (i, 0)), pl.BlockSpec((1, LANES), lambda i: (0, 0))],
        out_specs=[pl.BlockSpec((tm, LANES), lambda i: (i, 0))] * 2, out_shape=[out, out])(pos, inv)


def _swap_halves(v):
    half = ROPE // 2
    lane = lax.broadcasted_iota(jnp.int32, v.shape, 1)
    return jnp.where(lane < half, pltpu.roll(v, LANES - half, 1), pltpu.roll(v, half, 1))


def _head_norm_rope_fwd(raw, g, cos, sin, *, name):
    s = raw.shape[0]
    tm = _tile(s, 256)
    width = N_HEADS * HEAD_PAD

    def body(x_ref, g_ref, c_ref, s_ref, o_ref):
        cv, sv = c_ref[...], s_ref[...]
        for h in range(N_HEADS):
            lo = h * HEAD_PAD
            xa = x_ref[:, lo:lo + NOPE]
            xb = x_ref[:, lo + NOPE:lo + HEAD_PAD]
            ms = (jnp.sum(xa * xa, axis=-1, keepdims=True) + jnp.sum(xb * xb, axis=-1, keepdims=True)) * (1.0 / QK_DIM)
            r = lax.rsqrt(ms + EPS)
            o_ref[:, lo:lo + NOPE] = (xa * r * g_ref[:, 0:NOPE]).astype(o_ref.dtype)
            yb = xb * r * g_ref[:, NOPE:HEAD_PAD]
            o_ref[:, lo + NOPE:lo + HEAD_PAD] = (yb * cv + _swap_halves(yb) * sv).astype(o_ref.dtype)

    row_spec = pl.BlockSpec((tm, width), lambda i: (i, 0))
    tab_spec = pl.BlockSpec((tm, LANES), lambda i: (i, 0))
    return pl.pallas_call(
        body, name=name, grid=(s // tm,),
        in_specs=[row_spec, pl.BlockSpec((1, HEAD_PAD), lambda i: (0, 0)), tab_spec, tab_spec],
        out_specs=row_spec, out_shape=jax.ShapeDtypeStruct((s, width), MXU_DTYPE))(raw, g, cos, sin)


def _head_norm_rope_bwd(raw, g, cos, sin, dout, *, name):
    s = raw.shape[0]
    tm = _tile(s, 256)
    width = N_HEADS * HEAD_PAD

    def body(x_ref, g_ref, c_ref, s_ref, do_ref, dx_ref, dg_ref):
        @pl.when(pl.program_id(0) == 0)
        def _():
            dg_ref[...] = jnp.zeros_like(dg_ref)

        cv, sv = c_ref[...], s_ref[...]
        ga, gb = g_ref[:, 0:NOPE], g_ref[:, NOPE:HEAD_PAD]
        for h in range(N_HEADS):
            lo = h * HEAD_PAD
            xa = x_ref[:, lo:lo + NOPE]
            xb = x_ref[:, lo + NOPE:lo + HEAD_PAD]
            dya = do_ref[:, lo:lo + NOPE]
            dob = do_ref[:, lo + NOPE:lo + HEAD_PAD]
            dyb = dob * cv + _swap_halves(dob * sv)
            ms = (jnp.sum(xa * xa, axis=-1, keepdims=True) + jnp.sum(xb * xb, axis=-1, keepdims=True)) * (1.0 / QK_DIM)
            r = lax.rsqrt(ms + EPS)
            xha, xhb = xa * r, xb * r
            gya, gyb = dya * ga, dyb * gb
            dot = (jnp.sum(gya * xha, axis=-1, keepdims=True) + jnp.sum(gyb * xhb, axis=-1, keepdims=True)) * (1.0 / QK_DIM)
            dx_ref[:, lo:lo + NOPE] = r * (gya - xha * dot)
            dx_ref[:, lo + NOPE:lo + HEAD_PAD] = r * (gyb - xhb * dot)
            dg_ref[:, 0:NOPE] += _fold8(dya * xha)
            dg_ref[:, NOPE:HEAD_PAD] += _fold8(dyb * xhb)

    row_spec = pl.BlockSpec((tm, width), lambda i: (i, 0))
    tab_spec = pl.BlockSpec((tm, LANES), lambda i: (i, 0))
    return pl.pallas_call(
        body, name=name, grid=(s // tm,),
        in_specs=[row_spec, pl.BlockSpec((1, HEAD_PAD), lambda i: (0, 0)), tab_spec, tab_spec, row_spec],
        out_specs=[row_spec, pl.BlockSpec((SUBLANES, HEAD_PAD), lambda i: (0, 0))],
        out_shape=[jax.ShapeDtypeStruct((s, width), F32), jax.ShapeDtypeStruct((SUBLANES, HEAD_PAD), F32)])(raw, g, cos, sin, dout)


def _k_assemble(kn, ckv, *, name):
    s = kn.shape[0]
    tm = _tile(s, 512)
    width = N_HEADS * HEAD_PAD

    def body(kn_ref, pe_ref, o_ref):
        pe = pe_ref[...]
        for h in range(N_HEADS):
            o_ref[:, h * HEAD_PAD:h * HEAD_PAD + NOPE] = kn_ref[:, h * NOPE:(h + 1) * NOPE]
            o_ref[:, h * HEAD_PAD + NOPE:(h + 1) * HEAD_PAD] = pe

    return pl.pallas_call(
        body, name=name, grid=(s // tm,),
        in_specs=[pl.BlockSpec((tm, N_HEADS * NOPE), lambda i: (i, 0)),
                  pl.BlockSpec((tm, LANES), lambda i: (i, KV_LORA // LANES))],
        out_specs=pl.BlockSpec((tm, width), lambda i: (i, 0)),
        out_shape=jax.ShapeDtypeStruct((s, width), F32))(kn, ckv)


def _k_disassemble(dk_raw, *, name):
    s = dk_raw.shape[0]
    tm = _tile(s, 512)
    width = N_HEADS * HEAD_PAD

    def body(dk_ref, dkn_ref, dpe_ref):
        pe = dk_ref[:, NOPE:HEAD_PAD]
        for h in range(N_HEADS):
            dkn_ref[:, h * NOPE:(h + 1) * NOPE] = dk_ref[:, h * HEAD_PAD:h * HEAD_PAD + NOPE]
            if h:
                pe = pe + dk_ref[:, h * HEAD_PAD + NOPE:(h + 1) * HEAD_PAD]
        dpe_ref[...] = pe

    return pl.pallas_call(
        body, name=name, grid=(s // tm,),
        in_specs=[pl.BlockSpec((tm, width), lambda i: (i, 0))],
        out_specs=[pl.BlockSpec((tm, N_HEADS * NOPE), lambda i: (i, 0)), pl.BlockSpec((tm, LANES), lambda i: (i, 0))],
        out_shape=[jax.ShapeDtypeStruct((s, N_HEADS * NOPE), F32), jax.ShapeDtypeStruct((s, LANES), F32)])(dk_raw)


ATTN_SCALE = 1.0 / math.sqrt(QK_DIM)
MASKED = -1e30


ATTN_TILE = 512
ATTN_HEADS = 4


def _chunk_mask(q0, k0, shape, q_axis):
    qpos = q0 + lax.broadcasted_iota(jnp.int32, shape, q_axis)
    kpos = k0 + lax.broadcasted_iota(jnp.int32, shape, 1 - q_axis)
    return kpos // CHUNK <= qpos // CHUNK


LOG2E = math.log2(math.e)
SCORE_LOG2 = ATTN_SCALE * LOG2E


def _causal_pairs(n, by_key):
    if by_key:
        pairs = [(i, j) for j in range(n) for i in range(j, n)]
    else:
        pairs = [(i, j) for i in range(n) for j in range(i + 1)]
    return jnp.asarray([p[0] for p in pairs], jnp.int32), jnp.asarray([p[1] for p in pairs], jnp.int32)


def _attn_fwd(q, k, vt, *, name):
    s = q.shape[0]
    t = _tile(s, ATTN_TILE)
    n = s // t
    qi_tab, kj_tab = _causal_pairs(n, by_key=False)

    hg = ATTN_HEADS

    def body(qi_ref, kj_ref, q_ref, k_ref, vt_ref, o_ref, lse_ref, m_sc, l_sc, acc):
        pair = pl.program_id(1)
        qi, kj = qi_ref[pair], kj_ref[pair]

        @pl.when(kj == 0)
        def _():
            m_sc[...] = jnp.full_like(m_sc, MASKED)
            l_sc[...] = jnp.zeros_like(l_sc)
            acc[...] = jnp.zeros_like(acc)

        def step(masked):
            for g in range(hg):
                qk, vr = slice(g * HEAD_PAD, (g + 1) * HEAD_PAD), slice(g * V_DIM, (g + 1) * V_DIM)
                st = _nt(k_ref[:, qk], q_ref[:, qk])
                if masked:
                    st = jnp.where(_chunk_mask(qi * t, kj * t, (t, t), 1), st, MASKED)
                m_prev = m_sc[g]
                m_new = jnp.maximum(m_prev, jnp.max(st, axis=0, keepdims=True) * SCORE_LOG2)
                alpha = jnp.exp2(m_prev - m_new)
                pt = jnp.exp2(st * SCORE_LOG2 - m_new)
                l_new = alpha * l_sc[g] + jnp.sum(pt, axis=0, keepdims=True)
                a_new = alpha * acc[vr, :] + _nn(vt_ref[vr, :], pt)
                l_sc[g] = l_new
                acc[vr, :] = a_new
                m_sc[g] = m_new
                if masked:
                    o_ref[vr, :] = a_new / l_new
                    lse_ref[g] = m_new + jnp.log(l_new) * LOG2E

        @pl.when(kj < qi)
        def _():
            step(False)

        @pl.when(kj == qi)
        def _():
            step(True)

    return pl.pallas_call(
        body, name=name,
        grid_spec=pltpu.PrefetchScalarGridSpec(
            num_scalar_prefetch=2, grid=(N_HEADS // hg, int(qi_tab.shape[0])),
            in_specs=[pl.BlockSpec((t, hg * HEAD_PAD), lambda h, p, qi, kj: (qi[p], h)),
                      pl.BlockSpec((t, hg * HEAD_PAD), lambda h, p, qi, kj: (kj[p], h)),
                      pl.BlockSpec((hg * V_DIM, t), lambda h, p, qi, kj: (h, kj[p]))],
            out_specs=[pl.BlockSpec((hg * V_DIM, t), lambda h, p, qi, kj: (h, qi[p])),
                       pl.BlockSpec((hg, 1, t), lambda h, p, qi, kj: (h, 0, qi[p]))],
            scratch_shapes=[pltpu.VMEM((hg, 1, t), F32), pltpu.VMEM((hg, 1, t), F32), pltpu.VMEM((hg * V_DIM, t), F32)]),
        out_shape=[jax.ShapeDtypeStruct((N_HEADS * V_DIM, s), F32), jax.ShapeDtypeStruct((N_HEADS, 1, s), F32)])(qi_tab, kj_tab, q, k, vt)


def _attn_delta(ot, dot, *, name):
    s = ot.shape[1]
    t = _tile(s, 1024)

    def body(o_ref, do_ref, d_ref):
        d_ref[...] = jnp.sum(o_ref[...] * do_ref[...], axis=0, keepdims=True)

    blk = pl.BlockSpec((V_DIM, t), lambda h, i: (h, i))
    return pl.pallas_call(
        body, name=name, grid=(N_HEADS, s // t), in_specs=[blk, blk],
        out_specs=pl.BlockSpec((None, 1, t), lambda h, i: (h, 0, i)),
        out_shape=jax.ShapeDtypeStruct((N_HEADS, 1, s), F32))(ot, dot)


def _attn_bwd_dq(q, k, v, do, lse_col, delta_col, *, name):
    s = q.shape[0]
    t = _tile(s, ATTN_TILE)
    n = s // t
    qi_tab, kj_tab = _causal_pairs(n, by_key=False)

    hg = ATTN_HEADS

    def body(qi_ref, kj_ref, q_ref, k_ref, v_ref, do_ref, lse_ref, dl_ref, dq_ref, acc):
        pair = pl.program_id(1)
        qi, kj = qi_ref[pair], kj_ref[pair]

        @pl.when(kj == 0)
        def _():
            acc[...] = jnp.zeros_like(acc)

        def step(masked):
            for g in range(hg):
                qk, vc = slice(g * HEAD_PAD, (g + 1) * HEAD_PAD), slice(g * V_DIM, (g + 1) * V_DIM)
                kv = k_ref[:, qk]
                sc = _nt(q_ref[:, qk], kv)
                if masked:
                    sc = jnp.where(_chunk_mask(qi * t, kj * t, (t, t), 0), sc, MASKED)
                p = jnp.exp2(sc * SCORE_LOG2 - lse_ref[g])
                dp = _nt(do_ref[:, vc], v_ref[:, vc])
                total = acc[:, qk] + _nn(p * (dp - dl_ref[g]), kv)
                acc[:, qk] = total
                if masked:
                    dq_ref[:, qk] = total * ATTN_SCALE

        @pl.when(kj < qi)
        def _():
            step(False)

        @pl.when(kj == qi)
        def _():
            step(True)

    col = pl.BlockSpec((hg, t, 1), lambda h, p, qi, kj: (h, qi[p], 0))
    return pl.pallas_call(
        body, name=name,
        grid_spec=pltpu.PrefetchScalarGridSpec(
            num_scalar_prefetch=2, grid=(N_HEADS // hg, int(qi_tab.shape[0])),
            in_specs=[pl.BlockSpec((t, hg * HEAD_PAD), lambda h, p, qi, kj: (qi[p], h)),
                      pl.BlockSpec((t, hg * HEAD_PAD), lambda h, p, qi, kj: (kj[p], h)),
                      pl.BlockSpec((t, hg * V_DIM), lambda h, p, qi, kj: (kj[p], h)),
                      pl.BlockSpec((t, hg * V_DIM), lambda h, p, qi, kj: (qi[p], h)), col, col],
            out_specs=pl.BlockSpec((t, hg * HEAD_PAD), lambda h, p, qi, kj: (qi[p], h)),
            scratch_shapes=[pltpu.VMEM((t, hg * HEAD_PAD), F32)]),
        out_shape=jax.ShapeDtypeStruct((s, N_HEADS * HEAD_PAD), F32))(qi_tab, kj_tab, q, k, v, do, lse_col, delta_col)


def _attn_bwd_dkv(q, k, v, do, lse_row, delta_row, dk_in, dv_in, *, name):
    s = q.shape[0]
    t = _tile(s, ATTN_TILE)
    n = s // t
    has_in = dk_in is not None
    hg = ATTN_HEADS
    qi_tab, kj_tab = _causal_pairs(n, by_key=True)

    def body(qi_ref, kj_ref, *refs):
        if has_in:
            q_ref, k_ref, v_ref, do_ref, lse_ref, dl_ref, dki_ref, dvi_ref, dk_ref, dv_ref, acck, accv = refs
        else:
            q_ref, k_ref, v_ref, do_ref, lse_ref, dl_ref, dk_ref, dv_ref, acck, accv = refs
        pair = pl.program_id(1)
        qi, kj = qi_ref[pair], kj_ref[pair]

        def step(masked):
            for g in range(hg):
                qk, vc = slice(g * HEAD_PAD, (g + 1) * HEAD_PAD), slice(g * V_DIM, (g + 1) * V_DIM)
                qv, dov = q_ref[:, qk], do_ref[:, vc]
                st = _nt(k_ref[:, qk], qv)
                if masked:
                    st = jnp.where(_chunk_mask(qi * t, kj * t, (t, t), 1), st, MASKED)
                pt = jnp.exp2(st * SCORE_LOG2 - lse_ref[g])
                accv[:, vc] += _nn(pt, dov)
                dpt = _nt(v_ref[:, vc], dov)
                acck[:, qk] += _nn(pt * (dpt - dl_ref[g]), qv)

        @pl.when(qi == kj)
        def _():
            acck[...] = jnp.zeros_like(acck)
            accv[...] = jnp.zeros_like(accv)
            step(True)

        @pl.when(qi > kj)
        def _():
            step(False)

        @pl.when(qi == n - 1)
        def _():
            dk = acck[...] * ATTN_SCALE
            dv = accv[...]
            if has_in:
                dk = dki_ref[...] + dk
                dv = dvi_ref[...] + dv
            dk_ref[...] = dk
            dv_ref[...] = dv

    row = pl.BlockSpec((hg, 1, t), lambda h, p, qi, kj: (h, 0, qi[p]))
    k_spec = pl.BlockSpec((t, hg * HEAD_PAD), lambda h, p, qi, kj: (kj[p], h))
    v_spec = pl.BlockSpec((t, hg * V_DIM), lambda h, p, qi, kj: (kj[p], h))
    in_specs = [pl.BlockSpec((t, hg * HEAD_PAD), lambda h, p, qi, kj: (qi[p], h)), k_spec, v_spec,
                pl.BlockSpec((t, hg * V_DIM), lambda h, p, qi, kj: (qi[p], h)), row, row]
    args = [q, k, v, do, lse_row, delta_row]
    if has_in:
        in_specs += [k_spec, v_spec]
        args += [dk_in, dv_in]
    return pl.pallas_call(
        body, name=name,
        grid_spec=pltpu.PrefetchScalarGridSpec(
            num_scalar_prefetch=2, grid=(N_HEADS // hg, int(qi_tab.shape[0])), in_specs=in_specs, out_specs=[k_spec, v_spec],
            scratch_shapes=[pltpu.VMEM((t, hg * HEAD_PAD), F32), pltpu.VMEM((t, hg * V_DIM), F32)]),
        out_shape=[jax.ShapeDtypeStruct((s, N_HEADS * HEAD_PAD), F32), jax.ShapeDtypeStruct((s, N_HEADS * V_DIM), F32)])(qi_tab, kj_tab, *args)


def _loss_head(y, target, *, name):
    s, d = y.shape
    tm = _tile(s, 512)

    def body(y_ref, t_ref, dy_ref, l_ref):
        @pl.when(pl.program_id(0) == 0)
        def _():
            l_ref[...] = jnp.zeros_like(l_ref)

        err = y_ref[...] - t_ref[...]
        dy_ref[...] = err * (1.0 / d)
        sq = _fold8(err * err)
        part = sq[:, 0:LANES]
        for cb in range(1, d // LANES):
            part = part + sq[:, cb * LANES:(cb + 1) * LANES]
        l_ref[...] += part * (0.5 / d)

    row_spec = pl.BlockSpec((tm, d), lambda i: (i, 0))
    return pl.pallas_call(
        body, name=name, grid=(s // tm,), in_specs=[row_spec, row_spec],
        out_specs=[row_spec, pl.BlockSpec((SUBLANES, LANES), lambda i: (0, 0))],
        out_shape=[jax.ShapeDtypeStruct((s, d), F32), jax.ShapeDtypeStruct((SUBLANES, LANES), F32)])(y, target)


ADAMW_ROWS = 512


def _adamw_math(w, m, v, g):
    mn = ADAM_B1 * m + (1.0 - ADAM_B1) * g
    vn = ADAM_B2 * v + (1.0 - ADAM_B2) * (g * g)
    m_hat = mn / (1.0 - ADAM_B1 ** ADAM_STEP)
    v_hat = vn / (1.0 - ADAM_B2 ** ADAM_STEP)
    return -ADAM_LR * (m_hat / (jnp.sqrt(v_hat) + ADAM_EPS) + ADAM_WD * w), mn, vn


def _adamw_vectors(ws, ms, vs, gs, *, name):
    n = len(ws)

    def body(*refs):
        ins, outs = refs[:4 * n], refs[4 * n:]
        for a in range(n):
            g = ins[3 * n + a][...]
            outs[a][...] = g
            outs[n + a][...], outs[2 * n + a][...], outs[3 * n + a][...] = _adamw_math(ins[a][...], ins[n + a][...], ins[2 * n + a][...], g)

    shapes = [jax.ShapeDtypeStruct(w.shape, F32) for w in ws]
    out = pl.pallas_call(body, name=name, in_specs=[VMEM_SPEC] * (4 * n), out_specs=[VMEM_SPEC] * (4 * n),
                         out_shape=shapes * 4)(*ws, *ms, *vs, *gs)
    return out[:n], out[n:2 * n], out[2 * n:3 * n], out[3 * n:]


def _adamw(w, m, v, g_parts, *, name):
    rows, cols = w.shape
    tm = _tile(rows, ADAMW_ROWS)
    n_parts = len(g_parts)

    def body(*refs):
        w_ref, m_ref, v_ref = refs[:3]
        g_refs = refs[3:3 + n_parts]
        g_out, d_out, m_out, v_out = refs[3 + n_parts:]
        g = g_refs[0][...]
        for r in g_refs[1:]:
            g = g + r[...]
        g_out[...] = g
        d_out[...], m_out[...], v_out[...] = _adamw_math(w_ref[...], m_ref[...], v_ref[...], g)

    spec = pl.BlockSpec((tm, cols), lambda i: (i, 0))
    out = jax.ShapeDtypeStruct((rows, cols), F32)
    return pl.pallas_call(
        body, name=name, grid=(rows // tm,), in_specs=[spec] * (3 + n_parts),
        out_specs=[spec] * 4, out_shape=[out] * 4)(w, m, v, *g_parts)


def _sum_slots(parts, *, name):
    _, rows, cols = parts.shape
    tm = _tile(rows, 512)

    def body(p_ref, o_ref):
        acc = p_ref[0].astype(F32)
        for k in range(1, N_SHARD):
            acc = acc + p_ref[k].astype(F32)
        o_ref[...] = acc

    return pl.pallas_call(
        body, name=name, grid=(rows // tm,),
        in_specs=[pl.BlockSpec((N_SHARD, tm, cols), lambda i: (0, i, 0))],
        out_specs=pl.BlockSpec((tm, cols), lambda i: (i, 0)),
        out_shape=jax.ShapeDtypeStruct((rows, cols), F32))(parts)


def _mesh_pos():
    return lax.axis_index("x"), lax.axis_index("y"), lax.axis_index("c")


CHIP_FLIPS = ((1, 0), (0, 1), (1, 1))


class Exchange(NamedTuple):
    kind: str
    srcs: tuple
    lands: tuple
    layer: Any = None


HBM_SPEC = pl.BlockSpec(memory_space=pltpu.HBM)
SEM_SPEC = pl.BlockSpec(memory_space=pltpu.SEMAPHORE)
DATAFLOW = pltpu.SideEffectType.DATAFLOW_SIDE_EFFECTING


def _exchange_copies(ex, src_refs, land_refs, send_sems, recv_sems):
    x, y, c = _mesh_pos()
    mine = 2 * x + y

    def slot(ref, chip):
        return ref.at[chip] if ex.layer is None else ref.at[chip, ex.layer]

    pairs = []
    for a, (src, land) in enumerate(zip(src_refs, land_refs)):
        for k, (fx, fy) in enumerate(CHIP_FLIPS):
            px, py = x ^ fx, y ^ fy
            peer = 2 * px + py
            src_part = src if ex.kind == "gather" else src.at[peer]
            pair = a * len(CHIP_FLIPS) + k
            common = dict(src_ref=src_part, send_sem=send_sems.at[pair], recv_sem=recv_sems.at[pair],
                          device_id=(px, py, c), device_id_type=MESH)
            pairs.append((pltpu.make_async_remote_copy(dst_ref=slot(land, mine), **common),
                          pltpu.make_async_remote_copy(dst_ref=slot(land, peer), **common)))
    return pairs


def _exchange_start(exchanges, *, name):
    srcs = [s for ex in exchanges for s in ex.srcs]
    lands = [b for ex in exchanges for b in ex.lands]
    n_arr, n_ex = len(srcs) + len(lands), len(exchanges)

    def body(*refs):
        src_refs, land_refs = refs[:len(srcs)], refs[len(srcs):n_arr]
        sems, token = refs[n_arr:n_arr + 2 * n_ex], refs[-1]
        at = 0
        for e, ex in enumerate(exchanges):
            n = len(ex.srcs)
            for send, _ in _exchange_copies(ex, src_refs[at:at + n], land_refs[at:at + n], sems[2 * e], sems[2 * e + 1]):
                send.start()
            at += n
        token[...] = jnp.zeros_like(token)

    sem_shapes = [pltpu.SemaphoreType.DMA((len(ex.srcs) * len(CHIP_FLIPS),)) for ex in exchanges for _ in range(2)]
    out = pl.pallas_call(
        body, name=name,
        out_shape=sem_shapes + [pltpu.HBM(a.shape, a.dtype) for a in srcs + lands] + [jax.ShapeDtypeStruct((SUBLANES, LANES), F32)],
        in_specs=[HBM_SPEC] * n_arr, out_specs=[SEM_SPEC] * (2 * n_ex) + [HBM_SPEC] * n_arr + [VMEM_SPEC],
        input_output_aliases={i: 2 * n_ex + i for i in range(n_arr)},
        compiler_params=pltpu.CompilerParams(has_side_effects=DATAFLOW),
    )(*[pltpu.with_memory_space_constraint(a, pltpu.HBM) for a in srcs + lands])
    sems, thru = out[:2 * n_ex], out[2 * n_ex:-1]
    pending, at = [], 0
    for e, ex in enumerate(exchanges):
        n = len(ex.srcs)
        pending.append((ex._replace(srcs=tuple(thru[at:at + n]), lands=tuple(thru[len(srcs) + at:len(srcs) + at + n])),
                        sems[2 * e], sems[2 * e + 1]))
        at += n
    return pending, out[-1]


def _exchange_wait(pending, after, *, name):
    ex, send_sems, recv_sems = pending
    n = len(ex.srcs)

    def body(*refs):
        src_refs, land_refs = refs[:n], refs[n:2 * n]
        for send, arrive in _exchange_copies(ex, src_refs, land_refs, refs[2 * n], refs[2 * n + 1]):
            send.wait_send()
            arrive.wait_recv()

    arrays = list(ex.srcs) + list(ex.lands)
    out = pl.pallas_call(
        body, name=name, out_shape=[pltpu.HBM(a.shape, a.dtype) for a in arrays],
        in_specs=[HBM_SPEC] * (2 * n) + [SEM_SPEC, SEM_SPEC, ANY], out_specs=[HBM_SPEC] * (2 * n),
        input_output_aliases={i: i for i in range(2 * n)},
        compiler_params=pltpu.CompilerParams(has_side_effects=DATAFLOW),
    )(*arrays, send_sems, recv_sems, after)
    return out[n:]


def _swap_with_sibling(arrays, *, name):
    n = len(arrays)

    def body(*refs):
        ins, outs = refs[:n], refs[n:2 * n]
        send_sems, recv_sems = refs[2 * n:]
        x, y, c = _mesh_pos()
        copies = []
        for a in range(n):
            cp = pltpu.make_async_remote_copy(
                src_ref=ins[a], dst_ref=outs[a], send_sem=send_sems.at[a], recv_sem=recv_sems.at[a],
                device_id=(x, y, 1 - c), device_id_type=MESH)
            cp.start()
            copies.append(cp)
        for cp in copies:
            cp.wait()

    return pl.pallas_call(
        body, name=name, in_specs=[ANY] * n, out_specs=[ANY] * n,
        out_shape=[jax.ShapeDtypeStruct(a.shape, a.dtype) for a in arrays],
        scratch_shapes=[pltpu.SemaphoreType.DMA((n,)), pltpu.SemaphoreType.DMA((n,))])(*arrays)


N_DEV = 8


def _all_reduce_small(vec, *, name):
    rows = vec.shape[0]

    def body(v_ref, o_ref, land, send_sems, recv_sems):
        x, y, c = _mesh_pos()
        me = 4 * x + 2 * y + c
        land[me] = v_ref[...]
        copies = []
        for k in range(1, N_DEV):
            fx, fy, fc = (k >> 2) & 1, (k >> 1) & 1, k & 1
            px, py, pc = x ^ fx, y ^ fy, c ^ fc
            send = pltpu.make_async_remote_copy(
                src_ref=v_ref, dst_ref=land.at[me], send_sem=send_sems.at[k - 1], recv_sem=recv_sems.at[k - 1],
                device_id=(px, py, pc), device_id_type=MESH)
            send.start()
            arrive = pltpu.make_async_remote_copy(
                src_ref=v_ref, dst_ref=land.at[4 * px + 2 * py + pc], send_sem=send_sems.at[k - 1], recv_sem=recv_sems.at[k - 1],
                device_id=(px, py, pc), device_id_type=MESH)
            copies.append((send, arrive))
        for send, arrive in copies:
            send.wait_send()
            arrive.wait_recv()
        acc = land[0]
        for k in range(1, N_DEV):
            acc = acc + land[k]
        o_ref[...] = acc

    return pl.pallas_call(
        body, name=name, in_specs=[VMEM_SPEC], out_specs=VMEM_SPEC,
        out_shape=jax.ShapeDtypeStruct(vec.shape, F32),
        scratch_shapes=[pltpu.VMEM((N_DEV, rows, LANES), F32), pltpu.SemaphoreType.DMA((N_DEV - 1,)),
                        pltpu.SemaphoreType.DMA((N_DEV - 1,))])(vec)


PACK_UNIT = SUBLANES * LANES * 2


def _padded(n):
    return -(-n // PACK_UNIT) * PACK_UNIT


def _pack(arrays, dtype, lead=0):
    parts = []
    for a in arrays:
        lead_shape = a.shape[:lead]
        flat = a.astype(dtype).reshape(lead_shape + (-1,))
        n = flat.shape[-1]
        flat = jnp.pad(flat, [(0, 0)] * lead + [(0, _padded(n) - n)])
        parts.append(flat.reshape(lead_shape + (-1, LANES)))
    return jnp.concatenate(parts, axis=lead)


def _unpack(buf, shapes, lead=0):
    out, row = [], 0
    for shp in shapes:
        n = math.prod(shp)
        rows = _padded(n) // LANES
        part = lax.slice_in_dim(buf, row, row + rows, axis=lead)
        lead_shape = part.shape[:lead]
        part = part.reshape(lead_shape + (-1,))
        part = lax.slice_in_dim(part, 0, n, axis=lead)
        out.append(part.reshape(lead_shape + tuple(shp)))
        row += rows
    return out


def kernel(x, positions, ln_mix_a, w_pool, b_pool, pool_scale, ln_ffn, w_gate, w_up, w_down, ln_kv, w_dkv, g_kv_latent, w_uk, w_uv, g_k, ln_mix_b, w_dq, g_q_latent, w_uq, g_q, w_o, loss_target, m_ln_mix_a, m_w_pool, m_b_pool, m_pool_scale, m_ln_ffn, m_w_gate, m_w_up, m_w_down, m_ln_kv, m_w_dkv, m_g_kv_latent, m_w_uk, m_w_uv, m_g_k, m_ln_mix_b, m_w_dq, m_g_q_latent, m_w_uq, m_g_q, m_w_o, v_ln_mix_a, v_w_pool, v_b_pool, v_pool_scale, v_ln_ffn, v_w_gate, v_w_up, v_w_down, v_ln_kv, v_w_dkv, v_g_kv_latent, v_w_uk, v_w_uv, v_g_k, v_ln_mix_b, v_w_dq, v_g_q_latent, v_w_uq, v_g_q, v_w_o):
    weights = dict(ln_mix_a=ln_mix_a, w_pool=w_pool, b_pool=b_pool, pool_scale=pool_scale, ln_ffn=ln_ffn, w_gate=w_gate,
                   w_up=w_up, w_down=w_down, ln_kv=ln_kv, w_dkv=w_dkv, g_kv_latent=g_kv_latent, w_uk=w_uk, w_uv=w_uv, g_k=g_k,
                   ln_mix_b=ln_mix_b, w_dq=w_dq, g_q_latent=g_q_latent, w_uq=w_uq, g_q=g_q, w_o=w_o)
    mom_m = dict(ln_mix_a=m_ln_mix_a, w_pool=m_w_pool, b_pool=m_b_pool, pool_scale=m_pool_scale, ln_ffn=m_ln_ffn,
                 w_gate=m_w_gate, w_up=m_w_up, w_down=m_w_down, ln_kv=m_ln_kv, w_dkv=m_w_dkv, g_kv_latent=m_g_kv_latent,
                 w_uk=m_w_uk, w_uv=m_w_uv, g_k=m_g_k, ln_mix_b=m_ln_mix_b, w_dq=m_w_dq, g_q_latent=m_g_q_latent,
                 w_uq=m_w_uq, g_q=m_g_q, w_o=m_w_o)
    mom_v = dict(ln_mix_a=v_ln_mix_a, w_pool=v_w_pool, b_pool=v_b_pool, pool_scale=v_pool_scale, ln_ffn=v_ln_ffn,
                 w_gate=v_w_gate, w_up=v_w_up, w_down=v_w_down, ln_kv=v_ln_kv, w_dkv=v_w_dkv, g_kv_latent=v_g_kv_latent,
                 w_uk=v_w_uk, w_uv=v_w_uv, g_k=v_g_k, ln_mix_b=v_ln_mix_b, w_dq=v_w_dq, g_q_latent=v_g_q_latent,
                 w_uq=v_w_uq, g_q=v_g_q, w_o=v_w_o)
    order = list(weights)
    s = x.shape[1]
    d = D_MODEL
    xs = x.reshape(s, d)
    target = loss_target.reshape(s, d)
    my_chip = 2 * lax.axis_index("x") + lax.axis_index("y")

    mat_names = ("w_pool", "w_dkv", "w_uk", "w_uv", "w_dq", "w_uq", "w_o")
    vec_names = ("ln_mix_a", "b_pool", "pool_scale")
    mat_shapes = [weights[n].shape for n in mat_names]
    vec_shapes = [weights[n].shape for n in vec_names]

    def rows_of(a, lead=0):
        return a.reshape(a.shape[:lead] + (-1, a.shape[-1]))

    mats_local = tuple(rows_of(weights[n].astype(WIRE_DTYPE)) for n in mat_names)
    vecs_local = _pack([weights[n] for n in vec_names], F32)

    def landing(shard):
        return lax.dynamic_update_slice_in_dim(lax.empty((N_SHARD,) + shard.shape, shard.dtype), shard[None], my_chip, axis=0)

    small_local = mats_local + (vecs_local,)
    gathers = [Exchange("gather", small_local, tuple(landing(sh) for sh in small_local))]
    for l in range(DEPTH):
        shards = tuple(w[l].astype(WIRE_DTYPE) for w in (w_gate, w_up, w_down))
        gathers.append(Exchange("gather", shards, tuple(landing(sh) for sh in shards)))
    gathering, _ = _exchange_start(gathers, name="gather_start")

    inv = ROPE_THETA ** (-jnp.arange(ROPE // 2, dtype=F32) * 2.0 / ROPE)
    inv_lanes = jnp.concatenate([inv, inv, jnp.zeros((LANES - ROPE,), F32)]).reshape(1, LANES)
    cos_t, sin_t = _rope_tables(positions.reshape(s, 1).astype(F32), inv_lanes, name="rope_tables")

    *mats_all, vecs_all = _exchange_wait(gathering[0], cos_t, name="gather_wait_small")
    g_pool, g_dkv, g_uk, g_uv, g_dq, g_uq, g_o = (a.reshape((N_SHARD,) + shp) for a, shp in zip(mats_all, mat_shapes))
    g_lna, g_bp, g_ps = _unpack(vecs_all, vec_shapes, lead=1)

    wpool_f = g_pool.transpose(1, 2, 0, 3, 4).reshape(N_A, N_GROUPS, GROUP_DIM, GROUP_DIM)
    bpool_f = g_bp.transpose(1, 2, 0, 3).reshape(N_A, 1, d)
    pscale_f = g_ps.transpose(1, 0, 2).reshape(N_A, 1, d)
    lna_f = g_lna.transpose(1, 0, 2).reshape(N_A, 1, d)
    wdkv_f = jnp.pad(g_dkv.reshape(d, KV_LORA + ROPE), ((0, 0), (0, CKV_PAD - KV_LORA - ROPE)))
    wuk_f = g_uk.transpose(1, 0, 2).reshape(KV_LORA, N_HEADS * NOPE)
    wuv_f = g_uv.transpose(1, 0, 2).reshape(KV_LORA, N_HEADS * V_DIM)
    wdq_f = g_dq.transpose(1, 0, 2, 3).reshape(N_B, d, Q_LORA)
    wuq_f = jnp.pad(g_uq.transpose(1, 2, 0, 3).reshape(N_B, Q_LORA, N_HEADS, QK_DIM),
                    ((0, 0), (0, 0), (0, 0), (0, HEAD_PAD - QK_DIM))).reshape(N_B, Q_LORA, N_HEADS * HEAD_PAD)
    wo_f = g_o.transpose(1, 0, 2, 3).reshape(N_B, d, d)

    def head_gain(g):
        return jnp.pad(g.reshape(1, QK_DIM), ((0, 0), (0, HEAD_PAD - QK_DIM)))

    ffn_w = [None] * DEPTH

    def ffn_fwd(xin, layer):
        hf = _rms_fwd(xin, ln_ffn[layer].reshape(1, d), n=d, name="ffn_norm")
        ffn_w[layer] = wg, wu, wd = _exchange_wait(gathering[1 + layer], hf, name=f"gather_wait_{layer}")
        a, b, u = _ffn_up(hf, wg, wu, name="ffn_up")
        return _ffn_down(u, wd, xin, name="ffn_down"), (xin, hf, a, b, u)

    saved_a, saved_b, saved_f = [], [], []
    cur = xs
    for l in range(N_A):
        dpool = _rms_pool_fwd(cur, lna_f[l], name="pool_fwd")
        x1 = _pool_mm_fwd(dpool, wpool_f[l], bpool_f[l], pscale_f[l], cur, name="pool_mm")
        saved_a.append((cur, dpool))
        cur, sf = ffn_fwd(x1, l)
        saved_f.append(sf)

    x_kv = cur
    hk = _rms_fwd(x_kv, ln_kv.reshape(1, d), n=d, name="kv_norm")
    ckv = _mm(hk, wdkv_f, name="kv_down")
    c_lat = _rms_fwd(ckv, g_kv_latent.reshape(1, KV_LORA), n=KV_LORA, name="kv_latent_norm")
    kn_raw = _mm(c_lat, wuk_f, name="k_up")
    v_all = _mm(c_lat, wuv_f, out_dtype=MXU_DTYPE, name="v_up")
    vt_all = _mm(wuv_f.T, c_lat, tb=True, out_dtype=MXU_DTYPE, name="v_up_t")
    k_raw = _k_assemble(kn_raw, ckv, name="k_assemble")
    gk_pad = head_gain(g_k)
    k_cat = _head_norm_rope_fwd(k_raw, gk_pad, cos_t, sin_t, name="k_norm_rope")

    for j in range(N_B):
        l = N_A + j
        hq = _rms_fwd(cur, ln_mix_b[j].reshape(1, d), n=d, name="q_norm")
        cq_raw = _mm(hq, wdq_f[j], name="q_down")
        cq = _rms_fwd(cq_raw, g_q_latent[j].reshape(1, Q_LORA), n=Q_LORA, name="q_latent_norm")
        q_raw = _mm(cq, wuq_f[j], name="q_up")
        gq_pad = head_gain(g_q[j])
        q_cat = _head_norm_rope_fwd(q_raw, gq_pad, cos_t, sin_t, name="q_norm_rope")
        ot, lse = _attn_fwd(q_cat, k_cat, vt_all, name="attn_fwd")
        x1 = _mm(ot, wo_f[j], ta=True, resid=cur, name="attn_out")
        saved_b.append((cur, hq, cq_raw, cq, q_raw, gq_pad, q_cat, ot, lse))
        cur, sf = ffn_fwd(x1, l)
        saved_f.append(sf)

    dy, loss_part = _loss_head(cur, target, name="loss_head")

    ffn_landed = (lax.empty((N_SHARD, DEPTH, d, FF_SHARD), WIRE_DTYPE), lax.empty((N_SHARD, DEPTH, d, FF_SHARD), WIRE_DTYPE),
                  lax.empty((N_SHARD, DEPTH, FF_SHARD, d), WIRE_DTYPE))
    scattering = None
    grads = {}
    d_ln_ffn = [None] * DEPTH

    def own_part(full):
        return lax.dynamic_index_in_dim(full, my_chip, axis=0, keepdims=True)

    def ffn_bwd(dyv, layer):
        nonlocal ffn_landed, scattering
        xin, hf, a, b, u = saved_f[layer]
        wg, wu, wd = ffn_w[layer]
        da, db = _ffn_bwd_hidden(dyv, wd, a, b, name="ffn_bwd_hidden")
        dwd = _ffn_bwd_dwd(u, dyv, name="ffn_bwd_dwd")
        dwg, dwu = _ffn_bwd_dwgu(hf, da, db, name="ffn_bwd_dwgu")
        if scattering is not None:
            ffn_landed = _exchange_wait(scattering, dwg, name=f"scatter_wait_{layer + 1}")
        ffn_landed = tuple(lax.dynamic_update_slice(buf, own_part(g)[:, None], (my_chip, layer, 0, 0))
                           for buf, g in zip(ffn_landed, (dwg, dwu, dwd)))
        (scattering,), started = _exchange_start([Exchange("scatter", (dwg, dwu, dwd), ffn_landed, layer)],
                                                 name=f"scatter_start_{layer}")
        dhf = _ffn_bwd_dh(da, db, wg, wu, name="ffn_bwd_dh")
        dx, dg = _rms_bwd(xin, ln_ffn[layer].reshape(1, d), dhf, n=d, dx_in=dyv, after=started, name="ffn_norm_bwd")
        d_ln_ffn[layer] = dg.sum(axis=0)
        return dx

    dk_acc = dv_acc = None
    d_ln_mix_b, d_w_dq, d_g_q_latent, d_w_uq, d_g_q, d_w_o = ([None] * N_B for _ in range(6))
    dcur = dy
    for j in reversed(range(N_B)):
        l = N_A + j
        xin, hq, cq_raw, cq, q_raw, gq_pad, q_cat, ot, lse = saved_b[j]
        dx1 = ffn_bwd(dcur, l)
        do = _mm(dx1, wo_f[j], tb=True, out_dtype=MXU_DTYPE, name="attn_out_bwd")
        dot = _mm(wo_f[j], dx1, tb=True, name="attn_out_bwd_t")
        d_w_o[j] = _mm_tn(ot, dx1, at=True, name="attn_out_dw")
        delta = _attn_delta(ot, dot, name="attn_delta")
        lse_col, delta_col = lse.reshape(N_HEADS, s, 1), delta.reshape(N_HEADS, s, 1)
        dq_cat = _attn_bwd_dq(q_cat, k_cat, v_all, do, lse_col, delta_col, name="attn_bwd_dq")
        dk_acc, dv_acc = _attn_bwd_dkv(q_cat, k_cat, v_all, do, lse, delta, dk_acc, dv_acc, name="attn_bwd_dkv")
        dq_raw, dgq = _head_norm_rope_bwd(q_raw, gq_pad, cos_t, sin_t, dq_cat, name="q_norm_rope_bwd")
        d_g_q[j] = dgq.sum(axis=0)[:QK_DIM]
        dcq = _mm(dq_raw, wuq_f[j], tb=True, name="q_up_bwd")
        d_w_uq[j] = _mm_tn(cq, dq_raw, name="q_up_dw").reshape(Q_LORA, N_HEADS, HEAD_PAD)[:, :, :QK_DIM].reshape(Q_LORA, N_HEADS * QK_DIM)
        dcq_raw, dgl = _rms_bwd(cq_raw, g_q_latent[j].reshape(1, Q_LORA), dcq, n=Q_LORA, name="q_latent_norm_bwd")
        d_g_q_latent[j] = dgl.sum(axis=0)
        dhq = _mm(dcq_raw, wdq_f[j], tb=True, name="q_down_bwd")
        d_w_dq[j] = _mm_tn(hq, dcq_raw, name="q_down_dw")
        dcur, dgm = _rms_bwd(xin, ln_mix_b[j].reshape(1, d), dhq, n=d, dx_in=dx1, name="q_norm_bwd")
        d_ln_mix_b[j] = dgm.sum(axis=0)

    dk_raw, dgk = _head_norm_rope_bwd(k_raw, gk_pad, cos_t, sin_t, dk_acc, name="k_norm_rope_bwd")
    grads["g_k"] = dgk.sum(axis=0)[:QK_DIM]
    dc = _mm(dv_acc, wuv_f, tb=True, name="v_up_bwd")
    grads["w_uv"] = _mm_tn(c_lat, dv_acc, name="v_up_dw")
    dkn, dpe = _k_disassemble(dk_raw, name="k_disassemble")
    dc = _mm(dkn, wuk_f, tb=True, resid=dc, name="k_up_bwd")
    grads["w_uk"] = _mm_tn(c_lat, dkn, name="k_up_dw")
    dc_raw, dgl = _rms_bwd(ckv, g_kv_latent.reshape(1, KV_LORA), dc, n=KV_LORA, name="kv_latent_norm_bwd")
    grads["g_kv_latent"] = dgl.sum(axis=0)
    dckv = jnp.concatenate([dc_raw, dpe], axis=1)
    dhk = _mm(dckv, wdkv_f, tb=True, name="kv_down_bwd")
    grads["w_dkv"] = _mm_tn(hk, dckv, name="kv_down_dw")[:, :KV_LORA + ROPE]
    dcur, dg = _rms_bwd(x_kv, ln_kv.reshape(1, d), dhk, n=d, dx_in=dcur, name="kv_norm_bwd")
    grads["ln_kv"] = dg.sum(axis=0)

    d_ln_mix_a, d_w_pool, d_b_pool, d_pool_scale = ([None] * N_A for _ in range(4))
    for l in reversed(range(N_A)):
        xin, dpool = saved_a[l]
        dx1 = ffn_bwd(dcur, l)
        dd, dwp, dbp, dsp = _pool_mm_bwd(dpool, wpool_f[l], bpool_f[l], pscale_f[l], dx1, name="pool_mm_bwd")
        d_w_pool[l], d_b_pool[l], d_pool_scale[l] = dwp, dbp.sum(axis=0), dsp.sum(axis=0)
        dcur, dg = _rms_pool_bwd(xin, lna_f[l], dd, dx1, name="pool_bwd")
        d_ln_mix_a[l] = dg.sum(axis=0)
    grad_x = dcur.reshape(1, s, d)

    gm = {
        "w_pool": jnp.stack(d_w_pool).reshape(N_A, N_GROUPS, N_SHARD, GROUP_DIM // N_SHARD, GROUP_DIM).transpose(2, 0, 1, 3, 4),
        "w_dkv": grads["w_dkv"].reshape(N_SHARD, d // N_SHARD, KV_LORA + ROPE),
        "w_uk": grads["w_uk"].reshape(KV_LORA, N_SHARD, -1).transpose(1, 0, 2),
        "w_uv": grads["w_uv"].reshape(KV_LORA, N_SHARD, -1).transpose(1, 0, 2),
        "w_dq": jnp.stack(d_w_dq).reshape(N_B, N_SHARD, d // N_SHARD, Q_LORA).transpose(1, 0, 2, 3),
        "w_uq": jnp.stack(d_w_uq).reshape(N_B, Q_LORA, N_SHARD, -1).transpose(2, 0, 1, 3),
        "w_o": jnp.stack(d_w_o).reshape(N_B, N_SHARD, d // N_SHARD, d).transpose(1, 0, 2, 3),
    }
    mats_grad = tuple(rows_of(gm[n].astype(WIRE_DTYPE), lead=1) for n in mat_names)
    mats_landing = tuple(lax.dynamic_update_slice_in_dim(lax.empty(g.shape, WIRE_DTYPE), own_part(g), my_chip, axis=0)
                         for g in mats_grad)
    (mats_scatter,), _ = _exchange_start([Exchange("scatter", mats_grad, mats_landing)], name="scatter_start_small")
    ffn_landed = _exchange_wait(scattering, dcur, name="scatter_wait_0")
    mats_landed = _exchange_wait(mats_scatter, ffn_landed[0], name="scatter_wait_small")
    landed = [ffn_landed[0].reshape(N_SHARD, DEPTH * d, FF_SHARD), ffn_landed[1].reshape(N_SHARD, DEPTH * d, FF_SHARD),
              ffn_landed[2].reshape(N_SHARD, DEPTH * FF_SHARD, d), *mats_landed]
    chip_sums = [_sum_slots(p, name="sum_chips") for p in landed]
    sib_sums = _swap_with_sibling(chip_sums, name="swap_sibling")

    vec_full = {
        "ln_mix_a": jnp.stack(d_ln_mix_a), "b_pool": jnp.stack(d_b_pool).reshape(N_A, N_GROUPS, GROUP_DIM),
        "pool_scale": jnp.stack(d_pool_scale), "ln_ffn": jnp.stack(d_ln_ffn), "ln_kv": grads["ln_kv"],
        "g_kv_latent": grads["g_kv_latent"], "g_k": grads["g_k"], "ln_mix_b": jnp.stack(d_ln_mix_b),
        "g_q_latent": jnp.stack(d_g_q_latent), "g_q": jnp.stack(d_g_q),
    }
    small_names = list(vec_full)
    small_shapes = [vec_full[n].shape for n in small_names] + [(SUBLANES * LANES,)]
    small = _all_reduce_small(_pack([vec_full[n] for n in small_names] + [loss_part.reshape(-1)], F32), name="all_reduce_small")
    small_sum = _unpack(small, small_shapes)
    loss = jnp.sum(small_sum[-1])
    vec_grad = dict(zip(small_names, small_sum[:-1]))
    vec_grad["ln_mix_a"] = lax.dynamic_slice_in_dim(vec_grad["ln_mix_a"], my_chip * (d // N_SHARD), d // N_SHARD, axis=1)
    vec_grad["pool_scale"] = lax.dynamic_slice_in_dim(vec_grad["pool_scale"], my_chip * (d // N_SHARD), d // N_SHARD, axis=1)
    vec_grad["b_pool"] = lax.dynamic_slice_in_dim(vec_grad["b_pool"], my_chip * (GROUP_DIM // N_SHARD), GROUP_DIM // N_SHARD, axis=2)

    out_g, out_d, out_m, out_v = {}, {}, {}, {}
    for idx, (nm, rows, cols) in enumerate((("w_gate", DEPTH * d, FF_SHARD), ("w_up", DEPTH * d, FF_SHARD), ("w_down", DEPTH * FF_SHARD, d))):
        res = _adamw(weights[nm].reshape(rows, cols), mom_m[nm].reshape(rows, cols), mom_v[nm].reshape(rows, cols),
                     [chip_sums[idx], sib_sums[idx]], name="adamw_ffn")
        shp = weights[nm].shape
        out_g[nm], out_d[nm], out_m[nm], out_v[nm] = (r.reshape(shp) for r in res)

    for idx, nm in enumerate(mat_names, start=3):
        res = _adamw(rows_of(weights[nm]), rows_of(mom_m[nm]), rows_of(mom_v[nm]), [chip_sums[idx], sib_sums[idx]], name="adamw_mat")
        shp = weights[nm].shape
        out_g[nm], out_d[nm], out_m[nm], out_v[nm] = (r.reshape(shp) for r in res)

    def as_rows(a):
        return a.reshape(1, -1) if a.ndim == 1 else rows_of(a)

    res = _adamw_vectors([as_rows(weights[n]) for n in small_names], [as_rows(mom_m[n]) for n in small_names],
                         [as_rows(mom_v[n]) for n in small_names],
                         [as_rows(vec_grad[n].reshape(weights[n].shape)) for n in small_names], name="adamw_vectors")
    for tgt, arrs in zip((out_g, out_d, out_m, out_v), res):
        for n, arr in zip(small_names, arrs):
            tgt[n] = arr.reshape(weights[n].shape)

    return (loss, grad_x, *[out_g[n] for n in order], *[out_d[n] for n in order],
            *[out_m[n] for n in order], *[out_v[n] for n in order])
```

```python
import math
from typing import Any, NamedTuple

import jax
import jax.numpy as jnp
from jax import lax
from jax.experimental import pallas as pl
from jax.experimental.pallas import tpu as pltpu

F32 = jnp.float32
BF16 = jnp.bfloat16
MXU_DTYPE = BF16
WIRE_DTYPE = BF16
SAVED_DTYPE = BF16

D_MODEL = 1024
N_A = 2
N_B = 2
DEPTH = 4
POOL_WINDOWS = (2, 4, 8, 16)
N_GROUPS = 4
GROUP_DIM = 256
POOL_HALO = 16
N_HEADS = 8
NOPE = 128
ROPE = 64
QK_DIM = 192
HEAD_PAD = 256
V_DIM = 128
Q_LORA = 256
KV_LORA = 512
CKV_PAD = 640
ROPE_THETA = 10000.0
CHUNK = 64
EPS = 1e-6
N_SHARD = 4
FF_SHARD = 704
FFN_ROWS = 1024
FFN_GRAD_ROWS = 2048
LANES = 128
SUBLANES = 8
ADAM_LR, ADAM_B1, ADAM_B2, ADAM_EPS, ADAM_WD, ADAM_STEP = 0.001, 0.9, 0.999, 1e-08, 0.01, 10
MESH = pl.DeviceIdType.MESH
ANY = pl.BlockSpec(memory_space=pl.ANY)
VMEM_SPEC = pl.BlockSpec(memory_space=pltpu.VMEM)


def _tile(n, pref):
    if n <= pref:
        return n
    t = pref - pref % SUBLANES
    while n % t:
        t -= SUBLANES
    return t


def _fold8(v):
    r, n = v.shape
    return v.reshape(r // SUBLANES, SUBLANES, n).sum(axis=0)


def _dot(a, b, dims):
    return lax.dot_general(a.astype(MXU_DTYPE), b.astype(MXU_DTYPE), (dims, ((), ())),
                           preferred_element_type=F32)


def _nn(a, b):
    return _dot(a, b, ((1,), (0,)))


def _nt(a, b):
    return _dot(a, b, ((1,), (1,)))


def _tn(a, b):
    return _dot(a, b, ((0,), (0,)))


def _mm(a, b, *, ta=False, tb=False, resid=None, out_dtype=F32, name):
    assert not (ta and tb)
    m, k = (a.shape[1], a.shape[0]) if ta else a.shape
    n = b.shape[0] if tb else b.shape[1]
    tm, tn = _tile(m, 512), _tile(n, 1024)

    def body(*refs):
        if resid is None:
            a_ref, b_ref, o_ref = refs
        else:
            a_ref, b_ref, r_ref, o_ref = refs
        acc = (_tn if ta else _nt if tb else _nn)(a_ref[...], b_ref[...])
        if resid is not None:
            acc = r_ref[...] + acc
        o_ref[...] = acc.astype(o_ref.dtype)

    in_specs = [pl.BlockSpec((k, tm), lambda i, j: (0, i)) if ta else pl.BlockSpec((tm, k), lambda i, j: (i, 0)),
                pl.BlockSpec((tn, k), lambda i, j: (j, 0)) if tb else pl.BlockSpec((k, tn), lambda i, j: (0, j))]
    args = [a, b]
    if resid is not None:
        in_specs.append(pl.BlockSpec((tm, tn), lambda i, j: (i, j)))
        args.append(resid)
    return pl.pallas_call(
        body, name=name, grid=(m // tm, n // tn), in_specs=in_specs,
        out_specs=pl.BlockSpec((tm, tn), lambda i, j: (i, j)),
        out_shape=jax.ShapeDtypeStruct((m, n), out_dtype))(*args)


def _mm_tn(a, b, *, name, at=False, out_dtype=F32):
    m = b.shape[0]
    k1 = a.shape[0] if at else a.shape[1]
    n = b.shape[1]
    tm, tn = _tile(m, 512), _tile(n, 1024)
    nm = m // tm

    def body(a_ref, b_ref, o_ref, acc):
        i = pl.program_id(1)

        @pl.when(i == 0)
        def _():
            acc[...] = jnp.zeros_like(acc)

        acc[...] += (_nn if at else _tn)(a_ref[...], b_ref[...])

        @pl.when(i == nm - 1)
        def _():
            o_ref[...] = acc[...].astype(o_ref.dtype)

    return pl.pallas_call(
        body, name=name, grid=(n // tn, nm),
        in_specs=[pl.BlockSpec((k1, tm), lambda j, i: (0, i)) if at else pl.BlockSpec((tm, k1), lambda j, i: (i, 0)),
                  pl.BlockSpec((tm, tn), lambda j, i: (i, j))],
        out_specs=pl.BlockSpec((k1, tn), lambda j, i: (0, j)),
        out_shape=jax.ShapeDtypeStruct((k1, n), out_dtype),
        scratch_shapes=[pltpu.VMEM((k1, tn), F32)])(a, b)


def _rms_fwd(x, g, *, n, n_valid=None, name):
    out_dtype = MXU_DTYPE
    rows = x.shape[0]
    tm = _tile(rows, 512)
    inv_n = 1.0 / (n_valid or n)

    def body(x_ref, g_ref, o_ref):
        xv = x_ref[...]
        r = lax.rsqrt(jnp.sum(xv * xv, axis=-1, keepdims=True) * inv_n + EPS)
        o_ref[...] = (xv * r * g_ref[...]).astype(o_ref.dtype)

    return pl.pallas_call(
        body, name=name, grid=(rows // tm,),
        in_specs=[pl.BlockSpec((tm, n), lambda i: (i, 0)), pl.BlockSpec((1, n), lambda i: (0, 0))],
        out_specs=pl.BlockSpec((tm, n), lambda i: (i, 0)),
        out_shape=jax.ShapeDtypeStruct((rows, n), out_dtype))(x, g)


def _rms_bwd_math(xv, gv, dyv, inv_n):
    r = lax.rsqrt(jnp.sum(xv * xv, axis=-1, keepdims=True) * inv_n + EPS)
    xh = xv * r
    gy = dyv * gv
    dx = r * (gy - xh * (jnp.sum(gy * xh, axis=-1, keepdims=True) * inv_n))
    return dx, dyv * xh


def _rms_bwd(x, g, dy, *, n, dx_in=None, after=None, name):
    rows = x.shape[0]
    tm = _tile(rows, 512)
    inv_n = 1.0 / n

    def body(*refs):
        if after is not None:
            refs = refs[:-3] + refs[-2:]
        if dx_in is None:
            x_ref, g_ref, dy_ref, dx_ref, dg_ref = refs
        else:
            x_ref, g_ref, dy_ref, din_ref, dx_ref, dg_ref = refs
        dx, dgc = _rms_bwd_math(x_ref[...], g_ref[...], dy_ref[...], inv_n)
        if dx_in is not None:
            dx = din_ref[...] + dx
        dx_ref[...] = dx

        @pl.when(pl.program_id(0) == 0)
        def _():
            dg_ref[...] = jnp.zeros_like(dg_ref)

        dg_ref[...] += _fold8(dgc)

    row_spec = pl.BlockSpec((tm, n), lambda i: (i, 0))
    in_specs = [row_spec, pl.BlockSpec((1, n), lambda i: (0, 0)), row_spec]
    args = [x, g, dy]
    if dx_in is not None:
        in_specs.append(row_spec)
        args.append(dx_in)
    if after is not None:
        in_specs.append(ANY)
        args.append(after)
    return pl.pallas_call(
        body, name=name, grid=(rows // tm,), in_specs=in_specs,
        out_specs=[row_spec, pl.BlockSpec((SUBLANES, n), lambda i: (0, 0))],
        out_shape=[jax.ShapeDtypeStruct((rows, n), F32), jax.ShapeDtypeStruct((SUBLANES, n), F32)])(*args)


def _pool_counts(t0, tm, w):
    t = t0 + lax.broadcasted_iota(jnp.int32, (tm, 1), 0)
    return jnp.minimum(t + 1, w).astype(F32)


def _rms_pool_fwd(x, g, *, name):
    s, d = x.shape
    tm = _tile(s, 512)
    hb = tm // POOL_HALO

    def body(x_ref, halo_ref, g_ref, o_ref):
        i = pl.program_id(0)
        gv = g_ref[...]

        def norm(v):
            return v * lax.rsqrt(jnp.mean(v * v, axis=-1, keepdims=True) + EPS) * gv

        h = norm(x_ref[...])
        halo = norm(halo_ref[...]) * (i > 0).astype(F32)
        hh = jnp.concatenate([halo, h], axis=0)
        rows = tm + POOL_HALO
        for gi, w in enumerate(POOL_WINDOWS):
            cols = slice(gi * GROUP_DIM, (gi + 1) * GROUP_DIM)
            acc = hh[:, cols]
            k = 1
            while k < w:
                acc = acc + pltpu.roll(acc, k, 0)
                k *= 2
            win = acc[POOL_HALO:rows]
            o_ref[:, cols] = (win / _pool_counts(i * tm, tm, w) - h[:, cols]).astype(o_ref.dtype)

    return pl.pallas_call(
        body, name=name, grid=(s // tm,),
        in_specs=[pl.BlockSpec((tm, d), lambda i: (i, 0)),
                  pl.BlockSpec((POOL_HALO, d), lambda i: (jnp.maximum(i * hb - 1, 0), 0)),
                  pl.BlockSpec((1, d), lambda i: (0, 0))],
        out_specs=pl.BlockSpec((tm, d), lambda i: (i, 0)),
        out_shape=jax.ShapeDtypeStruct((s, d), MXU_DTYPE))(x, x, g)


def _rms_pool_bwd(x, g, dd, dx_in, *, name):
    s, d = x.shape
    tm = _tile(s, 512)
    hb = tm // POOL_HALO
    nt = s // tm

    def body(x_ref, g_ref, dd_ref, halo_ref, din_ref, dx_ref, dg_ref):
        i = pl.program_id(0)
        ddv = dd_ref[...]
        halo = halo_ref[...] * (i < nt - 1).astype(F32)
        rows = tm + POOL_HALO
        parts = []
        for gi, w in enumerate(POOL_WINDOWS):
            cols = slice(gi * GROUP_DIM, (gi + 1) * GROUP_DIM)
            acc = jnp.concatenate([ddv[:, cols] / _pool_counts(i * tm, tm, w), halo[:, cols] * (1.0 / w)], axis=0)
            k = 1
            while k < w:
                acc = acc + pltpu.roll(acc, rows - k, 0)
                k *= 2
            parts.append(acc[0:tm] - ddv[:, cols])
        dh = jnp.concatenate(parts, axis=1)
        dx, dgc = _rms_bwd_math(x_ref[...], g_ref[...], dh, 1.0 / d)
        dx_ref[...] = din_ref[...] + dx

        @pl.when(i == 0)
        def _():
            dg_ref[...] = jnp.zeros_like(dg_ref)

        dg_ref[...] += _fold8(dgc)

    row_spec = pl.BlockSpec((tm, d), lambda i: (i, 0))
    return pl.pallas_call(
        body, name=name, grid=(nt,),
        in_specs=[row_spec, pl.BlockSpec((1, d), lambda i: (0, 0)), row_spec,
                  pl.BlockSpec((POOL_HALO, d), lambda i: (jnp.minimum((i + 1) * hb, s // POOL_HALO - 1), 0)),
                  row_spec],
        out_specs=[row_spec, pl.BlockSpec((SUBLANES, d), lambda i: (0, 0))],
        out_shape=[jax.ShapeDtypeStruct((s, d), F32), jax.ShapeDtypeStruct((SUBLANES, d), F32)])(x, g, dd, dd, dx_in)


def _pool_mm_fwd(dpool, w, b, scale, x, *, name):
    s, d = x.shape
    tm = _tile(s, 512)

    def body(d_ref, w_ref, b_ref, s_ref, x_ref, o_ref):
        for gi in range(N_GROUPS):
            cols = slice(gi * GROUP_DIM, (gi + 1) * GROUP_DIM)
            y = _nn(d_ref[:, cols], w_ref[gi]) + b_ref[:, cols]
            o_ref[:, cols] = x_ref[:, cols] + y * s_ref[:, cols]

    row_spec = pl.BlockSpec((tm, d), lambda i: (i, 0))
    vec_spec = pl.BlockSpec((1, d), lambda i: (0, 0))
    return pl.pallas_call(
        body, name=name, grid=(s // tm,),
        in_specs=[row_spec, pl.BlockSpec((N_GROUPS, GROUP_DIM, GROUP_DIM), lambda i: (0, 0, 0)), vec_spec, vec_spec, row_spec],
        out_specs=row_spec, out_shape=jax.ShapeDtypeStruct((s, d), F32))(dpool, w, b, scale, x)


def _pool_mm_bwd(dpool, w, b, scale, dx, *, name):
    s, d = dx.shape
    tm = _tile(s, 512)

    def body(d_ref, w_ref, b_ref, s_ref, dx_ref, dd_ref, dw_ref, db_ref, ds_ref):
        @pl.when(pl.program_id(0) == 0)
        def _():
            dw_ref[...] = jnp.zeros_like(dw_ref)
            db_ref[...] = jnp.zeros_like(db_ref)
            ds_ref[...] = jnp.zeros_like(ds_ref)

        for gi in range(N_GROUPS):
            cols = slice(gi * GROUP_DIM, (gi + 1) * GROUP_DIM)
            dg = d_ref[:, cols]
            y = _nn(dg, w_ref[gi]) + b_ref[:, cols]
            dxg = dx_ref[:, cols]
            dy = dxg * s_ref[:, cols]
            ds_ref[:, cols] += _fold8(dxg * y)
            db_ref[:, cols] += _fold8(dy)
            dw_ref[gi] += _tn(dg, dy)
            dd_ref[:, cols] = _nt(dy, w_ref[gi])

    row_spec = pl.BlockSpec((tm, d), lambda i: (i, 0))
    vec_spec = pl.BlockSpec((1, d), lambda i: (0, 0))
    w_spec = pl.BlockSpec((N_GROUPS, GROUP_DIM, GROUP_DIM), lambda i: (0, 0, 0))
    part_spec = pl.BlockSpec((SUBLANES, d), lambda i: (0, 0))
    return pl.pallas_call(
        body, name=name, grid=(s // tm,),
        in_specs=[row_spec, w_spec, vec_spec, vec_spec, row_spec],
        out_specs=[row_spec, w_spec, part_spec, part_spec],
        out_shape=[jax.ShapeDtypeStruct((s, d), F32), jax.ShapeDtypeStruct((N_GROUPS, GROUP_DIM, GROUP_DIM), F32),
                   jax.ShapeDtypeStruct((SUBLANES, d), F32), jax.ShapeDtypeStruct((SUBLANES, d), F32)])(dpool, w, b, scale, dx)


def _ffn_up(hf, wg, wu, *, name):
    s, d = hf.shape
    tm = _tile(s, FFN_ROWS)

    def body(h_ref, wg_ref, wu_ref, a_ref, b_ref, u_ref):
        hv = h_ref[...]
        a = _nn(hv, wg_ref[...])
        b = _nn(hv, wu_ref[...])
        a_ref[...] = a.astype(a_ref.dtype)
        b_ref[...] = b.astype(b_ref.dtype)
        u_ref[...] = (a * (1.0 / (1.0 + jnp.exp(-a))) * b).astype(u_ref.dtype)

    w_spec = pl.BlockSpec((None, d, FF_SHARD), lambda j, i: (j, 0, 0))
    h_spec = pl.BlockSpec((None, tm, FF_SHARD), lambda j, i: (j, i, 0))
    hid = (N_SHARD, s, FF_SHARD)
    return pl.pallas_call(
        body, name=name, grid=(N_SHARD, s // tm),
        in_specs=[pl.BlockSpec((tm, d), lambda j, i: (i, 0)), w_spec, w_spec],
        out_specs=[h_spec, h_spec, h_spec],
        out_shape=[jax.ShapeDtypeStruct(hid, SAVED_DTYPE), jax.ShapeDtypeStruct(hid, SAVED_DTYPE),
                   jax.ShapeDtypeStruct(hid, MXU_DTYPE)])(hf, wg, wu)


def _ffn_down(u, wd, x, *, name):
    s, d = x.shape
    tm = _tile(s, 1024)

    def body(u_ref, w_ref, x_ref, o_ref):
        j = pl.program_id(1)

        @pl.when(j == 0)
        def _():
            o_ref[...] = x_ref[...]

        o_ref[...] += _nn(u_ref[...], w_ref[...])

    return pl.pallas_call(
        body, name=name, grid=(s // tm, N_SHARD),
        in_specs=[pl.BlockSpec((None, tm, FF_SHARD), lambda i, j: (j, i, 0)),
                  pl.BlockSpec((None, FF_SHARD, d), lambda i, j: (j, 0, 0)),
                  pl.BlockSpec((tm, d), lambda i, j: (i, 0))],
        out_specs=pl.BlockSpec((tm, d), lambda i, j: (i, 0)),
        out_shape=jax.ShapeDtypeStruct((s, d), F32))(u, wd, x)


def _ffn_bwd_hidden(dy, wd, a, b, *, name):
    s, d = dy.shape
    tm = _tile(s, FFN_ROWS)

    def body(dy_ref, w_ref, a_ref, b_ref, da_ref, db_ref):
        du = _nt(dy_ref[...], w_ref[...])
        av, bv = a_ref[...].astype(F32), b_ref[...].astype(F32)
        sg = 1.0 / (1.0 + jnp.exp(-av))
        da_ref[...] = (du * bv * (sg * (1.0 + av * (1.0 - sg)))).astype(da_ref.dtype)
        db_ref[...] = (du * (av * sg)).astype(db_ref.dtype)

    h_spec = pl.BlockSpec((None, tm, FF_SHARD), lambda j, i: (j, i, 0))
    hid = jax.ShapeDtypeStruct((N_SHARD, s, FF_SHARD), MXU_DTYPE)
    return pl.pallas_call(
        body, name=name, grid=(N_SHARD, s // tm),
        in_specs=[pl.BlockSpec((tm, d), lambda j, i: (i, 0)),
                  pl.BlockSpec((None, FF_SHARD, d), lambda j, i: (j, 0, 0)), h_spec, h_spec],
        out_specs=[h_spec, h_spec], out_shape=[hid, hid])(dy, wd, a, b)


def _ffn_bwd_dwd(u, dy, *, name):
    s, d = dy.shape
    tm = _tile(s, FFN_GRAD_ROWS)
    nm = s // tm

    def body(u_ref, dy_ref, o_ref, acc):
        i = pl.program_id(1)

        @pl.when(i == 0)
        def _():
            acc[...] = jnp.zeros_like(acc)

        acc[...] += _tn(u_ref[...], dy_ref[...])

        @pl.when(i == nm - 1)
        def _():
            o_ref[...] = acc[...].astype(o_ref.dtype)

    return pl.pallas_call(
        body, name=name, grid=(N_SHARD, nm),
        in_specs=[pl.BlockSpec((None, tm, FF_SHARD), lambda j, i: (j, i, 0)), pl.BlockSpec((tm, d), lambda j, i: (i, 0))],
        out_specs=pl.BlockSpec((None, FF_SHARD, d), lambda j, i: (j, 0, 0)),
        out_shape=jax.ShapeDtypeStruct((N_SHARD, FF_SHARD, d), WIRE_DTYPE),
        scratch_shapes=[pltpu.VMEM((FF_SHARD, d), F32)])(u, dy)


def _ffn_bwd_dwgu(hf, da, db, *, name):
    s, d = hf.shape
    tm = _tile(s, FFN_GRAD_ROWS)
    nm = s // tm

    def body(h_ref, da_ref, db_ref, og_ref, ou_ref, accg, accu):
        i = pl.program_id(1)

        @pl.when(i == 0)
        def _():
            accg[...] = jnp.zeros_like(accg)
            accu[...] = jnp.zeros_like(accu)

        hv = h_ref[...]
        accg[...] += _tn(hv, da_ref[...])
        accu[...] += _tn(hv, db_ref[...])

        @pl.when(i == nm - 1)
        def _():
            og_ref[...] = accg[...].astype(og_ref.dtype)
            ou_ref[...] = accu[...].astype(ou_ref.dtype)

    h_spec = pl.BlockSpec((None, tm, FF_SHARD), lambda j, i: (j, i, 0))
    w_spec = pl.BlockSpec((None, d, FF_SHARD), lambda j, i: (j, 0, 0))
    grad = jax.ShapeDtypeStruct((N_SHARD, d, FF_SHARD), WIRE_DTYPE)
    return pl.pallas_call(
        body, name=name, grid=(N_SHARD, nm),
        in_specs=[pl.BlockSpec((tm, d), lambda j, i: (i, 0)), h_spec, h_spec],
        out_specs=[w_spec, w_spec], out_shape=[grad, grad],
        scratch_shapes=[pltpu.VMEM((d, FF_SHARD), F32), pltpu.VMEM((d, FF_SHARD), F32)])(hf, da, db)


def _ffn_bwd_dh(da, db, wg, wu, *, name):
    s = da.shape[1]
    d = wg.shape[1]
    tm = _tile(s, 1024)

    def body(da_ref, db_ref, wg_ref, wu_ref, o_ref):
        j = pl.program_id(1)

        @pl.when(j == 0)
        def _():
            o_ref[...] = jnp.zeros_like(o_ref)

        o_ref[...] += _nt(da_ref[...], wg_ref[...]) + _nt(db_ref[...], wu_ref[...])

    h_spec = pl.BlockSpec((None, tm, FF_SHARD), lambda i, j: (j, i, 0))
    w_spec = pl.BlockSpec((None, d, FF_SHARD), lambda i, j: (j, 0, 0))
    return pl.pallas_call(
        body, name=name, grid=(s // tm, N_SHARD),
        in_specs=[h_spec, h_spec, w_spec, w_spec],
        out_specs=pl.BlockSpec((tm, d), lambda i, j: (i, 0)),
        out_shape=jax.ShapeDtypeStruct((s, d), F32))(da, db, wg, wu)


def _rope_tables(pos, inv, *, name):
    s = pos.shape[0]
    tm = _tile(s, 512)
    half = ROPE // 2

    def body(p_ref, i_ref, c_ref, s_ref):
        ang = p_ref[...] * i_ref[...]
        lane = lax.broadcasted_iota(jnp.int32, ang.shape, 1)
        live = lane < ROPE
        c_ref[...] = jnp.where(live, jnp.cos(ang), 0.0)
        sn = jnp.sin(ang)
        s_ref[...] = jnp.where(live, jnp.where(lane < half, -sn, sn), 0.0)

    out = jax.ShapeDtypeStruct((s, LANES), F32)
    return pl.pallas_call(
        body, name=name, grid=(s // tm,),
        in_specs=[pl.BlockSpec((tm, 1), lambda i: (i, 0)), pl.BlockSpec((1, LANES), lambda i: (0, 0))],
        out_specs=[pl.BlockSpec((tm, LANES), lambda i: (i, 0))] * 2, out_shape=[out, out])(pos, inv)


def _swap_halves(v):
    half = ROPE // 2
    lane = lax.broadcasted_iota(jnp.int32, v.shape, 1)
    return jnp.where(lane < half, pltpu.roll(v, LANES - half, 1), pltpu.roll(v, half, 1))


def _head_norm_rope_fwd(raw, g, cos, sin, *, name):
    s = raw.shape[0]
    tm = _tile(s, 256)
    width = N_HEADS * HEAD_PAD

    def body(x_ref, g_ref, c_ref, s_ref, o_ref):
        cv, sv = c_ref[...], s_ref[...]
        for h in range(N_HEADS):
            lo = h * HEAD_PAD
            xa = x_ref[:, lo:lo + NOPE]
            xb = x_ref[:, lo + NOPE:lo + HEAD_PAD]
            ms = (jnp.sum(xa * xa, axis=-1, keepdims=True) + jnp.sum(xb * xb, axis=-1, keepdims=True)) * (1.0 / QK_DIM)
            r = lax.rsqrt(ms + EPS)
            o_ref[:, lo:lo + NOPE] = (xa * r * g_ref[:, 0:NOPE]).astype(o_ref.dtype)
            yb = xb * r * g_ref[:, NOPE:HEAD_PAD]
            o_ref[:, lo + NOPE:lo + HEAD_PAD] = (yb * cv + _swap_halves(yb) * sv).astype(o_ref.dtype)

    row_spec = pl.BlockSpec((tm, width), lambda i: (i, 0))
    tab_spec = pl.BlockSpec((tm, LANES), lambda i: (i, 0))
    return pl.pallas_call(
        body, name=name, grid=(s // tm,),
        in_specs=[row_spec, pl.BlockSpec((1, HEAD_PAD), lambda i: (0, 0)), tab_spec, tab_spec],
        out_specs=row_spec, out_shape=jax.ShapeDtypeStruct((s, width), MXU_DTYPE))(raw, g, cos, sin)


def _head_norm_rope_bwd(raw, g, cos, sin, dout, *, name):
    s = raw.shape[0]
    tm = _tile(s, 256)
    width = N_HEADS * HEAD_PAD

    def body(x_ref, g_ref, c_ref, s_ref, do_ref, dx_ref, dg_ref):
        @pl.when(pl.program_id(0) == 0)
        def _():
            dg_ref[...] = jnp.zeros_like(dg_ref)

        cv, sv = c_ref[...], s_ref[...]
        ga, gb = g_ref[:, 0:NOPE], g_ref[:, NOPE:HEAD_PAD]
        for h in range(N_HEADS):
            lo = h * HEAD_PAD
            xa = x_ref[:, lo:lo + NOPE]
            xb = x_ref[:, lo + NOPE:lo + HEAD_PAD]
            dya = do_ref[:, lo:lo + NOPE]
            dob = do_ref[:, lo + NOPE:lo + HEAD_PAD]
            dyb = dob * cv + _swap_halves(dob * sv)
            ms = (jnp.sum(xa * xa, axis=-1, keepdims=True) + jnp.sum(xb * xb, axis=-1, keepdims=True)) * (1.0 / QK_DIM)
            r = lax.rsqrt(ms + EPS)
            xha, xhb = xa * r, xb * r
            gya, gyb = dya * ga, dyb * gb
            dot = (jnp.sum(gya * xha, axis=-1, keepdims=True) + jnp.sum(gyb * xhb, axis=-1, keepdims=True)) * (1.0 / QK_DIM)
            dx_ref[:, lo:lo + NOPE] = r * (gya - xha * dot)
            dx_ref[:, lo + NOPE:lo + HEAD_PAD] = r * (gyb - xhb * dot)
            dg_ref[:, 0:NOPE] += _fold8(dya * xha)
            dg_ref[:, NOPE:HEAD_PAD] += _fold8(dyb * xhb)

    row_spec = pl.BlockSpec((tm, width), lambda i: (i, 0))
    tab_spec = pl.BlockSpec((tm, LANES), lambda i: (i, 0))
    return pl.pallas_call(
        body, name=name, grid=(s // tm,),
        in_specs=[row_spec, pl.BlockSpec((1, HEAD_PAD), lambda i: (0, 0)), tab_spec, tab_spec, row_spec],
        out_specs=[row_spec, pl.BlockSpec((SUBLANES, HEAD_PAD), lambda i: (0, 0))],
        out_shape=[jax.ShapeDtypeStruct((s, width), F32), jax.ShapeDtypeStruct((SUBLANES, HEAD_PAD), F32)])(raw, g, cos, sin, dout)


def _k_assemble(kn, ckv, *, name):
    s = kn.shape[0]
    tm = _tile(s, 512)
    width = N_HEADS * HEAD_PAD

    def body(kn_ref, pe_ref, o_ref):
        pe = pe_ref[...]
        for h in range(N_HEADS):
            o_ref[:, h * HEAD_PAD:h * HEAD_PAD + NOPE] = kn_ref[:, h * NOPE:(h + 1) * NOPE]
            o_ref[:, h * HEAD_PAD + NOPE:(h + 1) * HEAD_PAD] = pe

    return pl.pallas_call(
        body, name=name, grid=(s // tm,),
        in_specs=[pl.BlockSpec((tm, N_HEADS * NOPE), lambda i: (i, 0)),
                  pl.BlockSpec((tm, LANES), lambda i: (i, KV_LORA // LANES))],
        out_specs=pl.BlockSpec((tm, width), lambda i: (i, 0)),
        out_shape=jax.ShapeDtypeStruct((s, width), F32))(kn, ckv)


def _k_disassemble(dk_raw, *, name):
    s = dk_raw.shape[0]
    tm = _tile(s, 512)
    width = N_HEADS * HEAD_PAD

    def body(dk_ref, dkn_ref, dpe_ref):
        pe = dk_ref[:, NOPE:HEAD_PAD]
        for h in range(N_HEADS):
            dkn_ref[:, h * NOPE:(h + 1) * NOPE] = dk_ref[:, h * HEAD_PAD:h * HEAD_PAD + NOPE]
            if h:
                pe = pe + dk_ref[:, h * HEAD_PAD + NOPE:(h + 1) * HEAD_PAD]
        dpe_ref[...] = pe

    return pl.pallas_call(
        body, name=name, grid=(s // tm,),
        in_specs=[pl.BlockSpec((tm, width), lambda i: (i, 0))],
        out_specs=[pl.BlockSpec((tm, N_HEADS * NOPE), lambda i: (i, 0)), pl.BlockSpec((tm, LANES), lambda i: (i, 0))],
        out_shape=[jax.ShapeDtypeStruct((s, N_HEADS * NOPE), F32), jax.ShapeDtypeStruct((s, LANES), F32)])(dk_raw)


ATTN_SCALE = 1.0 / math.sqrt(QK_DIM)
MASKED = -1e30


ATTN_TILE = 512
ATTN_HEADS = 8


def _chunk_mask(q0, k0, shape, q_axis):
    qpos = q0 + lax.broadcasted_iota(jnp.int32, shape, q_axis)
    kpos = k0 + lax.broadcasted_iota(jnp.int32, shape, 1 - q_axis)
    return kpos // CHUNK <= qpos // CHUNK


LOG2E = math.log2(math.e)
SCORE_LOG2 = ATTN_SCALE * LOG2E


def _causal_pairs(n, by_key):
    if by_key:
        pairs = [(i, j) for j in range(n) for i in range(j, n)]
    else:
        pairs = [(i, j) for i in range(n) for j in range(i + 1)]
    return jnp.asarray([p[0] for p in pairs], jnp.int32), jnp.asarray([p[1] for p in pairs], jnp.int32)


def _attn_fwd(q, k, vt, *, name):
    s = q.shape[0]
    t = _tile(s, ATTN_TILE)
    n = s // t
    qi_tab, kj_tab = _causal_pairs(n, by_key=False)

    hg = ATTN_HEADS

    def body(qi_ref, kj_ref, q_ref, k_ref, vt_ref, o_ref, lse_ref, m_sc, l_sc, acc):
        pair = pl.program_id(1)
        qi, kj = qi_ref[pair], kj_ref[pair]

        @pl.when(kj == 0)
        def _():
            m_sc[...] = jnp.full_like(m_sc, MASKED)
            l_sc[...] = jnp.zeros_like(l_sc)
            acc[...] = jnp.zeros_like(acc)

        def step(masked):
            for g in range(hg):
                qk, vr = slice(g * HEAD_PAD, (g + 1) * HEAD_PAD), slice(g * V_DIM, (g + 1) * V_DIM)
                st = _nt(k_ref[:, qk], q_ref[:, qk])
                if masked:
                    st = jnp.where(_chunk_mask(qi * t, kj * t, (t, t), 1), st, MASKED)
                m_prev = m_sc[g]
                m_new = jnp.maximum(m_prev, jnp.max(st, axis=0, keepdims=True) * SCORE_LOG2)
                alpha = jnp.exp2(m_prev - m_new)
                pt = jnp.exp2(st * SCORE_LOG2 - m_new)
                l_new = alpha * l_sc[g] + jnp.sum(pt, axis=0, keepdims=True)
                a_new = alpha * acc[vr, :] + _nn(vt_ref[vr, :], pt)
                l_sc[g] = l_new
                acc[vr, :] = a_new
                m_sc[g] = m_new
                if masked:
                    o_ref[vr, :] = a_new / l_new
                    lse_ref[g] = m_new + jnp.log(l_new) * LOG2E

        @pl.when(kj < qi)
        def _():
            step(False)

        @pl.when(kj == qi)
        def _():
            step(True)

    return pl.pallas_call(
        body, name=name,
        grid_spec=pltpu.PrefetchScalarGridSpec(
            num_scalar_prefetch=2, grid=(N_HEADS // hg, int(qi_tab.shape[0])),
            in_specs=[pl.BlockSpec((t, hg * HEAD_PAD), lambda h, p, qi, kj: (qi[p], h)),
                      pl.BlockSpec((t, hg * HEAD_PAD), lambda h, p, qi, kj: (kj[p], h)),
                      pl.BlockSpec((hg * V_DIM, t), lambda h, p, qi, kj: (h, kj[p]))],
            out_specs=[pl.BlockSpec((hg * V_DIM, t), lambda h, p, qi, kj: (h, qi[p])),
                       pl.BlockSpec((hg, 1, t), lambda h, p, qi, kj: (h, 0, qi[p]))],
            scratch_shapes=[pltpu.VMEM((hg, 1, t), F32), pltpu.VMEM((hg, 1, t), F32), pltpu.VMEM((hg * V_DIM, t), F32)]),
        out_shape=[jax.ShapeDtypeStruct((N_HEADS * V_DIM, s), F32), jax.ShapeDtypeStruct((N_HEADS, 1, s), F32)])(qi_tab, kj_tab, q, k, vt)


def _attn_delta(ot, dot, *, name):
    s = ot.shape[1]
    t = _tile(s, 1024)

    def body(o_ref, do_ref, d_ref):
        d_ref[...] = jnp.sum(o_ref[...] * do_ref[...], axis=0, keepdims=True)

    blk = pl.BlockSpec((V_DIM, t), lambda h, i: (h, i))
    return pl.pallas_call(
        body, name=name, grid=(N_HEADS, s // t), in_specs=[blk, blk],
        out_specs=pl.BlockSpec((None, 1, t), lambda h, i: (h, 0, i)),
        out_shape=jax.ShapeDtypeStruct((N_HEADS, 1, s), F32))(ot, dot)


def _attn_bwd_dq(q, k, v, do, lse_col, delta_col, *, name):
    s = q.shape[0]
    t = _tile(s, ATTN_TILE)
    n = s // t
    qi_tab, kj_tab = _causal_pairs(n, by_key=False)

    hg = ATTN_HEADS

    def body(qi_ref, kj_ref, q_ref, k_ref, v_ref, do_ref, lse_ref, dl_ref, dq_ref, acc):
        pair = pl.program_id(1)
        qi, kj = qi_ref[pair], kj_ref[pair]

        @pl.when(kj == 0)
        def _():
            acc[...] = jnp.zeros_like(acc)

        def step(masked):
            for g in range(hg):
                qk, vc = slice(g * HEAD_PAD, (g + 1) * HEAD_PAD), slice(g * V_DIM, (g + 1) * V_DIM)
                kv = k_ref[:, qk]
                sc = _nt(q_ref[:, qk], kv)
                if masked:
                    sc = jnp.where(_chunk_mask(qi * t, kj * t, (t, t), 0), sc, MASKED)
                p = jnp.exp2(sc * SCORE_LOG2 - lse_ref[g])
                dp = _nt(do_ref[:, vc], v_ref[:, vc])
                total = acc[:, qk] + _nn(p * (dp - dl_ref[g]), kv)
                acc[:, qk] = total
                if masked:
                    dq_ref[:, qk] = total * ATTN_SCALE

        @pl.when(kj < qi)
        def _():
            step(False)

        @pl.when(kj == qi)
        def _():
            step(True)

    col = pl.BlockSpec((hg, t, 1), lambda h, p, qi, kj: (h, qi[p], 0))
    return pl.pallas_call(
        body, name=name,
        grid_spec=pltpu.PrefetchScalarGridSpec(
            num_scalar_prefetch=2, grid=(N_HEADS // hg, int(qi_tab.shape[0])),
            in_specs=[pl.BlockSpec((t, hg * HEAD_PAD), lambda h, p, qi, kj: (qi[p], h)),
                      pl.BlockSpec((t, hg * HEAD_PAD), lambda h, p, qi, kj: (kj[p], h)),
                      pl.BlockSpec((t, hg * V_DIM), lambda h, p, qi, kj: (kj[p], h)),
                      pl.BlockSpec((t, hg * V_DIM), lambda h, p, qi, kj: (qi[p], h)), col, col],
            out_specs=pl.BlockSpec((t, hg * HEAD_PAD), lambda h, p, qi, kj: (qi[p], h)),
            scratch_shapes=[pltpu.VMEM((t, hg * HEAD_PAD), F32)]),
        out_shape=jax.ShapeDtypeStruct((s, N_HEADS * HEAD_PAD), F32))(qi_tab, kj_tab, q, k, v, do, lse_col, delta_col)


def _attn_bwd_dkv(q, k, v, do, lse_row, delta_row, dk_in, dv_in, *, name):
    s = q.shape[0]
    t = _tile(s, ATTN_TILE)
    n = s // t
    has_in = dk_in is not None
    hg = ATTN_HEADS
    qi_tab, kj_tab = _causal_pairs(n, by_key=True)

    def body(qi_ref, kj_ref, *refs):
        if has_in:
            q_ref, k_ref, v_ref, do_ref, lse_ref, dl_ref, dki_ref, dvi_ref, dk_ref, dv_ref, acck, accv = refs
        else:
            q_ref, k_ref, v_ref, do_ref, lse_ref, dl_ref, dk_ref, dv_ref, acck, accv = refs
        pair = pl.program_id(1)
        qi, kj = qi_ref[pair], kj_ref[pair]

        def step(masked):
            for g in range(hg):
                qk, vc = slice(g * HEAD_PAD, (g + 1) * HEAD_PAD), slice(g * V_DIM, (g + 1) * V_DIM)
                qv, dov = q_ref[:, qk], do_ref[:, vc]
                st = _nt(k_ref[:, qk], qv)
                if masked:
                    st = jnp.where(_chunk_mask(qi * t, kj * t, (t, t), 1), st, MASKED)
                pt = jnp.exp2(st * SCORE_LOG2 - lse_ref[g])
                accv[:, vc] += _nn(pt, dov)
                dpt = _nt(v_ref[:, vc], dov)
                acck[:, qk] += _nn(pt * (dpt - dl_ref[g]), qv)

        @pl.when(qi == kj)
        def _():
            acck[...] = jnp.zeros_like(acck)
            accv[...] = jnp.zeros_like(accv)
            step(True)

        @pl.when(qi > kj)
        def _():
            step(False)

        @pl.when(qi == n - 1)
        def _():
            dk = acck[...] * ATTN_SCALE
            dv = accv[...]
            if has_in:
                dk = dki_ref[...] + dk
                dv = dvi_ref[...] + dv
            dk_ref[...] = dk
            dv_ref[...] = dv

    row = pl.BlockSpec((hg, 1, t), lambda h, p, qi, kj: (h, 0, qi[p]))
    k_spec = pl.BlockSpec((t, hg * HEAD_PAD), lambda h, p, qi, kj: (kj[p], h))
    v_spec = pl.BlockSpec((t, hg * V_DIM), lambda h, p, qi, kj: (kj[p], h))
    in_specs = [pl.BlockSpec((t, hg * HEAD_PAD), lambda h, p, qi, kj: (qi[p], h)), k_spec, v_spec,
                pl.BlockSpec((t, hg * V_DIM), lambda h, p, qi, kj: (qi[p], h)), row, row]
    args = [q, k, v, do, lse_row, delta_row]
    if has_in:
        in_specs += [k_spec, v_spec]
        args += [dk_in, dv_in]
    return pl.pallas_call(
        body, name=name,
        grid_spec=pltpu.PrefetchScalarGridSpec(
            num_scalar_prefetch=2, grid=(N_HEADS // hg, int(qi_tab.shape[0])), in_specs=in_specs, out_specs=[k_spec, v_spec],
            scratch_shapes=[pltpu.VMEM((t, hg * HEAD_PAD), F32), pltpu.VMEM((t, hg * V_DIM), F32)]),
        out_shape=[jax.ShapeDtypeStruct((s, N_HEADS * HEAD_PAD), F32), jax.ShapeDtypeStruct((s, N_HEADS * V_DIM), F32)])(qi_tab, kj_tab, *args)


def _loss_head(y, target, *, name):
    s, d = y.shape
    tm = _tile(s, 512)

    def body(y_ref, t_ref, dy_ref, l_ref):
        @pl.when(pl.program_id(0) == 0)
        def _():
            l_ref[...] = jnp.zeros_like(l_ref)

        err = y_ref[...] - t_ref[...]
        dy_ref[...] = err * (1.0 / d)
        sq = _fold8(err * err)
        part = sq[:, 0:LANES]
        for cb in range(1, d // LANES):
            part = part + sq[:, cb * LANES:(cb + 1) * LANES]
        l_ref[...] += part * (0.5 / d)

    row_spec = pl.BlockSpec((tm, d), lambda i: (i, 0))
    return pl.pallas_call(
        body, name=name, grid=(s // tm,), in_specs=[row_spec, row_spec],
        out_specs=[row_spec, pl.BlockSpec((SUBLANES, LANES), lambda i: (0, 0))],
        out_shape=[jax.ShapeDtypeStruct((s, d), F32), jax.ShapeDtypeStruct((SUBLANES, LANES), F32)])(y, target)


ADAMW_ROWS = 512


def _adamw_math(w, m, v, g):
    mn = ADAM_B1 * m + (1.0 - ADAM_B1) * g
    vn = ADAM_B2 * v + (1.0 - ADAM_B2) * (g * g)
    m_hat = mn / (1.0 - ADAM_B1 ** ADAM_STEP)
    v_hat = vn / (1.0 - ADAM_B2 ** ADAM_STEP)
    return -ADAM_LR * (m_hat / (jnp.sqrt(v_hat) + ADAM_EPS) + ADAM_WD * w), mn, vn


def _adamw_vectors(ws, ms, vs, gs, *, name):
    n = len(ws)

    def body(*refs):
        ins, outs = refs[:4 * n], refs[4 * n:]
        for a in range(n):
            g = ins[3 * n + a][...]
            outs[a][...] = g
            outs[n + a][...], outs[2 * n + a][...], outs[3 * n + a][...] = _adamw_math(ins[a][...], ins[n + a][...], ins[2 * n + a][...], g)

    shapes = [jax.ShapeDtypeStruct(w.shape, F32) for w in ws]
    out = pl.pallas_call(body, name=name, in_specs=[VMEM_SPEC] * (4 * n), out_specs=[VMEM_SPEC] * (4 * n),
                         out_shape=shapes * 4)(*ws, *ms, *vs, *gs)
    return out[:n], out[n:2 * n], out[2 * n:3 * n], out[3 * n:]


def _adamw(w, m, v, g_parts, *, name):
    rows, cols = w.shape
    tm = _tile(rows, ADAMW_ROWS)
    n_parts = len(g_parts)

    def body(*refs):
        w_ref, m_ref, v_ref = refs[:3]
        g_refs = refs[3:3 + n_parts]
        g_out, d_out, m_out, v_out = refs[3 + n_parts:]
        g = g_refs[0][...]
        for r in g_refs[1:]:
            g = g + r[...]
        g_out[...] = g
        d_out[...], m_out[...], v_out[...] = _adamw_math(w_ref[...], m_ref[...], v_ref[...], g)

    spec = pl.BlockSpec((tm, cols), lambda i: (i, 0))
    out = jax.ShapeDtypeStruct((rows, cols), F32)
    return pl.pallas_call(
        body, name=name, grid=(rows // tm,), in_specs=[spec] * (3 + n_parts),
        out_specs=[spec] * 4, out_shape=[out] * 4)(w, m, v, *g_parts)


def _sum_slots(parts, *, name):
    _, rows, cols = parts.shape
    tm = _tile(rows, 512)

    def body(p_ref, o_ref):
        acc = p_ref[0].astype(F32)
        for k in range(1, N_SHARD):
            acc = acc + p_ref[k].astype(F32)
        o_ref[...] = acc

    return pl.pallas_call(
        body, name=name, grid=(rows // tm,),
        in_specs=[pl.BlockSpec((N_SHARD, tm, cols), lambda i: (0, i, 0))],
        out_specs=pl.BlockSpec((tm, cols), lambda i: (i, 0)),
        out_shape=jax.ShapeDtypeStruct((rows, cols), F32))(parts)


def _mesh_pos():
    return lax.axis_index("x"), lax.axis_index("y"), lax.axis_index("c")


CHIP_FLIPS = ((1, 0), (0, 1), (1, 1))


class Exchange(NamedTuple):
    kind: str
    srcs: tuple
    lands: tuple
    layer: Any = None


HBM_SPEC = pl.BlockSpec(memory_space=pltpu.HBM)
SEM_SPEC = pl.BlockSpec(memory_space=pltpu.SEMAPHORE)
DATAFLOW = pltpu.SideEffectType.DATAFLOW_SIDE_EFFECTING


def _exchange_copies(ex, src_refs, land_refs, send_sems, recv_sems):
    x, y, c = _mesh_pos()
    mine = 2 * x + y

    def slot(ref, chip):
        return ref.at[chip] if ex.layer is None else ref.at[chip, ex.layer]

    pairs = []
    for a, (src, land) in enumerate(zip(src_refs, land_refs)):
        for k, (fx, fy) in enumerate(CHIP_FLIPS):
            px, py = x ^ fx, y ^ fy
            peer = 2 * px + py
            src_part = src if ex.kind == "gather" else src.at[peer]
            pair = a * len(CHIP_FLIPS) + k
            common = dict(src_ref=src_part, send_sem=send_sems.at[pair], recv_sem=recv_sems.at[pair],
                          device_id=(px, py, c), device_id_type=MESH)
            pairs.append((pltpu.make_async_remote_copy(dst_ref=slot(land, mine), **common),
                          pltpu.make_async_remote_copy(dst_ref=slot(land, peer), **common)))
    return pairs


def _exchange_start(exchanges, *, name):
    srcs = [s for ex in exchanges for s in ex.srcs]
    lands = [b for ex in exchanges for b in ex.lands]
    n_arr, n_ex = len(srcs) + len(lands), len(exchanges)

    def body(*refs):
        src_refs, land_refs = refs[:len(srcs)], refs[len(srcs):n_arr]
        sems, token = refs[n_arr:n_arr + 2 * n_ex], refs[-1]
        at = 0
        for e, ex in enumerate(exchanges):
            n = len(ex.srcs)
            for send, _ in _exchange_copies(ex, src_refs[at:at + n], land_refs[at:at + n], sems[2 * e], sems[2 * e + 1]):
                send.start()
            at += n
        token[...] = jnp.zeros_like(token)

    sem_shapes = [pltpu.SemaphoreType.DMA((len(ex.srcs) * len(CHIP_FLIPS),)) for ex in exchanges for _ in range(2)]
    out = pl.pallas_call(
        body, name=name,
        out_shape=sem_shapes + [pltpu.HBM(a.shape, a.dtype) for a in srcs + lands] + [jax.ShapeDtypeStruct((SUBLANES, LANES), F32)],
        in_specs=[HBM_SPEC] * n_arr, out_specs=[SEM_SPEC] * (2 * n_ex) + [HBM_SPEC] * n_arr + [VMEM_SPEC],
        input_output_aliases={i: 2 * n_ex + i for i in range(n_arr)},
        compiler_params=pltpu.CompilerParams(has_side_effects=DATAFLOW),
    )(*[pltpu.with_memory_space_constraint(a, pltpu.HBM) for a in srcs + lands])
    sems, thru = out[:2 * n_ex], out[2 * n_ex:-1]
    pending, at = [], 0
    for e, ex in enumerate(exchanges):
        n = len(ex.srcs)
        pending.append((ex._replace(srcs=tuple(thru[at:at + n]), lands=tuple(thru[len(srcs) + at:len(srcs) + at + n])),
                        sems[2 * e], sems[2 * e + 1]))
        at += n
    return pending, out[-1]


def _exchange_wait(pending, after, *, name):
    ex, send_sems, recv_sems = pending
    n = len(ex.srcs)

    def body(*refs):
        src_refs, land_refs = refs[:n], refs[n:2 * n]
        for send, arrive in _exchange_copies(ex, src_refs, land_refs, refs[2 * n], refs[2 * n + 1]):
            send.wait_send()
            arrive.wait_recv()

    arrays = list(ex.srcs) + list(ex.lands)
    out = pl.pallas_call(
        body, name=name, out_shape=[pltpu.HBM(a.shape, a.dtype) for a in arrays],
        in_specs=[HBM_SPEC] * (2 * n) + [SEM_SPEC, SEM_SPEC, ANY], out_specs=[HBM_SPEC] * (2 * n),
        input_output_aliases={i: i for i in range(2 * n)},
        compiler_params=pltpu.CompilerParams(has_side_effects=DATAFLOW),
    )(*arrays, send_sems, recv_sems, after)
    return out[n:]


def _swap_with_sibling(arrays, *, name):
    n = len(arrays)

    def body(*refs):
        ins, outs = refs[:n], refs[n:2 * n]
        send_sems, recv_sems = refs[2 * n:]
        x, y, c = _mesh_pos()
        copies = []
        for a in range(n):
            cp = pltpu.make_async_remote_copy(
                src_ref=ins[a], dst_ref=outs[a], send_sem=send_sems.at[a], recv_sem=recv_sems.at[a],
                device_id=(x, y, 1 - c), device_id_type=MESH)
            cp.start()
            copies.append(cp)
        for cp in copies:
            cp.wait()

    return pl.pallas_call(
        body, name=name, in_specs=[ANY] * n, out_specs=[ANY] * n,
        out_shape=[jax.ShapeDtypeStruct(a.shape, a.dtype) for a in arrays],
        scratch_shapes=[pltpu.SemaphoreType.DMA((n,)), pltpu.SemaphoreType.DMA((n,))])(*arrays)


N_DEV = 8


def _all_reduce_small(vec, *, name):
    rows = vec.shape[0]

    def body(v_ref, o_ref, land, send_sems, recv_sems):
        x, y, c = _mesh_pos()
        me = 4 * x + 2 * y + c
        land[me] = v_ref[...]
        copies = []
        for k in range(1, N_DEV):
            fx, fy, fc = (k >> 2) & 1, (k >> 1) & 1, k & 1
            px, py, pc = x ^ fx, y ^ fy, c ^ fc
            send = pltpu.make_async_remote_copy(
                src_ref=v_ref, dst_ref=land.at[me], send_sem=send_sems.at[k - 1], recv_sem=recv_sems.at[k - 1],
                device_id=(px, py, pc), device_id_type=MESH)
            send.start()
            arrive = pltpu.make_async_remote_copy(
                src_ref=v_ref, dst_ref=land.at[4 * px + 2 * py + pc], send_sem=send_sems.at[k - 1], recv_sem=recv_sems.at[k - 1],
                device_id=(px, py, pc), device_id_type=MESH)
            copies.append((send, arrive))
        for send, arrive in copies:
            send.wait_send()
            arrive.wait_recv()
        acc = land[0]
        for k in range(1, N_DEV):
            acc = acc + land[k]
        o_ref[...] = acc

    return pl.pallas_call(
        body, name=name, in_specs=[VMEM_SPEC], out_specs=VMEM_SPEC,
        out_shape=jax.ShapeDtypeStruct(vec.shape, F32),
        scratch_shapes=[pltpu.VMEM((N_DEV, rows, LANES), F32), pltpu.SemaphoreType.DMA((N_DEV - 1,)),
                        pltpu.SemaphoreType.DMA((N_DEV - 1,))])(vec)


PACK_UNIT = SUBLANES * LANES * 2


def _padded(n):
    return -(-n // PACK_UNIT) * PACK_UNIT


def _pack(arrays, dtype, lead=0):
    parts = []
    for a in arrays:
        lead_shape = a.shape[:lead]
        flat = a.astype(dtype).reshape(lead_shape + (-1,))
        n = flat.shape[-1]
        flat = jnp.pad(flat, [(0, 0)] * lead + [(0, _padded(n) - n)])
        parts.append(flat.reshape(lead_shape + (-1, LANES)))
    return jnp.concatenate(parts, axis=lead)


def _unpack(buf, shapes, lead=0):
    out, row = [], 0
    for shp in shapes:
        n = math.prod(shp)
        rows = _padded(n) // LANES
        part = lax.slice_in_dim(buf, row, row + rows, axis=lead)
        lead_shape = part.shape[:lead]
        part = part.reshape(lead_shape + (-1,))
        part = lax.slice_in_dim(part, 0, n, axis=lead)
        out.append(part.reshape(lead_shape + tuple(shp)))
        row += rows
    return out


def kernel(x, positions, ln_mix_a, w_pool, b_pool, pool_scale, ln_ffn, w_gate, w_up, w_down, ln_kv, w_dkv, g_kv_latent, w_uk, w_uv, g_k, ln_mix_b, w_dq, g_q_latent, w_uq, g_q, w_o, loss_target, m_ln_mix_a, m_w_pool, m_b_pool, m_pool_scale, m_ln_ffn, m_w_gate, m_w_up, m_w_down, m_ln_kv, m_w_dkv, m_g_kv_latent, m_w_uk, m_w_uv, m_g_k, m_ln_mix_b, m_w_dq, m_g_q_latent, m_w_uq, m_g_q, m_w_o, v_ln_mix_a, v_w_pool, v_b_pool, v_pool_scale, v_ln_ffn, v_w_gate, v_w_up, v_w_down, v_ln_kv, v_w_dkv, v_g_kv_latent, v_w_uk, v_w_uv, v_g_k, v_ln_mix_b, v_w_dq, v_g_q_latent, v_w_uq, v_g_q, v_w_o):
    weights = dict(ln_mix_a=ln_mix_a, w_pool=w_pool, b_pool=b_pool, pool_scale=pool_scale, ln_ffn=ln_ffn, w_gate=w_gate,
                   w_up=w_up, w_down=w_down, ln_kv=ln_kv, w_dkv=w_dkv, g_kv_latent=g_kv_latent, w_uk=w_uk, w_uv=w_uv, g_k=g_k,
                   ln_mix_b=ln_mix_b, w_dq=w_dq, g_q_latent=g_q_latent, w_uq=w_uq, g_q=g_q, w_o=w_o)
    mom_m = dict(ln_mix_a=m_ln_mix_a, w_pool=m_w_pool, b_pool=m_b_pool, pool_scale=m_pool_scale, ln_ffn=m_ln_ffn,
                 w_gate=m_w_gate, w_up=m_w_up, w_down=m_w_down, ln_kv=m_ln_kv, w_dkv=m_w_dkv, g_kv_latent=m_g_kv_latent,
                 w_uk=m_w_uk, w_uv=m_w_uv, g_k=m_g_k, ln_mix_b=m_ln_mix_b, w_dq=m_w_dq, g_q_latent=m_g_q_latent,
                 w_uq=m_w_uq, g_q=m_g_q, w_o=m_w_o)
    mom_v = dict(ln_mix_a=v_ln_mix_a, w_pool=v_w_pool, b_pool=v_b_pool, pool_scale=v_pool_scale, ln_ffn=v_ln_ffn,
                 w_gate=v_w_gate, w_up=v_w_up, w_down=v_w_down, ln_kv=v_ln_kv, w_dkv=v_w_dkv, g_kv_latent=v_g_kv_latent,
                 w_uk=v_w_uk, w_uv=v_w_uv, g_k=v_g_k, ln_mix_b=v_ln_mix_b, w_dq=v_w_dq, g_q_latent=v_g_q_latent,
                 w_uq=v_w_uq, g_q=v_g_q, w_o=v_w_o)
    order = list(weights)
    s = x.shape[1]
    d = D_MODEL
    xs = x.reshape(s, d)
    target = loss_target.reshape(s, d)
    my_chip = 2 * lax.axis_index("x") + lax.axis_index("y")

    mat_names = ("w_pool", "w_dkv", "w_uk", "w_uv", "w_dq", "w_uq", "w_o")
    vec_names = ("ln_mix_a", "b_pool", "pool_scale")
    mat_shapes = [weights[n].shape for n in mat_names]
    vec_shapes = [weights[n].shape for n in vec_names]

    def rows_of(a, lead=0):
        return a.reshape(a.shape[:lead] + (-1, a.shape[-1]))

    mats_local = tuple(rows_of(weights[n].astype(WIRE_DTYPE)) for n in mat_names)
    vecs_local = _pack([weights[n] for n in vec_names], F32)

    def landing(shard):
        return lax.dynamic_update_slice_in_dim(lax.empty((N_SHARD,) + shard.shape, shard.dtype), shard[None], my_chip, axis=0)

    def gather_of(shards):
        return Exchange("gather", tuple(shards), tuple(landing(sh) for sh in shards))

    ffn_local = [tuple(w[l].astype(WIRE_DTYPE) for w in (w_gate, w_up, w_down)) for l in range(DEPTH)]
    gathers = ([gather_of(mats_local[:1] + (vecs_local,))] + [gather_of(ffn_local[l]) for l in range(N_A)]
               + [gather_of(mats_local[1:])] + [gather_of(ffn_local[l]) for l in range(N_A, DEPTH)])
    gathering, _ = _exchange_start(gathers, name="gather_start")
    ffn_gather = {l: gathering[1 + l + (l >= N_A)] for l in range(DEPTH)}

    inv = ROPE_THETA ** (-jnp.arange(ROPE // 2, dtype=F32) * 2.0 / ROPE)
    inv_lanes = jnp.concatenate([inv, inv, jnp.zeros((LANES - ROPE,), F32)]).reshape(1, LANES)
    cos_t, sin_t = _rope_tables(positions.reshape(s, 1).astype(F32), inv_lanes, name="rope_tables")

    g_pool, vecs_all = _exchange_wait(gathering[0], cos_t, name="gather_wait_small")
    g_lna, g_bp, g_ps = _unpack(vecs_all, vec_shapes, lead=1)
    wpool_f = g_pool.reshape((N_SHARD,) + mat_shapes[0]).transpose(1, 2, 0, 3, 4).reshape(N_A, N_GROUPS, GROUP_DIM, GROUP_DIM)
    bpool_f = g_bp.transpose(1, 2, 0, 3).reshape(N_A, 1, d)
    pscale_f = g_ps.transpose(1, 0, 2).reshape(N_A, 1, d)
    lna_f = g_lna.transpose(1, 0, 2).reshape(N_A, 1, d)

    def head_gain(g):
        return jnp.pad(g.reshape(1, QK_DIM), ((0, 0), (0, HEAD_PAD - QK_DIM)))

    ffn_w = [None] * DEPTH

    def ffn_fwd(xin, layer):
        hf = _rms_fwd(xin, ln_ffn[layer].reshape(1, d), n=d, name="ffn_norm")
        ffn_w[layer] = wg, wu, wd = _exchange_wait(ffn_gather[layer], hf, name=f"gather_wait_{layer}")
        a, b, u = _ffn_up(hf, wg, wu, name="ffn_up")
        return _ffn_down(u, wd, xin, name="ffn_down"), (xin, hf, a, b, u)

    saved_a, saved_b, saved_f = [], [], []
    cur = xs
    for l in range(N_A):
        dpool = _rms_pool_fwd(cur, lna_f[l], name="pool_fwd")
        x1 = _pool_mm_fwd(dpool, wpool_f[l], bpool_f[l], pscale_f[l], cur, name="pool_mm")
        saved_a.append((cur, dpool))
        cur, sf = ffn_fwd(x1, l)
        saved_f.append(sf)

    x_kv = cur
    hk = _rms_fwd(x_kv, ln_kv.reshape(1, d), n=d, name="kv_norm")
    g_dkv, g_uk, g_uv, g_dq, g_uq, g_o = (a.reshape((N_SHARD,) + shp) for a, shp in zip(
        _exchange_wait(gathering[1 + N_A], hk, name="gather_wait_attn"), mat_shapes[1:]))
    wdkv_f = jnp.pad(g_dkv.reshape(d, KV_LORA + ROPE), ((0, 0), (0, CKV_PAD - KV_LORA - ROPE)))
    wuk_f = g_uk.transpose(1, 0, 2).reshape(KV_LORA, N_HEADS * NOPE)
    wuv_f = g_uv.transpose(1, 0, 2).reshape(KV_LORA, N_HEADS * V_DIM)
    wdq_f = g_dq.transpose(1, 0, 2, 3).reshape(N_B, d, Q_LORA)
    wuq_f = jnp.pad(g_uq.transpose(1, 2, 0, 3).reshape(N_B, Q_LORA, N_HEADS, QK_DIM),
                    ((0, 0), (0, 0), (0, 0), (0, HEAD_PAD - QK_DIM))).reshape(N_B, Q_LORA, N_HEADS * HEAD_PAD)
    wo_f = g_o.transpose(1, 0, 2, 3).reshape(N_B, d, d)
    ckv = _mm(hk, wdkv_f, name="kv_down")
    c_lat = _rms_fwd(ckv, g_kv_latent.reshape(1, KV_LORA), n=KV_LORA, name="kv_latent_norm")
    kn_raw = _mm(c_lat, wuk_f, name="k_up")
    v_all = _mm(c_lat, wuv_f, out_dtype=MXU_DTYPE, name="v_up")
    vt_all = _mm(wuv_f.T, c_lat, tb=True, out_dtype=MXU_DTYPE, name="v_up_t")
    k_raw = _k_assemble(kn_raw, ckv, name="k_assemble")
    gk_pad = head_gain(g_k)
    k_cat = _head_norm_rope_fwd(k_raw, gk_pad, cos_t, sin_t, name="k_norm_rope")

    for j in range(N_B):
        l = N_A + j
        hq = _rms_fwd(cur, ln_mix_b[j].reshape(1, d), n=d, name="q_norm")
        cq_raw = _mm(hq, wdq_f[j], name="q_down")
        cq = _rms_fwd(cq_raw, g_q_latent[j].reshape(1, Q_LORA), n=Q_LORA, name="q_latent_norm")
        q_raw = _mm(cq, wuq_f[j], name="q_up")
        gq_pad = head_gain(g_q[j])
        q_cat = _head_norm_rope_fwd(q_raw, gq_pad, cos_t, sin_t, name="q_norm_rope")
        ot, lse = _attn_fwd(q_cat, k_cat, vt_all, name="attn_fwd")
        x1 = _mm(ot, wo_f[j], ta=True, resid=cur, name="attn_out")
        saved_b.append((cur, hq, cq_raw, cq, q_raw, gq_pad, q_cat, ot, lse))
        cur, sf = ffn_fwd(x1, l)
        saved_f.append(sf)

    dy, loss_part = _loss_head(cur, target, name="loss_head")

    ffn_landed = (lax.empty((N_SHARD, DEPTH, d, FF_SHARD), WIRE_DTYPE), lax.empty((N_SHARD, DEPTH, d, FF_SHARD), WIRE_DTYPE),
                  lax.empty((N_SHARD, DEPTH, FF_SHARD, d), WIRE_DTYPE))
    scattering = None
    grads = {}
    d_ln_ffn = [None] * DEPTH

    def own_part(full):
        return lax.dynamic_index_in_dim(full, my_chip, axis=0, keepdims=True)

    def ffn_bwd(dyv, layer):
        nonlocal ffn_landed, scattering
        xin, hf, a, b, u = saved_f[layer]
        wg, wu, wd = ffn_w[layer]
        da, db = _ffn_bwd_hidden(dyv, wd, a, b, name="ffn_bwd_hidden")
        dwd = _ffn_bwd_dwd(u, dyv, name="ffn_bwd_dwd")
        dwg, dwu = _ffn_bwd_dwgu(hf, da, db, name="ffn_bwd_dwgu")
        if scattering is not None:
            ffn_landed = _exchange_wait(scattering, dwg, name=f"scatter_wait_{layer + 1}")
        ffn_landed = tuple(lax.dynamic_update_slice(buf, own_part(g)[:, None], (my_chip, layer, 0, 0))
                           for buf, g in zip(ffn_landed, (dwg, dwu, dwd)))
        (scattering,), started = _exchange_start([Exchange("scatter", (dwg, dwu, dwd), ffn_landed, layer)],
                                                 name=f"scatter_start_{layer}")
        dhf = _ffn_bwd_dh(da, db, wg, wu, name="ffn_bwd_dh")
        dx, dg = _rms_bwd(xin, ln_ffn[layer].reshape(1, d), dhf, n=d, dx_in=dyv, after=started, name="ffn_norm_bwd")
        d_ln_ffn[layer] = dg.sum(axis=0)
        return dx

    dk_acc = dv_acc = None
    d_ln_mix_b, d_w_dq, d_g_q_latent, d_w_uq, d_g_q, d_w_o = ([None] * N_B for _ in range(6))
    dcur = dy
    for j in reversed(range(N_B)):
        l = N_A + j
        xin, hq, cq_raw, cq, q_raw, gq_pad, q_cat, ot, lse = saved_b[j]
        dx1 = ffn_bwd(dcur, l)
        do = _mm(dx1, wo_f[j], tb=True, out_dtype=MXU_DTYPE, name="attn_out_bwd")
        dot = _mm(wo_f[j], dx1, tb=True, name="attn_out_bwd_t")
        d_w_o[j] = _mm_tn(ot, dx1, at=True, name="attn_out_dw")
        delta = _attn_delta(ot, dot, name="attn_delta")
        lse_col, delta_col = lse.reshape(N_HEADS, s, 1), delta.reshape(N_HEADS, s, 1)
        dq_cat = _attn_bwd_dq(q_cat, k_cat, v_all, do, lse_col, delta_col, name="attn_bwd_dq")
        dk_acc, dv_acc = _attn_bwd_dkv(q_cat, k_cat, v_all, do, lse, delta, dk_acc, dv_acc, name="attn_bwd_dkv")
        dq_raw, dgq = _head_norm_rope_bwd(q_raw, gq_pad, cos_t, sin_t, dq_cat, name="q_norm_rope_bwd")
        d_g_q[j] = dgq.sum(axis=0)[:QK_DIM]
        dcq = _mm(dq_raw, wuq_f[j], tb=True, name="q_up_bwd")
        d_w_uq[j] = _mm_tn(cq, dq_raw, name="q_up_dw").reshape(Q_LORA, N_HEADS, HEAD_PAD)[:, :, :QK_DIM].reshape(Q_LORA, N_HEADS * QK_DIM)
        dcq_raw, dgl = _rms_bwd(cq_raw, g_q_latent[j].reshape(1, Q_LORA), dcq, n=Q_LORA, name="q_latent_norm_bwd")
        d_g_q_latent[j] = dgl.sum(axis=0)
        dhq = _mm(dcq_raw, wdq_f[j], tb=True, name="q_down_bwd")
        d_w_dq[j] = _mm_tn(hq, dcq_raw, name="q_down_dw")
        dcur, dgm = _rms_bwd(xin, ln_mix_b[j].reshape(1, d), dhq, n=d, dx_in=dx1, name="q_norm_bwd")
        d_ln_mix_b[j] = dgm.sum(axis=0)

    dk_raw, dgk = _head_norm_rope_bwd(k_raw, gk_pad, cos_t, sin_t, dk_acc, name="k_norm_rope_bwd")
    grads["g_k"] = dgk.sum(axis=0)[:QK_DIM]
    dc = _mm(dv_acc, wuv_f, tb=True, name="v_up_bwd")
    grads["w_uv"] = _mm_tn(c_lat, dv_acc, name="v_up_dw")
    dkn, dpe = _k_disassemble(dk_raw, name="k_disassemble")
    dc = _mm(dkn, wuk_f, tb=True, resid=dc, name="k_up_bwd")
    grads["w_uk"] = _mm_tn(c_lat, dkn, name="k_up_dw")
    dc_raw, dgl = _rms_bwd(ckv, g_kv_latent.reshape(1, KV_LORA), dc, n=KV_LORA, name="kv_latent_norm_bwd")
    grads["g_kv_latent"] = dgl.sum(axis=0)
    dckv = jnp.concatenate([dc_raw, dpe], axis=1)
    dhk = _mm(dckv, wdkv_f, tb=True, name="kv_down_bwd")
    grads["w_dkv"] = _mm_tn(hk, dckv, name="kv_down_dw")[:, :KV_LORA + ROPE]
    gm = {
        "w_dkv": grads["w_dkv"].reshape(N_SHARD, d // N_SHARD, KV_LORA + ROPE),
        "w_uk": grads["w_uk"].reshape(KV_LORA, N_SHARD, -1).transpose(1, 0, 2),
        "w_uv": grads["w_uv"].reshape(KV_LORA, N_SHARD, -1).transpose(1, 0, 2),
        "w_dq": jnp.stack(d_w_dq).reshape(N_B, N_SHARD, d // N_SHARD, Q_LORA).transpose(1, 0, 2, 3),
        "w_uq": jnp.stack(d_w_uq).reshape(N_B, Q_LORA, N_SHARD, -1).transpose(2, 0, 1, 3),
        "w_o": jnp.stack(d_w_o).reshape(N_B, N_SHARD, d // N_SHARD, d).transpose(1, 0, 2, 3),
    }

    def scatter_of(partials):
        zones = tuple(lax.dynamic_update_slice_in_dim(lax.empty(g.shape, WIRE_DTYPE), own_part(g), my_chip, axis=0) for g in partials)
        return Exchange("scatter", tuple(partials), zones)

    attn_partials = [rows_of(gm[n].astype(WIRE_DTYPE), lead=1) for n in mat_names[1:]]
    (attn_scatter,), started = _exchange_start([scatter_of(attn_partials)], name="scatter_start_attn")
    dcur, dg = _rms_bwd(x_kv, ln_kv.reshape(1, d), dhk, n=d, dx_in=dcur, after=started, name="kv_norm_bwd")
    grads["ln_kv"] = dg.sum(axis=0)

    d_ln_mix_a, d_w_pool, d_b_pool, d_pool_scale = ([None] * N_A for _ in range(4))
    for l in reversed(range(N_A)):
        xin, dpool = saved_a[l]
        dx1 = ffn_bwd(dcur, l)
        dd, dwp, dbp, dsp = _pool_mm_bwd(dpool, wpool_f[l], bpool_f[l], pscale_f[l], dx1, name="pool_mm_bwd")
        d_w_pool[l], d_b_pool[l], d_pool_scale[l] = dwp, dbp.sum(axis=0), dsp.sum(axis=0)
        dcur, dg = _rms_pool_bwd(xin, lna_f[l], dd, dx1, name="pool_bwd")
        d_ln_mix_a[l] = dg.sum(axis=0)
    grad_x = dcur.reshape(1, s, d)

    pool_partial = jnp.stack(d_w_pool).reshape(N_A, N_GROUPS, N_SHARD, GROUP_DIM // N_SHARD, GROUP_DIM).transpose(2, 0, 1, 3, 4)
    (pool_scatter,), _ = _exchange_start([scatter_of([rows_of(pool_partial.astype(WIRE_DTYPE), lead=1)])], name="scatter_start_small")
    attn_landed = _exchange_wait(attn_scatter, dcur, name="scatter_wait_attn")
    ffn_landed = _exchange_wait(scattering, attn_landed[0], name="scatter_wait_0")
    pool_landed = _exchange_wait(pool_scatter, ffn_landed[0], name="scatter_wait_small")
    landed = [ffn_landed[0].reshape(N_SHARD, DEPTH * d, FF_SHARD), ffn_landed[1].reshape(N_SHARD, DEPTH * d, FF_SHARD),
              ffn_landed[2].reshape(N_SHARD, DEPTH * FF_SHARD, d), *pool_landed, *attn_landed]
    chip_sums = [_sum_slots(p, name="sum_chips") for p in landed]
    sib_sums = _swap_with_sibling(chip_sums, name="swap_sibling")

    vec_full = {
        "ln_mix_a": jnp.stack(d_ln_mix_a), "b_pool": jnp.stack(d_b_pool).reshape(N_A, N_GROUPS, GROUP_DIM),
        "pool_scale": jnp.stack(d_pool_scale), "ln_ffn": jnp.stack(d_ln_ffn), "ln_kv": grads["ln_kv"],
        "g_kv_latent": grads["g_kv_latent"], "g_k": grads["g_k"], "ln_mix_b": jnp.stack(d_ln_mix_b),
        "g_q_latent": jnp.stack(d_g_q_latent), "g_q": jnp.stack(d_g_q),
    }
    small_names = list(vec_full)
    small_shapes = [vec_full[n].shape for n in small_names] + [(SUBLANES * LANES,)]
    small = _all_reduce_small(_pack([vec_full[n] for n in small_names] + [loss_part.reshape(-1)], F32), name="all_reduce_small")
    small_sum = _unpack(small, small_shapes)
    loss = jnp.sum(small_sum[-1])
    vec_grad = dict(zip(small_names, small_sum[:-1]))
    vec_grad["ln_mix_a"] = lax.dynamic_slice_in_dim(vec_grad["ln_mix_a"], my_chip * (d // N_SHARD), d // N_SHARD, axis=1)
    vec_grad["pool_scale"] = lax.dynamic_slice_in_dim(vec_grad["pool_scale"], my_chip * (d // N_SHARD), d // N_SHARD, axis=1)
    vec_grad["b_pool"] = lax.dynamic_slice_in_dim(vec_grad["b_pool"], my_chip * (GROUP_DIM // N_SHARD), GROUP_DIM // N_SHARD, axis=2)

    out_g, out_d, out_m, out_v = {}, {}, {}, {}
    for idx, (nm, rows, cols) in enumerate((("w_gate", DEPTH * d, FF_SHARD), ("w_up", DEPTH * d, FF_SHARD), ("w_down", DEPTH * FF_SHARD, d))):
        res = _adamw(weights[nm].reshape(rows, cols), mom_m[nm].reshape(rows, cols), mom_v[nm].reshape(rows, cols),
                     [chip_sums[idx], sib_sums[idx]], name="adamw_ffn")
        shp = weights[nm].shape
        out_g[nm], out_d[nm], out_m[nm], out_v[nm] = (r.reshape(shp) for r in res)

    for idx, nm in enumerate(mat_names, start=3):
        res = _adamw(rows_of(weights[nm]), rows_of(mom_m[nm]), rows_of(mom_v[nm]), [chip_sums[idx], sib_sums[idx]], name="adamw_mat")
        shp = weights[nm].shape
        out_g[nm], out_d[nm], out_m[nm], out_v[nm] = (r.reshape(shp) for r in res)

    def as_rows(a):
        return a.reshape(1, -1) if a.ndim == 1 else rows_of(a)

    res = _adamw_vectors([as_rows(weights[n]) for n in small_names], [as_rows(mom_m[n]) for n in small_names],
                         [as_rows(mom_v[n]) for n in small_names],
                         [as_rows(vec_grad[n].reshape(weights[n].shape)) for n in small_names], name="adamw_vectors")
    for tgt, arrs in zip((out_g, out_d, out_m, out_v), res):
        for n, arr in zip(small_names, arrs):
            tgt[n] = arr.reshape(weights[n].shape)

    return (loss, grad_x, *[out_g[n] for n in order], *[out_d[n] for n in order],
            *[out_m[n] for n in order], *[out_v[n] for n in order])
```

```python
import math
from typing import Any, NamedTuple

import jax
import jax.numpy as jnp
from jax import lax
from jax.experimental import pallas as pl
from jax.experimental.pallas import tpu as pltpu

F32 = jnp.float32
BF16 = jnp.bfloat16
MXU_DTYPE = BF16
WIRE_DTYPE = BF16
SAVED_DTYPE = BF16

D_MODEL = 1024
N_A = 2
N_B = 2
DEPTH = 4
POOL_WINDOWS = (2, 4, 8, 16)
N_GROUPS = 4
GROUP_DIM = 256
POOL_HALO = 16
N_HEADS = 8
NOPE = 128
ROPE = 64
QK_DIM = 192
HEAD_PAD = 256
V_DIM = 128
Q_LORA = 256
KV_LORA = 512
CKV_PAD = 640
ROPE_THETA = 10000.0
CHUNK = 64
EPS = 1e-6
N_SHARD = 4
FF_SHARD = 704
FFN_ROWS = 1024
FFN_GRAD_ROWS = 2048
LANES = 128
SUBLANES = 8
ADAM_LR, ADAM_B1, ADAM_B2, ADAM_EPS, ADAM_WD, ADAM_STEP = 0.001, 0.9, 0.999, 1e-08, 0.01, 10
MESH = pl.DeviceIdType.MESH
ANY = pl.BlockSpec(memory_space=pl.ANY)
VMEM_SPEC = pl.BlockSpec(memory_space=pltpu.VMEM)


def _tile(n, pref):
    if n <= pref:
        return n
    t = pref - pref % SUBLANES
    while n % t:
        t -= SUBLANES
    return t


def _fold8(v):
    r, n = v.shape
    return v.reshape(r // SUBLANES, SUBLANES, n).sum(axis=0)


def _dot(a, b, dims):
    return lax.dot_general(a.astype(MXU_DTYPE), b.astype(MXU_DTYPE), (dims, ((), ())),
                           preferred_element_type=F32)


def _nn(a, b):
    return _dot(a, b, ((1,), (0,)))


def _nt(a, b):
    return _dot(a, b, ((1,), (1,)))


def _tn(a, b):
    return _dot(a, b, ((0,), (0,)))


def _mm(a, b, *, ta=False, tb=False, resid=None, out_dtype=F32, name):
    assert not (ta and tb)
    m, k = (a.shape[1], a.shape[0]) if ta else a.shape
    n = b.shape[0] if tb else b.shape[1]
    tm, tn = _tile(m, 512), _tile(n, 1024)

    def body(*refs):
        if resid is None:
            a_ref, b_ref, o_ref = refs
        else:
            a_ref, b_ref, r_ref, o_ref = refs
        acc = (_tn if ta else _nt if tb else _nn)(a_ref[...], b_ref[...])
        if resid is not None:
            acc = r_ref[...] + acc
        o_ref[...] = acc.astype(o_ref.dtype)

    in_specs = [pl.BlockSpec((k, tm), lambda i, j: (0, i)) if ta else pl.BlockSpec((tm, k), lambda i, j: (i, 0)),
                pl.BlockSpec((tn, k), lambda i, j: (j, 0)) if tb else pl.BlockSpec((k, tn), lambda i, j: (0, j))]
    args = [a, b]
    if resid is not None:
        in_specs.append(pl.BlockSpec((tm, tn), lambda i, j: (i, j)))
        args.append(resid)
    return pl.pallas_call(
        body, name=name, grid=(m // tm, n // tn), in_specs=in_specs,
        out_specs=pl.BlockSpec((tm, tn), lambda i, j: (i, j)),
        out_shape=jax.ShapeDtypeStruct((m, n), out_dtype))(*args)


def _mm_tn(a, b, *, name, at=False, out_dtype=F32):
    m = b.shape[0]
    k1 = a.shape[0] if at else a.shape[1]
    n = b.shape[1]
    tm, tn = _tile(m, 512), _tile(n, 1024)
    nm = m // tm

    def body(a_ref, b_ref, o_ref, acc):
        i = pl.program_id(1)

        @pl.when(i == 0)
        def _():
            acc[...] = jnp.zeros_like(acc)

        acc[...] += (_nn if at else _tn)(a_ref[...], b_ref[...])

        @pl.when(i == nm - 1)
        def _():
            o_ref[...] = acc[...].astype(o_ref.dtype)

    return pl.pallas_call(
        body, name=name, grid=(n // tn, nm),
        in_specs=[pl.BlockSpec((k1, tm), lambda j, i: (0, i)) if at else pl.BlockSpec((tm, k1), lambda j, i: (i, 0)),
                  pl.BlockSpec((tm, tn), lambda j, i: (i, j))],
        out_specs=pl.BlockSpec((k1, tn), lambda j, i: (0, j)),
        out_shape=jax.ShapeDtypeStruct((k1, n), out_dtype),
        scratch_shapes=[pltpu.VMEM((k1, tn), F32)])(a, b)


def _rms_fwd(x, g, *, n, n_valid=None, name):
    out_dtype = MXU_DTYPE
    rows = x.shape[0]
    tm = _tile(rows, 512)
    inv_n = 1.0 / (n_valid or n)

    def body(x_ref, g_ref, o_ref):
        xv = x_ref[...]
        r = lax.rsqrt(jnp.sum(xv * xv, axis=-1, keepdims=True) * inv_n + EPS)
        o_ref[...] = (xv * r * g_ref[...]).astype(o_ref.dtype)

    return pl.pallas_call(
        body, name=name, grid=(rows // tm,),
        in_specs=[pl.BlockSpec((tm, n), lambda i: (i, 0)), pl.BlockSpec((1, n), lambda i: (0, 0))],
        out_specs=pl.BlockSpec((tm, n), lambda i: (i, 0)),
        out_shape=jax.ShapeDtypeStruct((rows, n), out_dtype))(x, g)


def _rms_bwd_math(xv, gv, dyv, inv_n):
    r = lax.rsqrt(jnp.sum(xv * xv, axis=-1, keepdims=True) * inv_n + EPS)
    xh = xv * r
    gy = dyv * gv
    dx = r * (gy - xh * (jnp.sum(gy * xh, axis=-1, keepdims=True) * inv_n))
    return dx, dyv * xh


def _rms_bwd(x, g, dy, *, n, dx_in=None, after=None, name):
    rows = x.shape[0]
    tm = _tile(rows, 512)
    inv_n = 1.0 / n

    def body(*refs):
        if after is not None:
            refs = refs[:-3] + refs[-2:]
        if dx_in is None:
            x_ref, g_ref, dy_ref, dx_ref, dg_ref = refs
        else:
            x_ref, g_ref, dy_ref, din_ref, dx_ref, dg_ref = refs
        dx, dgc = _rms_bwd_math(x_ref[...], g_ref[...], dy_ref[...], inv_n)
        if dx_in is not None:
            dx = din_ref[...] + dx
        dx_ref[...] = dx

        @pl.when(pl.program_id(0) == 0)
        def _():
            dg_ref[...] = jnp.zeros_like(dg_ref)

        dg_ref[...] += _fold8(dgc)

    row_spec = pl.BlockSpec((tm, n), lambda i: (i, 0))
    in_specs = [row_spec, pl.BlockSpec((1, n), lambda i: (0, 0)), row_spec]
    args = [x, g, dy]
    if dx_in is not None:
        in_specs.append(row_spec)
        args.append(dx_in)
    if after is not None:
        in_specs.append(ANY)
        args.append(after)
    return pl.pallas_call(
        body, name=name, grid=(rows // tm,), in_specs=in_specs,
        out_specs=[row_spec, pl.BlockSpec((SUBLANES, n), lambda i: (0, 0))],
        out_shape=[jax.ShapeDtypeStruct((rows, n), F32), jax.ShapeDtypeStruct((SUBLANES, n), F32)])(*args)


def _pool_counts(t0, tm, w):
    t = t0 + lax.broadcasted_iota(jnp.int32, (tm, 1), 0)
    return jnp.minimum(t + 1, w).astype(F32)


def _rms_pool_fwd(x, g, *, name):
    s, d = x.shape
    tm = _tile(s, 512)
    hb = tm // POOL_HALO

    def body(x_ref, halo_ref, g_ref, o_ref):
        i = pl.program_id(0)
        gv = g_ref[...]

        def norm(v):
            return v * lax.rsqrt(jnp.mean(v * v, axis=-1, keepdims=True) + EPS) * gv

        h = norm(x_ref[...])
        halo = norm(halo_ref[...]) * (i > 0).astype(F32)
        hh = jnp.concatenate([halo, h], axis=0)
        rows = tm + POOL_HALO
        for gi, w in enumerate(POOL_WINDOWS):
            cols = slice(gi * GROUP_DIM, (gi + 1) * GROUP_DIM)
            acc = hh[:, cols]
            k = 1
            while k < w:
                acc = acc + pltpu.roll(acc, k, 0)
                k *= 2
            win = acc[POOL_HALO:rows]
            o_ref[:, cols] = (win / _pool_counts(i * tm, tm, w) - h[:, cols]).astype(o_ref.dtype)

    return pl.pallas_call(
        body, name=name, grid=(s // tm,),
        in_specs=[pl.BlockSpec((tm, d), lambda i: (i, 0)),
                  pl.BlockSpec((POOL_HALO, d), lambda i: (jnp.maximum(i * hb - 1, 0), 0)),
                  pl.BlockSpec((1, d), lambda i: (0, 0))],
        out_specs=pl.BlockSpec((tm, d), lambda i: (i, 0)),
        out_shape=jax.ShapeDtypeStruct((s, d), MXU_DTYPE))(x, x, g)


def _rms_pool_bwd(x, g, dd, dx_in, *, name):
    s, d = x.shape
    tm = _tile(s, 512)
    hb = tm // POOL_HALO
    nt = s // tm

    def body(x_ref, g_ref, dd_ref, halo_ref, din_ref, dx_ref, dg_ref):
        i = pl.program_id(0)
        ddv = dd_ref[...]
        halo = halo_ref[...] * (i < nt - 1).astype(F32)
        rows = tm + POOL_HALO
        parts = []
        for gi, w in enumerate(POOL_WINDOWS):
            cols = slice(gi * GROUP_DIM, (gi + 1) * GROUP_DIM)
            acc = jnp.concatenate([ddv[:, cols] / _pool_counts(i * tm, tm, w), halo[:, cols] * (1.0 / w)], axis=0)
            k = 1
            while k < w:
                acc = acc + pltpu.roll(acc, rows - k, 0)
                k *= 2
            parts.append(acc[0:tm] - ddv[:, cols])
        dh = jnp.concatenate(parts, axis=1)
        dx, dgc = _rms_bwd_math(x_ref[...], g_ref[...], dh, 1.0 / d)
        dx_ref[...] = din_ref[...] + dx

        @pl.when(i == 0)
        def _():
            dg_ref[...] = jnp.zeros_like(dg_ref)

        dg_ref[...] += _fold8(dgc)

    row_spec = pl.BlockSpec((tm, d), lambda i: (i, 0))
    return pl.pallas_call(
        body, name=name, grid=(nt,),
        in_specs=[row_spec, pl.BlockSpec((1, d), lambda i: (0, 0)), row_spec,
                  pl.BlockSpec((POOL_HALO, d), lambda i: (jnp.minimum((i + 1) * hb, s // POOL_HALO - 1), 0)),
                  row_spec],
        out_specs=[row_spec, pl.BlockSpec((SUBLANES, d), lambda i: (0, 0))],
        out_shape=[jax.ShapeDtypeStruct((s, d), F32), jax.ShapeDtypeStruct((SUBLANES, d), F32)])(x, g, dd, dd, dx_in)


def _pool_mm_fwd(dpool, w, b, scale, x, *, name):
    s, d = x.shape
    tm = _tile(s, 512)

    def body(d_ref, w_ref, b_ref, s_ref, x_ref, o_ref):
        for gi in range(N_GROUPS):
            cols = slice(gi * GROUP_DIM, (gi + 1) * GROUP_DIM)
            y = _nn(d_ref[:, cols], w_ref[gi]) + b_ref[:, cols]
            o_ref[:, cols] = x_ref[:, cols] + y * s_ref[:, cols]

    row_spec = pl.BlockSpec((tm, d), lambda i: (i, 0))
    vec_spec = pl.BlockSpec((1, d), lambda i: (0, 0))
    return pl.pallas_call(
        body, name=name, grid=(s // tm,),
        in_specs=[row_spec, pl.BlockSpec((N_GROUPS, GROUP_DIM, GROUP_DIM), lambda i: (0, 0, 0)), vec_spec, vec_spec, row_spec],
        out_specs=row_spec, out_shape=jax.ShapeDtypeStruct((s, d), F32))(dpool, w, b, scale, x)


def _pool_mm_bwd(dpool, w, b, scale, dx, *, name):
    s, d = dx.shape
    tm = _tile(s, 512)

    def body(d_ref, w_ref, b_ref, s_ref, dx_ref, dd_ref, dw_ref, db_ref, ds_ref):
        @pl.when(pl.program_id(0) == 0)
        def _():
            dw_ref[...] = jnp.zeros_like(dw_ref)
            db_ref[...] = jnp.zeros_like(db_ref)
            ds_ref[...] = jnp.zeros_like(ds_ref)

        for gi in range(N_GROUPS):
            cols = slice(gi * GROUP_DIM, (gi + 1) * GROUP_DIM)
            dg = d_ref[:, cols]
            y = _nn(dg, w_ref[gi]) + b_ref[:, cols]
            dxg = dx_ref[:, cols]
            dy = dxg * s_ref[:, cols]
            ds_ref[:, cols] += _fold8(dxg * y)
            db_ref[:, cols] += _fold8(dy)
            dw_ref[gi] += _tn(dg, dy)
            dd_ref[:, cols] = _nt(dy, w_ref[gi])

    row_spec = pl.BlockSpec((tm, d), lambda i: (i, 0))
    vec_spec = pl.BlockSpec((1, d), lambda i: (0, 0))
    w_spec = pl.BlockSpec((N_GROUPS, GROUP_DIM, GROUP_DIM), lambda i: (0, 0, 0))
    part_spec = pl.BlockSpec((SUBLANES, d), lambda i: (0, 0))
    return pl.pallas_call(
        body, name=name, grid=(s // tm,),
        in_specs=[row_spec, w_spec, vec_spec, vec_spec, row_spec],
        out_specs=[row_spec, w_spec, part_spec, part_spec],
        out_shape=[jax.ShapeDtypeStruct((s, d), F32), jax.ShapeDtypeStruct((N_GROUPS, GROUP_DIM, GROUP_DIM), F32),
                   jax.ShapeDtypeStruct((SUBLANES, d), F32), jax.ShapeDtypeStruct((SUBLANES, d), F32)])(dpool, w, b, scale, dx)


def _sigmoid(a):
    return 0.5 * jnp.tanh(0.5 * a) + 0.5


def _ffn_up(hf, wg, wu, *, name):
    s, d = hf.shape
    tm = _tile(s, FFN_ROWS)

    def body(h_ref, wg_ref, wu_ref, a_ref, b_ref, u_ref):
        hv = h_ref[...]
        a = _nn(hv, wg_ref[...])
        b = _nn(hv, wu_ref[...])
        a_ref[...] = a.astype(a_ref.dtype)
        b_ref[...] = b.astype(b_ref.dtype)
        u_ref[...] = (a * _sigmoid(a) * b).astype(u_ref.dtype)

    w_spec = pl.BlockSpec((None, d, FF_SHARD), lambda j, i: (j, 0, 0))
    h_spec = pl.BlockSpec((None, tm, FF_SHARD), lambda j, i: (j, i, 0))
    hid = (N_SHARD, s, FF_SHARD)
    return pl.pallas_call(
        body, name=name, grid=(N_SHARD, s // tm),
        in_specs=[pl.BlockSpec((tm, d), lambda j, i: (i, 0)), w_spec, w_spec],
        out_specs=[h_spec, h_spec, h_spec],
        out_shape=[jax.ShapeDtypeStruct(hid, SAVED_DTYPE), jax.ShapeDtypeStruct(hid, SAVED_DTYPE),
                   jax.ShapeDtypeStruct(hid, MXU_DTYPE)])(hf, wg, wu)


def _ffn_down(u, wd, x, *, name):
    s, d = x.shape
    tm = _tile(s, 1024)

    def body(u_ref, w_ref, x_ref, o_ref):
        j = pl.program_id(1)

        @pl.when(j == 0)
        def _():
            o_ref[...] = x_ref[...]

        o_ref[...] += _nn(u_ref[...], w_ref[...])

    return pl.pallas_call(
        body, name=name, grid=(s // tm, N_SHARD),
        in_specs=[pl.BlockSpec((None, tm, FF_SHARD), lambda i, j: (j, i, 0)),
                  pl.BlockSpec((None, FF_SHARD, d), lambda i, j: (j, 0, 0)),
                  pl.BlockSpec((tm, d), lambda i, j: (i, 0))],
        out_specs=pl.BlockSpec((tm, d), lambda i, j: (i, 0)),
        out_shape=jax.ShapeDtypeStruct((s, d), F32))(u, wd, x)


def _ffn_bwd_hidden(dy, wd, a, b, *, name):
    s, d = dy.shape
    tm = _tile(s, FFN_ROWS)

    def body(dy_ref, w_ref, a_ref, b_ref, da_ref, db_ref):
        du = _nt(dy_ref[...], w_ref[...])
        av, bv = a_ref[...].astype(F32), b_ref[...].astype(F32)
        sg = _sigmoid(av)
        da_ref[...] = (du * bv * (sg * (1.0 + av * (1.0 - sg)))).astype(da_ref.dtype)
        db_ref[...] = (du * (av * sg)).astype(db_ref.dtype)

    h_spec = pl.BlockSpec((None, tm, FF_SHARD), lambda i, j: (j, i, 0))
    hid = jax.ShapeDtypeStruct((N_SHARD, s, FF_SHARD), MXU_DTYPE)
    return pl.pallas_call(
        body, name=name, grid=(s // tm, N_SHARD),
        in_specs=[pl.BlockSpec((tm, d), lambda i, j: (i, 0)),
                  pl.BlockSpec((None, FF_SHARD, d), lambda i, j: (j, 0, 0)), h_spec, h_spec],
        out_specs=[h_spec, h_spec], out_shape=[hid, hid])(dy, wd, a, b)


def _ffn_bwd_dwd(u, dy, *, name):
    s, d = dy.shape
    tm = _tile(s, FFN_GRAD_ROWS)
    nm = s // tm

    def body(u_ref, dy_ref, o_ref, acc):
        i = pl.program_id(1)

        @pl.when(i == 0)
        def _():
            acc[...] = jnp.zeros_like(acc)

        acc[...] += _tn(u_ref[...], dy_ref[...])

        @pl.when(i == nm - 1)
        def _():
            o_ref[...] = acc[...].astype(o_ref.dtype)

    return pl.pallas_call(
        body, name=name, grid=(N_SHARD, nm),
        in_specs=[pl.BlockSpec((None, tm, FF_SHARD), lambda j, i: (j, i, 0)), pl.BlockSpec((tm, d), lambda j, i: (i, 0))],
        out_specs=pl.BlockSpec((None, FF_SHARD, d), lambda j, i: (j, 0, 0)),
        out_shape=jax.ShapeDtypeStruct((N_SHARD, FF_SHARD, d), WIRE_DTYPE),
        scratch_shapes=[pltpu.VMEM((FF_SHARD, d), F32)])(u, dy)


def _ffn_bwd_dwgu(hf, da, db, *, name):
    s, d = hf.shape
    tm = _tile(s, FFN_GRAD_ROWS)
    nm = s // tm

    def body(h_ref, da_ref, db_ref, og_ref, ou_ref, accg, accu):
        i = pl.program_id(1)

        @pl.when(i == 0)
        def _():
            accg[...] = jnp.zeros_like(accg)
            accu[...] = jnp.zeros_like(accu)

        hv = h_ref[...]
        accg[...] += _tn(hv, da_ref[...])
        accu[...] += _tn(hv, db_ref[...])

        @pl.when(i == nm - 1)
        def _():
            og_ref[...] = accg[...].astype(og_ref.dtype)
            ou_ref[...] = accu[...].astype(ou_ref.dtype)

    h_spec = pl.BlockSpec((None, tm, FF_SHARD), lambda j, i: (j, i, 0))
    w_spec = pl.BlockSpec((None, d, FF_SHARD), lambda j, i: (j, 0, 0))
    grad = jax.ShapeDtypeStruct((N_SHARD, d, FF_SHARD), WIRE_DTYPE)
    return pl.pallas_call(
        body, name=name, grid=(N_SHARD, nm),
        in_specs=[pl.BlockSpec((tm, d), lambda j, i: (i, 0)), h_spec, h_spec],
        out_specs=[w_spec, w_spec], out_shape=[grad, grad],
        scratch_shapes=[pltpu.VMEM((d, FF_SHARD), F32), pltpu.VMEM((d, FF_SHARD), F32)])(hf, da, db)


def _ffn_bwd_dh(da, db, wg, wu, *, name):
    s = da.shape[1]
    d = wg.shape[1]
    tm = _tile(s, 1024)

    def body(da_ref, db_ref, wg_ref, wu_ref, o_ref):
        j = pl.program_id(1)

        @pl.when(j == 0)
        def _():
            o_ref[...] = jnp.zeros_like(o_ref)

        o_ref[...] += _nt(da_ref[...], wg_ref[...]) + _nt(db_ref[...], wu_ref[...])

    h_spec = pl.BlockSpec((None, tm, FF_SHARD), lambda i, j: (j, i, 0))
    w_spec = pl.BlockSpec((None, d, FF_SHARD), lambda i, j: (j, 0, 0))
    return pl.pallas_call(
        body, name=name, grid=(s // tm, N_SHARD),
        in_specs=[h_spec, h_spec, w_spec, w_spec],
        out_specs=pl.BlockSpec((tm, d), lambda i, j: (i, 0)),
        out_shape=jax.ShapeDtypeStruct((s, d), F32))(da, db, wg, wu)


def _rope_tables(pos, inv, *, name):
    s = pos.shape[0]
    tm = _tile(s, 512)
    half = ROPE // 2

    def body(p_ref, i_ref, c_ref, s_ref):
        ang = p_ref[...] * i_ref[...]
        lane = lax.broadcasted_iota(jnp.int32, ang.shape, 1)
        live = lane < ROPE
        c_ref[...] = jnp.where(live, jnp.cos(ang), 0.0)
        sn = jnp.sin(ang)
        s_ref[...] = jnp.where(live, jnp.where(lane < half, -sn, sn), 0.0)

    out = jax.ShapeDtypeStruct((s, LANES), F32)
    return pl.pallas_call(
        body, name=name, grid=(s // tm,),
        in_specs=[pl.BlockSpec((tm, 1), lambda i: (i, 0)), pl.BlockSpec((1, LANES), lambda i: (0, 0))],
        out_specs=[pl.BlockSpec((tm, LANES), lambda i: (i, 0))] * 2, out_shape=[out, out])(pos, inv)


def _swap_halves(v):
    half = ROPE // 2
    lane = lax.broadcasted_iota(jnp.int32, v.shape, 1)
    return jnp.where(lane < half, pltpu.roll(v, LANES - half, 1), pltpu.roll(v, half, 1))


def _head_norm_rope_fwd(raw, g, cos, sin, *, name):
    s = raw.shape[0]
    tm = _tile(s, 256)
    width = N_HEADS * HEAD_PAD

    def body(x_ref, g_ref, c_ref, s_ref, o_ref):
        cv, sv = c_ref[...], s_ref[...]
        for h in range(N_HEADS):
            lo = h * HEAD_PAD
            xa = x_ref[:, lo:lo + NOPE]
            xb = x_ref[:, lo + NOPE:lo + HEAD_PAD]
            ms = (jnp.sum(xa * xa, axis=-1, keepdims=True) + jnp.sum(xb * xb, axis=-1, keepdims=True)) * (1.0 / QK_DIM)
            r = lax.rsqrt(ms + EPS)
            o_ref[:, lo:lo + NOPE] = (xa * r * g_ref[:, 0:NOPE]).astype(o_ref.dtype)
            yb = xb * r * g_ref[:, NOPE:HEAD_PAD]
            o_ref[:, lo + NOPE:lo + HEAD_PAD] = (yb * cv + _swap_halves(yb) * sv).astype(o_ref.dtype)

    row_spec = pl.BlockSpec((tm, width), lambda i: (i, 0))
    tab_spec = pl.BlockSpec((tm, LANES), lambda i: (i, 0))
    return pl.pallas_call(
        body, name=name, grid=(s // tm,),
        in_specs=[row_spec, pl.BlockSpec((1, HEAD_PAD), lambda i: (0, 0)), tab_spec, tab_spec],
        out_specs=row_spec, out_shape=jax.ShapeDtypeStruct((s, width), MXU_DTYPE))(raw, g, cos, sin)


def _head_norm_rope_bwd(raw, g, cos, sin, dout, *, name):
    s = raw.shape[0]
    tm = _tile(s, 256)
    width = N_HEADS * HEAD_PAD

    def body(x_ref, g_ref, c_ref, s_ref, do_ref, dx_ref, dg_ref):
        @pl.when(pl.program_id(0) == 0)
        def _():
            dg_ref[...] = jnp.zeros_like(dg_ref)

        cv, sv = c_ref[...], s_ref[...]
        ga, gb = g_ref[:, 0:NOPE], g_ref[:, NOPE:HEAD_PAD]
        for h in range(N_HEADS):
            lo = h * HEAD_PAD
            xa = x_ref[:, lo:lo + NOPE]
            xb = x_ref[:, lo + NOPE:lo + HEAD_PAD]
            dya = do_ref[:, lo:lo + NOPE]
            dob = do_ref[:, lo + NOPE:lo + HEAD_PAD]
            dyb = dob * cv + _swap_halves(dob * sv)
            ms = (jnp.sum(xa * xa, axis=-1, keepdims=True) + jnp.sum(xb * xb, axis=-1, keepdims=True)) * (1.0 / QK_DIM)
            r = lax.rsqrt(ms + EPS)
            xha, xhb = xa * r, xb * r
            gya, gyb = dya * ga, dyb * gb
            dot = (jnp.sum(gya * xha, axis=-1, keepdims=True) + jnp.sum(gyb * xhb, axis=-1, keepdims=True)) * (1.0 / QK_DIM)
            dx_ref[:, lo:lo + NOPE] = r * (gya - xha * dot)
            dx_ref[:, lo + NOPE:lo + HEAD_PAD] = r * (gyb - xhb * dot)
            dg_ref[:, 0:NOPE] += _fold8(dya * xha)
            dg_ref[:, NOPE:HEAD_PAD] += _fold8(dyb * xhb)

    row_spec = pl.BlockSpec((tm, width), lambda i: (i, 0))
    tab_spec = pl.BlockSpec((tm, LANES), lambda i: (i, 0))
    return pl.pallas_call(
        body, name=name, grid=(s // tm,),
        in_specs=[row_spec, pl.BlockSpec((1, HEAD_PAD), lambda i: (0, 0)), tab_spec, tab_spec, row_spec],
        out_specs=[row_spec, pl.BlockSpec((SUBLANES, HEAD_PAD), lambda i: (0, 0))],
        out_shape=[jax.ShapeDtypeStruct((s, width), F32), jax.ShapeDtypeStruct((SUBLANES, HEAD_PAD), F32)])(raw, g, cos, sin, dout)


def _k_assemble(kn, ckv, *, name):
    s = kn.shape[0]
    tm = _tile(s, 512)
    width = N_HEADS * HEAD_PAD

    def body(kn_ref, pe_ref, o_ref):
        pe = pe_ref[...]
        for h in range(N_HEADS):
            o_ref[:, h * HEAD_PAD:h * HEAD_PAD + NOPE] = kn_ref[:, h * NOPE:(h + 1) * NOPE]
            o_ref[:, h * HEAD_PAD + NOPE:(h + 1) * HEAD_PAD] = pe

    return pl.pallas_call(
        body, name=name, grid=(s // tm,),
        in_specs=[pl.BlockSpec((tm, N_HEADS * NOPE), lambda i: (i, 0)),
                  pl.BlockSpec((tm, LANES), lambda i: (i, KV_LORA // LANES))],
        out_specs=pl.BlockSpec((tm, width), lambda i: (i, 0)),
        out_shape=jax.ShapeDtypeStruct((s, width), F32))(kn, ckv)


def _k_disassemble(dk_raw, *, name):
    s = dk_raw.shape[0]
    tm = _tile(s, 512)
    width = N_HEADS * HEAD_PAD

    def body(dk_ref, dkn_ref, dpe_ref):
        pe = dk_ref[:, NOPE:HEAD_PAD]
        for h in range(N_HEADS):
            dkn_ref[:, h * NOPE:(h + 1) * NOPE] = dk_ref[:, h * HEAD_PAD:h * HEAD_PAD + NOPE]
            if h:
                pe = pe + dk_ref[:, h * HEAD_PAD + NOPE:(h + 1) * HEAD_PAD]
        dpe_ref[...] = pe

    return pl.pallas_call(
        body, name=name, grid=(s // tm,),
        in_specs=[pl.BlockSpec((tm, width), lambda i: (i, 0))],
        out_specs=[pl.BlockSpec((tm, N_HEADS * NOPE), lambda i: (i, 0)), pl.BlockSpec((tm, LANES), lambda i: (i, 0))],
        out_shape=[jax.ShapeDtypeStruct((s, N_HEADS * NOPE), F32), jax.ShapeDtypeStruct((s, LANES), F32)])(dk_raw)


ATTN_SCALE = 1.0 / math.sqrt(QK_DIM)
MASKED = -1e30


ATTN_TILE = 512
ATTN_HEADS = 8


def _chunk_mask(q0, k0, shape, q_axis):
    qpos = q0 + lax.broadcasted_iota(jnp.int32, shape, q_axis)
    kpos = k0 + lax.broadcasted_iota(jnp.int32, shape, 1 - q_axis)
    return kpos // CHUNK <= qpos // CHUNK


LOG2E = math.log2(math.e)
SCORE_LOG2 = ATTN_SCALE * LOG2E


def _causal_pairs(n, by_key):
    if by_key:
        pairs = [(i, j) for j in range(n) for i in range(j, n)]
    else:
        pairs = [(i, j) for i in range(n) for j in range(i + 1)]
    return jnp.asarray([p[0] for p in pairs], jnp.int32), jnp.asarray([p[1] for p in pairs], jnp.int32)


def _attn_fwd(q, k, vt, *, name):
    s = q.shape[0]
    t = _tile(s, ATTN_TILE)
    n = s // t
    qi_tab, kj_tab = _causal_pairs(n, by_key=False)

    hg = ATTN_HEADS

    def body(qi_ref, kj_ref, q_ref, k_ref, vt_ref, o_ref, lse_ref, m_sc, l_sc, acc):
        pair = pl.program_id(1)
        qi, kj = qi_ref[pair], kj_ref[pair]

        @pl.when(kj == 0)
        def _():
            m_sc[...] = jnp.full_like(m_sc, MASKED)
            l_sc[...] = jnp.zeros_like(l_sc)
            acc[...] = jnp.zeros_like(acc)

        def step(masked):
            for g in range(hg):
                qk, vr = slice(g * HEAD_PAD, (g + 1) * HEAD_PAD), slice(g * V_DIM, (g + 1) * V_DIM)
                st = _nt(k_ref[:, qk], q_ref[:, qk])
                if masked:
                    st = jnp.where(_chunk_mask(qi * t, kj * t, (t, t), 1), st, MASKED)
                m_prev = m_sc[g]
                m_new = jnp.maximum(m_prev, jnp.max(st, axis=0, keepdims=True) * SCORE_LOG2)
                alpha = jnp.exp2(m_prev - m_new)
                pt = jnp.exp2(st * SCORE_LOG2 - m_new)
                l_new = alpha * l_sc[g] + jnp.sum(pt, axis=0, keepdims=True)
                a_new = alpha * acc[vr, :] + _nn(vt_ref[vr, :], pt)
                l_sc[g] = l_new
                acc[vr, :] = a_new
                m_sc[g] = m_new
                if masked:
                    o_ref[vr, :] = a_new / l_new
                    lse_ref[g] = m_new + jnp.log(l_new) * LOG2E

        @pl.when(kj < qi)
        def _():
            step(False)

        @pl.when(kj == qi)
        def _():
            step(True)

    return pl.pallas_call(
        body, name=name,
        grid_spec=pltpu.PrefetchScalarGridSpec(
            num_scalar_prefetch=2, grid=(N_HEADS // hg, int(qi_tab.shape[0])),
            in_specs=[pl.BlockSpec((t, hg * HEAD_PAD), lambda h, p, qi, kj: (qi[p], h)),
                      pl.BlockSpec((t, hg * HEAD_PAD), lambda h, p, qi, kj: (kj[p], h)),
                      pl.BlockSpec((hg * V_DIM, t), lambda h, p, qi, kj: (h, kj[p]))],
            out_specs=[pl.BlockSpec((hg * V_DIM, t), lambda h, p, qi, kj: (h, qi[p])),
                       pl.BlockSpec((hg, 1, t), lambda h, p, qi, kj: (h, 0, qi[p]))],
            scratch_shapes=[pltpu.VMEM((hg, 1, t), F32), pltpu.VMEM((hg, 1, t), F32), pltpu.VMEM((hg * V_DIM, t), F32)]),
        out_shape=[jax.ShapeDtypeStruct((N_HEADS * V_DIM, s), F32), jax.ShapeDtypeStruct((N_HEADS, 1, s), F32)])(qi_tab, kj_tab, q, k, vt)


def _attn_delta(ot, dot, *, name):
    s = ot.shape[1]
    t = _tile(s, 1024)

    def body(o_ref, do_ref, d_ref):
        d_ref[...] = jnp.sum(o_ref[...] * do_ref[...], axis=0, keepdims=True)

    blk = pl.BlockSpec((V_DIM, t), lambda h, i: (h, i))
    return pl.pallas_call(
        body, name=name, grid=(N_HEADS, s // t), in_specs=[blk, blk],
        out_specs=pl.BlockSpec((None, 1, t), lambda h, i: (h, 0, i)),
        out_shape=jax.ShapeDtypeStruct((N_HEADS, 1, s), F32))(ot, dot)


def _attn_bwd_dq(q, k, v, do, lse_col, delta_col, *, name):
    s = q.shape[0]
    t = _tile(s, ATTN_TILE)
    n = s // t
    qi_tab, kj_tab = _causal_pairs(n, by_key=False)

    hg = ATTN_HEADS

    def body(qi_ref, kj_ref, q_ref, k_ref, v_ref, do_ref, lse_ref, dl_ref, dq_ref, acc):
        pair = pl.program_id(1)
        qi, kj = qi_ref[pair], kj_ref[pair]

        @pl.when(kj == 0)
        def _():
            acc[...] = jnp.zeros_like(acc)

        def step(masked):
            for g in range(hg):
                qk, vc = slice(g * HEAD_PAD, (g + 1) * HEAD_PAD), slice(g * V_DIM, (g + 1) * V_DIM)
                kv = k_ref[:, qk]
                sc = _nt(q_ref[:, qk], kv)
                if masked:
                    sc = jnp.where(_chunk_mask(qi * t, kj * t, (t, t), 0), sc, MASKED)
                p = jnp.exp2(sc * SCORE_LOG2 - lse_ref[g])
                dp = _nt(do_ref[:, vc], v_ref[:, vc])
                total = acc[:, qk] + _nn(p * (dp - dl_ref[g]), kv)
                acc[:, qk] = total
                if masked:
                    dq_ref[:, qk] = total * ATTN_SCALE

        @pl.when(kj < qi)
        def _():
            step(False)

        @pl.when(kj == qi)
        def _():
            step(True)

    col = pl.BlockSpec((hg, t, 1), lambda h, p, qi, kj: (h, qi[p], 0))
    return pl.pallas_call(
        body, name=name,
        grid_spec=pltpu.PrefetchScalarGridSpec(
            num_scalar_prefetch=2, grid=(N_HEADS // hg, int(qi_tab.shape[0])),
            in_specs=[pl.BlockSpec((t, hg * HEAD_PAD), lambda h, p, qi, kj: (qi[p], h)),
                      pl.BlockSpec((t, hg * HEAD_PAD), lambda h, p, qi, kj: (kj[p], h)),
                      pl.BlockSpec((t, hg * V_DIM), lambda h, p, qi, kj: (kj[p], h)),
                      pl.BlockSpec((t, hg * V_DIM), lambda h, p, qi, kj: (qi[p], h)), col, col],
            out_specs=pl.BlockSpec((t, hg * HEAD_PAD), lambda h, p, qi, kj: (qi[p], h)),
            scratch_shapes=[pltpu.VMEM((t, hg * HEAD_PAD), F32)]),
        out_shape=jax.ShapeDtypeStruct((s, N_HEADS * HEAD_PAD), F32))(qi_tab, kj_tab, q, k, v, do, lse_col, delta_col)


def _attn_bwd_dkv(q, k, v, do, lse_row, delta_row, dk_in, dv_in, *, name):
    s = q.shape[0]
    t = _tile(s, ATTN_TILE)
    n = s // t
    has_in = dk_in is not None
    hg = ATTN_HEADS
    qi_tab, kj_tab = _causal_pairs(n, by_key=True)

    def body(qi_ref, kj_ref, *refs):
        if has_in:
            q_ref, k_ref, v_ref, do_ref, lse_ref, dl_ref, dki_ref, dvi_ref, dk_ref, dv_ref, acck, accv = refs
        else:
            q_ref, k_ref, v_ref, do_ref, lse_ref, dl_ref, dk_ref, dv_ref, acck, accv = refs
        pair = pl.program_id(1)
        qi, kj = qi_ref[pair], kj_ref[pair]

        def step(masked):
            for g in range(hg):
                qk, vc = slice(g * HEAD_PAD, (g + 1) * HEAD_PAD), slice(g * V_DIM, (g + 1) * V_DIM)
                qv, dov = q_ref[:, qk], do_ref[:, vc]
                st = _nt(k_ref[:, qk], qv)
                if masked:
                    st = jnp.where(_chunk_mask(qi * t, kj * t, (t, t), 1), st, MASKED)
                pt = jnp.exp2(st * SCORE_LOG2 - lse_ref[g])
                accv[:, vc] += _nn(pt, dov)
                dpt = _nt(v_ref[:, vc], dov)
                acck[:, qk] += _nn(pt * (dpt - dl_ref[g]), qv)

        @pl.when(qi == kj)
        def _():
            acck[...] = jnp.zeros_like(acck)
            accv[...] = jnp.zeros_like(accv)
            step(True)

        @pl.when(qi > kj)
        def _():
            step(False)

        @pl.when(qi == n - 1)
        def _():
            dk = acck[...] * ATTN_SCALE
            dv = accv[...]
            if has_in:
                dk = dki_ref[...] + dk
                dv = dvi_ref[...] + dv
            dk_ref[...] = dk
            dv_ref[...] = dv

    row = pl.BlockSpec((hg, 1, t), lambda h, p, qi, kj: (h, 0, qi[p]))
    k_spec = pl.BlockSpec((t, hg * HEAD_PAD), lambda h, p, qi, kj: (kj[p], h))
    v_spec = pl.BlockSpec((t, hg * V_DIM), lambda h, p, qi, kj: (kj[p], h))
    in_specs = [pl.BlockSpec((t, hg * HEAD_PAD), lambda h, p, qi, kj: (qi[p], h)), k_spec, v_spec,
                pl.BlockSpec((t, hg * V_DIM), lambda h, p, qi, kj: (qi[p], h)), row, row]
    args = [q, k, v, do, lse_row, delta_row]
    if has_in:
        in_specs += [k_spec, v_spec]
        args += [dk_in, dv_in]
    return pl.pallas_call(
        body, name=name,
        grid_spec=pltpu.PrefetchScalarGridSpec(
            num_scalar_prefetch=2, grid=(N_HEADS // hg, int(qi_tab.shape[0])), in_specs=in_specs, out_specs=[k_spec, v_spec],
            scratch_shapes=[pltpu.VMEM((t, hg * HEAD_PAD), F32), pltpu.VMEM((t, hg * V_DIM), F32)]),
        out_shape=[jax.ShapeDtypeStruct((s, N_HEADS * HEAD_PAD), F32), jax.ShapeDtypeStruct((s, N_HEADS * V_DIM), F32)])(qi_tab, kj_tab, *args)


def _loss_head(y, target, *, name):
    s, d = y.shape
    tm = _tile(s, 512)

    def body(y_ref, t_ref, dy_ref, l_ref):
        @pl.when(pl.program_id(0) == 0)
        def _():
            l_ref[...] = jnp.zeros_like(l_ref)

        err = y_ref[...] - t_ref[...]
        dy_ref[...] = err * (1.0 / d)
        sq = _fold8(err * err)
        part = sq[:, 0:LANES]
        for cb in range(1, d // LANES):
            part = part + sq[:, cb * LANES:(cb + 1) * LANES]
        l_ref[...] += part * (0.5 / d)

    row_spec = pl.BlockSpec((tm, d), lambda i: (i, 0))
    return pl.pallas_call(
        body, name=name, grid=(s // tm,), in_specs=[row_spec, row_spec],
        out_specs=[row_spec, pl.BlockSpec((SUBLANES, LANES), lambda i: (0, 0))],
        out_shape=[jax.ShapeDtypeStruct((s, d), F32), jax.ShapeDtypeStruct((SUBLANES, LANES), F32)])(y, target)


ADAMW_ROWS = 512


def _adamw_math(w, m, v, g):
    mn = ADAM_B1 * m + (1.0 - ADAM_B1) * g
    vn = ADAM_B2 * v + (1.0 - ADAM_B2) * (g * g)
    m_hat = mn / (1.0 - ADAM_B1 ** ADAM_STEP)
    v_hat = vn / (1.0 - ADAM_B2 ** ADAM_STEP)
    return -ADAM_LR * (m_hat / (jnp.sqrt(v_hat) + ADAM_EPS) + ADAM_WD * w), mn, vn


def _adamw_vectors(ws, ms, vs, gs, *, name):
    n = len(ws)

    def body(*refs):
        ins, outs = refs[:4 * n], refs[4 * n:]
        for a in range(n):
            g = ins[3 * n + a][...]
            outs[a][...] = g
            outs[n + a][...], outs[2 * n + a][...], outs[3 * n + a][...] = _adamw_math(ins[a][...], ins[n + a][...], ins[2 * n + a][...], g)

    shapes = [jax.ShapeDtypeStruct(w.shape, F32) for w in ws]
    out = pl.pallas_call(body, name=name, in_specs=[VMEM_SPEC] * (4 * n), out_specs=[VMEM_SPEC] * (4 * n),
                         out_shape=shapes * 4)(*ws, *ms, *vs, *gs)
    return out[:n], out[n:2 * n], out[2 * n:3 * n], out[3 * n:]


def _adamw(w, m, v, g_parts, *, name):
    rows, cols = w.shape
    tm = _tile(rows, ADAMW_ROWS)
    n_parts = len(g_parts)

    def body(*refs):
        w_ref, m_ref, v_ref = refs[:3]
        g_refs = refs[3:3 + n_parts]
        g_out, d_out, m_out, v_out = refs[3 + n_parts:]
        g = g_refs[0][...]
        for r in g_refs[1:]:
            g = g + r[...]
        g_out[...] = g
        d_out[...], m_out[...], v_out[...] = _adamw_math(w_ref[...], m_ref[...], v_ref[...], g)

    spec = pl.BlockSpec((tm, cols), lambda i: (i, 0))
    out = jax.ShapeDtypeStruct((rows, cols), F32)
    return pl.pallas_call(
        body, name=name, grid=(rows // tm,), in_specs=[spec] * (3 + n_parts),
        out_specs=[spec] * 4, out_shape=[out] * 4)(w, m, v, *g_parts)


def _sum_slots(parts, *, name):
    _, rows, cols = parts.shape
    tm = _tile(rows, 512)

    def body(p_ref, o_ref):
        acc = p_ref[0].astype(F32)
        for k in range(1, N_SHARD):
            acc = acc + p_ref[k].astype(F32)
        o_ref[...] = acc

    return pl.pallas_call(
        body, name=name, grid=(rows // tm,),
        in_specs=[pl.BlockSpec((N_SHARD, tm, cols), lambda i: (0, i, 0))],
        out_specs=pl.BlockSpec((tm, cols), lambda i: (i, 0)),
        out_shape=jax.ShapeDtypeStruct((rows, cols), F32))(parts)


def _mesh_pos():
    return lax.axis_index("x"), lax.axis_index("y"), lax.axis_index("c")


CHIP_FLIPS = ((1, 0), (0, 1), (1, 1))


class Exchange(NamedTuple):
    kind: str
    srcs: tuple
    lands: tuple
    layer: Any = None


HBM_SPEC = pl.BlockSpec(memory_space=pltpu.HBM)
SEM_SPEC = pl.BlockSpec(memory_space=pltpu.SEMAPHORE)
DATAFLOW = pltpu.SideEffectType.DATAFLOW_SIDE_EFFECTING


def _exchange_copies(ex, src_refs, land_refs, send_sems, recv_sems):
    x, y, c = _mesh_pos()
    mine = 2 * x + y

    def slot(ref, chip):
        return ref.at[chip] if ex.layer is None else ref.at[chip, ex.layer]

    pairs = []
    for a, (src, land) in enumerate(zip(src_refs, land_refs)):
        for k, (fx, fy) in enumerate(CHIP_FLIPS):
            px, py = x ^ fx, y ^ fy
            peer = 2 * px + py
            src_part = src if ex.kind == "gather" else src.at[peer]
            pair = a * len(CHIP_FLIPS) + k
            common = dict(src_ref=src_part, send_sem=send_sems.at[pair], recv_sem=recv_sems.at[pair],
                          device_id=(px, py, c), device_id_type=MESH)
            pairs.append((pltpu.make_async_remote_copy(dst_ref=slot(land, mine), **common),
                          pltpu.make_async_remote_copy(dst_ref=slot(land, peer), **common)))
    return pairs


def _exchange_start(exchanges, *, name):
    srcs = [s for ex in exchanges for s in ex.srcs]
    lands = [b for ex in exchanges for b in ex.lands]
    n_arr, n_ex = len(srcs) + len(lands), len(exchanges)

    def body(*refs):
        src_refs, land_refs = refs[:len(srcs)], refs[len(srcs):n_arr]
        sems, token = refs[n_arr:n_arr + 2 * n_ex], refs[-1]
        at = 0
        for e, ex in enumerate(exchanges):
            n = len(ex.srcs)
            for send, _ in _exchange_copies(ex, src_refs[at:at + n], land_refs[at:at + n], sems[2 * e], sems[2 * e + 1]):
                send.start()
            at += n
        token[...] = jnp.zeros_like(token)

    sem_shapes = [pltpu.SemaphoreType.DMA((len(ex.srcs) * len(CHIP_FLIPS),)) for ex in exchanges for _ in range(2)]
    out = pl.pallas_call(
        body, name=name,
        out_shape=sem_shapes + [pltpu.HBM(a.shape, a.dtype) for a in srcs + lands] + [jax.ShapeDtypeStruct((SUBLANES, LANES), F32)],
        in_specs=[HBM_SPEC] * n_arr, out_specs=[SEM_SPEC] * (2 * n_ex) + [HBM_SPEC] * n_arr + [VMEM_SPEC],
        input_output_aliases={i: 2 * n_ex + i for i in range(n_arr)},
        compiler_params=pltpu.CompilerParams(has_side_effects=DATAFLOW),
    )(*[pltpu.with_memory_space_constraint(a, pltpu.HBM) for a in srcs + lands])
    sems, thru = out[:2 * n_ex], out[2 * n_ex:-1]
    pending, at = [], 0
    for e, ex in enumerate(exchanges):
        n = len(ex.srcs)
        pending.append((ex._replace(srcs=tuple(thru[at:at + n]), lands=tuple(thru[len(srcs) + at:len(srcs) + at + n])),
                        sems[2 * e], sems[2 * e + 1]))
        at += n
    return pending, out[-1]


def _exchange_wait(pending, after, *, name):
    ex, send_sems, recv_sems = pending
    n = len(ex.srcs)

    def body(*refs):
        src_refs, land_refs = refs[:n], refs[n:2 * n]
        for send, arrive in _exchange_copies(ex, src_refs, land_refs, refs[2 * n], refs[2 * n + 1]):
            send.wait_send()
            arrive.wait_recv()

    arrays = list(ex.srcs) + list(ex.lands)
    out = pl.pallas_call(
        body, name=name, out_shape=[pltpu.HBM(a.shape, a.dtype) for a in arrays],
        in_specs=[HBM_SPEC] * (2 * n) + [SEM_SPEC, SEM_SPEC, ANY], out_specs=[HBM_SPEC] * (2 * n),
        input_output_aliases={i: i for i in range(2 * n)},
        compiler_params=pltpu.CompilerParams(has_side_effects=DATAFLOW),
    )(*arrays, send_sems, recv_sems, after)
    return out[n:]


def _swap_with_sibling(arrays, *, name):
    n = len(arrays)

    def body(*refs):
        ins, outs = refs[:n], refs[n:2 * n]
        send_sems, recv_sems = refs[2 * n:]
        x, y, c = _mesh_pos()
        copies = []
        for a in range(n):
            cp = pltpu.make_async_remote_copy(
                src_ref=ins[a], dst_ref=outs[a], send_sem=send_sems.at[a], recv_sem=recv_sems.at[a],
                device_id=(x, y, 1 - c), device_id_type=MESH)
            cp.start()
            copies.append(cp)
        for cp in copies:
            cp.wait()

    return pl.pallas_call(
        body, name=name, in_specs=[ANY] * n, out_specs=[ANY] * n,
        out_shape=[jax.ShapeDtypeStruct(a.shape, a.dtype) for a in arrays],
        scratch_shapes=[pltpu.SemaphoreType.DMA((n,)), pltpu.SemaphoreType.DMA((n,))])(*arrays)


N_DEV = 8


def _all_reduce_small(vec, *, name):
    rows = vec.shape[0]

    def body(v_ref, o_ref, land, send_sems, recv_sems):
        x, y, c = _mesh_pos()
        me = 4 * x + 2 * y + c
        land[me] = v_ref[...]
        copies = []
        for k in range(1, N_DEV):
            fx, fy, fc = (k >> 2) & 1, (k >> 1) & 1, k & 1
            px, py, pc = x ^ fx, y ^ fy, c ^ fc
            send = pltpu.make_async_remote_copy(
                src_ref=v_ref, dst_ref=land.at[me], send_sem=send_sems.at[k - 1], recv_sem=recv_sems.at[k - 1],
                device_id=(px, py, pc), device_id_type=MESH)
            send.start()
            arrive = pltpu.make_async_remote_copy(
                src_ref=v_ref, dst_ref=land.at[4 * px + 2 * py + pc], send_sem=send_sems.at[k - 1], recv_sem=recv_sems.at[k - 1],
                device_id=(px, py, pc), device_id_type=MESH)
            copies.append((send, arrive))
        for send, arrive in copies:
            send.wait_send()
            arrive.wait_recv()
        acc = land[0]
        for k in range(1, N_DEV):
            acc = acc + land[k]
        o_ref[...] = acc

    return pl.pallas_call(
        body, name=name, in_specs=[VMEM_SPEC], out_specs=VMEM_SPEC,
        out_shape=jax.ShapeDtypeStruct(vec.shape, F32),
        scratch_shapes=[pltpu.VMEM((N_DEV, rows, LANES), F32), pltpu.SemaphoreType.DMA((N_DEV - 1,)),
                        pltpu.SemaphoreType.DMA((N_DEV - 1,))])(vec)


PACK_UNIT = SUBLANES * LANES * 2


def _padded(n):
    return -(-n // PACK_UNIT) * PACK_UNIT


def _pack(arrays, dtype, lead=0):
    parts = []
    for a in arrays:
        lead_shape = a.shape[:lead]
        flat = a.astype(dtype).reshape(lead_shape + (-1,))
        n = flat.shape[-1]
        flat = jnp.pad(flat, [(0, 0)] * lead + [(0, _padded(n) - n)])
        parts.append(flat.reshape(lead_shape + (-1, LANES)))
    return jnp.concatenate(parts, axis=lead)


def _unpack(buf, shapes, lead=0):
    out, row = [], 0
    for shp in shapes:
        n = math.prod(shp)
        rows = _padded(n) // LANES
        part = lax.slice_in_dim(buf, row, row + rows, axis=lead)
        lead_shape = part.shape[:lead]
        part = part.reshape(lead_shape + (-1,))
        part = lax.slice_in_dim(part, 0, n, axis=lead)
        out.append(part.reshape(lead_shape + tuple(shp)))
        row += rows
    return out


def kernel(x, positions, ln_mix_a, w_pool, b_pool, pool_scale, ln_ffn, w_gate, w_up, w_down, ln_kv, w_dkv, g_kv_latent, w_uk, w_uv, g_k, ln_mix_b, w_dq, g_q_latent, w_uq, g_q, w_o, loss_target, m_ln_mix_a, m_w_pool, m_b_pool, m_pool_scale, m_ln_ffn, m_w_gate, m_w_up, m_w_down, m_ln_kv, m_w_dkv, m_g_kv_latent, m_w_uk, m_w_uv, m_g_k, m_ln_mix_b, m_w_dq, m_g_q_latent, m_w_uq, m_g_q, m_w_o, v_ln_mix_a, v_w_pool, v_b_pool, v_pool_scale, v_ln_ffn, v_w_gate, v_w_up, v_w_down, v_ln_kv, v_w_dkv, v_g_kv_latent, v_w_uk, v_w_uv, v_g_k, v_ln_mix_b, v_w_dq, v_g_q_latent, v_w_uq, v_g_q, v_w_o):
    weights = dict(ln_mix_a=ln_mix_a, w_pool=w_pool, b_pool=b_pool, pool_scale=pool_scale, ln_ffn=ln_ffn, w_gate=w_gate,
                   w_up=w_up, w_down=w_down, ln_kv=ln_kv, w_dkv=w_dkv, g_kv_latent=g_kv_latent, w_uk=w_uk, w_uv=w_uv, g_k=g_k,
                   ln_mix_b=ln_mix_b, w_dq=w_dq, g_q_latent=g_q_latent, w_uq=w_uq, g_q=g_q, w_o=w_o)
    mom_m = dict(ln_mix_a=m_ln_mix_a, w_pool=m_w_pool, b_pool=m_b_pool, pool_scale=m_pool_scale, ln_ffn=m_ln_ffn,
                 w_gate=m_w_gate, w_up=m_w_up, w_down=m_w_down, ln_kv=m_ln_kv, w_dkv=m_w_dkv, g_kv_latent=m_g_kv_latent,
                 w_uk=m_w_uk, w_uv=m_w_uv, g_k=m_g_k, ln_mix_b=m_ln_mix_b, w_dq=m_w_dq, g_q_latent=m_g_q_latent,
                 w_uq=m_w_uq, g_q=m_g_q, w_o=m_w_o)
    mom_v = dict(ln_mix_a=v_ln_mix_a, w_pool=v_w_pool, b_pool=v_b_pool, pool_scale=v_pool_scale, ln_ffn=v_ln_ffn,
                 w_gate=v_w_gate, w_up=v_w_up, w_down=v_w_down, ln_kv=v_ln_kv, w_dkv=v_w_dkv, g_kv_latent=v_g_kv_latent,
                 w_uk=v_w_uk, w_uv=v_w_uv, g_k=v_g_k, ln_mix_b=v_ln_mix_b, w_dq=v_w_dq, g_q_latent=v_g_q_latent,
                 w_uq=v_w_uq, g_q=v_g_q, w_o=v_w_o)
    order = list(weights)
    s = x.shape[1]
    d = D_MODEL
    xs = x.reshape(s, d)
    target = loss_target.reshape(s, d)
    my_chip = 2 * lax.axis_index("x") + lax.axis_index("y")

    mat_names = ("w_pool", "w_dkv", "w_uk", "w_uv", "w_dq", "w_uq", "w_o")
    vec_names = ("ln_mix_a", "b_pool", "pool_scale")
    mat_shapes = [weights[n].shape for n in mat_names]
    vec_shapes = [weights[n].shape for n in vec_names]

    def rows_of(a, lead=0):
        return a.reshape(a.shape[:lead] + (-1, a.shape[-1]))

    mats_local = tuple(rows_of(weights[n].astype(WIRE_DTYPE)) for n in mat_names)
    vecs_local = _pack([weights[n] for n in vec_names], F32)

    def landing(shard):
        return lax.dynamic_update_slice_in_dim(lax.empty((N_SHARD,) + shard.shape, shard.dtype), shard[None], my_chip, axis=0)

    def gather_of(shards):
        return Exchange("gather", tuple(shards), tuple(landing(sh) for sh in shards))

    def ffn_gathers(l):
        return [gather_of(tuple(w[l].astype(WIRE_DTYPE) for w in (w_gate, w_up))), gather_of((w_down[l].astype(WIRE_DTYPE),))]

    gathers = [gather_of(mats_local[:1] + (vecs_local,))]
    ffn_at = {}
    for l in range(DEPTH):
        if l == N_A:
            attn_at = len(gathers)
            gathers.append(gather_of(mats_local[1:]))
        ffn_at[l] = len(gathers)
        gathers += ffn_gathers(l)
    gathering, _ = _exchange_start(gathers, name="gather_start")

    inv = ROPE_THETA ** (-jnp.arange(ROPE // 2, dtype=F32) * 2.0 / ROPE)
    inv_lanes = jnp.concatenate([inv, inv, jnp.zeros((LANES - ROPE,), F32)]).reshape(1, LANES)
    cos_t, sin_t = _rope_tables(positions.reshape(s, 1).astype(F32), inv_lanes, name="rope_tables")

    g_pool, vecs_all = _exchange_wait(gathering[0], cos_t, name="gather_wait_small")
    g_lna, g_bp, g_ps = _unpack(vecs_all, vec_shapes, lead=1)
    wpool_f = g_pool.reshape((N_SHARD,) + mat_shapes[0]).transpose(1, 2, 0, 3, 4).reshape(N_A, N_GROUPS, GROUP_DIM, GROUP_DIM)
    bpool_f = g_bp.transpose(1, 2, 0, 3).reshape(N_A, 1, d)
    pscale_f = g_ps.transpose(1, 0, 2).reshape(N_A, 1, d)
    lna_f = g_lna.transpose(1, 0, 2).reshape(N_A, 1, d)

    def head_gain(g):
        return jnp.pad(g.reshape(1, QK_DIM), ((0, 0), (0, HEAD_PAD - QK_DIM)))

    ffn_w = [None] * DEPTH

    def ffn_fwd(xin, layer):
        hf = _rms_fwd(xin, ln_ffn[layer].reshape(1, d), n=d, name="ffn_norm")
        wg, wu = _exchange_wait(gathering[ffn_at[layer]], hf, name=f"gather_wait_up_{layer}")
        a, b, u = _ffn_up(hf, wg, wu, name="ffn_up")
        (wd,) = _exchange_wait(gathering[ffn_at[layer] + 1], u, name=f"gather_wait_down_{layer}")
        ffn_w[layer] = wg, wu, wd
        return _ffn_down(u, wd, xin, name="ffn_down"), (xin, hf, a, b, u)

    saved_a, saved_b, saved_f = [], [], []
    cur = xs
    for l in range(N_A):
        dpool = _rms_pool_fwd(cur, lna_f[l], name="pool_fwd")
        x1 = _pool_mm_fwd(dpool, wpool_f[l], bpool_f[l], pscale_f[l], cur, name="pool_mm")
        saved_a.append((cur, dpool))
        cur, sf = ffn_fwd(x1, l)
        saved_f.append(sf)

    x_kv = cur
    hk = _rms_fwd(x_kv, ln_kv.reshape(1, d), n=d, name="kv_norm")
    g_dkv, g_uk, g_uv, g_dq, g_uq, g_o = (a.reshape((N_SHARD,) + shp) for a, shp in zip(
        _exchange_wait(gathering[attn_at], hk, name="gather_wait_attn"), mat_shapes[1:]))
    wdkv_f = jnp.pad(g_dkv.reshape(d, KV_LORA + ROPE), ((0, 0), (0, CKV_PAD - KV_LORA - ROPE)))
    wuk_f = g_uk.transpose(1, 0, 2).reshape(KV_LORA, N_HEADS * NOPE)
    wuv_f = g_uv.transpose(1, 0, 2).reshape(KV_LORA, N_HEADS * V_DIM)
    wdq_f = g_dq.transpose(1, 0, 2, 3).reshape(N_B, d, Q_LORA)
    wuq_f = jnp.pad(g_uq.transpose(1, 2, 0, 3).reshape(N_B, Q_LORA, N_HEADS, QK_DIM),
                    ((0, 0), (0, 0), (0, 0), (0, HEAD_PAD - QK_DIM))).reshape(N_B, Q_LORA, N_HEADS * HEAD_PAD)
    wo_f = g_o.transpose(1, 0, 2, 3).reshape(N_B, d, d)
    ckv = _mm(hk, wdkv_f, name="kv_down")
    c_lat = _rms_fwd(ckv, g_kv_latent.reshape(1, KV_LORA), n=KV_LORA, name="kv_latent_norm")
    kn_raw = _mm(c_lat, wuk_f, name="k_up")
    v_all = _mm(c_lat, wuv_f, out_dtype=MXU_DTYPE, name="v_up")
    vt_all = _mm(wuv_f.T, c_lat, tb=True, out_dtype=MXU_DTYPE, name="v_up_t")
    k_raw = _k_assemble(kn_raw, ckv, name="k_assemble")
    gk_pad = head_gain(g_k)
    k_cat = _head_norm_rope_fwd(k_raw, gk_pad, cos_t, sin_t, name="k_norm_rope")

    for j in range(N_B):
        l = N_A + j
        hq = _rms_fwd(cur, ln_mix_b[j].reshape(1, d), n=d, name="q_norm")
        cq_raw = _mm(hq, wdq_f[j], name="q_down")
        cq = _rms_fwd(cq_raw, g_q_latent[j].reshape(1, Q_LORA), n=Q_LORA, name="q_latent_norm")
        q_raw = _mm(cq, wuq_f[j], name="q_up")
        gq_pad = head_gain(g_q[j])
        q_cat = _head_norm_rope_fwd(q_raw, gq_pad, cos_t, sin_t, name="q_norm_rope")
        ot, lse = _attn_fwd(q_cat, k_cat, vt_all, name="attn_fwd")
        x1 = _mm(ot, wo_f[j], ta=True, resid=cur, name="attn_out")
        saved_b.append((cur, hq, cq_raw, cq, q_raw, gq_pad, q_cat, ot, lse))
        cur, sf = ffn_fwd(x1, l)
        saved_f.append(sf)

    dy, loss_part = _loss_head(cur, target, name="loss_head")

    ffn_landed = (lax.empty((N_SHARD, DEPTH, d, FF_SHARD), WIRE_DTYPE), lax.empty((N_SHARD, DEPTH, d, FF_SHARD), WIRE_DTYPE),
                  lax.empty((N_SHARD, DEPTH, FF_SHARD, d), WIRE_DTYPE))
    scattering = None
    grads = {}
    d_ln_ffn = [None] * DEPTH

    def own_part(full):
        return lax.dynamic_index_in_dim(full, my_chip, axis=0, keepdims=True)

    def ffn_bwd(dyv, layer):
        nonlocal ffn_landed, scattering
        xin, hf, a, b, u = saved_f[layer]
        wg, wu, wd = ffn_w[layer]
        da, db = _ffn_bwd_hidden(dyv, wd, a, b, name="ffn_bwd_hidden")
        dwd = _ffn_bwd_dwd(u, dyv, name="ffn_bwd_dwd")
        dwg, dwu = _ffn_bwd_dwgu(hf, da, db, name="ffn_bwd_dwgu")
        if scattering is not None:
            ffn_landed = _exchange_wait(scattering, dwg, name=f"scatter_wait_{layer + 1}")
        ffn_landed = tuple(lax.dynamic_update_slice(buf, own_part(g)[:, None], (my_chip, layer, 0, 0))
                           for buf, g in zip(ffn_landed, (dwg, dwu, dwd)))
        (scattering,), started = _exchange_start([Exchange("scatter", (dwg, dwu, dwd), ffn_landed, layer)],
                                                 name=f"scatter_start_{layer}")
        dhf = _ffn_bwd_dh(da, db, wg, wu, name="ffn_bwd_dh")
        dx, dg = _rms_bwd(xin, ln_ffn[layer].reshape(1, d), dhf, n=d, dx_in=dyv, after=started, name="ffn_norm_bwd")
        d_ln_ffn[layer] = dg.sum(axis=0)
        return dx

    dk_acc = dv_acc = None
    d_ln_mix_b, d_w_dq, d_g_q_latent, d_w_uq, d_g_q, d_w_o = ([None] * N_B for _ in range(6))
    dcur = dy
    for j in reversed(range(N_B)):
        l = N_A + j
        xin, hq, cq_raw, cq, q_raw, gq_pad, q_cat, ot, lse = saved_b[j]
        dx1 = ffn_bwd(dcur, l)
        do = _mm(dx1, wo_f[j], tb=True, out_dtype=MXU_DTYPE, name="attn_out_bwd")
        dot = _mm(wo_f[j], dx1, tb=True, name="attn_out_bwd_t")
        d_w_o[j] = _mm_tn(ot, dx1, at=True, name="attn_out_dw")
        delta = _attn_delta(ot, dot, name="attn_delta")
        lse_col, delta_col = lse.reshape(N_HEADS, s, 1), delta.reshape(N_HEADS, s, 1)
        dq_cat = _attn_bwd_dq(q_cat, k_cat, v_all, do, lse_col, delta_col, name="attn_bwd_dq")
        dk_acc, dv_acc = _attn_bwd_dkv(q_cat, k_cat, v_all, do, lse, delta, dk_acc, dv_acc, name="attn_bwd_dkv")
        dq_raw, dgq = _head_norm_rope_bwd(q_raw, gq_pad, cos_t, sin_t, dq_cat, name="q_norm_rope_bwd")
        d_g_q[j] = dgq.sum(axis=0)[:QK_DIM]
        dcq = _mm(dq_raw, wuq_f[j], tb=True, name="q_up_bwd")
        d_w_uq[j] = _mm_tn(cq, dq_raw, name="q_up_dw").reshape(Q_LORA, N_HEADS, HEAD_PAD)[:, :, :QK_DIM].reshape(Q_LORA, N_HEADS * QK_DIM)
        dcq_raw, dgl = _rms_bwd(cq_raw, g_q_latent[j].reshape(1, Q_LORA), dcq, n=Q_LORA, name="q_latent_norm_bwd")
        d_g_q_latent[j] = dgl.sum(axis=0)
        dhq = _mm(dcq_raw, wdq_f[j], tb=True, name="q_down_bwd")
        d_w_dq[j] = _mm_tn(hq, dcq_raw, name="q_down_dw")
        dcur, dgm = _rms_bwd(xin, ln_mix_b[j].reshape(1, d), dhq, n=d, dx_in=dx1, name="q_norm_bwd")
        d_ln_mix_b[j] = dgm.sum(axis=0)

    dk_raw, dgk = _head_norm_rope_bwd(k_raw, gk_pad, cos_t, sin_t, dk_acc, name="k_norm_rope_bwd")
    grads["g_k"] = dgk.sum(axis=0)[:QK_DIM]
    dc = _mm(dv_acc, wuv_f, tb=True, name="v_up_bwd")
    grads["w_uv"] = _mm_tn(c_lat, dv_acc, name="v_up_dw")
    dkn, dpe = _k_disassemble(dk_raw, name="k_disassemble")
    dc = _mm(dkn, wuk_f, tb=True, resid=dc, name="k_up_bwd")
    grads["w_uk"] = _mm_tn(c_lat, dkn, name="k_up_dw")
    dc_raw, dgl = _rms_bwd(ckv, g_kv_latent.reshape(1, KV_LORA), dc, n=KV_LORA, name="kv_latent_norm_bwd")
    grads["g_kv_latent"] = dgl.sum(axis=0)
    dckv = jnp.concatenate([dc_raw, dpe], axis=1)
    dhk = _mm(dckv, wdkv_f, tb=True, name="kv_down_bwd")
    grads["w_dkv"] = _mm_tn(hk, dckv, name="kv_down_dw")[:, :KV_LORA + ROPE]
    gm = {
        "w_dkv": grads["w_dkv"].reshape(N_SHARD, d // N_SHARD, KV_LORA + ROPE),
        "w_uk": grads["w_uk"].reshape(KV_LORA, N_SHARD, -1).transpose(1, 0, 2),
        "w_uv": grads["w_uv"].reshape(KV_LORA, N_SHARD, -1).transpose(1, 0, 2),
        "w_dq": jnp.stack(d_w_dq).reshape(N_B, N_SHARD, d // N_SHARD, Q_LORA).transpose(1, 0, 2, 3),
        "w_uq": jnp.stack(d_w_uq).reshape(N_B, Q_LORA, N_SHARD, -1).transpose(2, 0, 1, 3),
        "w_o": jnp.stack(d_w_o).reshape(N_B, N_SHARD, d // N_SHARD, d).transpose(1, 0, 2, 3),
    }

    def scatter_of(partials):
        zones = tuple(lax.dynamic_update_slice_in_dim(lax.empty(g.shape, WIRE_DTYPE), own_part(g), my_chip, axis=0) for g in partials)
        return Exchange("scatter", tuple(partials), zones)

    attn_partials = [rows_of(gm[n].astype(WIRE_DTYPE), lead=1) for n in mat_names[1:]]
    (attn_scatter,), started = _exchange_start([scatter_of(attn_partials)], name="scatter_start_attn")
    dcur, dg = _rms_bwd(x_kv, ln_kv.reshape(1, d), dhk, n=d, dx_in=dcur, after=started, name="kv_norm_bwd")
    grads["ln_kv"] = dg.sum(axis=0)

    d_ln_mix_a, d_w_pool, d_b_pool, d_pool_scale = ([None] * N_A for _ in range(4))
    for l in reversed(range(N_A)):
        xin, dpool = saved_a[l]
        dx1 = ffn_bwd(dcur, l)
        dd, dwp, dbp, dsp = _pool_mm_bwd(dpool, wpool_f[l], bpool_f[l], pscale_f[l], dx1, name="pool_mm_bwd")
        d_w_pool[l], d_b_pool[l], d_pool_scale[l] = dwp, dbp.sum(axis=0), dsp.sum(axis=0)
        dcur, dg = _rms_pool_bwd(xin, lna_f[l], dd, dx1, name="pool_bwd")
        d_ln_mix_a[l] = dg.sum(axis=0)
    grad_x = dcur.reshape(1, s, d)

    pool_partial = jnp.stack(d_w_pool).reshape(N_A, N_GROUPS, N_SHARD, GROUP_DIM // N_SHARD, GROUP_DIM).transpose(2, 0, 1, 3, 4)
    (pool_scatter,), _ = _exchange_start([scatter_of([rows_of(pool_partial.astype(WIRE_DTYPE), lead=1)])], name="scatter_start_small")
    attn_landed = _exchange_wait(attn_scatter, dcur, name="scatter_wait_attn")
    ffn_landed = _exchange_wait(scattering, attn_landed[0], name="scatter_wait_0")
    pool_landed = _exchange_wait(pool_scatter, ffn_landed[0], name="scatter_wait_small")
    landed = [ffn_landed[0].reshape(N_SHARD, DEPTH * d, FF_SHARD), ffn_landed[1].reshape(N_SHARD, DEPTH * d, FF_SHARD),
              ffn_landed[2].reshape(N_SHARD, DEPTH * FF_SHARD, d), *pool_landed, *attn_landed]
    chip_sums = [_sum_slots(p, name="sum_chips") for p in landed]
    sib_sums = _swap_with_sibling(chip_sums, name="swap_sibling")

    vec_full = {
        "ln_mix_a": jnp.stack(d_ln_mix_a), "b_pool": jnp.stack(d_b_pool).reshape(N_A, N_GROUPS, GROUP_DIM),
        "pool_scale": jnp.stack(d_pool_scale), "ln_ffn": jnp.stack(d_ln_ffn), "ln_kv": grads["ln_kv"],
        "g_kv_latent": grads["g_kv_latent"], "g_k": grads["g_k"], "ln_mix_b": jnp.stack(d_ln_mix_b),
        "g_q_latent": jnp.stack(d_g_q_latent), "g_q": jnp.stack(d_g_q),
    }
    small_names = list(vec_full)
    small_shapes = [vec_full[n].shape for n in small_names] + [(SUBLANES * LANES,)]
    small = _all_reduce_small(_pack([vec_full[n] for n in small_names] + [loss_part.reshape(-1)], F32), name="all_reduce_small")
    small_sum = _unpack(small, small_shapes)
    loss = jnp.sum(small_sum[-1])
    vec_grad = dict(zip(small_names, small_sum[:-1]))
    vec_grad["ln_mix_a"] = lax.dynamic_slice_in_dim(vec_grad["ln_mix_a"], my_chip * (d // N_SHARD), d // N_SHARD, axis=1)
    vec_grad["pool_scale"] = lax.dynamic_slice_in_dim(vec_grad["pool_scale"], my_chip * (d // N_SHARD), d // N_SHARD, axis=1)
    vec_grad["b_pool"] = lax.dynamic_slice_in_dim(vec_grad["b_pool"], my_chip * (GROUP_DIM // N_SHARD), GROUP_DIM // N_SHARD, axis=2)

    out_g, out_d, out_m, out_v = {}, {}, {}, {}
    for idx, (nm, rows, cols) in enumerate((("w_gate", DEPTH * d, FF_SHARD), ("w_up", DEPTH * d, FF_SHARD), ("w_down", DEPTH * FF_SHARD, d))):
        res = _adamw(weights[nm].reshape(rows, cols), mom_m[nm].reshape(rows, cols), mom_v[nm].reshape(rows, cols),
                     [chip_sums[idx], sib_sums[idx]], name="adamw_ffn")
        shp = weights[nm].shape
        out_g[nm], out_d[nm], out_m[nm], out_v[nm] = (r.reshape(shp) for r in res)

    for idx, nm in enumerate(mat_names, start=3):
        res = _adamw(rows_of(weights[nm]), rows_of(mom_m[nm]), rows_of(mom_v[nm]), [chip_sums[idx], sib_sums[idx]], name="adamw_mat")
        shp = weights[nm].shape
        out_g[nm], out_d[nm], out_m[nm], out_v[nm] = (r.reshape(shp) for r in res)

    def as_rows(a):
        return a.reshape(1, -1) if a.ndim == 1 else rows_of(a)

    res = _adamw_vectors([as_rows(weights[n]) for n in small_names], [as_rows(mom_m[n]) for n in small_names],
                         [as_rows(mom_v[n]) for n in small_names],
                         [as_rows(vec_grad[n].reshape(weights[n].shape)) for n in small_names], name="adamw_vectors")
    for tgt, arrs in zip((out_g, out_d, out_m, out_v), res):
        for n, arr in zip(small_names, arrs):
            tgt[n] = arr.reshape(weights[n].shape)

    return (loss, grad_x, *[out_g[n] for n in order], *[out_d[n] for n in order],
            *[out_m[n] for n in order], *[out_v[n] for n in order])
```

```python
import math
from typing import Any, NamedTuple

import jax
import jax.numpy as jnp
from jax import lax
from jax.experimental import pallas as pl
from jax.experimental.pallas import tpu as pltpu

F32 = jnp.float32
BF16 = jnp.bfloat16
MXU_DTYPE = BF16
WIRE_DTYPE = BF16
SAVED_DTYPE = BF16

D_MODEL = 1024
N_A = 2
N_B = 2
DEPTH = 4
POOL_WINDOWS = (2, 4, 8, 16)
N_GROUPS = 4
GROUP_DIM = 256
POOL_HALO = 16
N_HEADS = 8
NOPE = 128
ROPE = 64
QK_DIM = 192
HEAD_PAD = 256
V_DIM = 128
Q_LORA = 256
KV_LORA = 512
CKV_PAD = 640
ROPE_THETA = 10000.0
CHUNK = 64
EPS = 1e-6
N_SHARD = 4
FF_SHARD = 704
FFN_ROWS = 1024
FFN_GRAD_ROWS = 2048
LANES = 128
SUBLANES = 8
ADAM_LR, ADAM_B1, ADAM_B2, ADAM_EPS, ADAM_WD, ADAM_STEP = 0.001, 0.9, 0.999, 1e-08, 0.01, 10
MESH = pl.DeviceIdType.MESH
ANY = pl.BlockSpec(memory_space=pl.ANY)
VMEM_SPEC = pl.BlockSpec(memory_space=pltpu.VMEM)


def _tile(n, pref):
    if n <= pref:
        return n
    t = pref - pref % SUBLANES
    while n % t:
        t -= SUBLANES
    return t


def _fold8(v):
    r, n = v.shape
    return v.reshape(r // SUBLANES, SUBLANES, n).sum(axis=0)


def _dot(a, b, dims):
    return lax.dot_general(a.astype(MXU_DTYPE), b.astype(MXU_DTYPE), (dims, ((), ())),
                           preferred_element_type=F32)


def _nn(a, b):
    return _dot(a, b, ((1,), (0,)))


def _nt(a, b):
    return _dot(a, b, ((1,), (1,)))


def _tn(a, b):
    return _dot(a, b, ((0,), (0,)))


def _mm(a, b, *, ta=False, tb=False, resid=None, out_dtype=F32, name):
    assert not (ta and tb)
    m, k = (a.shape[1], a.shape[0]) if ta else a.shape
    n = b.shape[0] if tb else b.shape[1]
    tm, tn = _tile(m, 512), _tile(n, 1024)

    def body(*refs):
        if resid is None:
            a_ref, b_ref, o_ref = refs
        else:
            a_ref, b_ref, r_ref, o_ref = refs
        acc = (_tn if ta else _nt if tb else _nn)(a_ref[...], b_ref[...])
        if resid is not None:
            acc = r_ref[...] + acc
        o_ref[...] = acc.astype(o_ref.dtype)

    in_specs = [pl.BlockSpec((k, tm), lambda i, j: (0, i)) if ta else pl.BlockSpec((tm, k), lambda i, j: (i, 0)),
                pl.BlockSpec((tn, k), lambda i, j: (j, 0)) if tb else pl.BlockSpec((k, tn), lambda i, j: (0, j))]
    args = [a, b]
    if resid is not None:
        in_specs.append(pl.BlockSpec((tm, tn), lambda i, j: (i, j)))
        args.append(resid)
    return pl.pallas_call(
        body, name=name, grid=(m // tm, n // tn), in_specs=in_specs,
        out_specs=pl.BlockSpec((tm, tn), lambda i, j: (i, j)),
        out_shape=jax.ShapeDtypeStruct((m, n), out_dtype))(*args)


def _mm_tn(a, b, *, name, at=False, out_dtype=F32):
    m = b.shape[0]
    k1 = a.shape[0] if at else a.shape[1]
    n = b.shape[1]
    tm, tn = _tile(m, 512), _tile(n, 1024)
    nm = m // tm

    def body(a_ref, b_ref, o_ref, acc):
        i = pl.program_id(1)

        @pl.when(i == 0)
        def _():
            acc[...] = jnp.zeros_like(acc)

        acc[...] += (_nn if at else _tn)(a_ref[...], b_ref[...])

        @pl.when(i == nm - 1)
        def _():
            o_ref[...] = acc[...].astype(o_ref.dtype)

    return pl.pallas_call(
        body, name=name, grid=(n // tn, nm),
        in_specs=[pl.BlockSpec((k1, tm), lambda j, i: (0, i)) if at else pl.BlockSpec((tm, k1), lambda j, i: (i, 0)),
                  pl.BlockSpec((tm, tn), lambda j, i: (i, j))],
        out_specs=pl.BlockSpec((k1, tn), lambda j, i: (0, j)),
        out_shape=jax.ShapeDtypeStruct((k1, n), out_dtype),
        scratch_shapes=[pltpu.VMEM((k1, tn), F32)])(a, b)


def _rms_fwd(x, g, *, n, n_valid=None, name):
    out_dtype = MXU_DTYPE
    rows = x.shape[0]
    tm = _tile(rows, 512)
    inv_n = 1.0 / (n_valid or n)

    def body(x_ref, g_ref, o_ref):
        xv = x_ref[...]
        r = lax.rsqrt(jnp.sum(xv * xv, axis=-1, keepdims=True) * inv_n + EPS)
        o_ref[...] = (xv * r * g_ref[...]).astype(o_ref.dtype)

    return pl.pallas_call(
        body, name=name, grid=(rows // tm,),
        in_specs=[pl.BlockSpec((tm, n), lambda i: (i, 0)), pl.BlockSpec((1, n), lambda i: (0, 0))],
        out_specs=pl.BlockSpec((tm, n), lambda i: (i, 0)),
        out_shape=jax.ShapeDtypeStruct((rows, n), out_dtype))(x, g)


def _rms_bwd_math(xv, gv, dyv, inv_n):
    r = lax.rsqrt(jnp.sum(xv * xv, axis=-1, keepdims=True) * inv_n + EPS)
    xh = xv * r
    gy = dyv * gv
    dx = r * (gy - xh * (jnp.sum(gy * xh, axis=-1, keepdims=True) * inv_n))
    return dx, dyv * xh


def _rms_bwd(x, g, dy, *, n, dx_in=None, after=None, name):
    rows = x.shape[0]
    tm = _tile(rows, 512)
    inv_n = 1.0 / n

    def body(*refs):
        if after is not None:
            refs = refs[:-3] + refs[-2:]
        if dx_in is None:
            x_ref, g_ref, dy_ref, dx_ref, dg_ref = refs
        else:
            x_ref, g_ref, dy_ref, din_ref, dx_ref, dg_ref = refs
        dx, dgc = _rms_bwd_math(x_ref[...], g_ref[...], dy_ref[...], inv_n)
        if dx_in is not None:
            dx = din_ref[...] + dx
        dx_ref[...] = dx

        @pl.when(pl.program_id(0) == 0)
        def _():
            dg_ref[...] = jnp.zeros_like(dg_ref)

        dg_ref[...] += _fold8(dgc)

    row_spec = pl.BlockSpec((tm, n), lambda i: (i, 0))
    in_specs = [row_spec, pl.BlockSpec((1, n), lambda i: (0, 0)), row_spec]
    args = [x, g, dy]
    if dx_in is not None:
        in_specs.append(row_spec)
        args.append(dx_in)
    if after is not None:
        in_specs.append(ANY)
        args.append(after)
    return pl.pallas_call(
        body, name=name, grid=(rows // tm,), in_specs=in_specs,
        out_specs=[row_spec, pl.BlockSpec((SUBLANES, n), lambda i: (0, 0))],
        out_shape=[jax.ShapeDtypeStruct((rows, n), F32), jax.ShapeDtypeStruct((SUBLANES, n), F32)])(*args)


def _pool_counts(t0, tm, w):
    t = t0 + lax.broadcasted_iota(jnp.int32, (tm, 1), 0)
    return jnp.minimum(t + 1, w).astype(F32)


def _rms_pool_fwd(x, g, *, name):
    s, d = x.shape
    tm = _tile(s, 512)
    hb = tm // POOL_HALO

    def body(x_ref, halo_ref, g_ref, o_ref):
        i = pl.program_id(0)
        gv = g_ref[...]

        def norm(v):
            return v * lax.rsqrt(jnp.mean(v * v, axis=-1, keepdims=True) + EPS) * gv

        h = norm(x_ref[...])
        halo = norm(halo_ref[...]) * (i > 0).astype(F32)
        hh = jnp.concatenate([halo, h], axis=0)
        rows = tm + POOL_HALO
        for gi, w in enumerate(POOL_WINDOWS):
            cols = slice(gi * GROUP_DIM, (gi + 1) * GROUP_DIM)
            acc = hh[:, cols]
            k = 1
            while k < w:
                acc = acc + pltpu.roll(acc, k, 0)
                k *= 2
            win = acc[POOL_HALO:rows]
            o_ref[:, cols] = (win / _pool_counts(i * tm, tm, w) - h[:, cols]).astype(o_ref.dtype)

    return pl.pallas_call(
        body, name=name, grid=(s // tm,),
        in_specs=[pl.BlockSpec((tm, d), lambda i: (i, 0)),
                  pl.BlockSpec((POOL_HALO, d), lambda i: (jnp.maximum(i * hb - 1, 0), 0)),
                  pl.BlockSpec((1, d), lambda i: (0, 0))],
        out_specs=pl.BlockSpec((tm, d), lambda i: (i, 0)),
        out_shape=jax.ShapeDtypeStruct((s, d), MXU_DTYPE))(x, x, g)


def _rms_pool_bwd(x, g, dd, dx_in, *, name):
    s, d = x.shape
    tm = _tile(s, 512)
    hb = tm // POOL_HALO
    nt = s // tm

    def body(x_ref, g_ref, dd_ref, halo_ref, din_ref, dx_ref, dg_ref):
        i = pl.program_id(0)
        ddv = dd_ref[...]
        halo = halo_ref[...] * (i < nt - 1).astype(F32)
        rows = tm + POOL_HALO
        parts = []
        for gi, w in enumerate(POOL_WINDOWS):
            cols = slice(gi * GROUP_DIM, (gi + 1) * GROUP_DIM)
            acc = jnp.concatenate([ddv[:, cols] / _pool_counts(i * tm, tm, w), halo[:, cols] * (1.0 / w)], axis=0)
            k = 1
            while k < w:
                acc = acc + pltpu.roll(acc, rows - k, 0)
                k *= 2
            parts.append(acc[0:tm] - ddv[:, cols])
        dh = jnp.concatenate(parts, axis=1)
        dx, dgc = _rms_bwd_math(x_ref[...], g_ref[...], dh, 1.0 / d)
        dx_ref[...] = din_ref[...] + dx

        @pl.when(i == 0)
        def _():
            dg_ref[...] = jnp.zeros_like(dg_ref)

        dg_ref[...] += _fold8(dgc)

    row_spec = pl.BlockSpec((tm, d), lambda i: (i, 0))
    return pl.pallas_call(
        body, name=name, grid=(nt,),
        in_specs=[row_spec, pl.BlockSpec((1, d), lambda i: (0, 0)), row_spec,
                  pl.BlockSpec((POOL_HALO, d), lambda i: (jnp.minimum((i + 1) * hb, s // POOL_HALO - 1), 0)),
                  row_spec],
        out_specs=[row_spec, pl.BlockSpec((SUBLANES, d), lambda i: (0, 0))],
        out_shape=[jax.ShapeDtypeStruct((s, d), F32), jax.ShapeDtypeStruct((SUBLANES, d), F32)])(x, g, dd, dd, dx_in)


def _pool_mm_fwd(dpool, w, b, scale, x, *, name):
    s, d = x.shape
    tm = _tile(s, 512)

    def body(d_ref, w_ref, b_ref, s_ref, x_ref, o_ref):
        for gi in range(N_GROUPS):
            cols = slice(gi * GROUP_DIM, (gi + 1) * GROUP_DIM)
            y = _nn(d_ref[:, cols], w_ref[gi]) + b_ref[:, cols]
            o_ref[:, cols] = x_ref[:, cols] + y * s_ref[:, cols]

    row_spec = pl.BlockSpec((tm, d), lambda i: (i, 0))
    vec_spec = pl.BlockSpec((1, d), lambda i: (0, 0))
    return pl.pallas_call(
        body, name=name, grid=(s // tm,),
        in_specs=[row_spec, pl.BlockSpec((N_GROUPS, GROUP_DIM, GROUP_DIM), lambda i: (0, 0, 0)), vec_spec, vec_spec, row_spec],
        out_specs=row_spec, out_shape=jax.ShapeDtypeStruct((s, d), F32))(dpool, w, b, scale, x)


def _pool_mm_bwd(dpool, w, b, scale, dx, *, name):
    s, d = dx.shape
    tm = _tile(s, 512)

    def body(d_ref, w_ref, b_ref, s_ref, dx_ref, dd_ref, dw_ref, db_ref, ds_ref):
        @pl.when(pl.program_id(0) == 0)
        def _():
            dw_ref[...] = jnp.zeros_like(dw_ref)
            db_ref[...] = jnp.zeros_like(db_ref)
            ds_ref[...] = jnp.zeros_like(ds_ref)

        for gi in range(N_GROUPS):
            cols = slice(gi * GROUP_DIM, (gi + 1) * GROUP_DIM)
            dg = d_ref[:, cols]
            y = _nn(dg, w_ref[gi]) + b_ref[:, cols]
            dxg = dx_ref[:, cols]
            dy = dxg * s_ref[:, cols]
            ds_ref[:, cols] += _fold8(dxg * y)
            db_ref[:, cols] += _fold8(dy)
            dw_ref[gi] += _tn(dg, dy)
            dd_ref[:, cols] = _nt(dy, w_ref[gi])

    row_spec = pl.BlockSpec((tm, d), lambda i: (i, 0))
    vec_spec = pl.BlockSpec((1, d), lambda i: (0, 0))
    w_spec = pl.BlockSpec((N_GROUPS, GROUP_DIM, GROUP_DIM), lambda i: (0, 0, 0))
    part_spec = pl.BlockSpec((SUBLANES, d), lambda i: (0, 0))
    return pl.pallas_call(
        body, name=name, grid=(s // tm,),
        in_specs=[row_spec, w_spec, vec_spec, vec_spec, row_spec],
        out_specs=[row_spec, w_spec, part_spec, part_spec],
        out_shape=[jax.ShapeDtypeStruct((s, d), F32), jax.ShapeDtypeStruct((N_GROUPS, GROUP_DIM, GROUP_DIM), F32),
                   jax.ShapeDtypeStruct((SUBLANES, d), F32), jax.ShapeDtypeStruct((SUBLANES, d), F32)])(dpool, w, b, scale, dx)


def _sigmoid(a):
    return 0.5 * jnp.tanh(0.5 * a) + 0.5


def _ffn_up(hf, wg, wu, *, name):
    s, d = hf.shape
    tm = _tile(s, FFN_ROWS)

    def body(h_ref, wg_ref, wu_ref, a_ref, b_ref, u_ref):
        hv = h_ref[...]
        a = _nn(hv, wg_ref[...])
        b = _nn(hv, wu_ref[...])
        a_ref[...] = a.astype(a_ref.dtype)
        b_ref[...] = b.astype(b_ref.dtype)
        u_ref[...] = (a * _sigmoid(a) * b).astype(u_ref.dtype)

    w_spec = pl.BlockSpec((None, d, FF_SHARD), lambda j, i: (j, 0, 0))
    h_spec = pl.BlockSpec((None, tm, FF_SHARD), lambda j, i: (j, i, 0))
    hid = (N_SHARD, s, FF_SHARD)
    return pl.pallas_call(
        body, name=name, grid=(N_SHARD, s // tm),
        in_specs=[pl.BlockSpec((tm, d), lambda j, i: (i, 0)), w_spec, w_spec],
        out_specs=[h_spec, h_spec, h_spec],
        out_shape=[jax.ShapeDtypeStruct(hid, SAVED_DTYPE), jax.ShapeDtypeStruct(hid, SAVED_DTYPE),
                   jax.ShapeDtypeStruct(hid, MXU_DTYPE)])(hf, wg, wu)


def _ffn_down(u, wd, x, *, name):
    s, d = x.shape
    tm = _tile(s, 1024)

    def body(u_ref, w_ref, x_ref, o_ref):
        j = pl.program_id(1)

        @pl.when(j == 0)
        def _():
            o_ref[...] = x_ref[...]

        o_ref[...] += _nn(u_ref[...], w_ref[...])

    return pl.pallas_call(
        body, name=name, grid=(s // tm, N_SHARD),
        in_specs=[pl.BlockSpec((None, tm, FF_SHARD), lambda i, j: (j, i, 0)),
                  pl.BlockSpec((None, FF_SHARD, d), lambda i, j: (j, 0, 0)),
                  pl.BlockSpec((tm, d), lambda i, j: (i, 0))],
        out_specs=pl.BlockSpec((tm, d), lambda i, j: (i, 0)),
        out_shape=jax.ShapeDtypeStruct((s, d), F32))(u, wd, x)


def _ffn_bwd_hidden(dy, wd, a, b, *, name):
    s, d = dy.shape
    tm = _tile(s, FFN_ROWS)

    def body(dy_ref, w_ref, a_ref, b_ref, da_ref, db_ref):
        du = _nt(dy_ref[...], w_ref[...])
        av, bv = a_ref[...].astype(F32), b_ref[...].astype(F32)
        sg = _sigmoid(av)
        da_ref[...] = (du * bv * (sg * (1.0 + av * (1.0 - sg)))).astype(da_ref.dtype)
        db_ref[...] = (du * (av * sg)).astype(db_ref.dtype)

    h_spec = pl.BlockSpec((None, tm, FF_SHARD), lambda i, j: (j, i, 0))
    hid = jax.ShapeDtypeStruct((N_SHARD, s, FF_SHARD), MXU_DTYPE)
    return pl.pallas_call(
        body, name=name, grid=(s // tm, N_SHARD),
        in_specs=[pl.BlockSpec((tm, d), lambda i, j: (i, 0)),
                  pl.BlockSpec((None, FF_SHARD, d), lambda i, j: (j, 0, 0)), h_spec, h_spec],
        out_specs=[h_spec, h_spec], out_shape=[hid, hid])(dy, wd, a, b)


def _ffn_bwd_dwd(u, dy, *, name):
    s, d = dy.shape
    tm = _tile(s, FFN_GRAD_ROWS)
    nm = s // tm

    def body(u_ref, dy_ref, o_ref, acc):
        i = pl.program_id(1)

        @pl.when(i == 0)
        def _():
            acc[...] = jnp.zeros_like(acc)

        acc[...] += _tn(u_ref[...], dy_ref[...])

        @pl.when(i == nm - 1)
        def _():
            o_ref[...] = acc[...].astype(o_ref.dtype)

    return pl.pallas_call(
        body, name=name, grid=(N_SHARD, nm),
        in_specs=[pl.BlockSpec((None, tm, FF_SHARD), lambda j, i: (j, i, 0)), pl.BlockSpec((tm, d), lambda j, i: (i, 0))],
        out_specs=pl.BlockSpec((None, FF_SHARD, d), lambda j, i: (j, 0, 0)),
        out_shape=jax.ShapeDtypeStruct((N_SHARD, FF_SHARD, d), WIRE_DTYPE),
        scratch_shapes=[pltpu.VMEM((FF_SHARD, d), F32)])(u, dy)


def _ffn_bwd_dwgu(hf, da, db, *, name):
    s, d = hf.shape
    tm = _tile(s, FFN_GRAD_ROWS)
    nm = s // tm

    def body(h_ref, da_ref, db_ref, og_ref, ou_ref, accg, accu):
        i = pl.program_id(1)

        @pl.when(i == 0)
        def _():
            accg[...] = jnp.zeros_like(accg)
            accu[...] = jnp.zeros_like(accu)

        hv = h_ref[...]
        accg[...] += _tn(hv, da_ref[...])
        accu[...] += _tn(hv, db_ref[...])

        @pl.when(i == nm - 1)
        def _():
            og_ref[...] = accg[...].astype(og_ref.dtype)
            ou_ref[...] = accu[...].astype(ou_ref.dtype)

    h_spec = pl.BlockSpec((None, tm, FF_SHARD), lambda j, i: (j, i, 0))
    w_spec = pl.BlockSpec((None, d, FF_SHARD), lambda j, i: (j, 0, 0))
    grad = jax.ShapeDtypeStruct((N_SHARD, d, FF_SHARD), WIRE_DTYPE)
    return pl.pallas_call(
        body, name=name, grid=(N_SHARD, nm),
        in_specs=[pl.BlockSpec((tm, d), lambda j, i: (i, 0)), h_spec, h_spec],
        out_specs=[w_spec, w_spec], out_shape=[grad, grad],
        scratch_shapes=[pltpu.VMEM((d, FF_SHARD), F32), pltpu.VMEM((d, FF_SHARD), F32)])(hf, da, db)


def _ffn_bwd_dh(da, db, wg, wu, *, name):
    s = da.shape[1]
    d = wg.shape[1]
    tm = _tile(s, 1024)

    def body(da_ref, db_ref, wg_ref, wu_ref, o_ref):
        j = pl.program_id(1)

        @pl.when(j == 0)
        def _():
            o_ref[...] = jnp.zeros_like(o_ref)

        o_ref[...] += _nt(da_ref[...], wg_ref[...]) + _nt(db_ref[...], wu_ref[...])

    h_spec = pl.BlockSpec((None, tm, FF_SHARD), lambda i, j: (j, i, 0))
    w_spec = pl.BlockSpec((None, d, FF_SHARD), lambda i, j: (j, 0, 0))
    return pl.pallas_call(
        body, name=name, grid=(s // tm, N_SHARD),
        in_specs=[h_spec, h_spec, w_spec, w_spec],
        out_specs=pl.BlockSpec((tm, d), lambda i, j: (i, 0)),
        out_shape=jax.ShapeDtypeStruct((s, d), F32))(da, db, wg, wu)


def _rope_tables(pos, inv, *, name):
    s = pos.shape[0]
    tm = _tile(s, 512)
    half = ROPE // 2

    def body(p_ref, i_ref, c_ref, s_ref):
        ang = p_ref[...] * i_ref[...]
        lane = lax.broadcasted_iota(jnp.int32, ang.shape, 1)
        live = lane < ROPE
        c_ref[...] = jnp.where(live, jnp.cos(ang), 0.0)
        sn = jnp.sin(ang)
        s_ref[...] = jnp.where(live, jnp.where(lane < half, -sn, sn), 0.0)

    out = jax.ShapeDtypeStruct((s, LANES), F32)
    return pl.pallas_call(
        body, name=name, grid=(s // tm,),
        in_specs=[pl.BlockSpec((tm, 1), lambda i: (i, 0)), pl.BlockSpec((1, LANES), lambda i: (0, 0))],
        out_specs=[pl.BlockSpec((tm, LANES), lambda i: (i, 0))] * 2, out_shape=[out, out])(pos, inv)


def _swap_halves(v):
    half = ROPE // 2
    lane = lax.broadcasted_iota(jnp.int32, v.shape, 1)
    return jnp.where(lane < half, pltpu.roll(v, LANES - half, 1), pltpu.roll(v, half, 1))


def _head_norm_rope_fwd(raw, g, cos, sin, *, name):
    s = raw.shape[0]
    tm = _tile(s, 256)
    width = N_HEADS * HEAD_PAD

    def body(x_ref, g_ref, c_ref, s_ref, o_ref):
        cv, sv = c_ref[...], s_ref[...]
        for h in range(N_HEADS):
            lo = h * HEAD_PAD
            xa = x_ref[:, lo:lo + NOPE]
            xb = x_ref[:, lo + NOPE:lo + HEAD_PAD]
            ms = (jnp.sum(xa * xa, axis=-1, keepdims=True) + jnp.sum(xb * xb, axis=-1, keepdims=True)) * (1.0 / QK_DIM)
            r = lax.rsqrt(ms + EPS)
            o_ref[:, lo:lo + NOPE] = (xa * r * g_ref[:, 0:NOPE]).astype(o_ref.dtype)
            yb = xb * r * g_ref[:, NOPE:HEAD_PAD]
            o_ref[:, lo + NOPE:lo + HEAD_PAD] = (yb * cv + _swap_halves(yb) * sv).astype(o_ref.dtype)

    row_spec = pl.BlockSpec((tm, width), lambda i: (i, 0))
    tab_spec = pl.BlockSpec((tm, LANES), lambda i: (i, 0))
    return pl.pallas_call(
        body, name=name, grid=(s // tm,),
        in_specs=[row_spec, pl.BlockSpec((1, HEAD_PAD), lambda i: (0, 0)), tab_spec, tab_spec],
        out_specs=row_spec, out_shape=jax.ShapeDtypeStruct((s, width), MXU_DTYPE))(raw, g, cos, sin)


def _head_norm_rope_bwd(raw, g, cos, sin, dout, *, name):
    s = raw.shape[0]
    tm = _tile(s, 256)
    width = N_HEADS * HEAD_PAD

    def body(x_ref, g_ref, c_ref, s_ref, do_ref, dx_ref, dg_ref):
        @pl.when(pl.program_id(0) == 0)
        def _():
            dg_ref[...] = jnp.zeros_like(dg_ref)

        cv, sv = c_ref[...], s_ref[...]
        ga, gb = g_ref[:, 0:NOPE], g_ref[:, NOPE:HEAD_PAD]
        for h in range(N_HEADS):
            lo = h * HEAD_PAD
            xa = x_ref[:, lo:lo + NOPE]
            xb = x_ref[:, lo + NOPE:lo + HEAD_PAD]
            dya = do_ref[:, lo:lo + NOPE]
            dob = do_ref[:, lo + NOPE:lo + HEAD_PAD]
            dyb = dob * cv + _swap_halves(dob * sv)
            ms = (jnp.sum(xa * xa, axis=-1, keepdims=True) + jnp.sum(xb * xb, axis=-1, keepdims=True)) * (1.0 / QK_DIM)
            r = lax.rsqrt(ms + EPS)
            xha, xhb = xa * r, xb * r
            gya, gyb = dya * ga, dyb * gb
            dot = (jnp.sum(gya * xha, axis=-1, keepdims=True) + jnp.sum(gyb * xhb, axis=-1, keepdims=True)) * (1.0 / QK_DIM)
            dx_ref[:, lo:lo + NOPE] = r * (gya - xha * dot)
            dx_ref[:, lo + NOPE:lo + HEAD_PAD] = r * (gyb - xhb * dot)
            dg_ref[:, 0:NOPE] += _fold8(dya * xha)
            dg_ref[:, NOPE:HEAD_PAD] += _fold8(dyb * xhb)

    row_spec = pl.BlockSpec((tm, width), lambda i: (i, 0))
    tab_spec = pl.BlockSpec((tm, LANES), lambda i: (i, 0))
    return pl.pallas_call(
        body, name=name, grid=(s // tm,),
        in_specs=[row_spec, pl.BlockSpec((1, HEAD_PAD), lambda i: (0, 0)), tab_spec, tab_spec, row_spec],
        out_specs=[row_spec, pl.BlockSpec((SUBLANES, HEAD_PAD), lambda i: (0, 0))],
        out_shape=[jax.ShapeDtypeStruct((s, width), F32), jax.ShapeDtypeStruct((SUBLANES, HEAD_PAD), F32)])(raw, g, cos, sin, dout)


def _k_assemble(kn, ckv, *, name):
    s = kn.shape[0]
    tm = _tile(s, 512)
    width = N_HEADS * HEAD_PAD

    def body(kn_ref, pe_ref, o_ref):
        pe = pe_ref[...]
        for h in range(N_HEADS):
            o_ref[:, h * HEAD_PAD:h * HEAD_PAD + NOPE] = kn_ref[:, h * NOPE:(h + 1) * NOPE]
            o_ref[:, h * HEAD_PAD + NOPE:(h + 1) * HEAD_PAD] = pe

    return pl.pallas_call(
        body, name=name, grid=(s // tm,),
        in_specs=[pl.BlockSpec((tm, N_HEADS * NOPE), lambda i: (i, 0)),
                  pl.BlockSpec((tm, LANES), lambda i: (i, KV_LORA // LANES))],
        out_specs=pl.BlockSpec((tm, width), lambda i: (i, 0)),
        out_shape=jax.ShapeDtypeStruct((s, width), F32))(kn, ckv)


def _k_disassemble(dk_raw, *, name):
    s = dk_raw.shape[0]
    tm = _tile(s, 512)
    width = N_HEADS * HEAD_PAD

    def body(dk_ref, dkn_ref, dpe_ref):
        pe = dk_ref[:, NOPE:HEAD_PAD]
        for h in range(N_HEADS):
            dkn_ref[:, h * NOPE:(h + 1) * NOPE] = dk_ref[:, h * HEAD_PAD:h * HEAD_PAD + NOPE]
            if h:
                pe = pe + dk_ref[:, h * HEAD_PAD + NOPE:(h + 1) * HEAD_PAD]
        dpe_ref[...] = pe

    return pl.pallas_call(
        body, name=name, grid=(s // tm,),
        in_specs=[pl.BlockSpec((tm, width), lambda i: (i, 0))],
        out_specs=[pl.BlockSpec((tm, N_HEADS * NOPE), lambda i: (i, 0)), pl.BlockSpec((tm, LANES), lambda i: (i, 0))],
        out_shape=[jax.ShapeDtypeStruct((s, N_HEADS * NOPE), F32), jax.ShapeDtypeStruct((s, LANES), F32)])(dk_raw)


ATTN_SCALE = 1.0 / math.sqrt(QK_DIM)
MASKED = -1e30


ATTN_TILE = 512
ATTN_HEADS = 8


def _chunk_mask(q0, k0, shape, q_axis):
    qpos = q0 + lax.broadcasted_iota(jnp.int32, shape, q_axis)
    kpos = k0 + lax.broadcasted_iota(jnp.int32, shape, 1 - q_axis)
    return kpos // CHUNK <= qpos // CHUNK


LOG2E = math.log2(math.e)
SCORE_LOG2 = ATTN_SCALE * LOG2E


def _causal_pairs(n, by_key):
    if by_key:
        pairs = [(i, j) for j in range(n) for i in range(j, n)]
    else:
        pairs = [(i, j) for i in range(n) for j in range(i + 1)]
    return jnp.asarray([p[0] for p in pairs], jnp.int32), jnp.asarray([p[1] for p in pairs], jnp.int32)


def _attn_fwd(q, k, vt, *, name):
    s = q.shape[0]
    t = _tile(s, ATTN_TILE)
    n = s // t
    qi_tab, kj_tab = _causal_pairs(n, by_key=False)

    hg = ATTN_HEADS

    def body(qi_ref, kj_ref, q_ref, k_ref, vt_ref, o_ref, lse_ref, m_sc, l_sc, acc):
        pair = pl.program_id(1)
        qi, kj = qi_ref[pair], kj_ref[pair]

        @pl.when(kj == 0)
        def _():
            m_sc[...] = jnp.full_like(m_sc, MASKED)
            l_sc[...] = jnp.zeros_like(l_sc)
            acc[...] = jnp.zeros_like(acc)

        def step(masked):
            for g in range(hg):
                qk, vr = slice(g * HEAD_PAD, (g + 1) * HEAD_PAD), slice(g * V_DIM, (g + 1) * V_DIM)
                st = _nt(k_ref[:, qk], q_ref[:, qk])
                if masked:
                    st = jnp.where(_chunk_mask(qi * t, kj * t, (t, t), 1), st, MASKED)
                m_prev = m_sc[g]
                m_new = jnp.maximum(m_prev, jnp.max(st, axis=0, keepdims=True) * SCORE_LOG2)
                alpha = jnp.exp2(m_prev - m_new)
                pt = jnp.exp2(st * SCORE_LOG2 - m_new)
                l_new = alpha * l_sc[g] + jnp.sum(pt, axis=0, keepdims=True)
                a_new = alpha * acc[vr, :] + _nn(vt_ref[vr, :], pt)
                l_sc[g] = l_new
                acc[vr, :] = a_new
                m_sc[g] = m_new
                if masked:
                    o_ref[vr, :] = a_new / l_new
                    lse_ref[g] = m_new + jnp.log(l_new) * LOG2E

        @pl.when(kj < qi)
        def _():
            step(False)

        @pl.when(kj == qi)
        def _():
            step(True)

    return pl.pallas_call(
        body, name=name,
        grid_spec=pltpu.PrefetchScalarGridSpec(
            num_scalar_prefetch=2, grid=(N_HEADS // hg, int(qi_tab.shape[0])),
            in_specs=[pl.BlockSpec((t, hg * HEAD_PAD), lambda h, p, qi, kj: (qi[p], h)),
                      pl.BlockSpec((t, hg * HEAD_PAD), lambda h, p, qi, kj: (kj[p], h)),
                      pl.BlockSpec((hg * V_DIM, t), lambda h, p, qi, kj: (h, kj[p]))],
            out_specs=[pl.BlockSpec((hg * V_DIM, t), lambda h, p, qi, kj: (h, qi[p])),
                       pl.BlockSpec((hg, 1, t), lambda h, p, qi, kj: (h, 0, qi[p]))],
            scratch_shapes=[pltpu.VMEM((hg, 1, t), F32), pltpu.VMEM((hg, 1, t), F32), pltpu.VMEM((hg * V_DIM, t), F32)]),
        out_shape=[jax.ShapeDtypeStruct((N_HEADS * V_DIM, s), F32), jax.ShapeDtypeStruct((N_HEADS, 1, s), F32)])(qi_tab, kj_tab, q, k, vt)


def _attn_delta(ot, dot, *, name):
    s = ot.shape[1]
    t = _tile(s, 1024)

    def body(o_ref, do_ref, d_ref):
        d_ref[...] = jnp.sum(o_ref[...] * do_ref[...], axis=0, keepdims=True)

    blk = pl.BlockSpec((V_DIM, t), lambda h, i: (h, i))
    return pl.pallas_call(
        body, name=name, grid=(N_HEADS, s // t), in_specs=[blk, blk],
        out_specs=pl.BlockSpec((None, 1, t), lambda h, i: (h, 0, i)),
        out_shape=jax.ShapeDtypeStruct((N_HEADS, 1, s), F32))(ot, dot)


def _attn_bwd_dq(q, k, v, do, lse_col, delta_col, *, name):
    s = q.shape[0]
    t = _tile(s, ATTN_TILE)
    n = s // t
    qi_tab, kj_tab = _causal_pairs(n, by_key=False)

    hg = ATTN_HEADS

    def body(qi_ref, kj_ref, q_ref, k_ref, v_ref, do_ref, lse_ref, dl_ref, dq_ref, acc):
        pair = pl.program_id(1)
        qi, kj = qi_ref[pair], kj_ref[pair]

        @pl.when(kj == 0)
        def _():
            acc[...] = jnp.zeros_like(acc)

        def step(masked):
            for g in range(hg):
                qk, vc = slice(g * HEAD_PAD, (g + 1) * HEAD_PAD), slice(g * V_DIM, (g + 1) * V_DIM)
                kv = k_ref[:, qk]
                sc = _nt(q_ref[:, qk], kv)
                if masked:
                    sc = jnp.where(_chunk_mask(qi * t, kj * t, (t, t), 0), sc, MASKED)
                p = jnp.exp2(sc * SCORE_LOG2 - lse_ref[g])
                dp = _nt(do_ref[:, vc], v_ref[:, vc])
                total = acc[:, qk] + _nn(p * (dp - dl_ref[g]), kv)
                acc[:, qk] = total
                if masked:
                    dq_ref[:, qk] = total * ATTN_SCALE

        @pl.when(kj < qi)
        def _():
            step(False)

        @pl.when(kj == qi)
        def _():
            step(True)

    col = pl.BlockSpec((hg, t, 1), lambda h, p, qi, kj: (h, qi[p], 0))
    return pl.pallas_call(
        body, name=name,
        grid_spec=pltpu.PrefetchScalarGridSpec(
            num_scalar_prefetch=2, grid=(N_HEADS // hg, int(qi_tab.shape[0])),
            in_specs=[pl.BlockSpec((t, hg * HEAD_PAD), lambda h, p, qi, kj: (qi[p], h)),
                      pl.BlockSpec((t, hg * HEAD_PAD), lambda h, p, qi, kj: (kj[p], h)),
                      pl.BlockSpec((t, hg * V_DIM), lambda h, p, qi, kj: (kj[p], h)),
                      pl.BlockSpec((t, hg * V_DIM), lambda h, p, qi, kj: (qi[p], h)), col, col],
            out_specs=pl.BlockSpec((t, hg * HEAD_PAD), lambda h, p, qi, kj: (qi[p], h)),
            scratch_shapes=[pltpu.VMEM((t, hg * HEAD_PAD), F32)]),
        out_shape=jax.ShapeDtypeStruct((s, N_HEADS * HEAD_PAD), F32))(qi_tab, kj_tab, q, k, v, do, lse_col, delta_col)


def _attn_bwd_dkv(q, k, v, do, lse_row, delta_row, dk_in, dv_in, *, name):
    s = q.shape[0]
    t = _tile(s, ATTN_TILE)
    n = s // t
    has_in = dk_in is not None
    hg = ATTN_HEADS
    qi_tab, kj_tab = _causal_pairs(n, by_key=True)

    def body(qi_ref, kj_ref, *refs):
        if has_in:
            q_ref, k_ref, v_ref, do_ref, lse_ref, dl_ref, dki_ref, dvi_ref, dk_ref, dv_ref, acck, accv = refs
        else:
            q_ref, k_ref, v_ref, do_ref, lse_ref, dl_ref, dk_ref, dv_ref, acck, accv = refs
        pair = pl.program_id(1)
        qi, kj = qi_ref[pair], kj_ref[pair]

        def step(masked):
            for g in range(hg):
                qk, vc = slice(g * HEAD_PAD, (g + 1) * HEAD_PAD), slice(g * V_DIM, (g + 1) * V_DIM)
                qv, dov = q_ref[:, qk], do_ref[:, vc]
                st = _nt(k_ref[:, qk], qv)
                if masked:
                    st = jnp.where(_chunk_mask(qi * t, kj * t, (t, t), 1), st, MASKED)
                pt = jnp.exp2(st * SCORE_LOG2 - lse_ref[g])
                accv[:, vc] += _nn(pt, dov)
                dpt = _nt(v_ref[:, vc], dov)
                acck[:, qk] += _nn(pt * (dpt - dl_ref[g]), qv)

        @pl.when(qi == kj)
        def _():
            acck[...] = jnp.zeros_like(acck)
            accv[...] = jnp.zeros_like(accv)
            step(True)

        @pl.when(qi > kj)
        def _():
            step(False)

        @pl.when(qi == n - 1)
        def _():
            dk = acck[...] * ATTN_SCALE
            dv = accv[...]
            if has_in:
                dk = dki_ref[...] + dk
                dv = dvi_ref[...] + dv
            dk_ref[...] = dk
            dv_ref[...] = dv

    row = pl.BlockSpec((hg, 1, t), lambda h, p, qi, kj: (h, 0, qi[p]))
    k_spec = pl.BlockSpec((t, hg * HEAD_PAD), lambda h, p, qi, kj: (kj[p], h))
    v_spec = pl.BlockSpec((t, hg * V_DIM), lambda h, p, qi, kj: (kj[p], h))
    in_specs = [pl.BlockSpec((t, hg * HEAD_PAD), lambda h, p, qi, kj: (qi[p], h)), k_spec, v_spec,
                pl.BlockSpec((t, hg * V_DIM), lambda h, p, qi, kj: (qi[p], h)), row, row]
    args = [q, k, v, do, lse_row, delta_row]
    if has_in:
        in_specs += [k_spec, v_spec]
        args += [dk_in, dv_in]
    return pl.pallas_call(
        body, name=name,
        grid_spec=pltpu.PrefetchScalarGridSpec(
            num_scalar_prefetch=2, grid=(N_HEADS // hg, int(qi_tab.shape[0])), in_specs=in_specs, out_specs=[k_spec, v_spec],
            scratch_shapes=[pltpu.VMEM((t, hg * HEAD_PAD), F32), pltpu.VMEM((t, hg * V_DIM), F32)]),
        out_shape=[jax.ShapeDtypeStruct((s, N_HEADS * HEAD_PAD), F32), jax.ShapeDtypeStruct((s, N_HEADS * V_DIM), F32)])(qi_tab, kj_tab, *args)


def _loss_head(y, target, *, name):
    s, d = y.shape
    tm = _tile(s, 512)

    def body(y_ref, t_ref, dy_ref, l_ref):
        @pl.when(pl.program_id(0) == 0)
        def _():
            l_ref[...] = jnp.zeros_like(l_ref)

        err = y_ref[...] - t_ref[...]
        dy_ref[...] = err * (1.0 / d)
        sq = _fold8(err * err)
        part = sq[:, 0:LANES]
        for cb in range(1, d // LANES):
            part = part + sq[:, cb * LANES:(cb + 1) * LANES]
        l_ref[...] += part * (0.5 / d)

    row_spec = pl.BlockSpec((tm, d), lambda i: (i, 0))
    return pl.pallas_call(
        body, name=name, grid=(s // tm,), in_specs=[row_spec, row_spec],
        out_specs=[row_spec, pl.BlockSpec((SUBLANES, LANES), lambda i: (0, 0))],
        out_shape=[jax.ShapeDtypeStruct((s, d), F32), jax.ShapeDtypeStruct((SUBLANES, LANES), F32)])(y, target)


ADAMW_ROWS = 512


def _adamw_math(w, m, v, g):
    mn = ADAM_B1 * m + (1.0 - ADAM_B1) * g
    vn = ADAM_B2 * v + (1.0 - ADAM_B2) * (g * g)
    m_hat = mn / (1.0 - ADAM_B1 ** ADAM_STEP)
    v_hat = vn / (1.0 - ADAM_B2 ** ADAM_STEP)
    return -ADAM_LR * (m_hat / (jnp.sqrt(v_hat) + ADAM_EPS) + ADAM_WD * w), mn, vn


def _adamw_vectors(ws, ms, vs, gs, *, name):
    n = len(ws)

    def body(*refs):
        ins, outs = refs[:4 * n], refs[4 * n:]
        for a in range(n):
            g = ins[3 * n + a][...]
            outs[a][...] = g
            outs[n + a][...], outs[2 * n + a][...], outs[3 * n + a][...] = _adamw_math(ins[a][...], ins[n + a][...], ins[2 * n + a][...], g)

    shapes = [jax.ShapeDtypeStruct(w.shape, F32) for w in ws]
    out = pl.pallas_call(body, name=name, in_specs=[VMEM_SPEC] * (4 * n), out_specs=[VMEM_SPEC] * (4 * n),
                         out_shape=shapes * 4)(*ws, *ms, *vs, *gs)
    return out[:n], out[n:2 * n], out[2 * n:3 * n], out[3 * n:]


def _adamw(w, m, v, g_parts, *, name):
    rows, cols = w.shape
    tm = _tile(rows, ADAMW_ROWS)
    n_parts = len(g_parts)

    def body(*refs):
        w_ref, m_ref, v_ref = refs[:3]
        g_refs = refs[3:3 + n_parts]
        g_out, d_out, m_out, v_out = refs[3 + n_parts:]
        g = g_refs[0][...]
        for r in g_refs[1:]:
            g = g + r[...]
        g_out[...] = g
        d_out[...], m_out[...], v_out[...] = _adamw_math(w_ref[...], m_ref[...], v_ref[...], g)

    spec = pl.BlockSpec((tm, cols), lambda i: (i, 0))
    out = jax.ShapeDtypeStruct((rows, cols), F32)
    return pl.pallas_call(
        body, name=name, grid=(rows // tm,), in_specs=[spec] * (3 + n_parts),
        out_specs=[spec] * 4, out_shape=[out] * 4)(w, m, v, *g_parts)


def _sum_slots(parts, *, name):
    _, rows, cols = parts.shape
    tm = _tile(rows, 512)

    def body(p_ref, o_ref):
        acc = p_ref[0].astype(F32)
        for k in range(1, N_SHARD):
            acc = acc + p_ref[k].astype(F32)
        o_ref[...] = acc

    return pl.pallas_call(
        body, name=name, grid=(rows // tm,),
        in_specs=[pl.BlockSpec((N_SHARD, tm, cols), lambda i: (0, i, 0))],
        out_specs=pl.BlockSpec((tm, cols), lambda i: (i, 0)),
        out_shape=jax.ShapeDtypeStruct((rows, cols), F32))(parts)


def _mesh_pos():
    return lax.axis_index("x"), lax.axis_index("y"), lax.axis_index("c")


CHIP_FLIPS = ((1, 0), (0, 1), (1, 1))


class Exchange(NamedTuple):
    kind: str
    srcs: tuple
    lands: tuple
    layer: Any = None


HBM_SPEC = pl.BlockSpec(memory_space=pltpu.HBM)
SEM_SPEC = pl.BlockSpec(memory_space=pltpu.SEMAPHORE)
DATAFLOW = pltpu.SideEffectType.DATAFLOW_SIDE_EFFECTING


def _exchange_copies(ex, src_refs, land_refs, send_sems, recv_sems):
    x, y, c = _mesh_pos()
    mine = 2 * x + y

    def slot(ref, chip):
        return ref.at[chip] if ex.layer is None else ref.at[chip, ex.layer]

    pairs = []
    for a, (src, land) in enumerate(zip(src_refs, land_refs)):
        for k, (fx, fy) in enumerate(CHIP_FLIPS):
            px, py = x ^ fx, y ^ fy
            peer = 2 * px + py
            src_part = src if ex.kind == "gather" else src.at[peer]
            pair = a * len(CHIP_FLIPS) + k
            common = dict(src_ref=src_part, send_sem=send_sems.at[pair], recv_sem=recv_sems.at[pair],
                          device_id=(px, py, c), device_id_type=MESH)
            pairs.append((pltpu.make_async_remote_copy(dst_ref=slot(land, mine), **common),
                          pltpu.make_async_remote_copy(dst_ref=slot(land, peer), **common)))
    return pairs


def _exchange_start(exchanges, *, name):
    srcs = [s for ex in exchanges for s in ex.srcs]
    lands = [b for ex in exchanges for b in ex.lands]
    n_arr, n_ex = len(srcs) + len(lands), len(exchanges)

    def body(*refs):
        src_refs, land_refs = refs[:len(srcs)], refs[len(srcs):n_arr]
        sems, token = refs[n_arr:n_arr + 2 * n_ex], refs[-1]
        at = 0
        for e, ex in enumerate(exchanges):
            n = len(ex.srcs)
            for send, _ in _exchange_copies(ex, src_refs[at:at + n], land_refs[at:at + n], sems[2 * e], sems[2 * e + 1]):
                send.start()
            at += n
        token[...] = jnp.zeros_like(token)

    sem_shapes = [pltpu.SemaphoreType.DMA((len(ex.srcs) * len(CHIP_FLIPS),)) for ex in exchanges for _ in range(2)]
    out = pl.pallas_call(
        body, name=name,
        out_shape=sem_shapes + [pltpu.HBM(a.shape, a.dtype) for a in srcs + lands] + [jax.ShapeDtypeStruct((SUBLANES, LANES), F32)],
        in_specs=[HBM_SPEC] * n_arr, out_specs=[SEM_SPEC] * (2 * n_ex) + [HBM_SPEC] * n_arr + [VMEM_SPEC],
        input_output_aliases={i: 2 * n_ex + i for i in range(n_arr)},
        compiler_params=pltpu.CompilerParams(has_side_effects=DATAFLOW),
    )(*[pltpu.with_memory_space_constraint(a, pltpu.HBM) for a in srcs + lands])
    sems, thru = out[:2 * n_ex], out[2 * n_ex:-1]
    pending, at = [], 0
    for e, ex in enumerate(exchanges):
        n = len(ex.srcs)
        pending.append((ex._replace(srcs=tuple(thru[at:at + n]), lands=tuple(thru[len(srcs) + at:len(srcs) + at + n])),
                        sems[2 * e], sems[2 * e + 1]))
        at += n
    return pending, out[-1]


def _exchange_wait(pending, after, *, name):
    ex, send_sems, recv_sems = pending
    n = len(ex.srcs)
    after = list(after) if isinstance(after, (list, tuple)) else [after]

    def body(*refs):
        src_refs, land_refs = refs[:n], refs[n:2 * n]
        for send, arrive in _exchange_copies(ex, src_refs, land_refs, refs[2 * n], refs[2 * n + 1]):
            send.wait_send()
            arrive.wait_recv()

    arrays = list(ex.srcs) + list(ex.lands)
    out = pl.pallas_call(
        body, name=name, out_shape=[pltpu.HBM(a.shape, a.dtype) for a in arrays],
        in_specs=[HBM_SPEC] * (2 * n) + [SEM_SPEC, SEM_SPEC] + [ANY] * len(after), out_specs=[HBM_SPEC] * (2 * n),
        input_output_aliases={i: i for i in range(2 * n)},
        compiler_params=pltpu.CompilerParams(has_side_effects=DATAFLOW),
    )(*arrays, send_sems, recv_sems, *after)
    return out[n:]


def _swap_with_sibling(arrays, *, name):
    n = len(arrays)

    def body(*refs):
        ins, outs = refs[:n], refs[n:2 * n]
        send_sems, recv_sems = refs[2 * n:]
        x, y, c = _mesh_pos()
        copies = []
        for a in range(n):
            cp = pltpu.make_async_remote_copy(
                src_ref=ins[a], dst_ref=outs[a], send_sem=send_sems.at[a], recv_sem=recv_sems.at[a],
                device_id=(x, y, 1 - c), device_id_type=MESH)
            cp.start()
            copies.append(cp)
        for cp in copies:
            cp.wait()

    return pl.pallas_call(
        body, name=name, in_specs=[ANY] * n, out_specs=[ANY] * n,
        out_shape=[jax.ShapeDtypeStruct(a.shape, a.dtype) for a in arrays],
        scratch_shapes=[pltpu.SemaphoreType.DMA((n,)), pltpu.SemaphoreType.DMA((n,))])(*arrays)


N_DEV = 8


def _all_reduce_small(vec, *, name):
    rows = vec.shape[0]

    def body(v_ref, o_ref, land, send_sems, recv_sems):
        x, y, c = _mesh_pos()
        me = 4 * x + 2 * y + c
        land[me] = v_ref[...]
        copies = []
        for k in range(1, N_DEV):
            fx, fy, fc = (k >> 2) & 1, (k >> 1) & 1, k & 1
            px, py, pc = x ^ fx, y ^ fy, c ^ fc
            send = pltpu.make_async_remote_copy(
                src_ref=v_ref, dst_ref=land.at[me], send_sem=send_sems.at[k - 1], recv_sem=recv_sems.at[k - 1],
                device_id=(px, py, pc), device_id_type=MESH)
            send.start()
            arrive = pltpu.make_async_remote_copy(
                src_ref=v_ref, dst_ref=land.at[4 * px + 2 * py + pc], send_sem=send_sems.at[k - 1], recv_sem=recv_sems.at[k - 1],
                device_id=(px, py, pc), device_id_type=MESH)
            copies.append((send, arrive))
        for send, arrive in copies:
            send.wait_send()
            arrive.wait_recv()
        acc = land[0]
        for k in range(1, N_DEV):
            acc = acc + land[k]
        o_ref[...] = acc

    return pl.pallas_call(
        body, name=name, in_specs=[VMEM_SPEC], out_specs=VMEM_SPEC,
        out_shape=jax.ShapeDtypeStruct(vec.shape, F32),
        scratch_shapes=[pltpu.VMEM((N_DEV, rows, LANES), F32), pltpu.SemaphoreType.DMA((N_DEV - 1,)),
                        pltpu.SemaphoreType.DMA((N_DEV - 1,))])(vec)


PACK_UNIT = SUBLANES * LANES * 2


def _padded(n):
    return -(-n // PACK_UNIT) * PACK_UNIT


def _pack(arrays, dtype, lead=0):
    parts = []
    for a in arrays:
        lead_shape = a.shape[:lead]
        flat = a.astype(dtype).reshape(lead_shape + (-1,))
        n = flat.shape[-1]
        flat = jnp.pad(flat, [(0, 0)] * lead + [(0, _padded(n) - n)])
        parts.append(flat.reshape(lead_shape + (-1, LANES)))
    return jnp.concatenate(parts, axis=lead)


def _unpack(buf, shapes, lead=0):
    out, row = [], 0
    for shp in shapes:
        n = math.prod(shp)
        rows = _padded(n) // LANES
        part = lax.slice_in_dim(buf, row, row + rows, axis=lead)
        lead_shape = part.shape[:lead]
        part = part.reshape(lead_shape + (-1,))
        part = lax.slice_in_dim(part, 0, n, axis=lead)
        out.append(part.reshape(lead_shape + tuple(shp)))
        row += rows
    return out


def kernel(x, positions, ln_mix_a, w_pool, b_pool, pool_scale, ln_ffn, w_gate, w_up, w_down, ln_kv, w_dkv, g_kv_latent, w_uk, w_uv, g_k, ln_mix_b, w_dq, g_q_latent, w_uq, g_q, w_o, loss_target, m_ln_mix_a, m_w_pool, m_b_pool, m_pool_scale, m_ln_ffn, m_w_gate, m_w_up, m_w_down, m_ln_kv, m_w_dkv, m_g_kv_latent, m_w_uk, m_w_uv, m_g_k, m_ln_mix_b, m_w_dq, m_g_q_latent, m_w_uq, m_g_q, m_w_o, v_ln_mix_a, v_w_pool, v_b_pool, v_pool_scale, v_ln_ffn, v_w_gate, v_w_up, v_w_down, v_ln_kv, v_w_dkv, v_g_kv_latent, v_w_uk, v_w_uv, v_g_k, v_ln_mix_b, v_w_dq, v_g_q_latent, v_w_uq, v_g_q, v_w_o):
    weights = dict(ln_mix_a=ln_mix_a, w_pool=w_pool, b_pool=b_pool, pool_scale=pool_scale, ln_ffn=ln_ffn, w_gate=w_gate,
                   w_up=w_up, w_down=w_down, ln_kv=ln_kv, w_dkv=w_dkv, g_kv_latent=g_kv_latent, w_uk=w_uk, w_uv=w_uv, g_k=g_k,
                   ln_mix_b=ln_mix_b, w_dq=w_dq, g_q_latent=g_q_latent, w_uq=w_uq, g_q=g_q, w_o=w_o)
    mom_m = dict(ln_mix_a=m_ln_mix_a, w_pool=m_w_pool, b_pool=m_b_pool, pool_scale=m_pool_scale, ln_ffn=m_ln_ffn,
                 w_gate=m_w_gate, w_up=m_w_up, w_down=m_w_down, ln_kv=m_ln_kv, w_dkv=m_w_dkv, g_kv_latent=m_g_kv_latent,
                 w_uk=m_w_uk, w_uv=m_w_uv, g_k=m_g_k, ln_mix_b=m_ln_mix_b, w_dq=m_w_dq, g_q_latent=m_g_q_latent,
                 w_uq=m_w_uq, g_q=m_g_q, w_o=m_w_o)
    mom_v = dict(ln_mix_a=v_ln_mix_a, w_pool=v_w_pool, b_pool=v_b_pool, pool_scale=v_pool_scale, ln_ffn=v_ln_ffn,
                 w_gate=v_w_gate, w_up=v_w_up, w_down=v_w_down, ln_kv=v_ln_kv, w_dkv=v_w_dkv, g_kv_latent=v_g_kv_latent,
                 w_uk=v_w_uk, w_uv=v_w_uv, g_k=v_g_k, ln_mix_b=v_ln_mix_b, w_dq=v_w_dq, g_q_latent=v_g_q_latent,
                 w_uq=v_w_uq, g_q=v_g_q, w_o=v_w_o)
    order = list(weights)
    s = x.shape[1]
    d = D_MODEL
    xs = x.reshape(s, d)
    target = loss_target.reshape(s, d)
    my_chip = 2 * lax.axis_index("x") + lax.axis_index("y")

    mat_names = ("w_pool", "w_dkv", "w_uk", "w_uv", "w_dq", "w_uq", "w_o")
    vec_names = ("ln_mix_a", "b_pool", "pool_scale")
    mat_shapes = [weights[n].shape for n in mat_names]
    vec_shapes = [weights[n].shape for n in vec_names]

    def rows_of(a, lead=0):
        return a.reshape(a.shape[:lead] + (-1, a.shape[-1]))

    mats_local = tuple(rows_of(weights[n].astype(WIRE_DTYPE)) for n in mat_names)
    vecs_local = _pack([weights[n] for n in vec_names], F32)

    def landing(shard):
        return lax.dynamic_update_slice_in_dim(lax.empty((N_SHARD,) + shard.shape, shard.dtype), shard[None], my_chip, axis=0)

    def gather_of(shards):
        return Exchange("gather", tuple(shards), tuple(landing(sh) for sh in shards))

    def ffn_gathers(l):
        return [gather_of(tuple(w[l].astype(WIRE_DTYPE) for w in (w_gate, w_up))), gather_of((w_down[l].astype(WIRE_DTYPE),))]

    gathers = [gather_of(mats_local[:1] + (vecs_local,))]
    ffn_at = {}
    for l in range(DEPTH):
        if l == N_A:
            attn_at = len(gathers)
            gathers.append(gather_of(mats_local[1:]))
        ffn_at[l] = len(gathers)
        gathers += ffn_gathers(l)
    gathering, _ = _exchange_start(gathers, name="gather_start")

    inv = ROPE_THETA ** (-jnp.arange(ROPE // 2, dtype=F32) * 2.0 / ROPE)
    inv_lanes = jnp.concatenate([inv, inv, jnp.zeros((LANES - ROPE,), F32)]).reshape(1, LANES)
    cos_t, sin_t = _rope_tables(positions.reshape(s, 1).astype(F32), inv_lanes, name="rope_tables")

    g_pool, vecs_all = _exchange_wait(gathering[0], cos_t, name="gather_wait_small")
    g_lna, g_bp, g_ps = _unpack(vecs_all, vec_shapes, lead=1)
    wpool_f = g_pool.reshape((N_SHARD,) + mat_shapes[0]).transpose(1, 2, 0, 3, 4).reshape(N_A, N_GROUPS, GROUP_DIM, GROUP_DIM)
    bpool_f = g_bp.transpose(1, 2, 0, 3).reshape(N_A, 1, d)
    pscale_f = g_ps.transpose(1, 0, 2).reshape(N_A, 1, d)
    lna_f = g_lna.transpose(1, 0, 2).reshape(N_A, 1, d)

    def head_gain(g):
        return jnp.pad(g.reshape(1, QK_DIM), ((0, 0), (0, HEAD_PAD - QK_DIM)))

    ffn_w = [None] * DEPTH
    ffn_names = ("w_gate", "w_up", "w_down")
    ffn_2d = {nm: tuple(rows_of(src[nm]) for src in (weights, mom_m, mom_v)) for nm in ffn_names}
    moment_views = {0: ffn_2d["w_gate"][1:], 1: ffn_2d["w_up"][1:]}

    def ffn_fwd(xin, layer):
        hf = _rms_fwd(xin, ln_ffn[layer].reshape(1, d), n=d, name="ffn_norm")
        wg, wu = _exchange_wait(gathering[ffn_at[layer]], [hf, *moment_views.get(layer, ())], name=f"gather_wait_up_{layer}")
        a, b, u = _ffn_up(hf, wg, wu, name="ffn_up")
        (wd,) = _exchange_wait(gathering[ffn_at[layer] + 1], u, name=f"gather_wait_down_{layer}")
        ffn_w[layer] = wg, wu, wd
        return _ffn_down(u, wd, xin, name="ffn_down"), (xin, hf, a, b, u)

    saved_a, saved_b, saved_f = [], [], []
    cur = xs
    for l in range(N_A):
        dpool = _rms_pool_fwd(cur, lna_f[l], name="pool_fwd")
        x1 = _pool_mm_fwd(dpool, wpool_f[l], bpool_f[l], pscale_f[l], cur, name="pool_mm")
        saved_a.append((cur, dpool))
        cur, sf = ffn_fwd(x1, l)
        saved_f.append(sf)

    x_kv = cur
    hk = _rms_fwd(x_kv, ln_kv.reshape(1, d), n=d, name="kv_norm")
    g_dkv, g_uk, g_uv, g_dq, g_uq, g_o = (a.reshape((N_SHARD,) + shp) for a, shp in zip(
        _exchange_wait(gathering[attn_at], hk, name="gather_wait_attn"), mat_shapes[1:]))
    wdkv_f = jnp.pad(g_dkv.reshape(d, KV_LORA + ROPE), ((0, 0), (0, CKV_PAD - KV_LORA - ROPE)))
    wuk_f = g_uk.transpose(1, 0, 2).reshape(KV_LORA, N_HEADS * NOPE)
    wuv_f = g_uv.transpose(1, 0, 2).reshape(KV_LORA, N_HEADS * V_DIM)
    wdq_f = g_dq.transpose(1, 0, 2, 3).reshape(N_B, d, Q_LORA)
    wuq_f = jnp.pad(g_uq.transpose(1, 2, 0, 3).reshape(N_B, Q_LORA, N_HEADS, QK_DIM),
                    ((0, 0), (0, 0), (0, 0), (0, HEAD_PAD - QK_DIM))).reshape(N_B, Q_LORA, N_HEADS * HEAD_PAD)
    wo_f = g_o.transpose(1, 0, 2, 3).reshape(N_B, d, d)
    ckv = _mm(hk, wdkv_f, name="kv_down")
    c_lat = _rms_fwd(ckv, g_kv_latent.reshape(1, KV_LORA), n=KV_LORA, name="kv_latent_norm")
    kn_raw = _mm(c_lat, wuk_f, name="k_up")
    v_all = _mm(c_lat, wuv_f, out_dtype=MXU_DTYPE, name="v_up")
    vt_all = _mm(wuv_f.T, c_lat, tb=True, out_dtype=MXU_DTYPE, name="v_up_t")
    k_raw = _k_assemble(kn_raw, ckv, name="k_assemble")
    gk_pad = head_gain(g_k)
    k_cat = _head_norm_rope_fwd(k_raw, gk_pad, cos_t, sin_t, name="k_norm_rope")

    for j in range(N_B):
        l = N_A + j
        hq = _rms_fwd(cur, ln_mix_b[j].reshape(1, d), n=d, name="q_norm")
        cq_raw = _mm(hq, wdq_f[j], name="q_down")
        cq = _rms_fwd(cq_raw, g_q_latent[j].reshape(1, Q_LORA), n=Q_LORA, name="q_latent_norm")
        q_raw = _mm(cq, wuq_f[j], name="q_up")
        gq_pad = head_gain(g_q[j])
        q_cat = _head_norm_rope_fwd(q_raw, gq_pad, cos_t, sin_t, name="q_norm_rope")
        ot, lse = _attn_fwd(q_cat, k_cat, vt_all, name="attn_fwd")
        x1 = _mm(ot, wo_f[j], ta=True, resid=cur, name="attn_out")
        saved_b.append((cur, hq, cq_raw, cq, q_raw, gq_pad, q_cat, ot, lse))
        cur, sf = ffn_fwd(x1, l)
        saved_f.append(sf)

    dy, loss_part = _loss_head(cur, target, name="loss_head")

    ffn_landed = (lax.empty((N_SHARD, DEPTH, d, FF_SHARD), WIRE_DTYPE), lax.empty((N_SHARD, DEPTH, d, FF_SHARD), WIRE_DTYPE),
                  lax.empty((N_SHARD, DEPTH, FF_SHARD, d), WIRE_DTYPE))
    scattering = None
    grads = {}
    d_ln_ffn = [None] * DEPTH

    def own_part(full):
        return lax.dynamic_index_in_dim(full, my_chip, axis=0, keepdims=True)

    def ffn_bwd(dyv, layer):
        nonlocal ffn_landed, scattering
        xin, hf, a, b, u = saved_f[layer]
        wg, wu, wd = ffn_w[layer]
        da, db = _ffn_bwd_hidden(dyv, wd, a, b, name="ffn_bwd_hidden")
        dwd = _ffn_bwd_dwd(u, dyv, name="ffn_bwd_dwd")
        dwg, dwu = _ffn_bwd_dwgu(hf, da, db, name="ffn_bwd_dwgu")
        if scattering is not None:
            ffn_landed = _exchange_wait(scattering, dwg, name=f"scatter_wait_{layer + 1}")
        ffn_landed = tuple(lax.dynamic_update_slice(buf, own_part(g)[:, None], (my_chip, layer, 0, 0))
                           for buf, g in zip(ffn_landed, (dwg, dwu, dwd)))
        (scattering,), started = _exchange_start([Exchange("scatter", (dwg, dwu, dwd), ffn_landed, layer)],
                                                 name=f"scatter_start_{layer}")
        dhf = _ffn_bwd_dh(da, db, wg, wu, name="ffn_bwd_dh")
        dx, dg = _rms_bwd(xin, ln_ffn[layer].reshape(1, d), dhf, n=d, dx_in=dyv, after=started, name="ffn_norm_bwd")
        d_ln_ffn[layer] = dg.sum(axis=0)
        return dx

    dk_acc = dv_acc = None
    d_ln_mix_b, d_w_dq, d_g_q_latent, d_w_uq, d_g_q, d_w_o = ([None] * N_B for _ in range(6))
    dcur = dy
    for j in reversed(range(N_B)):
        l = N_A + j
        xin, hq, cq_raw, cq, q_raw, gq_pad, q_cat, ot, lse = saved_b[j]
        dx1 = ffn_bwd(dcur, l)
        do = _mm(dx1, wo_f[j], tb=True, out_dtype=MXU_DTYPE, name="attn_out_bwd")
        dot = _mm(wo_f[j], dx1, tb=True, name="attn_out_bwd_t")
        d_w_o[j] = _mm_tn(ot, dx1, at=True, name="attn_out_dw")
        delta = _attn_delta(ot, dot, name="attn_delta")
        lse_col, delta_col = lse.reshape(N_HEADS, s, 1), delta.reshape(N_HEADS, s, 1)
        dq_cat = _attn_bwd_dq(q_cat, k_cat, v_all, do, lse_col, delta_col, name="attn_bwd_dq")
        dk_acc, dv_acc = _attn_bwd_dkv(q_cat, k_cat, v_all, do, lse, delta, dk_acc, dv_acc, name="attn_bwd_dkv")
        dq_raw, dgq = _head_norm_rope_bwd(q_raw, gq_pad, cos_t, sin_t, dq_cat, name="q_norm_rope_bwd")
        d_g_q[j] = dgq.sum(axis=0)[:QK_DIM]
        dcq = _mm(dq_raw, wuq_f[j], tb=True, name="q_up_bwd")
        d_w_uq[j] = _mm_tn(cq, dq_raw, name="q_up_dw").reshape(Q_LORA, N_HEADS, HEAD_PAD)[:, :, :QK_DIM].reshape(Q_LORA, N_HEADS * QK_DIM)
        dcq_raw, dgl = _rms_bwd(cq_raw, g_q_latent[j].reshape(1, Q_LORA), dcq, n=Q_LORA, name="q_latent_norm_bwd")
        d_g_q_latent[j] = dgl.sum(axis=0)
        dhq = _mm(dcq_raw, wdq_f[j], tb=True, name="q_down_bwd")
        d_w_dq[j] = _mm_tn(hq, dcq_raw, name="q_down_dw")
        dcur, dgm = _rms_bwd(xin, ln_mix_b[j].reshape(1, d), dhq, n=d, dx_in=dx1, name="q_norm_bwd")
        d_ln_mix_b[j] = dgm.sum(axis=0)

    dk_raw, dgk = _head_norm_rope_bwd(k_raw, gk_pad, cos_t, sin_t, dk_acc, name="k_norm_rope_bwd")
    grads["g_k"] = dgk.sum(axis=0)[:QK_DIM]
    dc = _mm(dv_acc, wuv_f, tb=True, name="v_up_bwd")
    grads["w_uv"] = _mm_tn(c_lat, dv_acc, name="v_up_dw")
    dkn, dpe = _k_disassemble(dk_raw, name="k_disassemble")
    dc = _mm(dkn, wuk_f, tb=True, resid=dc, name="k_up_bwd")
    grads["w_uk"] = _mm_tn(c_lat, dkn, name="k_up_dw")
    dc_raw, dgl = _rms_bwd(ckv, g_kv_latent.reshape(1, KV_LORA), dc, n=KV_LORA, name="kv_latent_norm_bwd")
    grads["g_kv_latent"] = dgl.sum(axis=0)
    dckv = jnp.concatenate([dc_raw, dpe], axis=1)
    dhk = _mm(dckv, wdkv_f, tb=True, name="kv_down_bwd")
    grads["w_dkv"] = _mm_tn(hk, dckv, name="kv_down_dw")[:, :KV_LORA + ROPE]
    gm = {
        "w_dkv": grads["w_dkv"].reshape(N_SHARD, d // N_SHARD, KV_LORA + ROPE),
        "w_uk": grads["w_uk"].reshape(KV_LORA, N_SHARD, -1).transpose(1, 0, 2),
        "w_uv": grads["w_uv"].reshape(KV_LORA, N_SHARD, -1).transpose(1, 0, 2),
        "w_dq": jnp.stack(d_w_dq).reshape(N_B, N_SHARD, d // N_SHARD, Q_LORA).transpose(1, 0, 2, 3),
        "w_uq": jnp.stack(d_w_uq).reshape(N_B, Q_LORA, N_SHARD, -1).transpose(2, 0, 1, 3),
        "w_o": jnp.stack(d_w_o).reshape(N_B, N_SHARD, d // N_SHARD, d).transpose(1, 0, 2, 3),
    }

    def scatter_of(partials):
        zones = tuple(lax.dynamic_update_slice_in_dim(lax.empty(g.shape, WIRE_DTYPE), own_part(g), my_chip, axis=0) for g in partials)
        return Exchange("scatter", tuple(partials), zones)

    attn_partials = [rows_of(gm[n].astype(WIRE_DTYPE), lead=1) for n in mat_names[1:]]
    (attn_scatter,), started = _exchange_start([scatter_of(attn_partials)], name="scatter_start_attn")
    dcur, dg = _rms_bwd(x_kv, ln_kv.reshape(1, d), dhk, n=d, dx_in=dcur, after=started, name="kv_norm_bwd")
    grads["ln_kv"] = dg.sum(axis=0)

    d_ln_mix_a, d_w_pool, d_b_pool, d_pool_scale = ([None] * N_A for _ in range(4))
    for l in reversed(range(N_A)):
        xin, dpool = saved_a[l]
        dx1 = ffn_bwd(dcur, l)
        dd, dwp, dbp, dsp = _pool_mm_bwd(dpool, wpool_f[l], bpool_f[l], pscale_f[l], dx1, name="pool_mm_bwd")
        d_w_pool[l], d_b_pool[l], d_pool_scale[l] = dwp, dbp.sum(axis=0), dsp.sum(axis=0)
        dcur, dg = _rms_pool_bwd(xin, lna_f[l], dd, dx1, name="pool_bwd")
        d_ln_mix_a[l] = dg.sum(axis=0)
    grad_x = dcur.reshape(1, s, d)

    pool_partial = jnp.stack(d_w_pool).reshape(N_A, N_GROUPS, N_SHARD, GROUP_DIM // N_SHARD, GROUP_DIM).transpose(2, 0, 1, 3, 4)
    (pool_scatter,), _ = _exchange_start([scatter_of([rows_of(pool_partial.astype(WIRE_DTYPE), lead=1)])], name="scatter_start_small")
    out_g, out_d, out_m, out_v = {}, {}, {}, {}

    def reduce_and_update(names, views, landed, *, tag):
        chip_sums = [_sum_slots(p, name="sum_chips") for p in landed]
        sib_sums = _swap_with_sibling(chip_sums, name=f"swap_sibling_{tag}")
        deltas = []
        for nm, (w2, m2, v2), own, sib in zip(names, views, chip_sums, sib_sums):
            res = _adamw(w2, m2, v2, [own, sib], name=f"adamw_{tag}")
            shp = weights[nm].shape
            out_g[nm], out_d[nm], out_m[nm], out_v[nm] = (r.reshape(shp) for r in res)
            deltas.append(res[1])
        return deltas

    def mat_views(nm):
        return tuple(rows_of(src[nm]) for src in (weights, mom_m, mom_v))

    attn_landed = _exchange_wait(attn_scatter, dcur, name="scatter_wait_attn")
    updated = reduce_and_update(mat_names[1:], [mat_views(n) for n in mat_names[1:]], attn_landed, tag="attn")

    vec_full = {
        "ln_mix_a": jnp.stack(d_ln_mix_a), "b_pool": jnp.stack(d_b_pool).reshape(N_A, N_GROUPS, GROUP_DIM),
        "pool_scale": jnp.stack(d_pool_scale), "ln_ffn": jnp.stack(d_ln_ffn), "ln_kv": grads["ln_kv"],
        "g_kv_latent": grads["g_kv_latent"], "g_k": grads["g_k"], "ln_mix_b": jnp.stack(d_ln_mix_b),
        "g_q_latent": jnp.stack(d_g_q_latent), "g_q": jnp.stack(d_g_q),
    }
    small_names = list(vec_full)
    small_shapes = [vec_full[n].shape for n in small_names] + [(SUBLANES * LANES,)]
    small = _all_reduce_small(_pack([vec_full[n] for n in small_names] + [loss_part.reshape(-1)], F32), name="all_reduce_small")
    small_sum = _unpack(small, small_shapes)
    loss = jnp.sum(small_sum[-1])
    vec_grad = dict(zip(small_names, small_sum[:-1]))
    vec_grad["ln_mix_a"] = lax.dynamic_slice_in_dim(vec_grad["ln_mix_a"], my_chip * (d // N_SHARD), d // N_SHARD, axis=1)
    vec_grad["pool_scale"] = lax.dynamic_slice_in_dim(vec_grad["pool_scale"], my_chip * (d // N_SHARD), d // N_SHARD, axis=1)
    vec_grad["b_pool"] = lax.dynamic_slice_in_dim(vec_grad["b_pool"], my_chip * (GROUP_DIM // N_SHARD), GROUP_DIM // N_SHARD, axis=2)

    def as_rows(a):
        return a.reshape(1, -1) if a.ndim == 1 else rows_of(a)

    res = _adamw_vectors([as_rows(weights[n]) for n in small_names], [as_rows(mom_m[n]) for n in small_names],
                         [as_rows(mom_v[n]) for n in small_names],
                         [as_rows(vec_grad[n].reshape(weights[n].shape)) for n in small_names], name="adamw_vectors")
    for tgt, arrs in zip((out_g, out_d, out_m, out_v), res):
        for n, arr in zip(small_names, arrs):
            tgt[n] = arr.reshape(weights[n].shape)

    ffn_landed = _exchange_wait(scattering, updated + [res[1][0]], name="scatter_wait_0")
    pool_landed = _exchange_wait(pool_scatter, ffn_landed[0], name="scatter_wait_small")
    landed = [ffn_landed[0].reshape(N_SHARD, DEPTH * d, FF_SHARD), ffn_landed[1].reshape(N_SHARD, DEPTH * d, FF_SHARD),
              ffn_landed[2].reshape(N_SHARD, DEPTH * FF_SHARD, d), *pool_landed]
    reduce_and_update(ffn_names + mat_names[:1], [ffn_2d[n] for n in ffn_names] + [mat_views(mat_names[0])], landed, tag="ffn")

    return (loss, grad_x, *[out_g[n] for n in order], *[out_d[n] for n in order],
            *[out_m[n] for n in order], *[out_v[n] for n in order])
```

```python
import math
from typing import Any, NamedTuple

import jax
import jax.numpy as jnp
from jax import lax
from jax.experimental import pallas as pl
from jax.experimental.pallas import tpu as pltpu

F32 = jnp.float32
BF16 = jnp.bfloat16
MXU_DTYPE = BF16
WIRE_DTYPE = BF16
SAVED_DTYPE = BF16

D_MODEL = 1024
N_A = 2
N_B = 2
DEPTH = 4
POOL_WINDOWS = (2, 4, 8, 16)
N_GROUPS = 4
GROUP_DIM = 256
POOL_HALO = 16
N_HEADS = 8
NOPE = 128
ROPE = 64
QK_DIM = 192
HEAD_PAD = 256
V_DIM = 128
Q_LORA = 256
KV_LORA = 512
CKV_PAD = 640
ROPE_THETA = 10000.0
CHUNK = 64
EPS = 1e-6
N_SHARD = 4
FF_SHARD = 704
FFN_ROWS = 1024
FFN_GRAD_ROWS = 2048
LANES = 128
SUBLANES = 8
ADAM_LR, ADAM_B1, ADAM_B2, ADAM_EPS, ADAM_WD, ADAM_STEP = 0.001, 0.9, 0.999, 1e-08, 0.01, 10
MESH = pl.DeviceIdType.MESH
ANY = pl.BlockSpec(memory_space=pl.ANY)
VMEM_SPEC = pl.BlockSpec(memory_space=pltpu.VMEM)


def _tile(n, pref):
    if n <= pref:
        return n
    t = pref - pref % SUBLANES
    while n % t:
        t -= SUBLANES
    return t


def _fold8(v):
    r, n = v.shape
    return v.reshape(r // SUBLANES, SUBLANES, n).sum(axis=0)


def _dot(a, b, dims):
    return lax.dot_general(a.astype(MXU_DTYPE), b.astype(MXU_DTYPE), (dims, ((), ())),
                           preferred_element_type=F32)


def _nn(a, b):
    return _dot(a, b, ((1,), (0,)))


def _nt(a, b):
    return _dot(a, b, ((1,), (1,)))


def _tn(a, b):
    return _dot(a, b, ((0,), (0,)))


def _mm(a, b, *, ta=False, tb=False, resid=None, out_dtype=F32, name):
    assert not (ta and tb)
    m, k = (a.shape[1], a.shape[0]) if ta else a.shape
    n = b.shape[0] if tb else b.shape[1]
    tm, tn = _tile(m, 512), _tile(n, 1024)

    def body(*refs):
        if resid is None:
            a_ref, b_ref, o_ref = refs
        else:
            a_ref, b_ref, r_ref, o_ref = refs
        acc = (_tn if ta else _nt if tb else _nn)(a_ref[...], b_ref[...])
        if resid is not None:
            acc = r_ref[...] + acc
        o_ref[...] = acc.astype(o_ref.dtype)

    in_specs = [pl.BlockSpec((k, tm), lambda i, j: (0, i)) if ta else pl.BlockSpec((tm, k), lambda i, j: (i, 0)),
                pl.BlockSpec((tn, k), lambda i, j: (j, 0)) if tb else pl.BlockSpec((k, tn), lambda i, j: (0, j))]
    args = [a, b]
    if resid is not None:
        in_specs.append(pl.BlockSpec((tm, tn), lambda i, j: (i, j)))
        args.append(resid)
    return pl.pallas_call(
        body, name=name, grid=(m // tm, n // tn), in_specs=in_specs,
        out_specs=pl.BlockSpec((tm, tn), lambda i, j: (i, j)),
        out_shape=jax.ShapeDtypeStruct((m, n), out_dtype))(*args)


def _mm_tn(a, b, *, name, at=False, out_dtype=F32):
    m = b.shape[0]
    k1 = a.shape[0] if at else a.shape[1]
    n = b.shape[1]
    tm, tn = _tile(m, 512), _tile(n, 1024)
    nm = m // tm

    def body(a_ref, b_ref, o_ref, acc):
        i = pl.program_id(1)

        @pl.when(i == 0)
        def _():
            acc[...] = jnp.zeros_like(acc)

        acc[...] += (_nn if at else _tn)(a_ref[...], b_ref[...])

        @pl.when(i == nm - 1)
        def _():
            o_ref[...] = acc[...].astype(o_ref.dtype)

    return pl.pallas_call(
        body, name=name, grid=(n // tn, nm),
        in_specs=[pl.BlockSpec((k1, tm), lambda j, i: (0, i)) if at else pl.BlockSpec((tm, k1), lambda j, i: (i, 0)),
                  pl.BlockSpec((tm, tn), lambda j, i: (i, j))],
        out_specs=pl.BlockSpec((k1, tn), lambda j, i: (0, j)),
        out_shape=jax.ShapeDtypeStruct((k1, n), out_dtype),
        scratch_shapes=[pltpu.VMEM((k1, tn), F32)])(a, b)


def _rms_fwd(x, g, *, n, n_valid=None, name):
    out_dtype = MXU_DTYPE
    rows = x.shape[0]
    tm = _tile(rows, 512)
    inv_n = 1.0 / (n_valid or n)

    def body(x_ref, g_ref, o_ref):
        xv = x_ref[...]
        r = lax.rsqrt(jnp.sum(xv * xv, axis=-1, keepdims=True) * inv_n + EPS)
        o_ref[...] = (xv * r * g_ref[...]).astype(o_ref.dtype)

    return pl.pallas_call(
        body, name=name, grid=(rows // tm,),
        in_specs=[pl.BlockSpec((tm, n), lambda i: (i, 0)), pl.BlockSpec((1, n), lambda i: (0, 0))],
        out_specs=pl.BlockSpec((tm, n), lambda i: (i, 0)),
        out_shape=jax.ShapeDtypeStruct((rows, n), out_dtype))(x, g)


def _rms_bwd_math(xv, gv, dyv, inv_n):
    r = lax.rsqrt(jnp.sum(xv * xv, axis=-1, keepdims=True) * inv_n + EPS)
    xh = xv * r
    gy = dyv * gv
    dx = r * (gy - xh * (jnp.sum(gy * xh, axis=-1, keepdims=True) * inv_n))
    return dx, dyv * xh


def _rms_bwd(x, g, dy, *, n, dx_in=None, after=None, name):
    rows = x.shape[0]
    tm = _tile(rows, 512)
    inv_n = 1.0 / n

    def body(*refs):
        if after is not None:
            refs = refs[:-3] + refs[-2:]
        if dx_in is None:
            x_ref, g_ref, dy_ref, dx_ref, dg_ref = refs
        else:
            x_ref, g_ref, dy_ref, din_ref, dx_ref, dg_ref = refs
        dx, dgc = _rms_bwd_math(x_ref[...], g_ref[...], dy_ref[...], inv_n)
        if dx_in is not None:
            dx = din_ref[...] + dx
        dx_ref[...] = dx

        @pl.when(pl.program_id(0) == 0)
        def _():
            dg_ref[...] = jnp.zeros_like(dg_ref)

        dg_ref[...] += _fold8(dgc)

    row_spec = pl.BlockSpec((tm, n), lambda i: (i, 0))
    in_specs = [row_spec, pl.BlockSpec((1, n), lambda i: (0, 0)), row_spec]
    args = [x, g, dy]
    if dx_in is not None:
        in_specs.append(row_spec)
        args.append(dx_in)
    if after is not None:
        in_specs.append(ANY)
        args.append(after)
    return pl.pallas_call(
        body, name=name, grid=(rows // tm,), in_specs=in_specs,
        out_specs=[row_spec, pl.BlockSpec((SUBLANES, n), lambda i: (0, 0))],
        out_shape=[jax.ShapeDtypeStruct((rows, n), F32), jax.ShapeDtypeStruct((SUBLANES, n), F32)])(*args)


def _pool_counts(t0, tm, w):
    t = t0 + lax.broadcasted_iota(jnp.int32, (tm, 1), 0)
    return jnp.minimum(t + 1, w).astype(F32)


def _rms_pool_fwd(x, g, *, name):
    s, d = x.shape
    tm = _tile(s, 512)
    hb = tm // POOL_HALO

    def body(x_ref, halo_ref, g_ref, o_ref):
        i = pl.program_id(0)
        gv = g_ref[...]

        def norm(v):
            return v * lax.rsqrt(jnp.mean(v * v, axis=-1, keepdims=True) + EPS) * gv

        h = norm(x_ref[...])
        halo = norm(halo_ref[...]) * (i > 0).astype(F32)
        hh = jnp.concatenate([halo, h], axis=0)
        rows = tm + POOL_HALO
        for gi, w in enumerate(POOL_WINDOWS):
            cols = slice(gi * GROUP_DIM, (gi + 1) * GROUP_DIM)
            acc = hh[:, cols]
            k = 1
            while k < w:
                acc = acc + pltpu.roll(acc, k, 0)
                k *= 2
            win = acc[POOL_HALO:rows]
            o_ref[:, cols] = (win / _pool_counts(i * tm, tm, w) - h[:, cols]).astype(o_ref.dtype)

    return pl.pallas_call(
        body, name=name, grid=(s // tm,),
        in_specs=[pl.BlockSpec((tm, d), lambda i: (i, 0)),
                  pl.BlockSpec((POOL_HALO, d), lambda i: (jnp.maximum(i * hb - 1, 0), 0)),
                  pl.BlockSpec((1, d), lambda i: (0, 0))],
        out_specs=pl.BlockSpec((tm, d), lambda i: (i, 0)),
        out_shape=jax.ShapeDtypeStruct((s, d), MXU_DTYPE))(x, x, g)


def _rms_pool_bwd(x, g, dd, dx_in, *, name):
    s, d = x.shape
    tm = _tile(s, 512)
    hb = tm // POOL_HALO
    nt = s // tm

    def body(x_ref, g_ref, dd_ref, halo_ref, din_ref, dx_ref, dg_ref):
        i = pl.program_id(0)
        ddv = dd_ref[...]
        halo = halo_ref[...] * (i < nt - 1).astype(F32)
        rows = tm + POOL_HALO
        parts = []
        for gi, w in enumerate(POOL_WINDOWS):
            cols = slice(gi * GROUP_DIM, (gi + 1) * GROUP_DIM)
            acc = jnp.concatenate([ddv[:, cols] / _pool_counts(i * tm, tm, w), halo[:, cols] * (1.0 / w)], axis=0)
            k = 1
            while k < w:
                acc = acc + pltpu.roll(acc, rows - k, 0)
                k *= 2
            parts.append(acc[0:tm] - ddv[:, cols])
        dh = jnp.concatenate(parts, axis=1)
        dx, dgc = _rms_bwd_math(x_ref[...], g_ref[...], dh, 1.0 / d)
        dx_ref[...] = din_ref[...] + dx

        @pl.when(i == 0)
        def _():
            dg_ref[...] = jnp.zeros_like(dg_ref)

        dg_ref[...] += _fold8(dgc)

    row_spec = pl.BlockSpec((tm, d), lambda i: (i, 0))
    return pl.pallas_call(
        body, name=name, grid=(nt,),
        in_specs=[row_spec, pl.BlockSpec((1, d), lambda i: (0, 0)), row_spec,
                  pl.BlockSpec((POOL_HALO, d), lambda i: (jnp.minimum((i + 1) * hb, s // POOL_HALO - 1), 0)),
                  row_spec],
        out_specs=[row_spec, pl.BlockSpec((SUBLANES, d), lambda i: (0, 0))],
        out_shape=[jax.ShapeDtypeStruct((s, d), F32), jax.ShapeDtypeStruct((SUBLANES, d), F32)])(x, g, dd, dd, dx_in)


def _pool_mm_fwd(dpool, w, b, scale, x, *, name):
    s, d = x.shape
    tm = _tile(s, 512)

    def body(d_ref, w_ref, b_ref, s_ref, x_ref, o_ref):
        for gi in range(N_GROUPS):
            cols = slice(gi * GROUP_DIM, (gi + 1) * GROUP_DIM)
            y = _nn(d_ref[:, cols], w_ref[gi]) + b_ref[:, cols]
            o_ref[:, cols] = x_ref[:, cols] + y * s_ref[:, cols]

    row_spec = pl.BlockSpec((tm, d), lambda i: (i, 0))
    vec_spec = pl.BlockSpec((1, d), lambda i: (0, 0))
    return pl.pallas_call(
        body, name=name, grid=(s // tm,),
        in_specs=[row_spec, pl.BlockSpec((N_GROUPS, GROUP_DIM, GROUP_DIM), lambda i: (0, 0, 0)), vec_spec, vec_spec, row_spec],
        out_specs=row_spec, out_shape=jax.ShapeDtypeStruct((s, d), F32))(dpool, w, b, scale, x)


def _pool_mm_bwd(dpool, w, b, scale, dx, *, name):
    s, d = dx.shape
    tm = _tile(s, 512)

    def body(d_ref, w_ref, b_ref, s_ref, dx_ref, dd_ref, dw_ref, db_ref, ds_ref):
        @pl.when(pl.program_id(0) == 0)
        def _():
            dw_ref[...] = jnp.zeros_like(dw_ref)
            db_ref[...] = jnp.zeros_like(db_ref)
            ds_ref[...] = jnp.zeros_like(ds_ref)

        for gi in range(N_GROUPS):
            cols = slice(gi * GROUP_DIM, (gi + 1) * GROUP_DIM)
            dg = d_ref[:, cols]
            y = _nn(dg, w_ref[gi]) + b_ref[:, cols]
            dxg = dx_ref[:, cols]
            dy = dxg * s_ref[:, cols]
            ds_ref[:, cols] += _fold8(dxg * y)
            db_ref[:, cols] += _fold8(dy)
            dw_ref[gi] += _tn(dg, dy)
            dd_ref[:, cols] = _nt(dy, w_ref[gi])

    row_spec = pl.BlockSpec((tm, d), lambda i: (i, 0))
    vec_spec = pl.BlockSpec((1, d), lambda i: (0, 0))
    w_spec = pl.BlockSpec((N_GROUPS, GROUP_DIM, GROUP_DIM), lambda i: (0, 0, 0))
    part_spec = pl.BlockSpec((SUBLANES, d), lambda i: (0, 0))
    return pl.pallas_call(
        body, name=name, grid=(s // tm,),
        in_specs=[row_spec, w_spec, vec_spec, vec_spec, row_spec],
        out_specs=[row_spec, w_spec, part_spec, part_spec],
        out_shape=[jax.ShapeDtypeStruct((s, d), F32), jax.ShapeDtypeStruct((N_GROUPS, GROUP_DIM, GROUP_DIM), F32),
                   jax.ShapeDtypeStruct((SUBLANES, d), F32), jax.ShapeDtypeStruct((SUBLANES, d), F32)])(dpool, w, b, scale, dx)


def _sigmoid(a):
    return 0.5 * jnp.tanh(0.5 * a) + 0.5


def _ffn_up(hf, wg, wu, *, name):
    s, d = hf.shape
    tm = _tile(s, FFN_ROWS)

    def body(h_ref, wg_ref, wu_ref, a_ref, b_ref, u_ref):
        hv = h_ref[...]
        a = _nn(hv, wg_ref[...])
        b = _nn(hv, wu_ref[...])
        a_ref[...] = a.astype(a_ref.dtype)
        b_ref[...] = b.astype(b_ref.dtype)
        u_ref[...] = (a * _sigmoid(a) * b).astype(u_ref.dtype)

    w_spec = pl.BlockSpec((None, d, FF_SHARD), lambda j, i: (j, 0, 0))
    h_spec = pl.BlockSpec((None, tm, FF_SHARD), lambda j, i: (j, i, 0))
    hid = (N_SHARD, s, FF_SHARD)
    return pl.pallas_call(
        body, name=name, grid=(N_SHARD, s // tm),
        in_specs=[pl.BlockSpec((tm, d), lambda j, i: (i, 0)), w_spec, w_spec],
        out_specs=[h_spec, h_spec, h_spec],
        out_shape=[jax.ShapeDtypeStruct(hid, SAVED_DTYPE), jax.ShapeDtypeStruct(hid, SAVED_DTYPE),
                   jax.ShapeDtypeStruct(hid, MXU_DTYPE)])(hf, wg, wu)


def _ffn_down(u, wd, x, *, name):
    s, d = x.shape
    tm = _tile(s, 1024)

    def body(u_ref, w_ref, x_ref, o_ref):
        j = pl.program_id(1)

        @pl.when(j == 0)
        def _():
            o_ref[...] = x_ref[...]

        o_ref[...] += _nn(u_ref[...], w_ref[...])

    return pl.pallas_call(
        body, name=name, grid=(s // tm, N_SHARD),
        in_specs=[pl.BlockSpec((None, tm, FF_SHARD), lambda i, j: (j, i, 0)),
                  pl.BlockSpec((None, FF_SHARD, d), lambda i, j: (j, 0, 0)),
                  pl.BlockSpec((tm, d), lambda i, j: (i, 0))],
        out_specs=pl.BlockSpec((tm, d), lambda i, j: (i, 0)),
        out_shape=jax.ShapeDtypeStruct((s, d), F32))(u, wd, x)


def _ffn_bwd_hidden(dy, wd, a, b, *, name):
    s, d = dy.shape
    tm = _tile(s, FFN_ROWS)

    def body(dy_ref, w_ref, a_ref, b_ref, da_ref, db_ref):
        du = _nt(dy_ref[...], w_ref[...])
        av, bv = a_ref[...].astype(F32), b_ref[...].astype(F32)
        sg = _sigmoid(av)
        da_ref[...] = (du * bv * (sg * (1.0 + av * (1.0 - sg)))).astype(da_ref.dtype)
        db_ref[...] = (du * (av * sg)).astype(db_ref.dtype)

    h_spec = pl.BlockSpec((None, tm, FF_SHARD), lambda i, j: (j, i, 0))
    hid = jax.ShapeDtypeStruct((N_SHARD, s, FF_SHARD), MXU_DTYPE)
    return pl.pallas_call(
        body, name=name, grid=(s // tm, N_SHARD),
        in_specs=[pl.BlockSpec((tm, d), lambda i, j: (i, 0)),
                  pl.BlockSpec((None, FF_SHARD, d), lambda i, j: (j, 0, 0)), h_spec, h_spec],
        out_specs=[h_spec, h_spec], out_shape=[hid, hid])(dy, wd, a, b)


def _ffn_bwd_dwd(u, dy, *, name):
    s, d = dy.shape
    tm = _tile(s, FFN_GRAD_ROWS)
    nm = s // tm

    def body(u_ref, dy_ref, o_ref, acc):
        i = pl.program_id(1)

        @pl.when(i == 0)
        def _():
            acc[...] = jnp.zeros_like(acc)

        acc[...] += _tn(u_ref[...], dy_ref[...])

        @pl.when(i == nm - 1)
        def _():
            o_ref[...] = acc[...].astype(o_ref.dtype)

    return pl.pallas_call(
        body, name=name, grid=(N_SHARD, nm),
        in_specs=[pl.BlockSpec((None, tm, FF_SHARD), lambda j, i: (j, i, 0)), pl.BlockSpec((tm, d), lambda j, i: (i, 0))],
        out_specs=pl.BlockSpec((None, FF_SHARD, d), lambda j, i: (j, 0, 0)),
        out_shape=jax.ShapeDtypeStruct((N_SHARD, FF_SHARD, d), WIRE_DTYPE),
        scratch_shapes=[pltpu.VMEM((FF_SHARD, d), F32)])(u, dy)


def _ffn_bwd_dwgu(hf, da, db, *, name):
    s, d = hf.shape
    tm = _tile(s, FFN_GRAD_ROWS)
    nm = s // tm

    def body(h_ref, da_ref, db_ref, og_ref, ou_ref, accg, accu):
        i = pl.program_id(1)

        @pl.when(i == 0)
        def _():
            accg[...] = jnp.zeros_like(accg)
            accu[...] = jnp.zeros_like(accu)

        hv = h_ref[...]
        accg[...] += _tn(hv, da_ref[...])
        accu[...] += _tn(hv, db_ref[...])

        @pl.when(i == nm - 1)
        def _():
            og_ref[...] = accg[...].astype(og_ref.dtype)
            ou_ref[...] = accu[...].astype(ou_ref.dtype)

    h_spec = pl.BlockSpec((None, tm, FF_SHARD), lambda j, i: (j, i, 0))
    w_spec = pl.BlockSpec((None, d, FF_SHARD), lambda j, i: (j, 0, 0))
    grad = jax.ShapeDtypeStruct((N_SHARD, d, FF_SHARD), WIRE_DTYPE)
    return pl.pallas_call(
        body, name=name, grid=(N_SHARD, nm),
        in_specs=[pl.BlockSpec((tm, d), lambda j, i: (i, 0)), h_spec, h_spec],
        out_specs=[w_spec, w_spec], out_shape=[grad, grad],
        scratch_shapes=[pltpu.VMEM((d, FF_SHARD), F32), pltpu.VMEM((d, FF_SHARD), F32)])(hf, da, db)


def _ffn_bwd_dh(da, db, wg, wu, *, name):
    s = da.shape[1]
    d = wg.shape[1]
    tm = _tile(s, 1024)

    def body(da_ref, db_ref, wg_ref, wu_ref, o_ref):
        j = pl.program_id(1)

        @pl.when(j == 0)
        def _():
            o_ref[...] = jnp.zeros_like(o_ref)

        o_ref[...] += _nt(da_ref[...], wg_ref[...]) + _nt(db_ref[...], wu_ref[...])

    h_spec = pl.BlockSpec((None, tm, FF_SHARD), lambda i, j: (j, i, 0))
    w_spec = pl.BlockSpec((None, d, FF_SHARD), lambda i, j: (j, 0, 0))
    return pl.pallas_call(
        body, name=name, grid=(s // tm, N_SHARD),
        in_specs=[h_spec, h_spec, w_spec, w_spec],
        out_specs=pl.BlockSpec((tm, d), lambda i, j: (i, 0)),
        out_shape=jax.ShapeDtypeStruct((s, d), F32))(da, db, wg, wu)


def _rope_tables(pos, inv, *, name):
    s = pos.shape[0]
    tm = _tile(s, 512)
    half = ROPE // 2

    def body(p_ref, i_ref, c_ref, s_ref):
        ang = p_ref[...] * i_ref[...]
        lane = lax.broadcasted_iota(jnp.int32, ang.shape, 1)
        live = lane < ROPE
        c_ref[...] = jnp.where(live, jnp.cos(ang), 0.0)
        sn = jnp.sin(ang)
        s_ref[...] = jnp.where(live, jnp.where(lane < half, -sn, sn), 0.0)

    out = jax.ShapeDtypeStruct((s, LANES), F32)
    return pl.pallas_call(
        body, name=name, grid=(s // tm,),
        in_specs=[pl.BlockSpec((tm, 1), lambda i: (i, 0)), pl.BlockSpec((1, LANES), lambda i: (0, 0))],
        out_specs=[pl.BlockSpec((tm, LANES), lambda i: (i, 0))] * 2, out_shape=[out, out])(pos, inv)


def _swap_halves(v):
    half = ROPE // 2
    lane = lax.broadcasted_iota(jnp.int32, v.shape, 1)
    return jnp.where(lane < half, pltpu.roll(v, LANES - half, 1), pltpu.roll(v, half, 1))


def _head_norm_rope_fwd(raw, g, cos, sin, *, name):
    s = raw.shape[0]
    tm = _tile(s, 256)
    width = N_HEADS * HEAD_PAD

    def body(x_ref, g_ref, c_ref, s_ref, o_ref):
        cv, sv = c_ref[...], s_ref[...]
        for h in range(N_HEADS):
            lo = h * HEAD_PAD
            xa = x_ref[:, lo:lo + NOPE]
            xb = x_ref[:, lo + NOPE:lo + HEAD_PAD]
            ms = (jnp.sum(xa * xa, axis=-1, keepdims=True) + jnp.sum(xb * xb, axis=-1, keepdims=True)) * (1.0 / QK_DIM)
            r = lax.rsqrt(ms + EPS)
            o_ref[:, lo:lo + NOPE] = (xa * r * g_ref[:, 0:NOPE]).astype(o_ref.dtype)
            yb = xb * r * g_ref[:, NOPE:HEAD_PAD]
            o_ref[:, lo + NOPE:lo + HEAD_PAD] = (yb * cv + _swap_halves(yb) * sv).astype(o_ref.dtype)

    row_spec = pl.BlockSpec((tm, width), lambda i: (i, 0))
    tab_spec = pl.BlockSpec((tm, LANES), lambda i: (i, 0))
    return pl.pallas_call(
        body, name=name, grid=(s // tm,),
        in_specs=[row_spec, pl.BlockSpec((1, HEAD_PAD), lambda i: (0, 0)), tab_spec, tab_spec],
        out_specs=row_spec, out_shape=jax.ShapeDtypeStruct((s, width), MXU_DTYPE))(raw, g, cos, sin)


def _head_norm_rope_bwd(raw, g, cos, sin, dout, *, name):
    s = raw.shape[0]
    tm = _tile(s, 256)
    width = N_HEADS * HEAD_PAD

    def body(x_ref, g_ref, c_ref, s_ref, do_ref, dx_ref, dg_ref):
        @pl.when(pl.program_id(0) == 0)
        def _():
            dg_ref[...] = jnp.zeros_like(dg_ref)

        cv, sv = c_ref[...], s_ref[...]
        ga, gb = g_ref[:, 0:NOPE], g_ref[:, NOPE:HEAD_PAD]
        for h in range(N_HEADS):
            lo = h * HEAD_PAD
            xa = x_ref[:, lo:lo + NOPE]
            xb = x_ref[:, lo + NOPE:lo + HEAD_PAD]
            dya = do_ref[:, lo:lo + NOPE]
            dob = do_ref[:, lo + NOPE:lo + HEAD_PAD]
            dyb = dob * cv + _swap_halves(dob * sv)
            ms = (jnp.sum(xa * xa, axis=-1, keepdims=True) + jnp.sum(xb * xb, axis=-1, keepdims=True)) * (1.0 / QK_DIM)
            r = lax.rsqrt(ms + EPS)
            xha, xhb = xa * r, xb * r
            gya, gyb = dya * ga, dyb * gb
            dot = (jnp.sum(gya * xha, axis=-1, keepdims=True) + jnp.sum(gyb * xhb, axis=-1, keepdims=True)) * (1.0 / QK_DIM)
            dx_ref[:, lo:lo + NOPE] = r * (gya - xha * dot)
            dx_ref[:, lo + NOPE:lo + HEAD_PAD] = r * (gyb - xhb * dot)
            dg_ref[:, 0:NOPE] += _fold8(dya * xha)
            dg_ref[:, NOPE:HEAD_PAD] += _fold8(dyb * xhb)

    row_spec = pl.BlockSpec((tm, width), lambda i: (i, 0))
    tab_spec = pl.BlockSpec((tm, LANES), lambda i: (i, 0))
    return pl.pallas_call(
        body, name=name, grid=(s // tm,),
        in_specs=[row_spec, pl.BlockSpec((1, HEAD_PAD), lambda i: (0, 0)), tab_spec, tab_spec, row_spec],
        out_specs=[row_spec, pl.BlockSpec((SUBLANES, HEAD_PAD), lambda i: (0, 0))],
        out_shape=[jax.ShapeDtypeStruct((s, width), F32), jax.ShapeDtypeStruct((SUBLANES, HEAD_PAD), F32)])(raw, g, cos, sin, dout)


def _k_assemble(kn, ckv, *, name):
    s = kn.shape[0]
    tm = _tile(s, 512)
    width = N_HEADS * HEAD_PAD

    def body(kn_ref, pe_ref, o_ref):
        pe = pe_ref[...]
        for h in range(N_HEADS):
            o_ref[:, h * HEAD_PAD:h * HEAD_PAD + NOPE] = kn_ref[:, h * NOPE:(h + 1) * NOPE]
            o_ref[:, h * HEAD_PAD + NOPE:(h + 1) * HEAD_PAD] = pe

    return pl.pallas_call(
        body, name=name, grid=(s // tm,),
        in_specs=[pl.BlockSpec((tm, N_HEADS * NOPE), lambda i: (i, 0)),
                  pl.BlockSpec((tm, LANES), lambda i: (i, KV_LORA // LANES))],
        out_specs=pl.BlockSpec((tm, width), lambda i: (i, 0)),
        out_shape=jax.ShapeDtypeStruct((s, width), F32))(kn, ckv)


def _k_disassemble(dk_raw, *, name):
    s = dk_raw.shape[0]
    tm = _tile(s, 512)
    width = N_HEADS * HEAD_PAD

    def body(dk_ref, dkn_ref, dpe_ref):
        pe = dk_ref[:, NOPE:HEAD_PAD]
        for h in range(N_HEADS):
            dkn_ref[:, h * NOPE:(h + 1) * NOPE] = dk_ref[:, h * HEAD_PAD:h * HEAD_PAD + NOPE]
            if h:
                pe = pe + dk_ref[:, h * HEAD_PAD + NOPE:(h + 1) * HEAD_PAD]
        dpe_ref[...] = pe

    return pl.pallas_call(
        body, name=name, grid=(s // tm,),
        in_specs=[pl.BlockSpec((tm, width), lambda i: (i, 0))],
        out_specs=[pl.BlockSpec((tm, N_HEADS * NOPE), lambda i: (i, 0)), pl.BlockSpec((tm, LANES), lambda i: (i, 0))],
        out_shape=[jax.ShapeDtypeStruct((s, N_HEADS * NOPE), F32), jax.ShapeDtypeStruct((s, LANES), F32)])(dk_raw)


ATTN_SCALE = 1.0 / math.sqrt(QK_DIM)
MASKED = -1e30


ATTN_TILE = 512
ATTN_HEADS = 8


def _chunk_mask(q0, k0, shape, q_axis):
    qpos = q0 + lax.broadcasted_iota(jnp.int32, shape, q_axis)
    kpos = k0 + lax.broadcasted_iota(jnp.int32, shape, 1 - q_axis)
    return kpos // CHUNK <= qpos // CHUNK


LOG2E = math.log2(math.e)
SCORE_LOG2 = ATTN_SCALE * LOG2E


def _causal_pairs(n, by_key):
    if by_key:
        pairs = [(i, j) for j in range(n) for i in range(j, n)]
    else:
        pairs = [(i, j) for i in range(n) for j in range(i + 1)]
    return jnp.asarray([p[0] for p in pairs], jnp.int32), jnp.asarray([p[1] for p in pairs], jnp.int32)


def _attn_fwd(q, k, vt, *, name):
    s = q.shape[0]
    t = _tile(s, ATTN_TILE)
    n = s // t
    qi_tab, kj_tab = _causal_pairs(n, by_key=False)

    hg = ATTN_HEADS

    def body(qi_ref, kj_ref, q_ref, k_ref, vt_ref, o_ref, lse_ref, m_sc, l_sc, acc):
        pair = pl.program_id(1)
        qi, kj = qi_ref[pair], kj_ref[pair]

        @pl.when(kj == 0)
        def _():
            m_sc[...] = jnp.full_like(m_sc, MASKED)
            l_sc[...] = jnp.zeros_like(l_sc)
            acc[...] = jnp.zeros_like(acc)

        def step(masked):
            for g in range(hg):
                qk, vr = slice(g * HEAD_PAD, (g + 1) * HEAD_PAD), slice(g * V_DIM, (g + 1) * V_DIM)
                st = _nt(k_ref[:, qk], q_ref[:, qk])
                if masked:
                    st = jnp.where(_chunk_mask(qi * t, kj * t, (t, t), 1), st, MASKED)
                m_prev = m_sc[g]
                m_new = jnp.maximum(m_prev, jnp.max(st, axis=0, keepdims=True) * SCORE_LOG2)
                alpha = jnp.exp2(m_prev - m_new)
                pt = jnp.exp2(st * SCORE_LOG2 - m_new)
                l_new = alpha * l_sc[g] + jnp.sum(pt, axis=0, keepdims=True)
                a_new = alpha * acc[vr, :] + _nn(vt_ref[vr, :], pt)
                l_sc[g] = l_new
                acc[vr, :] = a_new
                m_sc[g] = m_new
                if masked:
                    o_ref[vr, :] = a_new / l_new
                    lse_ref[g] = m_new + jnp.log(l_new) * LOG2E

        @pl.when(kj < qi)
        def _():
            step(False)

        @pl.when(kj == qi)
        def _():
            step(True)

    return pl.pallas_call(
        body, name=name,
        grid_spec=pltpu.PrefetchScalarGridSpec(
            num_scalar_prefetch=2, grid=(N_HEADS // hg, int(qi_tab.shape[0])),
            in_specs=[pl.BlockSpec((t, hg * HEAD_PAD), lambda h, p, qi, kj: (qi[p], h)),
                      pl.BlockSpec((t, hg * HEAD_PAD), lambda h, p, qi, kj: (kj[p], h)),
                      pl.BlockSpec((hg * V_DIM, t), lambda h, p, qi, kj: (h, kj[p]))],
            out_specs=[pl.BlockSpec((hg * V_DIM, t), lambda h, p, qi, kj: (h, qi[p])),
                       pl.BlockSpec((hg, 1, t), lambda h, p, qi, kj: (h, 0, qi[p]))],
            scratch_shapes=[pltpu.VMEM((hg, 1, t), F32), pltpu.VMEM((hg, 1, t), F32), pltpu.VMEM((hg * V_DIM, t), F32)]),
        out_shape=[jax.ShapeDtypeStruct((N_HEADS * V_DIM, s), F32), jax.ShapeDtypeStruct((N_HEADS, 1, s), F32)])(qi_tab, kj_tab, q, k, vt)


def _attn_delta(ot, dot, *, name):
    s = ot.shape[1]
    t = _tile(s, 1024)

    def body(o_ref, do_ref, d_ref):
        d_ref[...] = jnp.sum(o_ref[...] * do_ref[...], axis=0, keepdims=True)

    blk = pl.BlockSpec((V_DIM, t), lambda h, i: (h, i))
    return pl.pallas_call(
        body, name=name, grid=(N_HEADS, s // t), in_specs=[blk, blk],
        out_specs=pl.BlockSpec((None, 1, t), lambda h, i: (h, 0, i)),
        out_shape=jax.ShapeDtypeStruct((N_HEADS, 1, s), F32))(ot, dot)


def _attn_bwd_dq(q, k, v, do, lse_col, delta_col, *, name):
    s = q.shape[0]
    t = _tile(s, ATTN_TILE)
    n = s // t
    qi_tab, kj_tab = _causal_pairs(n, by_key=False)

    hg = ATTN_HEADS

    def body(qi_ref, kj_ref, q_ref, k_ref, v_ref, do_ref, lse_ref, dl_ref, dq_ref, acc):
        pair = pl.program_id(1)
        qi, kj = qi_ref[pair], kj_ref[pair]

        @pl.when(kj == 0)
        def _():
            acc[...] = jnp.zeros_like(acc)

        def step(masked):
            for g in range(hg):
                qk, vc = slice(g * HEAD_PAD, (g + 1) * HEAD_PAD), slice(g * V_DIM, (g + 1) * V_DIM)
                kv = k_ref[:, qk]
                sc = _nt(q_ref[:, qk], kv)
                if masked:
                    sc = jnp.where(_chunk_mask(qi * t, kj * t, (t, t), 0), sc, MASKED)
                p = jnp.exp2(sc * SCORE_LOG2 - lse_ref[g])
                dp = _nt(do_ref[:, vc], v_ref[:, vc])
                total = acc[:, qk] + _nn(p * (dp - dl_ref[g]), kv)
                acc[:, qk] = total
                if masked:
                    dq_ref[:, qk] = total * ATTN_SCALE

        @pl.when(kj < qi)
        def _():
            step(False)

        @pl.when(kj == qi)
        def _():
            step(True)

    col = pl.BlockSpec((hg, t, 1), lambda h, p, qi, kj: (h, qi[p], 0))
    return pl.pallas_call(
        body, name=name,
        grid_spec=pltpu.PrefetchScalarGridSpec(
            num_scalar_prefetch=2, grid=(N_HEADS // hg, int(qi_tab.shape[0])),
            in_specs=[pl.BlockSpec((t, hg * HEAD_PAD), lambda h, p, qi, kj: (qi[p], h)),
                      pl.BlockSpec((t, hg * HEAD_PAD), lambda h, p, qi, kj: (kj[p], h)),
                      pl.BlockSpec((t, hg * V_DIM), lambda h, p, qi, kj: (kj[p], h)),
                      pl.BlockSpec((t, hg * V_DIM), lambda h, p, qi, kj: (qi[p], h)), col, col],
            out_specs=pl.BlockSpec((t, hg * HEAD_PAD), lambda h, p, qi, kj: (qi[p], h)),
            scratch_shapes=[pltpu.VMEM((t, hg * HEAD_PAD), F32)]),
        out_shape=jax.ShapeDtypeStruct((s, N_HEADS * HEAD_PAD), F32))(qi_tab, kj_tab, q, k, v, do, lse_col, delta_col)


def _attn_bwd_dkv(q, k, v, do, lse_row, delta_row, dk_in, dv_in, *, name):
    s = q.shape[0]
    t = _tile(s, ATTN_TILE)
    n = s // t
    has_in = dk_in is not None
    hg = ATTN_HEADS
    qi_tab, kj_tab = _causal_pairs(n, by_key=True)

    def body(qi_ref, kj_ref, *refs):
        if has_in:
            q_ref, k_ref, v_ref, do_ref, lse_ref, dl_ref, dki_ref, dvi_ref, dk_ref, dv_ref, acck, accv = refs
        else:
            q_ref, k_ref, v_ref, do_ref, lse_ref, dl_ref, dk_ref, dv_ref, acck, accv = refs
        pair = pl.program_id(1)
        qi, kj = qi_ref[pair], kj_ref[pair]

        def step(masked):
            for g in range(hg):
                qk, vc = slice(g * HEAD_PAD, (g + 1) * HEAD_PAD), slice(g * V_DIM, (g + 1) * V_DIM)
                qv, dov = q_ref[:, qk], do_ref[:, vc]
                st = _nt(k_ref[:, qk], qv)
                if masked:
                    st = jnp.where(_chunk_mask(qi * t, kj * t, (t, t), 1), st, MASKED)
                pt = jnp.exp2(st * SCORE_LOG2 - lse_ref[g])
                accv[:, vc] += _nn(pt, dov)
                dpt = _nt(v_ref[:, vc], dov)
                acck[:, qk] += _nn(pt * (dpt - dl_ref[g]), qv)

        @pl.when(qi == kj)
        def _():
            acck[...] = jnp.zeros_like(acck)
            accv[...] = jnp.zeros_like(accv)
            step(True)

        @pl.when(qi > kj)
        def _():
            step(False)

        @pl.when(qi == n - 1)
        def _():
            dk = acck[...] * ATTN_SCALE
            dv = accv[...]
            if has_in:
                dk = dki_ref[...] + dk
                dv = dvi_ref[...] + dv
            dk_ref[...] = dk
            dv_ref[...] = dv

    row = pl.BlockSpec((hg, 1, t), lambda h, p, qi, kj: (h, 0, qi[p]))
    k_spec = pl.BlockSpec((t, hg * HEAD_PAD), lambda h, p, qi, kj: (kj[p], h))
    v_spec = pl.BlockSpec((t, hg * V_DIM), lambda h, p, qi, kj: (kj[p], h))
    in_specs = [pl.BlockSpec((t, hg * HEAD_PAD), lambda h, p, qi, kj: (qi[p], h)), k_spec, v_spec,
                pl.BlockSpec((t, hg * V_DIM), lambda h, p, qi, kj: (qi[p], h)), row, row]
    args = [q, k, v, do, lse_row, delta_row]
    if has_in:
        in_specs += [k_spec, v_spec]
        args += [dk_in, dv_in]
    return pl.pallas_call(
        body, name=name,
        grid_spec=pltpu.PrefetchScalarGridSpec(
            num_scalar_prefetch=2, grid=(N_HEADS // hg, int(qi_tab.shape[0])), in_specs=in_specs, out_specs=[k_spec, v_spec],
            scratch_shapes=[pltpu.VMEM((t, hg * HEAD_PAD), F32), pltpu.VMEM((t, hg * V_DIM), F32)]),
        out_shape=[jax.ShapeDtypeStruct((s, N_HEADS * HEAD_PAD), F32), jax.ShapeDtypeStruct((s, N_HEADS * V_DIM), F32)])(qi_tab, kj_tab, *args)


def _loss_head(y, target, *, name):
    s, d = y.shape
    tm = _tile(s, 512)

    def body(y_ref, t_ref, dy_ref, l_ref):
        @pl.when(pl.program_id(0) == 0)
        def _():
            l_ref[...] = jnp.zeros_like(l_ref)

        err = y_ref[...] - t_ref[...]
        dy_ref[...] = err * (1.0 / d)
        sq = _fold8(err * err)
        part = sq[:, 0:LANES]
        for cb in range(1, d // LANES):
            part = part + sq[:, cb * LANES:(cb + 1) * LANES]
        l_ref[...] += part * (0.5 / d)

    row_spec = pl.BlockSpec((tm, d), lambda i: (i, 0))
    return pl.pallas_call(
        body, name=name, grid=(s // tm,), in_specs=[row_spec, row_spec],
        out_specs=[row_spec, pl.BlockSpec((SUBLANES, LANES), lambda i: (0, 0))],
        out_shape=[jax.ShapeDtypeStruct((s, d), F32), jax.ShapeDtypeStruct((SUBLANES, LANES), F32)])(y, target)


ADAMW_ROWS = 512


def _adamw_math(w, m, v, g):
    mn = ADAM_B1 * m + (1.0 - ADAM_B1) * g
    vn = ADAM_B2 * v + (1.0 - ADAM_B2) * (g * g)
    m_hat = mn / (1.0 - ADAM_B1 ** ADAM_STEP)
    v_hat = vn / (1.0 - ADAM_B2 ** ADAM_STEP)
    return -ADAM_LR * (m_hat / (jnp.sqrt(v_hat) + ADAM_EPS) + ADAM_WD * w), mn, vn


def _adamw_vectors(ws, ms, vs, gs, *, name):
    n = len(ws)

    def body(*refs):
        ins, outs = refs[:4 * n], refs[4 * n:]
        for a in range(n):
            g = ins[3 * n + a][...]
            outs[a][...] = g
            outs[n + a][...], outs[2 * n + a][...], outs[3 * n + a][...] = _adamw_math(ins[a][...], ins[n + a][...], ins[2 * n + a][...], g)

    shapes = [jax.ShapeDtypeStruct(w.shape, F32) for w in ws]
    out = pl.pallas_call(body, name=name, in_specs=[VMEM_SPEC] * (4 * n), out_specs=[VMEM_SPEC] * (4 * n),
                         out_shape=shapes * 4)(*ws, *ms, *vs, *gs)
    return out[:n], out[n:2 * n], out[2 * n:3 * n], out[3 * n:]


def _adamw(w, m, v, g_parts, *, name):
    rows, cols = w.shape
    tm = _tile(rows, ADAMW_ROWS)
    n_parts = len(g_parts)

    def body(*refs):
        w_ref, m_ref, v_ref = refs[:3]
        g_refs = refs[3:3 + n_parts]
        g_out, d_out, m_out, v_out = refs[3 + n_parts:]
        g = g_refs[0][...]
        for r in g_refs[1:]:
            g = g + r[...]
        g_out[...] = g
        d_out[...], m_out[...], v_out[...] = _adamw_math(w_ref[...], m_ref[...], v_ref[...], g)

    spec = pl.BlockSpec((tm, cols), lambda i: (i, 0))
    out = jax.ShapeDtypeStruct((rows, cols), F32)
    return pl.pallas_call(
        body, name=name, grid=(rows // tm,), in_specs=[spec] * (3 + n_parts),
        out_specs=[spec] * 4, out_shape=[out] * 4)(w, m, v, *g_parts)


def _sum_slots(parts, *, name):
    _, rows, cols = parts.shape
    tm = _tile(rows, 512)

    def body(p_ref, o_ref):
        acc = p_ref[0].astype(F32)
        for k in range(1, N_SHARD):
            acc = acc + p_ref[k].astype(F32)
        o_ref[...] = acc

    return pl.pallas_call(
        body, name=name, grid=(rows // tm,),
        in_specs=[pl.BlockSpec((N_SHARD, tm, cols), lambda i: (0, i, 0))],
        out_specs=pl.BlockSpec((tm, cols), lambda i: (i, 0)),
        out_shape=jax.ShapeDtypeStruct((rows, cols), F32))(parts)


def _mesh_pos():
    return lax.axis_index("x"), lax.axis_index("y"), lax.axis_index("c")


CHIP_FLIPS = ((1, 0), (0, 1), (1, 1))


class Exchange(NamedTuple):
    kind: str
    srcs: tuple
    lands: tuple
    layer: Any = None


HBM_SPEC = pl.BlockSpec(memory_space=pltpu.HBM)
SEM_SPEC = pl.BlockSpec(memory_space=pltpu.SEMAPHORE)
DATAFLOW = pltpu.SideEffectType.DATAFLOW_SIDE_EFFECTING


def _exchange_copies(ex, src_refs, land_refs, send_sems, recv_sems):
    x, y, c = _mesh_pos()
    mine = 2 * x + y

    def slot(ref, chip):
        return ref.at[chip] if ex.layer is None else ref.at[chip, ex.layer]

    pairs = []
    for a, (src, land) in enumerate(zip(src_refs, land_refs)):
        for k, (fx, fy) in enumerate(CHIP_FLIPS):
            px, py = x ^ fx, y ^ fy
            peer = 2 * px + py
            src_part = src if ex.kind == "gather" else src.at[peer]
            pair = a * len(CHIP_FLIPS) + k
            common = dict(src_ref=src_part, send_sem=send_sems.at[pair], recv_sem=recv_sems.at[pair],
                          device_id=(px, py, c), device_id_type=MESH)
            pairs.append((pltpu.make_async_remote_copy(dst_ref=slot(land, mine), **common),
                          pltpu.make_async_remote_copy(dst_ref=slot(land, peer), **common)))
    return pairs


def _exchange_start(exchanges, *, name):
    srcs = [s for ex in exchanges for s in ex.srcs]
    lands = [b for ex in exchanges for b in ex.lands]
    n_arr, n_ex = len(srcs) + len(lands), len(exchanges)

    def body(*refs):
        src_refs, land_refs = refs[:len(srcs)], refs[len(srcs):n_arr]
        sems, token = refs[n_arr:n_arr + 2 * n_ex], refs[-1]
        at = 0
        for e, ex in enumerate(exchanges):
            n = len(ex.srcs)
            for send, _ in _exchange_copies(ex, src_refs[at:at + n], land_refs[at:at + n], sems[2 * e], sems[2 * e + 1]):
                send.start()
            at += n
        token[...] = jnp.zeros_like(token)

    sem_shapes = [pltpu.SemaphoreType.DMA((len(ex.srcs) * len(CHIP_FLIPS),)) for ex in exchanges for _ in range(2)]
    out = pl.pallas_call(
        body, name=name,
        out_shape=sem_shapes + [pltpu.HBM(a.shape, a.dtype) for a in srcs + lands] + [jax.ShapeDtypeStruct((SUBLANES, LANES), F32)],
        in_specs=[HBM_SPEC] * n_arr, out_specs=[SEM_SPEC] * (2 * n_ex) + [HBM_SPEC] * n_arr + [VMEM_SPEC],
        input_output_aliases={i: 2 * n_ex + i for i in range(n_arr)},
        compiler_params=pltpu.CompilerParams(has_side_effects=DATAFLOW),
    )(*[pltpu.with_memory_space_constraint(a, pltpu.HBM) for a in srcs + lands])
    sems, thru = out[:2 * n_ex], out[2 * n_ex:-1]
    pending, at = [], 0
    for e, ex in enumerate(exchanges):
        n = len(ex.srcs)
        pending.append((ex._replace(srcs=tuple(thru[at:at + n]), lands=tuple(thru[len(srcs) + at:len(srcs) + at + n])),
                        sems[2 * e], sems[2 * e + 1]))
        at += n
    return pending, out[-1]


def _exchange_wait(pending, after, *, name):
    ex, send_sems, recv_sems = pending
    n = len(ex.srcs)
    after = list(after) if isinstance(after, (list, tuple)) else [after]

    def body(*refs):
        src_refs, land_refs = refs[:n], refs[n:2 * n]
        for send, arrive in _exchange_copies(ex, src_refs, land_refs, refs[2 * n], refs[2 * n + 1]):
            send.wait_send()
            arrive.wait_recv()

    arrays = list(ex.srcs) + list(ex.lands)
    out = pl.pallas_call(
        body, name=name, out_shape=[pltpu.HBM(a.shape, a.dtype) for a in arrays],
        in_specs=[HBM_SPEC] * (2 * n) + [SEM_SPEC, SEM_SPEC] + [ANY] * len(after), out_specs=[HBM_SPEC] * (2 * n),
        input_output_aliases={i: i for i in range(2 * n)},
        compiler_params=pltpu.CompilerParams(has_side_effects=DATAFLOW),
    )(*arrays, send_sems, recv_sems, *after)
    return out[n:]


def _swap_with_sibling(arrays, *, name):
    n = len(arrays)

    def body(*refs):
        ins, outs = refs[:n], refs[n:2 * n]
        send_sems, recv_sems = refs[2 * n:]
        x, y, c = _mesh_pos()
        copies = []
        for a in range(n):
            cp = pltpu.make_async_remote_copy(
                src_ref=ins[a], dst_ref=outs[a], send_sem=send_sems.at[a], recv_sem=recv_sems.at[a],
                device_id=(x, y, 1 - c), device_id_type=MESH)
            cp.start()
            copies.append(cp)
        for cp in copies:
            cp.wait()

    return pl.pallas_call(
        body, name=name, in_specs=[ANY] * n, out_specs=[ANY] * n,
        out_shape=[jax.ShapeDtypeStruct(a.shape, a.dtype) for a in arrays],
        scratch_shapes=[pltpu.SemaphoreType.DMA((n,)), pltpu.SemaphoreType.DMA((n,))])(*arrays)


N_DEV = 8


def _all_reduce_small(vec, *, after=(), name):
    rows = vec.shape[0]

    def body(v_ref, *refs):
        o_ref, land, send_sems, recv_sems = refs[len(after):]
        x, y, c = _mesh_pos()
        me = 4 * x + 2 * y + c
        land[me] = v_ref[...]
        copies = []
        for k in range(1, N_DEV):
            fx, fy, fc = (k >> 2) & 1, (k >> 1) & 1, k & 1
            px, py, pc = x ^ fx, y ^ fy, c ^ fc
            send = pltpu.make_async_remote_copy(
                src_ref=v_ref, dst_ref=land.at[me], send_sem=send_sems.at[k - 1], recv_sem=recv_sems.at[k - 1],
                device_id=(px, py, pc), device_id_type=MESH)
            send.start()
            arrive = pltpu.make_async_remote_copy(
                src_ref=v_ref, dst_ref=land.at[4 * px + 2 * py + pc], send_sem=send_sems.at[k - 1], recv_sem=recv_sems.at[k - 1],
                device_id=(px, py, pc), device_id_type=MESH)
            copies.append((send, arrive))
        for send, arrive in copies:
            send.wait_send()
            arrive.wait_recv()
        acc = land[0]
        for k in range(1, N_DEV):
            acc = acc + land[k]
        o_ref[...] = acc

    return pl.pallas_call(
        body, name=name, in_specs=[VMEM_SPEC] + [ANY] * len(after), out_specs=VMEM_SPEC,
        out_shape=jax.ShapeDtypeStruct(vec.shape, F32),
        scratch_shapes=[pltpu.VMEM((N_DEV, rows, LANES), F32), pltpu.SemaphoreType.DMA((N_DEV - 1,)),
                        pltpu.SemaphoreType.DMA((N_DEV - 1,))])(vec, *after)


PACK_UNIT = SUBLANES * LANES * 2


def _padded(n):
    return -(-n // PACK_UNIT) * PACK_UNIT


def _pack(arrays, dtype, lead=0):
    parts = []
    for a in arrays:
        lead_shape = a.shape[:lead]
        flat = a.astype(dtype).reshape(lead_shape + (-1,))
        n = flat.shape[-1]
        flat = jnp.pad(flat, [(0, 0)] * lead + [(0, _padded(n) - n)])
        parts.append(flat.reshape(lead_shape + (-1, LANES)))
    return jnp.concatenate(parts, axis=lead)


def _unpack(buf, shapes, lead=0):
    out, row = [], 0
    for shp in shapes:
        n = math.prod(shp)
        rows = _padded(n) // LANES
        part = lax.slice_in_dim(buf, row, row + rows, axis=lead)
        lead_shape = part.shape[:lead]
        part = part.reshape(lead_shape + (-1,))
        part = lax.slice_in_dim(part, 0, n, axis=lead)
        out.append(part.reshape(lead_shape + tuple(shp)))
        row += rows
    return out


def kernel(x, positions, ln_mix_a, w_pool, b_pool, pool_scale, ln_ffn, w_gate, w_up, w_down, ln_kv, w_dkv, g_kv_latent, w_uk, w_uv, g_k, ln_mix_b, w_dq, g_q_latent, w_uq, g_q, w_o, loss_target, m_ln_mix_a, m_w_pool, m_b_pool, m_pool_scale, m_ln_ffn, m_w_gate, m_w_up, m_w_down, m_ln_kv, m_w_dkv, m_g_kv_latent, m_w_uk, m_w_uv, m_g_k, m_ln_mix_b, m_w_dq, m_g_q_latent, m_w_uq, m_g_q, m_w_o, v_ln_mix_a, v_w_pool, v_b_pool, v_pool_scale, v_ln_ffn, v_w_gate, v_w_up, v_w_down, v_ln_kv, v_w_dkv, v_g_kv_latent, v_w_uk, v_w_uv, v_g_k, v_ln_mix_b, v_w_dq, v_g_q_latent, v_w_uq, v_g_q, v_w_o):
    weights = dict(ln_mix_a=ln_mix_a, w_pool=w_pool, b_pool=b_pool, pool_scale=pool_scale, ln_ffn=ln_ffn, w_gate=w_gate,
                   w_up=w_up, w_down=w_down, ln_kv=ln_kv, w_dkv=w_dkv, g_kv_latent=g_kv_latent, w_uk=w_uk, w_uv=w_uv, g_k=g_k,
                   ln_mix_b=ln_mix_b, w_dq=w_dq, g_q_latent=g_q_latent, w_uq=w_uq, g_q=g_q, w_o=w_o)
    mom_m = dict(ln_mix_a=m_ln_mix_a, w_pool=m_w_pool, b_pool=m_b_pool, pool_scale=m_pool_scale, ln_ffn=m_ln_ffn,
                 w_gate=m_w_gate, w_up=m_w_up, w_down=m_w_down, ln_kv=m_ln_kv, w_dkv=m_w_dkv, g_kv_latent=m_g_kv_latent,
                 w_uk=m_w_uk, w_uv=m_w_uv, g_k=m_g_k, ln_mix_b=m_ln_mix_b, w_dq=m_w_dq, g_q_latent=m_g_q_latent,
                 w_uq=m_w_uq, g_q=m_g_q, w_o=m_w_o)
    mom_v = dict(ln_mix_a=v_ln_mix_a, w_pool=v_w_pool, b_pool=v_b_pool, pool_scale=v_pool_scale, ln_ffn=v_ln_ffn,
                 w_gate=v_w_gate, w_up=v_w_up, w_down=v_w_down, ln_kv=v_ln_kv, w_dkv=v_w_dkv, g_kv_latent=v_g_kv_latent,
                 w_uk=v_w_uk, w_uv=v_w_uv, g_k=v_g_k, ln_mix_b=v_ln_mix_b, w_dq=v_w_dq, g_q_latent=v_g_q_latent,
                 w_uq=v_w_uq, g_q=v_g_q, w_o=v_w_o)
    order = list(weights)
    s = x.shape[1]
    d = D_MODEL
    xs = x.reshape(s, d)
    target = loss_target.reshape(s, d)
    my_chip = 2 * lax.axis_index("x") + lax.axis_index("y")

    mat_names = ("w_pool", "w_dkv", "w_uk", "w_uv", "w_dq", "w_uq", "w_o")
    vec_names = ("ln_mix_a", "b_pool", "pool_scale")
    mat_shapes = [weights[n].shape for n in mat_names]
    vec_shapes = [weights[n].shape for n in vec_names]

    def rows_of(a, lead=0):
        return a.reshape(a.shape[:lead] + (-1, a.shape[-1]))

    mats_local = tuple(rows_of(weights[n].astype(WIRE_DTYPE)) for n in mat_names)
    vecs_local = _pack([weights[n] for n in vec_names], F32)

    def landing(shard):
        return lax.dynamic_update_slice_in_dim(lax.empty((N_SHARD,) + shard.shape, shard.dtype), shard[None], my_chip, axis=0)

    def gather_of(shards):
        return Exchange("gather", tuple(shards), tuple(landing(sh) for sh in shards))

    def ffn_gathers(l):
        return [gather_of(tuple(w[l].astype(WIRE_DTYPE) for w in (w_gate, w_up))), gather_of((w_down[l].astype(WIRE_DTYPE),))]

    gathers = [gather_of(mats_local[:1] + (vecs_local,))]
    ffn_at = {}
    for l in range(DEPTH):
        if l == N_A:
            attn_at = len(gathers)
            gathers.append(gather_of(mats_local[1:]))
        ffn_at[l] = len(gathers)
        gathers += ffn_gathers(l)
    gathering, _ = _exchange_start(gathers, name="gather_start")

    inv = ROPE_THETA ** (-jnp.arange(ROPE // 2, dtype=F32) * 2.0 / ROPE)
    inv_lanes = jnp.concatenate([inv, inv, jnp.zeros((LANES - ROPE,), F32)]).reshape(1, LANES)
    cos_t, sin_t = _rope_tables(positions.reshape(s, 1).astype(F32), inv_lanes, name="rope_tables")

    g_pool, vecs_all = _exchange_wait(gathering[0], cos_t, name="gather_wait_small")
    g_lna, g_bp, g_ps = _unpack(vecs_all, vec_shapes, lead=1)
    wpool_f = g_pool.reshape((N_SHARD,) + mat_shapes[0]).transpose(1, 2, 0, 3, 4).reshape(N_A, N_GROUPS, GROUP_DIM, GROUP_DIM)
    bpool_f = g_bp.transpose(1, 2, 0, 3).reshape(N_A, 1, d)
    pscale_f = g_ps.transpose(1, 0, 2).reshape(N_A, 1, d)
    lna_f = g_lna.transpose(1, 0, 2).reshape(N_A, 1, d)

    def head_gain(g):
        return jnp.pad(g.reshape(1, QK_DIM), ((0, 0), (0, HEAD_PAD - QK_DIM)))

    ffn_w = [None] * DEPTH
    ffn_names = ("w_gate", "w_up", "w_down")
    ffn_2d = {nm: tuple(rows_of(src[nm]) for src in (weights, mom_m, mom_v)) for nm in ffn_names}
    moment_views = {0: ffn_2d["w_gate"][1:], 1: ffn_2d["w_up"][1:]}

    def ffn_fwd(xin, layer):
        hf = _rms_fwd(xin, ln_ffn[layer].reshape(1, d), n=d, name="ffn_norm")
        wg, wu = _exchange_wait(gathering[ffn_at[layer]], [hf, *moment_views.get(layer, ())], name=f"gather_wait_up_{layer}")
        a, b, u = _ffn_up(hf, wg, wu, name="ffn_up")
        (wd,) = _exchange_wait(gathering[ffn_at[layer] + 1], u, name=f"gather_wait_down_{layer}")
        ffn_w[layer] = wg, wu, wd
        return _ffn_down(u, wd, xin, name="ffn_down"), (xin, hf, a, b, u)

    saved_a, saved_b, saved_f = [], [], []
    cur = xs
    for l in range(N_A):
        dpool = _rms_pool_fwd(cur, lna_f[l], name="pool_fwd")
        x1 = _pool_mm_fwd(dpool, wpool_f[l], bpool_f[l], pscale_f[l], cur, name="pool_mm")
        saved_a.append((cur, dpool))
        cur, sf = ffn_fwd(x1, l)
        saved_f.append(sf)

    x_kv = cur
    hk = _rms_fwd(x_kv, ln_kv.reshape(1, d), n=d, name="kv_norm")
    g_dkv, g_uk, g_uv, g_dq, g_uq, g_o = (a.reshape((N_SHARD,) + shp) for a, shp in zip(
        _exchange_wait(gathering[attn_at], hk, name="gather_wait_attn"), mat_shapes[1:]))
    wdkv_f = jnp.pad(g_dkv.reshape(d, KV_LORA + ROPE), ((0, 0), (0, CKV_PAD - KV_LORA - ROPE)))
    wuk_f = g_uk.transpose(1, 0, 2).reshape(KV_LORA, N_HEADS * NOPE)
    wuv_f = g_uv.transpose(1, 0, 2).reshape(KV_LORA, N_HEADS * V_DIM)
    wdq_f = g_dq.transpose(1, 0, 2, 3).reshape(N_B, d, Q_LORA)
    wuq_f = jnp.pad(g_uq.transpose(1, 2, 0, 3).reshape(N_B, Q_LORA, N_HEADS, QK_DIM),
                    ((0, 0), (0, 0), (0, 0), (0, HEAD_PAD - QK_DIM))).reshape(N_B, Q_LORA, N_HEADS * HEAD_PAD)
    wo_f = g_o.transpose(1, 0, 2, 3).reshape(N_B, d, d)
    ckv = _mm(hk, wdkv_f, name="kv_down")
    c_lat = _rms_fwd(ckv, g_kv_latent.reshape(1, KV_LORA), n=KV_LORA, name="kv_latent_norm")
    kn_raw = _mm(c_lat, wuk_f, name="k_up")
    v_all = _mm(c_lat, wuv_f, out_dtype=MXU_DTYPE, name="v_up")
    vt_all = _mm(wuv_f.T, c_lat, tb=True, out_dtype=MXU_DTYPE, name="v_up_t")
    k_raw = _k_assemble(kn_raw, ckv, name="k_assemble")
    gk_pad = head_gain(g_k)
    k_cat = _head_norm_rope_fwd(k_raw, gk_pad, cos_t, sin_t, name="k_norm_rope")

    for j in range(N_B):
        l = N_A + j
        hq = _rms_fwd(cur, ln_mix_b[j].reshape(1, d), n=d, name="q_norm")
        cq_raw = _mm(hq, wdq_f[j], name="q_down")
        cq = _rms_fwd(cq_raw, g_q_latent[j].reshape(1, Q_LORA), n=Q_LORA, name="q_latent_norm")
        q_raw = _mm(cq, wuq_f[j], name="q_up")
        gq_pad = head_gain(g_q[j])
        q_cat = _head_norm_rope_fwd(q_raw, gq_pad, cos_t, sin_t, name="q_norm_rope")
        ot, lse = _attn_fwd(q_cat, k_cat, vt_all, name="attn_fwd")
        x1 = _mm(ot, wo_f[j], ta=True, resid=cur, name="attn_out")
        saved_b.append((cur, hq, cq_raw, cq, q_raw, gq_pad, q_cat, ot, lse))
        cur, sf = ffn_fwd(x1, l)
        saved_f.append(sf)

    dy, loss_part = _loss_head(cur, target, name="loss_head")

    ffn_landed = (lax.empty((N_SHARD, DEPTH, d, FF_SHARD), WIRE_DTYPE), lax.empty((N_SHARD, DEPTH, d, FF_SHARD), WIRE_DTYPE),
                  lax.empty((N_SHARD, DEPTH, FF_SHARD, d), WIRE_DTYPE))
    scattering = None
    grads = {}
    d_ln_ffn = [None] * DEPTH

    def own_part(full):
        return lax.dynamic_index_in_dim(full, my_chip, axis=0, keepdims=True)

    def ffn_bwd(dyv, layer):
        nonlocal ffn_landed, scattering
        xin, hf, a, b, u = saved_f[layer]
        wg, wu, wd = ffn_w[layer]
        da, db = _ffn_bwd_hidden(dyv, wd, a, b, name="ffn_bwd_hidden")
        dwd = _ffn_bwd_dwd(u, dyv, name="ffn_bwd_dwd")
        dwg, dwu = _ffn_bwd_dwgu(hf, da, db, name="ffn_bwd_dwgu")
        if scattering is not None:
            ffn_landed = _exchange_wait(scattering, dwg, name=f"scatter_wait_{layer + 1}")
        ffn_landed = tuple(lax.dynamic_update_slice(buf, own_part(g)[:, None], (my_chip, layer, 0, 0))
                           for buf, g in zip(ffn_landed, (dwg, dwu, dwd)))
        (scattering,), started = _exchange_start([Exchange("scatter", (dwg, dwu, dwd), ffn_landed, layer)],
                                                 name=f"scatter_start_{layer}")
        dhf = _ffn_bwd_dh(da, db, wg, wu, name="ffn_bwd_dh")
        dx, dg = _rms_bwd(xin, ln_ffn[layer].reshape(1, d), dhf, n=d, dx_in=dyv, after=started, name="ffn_norm_bwd")
        d_ln_ffn[layer] = dg.sum(axis=0)
        return dx

    dk_acc = dv_acc = None
    d_ln_mix_b, d_w_dq, d_g_q_latent, d_w_uq, d_g_q, d_w_o = ([None] * N_B for _ in range(6))
    dcur = dy
    for j in reversed(range(N_B)):
        l = N_A + j
        xin, hq, cq_raw, cq, q_raw, gq_pad, q_cat, ot, lse = saved_b[j]
        dx1 = ffn_bwd(dcur, l)
        do = _mm(dx1, wo_f[j], tb=True, out_dtype=MXU_DTYPE, name="attn_out_bwd")
        dot = _mm(wo_f[j], dx1, tb=True, name="attn_out_bwd_t")
        d_w_o[j] = _mm_tn(ot, dx1, at=True, name="attn_out_dw")
        delta = _attn_delta(ot, dot, name="attn_delta")
        lse_col, delta_col = lse.reshape(N_HEADS, s, 1), delta.reshape(N_HEADS, s, 1)
        dq_cat = _attn_bwd_dq(q_cat, k_cat, v_all, do, lse_col, delta_col, name="attn_bwd_dq")
        dk_acc, dv_acc = _attn_bwd_dkv(q_cat, k_cat, v_all, do, lse, delta, dk_acc, dv_acc, name="attn_bwd_dkv")
        dq_raw, dgq = _head_norm_rope_bwd(q_raw, gq_pad, cos_t, sin_t, dq_cat, name="q_norm_rope_bwd")
        d_g_q[j] = dgq.sum(axis=0)[:QK_DIM]
        dcq = _mm(dq_raw, wuq_f[j], tb=True, name="q_up_bwd")
        d_w_uq[j] = _mm_tn(cq, dq_raw, name="q_up_dw").reshape(Q_LORA, N_HEADS, HEAD_PAD)[:, :, :QK_DIM].reshape(Q_LORA, N_HEADS * QK_DIM)
        dcq_raw, dgl = _rms_bwd(cq_raw, g_q_latent[j].reshape(1, Q_LORA), dcq, n=Q_LORA, name="q_latent_norm_bwd")
        d_g_q_latent[j] = dgl.sum(axis=0)
        dhq = _mm(dcq_raw, wdq_f[j], tb=True, name="q_down_bwd")
        d_w_dq[j] = _mm_tn(hq, dcq_raw, name="q_down_dw")
        dcur, dgm = _rms_bwd(xin, ln_mix_b[j].reshape(1, d), dhq, n=d, dx_in=dx1, name="q_norm_bwd")
        d_ln_mix_b[j] = dgm.sum(axis=0)

    dk_raw, dgk = _head_norm_rope_bwd(k_raw, gk_pad, cos_t, sin_t, dk_acc, name="k_norm_rope_bwd")
    grads["g_k"] = dgk.sum(axis=0)[:QK_DIM]
    dc = _mm(dv_acc, wuv_f, tb=True, name="v_up_bwd")
    grads["w_uv"] = _mm_tn(c_lat, dv_acc, name="v_up_dw")
    dkn, dpe = _k_disassemble(dk_raw, name="k_disassemble")
    dc = _mm(dkn, wuk_f, tb=True, resid=dc, name="k_up_bwd")
    grads["w_uk"] = _mm_tn(c_lat, dkn, name="k_up_dw")
    dc_raw, dgl = _rms_bwd(ckv, g_kv_latent.reshape(1, KV_LORA), dc, n=KV_LORA, name="kv_latent_norm_bwd")
    grads["g_kv_latent"] = dgl.sum(axis=0)
    dckv = jnp.concatenate([dc_raw, dpe], axis=1)
    dhk = _mm(dckv, wdkv_f, tb=True, name="kv_down_bwd")
    grads["w_dkv"] = _mm_tn(hk, dckv, name="kv_down_dw")[:, :KV_LORA + ROPE]
    gm = {
        "w_dkv": grads["w_dkv"].reshape(N_SHARD, d // N_SHARD, KV_LORA + ROPE),
        "w_uk": grads["w_uk"].reshape(KV_LORA, N_SHARD, -1).transpose(1, 0, 2),
        "w_uv": grads["w_uv"].reshape(KV_LORA, N_SHARD, -1).transpose(1, 0, 2),
        "w_dq": jnp.stack(d_w_dq).reshape(N_B, N_SHARD, d // N_SHARD, Q_LORA).transpose(1, 0, 2, 3),
        "w_uq": jnp.stack(d_w_uq).reshape(N_B, Q_LORA, N_SHARD, -1).transpose(2, 0, 1, 3),
        "w_o": jnp.stack(d_w_o).reshape(N_B, N_SHARD, d // N_SHARD, d).transpose(1, 0, 2, 3),
    }

    def scatter_of(partials):
        zones = tuple(lax.dynamic_update_slice_in_dim(lax.empty(g.shape, WIRE_DTYPE), own_part(g), my_chip, axis=0) for g in partials)
        return Exchange("scatter", tuple(partials), zones)

    attn_partials = [rows_of(gm[n].astype(WIRE_DTYPE), lead=1) for n in mat_names[1:]]
    (attn_scatter,), started = _exchange_start([scatter_of(attn_partials)], name="scatter_start_attn")
    dcur, dg = _rms_bwd(x_kv, ln_kv.reshape(1, d), dhk, n=d, dx_in=dcur, after=started, name="kv_norm_bwd")
    grads["ln_kv"] = dg.sum(axis=0)

    d_ln_mix_a, d_w_pool, d_b_pool, d_pool_scale = ([None] * N_A for _ in range(4))
    for l in reversed(range(N_A)):
        xin, dpool = saved_a[l]
        dx1 = ffn_bwd(dcur, l)
        dd, dwp, dbp, dsp = _pool_mm_bwd(dpool, wpool_f[l], bpool_f[l], pscale_f[l], dx1, name="pool_mm_bwd")
        d_w_pool[l], d_b_pool[l], d_pool_scale[l] = dwp, dbp.sum(axis=0), dsp.sum(axis=0)
        dcur, dg = _rms_pool_bwd(xin, lna_f[l], dd, dx1, name="pool_bwd")
        d_ln_mix_a[l] = dg.sum(axis=0)
    grad_x = dcur.reshape(1, s, d)

    pool_partial = jnp.stack(d_w_pool).reshape(N_A, N_GROUPS, N_SHARD, GROUP_DIM // N_SHARD, GROUP_DIM).transpose(2, 0, 1, 3, 4)
    (pool_scatter,), _ = _exchange_start([scatter_of([rows_of(pool_partial.astype(WIRE_DTYPE), lead=1)])], name="scatter_start_small")
    out_g, out_d, out_m, out_v = {}, {}, {}, {}

    def reduce_and_update(names, views, landed, *, tag):
        chip_sums = [_sum_slots(p, name="sum_chips") for p in landed]
        sib_sums = _swap_with_sibling(chip_sums, name=f"swap_sibling_{tag}")
        deltas = []
        for nm, (w2, m2, v2), own, sib in zip(names, views, chip_sums, sib_sums):
            res = _adamw(w2, m2, v2, [own, sib], name=f"adamw_{tag}")
            shp = weights[nm].shape
            out_g[nm], out_d[nm], out_m[nm], out_v[nm] = (r.reshape(shp) for r in res)
            deltas.append(res[1])
        return deltas

    def mat_views(nm):
        return tuple(rows_of(src[nm]) for src in (weights, mom_m, mom_v))

    attn_landed = _exchange_wait(attn_scatter, dcur, name="scatter_wait_attn")
    updated = reduce_and_update(mat_names[1:], [mat_views(n) for n in mat_names[1:]], attn_landed, tag="attn")

    vec_full = {
        "ln_mix_a": jnp.stack(d_ln_mix_a), "b_pool": jnp.stack(d_b_pool).reshape(N_A, N_GROUPS, GROUP_DIM),
        "pool_scale": jnp.stack(d_pool_scale), "ln_ffn": jnp.stack(d_ln_ffn), "ln_kv": grads["ln_kv"],
        "g_kv_latent": grads["g_kv_latent"], "g_k": grads["g_k"], "ln_mix_b": jnp.stack(d_ln_mix_b),
        "g_q_latent": jnp.stack(d_g_q_latent), "g_q": jnp.stack(d_g_q),
    }
    small_names = list(vec_full)
    small_shapes = [vec_full[n].shape for n in small_names] + [(SUBLANES * LANES,)]
    small = _all_reduce_small(_pack([vec_full[n] for n in small_names] + [loss_part.reshape(-1)], F32), after=updated,
                              name="all_reduce_small")
    small_sum = _unpack(small, small_shapes)
    loss = jnp.sum(small_sum[-1])
    vec_grad = dict(zip(small_names, small_sum[:-1]))
    vec_grad["ln_mix_a"] = lax.dynamic_slice_in_dim(vec_grad["ln_mix_a"], my_chip * (d // N_SHARD), d // N_SHARD, axis=1)
    vec_grad["pool_scale"] = lax.dynamic_slice_in_dim(vec_grad["pool_scale"], my_chip * (d // N_SHARD), d // N_SHARD, axis=1)
    vec_grad["b_pool"] = lax.dynamic_slice_in_dim(vec_grad["b_pool"], my_chip * (GROUP_DIM // N_SHARD), GROUP_DIM // N_SHARD, axis=2)

    def as_rows(a):
        return a.reshape(1, -1) if a.ndim == 1 else rows_of(a)

    res = _adamw_vectors([as_rows(weights[n]) for n in small_names], [as_rows(mom_m[n]) for n in small_names],
                         [as_rows(mom_v[n]) for n in small_names],
                         [as_rows(vec_grad[n].reshape(weights[n].shape)) for n in small_names], name="adamw_vectors")
    for tgt, arrs in zip((out_g, out_d, out_m, out_v), res):
        for n, arr in zip(small_names, arrs):
            tgt[n] = arr.reshape(weights[n].shape)

    ffn_landed = _exchange_wait(scattering, updated + [res[1][0]], name="scatter_wait_0")
    pool_landed = _exchange_wait(pool_scatter, ffn_landed[0], name="scatter_wait_small")
    landed = [ffn_landed[0].reshape(N_SHARD, DEPTH * d, FF_SHARD), ffn_landed[1].reshape(N_SHARD, DEPTH * d, FF_SHARD),
              ffn_landed[2].reshape(N_SHARD, DEPTH * FF_SHARD, d), *pool_landed]
    reduce_and_update(ffn_names + mat_names[:1], [ffn_2d[n] for n in ffn_names] + [mat_views(mat_names[0])], landed, tag="ffn")

    return (loss, grad_x, *[out_g[n] for n in order], *[out_d[n] for n in order],
            *[out_m[n] for n in order], *[out_v[n] for n in order])
```

```python
import math
from typing import Any, NamedTuple

import jax
import jax.numpy as jnp
from jax import lax
from jax.experimental import pallas as pl
from jax.experimental.pallas import tpu as pltpu

F32 = jnp.float32
BF16 = jnp.bfloat16
MXU_DTYPE = BF16
WIRE_DTYPE = BF16
SAVED_DTYPE = BF16

D_MODEL = 1024
N_A = 2
N_B = 2
DEPTH = 4
POOL_WINDOWS = (2, 4, 8, 16)
N_GROUPS = 4
GROUP_DIM = 256
POOL_HALO = 16
N_HEADS = 8
NOPE = 128
ROPE = 64
QK_DIM = 192
HEAD_PAD = 256
V_DIM = 128
Q_LORA = 256
KV_LORA = 512
CKV_PAD = 640
ROPE_THETA = 10000.0
CHUNK = 64
EPS = 1e-6
N_SHARD = 4
FF_SHARD = 704
ROW_TILE = 1024
FFN_ROWS = 1024
FFN_GRAD_ROWS = 2048
LANES = 128
SUBLANES = 8
ADAM_LR, ADAM_B1, ADAM_B2, ADAM_EPS, ADAM_WD, ADAM_STEP = 0.001, 0.9, 0.999, 1e-08, 0.01, 10
MESH = pl.DeviceIdType.MESH
ANY = pl.BlockSpec(memory_space=pl.ANY)
VMEM_SPEC = pl.BlockSpec(memory_space=pltpu.VMEM)


def _tile(n, pref):
    if n <= pref:
        return n
    t = pref - pref % SUBLANES
    while n % t:
        t -= SUBLANES
    return t


def _fold8(v):
    r, n = v.shape
    return v.reshape(r // SUBLANES, SUBLANES, n).sum(axis=0)


def _dot(a, b, dims):
    return lax.dot_general(a.astype(MXU_DTYPE), b.astype(MXU_DTYPE), (dims, ((), ())),
                           preferred_element_type=F32)


def _nn(a, b):
    return _dot(a, b, ((1,), (0,)))


def _nt(a, b):
    return _dot(a, b, ((1,), (1,)))


def _tn(a, b):
    return _dot(a, b, ((0,), (0,)))


def _mm(a, b, *, ta=False, tb=False, resid=None, out_dtype=F32, name):
    assert not (ta and tb)
    m, k = (a.shape[1], a.shape[0]) if ta else a.shape
    n = b.shape[0] if tb else b.shape[1]
    tm, tn = _tile(m, ROW_TILE), _tile(n, 1024)

    def body(*refs):
        if resid is None:
            a_ref, b_ref, o_ref = refs
        else:
            a_ref, b_ref, r_ref, o_ref = refs
        acc = (_tn if ta else _nt if tb else _nn)(a_ref[...], b_ref[...])
        if resid is not None:
            acc = r_ref[...] + acc
        o_ref[...] = acc.astype(o_ref.dtype)

    in_specs = [pl.BlockSpec((k, tm), lambda i, j: (0, i)) if ta else pl.BlockSpec((tm, k), lambda i, j: (i, 0)),
                pl.BlockSpec((tn, k), lambda i, j: (j, 0)) if tb else pl.BlockSpec((k, tn), lambda i, j: (0, j))]
    args = [a, b]
    if resid is not None:
        in_specs.append(pl.BlockSpec((tm, tn), lambda i, j: (i, j)))
        args.append(resid)
    return pl.pallas_call(
        body, name=name, grid=(m // tm, n // tn), in_specs=in_specs,
        out_specs=pl.BlockSpec((tm, tn), lambda i, j: (i, j)),
        out_shape=jax.ShapeDtypeStruct((m, n), out_dtype))(*args)


def _mm_tn(a, b, *, name, at=False, out_dtype=F32):
    m = b.shape[0]
    k1 = a.shape[0] if at else a.shape[1]
    n = b.shape[1]
    tm, tn = _tile(m, ROW_TILE), _tile(n, 1024)
    nm = m // tm

    def body(a_ref, b_ref, o_ref, acc):
        i = pl.program_id(1)

        @pl.when(i == 0)
        def _():
            acc[...] = jnp.zeros_like(acc)

        acc[...] += (_nn if at else _tn)(a_ref[...], b_ref[...])

        @pl.when(i == nm - 1)
        def _():
            o_ref[...] = acc[...].astype(o_ref.dtype)

    return pl.pallas_call(
        body, name=name, grid=(n // tn, nm),
        in_specs=[pl.BlockSpec((k1, tm), lambda j, i: (0, i)) if at else pl.BlockSpec((tm, k1), lambda j, i: (i, 0)),
                  pl.BlockSpec((tm, tn), lambda j, i: (i, j))],
        out_specs=pl.BlockSpec((k1, tn), lambda j, i: (0, j)),
        out_shape=jax.ShapeDtypeStruct((k1, n), out_dtype),
        scratch_shapes=[pltpu.VMEM((k1, tn), F32)])(a, b)


def _rms_fwd(x, g, *, n, n_valid=None, name):
    out_dtype = MXU_DTYPE
    rows = x.shape[0]
    tm = _tile(rows, ROW_TILE)
    inv_n = 1.0 / (n_valid or n)

    def body(x_ref, g_ref, o_ref):
        xv = x_ref[...]
        r = lax.rsqrt(jnp.sum(xv * xv, axis=-1, keepdims=True) * inv_n + EPS)
        o_ref[...] = (xv * r * g_ref[...]).astype(o_ref.dtype)

    return pl.pallas_call(
        body, name=name, grid=(rows // tm,),
        in_specs=[pl.BlockSpec((tm, n), lambda i: (i, 0)), pl.BlockSpec((1, n), lambda i: (0, 0))],
        out_specs=pl.BlockSpec((tm, n), lambda i: (i, 0)),
        out_shape=jax.ShapeDtypeStruct((rows, n), out_dtype))(x, g)


def _rms_bwd_math(xv, gv, dyv, inv_n):
    r = lax.rsqrt(jnp.sum(xv * xv, axis=-1, keepdims=True) * inv_n + EPS)
    xh = xv * r
    gy = dyv * gv
    dx = r * (gy - xh * (jnp.sum(gy * xh, axis=-1, keepdims=True) * inv_n))
    return dx, dyv * xh


def _rms_bwd(x, g, dy, *, n, dx_in=None, after=None, name):
    rows = x.shape[0]
    tm = _tile(rows, ROW_TILE)
    inv_n = 1.0 / n

    def body(*refs):
        if after is not None:
            refs = refs[:-3] + refs[-2:]
        if dx_in is None:
            x_ref, g_ref, dy_ref, dx_ref, dg_ref = refs
        else:
            x_ref, g_ref, dy_ref, din_ref, dx_ref, dg_ref = refs
        dx, dgc = _rms_bwd_math(x_ref[...], g_ref[...], dy_ref[...], inv_n)
        if dx_in is not None:
            dx = din_ref[...] + dx
        dx_ref[...] = dx

        @pl.when(pl.program_id(0) == 0)
        def _():
            dg_ref[...] = jnp.zeros_like(dg_ref)

        dg_ref[...] += _fold8(dgc)

    row_spec = pl.BlockSpec((tm, n), lambda i: (i, 0))
    in_specs = [row_spec, pl.BlockSpec((1, n), lambda i: (0, 0)), row_spec]
    args = [x, g, dy]
    if dx_in is not None:
        in_specs.append(row_spec)
        args.append(dx_in)
    if after is not None:
        in_specs.append(ANY)
        args.append(after)
    return pl.pallas_call(
        body, name=name, grid=(rows // tm,), in_specs=in_specs,
        out_specs=[row_spec, pl.BlockSpec((SUBLANES, n), lambda i: (0, 0))],
        out_shape=[jax.ShapeDtypeStruct((rows, n), F32), jax.ShapeDtypeStruct((SUBLANES, n), F32)])(*args)


def _pool_counts(t0, tm, w):
    t = t0 + lax.broadcasted_iota(jnp.int32, (tm, 1), 0)
    return jnp.minimum(t + 1, w).astype(F32)


def _rms_pool_fwd(x, g, *, name):
    s, d = x.shape
    tm = _tile(s, 512)
    hb = tm // POOL_HALO

    def body(x_ref, halo_ref, g_ref, o_ref):
        i = pl.program_id(0)
        gv = g_ref[...]

        def norm(v):
            return v * lax.rsqrt(jnp.mean(v * v, axis=-1, keepdims=True) + EPS) * gv

        h = norm(x_ref[...])
        halo = norm(halo_ref[...]) * (i > 0).astype(F32)
        hh = jnp.concatenate([halo, h], axis=0)
        rows = tm + POOL_HALO
        for gi, w in enumerate(POOL_WINDOWS):
            cols = slice(gi * GROUP_DIM, (gi + 1) * GROUP_DIM)
            acc = hh[:, cols]
            k = 1
            while k < w:
                acc = acc + pltpu.roll(acc, k, 0)
                k *= 2
            win = acc[POOL_HALO:rows]
            o_ref[:, cols] = (win / _pool_counts(i * tm, tm, w) - h[:, cols]).astype(o_ref.dtype)

    return pl.pallas_call(
        body, name=name, grid=(s // tm,),
        in_specs=[pl.BlockSpec((tm, d), lambda i: (i, 0)),
                  pl.BlockSpec((POOL_HALO, d), lambda i: (jnp.maximum(i * hb - 1, 0), 0)),
                  pl.BlockSpec((1, d), lambda i: (0, 0))],
        out_specs=pl.BlockSpec((tm, d), lambda i: (i, 0)),
        out_shape=jax.ShapeDtypeStruct((s, d), MXU_DTYPE))(x, x, g)


def _rms_pool_bwd(x, g, dd, dx_in, *, name):
    s, d = x.shape
    tm = _tile(s, 512)
    hb = tm // POOL_HALO
    nt = s // tm

    def body(x_ref, g_ref, dd_ref, halo_ref, din_ref, dx_ref, dg_ref):
        i = pl.program_id(0)
        ddv = dd_ref[...]
        halo = halo_ref[...] * (i < nt - 1).astype(F32)
        rows = tm + POOL_HALO
        parts = []
        for gi, w in enumerate(POOL_WINDOWS):
            cols = slice(gi * GROUP_DIM, (gi + 1) * GROUP_DIM)
            acc = jnp.concatenate([ddv[:, cols] / _pool_counts(i * tm, tm, w), halo[:, cols] * (1.0 / w)], axis=0)
            k = 1
            while k < w:
                acc = acc + pltpu.roll(acc, rows - k, 0)
                k *= 2
            parts.append(acc[0:tm] - ddv[:, cols])
        dh = jnp.concatenate(parts, axis=1)
        dx, dgc = _rms_bwd_math(x_ref[...], g_ref[...], dh, 1.0 / d)
        dx_ref[...] = din_ref[...] + dx

        @pl.when(i == 0)
        def _():
            dg_ref[...] = jnp.zeros_like(dg_ref)

        dg_ref[...] += _fold8(dgc)

    row_spec = pl.BlockSpec((tm, d), lambda i: (i, 0))
    return pl.pallas_call(
        body, name=name, grid=(nt,),
        in_specs=[row_spec, pl.BlockSpec((1, d), lambda i: (0, 0)), row_spec,
                  pl.BlockSpec((POOL_HALO, d), lambda i: (jnp.minimum((i + 1) * hb, s // POOL_HALO - 1), 0)),
                  row_spec],
        out_specs=[row_spec, pl.BlockSpec((SUBLANES, d), lambda i: (0, 0))],
        out_shape=[jax.ShapeDtypeStruct((s, d), F32), jax.ShapeDtypeStruct((SUBLANES, d), F32)])(x, g, dd, dd, dx_in)


def _pool_mm_fwd(dpool, w, b, scale, x, *, name):
    s, d = x.shape
    tm = _tile(s, 512)

    def body(d_ref, w_ref, b_ref, s_ref, x_ref, o_ref):
        for gi in range(N_GROUPS):
            cols = slice(gi * GROUP_DIM, (gi + 1) * GROUP_DIM)
            y = _nn(d_ref[:, cols], w_ref[gi]) + b_ref[:, cols]
            o_ref[:, cols] = x_ref[:, cols] + y * s_ref[:, cols]

    row_spec = pl.BlockSpec((tm, d), lambda i: (i, 0))
    vec_spec = pl.BlockSpec((1, d), lambda i: (0, 0))
    return pl.pallas_call(
        body, name=name, grid=(s // tm,),
        in_specs=[row_spec, pl.BlockSpec((N_GROUPS, GROUP_DIM, GROUP_DIM), lambda i: (0, 0, 0)), vec_spec, vec_spec, row_spec],
        out_specs=row_spec, out_shape=jax.ShapeDtypeStruct((s, d), F32))(dpool, w, b, scale, x)


def _pool_mm_bwd(dpool, w, b, scale, dx, *, name):
    s, d = dx.shape
    tm = _tile(s, 512)

    def body(d_ref, w_ref, b_ref, s_ref, dx_ref, dd_ref, dw_ref, db_ref, ds_ref):
        @pl.when(pl.program_id(0) == 0)
        def _():
            dw_ref[...] = jnp.zeros_like(dw_ref)
            db_ref[...] = jnp.zeros_like(db_ref)
            ds_ref[...] = jnp.zeros_like(ds_ref)

        for gi in range(N_GROUPS):
            cols = slice(gi * GROUP_DIM, (gi + 1) * GROUP_DIM)
            dg = d_ref[:, cols]
            y = _nn(dg, w_ref[gi]) + b_ref[:, cols]
            dxg = dx_ref[:, cols]
            dy = dxg * s_ref[:, cols]
            ds_ref[:, cols] += _fold8(dxg * y)
            db_ref[:, cols] += _fold8(dy)
            dw_ref[gi] += _tn(dg, dy)
            dd_ref[:, cols] = _nt(dy, w_ref[gi])

    row_spec = pl.BlockSpec((tm, d), lambda i: (i, 0))
    vec_spec = pl.BlockSpec((1, d), lambda i: (0, 0))
    w_spec = pl.BlockSpec((N_GROUPS, GROUP_DIM, GROUP_DIM), lambda i: (0, 0, 0))
    part_spec = pl.BlockSpec((SUBLANES, d), lambda i: (0, 0))
    return pl.pallas_call(
        body, name=name, grid=(s // tm,),
        in_specs=[row_spec, w_spec, vec_spec, vec_spec, row_spec],
        out_specs=[row_spec, w_spec, part_spec, part_spec],
        out_shape=[jax.ShapeDtypeStruct((s, d), F32), jax.ShapeDtypeStruct((N_GROUPS, GROUP_DIM, GROUP_DIM), F32),
                   jax.ShapeDtypeStruct((SUBLANES, d), F32), jax.ShapeDtypeStruct((SUBLANES, d), F32)])(dpool, w, b, scale, dx)


def _sigmoid(a):
    return 0.5 * jnp.tanh(0.5 * a) + 0.5


def _ffn_up(hf, wg, wu, *, name):
    s, d = hf.shape
    tm = _tile(s, FFN_ROWS)

    def body(h_ref, wg_ref, wu_ref, a_ref, b_ref, u_ref):
        hv = h_ref[...]
        a = _nn(hv, wg_ref[...])
        b = _nn(hv, wu_ref[...])
        a_ref[...] = a.astype(a_ref.dtype)
        b_ref[...] = b.astype(b_ref.dtype)
        u_ref[...] = (a * _sigmoid(a) * b).astype(u_ref.dtype)

    w_spec = pl.BlockSpec((None, d, FF_SHARD), lambda j, i: (j, 0, 0))
    h_spec = pl.BlockSpec((None, tm, FF_SHARD), lambda j, i: (j, i, 0))
    hid = (N_SHARD, s, FF_SHARD)
    return pl.pallas_call(
        body, name=name, grid=(N_SHARD, s // tm),
        in_specs=[pl.BlockSpec((tm, d), lambda j, i: (i, 0)), w_spec, w_spec],
        out_specs=[h_spec, h_spec, h_spec],
        out_shape=[jax.ShapeDtypeStruct(hid, SAVED_DTYPE), jax.ShapeDtypeStruct(hid, SAVED_DTYPE),
                   jax.ShapeDtypeStruct(hid, MXU_DTYPE)])(hf, wg, wu)


def _ffn_down(u, wd, x, *, name):
    s, d = x.shape
    tm = _tile(s, 1024)

    def body(u_ref, w_ref, x_ref, o_ref):
        j = pl.program_id(1)

        @pl.when(j == 0)
        def _():
            o_ref[...] = x_ref[...]

        o_ref[...] += _nn(u_ref[...], w_ref[...])

    return pl.pallas_call(
        body, name=name, grid=(s // tm, N_SHARD),
        in_specs=[pl.BlockSpec((None, tm, FF_SHARD), lambda i, j: (j, i, 0)),
                  pl.BlockSpec((None, FF_SHARD, d), lambda i, j: (j, 0, 0)),
                  pl.BlockSpec((tm, d), lambda i, j: (i, 0))],
        out_specs=pl.BlockSpec((tm, d), lambda i, j: (i, 0)),
        out_shape=jax.ShapeDtypeStruct((s, d), F32))(u, wd, x)


def _ffn_bwd_hidden(dy, wd, a, b, *, name):
    s, d = dy.shape
    tm = _tile(s, FFN_ROWS)

    def body(dy_ref, w_ref, a_ref, b_ref, da_ref, db_ref):
        du = _nt(dy_ref[...], w_ref[...])
        av, bv = a_ref[...].astype(F32), b_ref[...].astype(F32)
        sg = _sigmoid(av)
        da_ref[...] = (du * bv * (sg * (1.0 + av * (1.0 - sg)))).astype(da_ref.dtype)
        db_ref[...] = (du * (av * sg)).astype(db_ref.dtype)

    h_spec = pl.BlockSpec((None, tm, FF_SHARD), lambda i, j: (j, i, 0))
    hid = jax.ShapeDtypeStruct((N_SHARD, s, FF_SHARD), MXU_DTYPE)
    return pl.pallas_call(
        body, name=name, grid=(s // tm, N_SHARD),
        in_specs=[pl.BlockSpec((tm, d), lambda i, j: (i, 0)),
                  pl.BlockSpec((None, FF_SHARD, d), lambda i, j: (j, 0, 0)), h_spec, h_spec],
        out_specs=[h_spec, h_spec], out_shape=[hid, hid])(dy, wd, a, b)


def _ffn_bwd_dwd(u, dy, *, name):
    s, d = dy.shape
    tm = _tile(s, FFN_GRAD_ROWS)
    nm = s // tm

    def body(u_ref, dy_ref, o_ref, acc):
        i = pl.program_id(1)

        @pl.when(i == 0)
        def _():
            acc[...] = jnp.zeros_like(acc)

        acc[...] += _tn(u_ref[...], dy_ref[...])

        @pl.when(i == nm - 1)
        def _():
            o_ref[...] = acc[...].astype(o_ref.dtype)

    return pl.pallas_call(
        body, name=name, grid=(N_SHARD, nm),
        in_specs=[pl.BlockSpec((None, tm, FF_SHARD), lambda j, i: (j, i, 0)), pl.BlockSpec((tm, d), lambda j, i: (i, 0))],
        out_specs=pl.BlockSpec((None, FF_SHARD, d), lambda j, i: (j, 0, 0)),
        out_shape=jax.ShapeDtypeStruct((N_SHARD, FF_SHARD, d), WIRE_DTYPE),
        scratch_shapes=[pltpu.VMEM((FF_SHARD, d), F32)])(u, dy)


def _ffn_bwd_dwgu(hf, da, db, *, name):
    s, d = hf.shape
    tm = _tile(s, FFN_GRAD_ROWS)
    nm = s // tm

    def body(h_ref, da_ref, db_ref, og_ref, ou_ref, accg, accu):
        i = pl.program_id(1)

        @pl.when(i == 0)
        def _():
            accg[...] = jnp.zeros_like(accg)
            accu[...] = jnp.zeros_like(accu)

        hv = h_ref[...]
        accg[...] += _tn(hv, da_ref[...])
        accu[...] += _tn(hv, db_ref[...])

        @pl.when(i == nm - 1)
        def _():
            og_ref[...] = accg[...].astype(og_ref.dtype)
            ou_ref[...] = accu[...].astype(ou_ref.dtype)

    h_spec = pl.BlockSpec((None, tm, FF_SHARD), lambda j, i: (j, i, 0))
    w_spec = pl.BlockSpec((None, d, FF_SHARD), lambda j, i: (j, 0, 0))
    grad = jax.ShapeDtypeStruct((N_SHARD, d, FF_SHARD), WIRE_DTYPE)
    return pl.pallas_call(
        body, name=name, grid=(N_SHARD, nm),
        in_specs=[pl.BlockSpec((tm, d), lambda j, i: (i, 0)), h_spec, h_spec],
        out_specs=[w_spec, w_spec], out_shape=[grad, grad],
        scratch_shapes=[pltpu.VMEM((d, FF_SHARD), F32), pltpu.VMEM((d, FF_SHARD), F32)])(hf, da, db)


def _ffn_bwd_dh(da, db, wg, wu, *, name):
    s = da.shape[1]
    d = wg.shape[1]
    tm = _tile(s, 1024)

    def body(da_ref, db_ref, wg_ref, wu_ref, o_ref):
        j = pl.program_id(1)

        @pl.when(j == 0)
        def _():
            o_ref[...] = jnp.zeros_like(o_ref)

        o_ref[...] += _nt(da_ref[...], wg_ref[...]) + _nt(db_ref[...], wu_ref[...])

    h_spec = pl.BlockSpec((None, tm, FF_SHARD), lambda i, j: (j, i, 0))
    w_spec = pl.BlockSpec((None, d, FF_SHARD), lambda i, j: (j, 0, 0))
    return pl.pallas_call(
        body, name=name, grid=(s // tm, N_SHARD),
        in_specs=[h_spec, h_spec, w_spec, w_spec],
        out_specs=pl.BlockSpec((tm, d), lambda i, j: (i, 0)),
        out_shape=jax.ShapeDtypeStruct((s, d), F32))(da, db, wg, wu)


def _rope_tables(pos, inv, *, name):
    s = pos.shape[0]
    tm = _tile(s, 512)
    half = ROPE // 2

    def body(p_ref, i_ref, c_ref, s_ref):
        ang = p_ref[...] * i_ref[...]
        lane = lax.broadcasted_iota(jnp.int32, ang.shape, 1)
        live = lane < ROPE
        c_ref[...] = jnp.where(live, jnp.cos(ang), 0.0)
        sn = jnp.sin(ang)
        s_ref[...] = jnp.where(live, jnp.where(lane < half, -sn, sn), 0.0)

    out = jax.ShapeDtypeStruct((s, LANES), F32)
    return pl.pallas_call(
        body, name=name, grid=(s // tm,),
        in_specs=[pl.BlockSpec((tm, 1), lambda i: (i, 0)), pl.BlockSpec((1, LANES), lambda i: (0, 0))],
        out_specs=[pl.BlockSpec((tm, LANES), lambda i: (i, 0))] * 2, out_shape=[out, out])(pos, inv)


def _swap_halves(v):
    half = ROPE // 2
    lane = lax.broadcasted_iota(jnp.int32, v.shape, 1)
    return jnp.where(lane < half, pltpu.roll(v, LANES - half, 1), pltpu.roll(v, half, 1))


def _head_norm_rope_fwd(raw, g, cos, sin, *, name):
    s = raw.shape[0]
    tm = _tile(s, 256)
    width = N_HEADS * HEAD_PAD

    def body(x_ref, g_ref, c_ref, s_ref, o_ref):
        cv, sv = c_ref[...], s_ref[...]
        for h in range(N_HEADS):
            lo = h * HEAD_PAD
            xa = x_ref[:, lo:lo + NOPE]
            xb = x_ref[:, lo + NOPE:lo + HEAD_PAD]
            ms = (jnp.sum(xa * xa, axis=-1, keepdims=True) + jnp.sum(xb * xb, axis=-1, keepdims=True)) * (1.0 / QK_DIM)
            r = lax.rsqrt(ms + EPS)
            o_ref[:, lo:lo + NOPE] = (xa * r * g_ref[:, 0:NOPE]).astype(o_ref.dtype)
            yb = xb * r * g_ref[:, NOPE:HEAD_PAD]
            o_ref[:, lo + NOPE:lo + HEAD_PAD] = (yb * cv + _swap_halves(yb) * sv).astype(o_ref.dtype)

    row_spec = pl.BlockSpec((tm, width), lambda i: (i, 0))
    tab_spec = pl.BlockSpec((tm, LANES), lambda i: (i, 0))
    return pl.pallas_call(
        body, name=name, grid=(s // tm,),
        in_specs=[row_spec, pl.BlockSpec((1, HEAD_PAD), lambda i: (0, 0)), tab_spec, tab_spec],
        out_specs=row_spec, out_shape=jax.ShapeDtypeStruct((s, width), MXU_DTYPE))(raw, g, cos, sin)


def _head_norm_rope_bwd(raw, g, cos, sin, dout, *, name):
    s = raw.shape[0]
    tm = _tile(s, 256)
    width = N_HEADS * HEAD_PAD

    def body(x_ref, g_ref, c_ref, s_ref, do_ref, dx_ref, dg_ref):
        @pl.when(pl.program_id(0) == 0)
        def _():
            dg_ref[...] = jnp.zeros_like(dg_ref)

        cv, sv = c_ref[...], s_ref[...]
        ga, gb = g_ref[:, 0:NOPE], g_ref[:, NOPE:HEAD_PAD]
        for h in range(N_HEADS):
            lo = h * HEAD_PAD
            xa = x_ref[:, lo:lo + NOPE]
            xb = x_ref[:, lo + NOPE:lo + HEAD_PAD]
            dya = do_ref[:, lo:lo + NOPE]
            dob = do_ref[:, lo + NOPE:lo + HEAD_PAD]
            dyb = dob * cv + _swap_halves(dob * sv)
            ms = (jnp.sum(xa * xa, axis=-1, keepdims=True) + jnp.sum(xb * xb, axis=-1, keepdims=True)) * (1.0 / QK_DIM)
            r = lax.rsqrt(ms + EPS)
            xha, xhb = xa * r, xb * r
            gya, gyb = dya * ga, dyb * gb
            dot = (jnp.sum(gya * xha, axis=-1, keepdims=True) + jnp.sum(gyb * xhb, axis=-1, keepdims=True)) * (1.0 / QK_DIM)
            dx_ref[:, lo:lo + NOPE] = r * (gya - xha * dot)
            dx_ref[:, lo + NOPE:lo + HEAD_PAD] = r * (gyb - xhb * dot)
            dg_ref[:, 0:NOPE] += _fold8(dya * xha)
            dg_ref[:, NOPE:HEAD_PAD] += _fold8(dyb * xhb)

    row_spec = pl.BlockSpec((tm, width), lambda i: (i, 0))
    tab_spec = pl.BlockSpec((tm, LANES), lambda i: (i, 0))
    return pl.pallas_call(
        body, name=name, grid=(s // tm,),
        in_specs=[row_spec, pl.BlockSpec((1, HEAD_PAD), lambda i: (0, 0)), tab_spec, tab_spec, row_spec],
        out_specs=[row_spec, pl.BlockSpec((SUBLANES, HEAD_PAD), lambda i: (0, 0))],
        out_shape=[jax.ShapeDtypeStruct((s, width), F32), jax.ShapeDtypeStruct((SUBLANES, HEAD_PAD), F32)])(raw, g, cos, sin, dout)


def _k_assemble(kn, ckv, *, name):
    s = kn.shape[0]
    tm = _tile(s, 512)
    width = N_HEADS * HEAD_PAD

    def body(kn_ref, pe_ref, o_ref):
        pe = pe_ref[...]
        for h in range(N_HEADS):
            o_ref[:, h * HEAD_PAD:h * HEAD_PAD + NOPE] = kn_ref[:, h * NOPE:(h + 1) * NOPE]
            o_ref[:, h * HEAD_PAD + NOPE:(h + 1) * HEAD_PAD] = pe

    return pl.pallas_call(
        body, name=name, grid=(s // tm,),
        in_specs=[pl.BlockSpec((tm, N_HEADS * NOPE), lambda i: (i, 0)),
                  pl.BlockSpec((tm, LANES), lambda i: (i, KV_LORA // LANES))],
        out_specs=pl.BlockSpec((tm, width), lambda i: (i, 0)),
        out_shape=jax.ShapeDtypeStruct((s, width), F32))(kn, ckv)


def _k_disassemble(dk_raw, *, name):
    s = dk_raw.shape[0]
    tm = _tile(s, 512)
    width = N_HEADS * HEAD_PAD

    def body(dk_ref, dkn_ref, dpe_ref):
        pe = dk_ref[:, NOPE:HEAD_PAD]
        for h in range(N_HEADS):
            dkn_ref[:, h * NOPE:(h + 1) * NOPE] = dk_ref[:, h * HEAD_PAD:h * HEAD_PAD + NOPE]
            if h:
                pe = pe + dk_ref[:, h * HEAD_PAD + NOPE:(h + 1) * HEAD_PAD]
        dpe_ref[...] = pe

    return pl.pallas_call(
        body, name=name, grid=(s // tm,),
        in_specs=[pl.BlockSpec((tm, width), lambda i: (i, 0))],
        out_specs=[pl.BlockSpec((tm, N_HEADS * NOPE), lambda i: (i, 0)), pl.BlockSpec((tm, LANES), lambda i: (i, 0))],
        out_shape=[jax.ShapeDtypeStruct((s, N_HEADS * NOPE), F32), jax.ShapeDtypeStruct((s, LANES), F32)])(dk_raw)


ATTN_SCALE = 1.0 / math.sqrt(QK_DIM)
MASKED = -1e30


ATTN_TILE = 512
ATTN_HEADS = 8


def _chunk_mask(q0, k0, shape, q_axis):
    qpos = q0 + lax.broadcasted_iota(jnp.int32, shape, q_axis)
    kpos = k0 + lax.broadcasted_iota(jnp.int32, shape, 1 - q_axis)
    return kpos // CHUNK <= qpos // CHUNK


LOG2E = math.log2(math.e)
SCORE_LOG2 = ATTN_SCALE * LOG2E


def _causal_pairs(n, by_key):
    if by_key:
        pairs = [(i, j) for j in range(n) for i in range(j, n)]
    else:
        pairs = [(i, j) for i in range(n) for j in range(i + 1)]
    return jnp.asarray([p[0] for p in pairs], jnp.int32), jnp.asarray([p[1] for p in pairs], jnp.int32)


def _attn_fwd(q, k, vt, *, name):
    s = q.shape[0]
    t = _tile(s, ATTN_TILE)
    n = s // t
    qi_tab, kj_tab = _causal_pairs(n, by_key=False)

    hg = ATTN_HEADS

    def body(qi_ref, kj_ref, q_ref, k_ref, vt_ref, o_ref, lse_ref, m_sc, l_sc, acc):
        pair = pl.program_id(1)
        qi, kj = qi_ref[pair], kj_ref[pair]

        @pl.when(kj == 0)
        def _():
            m_sc[...] = jnp.full_like(m_sc, MASKED)
            l_sc[...] = jnp.zeros_like(l_sc)
            acc[...] = jnp.zeros_like(acc)

        def step(masked):
            for g in range(hg):
                qk, vr = slice(g * HEAD_PAD, (g + 1) * HEAD_PAD), slice(g * V_DIM, (g + 1) * V_DIM)
                st = _nt(k_ref[:, qk], q_ref[:, qk])
                if masked:
                    st = jnp.where(_chunk_mask(qi * t, kj * t, (t, t), 1), st, MASKED)
                m_prev = m_sc[g]
                m_new = jnp.maximum(m_prev, jnp.max(st, axis=0, keepdims=True) * SCORE_LOG2)
                alpha = jnp.exp2(m_prev - m_new)
                pt = jnp.exp2(st * SCORE_LOG2 - m_new)
                l_new = alpha * l_sc[g] + jnp.sum(pt, axis=0, keepdims=True)
                a_new = alpha * acc[vr, :] + _nn(vt_ref[vr, :], pt)
                l_sc[g] = l_new
                acc[vr, :] = a_new
                m_sc[g] = m_new
                if masked:
                    o_ref[vr, :] = a_new / l_new
                    lse_ref[g] = m_new + jnp.log(l_new) * LOG2E

        @pl.when(kj < qi)
        def _():
            step(False)

        @pl.when(kj == qi)
        def _():
            step(True)

    return pl.pallas_call(
        body, name=name,
        grid_spec=pltpu.PrefetchScalarGridSpec(
            num_scalar_prefetch=2, grid=(N_HEADS // hg, int(qi_tab.shape[0])),
            in_specs=[pl.BlockSpec((t, hg * HEAD_PAD), lambda h, p, qi, kj: (qi[p], h)),
                      pl.BlockSpec((t, hg * HEAD_PAD), lambda h, p, qi, kj: (kj[p], h)),
                      pl.BlockSpec((hg * V_DIM, t), lambda h, p, qi, kj: (h, kj[p]))],
            out_specs=[pl.BlockSpec((hg * V_DIM, t), lambda h, p, qi, kj: (h, qi[p])),
                       pl.BlockSpec((hg, 1, t), lambda h, p, qi, kj: (h, 0, qi[p]))],
            scratch_shapes=[pltpu.VMEM((hg, 1, t), F32), pltpu.VMEM((hg, 1, t), F32), pltpu.VMEM((hg * V_DIM, t), F32)]),
        out_shape=[jax.ShapeDtypeStruct((N_HEADS * V_DIM, s), F32), jax.ShapeDtypeStruct((N_HEADS, 1, s), F32)])(qi_tab, kj_tab, q, k, vt)


def _attn_delta(ot, dot, *, name):
    s = ot.shape[1]
    t = _tile(s, 1024)

    def body(o_ref, do_ref, d_ref):
        d_ref[...] = jnp.sum(o_ref[...] * do_ref[...], axis=0, keepdims=True)

    blk = pl.BlockSpec((V_DIM, t), lambda h, i: (h, i))
    return pl.pallas_call(
        body, name=name, grid=(N_HEADS, s // t), in_specs=[blk, blk],
        out_specs=pl.BlockSpec((None, 1, t), lambda h, i: (h, 0, i)),
        out_shape=jax.ShapeDtypeStruct((N_HEADS, 1, s), F32))(ot, dot)


def _attn_bwd_dq(q, k, v, do, lse_col, delta_col, *, name):
    s = q.shape[0]
    t = _tile(s, ATTN_TILE)
    n = s // t
    qi_tab, kj_tab = _causal_pairs(n, by_key=False)

    hg = ATTN_HEADS

    def body(qi_ref, kj_ref, q_ref, k_ref, v_ref, do_ref, lse_ref, dl_ref, dq_ref, acc):
        pair = pl.program_id(1)
        qi, kj = qi_ref[pair], kj_ref[pair]

        @pl.when(kj == 0)
        def _():
            acc[...] = jnp.zeros_like(acc)

        def step(masked):
            for g in range(hg):
                qk, vc = slice(g * HEAD_PAD, (g + 1) * HEAD_PAD), slice(g * V_DIM, (g + 1) * V_DIM)
                kv = k_ref[:, qk]
                sc = _nt(q_ref[:, qk], kv)
                if masked:
                    sc = jnp.where(_chunk_mask(qi * t, kj * t, (t, t), 0), sc, MASKED)
                p = jnp.exp2(sc * SCORE_LOG2 - lse_ref[g])
                dp = _nt(do_ref[:, vc], v_ref[:, vc])
                total = acc[:, qk] + _nn(p * (dp - dl_ref[g]), kv)
                acc[:, qk] = total
                if masked:
                    dq_ref[:, qk] = total * ATTN_SCALE

        @pl.when(kj < qi)
        def _():
            step(False)

        @pl.when(kj == qi)
        def _():
            step(True)

    col = pl.BlockSpec((hg, t, 1), lambda h, p, qi, kj: (h, qi[p], 0))
    return pl.pallas_call(
        body, name=name,
        grid_spec=pltpu.PrefetchScalarGridSpec(
            num_scalar_prefetch=2, grid=(N_HEADS // hg, int(qi_tab.shape[0])),
            in_specs=[pl.BlockSpec((t, hg * HEAD_PAD), lambda h, p, qi, kj: (qi[p], h)),
                      pl.BlockSpec((t, hg * HEAD_PAD), lambda h, p, qi, kj: (kj[p], h)),
                      pl.BlockSpec((t, hg * V_DIM), lambda h, p, qi, kj: (kj[p], h)),
                      pl.BlockSpec((t, hg * V_DIM), lambda h, p, qi, kj: (qi[p], h)), col, col],
            out_specs=pl.BlockSpec((t, hg * HEAD_PAD), lambda h, p, qi, kj: (qi[p], h)),
            scratch_shapes=[pltpu.VMEM((t, hg * HEAD_PAD), F32)]),
        out_shape=jax.ShapeDtypeStruct((s, N_HEADS * HEAD_PAD), F32))(qi_tab, kj_tab, q, k, v, do, lse_col, delta_col)


def _attn_bwd_dkv(q, k, v, do, lse_row, delta_row, dk_in, dv_in, *, name):
    s = q.shape[0]
    t = _tile(s, ATTN_TILE)
    n = s // t
    has_in = dk_in is not None
    hg = ATTN_HEADS
    qi_tab, kj_tab = _causal_pairs(n, by_key=True)

    def body(qi_ref, kj_ref, *refs):
        if has_in:
            q_ref, k_ref, v_ref, do_ref, lse_ref, dl_ref, dki_ref, dvi_ref, dk_ref, dv_ref, acck, accv = refs
        else:
            q_ref, k_ref, v_ref, do_ref, lse_ref, dl_ref, dk_ref, dv_ref, acck, accv = refs
        pair = pl.program_id(1)
        qi, kj = qi_ref[pair], kj_ref[pair]

        def step(masked):
            for g in range(hg):
                qk, vc = slice(g * HEAD_PAD, (g + 1) * HEAD_PAD), slice(g * V_DIM, (g + 1) * V_DIM)
                qv, dov = q_ref[:, qk], do_ref[:, vc]
                st = _nt(k_ref[:, qk], qv)
                if masked:
                    st = jnp.where(_chunk_mask(qi * t, kj * t, (t, t), 1), st, MASKED)
                pt = jnp.exp2(st * SCORE_LOG2 - lse_ref[g])
                accv[:, vc] += _nn(pt, dov)
                dpt = _nt(v_ref[:, vc], dov)
                acck[:, qk] += _nn(pt * (dpt - dl_ref[g]), qv)

        @pl.when(qi == kj)
        def _():
            acck[...] = jnp.zeros_like(acck)
            accv[...] = jnp.zeros_like(accv)
            step(True)

        @pl.when(qi > kj)
        def _():
            step(False)

        @pl.when(qi == n - 1)
        def _():
            dk = acck[...] * ATTN_SCALE
            dv = accv[...]
            if has_in:
                dk = dki_ref[...] + dk
                dv = dvi_ref[...] + dv
            dk_ref[...] = dk
            dv_ref[...] = dv

    row = pl.BlockSpec((hg, 1, t), lambda h, p, qi, kj: (h, 0, qi[p]))
    k_spec = pl.BlockSpec((t, hg * HEAD_PAD), lambda h, p, qi, kj: (kj[p], h))
    v_spec = pl.BlockSpec((t, hg * V_DIM), lambda h, p, qi, kj: (kj[p], h))
    in_specs = [pl.BlockSpec((t, hg * HEAD_PAD), lambda h, p, qi, kj: (qi[p], h)), k_spec, v_spec,
                pl.BlockSpec((t, hg * V_DIM), lambda h, p, qi, kj: (qi[p], h)), row, row]
    args = [q, k, v, do, lse_row, delta_row]
    if has_in:
        in_specs += [k_spec, v_spec]
        args += [dk_in, dv_in]
    return pl.pallas_call(
        body, name=name,
        grid_spec=pltpu.PrefetchScalarGridSpec(
            num_scalar_prefetch=2, grid=(N_HEADS // hg, int(qi_tab.shape[0])), in_specs=in_specs, out_specs=[k_spec, v_spec],
            scratch_shapes=[pltpu.VMEM((t, hg * HEAD_PAD), F32), pltpu.VMEM((t, hg * V_DIM), F32)]),
        out_shape=[jax.ShapeDtypeStruct((s, N_HEADS * HEAD_PAD), F32), jax.ShapeDtypeStruct((s, N_HEADS * V_DIM), F32)])(qi_tab, kj_tab, *args)


def _loss_head(y, target, *, name):
    s, d = y.shape
    tm = _tile(s, 512)

    def body(y_ref, t_ref, dy_ref, l_ref):
        @pl.when(pl.program_id(0) == 0)
        def _():
            l_ref[...] = jnp.zeros_like(l_ref)

        err = y_ref[...] - t_ref[...]
        dy_ref[...] = err * (1.0 / d)
        sq = _fold8(err * err)
        part = sq[:, 0:LANES]
        for cb in range(1, d // LANES):
            part = part + sq[:, cb * LANES:(cb + 1) * LANES]
        l_ref[...] += part * (0.5 / d)

    row_spec = pl.BlockSpec((tm, d), lambda i: (i, 0))
    return pl.pallas_call(
        body, name=name, grid=(s // tm,), in_specs=[row_spec, row_spec],
        out_specs=[row_spec, pl.BlockSpec((SUBLANES, LANES), lambda i: (0, 0))],
        out_shape=[jax.ShapeDtypeStruct((s, d), F32), jax.ShapeDtypeStruct((SUBLANES, LANES), F32)])(y, target)


ADAMW_ROWS = 512


def _adamw_math(w, m, v, g):
    mn = ADAM_B1 * m + (1.0 - ADAM_B1) * g
    vn = ADAM_B2 * v + (1.0 - ADAM_B2) * (g * g)
    m_hat = mn / (1.0 - ADAM_B1 ** ADAM_STEP)
    v_hat = vn / (1.0 - ADAM_B2 ** ADAM_STEP)
    return -ADAM_LR * (m_hat / (jnp.sqrt(v_hat) + ADAM_EPS) + ADAM_WD * w), mn, vn


def _adamw_vectors(ws, ms, vs, gs, *, name):
    n = len(ws)

    def body(*refs):
        ins, outs = refs[:4 * n], refs[4 * n:]
        for a in range(n):
            g = ins[3 * n + a][...]
            outs[a][...] = g
            outs[n + a][...], outs[2 * n + a][...], outs[3 * n + a][...] = _adamw_math(ins[a][...], ins[n + a][...], ins[2 * n + a][...], g)

    shapes = [jax.ShapeDtypeStruct(w.shape, F32) for w in ws]
    out = pl.pallas_call(body, name=name, in_specs=[VMEM_SPEC] * (4 * n), out_specs=[VMEM_SPEC] * (4 * n),
                         out_shape=shapes * 4)(*ws, *ms, *vs, *gs)
    return out[:n], out[n:2 * n], out[2 * n:3 * n], out[3 * n:]


def _adamw(w, m, v, g_parts, *, name):
    rows, cols = w.shape
    tm = _tile(rows, ADAMW_ROWS)
    n_parts = len(g_parts)

    def body(*refs):
        w_ref, m_ref, v_ref = refs[:3]
        g_refs = refs[3:3 + n_parts]
        g_out, d_out, m_out, v_out = refs[3 + n_parts:]
        g = g_refs[0][...]
        for r in g_refs[1:]:
            g = g + r[...]
        g_out[...] = g
        d_out[...], m_out[...], v_out[...] = _adamw_math(w_ref[...], m_ref[...], v_ref[...], g)

    spec = pl.BlockSpec((tm, cols), lambda i: (i, 0))
    out = jax.ShapeDtypeStruct((rows, cols), F32)
    return pl.pallas_call(
        body, name=name, grid=(rows // tm,), in_specs=[spec] * (3 + n_parts),
        out_specs=[spec] * 4, out_shape=[out] * 4)(w, m, v, *g_parts)


def _sum_slots(parts, *, name):
    _, rows, cols = parts.shape
    tm = _tile(rows, 512)

    def body(p_ref, o_ref):
        acc = p_ref[0].astype(F32)
        for k in range(1, N_SHARD):
            acc = acc + p_ref[k].astype(F32)
        o_ref[...] = acc

    return pl.pallas_call(
        body, name=name, grid=(rows // tm,),
        in_specs=[pl.BlockSpec((N_SHARD, tm, cols), lambda i: (0, i, 0))],
        out_specs=pl.BlockSpec((tm, cols), lambda i: (i, 0)),
        out_shape=jax.ShapeDtypeStruct((rows, cols), F32))(parts)


def _mesh_pos():
    return lax.axis_index("x"), lax.axis_index("y"), lax.axis_index("c")


CHIP_FLIPS = ((1, 0), (0, 1), (1, 1))


class Exchange(NamedTuple):
    kind: str
    srcs: tuple
    lands: tuple
    layer: Any = None


HBM_SPEC = pl.BlockSpec(memory_space=pltpu.HBM)
SEM_SPEC = pl.BlockSpec(memory_space=pltpu.SEMAPHORE)
DATAFLOW = pltpu.SideEffectType.DATAFLOW_SIDE_EFFECTING


def _exchange_copies(ex, src_refs, land_refs, send_sems, recv_sems):
    x, y, c = _mesh_pos()
    mine = 2 * x + y

    def slot(ref, chip):
        return ref.at[chip] if ex.layer is None else ref.at[chip, ex.layer]

    pairs = []
    for a, (src, land) in enumerate(zip(src_refs, land_refs)):
        for k, (fx, fy) in enumerate(CHIP_FLIPS):
            px, py = x ^ fx, y ^ fy
            peer = 2 * px + py
            src_part = src if ex.kind == "gather" else src.at[peer]
            pair = a * len(CHIP_FLIPS) + k
            common = dict(src_ref=src_part, send_sem=send_sems.at[pair], recv_sem=recv_sems.at[pair],
                          device_id=(px, py, c), device_id_type=MESH)
            pairs.append((pltpu.make_async_remote_copy(dst_ref=slot(land, mine), **common),
                          pltpu.make_async_remote_copy(dst_ref=slot(land, peer), **common)))
    return pairs


def _exchange_start(exchanges, *, name):
    srcs = [s for ex in exchanges for s in ex.srcs]
    lands = [b for ex in exchanges for b in ex.lands]
    n_arr, n_ex = len(srcs) + len(lands), len(exchanges)

    def body(*refs):
        src_refs, land_refs = refs[:len(srcs)], refs[len(srcs):n_arr]
        sems, token = refs[n_arr:n_arr + 2 * n_ex], refs[-1]
        at = 0
        for e, ex in enumerate(exchanges):
            n = len(ex.srcs)
            for send, _ in _exchange_copies(ex, src_refs[at:at + n], land_refs[at:at + n], sems[2 * e], sems[2 * e + 1]):
                send.start()
            at += n
        token[...] = jnp.zeros_like(token)

    sem_shapes = [pltpu.SemaphoreType.DMA((len(ex.srcs) * len(CHIP_FLIPS),)) for ex in exchanges for _ in range(2)]
    out = pl.pallas_call(
        body, name=name,
        out_shape=sem_shapes + [pltpu.HBM(a.shape, a.dtype) for a in srcs + lands] + [jax.ShapeDtypeStruct((SUBLANES, LANES), F32)],
        in_specs=[HBM_SPEC] * n_arr, out_specs=[SEM_SPEC] * (2 * n_ex) + [HBM_SPEC] * n_arr + [VMEM_SPEC],
        input_output_aliases={i: 2 * n_ex + i for i in range(n_arr)},
        compiler_params=pltpu.CompilerParams(has_side_effects=DATAFLOW),
    )(*[pltpu.with_memory_space_constraint(a, pltpu.HBM) for a in srcs + lands])
    sems, thru = out[:2 * n_ex], out[2 * n_ex:-1]
    pending, at = [], 0
    for e, ex in enumerate(exchanges):
        n = len(ex.srcs)
        pending.append((ex._replace(srcs=tuple(thru[at:at + n]), lands=tuple(thru[len(srcs) + at:len(srcs) + at + n])),
                        sems[2 * e], sems[2 * e + 1]))
        at += n
    return pending, out[-1]


def _exchange_wait(pending, after, *, name):
    ex, send_sems, recv_sems = pending
    n = len(ex.srcs)
    after = list(after) if isinstance(after, (list, tuple)) else [after]

    def body(*refs):
        src_refs, land_refs = refs[:n], refs[n:2 * n]
        for send, arrive in _exchange_copies(ex, src_refs, land_refs, refs[2 * n], refs[2 * n + 1]):
            send.wait_send()
            arrive.wait_recv()

    arrays = list(ex.srcs) + list(ex.lands)
    out = pl.pallas_call(
        body, name=name, out_shape=[pltpu.HBM(a.shape, a.dtype) for a in arrays],
        in_specs=[HBM_SPEC] * (2 * n) + [SEM_SPEC, SEM_SPEC] + [ANY] * len(after), out_specs=[HBM_SPEC] * (2 * n),
        input_output_aliases={i: i for i in range(2 * n)},
        compiler_params=pltpu.CompilerParams(has_side_effects=DATAFLOW),
    )(*arrays, send_sems, recv_sems, *after)
    return out[n:]


def _swap_with_sibling(arrays, *, name):
    n = len(arrays)

    def body(*refs):
        ins, outs = refs[:n], refs[n:2 * n]
        send_sems, recv_sems = refs[2 * n:]
        x, y, c = _mesh_pos()
        copies = []
        for a in range(n):
            cp = pltpu.make_async_remote_copy(
                src_ref=ins[a], dst_ref=outs[a], send_sem=send_sems.at[a], recv_sem=recv_sems.at[a],
                device_id=(x, y, 1 - c), device_id_type=MESH)
            cp.start()
            copies.append(cp)
        for cp in copies:
            cp.wait()

    return pl.pallas_call(
        body, name=name, in_specs=[ANY] * n, out_specs=[ANY] * n,
        out_shape=[jax.ShapeDtypeStruct(a.shape, a.dtype) for a in arrays],
        scratch_shapes=[pltpu.SemaphoreType.DMA((n,)), pltpu.SemaphoreType.DMA((n,))])(*arrays)


N_DEV = 8


def _all_reduce_small(vec, *, after=(), name):
    rows = vec.shape[0]

    def body(v_ref, *refs):
        o_ref, land, send_sems, recv_sems = refs[len(after):]
        x, y, c = _mesh_pos()
        me = 4 * x + 2 * y + c
        land[me] = v_ref[...]
        copies = []
        for k in range(1, N_DEV):
            fx, fy, fc = (k >> 2) & 1, (k >> 1) & 1, k & 1
            px, py, pc = x ^ fx, y ^ fy, c ^ fc
            send = pltpu.make_async_remote_copy(
                src_ref=v_ref, dst_ref=land.at[me], send_sem=send_sems.at[k - 1], recv_sem=recv_sems.at[k - 1],
                device_id=(px, py, pc), device_id_type=MESH)
            send.start()
            arrive = pltpu.make_async_remote_copy(
                src_ref=v_ref, dst_ref=land.at[4 * px + 2 * py + pc], send_sem=send_sems.at[k - 1], recv_sem=recv_sems.at[k - 1],
                device_id=(px, py, pc), device_id_type=MESH)
            copies.append((send, arrive))
        for send, arrive in copies:
            send.wait_send()
            arrive.wait_recv()
        acc = land[0]
        for k in range(1, N_DEV):
            acc = acc + land[k]
        o_ref[...] = acc

    return pl.pallas_call(
        body, name=name, in_specs=[VMEM_SPEC] + [ANY] * len(after), out_specs=VMEM_SPEC,
        out_shape=jax.ShapeDtypeStruct(vec.shape, F32),
        scratch_shapes=[pltpu.VMEM((N_DEV, rows, LANES), F32), pltpu.SemaphoreType.DMA((N_DEV - 1,)),
                        pltpu.SemaphoreType.DMA((N_DEV - 1,))])(vec, *after)


PACK_UNIT = SUBLANES * LANES * 2


def _padded(n):
    return -(-n // PACK_UNIT) * PACK_UNIT


def _pack(arrays, dtype, lead=0):
    parts = []
    for a in arrays:
        lead_shape = a.shape[:lead]
        flat = a.astype(dtype).reshape(lead_shape + (-1,))
        n = flat.shape[-1]
        flat = jnp.pad(flat, [(0, 0)] * lead + [(0, _padded(n) - n)])
        parts.append(flat.reshape(lead_shape + (-1, LANES)))
    return jnp.concatenate(parts, axis=lead)


def _unpack(buf, shapes, lead=0):
    out, row = [], 0
    for shp in shapes:
        n = math.prod(shp)
        rows = _padded(n) // LANES
        part = lax.slice_in_dim(buf, row, row + rows, axis=lead)
        lead_shape = part.shape[:lead]
        part = part.reshape(lead_shape + (-1,))
        part = lax.slice_in_dim(part, 0, n, axis=lead)
        out.append(part.reshape(lead_shape + tuple(shp)))
        row += rows
    return out


def kernel(x, positions, ln_mix_a, w_pool, b_pool, pool_scale, ln_ffn, w_gate, w_up, w_down, ln_kv, w_dkv, g_kv_latent, w_uk, w_uv, g_k, ln_mix_b, w_dq, g_q_latent, w_uq, g_q, w_o, loss_target, m_ln_mix_a, m_w_pool, m_b_pool, m_pool_scale, m_ln_ffn, m_w_gate, m_w_up, m_w_down, m_ln_kv, m_w_dkv, m_g_kv_latent, m_w_uk, m_w_uv, m_g_k, m_ln_mix_b, m_w_dq, m_g_q_latent, m_w_uq, m_g_q, m_w_o, v_ln_mix_a, v_w_pool, v_b_pool, v_pool_scale, v_ln_ffn, v_w_gate, v_w_up, v_w_down, v_ln_kv, v_w_dkv, v_g_kv_latent, v_w_uk, v_w_uv, v_g_k, v_ln_mix_b, v_w_dq, v_g_q_latent, v_w_uq, v_g_q, v_w_o):
    weights = dict(ln_mix_a=ln_mix_a, w_pool=w_pool, b_pool=b_pool, pool_scale=pool_scale, ln_ffn=ln_ffn, w_gate=w_gate,
                   w_up=w_up, w_down=w_down, ln_kv=ln_kv, w_dkv=w_dkv, g_kv_latent=g_kv_latent, w_uk=w_uk, w_uv=w_uv, g_k=g_k,
                   ln_mix_b=ln_mix_b, w_dq=w_dq, g_q_latent=g_q_latent, w_uq=w_uq, g_q=g_q, w_o=w_o)
    mom_m = dict(ln_mix_a=m_ln_mix_a, w_pool=m_w_pool, b_pool=m_b_pool, pool_scale=m_pool_scale, ln_ffn=m_ln_ffn,
                 w_gate=m_w_gate, w_up=m_w_up, w_down=m_w_down, ln_kv=m_ln_kv, w_dkv=m_w_dkv, g_kv_latent=m_g_kv_latent,
                 w_uk=m_w_uk, w_uv=m_w_uv, g_k=m_g_k, ln_mix_b=m_ln_mix_b, w_dq=m_w_dq, g_q_latent=m_g_q_latent,
                 w_uq=m_w_uq, g_q=m_g_q, w_o=m_w_o)
    mom_v = dict(ln_mix_a=v_ln_mix_a, w_pool=v_w_pool, b_pool=v_b_pool, pool_scale=v_pool_scale, ln_ffn=v_ln_ffn,
                 w_gate=v_w_gate, w_up=v_w_up, w_down=v_w_down, ln_kv=v_ln_kv, w_dkv=v_w_dkv, g_kv_latent=v_g_kv_latent,
                 w_uk=v_w_uk, w_uv=v_w_uv, g_k=v_g_k, ln_mix_b=v_ln_mix_b, w_dq=v_w_dq, g_q_latent=v_g_q_latent,
                 w_uq=v_w_uq, g_q=v_g_q, w_o=v_w_o)
    order = list(weights)
    s = x.shape[1]
    d = D_MODEL
    xs = x.reshape(s, d)
    target = loss_target.reshape(s, d)
    my_chip = 2 * lax.axis_index("x") + lax.axis_index("y")

    mat_names = ("w_pool", "w_dkv", "w_uk", "w_uv", "w_dq", "w_uq", "w_o")
    vec_names = ("ln_mix_a", "b_pool", "pool_scale")
    mat_shapes = [weights[n].shape for n in mat_names]
    vec_shapes = [weights[n].shape for n in vec_names]

    def rows_of(a, lead=0):
        return a.reshape(a.shape[:lead] + (-1, a.shape[-1]))

    mats_local = tuple(rows_of(weights[n].astype(WIRE_DTYPE)) for n in mat_names)
    vecs_local = _pack([weights[n] for n in vec_names], F32)

    def landing(shard):
        return lax.dynamic_update_slice_in_dim(lax.empty((N_SHARD,) + shard.shape, shard.dtype), shard[None], my_chip, axis=0)

    def gather_of(shards):
        return Exchange("gather", tuple(shards), tuple(landing(sh) for sh in shards))

    def ffn_gathers(l):
        return [gather_of(tuple(w[l].astype(WIRE_DTYPE) for w in (w_gate, w_up))), gather_of((w_down[l].astype(WIRE_DTYPE),))]

    gathers = [gather_of(mats_local[:1] + (vecs_local,))]
    ffn_at = {}
    for l in range(DEPTH):
        if l == N_A:
            attn_at = len(gathers)
            gathers.append(gather_of(mats_local[1:]))
        ffn_at[l] = len(gathers)
        gathers += ffn_gathers(l)
    gathering, _ = _exchange_start(gathers, name="gather_start")

    inv = ROPE_THETA ** (-jnp.arange(ROPE // 2, dtype=F32) * 2.0 / ROPE)
    inv_lanes = jnp.concatenate([inv, inv, jnp.zeros((LANES - ROPE,), F32)]).reshape(1, LANES)
    cos_t, sin_t = _rope_tables(positions.reshape(s, 1).astype(F32), inv_lanes, name="rope_tables")

    g_pool, vecs_all = _exchange_wait(gathering[0], cos_t, name="gather_wait_small")
    g_lna, g_bp, g_ps = _unpack(vecs_all, vec_shapes, lead=1)
    wpool_f = g_pool.reshape((N_SHARD,) + mat_shapes[0]).transpose(1, 2, 0, 3, 4).reshape(N_A, N_GROUPS, GROUP_DIM, GROUP_DIM)
    bpool_f = g_bp.transpose(1, 2, 0, 3).reshape(N_A, 1, d)
    pscale_f = g_ps.transpose(1, 0, 2).reshape(N_A, 1, d)
    lna_f = g_lna.transpose(1, 0, 2).reshape(N_A, 1, d)

    def head_gain(g):
        return jnp.pad(g.reshape(1, QK_DIM), ((0, 0), (0, HEAD_PAD - QK_DIM)))

    ffn_w = [None] * DEPTH
    ffn_names = ("w_gate", "w_up", "w_down")
    ffn_2d = {nm: tuple(rows_of(src[nm]) for src in (weights, mom_m, mom_v)) for nm in ffn_names}
    moment_views = {0: ffn_2d["w_gate"][1:], 1: ffn_2d["w_up"][1:]}

    def ffn_fwd(xin, layer):
        hf = _rms_fwd(xin, ln_ffn[layer].reshape(1, d), n=d, name="ffn_norm")
        wg, wu = _exchange_wait(gathering[ffn_at[layer]], [hf, *moment_views.get(layer, ())], name=f"gather_wait_up_{layer}")
        a, b, u = _ffn_up(hf, wg, wu, name="ffn_up")
        (wd,) = _exchange_wait(gathering[ffn_at[layer] + 1], u, name=f"gather_wait_down_{layer}")
        ffn_w[layer] = wg, wu, wd
        return _ffn_down(u, wd, xin, name="ffn_down"), (xin, hf, a, b, u)

    saved_a, saved_b, saved_f = [], [], []
    cur = xs
    for l in range(N_A):
        dpool = _rms_pool_fwd(cur, lna_f[l], name="pool_fwd")
        x1 = _pool_mm_fwd(dpool, wpool_f[l], bpool_f[l], pscale_f[l], cur, name="pool_mm")
        saved_a.append((cur, dpool))
        cur, sf = ffn_fwd(x1, l)
        saved_f.append(sf)

    x_kv = cur
    hk = _rms_fwd(x_kv, ln_kv.reshape(1, d), n=d, name="kv_norm")
    g_dkv, g_uk, g_uv, g_dq, g_uq, g_o = (a.reshape((N_SHARD,) + shp) for a, shp in zip(
        _exchange_wait(gathering[attn_at], hk, name="gather_wait_attn"), mat_shapes[1:]))
    wdkv_f = jnp.pad(g_dkv.reshape(d, KV_LORA + ROPE), ((0, 0), (0, CKV_PAD - KV_LORA - ROPE)))
    wuk_f = g_uk.transpose(1, 0, 2).reshape(KV_LORA, N_HEADS * NOPE)
    wuv_f = g_uv.transpose(1, 0, 2).reshape(KV_LORA, N_HEADS * V_DIM)
    wdq_f = g_dq.transpose(1, 0, 2, 3).reshape(N_B, d, Q_LORA)
    wuq_f = jnp.pad(g_uq.transpose(1, 2, 0, 3).reshape(N_B, Q_LORA, N_HEADS, QK_DIM),
                    ((0, 0), (0, 0), (0, 0), (0, HEAD_PAD - QK_DIM))).reshape(N_B, Q_LORA, N_HEADS * HEAD_PAD)
    wo_f = g_o.transpose(1, 0, 2, 3).reshape(N_B, d, d)
    ckv = _mm(hk, wdkv_f, name="kv_down")
    c_lat = _rms_fwd(ckv, g_kv_latent.reshape(1, KV_LORA), n=KV_LORA, name="kv_latent_norm")
    kn_raw = _mm(c_lat, wuk_f, name="k_up")
    v_all = _mm(c_lat, wuv_f, out_dtype=MXU_DTYPE, name="v_up")
    vt_all = _mm(wuv_f.T, c_lat, tb=True, out_dtype=MXU_DTYPE, name="v_up_t")
    k_raw = _k_assemble(kn_raw, ckv, name="k_assemble")
    gk_pad = head_gain(g_k)
    k_cat = _head_norm_rope_fwd(k_raw, gk_pad, cos_t, sin_t, name="k_norm_rope")

    for j in range(N_B):
        l = N_A + j
        hq = _rms_fwd(cur, ln_mix_b[j].reshape(1, d), n=d, name="q_norm")
        cq_raw = _mm(hq, wdq_f[j], name="q_down")
        cq = _rms_fwd(cq_raw, g_q_latent[j].reshape(1, Q_LORA), n=Q_LORA, name="q_latent_norm")
        q_raw = _mm(cq, wuq_f[j], name="q_up")
        gq_pad = head_gain(g_q[j])
        q_cat = _head_norm_rope_fwd(q_raw, gq_pad, cos_t, sin_t, name="q_norm_rope")
        ot, lse = _attn_fwd(q_cat, k_cat, vt_all, name="attn_fwd")
        x1 = _mm(ot, wo_f[j], ta=True, resid=cur, name="attn_out")
        saved_b.append((cur, hq, cq_raw, cq, q_raw, gq_pad, q_cat, ot, lse))
        cur, sf = ffn_fwd(x1, l)
        saved_f.append(sf)

    dy, loss_part = _loss_head(cur, target, name="loss_head")

    ffn_landed = (lax.empty((N_SHARD, DEPTH, d, FF_SHARD), WIRE_DTYPE), lax.empty((N_SHARD, DEPTH, d, FF_SHARD), WIRE_DTYPE),
                  lax.empty((N_SHARD, DEPTH, FF_SHARD, d), WIRE_DTYPE))
    scattering = None
    grads = {}
    d_ln_ffn = [None] * DEPTH

    def own_part(full):
        return lax.dynamic_index_in_dim(full, my_chip, axis=0, keepdims=True)

    def ffn_bwd(dyv, layer):
        nonlocal ffn_landed, scattering
        xin, hf, a, b, u = saved_f[layer]
        wg, wu, wd = ffn_w[layer]
        da, db = _ffn_bwd_hidden(dyv, wd, a, b, name="ffn_bwd_hidden")
        dwd = _ffn_bwd_dwd(u, dyv, name="ffn_bwd_dwd")
        dwg, dwu = _ffn_bwd_dwgu(hf, da, db, name="ffn_bwd_dwgu")
        if scattering is not None:
            ffn_landed = _exchange_wait(scattering, dwg, name=f"scatter_wait_{layer + 1}")
        ffn_landed = tuple(lax.dynamic_update_slice(buf, own_part(g)[:, None], (my_chip, layer, 0, 0))
                           for buf, g in zip(ffn_landed, (dwg, dwu, dwd)))
        (scattering,), started = _exchange_start([Exchange("scatter", (dwg, dwu, dwd), ffn_landed, layer)],
                                                 name=f"scatter_start_{layer}")
        dhf = _ffn_bwd_dh(da, db, wg, wu, name="ffn_bwd_dh")
        dx, dg = _rms_bwd(xin, ln_ffn[layer].reshape(1, d), dhf, n=d, dx_in=dyv, after=started, name="ffn_norm_bwd")
        d_ln_ffn[layer] = dg.sum(axis=0)
        return dx

    dk_acc = dv_acc = None
    d_ln_mix_b, d_w_dq, d_g_q_latent, d_w_uq, d_g_q, d_w_o = ([None] * N_B for _ in range(6))
    dcur = dy
    for j in reversed(range(N_B)):
        l = N_A + j
        xin, hq, cq_raw, cq, q_raw, gq_pad, q_cat, ot, lse = saved_b[j]
        dx1 = ffn_bwd(dcur, l)
        do = _mm(dx1, wo_f[j], tb=True, out_dtype=MXU_DTYPE, name="attn_out_bwd")
        dot = _mm(wo_f[j], dx1, tb=True, name="attn_out_bwd_t")
        d_w_o[j] = _mm_tn(ot, dx1, at=True, name="attn_out_dw")
        delta = _attn_delta(ot, dot, name="attn_delta")
        lse_col, delta_col = lse.reshape(N_HEADS, s, 1), delta.reshape(N_HEADS, s, 1)
        dq_cat = _attn_bwd_dq(q_cat, k_cat, v_all, do, lse_col, delta_col, name="attn_bwd_dq")
        dk_acc, dv_acc = _attn_bwd_dkv(q_cat, k_cat, v_all, do, lse, delta, dk_acc, dv_acc, name="attn_bwd_dkv")
        dq_raw, dgq = _head_norm_rope_bwd(q_raw, gq_pad, cos_t, sin_t, dq_cat, name="q_norm_rope_bwd")
        d_g_q[j] = dgq.sum(axis=0)[:QK_DIM]
        dcq = _mm(dq_raw, wuq_f[j], tb=True, name="q_up_bwd")
        d_w_uq[j] = _mm_tn(cq, dq_raw, name="q_up_dw").reshape(Q_LORA, N_HEADS, HEAD_PAD)[:, :, :QK_DIM].reshape(Q_LORA, N_HEADS * QK_DIM)
        dcq_raw, dgl = _rms_bwd(cq_raw, g_q_latent[j].reshape(1, Q_LORA), dcq, n=Q_LORA, name="q_latent_norm_bwd")
        d_g_q_latent[j] = dgl.sum(axis=0)
        dhq = _mm(dcq_raw, wdq_f[j], tb=True, name="q_down_bwd")
        d_w_dq[j] = _mm_tn(hq, dcq_raw, name="q_down_dw")
        dcur, dgm = _rms_bwd(xin, ln_mix_b[j].reshape(1, d), dhq, n=d, dx_in=dx1, name="q_norm_bwd")
        d_ln_mix_b[j] = dgm.sum(axis=0)

    dk_raw, dgk = _head_norm_rope_bwd(k_raw, gk_pad, cos_t, sin_t, dk_acc, name="k_norm_rope_bwd")
    grads["g_k"] = dgk.sum(axis=0)[:QK_DIM]
    dc = _mm(dv_acc, wuv_f, tb=True, name="v_up_bwd")
    grads["w_uv"] = _mm_tn(c_lat, dv_acc, name="v_up_dw")
    dkn, dpe = _k_disassemble(dk_raw, name="k_disassemble")
    dc = _mm(dkn, wuk_f, tb=True, resid=dc, name="k_up_bwd")
    grads["w_uk"] = _mm_tn(c_lat, dkn, name="k_up_dw")
    dc_raw, dgl = _rms_bwd(ckv, g_kv_latent.reshape(1, KV_LORA), dc, n=KV_LORA, name="kv_latent_norm_bwd")
    grads["g_kv_latent"] = dgl.sum(axis=0)
    dckv = jnp.concatenate([dc_raw, dpe], axis=1)
    dhk = _mm(dckv, wdkv_f, tb=True, name="kv_down_bwd")
    grads["w_dkv"] = _mm_tn(hk, dckv, name="kv_down_dw")[:, :KV_LORA + ROPE]
    gm = {
        "w_dkv": grads["w_dkv"].reshape(N_SHARD, d // N_SHARD, KV_LORA + ROPE),
        "w_uk": grads["w_uk"].reshape(KV_LORA, N_SHARD, -1).transpose(1, 0, 2),
        "w_uv": grads["w_uv"].reshape(KV_LORA, N_SHARD, -1).transpose(1, 0, 2),
        "w_dq": jnp.stack(d_w_dq).reshape(N_B, N_SHARD, d // N_SHARD, Q_LORA).transpose(1, 0, 2, 3),
        "w_uq": jnp.stack(d_w_uq).reshape(N_B, Q_LORA, N_SHARD, -1).transpose(2, 0, 1, 3),
        "w_o": jnp.stack(d_w_o).reshape(N_B, N_SHARD, d // N_SHARD, d).transpose(1, 0, 2, 3),
    }

    def scatter_of(partials):
        zones = tuple(lax.dynamic_update_slice_in_dim(lax.empty(g.shape, WIRE_DTYPE), own_part(g), my_chip, axis=0) for g in partials)
        return Exchange("scatter", tuple(partials), zones)

    attn_partials = [rows_of(gm[n].astype(WIRE_DTYPE), lead=1) for n in mat_names[1:]]
    (attn_scatter,), started = _exchange_start([scatter_of(attn_partials)], name="scatter_start_attn")
    dcur, dg = _rms_bwd(x_kv, ln_kv.reshape(1, d), dhk, n=d, dx_in=dcur, after=started, name="kv_norm_bwd")
    grads["ln_kv"] = dg.sum(axis=0)

    d_ln_mix_a, d_w_pool, d_b_pool, d_pool_scale = ([None] * N_A for _ in range(4))
    for l in reversed(range(N_A)):
        xin, dpool = saved_a[l]
        dx1 = ffn_bwd(dcur, l)
        dd, dwp, dbp, dsp = _pool_mm_bwd(dpool, wpool_f[l], bpool_f[l], pscale_f[l], dx1, name="pool_mm_bwd")
        d_w_pool[l], d_b_pool[l], d_pool_scale[l] = dwp, dbp.sum(axis=0), dsp.sum(axis=0)
        dcur, dg = _rms_pool_bwd(xin, lna_f[l], dd, dx1, name="pool_bwd")
        d_ln_mix_a[l] = dg.sum(axis=0)
    grad_x = dcur.reshape(1, s, d)

    pool_partial = jnp.stack(d_w_pool).reshape(N_A, N_GROUPS, N_SHARD, GROUP_DIM // N_SHARD, GROUP_DIM).transpose(2, 0, 1, 3, 4)
    (pool_scatter,), _ = _exchange_start([scatter_of([rows_of(pool_partial.astype(WIRE_DTYPE), lead=1)])], name="scatter_start_small")
    out_g, out_d, out_m, out_v = {}, {}, {}, {}

    def reduce_and_update(names, views, landed, *, tag):
        chip_sums = [_sum_slots(p, name="sum_chips") for p in landed]
        sib_sums = _swap_with_sibling(chip_sums, name=f"swap_sibling_{tag}")
        deltas = []
        for nm, (w2, m2, v2), own, sib in zip(names, views, chip_sums, sib_sums):
            res = _adamw(w2, m2, v2, [own, sib], name=f"adamw_{tag}")
            shp = weights[nm].shape
            out_g[nm], out_d[nm], out_m[nm], out_v[nm] = (r.reshape(shp) for r in res)
            deltas.append(res[1])
        return deltas

    def mat_views(nm):
        return tuple(rows_of(src[nm]) for src in (weights, mom_m, mom_v))

    attn_landed = _exchange_wait(attn_scatter, dcur, name="scatter_wait_attn")
    updated = reduce_and_update(mat_names[1:], [mat_views(n) for n in mat_names[1:]], attn_landed, tag="attn")

    vec_full = {
        "ln_mix_a": jnp.stack(d_ln_mix_a), "b_pool": jnp.stack(d_b_pool).reshape(N_A, N_GROUPS, GROUP_DIM),
        "pool_scale": jnp.stack(d_pool_scale), "ln_ffn": jnp.stack(d_ln_ffn), "ln_kv": grads["ln_kv"],
        "g_kv_latent": grads["g_kv_latent"], "g_k": grads["g_k"], "ln_mix_b": jnp.stack(d_ln_mix_b),
        "g_q_latent": jnp.stack(d_g_q_latent), "g_q": jnp.stack(d_g_q),
    }
    small_names = list(vec_full)
    small_shapes = [vec_full[n].shape for n in small_names] + [(SUBLANES * LANES,)]
    small = _all_reduce_small(_pack([vec_full[n] for n in small_names] + [loss_part.reshape(-1)], F32), after=updated,
                              name="all_reduce_small")
    small_sum = _unpack(small, small_shapes)
    loss = jnp.sum(small_sum[-1])
    vec_grad = dict(zip(small_names, small_sum[:-1]))
    vec_grad["ln_mix_a"] = lax.dynamic_slice_in_dim(vec_grad["ln_mix_a"], my_chip * (d // N_SHARD), d // N_SHARD, axis=1)
    vec_grad["pool_scale"] = lax.dynamic_slice_in_dim(vec_grad["pool_scale"], my_chip * (d // N_SHARD), d // N_SHARD, axis=1)
    vec_grad["b_pool"] = lax.dynamic_slice_in_dim(vec_grad["b_pool"], my_chip * (GROUP_DIM // N_SHARD), GROUP_DIM // N_SHARD, axis=2)

    def as_rows(a):
        return a.reshape(1, -1) if a.ndim == 1 else rows_of(a)

    res = _adamw_vectors([as_rows(weights[n]) for n in small_names], [as_rows(mom_m[n]) for n in small_names],
                         [as_rows(mom_v[n]) for n in small_names],
                         [as_rows(vec_grad[n].reshape(weights[n].shape)) for n in small_names], name="adamw_vectors")
    for tgt, arrs in zip((out_g, out_d, out_m, out_v), res):
        for n, arr in zip(small_names, arrs):
            tgt[n] = arr.reshape(weights[n].shape)

    ffn_landed = _exchange_wait(scattering, updated + [res[1][0]], name="scatter_wait_0")
    pool_landed = _exchange_wait(pool_scatter, ffn_landed[0], name="scatter_wait_small")
    landed = [ffn_landed[0].reshape(N_SHARD, DEPTH * d, FF_SHARD), ffn_landed[1].reshape(N_SHARD, DEPTH * d, FF_SHARD),
              ffn_landed[2].reshape(N_SHARD, DEPTH * FF_SHARD, d), *pool_landed]
    reduce_and_update(ffn_names + mat_names[:1], [ffn_2d[n] for n in ffn_names] + [mat_views(mat_names[0])], landed, tag="ffn")

    return (loss, grad_x, *[out_g[n] for n in order], *[out_d[n] for n in order],
            *[out_m[n] for n in order], *[out_v[n] for n in order])
```

```python
import math
from typing import Any, NamedTuple

import jax
import jax.numpy as jnp
from jax import lax
from jax.experimental import pallas as pl
from jax.experimental.pallas import tpu as pltpu

F32 = jnp.float32
BF16 = jnp.bfloat16
MXU_DTYPE = BF16
WIRE_DTYPE = BF16
SAVED_DTYPE = BF16

D_MODEL = 1024
N_A = 2
N_B = 2
DEPTH = 4
POOL_WINDOWS = (2, 4, 8, 16)
N_GROUPS = 4
GROUP_DIM = 256
POOL_HALO = 16
N_HEADS = 8
NOPE = 128
ROPE = 64
QK_DIM = 192
HEAD_PAD = 256
V_DIM = 128
Q_LORA = 256
KV_LORA = 512
CKV_PAD = 640
ROPE_THETA = 10000.0
CHUNK = 64
EPS = 1e-6
N_SHARD = 4
FF_SHARD = 704
ROW_TILE = 1024
FFN_ROWS = 1024
FFN_GRAD_ROWS = 2048
LANES = 128
SUBLANES = 8
ADAM_LR, ADAM_B1, ADAM_B2, ADAM_EPS, ADAM_WD, ADAM_STEP = 0.001, 0.9, 0.999, 1e-08, 0.01, 10
MESH = pl.DeviceIdType.MESH
ANY = pl.BlockSpec(memory_space=pl.ANY)
VMEM_SPEC = pl.BlockSpec(memory_space=pltpu.VMEM)


def _tile(n, pref):
    if n <= pref:
        return n
    t = pref - pref % SUBLANES
    while n % t:
        t -= SUBLANES
    return t


def _fold8(v):
    r, n = v.shape
    return v.reshape(r // SUBLANES, SUBLANES, n).sum(axis=0)


def _dot(a, b, dims):
    return lax.dot_general(a.astype(MXU_DTYPE), b.astype(MXU_DTYPE), (dims, ((), ())),
                           preferred_element_type=F32)


def _nn(a, b):
    return _dot(a, b, ((1,), (0,)))


def _nt(a, b):
    return _dot(a, b, ((1,), (1,)))


def _tn(a, b):
    return _dot(a, b, ((0,), (0,)))


def _mm(a, b, *, ta=False, tb=False, resid=None, out_dtype=F32, name):
    assert not (ta and tb)
    m, k = (a.shape[1], a.shape[0]) if ta else a.shape
    n = b.shape[0] if tb else b.shape[1]
    tm, tn = _tile(m, ROW_TILE), _tile(n, 1024)

    def body(*refs):
        if resid is None:
            a_ref, b_ref, o_ref = refs
        else:
            a_ref, b_ref, r_ref, o_ref = refs
        acc = (_tn if ta else _nt if tb else _nn)(a_ref[...], b_ref[...])
        if resid is not None:
            acc = r_ref[...] + acc
        o_ref[...] = acc.astype(o_ref.dtype)

    in_specs = [pl.BlockSpec((k, tm), lambda i, j: (0, i)) if ta else pl.BlockSpec((tm, k), lambda i, j: (i, 0)),
                pl.BlockSpec((tn, k), lambda i, j: (j, 0)) if tb else pl.BlockSpec((k, tn), lambda i, j: (0, j))]
    args = [a, b]
    if resid is not None:
        in_specs.append(pl.BlockSpec((tm, tn), lambda i, j: (i, j)))
        args.append(resid)
    return pl.pallas_call(
        body, name=name, grid=(m // tm, n // tn), in_specs=in_specs,
        out_specs=pl.BlockSpec((tm, tn), lambda i, j: (i, j)),
        out_shape=jax.ShapeDtypeStruct((m, n), out_dtype))(*args)


def _mm_tn(a, b, *, name, at=False, out_dtype=F32):
    m = b.shape[0]
    k1 = a.shape[0] if at else a.shape[1]
    n = b.shape[1]
    tm, tn = _tile(m, ROW_TILE), _tile(n, 1024)
    nm = m // tm

    def body(a_ref, b_ref, o_ref, acc):
        i = pl.program_id(1)

        @pl.when(i == 0)
        def _():
            acc[...] = jnp.zeros_like(acc)

        acc[...] += (_nn if at else _tn)(a_ref[...], b_ref[...])

        @pl.when(i == nm - 1)
        def _():
            o_ref[...] = acc[...].astype(o_ref.dtype)

    return pl.pallas_call(
        body, name=name, grid=(n // tn, nm),
        in_specs=[pl.BlockSpec((k1, tm), lambda j, i: (0, i)) if at else pl.BlockSpec((tm, k1), lambda j, i: (i, 0)),
                  pl.BlockSpec((tm, tn), lambda j, i: (i, j))],
        out_specs=pl.BlockSpec((k1, tn), lambda j, i: (0, j)),
        out_shape=jax.ShapeDtypeStruct((k1, n), out_dtype),
        scratch_shapes=[pltpu.VMEM((k1, tn), F32)])(a, b)


def _rms_fwd(x, g, *, n, n_valid=None, name):
    out_dtype = MXU_DTYPE
    rows = x.shape[0]
    tm = _tile(rows, ROW_TILE)
    inv_n = 1.0 / (n_valid or n)

    def body(x_ref, g_ref, o_ref):
        xv = x_ref[...]
        r = lax.rsqrt(jnp.sum(xv * xv, axis=-1, keepdims=True) * inv_n + EPS)
        o_ref[...] = (xv * r * g_ref[...]).astype(o_ref.dtype)

    return pl.pallas_call(
        body, name=name, grid=(rows // tm,),
        in_specs=[pl.BlockSpec((tm, n), lambda i: (i, 0)), pl.BlockSpec((1, n), lambda i: (0, 0))],
        out_specs=pl.BlockSpec((tm, n), lambda i: (i, 0)),
        out_shape=jax.ShapeDtypeStruct((rows, n), out_dtype))(x, g)


def _rms_bwd_math(xv, gv, dyv, inv_n):
    r = lax.rsqrt(jnp.sum(xv * xv, axis=-1, keepdims=True) * inv_n + EPS)
    xh = xv * r
    gy = dyv * gv
    dx = r * (gy - xh * (jnp.sum(gy * xh, axis=-1, keepdims=True) * inv_n))
    return dx, dyv * xh


def _rms_bwd(x, g, dy, *, n, dx_in=None, after=None, name):
    rows = x.shape[0]
    tm = _tile(rows, ROW_TILE)
    inv_n = 1.0 / n

    def body(*refs):
        if after is not None:
            refs = refs[:-3] + refs[-2:]
        if dx_in is None:
            x_ref, g_ref, dy_ref, dx_ref, dg_ref = refs
        else:
            x_ref, g_ref, dy_ref, din_ref, dx_ref, dg_ref = refs
        dx, dgc = _rms_bwd_math(x_ref[...], g_ref[...], dy_ref[...], inv_n)
        if dx_in is not None:
            dx = din_ref[...] + dx
        dx_ref[...] = dx

        @pl.when(pl.program_id(0) == 0)
        def _():
            dg_ref[...] = jnp.zeros_like(dg_ref)

        dg_ref[...] += _fold8(dgc)

    row_spec = pl.BlockSpec((tm, n), lambda i: (i, 0))
    in_specs = [row_spec, pl.BlockSpec((1, n), lambda i: (0, 0)), row_spec]
    args = [x, g, dy]
    if dx_in is not None:
        in_specs.append(row_spec)
        args.append(dx_in)
    if after is not None:
        in_specs.append(ANY)
        args.append(after)
    return pl.pallas_call(
        body, name=name, grid=(rows // tm,), in_specs=in_specs,
        out_specs=[row_spec, pl.BlockSpec((SUBLANES, n), lambda i: (0, 0))],
        out_shape=[jax.ShapeDtypeStruct((rows, n), F32), jax.ShapeDtypeStruct((SUBLANES, n), F32)])(*args)


def _pool_counts(t0, tm, w):
    t = t0 + lax.broadcasted_iota(jnp.int32, (tm, 1), 0)
    return jnp.minimum(t + 1, w).astype(F32)


def _rms_pool_fwd(x, g, *, name):
    s, d = x.shape
    tm = _tile(s, 512)
    hb = tm // POOL_HALO

    def body(x_ref, halo_ref, g_ref, o_ref):
        i = pl.program_id(0)
        gv = g_ref[...]

        def norm(v):
            return v * lax.rsqrt(jnp.mean(v * v, axis=-1, keepdims=True) + EPS) * gv

        h = norm(x_ref[...])
        halo = norm(halo_ref[...]) * (i > 0).astype(F32)
        hh = jnp.concatenate([halo, h], axis=0)
        rows = tm + POOL_HALO
        for gi, w in enumerate(POOL_WINDOWS):
            cols = slice(gi * GROUP_DIM, (gi + 1) * GROUP_DIM)
            acc = hh[:, cols]
            k = 1
            while k < w:
                acc = acc + pltpu.roll(acc, k, 0)
                k *= 2
            win = acc[POOL_HALO:rows]
            o_ref[:, cols] = (win / _pool_counts(i * tm, tm, w) - h[:, cols]).astype(o_ref.dtype)

    return pl.pallas_call(
        body, name=name, grid=(s // tm,),
        in_specs=[pl.BlockSpec((tm, d), lambda i: (i, 0)),
                  pl.BlockSpec((POOL_HALO, d), lambda i: (jnp.maximum(i * hb - 1, 0), 0)),
                  pl.BlockSpec((1, d), lambda i: (0, 0))],
        out_specs=pl.BlockSpec((tm, d), lambda i: (i, 0)),
        out_shape=jax.ShapeDtypeStruct((s, d), MXU_DTYPE))(x, x, g)


def _rms_pool_bwd(x, g, dd, dx_in, *, name):
    s, d = x.shape
    tm = _tile(s, 512)
    hb = tm // POOL_HALO
    nt = s // tm

    def body(x_ref, g_ref, dd_ref, halo_ref, din_ref, dx_ref, dg_ref):
        i = pl.program_id(0)
        ddv = dd_ref[...]
        halo = halo_ref[...] * (i < nt - 1).astype(F32)
        rows = tm + POOL_HALO
        parts = []
        for gi, w in enumerate(POOL_WINDOWS):
            cols = slice(gi * GROUP_DIM, (gi + 1) * GROUP_DIM)
            acc = jnp.concatenate([ddv[:, cols] / _pool_counts(i * tm, tm, w), halo[:, cols] * (1.0 / w)], axis=0)
            k = 1
            while k < w:
                acc = acc + pltpu.roll(acc, rows - k, 0)
                k *= 2
            parts.append(acc[0:tm] - ddv[:, cols])
        dh = jnp.concatenate(parts, axis=1)
        dx, dgc = _rms_bwd_math(x_ref[...], g_ref[...], dh, 1.0 / d)
        dx_ref[...] = din_ref[...] + dx

        @pl.when(i == 0)
        def _():
            dg_ref[...] = jnp.zeros_like(dg_ref)

        dg_ref[...] += _fold8(dgc)

    row_spec = pl.BlockSpec((tm, d), lambda i: (i, 0))
    return pl.pallas_call(
        body, name=name, grid=(nt,),
        in_specs=[row_spec, pl.BlockSpec((1, d), lambda i: (0, 0)), row_spec,
                  pl.BlockSpec((POOL_HALO, d), lambda i: (jnp.minimum((i + 1) * hb, s // POOL_HALO - 1), 0)),
                  row_spec],
        out_specs=[row_spec, pl.BlockSpec((SUBLANES, d), lambda i: (0, 0))],
        out_shape=[jax.ShapeDtypeStruct((s, d), F32), jax.ShapeDtypeStruct((SUBLANES, d), F32)])(x, g, dd, dd, dx_in)


def _pool_mm_fwd(dpool, w, b, scale, x, *, name):
    s, d = x.shape
    tm = _tile(s, 512)

    def body(d_ref, w_ref, b_ref, s_ref, x_ref, o_ref):
        for gi in range(N_GROUPS):
            cols = slice(gi * GROUP_DIM, (gi + 1) * GROUP_DIM)
            y = _nn(d_ref[:, cols], w_ref[gi]) + b_ref[:, cols]
            o_ref[:, cols] = x_ref[:, cols] + y * s_ref[:, cols]

    row_spec = pl.BlockSpec((tm, d), lambda i: (i, 0))
    vec_spec = pl.BlockSpec((1, d), lambda i: (0, 0))
    return pl.pallas_call(
        body, name=name, grid=(s // tm,),
        in_specs=[row_spec, pl.BlockSpec((N_GROUPS, GROUP_DIM, GROUP_DIM), lambda i: (0, 0, 0)), vec_spec, vec_spec, row_spec],
        out_specs=row_spec, out_shape=jax.ShapeDtypeStruct((s, d), F32))(dpool, w, b, scale, x)


def _pool_mm_bwd(dpool, w, b, scale, dx, *, name):
    s, d = dx.shape
    tm = _tile(s, 512)

    def body(d_ref, w_ref, b_ref, s_ref, dx_ref, dd_ref, dw_ref, db_ref, ds_ref):
        @pl.when(pl.program_id(0) == 0)
        def _():
            dw_ref[...] = jnp.zeros_like(dw_ref)
            db_ref[...] = jnp.zeros_like(db_ref)
            ds_ref[...] = jnp.zeros_like(ds_ref)

        for gi in range(N_GROUPS):
            cols = slice(gi * GROUP_DIM, (gi + 1) * GROUP_DIM)
            dg = d_ref[:, cols]
            y = _nn(dg, w_ref[gi]) + b_ref[:, cols]
            dxg = dx_ref[:, cols]
            dy = dxg * s_ref[:, cols]
            ds_ref[:, cols] += _fold8(dxg * y)
            db_ref[:, cols] += _fold8(dy)
            dw_ref[gi] += _tn(dg, dy)
            dd_ref[:, cols] = _nt(dy, w_ref[gi])

    row_spec = pl.BlockSpec((tm, d), lambda i: (i, 0))
    vec_spec = pl.BlockSpec((1, d), lambda i: (0, 0))
    w_spec = pl.BlockSpec((N_GROUPS, GROUP_DIM, GROUP_DIM), lambda i: (0, 0, 0))
    part_spec = pl.BlockSpec((SUBLANES, d), lambda i: (0, 0))
    return pl.pallas_call(
        body, name=name, grid=(s // tm,),
        in_specs=[row_spec, w_spec, vec_spec, vec_spec, row_spec],
        out_specs=[row_spec, w_spec, part_spec, part_spec],
        out_shape=[jax.ShapeDtypeStruct((s, d), F32), jax.ShapeDtypeStruct((N_GROUPS, GROUP_DIM, GROUP_DIM), F32),
                   jax.ShapeDtypeStruct((SUBLANES, d), F32), jax.ShapeDtypeStruct((SUBLANES, d), F32)])(dpool, w, b, scale, dx)


def _sigmoid(a):
    return 0.5 * jnp.tanh(0.5 * a) + 0.5


def _ffn_up(hf, wg, wu, *, name):
    s, d = hf.shape
    tm = _tile(s, FFN_ROWS)

    def body(h_ref, wg_ref, wu_ref, a_ref, b_ref, u_ref):
        hv = h_ref[...]
        a = _nn(hv, wg_ref[...])
        b = _nn(hv, wu_ref[...])
        a_ref[...] = a.astype(a_ref.dtype)
        b_ref[...] = b.astype(b_ref.dtype)
        u_ref[...] = (a * _sigmoid(a) * b).astype(u_ref.dtype)

    w_spec = pl.BlockSpec((None, d, FF_SHARD), lambda j, i: (j, 0, 0))
    h_spec = pl.BlockSpec((None, tm, FF_SHARD), lambda j, i: (j, i, 0))
    hid = (N_SHARD, s, FF_SHARD)
    return pl.pallas_call(
        body, name=name, grid=(N_SHARD, s // tm),
        in_specs=[pl.BlockSpec((tm, d), lambda j, i: (i, 0)), w_spec, w_spec],
        out_specs=[h_spec, h_spec, h_spec],
        out_shape=[jax.ShapeDtypeStruct(hid, SAVED_DTYPE), jax.ShapeDtypeStruct(hid, SAVED_DTYPE),
                   jax.ShapeDtypeStruct(hid, MXU_DTYPE)])(hf, wg, wu)


def _ffn_down(u, wd, x, *, name):
    s, d = x.shape
    tm = _tile(s, 1024)

    def body(u_ref, w_ref, x_ref, o_ref):
        j = pl.program_id(1)

        @pl.when(j == 0)
        def _():
            o_ref[...] = x_ref[...]

        o_ref[...] += _nn(u_ref[...], w_ref[...])

    return pl.pallas_call(
        body, name=name, grid=(s // tm, N_SHARD),
        in_specs=[pl.BlockSpec((None, tm, FF_SHARD), lambda i, j: (j, i, 0)),
                  pl.BlockSpec((None, FF_SHARD, d), lambda i, j: (j, 0, 0)),
                  pl.BlockSpec((tm, d), lambda i, j: (i, 0))],
        out_specs=pl.BlockSpec((tm, d), lambda i, j: (i, 0)),
        out_shape=jax.ShapeDtypeStruct((s, d), F32))(u, wd, x)


def _ffn_bwd_hidden(dy, wd, a, b, *, name):
    s, d = dy.shape
    tm = _tile(s, FFN_ROWS)

    def body(dy_ref, w_ref, a_ref, b_ref, da_ref, db_ref):
        du = _nt(dy_ref[...], w_ref[...])
        av, bv = a_ref[...].astype(F32), b_ref[...].astype(F32)
        sg = _sigmoid(av)
        da_ref[...] = (du * bv * (sg * (1.0 + av * (1.0 - sg)))).astype(da_ref.dtype)
        db_ref[...] = (du * (av * sg)).astype(db_ref.dtype)

    h_spec = pl.BlockSpec((None, tm, FF_SHARD), lambda i, j: (j, i, 0))
    hid = jax.ShapeDtypeStruct((N_SHARD, s, FF_SHARD), MXU_DTYPE)
    return pl.pallas_call(
        body, name=name, grid=(s // tm, N_SHARD),
        in_specs=[pl.BlockSpec((tm, d), lambda i, j: (i, 0)),
                  pl.BlockSpec((None, FF_SHARD, d), lambda i, j: (j, 0, 0)), h_spec, h_spec],
        out_specs=[h_spec, h_spec], out_shape=[hid, hid])(dy, wd, a, b)


def _ffn_bwd_dwd(u, dy, *, name):
    s, d = dy.shape
    tm = _tile(s, FFN_GRAD_ROWS)
    nm = s // tm

    def body(u_ref, dy_ref, o_ref, acc):
        i = pl.program_id(1)

        @pl.when(i == 0)
        def _():
            acc[...] = jnp.zeros_like(acc)

        acc[...] += _tn(u_ref[...], dy_ref[...])

        @pl.when(i == nm - 1)
        def _():
            o_ref[...] = acc[...].astype(o_ref.dtype)

    return pl.pallas_call(
        body, name=name, grid=(N_SHARD, nm),
        in_specs=[pl.BlockSpec((None, tm, FF_SHARD), lambda j, i: (j, i, 0)), pl.BlockSpec((tm, d), lambda j, i: (i, 0))],
        out_specs=pl.BlockSpec((None, FF_SHARD, d), lambda j, i: (j, 0, 0)),
        out_shape=jax.ShapeDtypeStruct((N_SHARD, FF_SHARD, d), WIRE_DTYPE),
        scratch_shapes=[pltpu.VMEM((FF_SHARD, d), F32)])(u, dy)


def _ffn_bwd_dwgu(hf, da, db, *, name):
    s, d = hf.shape
    tm = _tile(s, FFN_GRAD_ROWS)
    nm = s // tm

    def body(h_ref, da_ref, db_ref, og_ref, ou_ref, accg, accu):
        i = pl.program_id(1)

        @pl.when(i == 0)
        def _():
            accg[...] = jnp.zeros_like(accg)
            accu[...] = jnp.zeros_like(accu)

        hv = h_ref[...]
        accg[...] += _tn(hv, da_ref[...])
        accu[...] += _tn(hv, db_ref[...])

        @pl.when(i == nm - 1)
        def _():
            og_ref[...] = accg[...].astype(og_ref.dtype)
            ou_ref[...] = accu[...].astype(ou_ref.dtype)

    h_spec = pl.BlockSpec((None, tm, FF_SHARD), lambda j, i: (j, i, 0))
    w_spec = pl.BlockSpec((None, d, FF_SHARD), lambda j, i: (j, 0, 0))
    grad = jax.ShapeDtypeStruct((N_SHARD, d, FF_SHARD), WIRE_DTYPE)
    return pl.pallas_call(
        body, name=name, grid=(N_SHARD, nm),
        in_specs=[pl.BlockSpec((tm, d), lambda j, i: (i, 0)), h_spec, h_spec],
        out_specs=[w_spec, w_spec], out_shape=[grad, grad],
        scratch_shapes=[pltpu.VMEM((d, FF_SHARD), F32), pltpu.VMEM((d, FF_SHARD), F32)])(hf, da, db)


def _ffn_bwd_dh(da, db, wg, wu, *, name):
    s = da.shape[1]
    d = wg.shape[1]
    tm = _tile(s, 1024)

    def body(da_ref, db_ref, wg_ref, wu_ref, o_ref):
        j = pl.program_id(1)

        @pl.when(j == 0)
        def _():
            o_ref[...] = jnp.zeros_like(o_ref)

        o_ref[...] += _nt(da_ref[...], wg_ref[...]) + _nt(db_ref[...], wu_ref[...])

    h_spec = pl.BlockSpec((None, tm, FF_SHARD), lambda i, j: (j, i, 0))
    w_spec = pl.BlockSpec((None, d, FF_SHARD), lambda i, j: (j, 0, 0))
    return pl.pallas_call(
        body, name=name, grid=(s // tm, N_SHARD),
        in_specs=[h_spec, h_spec, w_spec, w_spec],
        out_specs=pl.BlockSpec((tm, d), lambda i, j: (i, 0)),
        out_shape=jax.ShapeDtypeStruct((s, d), F32))(da, db, wg, wu)


def _rope_tables(pos, inv, *, name):
    s = pos.shape[0]
    tm = _tile(s, 512)
    half = ROPE // 2

    def body(p_ref, i_ref, c_ref, s_ref):
        ang = p_ref[...] * i_ref[...]
        lane = lax.broadcasted_iota(jnp.int32, ang.shape, 1)
        live = lane < ROPE
        c_ref[...] = jnp.where(live, jnp.cos(ang), 0.0)
        sn = jnp.sin(ang)
        s_ref[...] = jnp.where(live, jnp.where(lane < half, -sn, sn), 0.0)

    out = jax.ShapeDtypeStruct((s, LANES), F32)
    return pl.pallas_call(
        body, name=name, grid=(s // tm,),
        in_specs=[pl.BlockSpec((tm, 1), lambda i: (i, 0)), pl.BlockSpec((1, LANES), lambda i: (0, 0))],
        out_specs=[pl.BlockSpec((tm, LANES), lambda i: (i, 0))] * 2, out_shape=[out, out])(pos, inv)


def _swap_halves(v):
    half = ROPE // 2
    lane = lax.broadcasted_iota(jnp.int32, v.shape, 1)
    return jnp.where(lane < half, pltpu.roll(v, LANES - half, 1), pltpu.roll(v, half, 1))


def _head_norm_rope_fwd(raw, g, cos, sin, *, name):
    s = raw.shape[0]
    tm = _tile(s, 512)
    width = N_HEADS * HEAD_PAD

    def body(x_ref, g_ref, c_ref, s_ref, o_ref):
        cv, sv = c_ref[...], s_ref[...]
        for h in range(N_HEADS):
            lo = h * HEAD_PAD
            xa = x_ref[:, lo:lo + NOPE]
            xb = x_ref[:, lo + NOPE:lo + HEAD_PAD]
            ms = (jnp.sum(xa * xa, axis=-1, keepdims=True) + jnp.sum(xb * xb, axis=-1, keepdims=True)) * (1.0 / QK_DIM)
            r = lax.rsqrt(ms + EPS)
            o_ref[:, lo:lo + NOPE] = (xa * r * g_ref[:, 0:NOPE]).astype(o_ref.dtype)
            yb = xb * r * g_ref[:, NOPE:HEAD_PAD]
            o_ref[:, lo + NOPE:lo + HEAD_PAD] = (yb * cv + _swap_halves(yb) * sv).astype(o_ref.dtype)

    row_spec = pl.BlockSpec((tm, width), lambda i: (i, 0))
    tab_spec = pl.BlockSpec((tm, LANES), lambda i: (i, 0))
    return pl.pallas_call(
        body, name=name, grid=(s // tm,),
        in_specs=[row_spec, pl.BlockSpec((1, HEAD_PAD), lambda i: (0, 0)), tab_spec, tab_spec],
        out_specs=row_spec, out_shape=jax.ShapeDtypeStruct((s, width), MXU_DTYPE))(raw, g, cos, sin)


def _head_norm_rope_bwd(raw, g, cos, sin, dout, *, name):
    s = raw.shape[0]
    tm = _tile(s, 512)
    width = N_HEADS * HEAD_PAD

    def body(x_ref, g_ref, c_ref, s_ref, do_ref, dx_ref, dg_ref):
        @pl.when(pl.program_id(0) == 0)
        def _():
            dg_ref[...] = jnp.zeros_like(dg_ref)

        cv, sv = c_ref[...], s_ref[...]
        ga, gb = g_ref[:, 0:NOPE], g_ref[:, NOPE:HEAD_PAD]
        for h in range(N_HEADS):
            lo = h * HEAD_PAD
            xa = x_ref[:, lo:lo + NOPE]
            xb = x_ref[:, lo + NOPE:lo + HEAD_PAD]
            dya = do_ref[:, lo:lo + NOPE]
            dob = do_ref[:, lo + NOPE:lo + HEAD_PAD]
            dyb = dob * cv + _swap_halves(dob * sv)
            ms = (jnp.sum(xa * xa, axis=-1, keepdims=True) + jnp.sum(xb * xb, axis=-1, keepdims=True)) * (1.0 / QK_DIM)
            r = lax.rsqrt(ms + EPS)
            xha, xhb = xa * r, xb * r
            gya, gyb = dya * ga, dyb * gb
            dot = (jnp.sum(gya * xha, axis=-1, keepdims=True) + jnp.sum(gyb * xhb, axis=-1, keepdims=True)) * (1.0 / QK_DIM)
            dx_ref[:, lo:lo + NOPE] = r * (gya - xha * dot)
            dx_ref[:, lo + NOPE:lo + HEAD_PAD] = r * (gyb - xhb * dot)
            dg_ref[:, 0:NOPE] += _fold8(dya * xha)
            dg_ref[:, NOPE:HEAD_PAD] += _fold8(dyb * xhb)

    row_spec = pl.BlockSpec((tm, width), lambda i: (i, 0))
    tab_spec = pl.BlockSpec((tm, LANES), lambda i: (i, 0))
    return pl.pallas_call(
        body, name=name, grid=(s // tm,),
        in_specs=[row_spec, pl.BlockSpec((1, HEAD_PAD), lambda i: (0, 0)), tab_spec, tab_spec, row_spec],
        out_specs=[row_spec, pl.BlockSpec((SUBLANES, HEAD_PAD), lambda i: (0, 0))],
        out_shape=[jax.ShapeDtypeStruct((s, width), F32), jax.ShapeDtypeStruct((SUBLANES, HEAD_PAD), F32)])(raw, g, cos, sin, dout)


def _k_assemble(kn, ckv, *, name):
    s = kn.shape[0]
    tm = _tile(s, 512)
    width = N_HEADS * HEAD_PAD

    def body(kn_ref, pe_ref, o_ref):
        pe = pe_ref[...]
        for h in range(N_HEADS):
            o_ref[:, h * HEAD_PAD:h * HEAD_PAD + NOPE] = kn_ref[:, h * NOPE:(h + 1) * NOPE]
            o_ref[:, h * HEAD_PAD + NOPE:(h + 1) * HEAD_PAD] = pe

    return pl.pallas_call(
        body, name=name, grid=(s // tm,),
        in_specs=[pl.BlockSpec((tm, N_HEADS * NOPE), lambda i: (i, 0)),
                  pl.BlockSpec((tm, LANES), lambda i: (i, KV_LORA // LANES))],
        out_specs=pl.BlockSpec((tm, width), lambda i: (i, 0)),
        out_shape=jax.ShapeDtypeStruct((s, width), F32))(kn, ckv)


def _k_disassemble(dk_raw, *, name):
    s = dk_raw.shape[0]
    tm = _tile(s, 512)
    width = N_HEADS * HEAD_PAD

    def body(dk_ref, dkn_ref, dpe_ref):
        pe = dk_ref[:, NOPE:HEAD_PAD]
        for h in range(N_HEADS):
            dkn_ref[:, h * NOPE:(h + 1) * NOPE] = dk_ref[:, h * HEAD_PAD:h * HEAD_PAD + NOPE]
            if h:
                pe = pe + dk_ref[:, h * HEAD_PAD + NOPE:(h + 1) * HEAD_PAD]
        dpe_ref[...] = pe

    return pl.pallas_call(
        body, name=name, grid=(s // tm,),
        in_specs=[pl.BlockSpec((tm, width), lambda i: (i, 0))],
        out_specs=[pl.BlockSpec((tm, N_HEADS * NOPE), lambda i: (i, 0)), pl.BlockSpec((tm, LANES), lambda i: (i, 0))],
        out_shape=[jax.ShapeDtypeStruct((s, N_HEADS * NOPE), F32), jax.ShapeDtypeStruct((s, LANES), F32)])(dk_raw)


ATTN_SCALE = 1.0 / math.sqrt(QK_DIM)
MASKED = -1e30


ATTN_TILE = 512
ATTN_HEADS = 8


def _chunk_mask(q0, k0, shape, q_axis):
    qpos = q0 + lax.broadcasted_iota(jnp.int32, shape, q_axis)
    kpos = k0 + lax.broadcasted_iota(jnp.int32, shape, 1 - q_axis)
    return kpos // CHUNK <= qpos // CHUNK


LOG2E = math.log2(math.e)
SCORE_LOG2 = ATTN_SCALE * LOG2E


def _causal_pairs(n, by_key):
    if by_key:
        pairs = [(i, j) for j in range(n) for i in range(j, n)]
    else:
        pairs = [(i, j) for i in range(n) for j in range(i + 1)]
    return jnp.asarray([p[0] for p in pairs], jnp.int32), jnp.asarray([p[1] for p in pairs], jnp.int32)


def _attn_fwd(q, k, vt, *, name):
    s = q.shape[0]
    t = _tile(s, ATTN_TILE)
    n = s // t
    qi_tab, kj_tab = _causal_pairs(n, by_key=False)

    hg = ATTN_HEADS

    def body(qi_ref, kj_ref, q_ref, k_ref, vt_ref, o_ref, lse_ref, m_sc, l_sc, acc):
        pair = pl.program_id(1)
        qi, kj = qi_ref[pair], kj_ref[pair]

        @pl.when(kj == 0)
        def _():
            m_sc[...] = jnp.full_like(m_sc, MASKED)
            l_sc[...] = jnp.zeros_like(l_sc)
            acc[...] = jnp.zeros_like(acc)

        def step(masked):
            for g in range(hg):
                qk, vr = slice(g * HEAD_PAD, (g + 1) * HEAD_PAD), slice(g * V_DIM, (g + 1) * V_DIM)
                st = _nt(k_ref[:, qk], q_ref[:, qk])
                if masked:
                    st = jnp.where(_chunk_mask(qi * t, kj * t, (t, t), 1), st, MASKED)
                m_prev = m_sc[g]
                m_new = jnp.maximum(m_prev, jnp.max(st, axis=0, keepdims=True) * SCORE_LOG2)
                alpha = jnp.exp2(m_prev - m_new)
                pt = jnp.exp2(st * SCORE_LOG2 - m_new)
                l_new = alpha * l_sc[g] + jnp.sum(pt, axis=0, keepdims=True)
                a_new = alpha * acc[vr, :] + _nn(vt_ref[vr, :], pt)
                l_sc[g] = l_new
                acc[vr, :] = a_new
                m_sc[g] = m_new
                if masked:
                    o_ref[vr, :] = a_new / l_new
                    lse_ref[g] = m_new + jnp.log(l_new) * LOG2E

        @pl.when(kj < qi)
        def _():
            step(False)

        @pl.when(kj == qi)
        def _():
            step(True)

    return pl.pallas_call(
        body, name=name,
        grid_spec=pltpu.PrefetchScalarGridSpec(
            num_scalar_prefetch=2, grid=(N_HEADS // hg, int(qi_tab.shape[0])),
            in_specs=[pl.BlockSpec((t, hg * HEAD_PAD), lambda h, p, qi, kj: (qi[p], h)),
                      pl.BlockSpec((t, hg * HEAD_PAD), lambda h, p, qi, kj: (kj[p], h)),
                      pl.BlockSpec((hg * V_DIM, t), lambda h, p, qi, kj: (h, kj[p]))],
            out_specs=[pl.BlockSpec((hg * V_DIM, t), lambda h, p, qi, kj: (h, qi[p])),
                       pl.BlockSpec((hg, 1, t), lambda h, p, qi, kj: (h, 0, qi[p]))],
            scratch_shapes=[pltpu.VMEM((hg, 1, t), F32), pltpu.VMEM((hg, 1, t), F32), pltpu.VMEM((hg * V_DIM, t), F32)]),
        out_shape=[jax.ShapeDtypeStruct((N_HEADS * V_DIM, s), F32), jax.ShapeDtypeStruct((N_HEADS, 1, s), F32)])(qi_tab, kj_tab, q, k, vt)


def _attn_delta(ot, dot, *, name):
    s = ot.shape[1]
    t = _tile(s, 1024)

    def body(o_ref, do_ref, d_ref):
        d_ref[...] = jnp.sum(o_ref[...] * do_ref[...], axis=0, keepdims=True)

    blk = pl.BlockSpec((V_DIM, t), lambda h, i: (h, i))
    return pl.pallas_call(
        body, name=name, grid=(N_HEADS, s // t), in_specs=[blk, blk],
        out_specs=pl.BlockSpec((None, 1, t), lambda h, i: (h, 0, i)),
        out_shape=jax.ShapeDtypeStruct((N_HEADS, 1, s), F32))(ot, dot)


def _attn_bwd_dq(q, k, v, do, lse_col, delta_col, *, name):
    s = q.shape[0]
    t = _tile(s, ATTN_TILE)
    n = s // t
    qi_tab, kj_tab = _causal_pairs(n, by_key=False)

    hg = ATTN_HEADS

    def body(qi_ref, kj_ref, q_ref, k_ref, v_ref, do_ref, lse_ref, dl_ref, dq_ref, acc):
        pair = pl.program_id(1)
        qi, kj = qi_ref[pair], kj_ref[pair]

        @pl.when(kj == 0)
        def _():
            acc[...] = jnp.zeros_like(acc)

        def step(masked):
            for g in range(hg):
                qk, vc = slice(g * HEAD_PAD, (g + 1) * HEAD_PAD), slice(g * V_DIM, (g + 1) * V_DIM)
                kv = k_ref[:, qk]
                sc = _nt(q_ref[:, qk], kv)
                if masked:
                    sc = jnp.where(_chunk_mask(qi * t, kj * t, (t, t), 0), sc, MASKED)
                p = jnp.exp2(sc * SCORE_LOG2 - lse_ref[g])
                dp = _nt(do_ref[:, vc], v_ref[:, vc])
                total = acc[:, qk] + _nn(p * (dp - dl_ref[g]), kv)
                acc[:, qk] = total
                if masked:
                    dq_ref[:, qk] = total * ATTN_SCALE

        @pl.when(kj < qi)
        def _():
            step(False)

        @pl.when(kj == qi)
        def _():
            step(True)

    col = pl.BlockSpec((hg, t, 1), lambda h, p, qi, kj: (h, qi[p], 0))
    return pl.pallas_call(
        body, name=name,
        grid_spec=pltpu.PrefetchScalarGridSpec(
            num_scalar_prefetch=2, grid=(N_HEADS // hg, int(qi_tab.shape[0])),
            in_specs=[pl.BlockSpec((t, hg * HEAD_PAD), lambda h, p, qi, kj: (qi[p], h)),
                      pl.BlockSpec((t, hg * HEAD_PAD), lambda h, p, qi, kj: (kj[p], h)),
                      pl.BlockSpec((t, hg * V_DIM), lambda h, p, qi, kj: (kj[p], h)),
                      pl.BlockSpec((t, hg * V_DIM), lambda h, p, qi, kj: (qi[p], h)), col, col],
            out_specs=pl.BlockSpec((t, hg * HEAD_PAD), lambda h, p, qi, kj: (qi[p], h)),
            scratch_shapes=[pltpu.VMEM((t, hg * HEAD_PAD), F32)]),
        out_shape=jax.ShapeDtypeStruct((s, N_HEADS * HEAD_PAD), F32))(qi_tab, kj_tab, q, k, v, do, lse_col, delta_col)


def _attn_bwd_dkv(q, k, v, do, lse_row, delta_row, dk_in, dv_in, *, name):
    s = q.shape[0]
    t = _tile(s, ATTN_TILE)
    n = s // t
    has_in = dk_in is not None
    hg = ATTN_HEADS
    qi_tab, kj_tab = _causal_pairs(n, by_key=True)

    def body(qi_ref, kj_ref, *refs):
        if has_in:
            q_ref, k_ref, v_ref, do_ref, lse_ref, dl_ref, dki_ref, dvi_ref, dk_ref, dv_ref, acck, accv = refs
        else:
            q_ref, k_ref, v_ref, do_ref, lse_ref, dl_ref, dk_ref, dv_ref, acck, accv = refs
        pair = pl.program_id(1)
        qi, kj = qi_ref[pair], kj_ref[pair]

        def step(masked):
            for g in range(hg):
                qk, vc = slice(g * HEAD_PAD, (g + 1) * HEAD_PAD), slice(g * V_DIM, (g + 1) * V_DIM)
                qv, dov = q_ref[:, qk], do_ref[:, vc]
                st = _nt(k_ref[:, qk], qv)
                if masked:
                    st = jnp.where(_chunk_mask(qi * t, kj * t, (t, t), 1), st, MASKED)
                pt = jnp.exp2(st * SCORE_LOG2 - lse_ref[g])
                accv[:, vc] += _nn(pt, dov)
                dpt = _nt(v_ref[:, vc], dov)
                acck[:, qk] += _nn(pt * (dpt - dl_ref[g]), qv)

        @pl.when(qi == kj)
        def _():
            acck[...] = jnp.zeros_like(acck)
            accv[...] = jnp.zeros_like(accv)
            step(True)

        @pl.when(qi > kj)
        def _():
            step(False)

        @pl.when(qi == n - 1)
        def _():
            dk = acck[...] * ATTN_SCALE
            dv = accv[...]
            if has_in:
                dk = dki_ref[...] + dk
                dv = dvi_ref[...] + dv
            dk_ref[...] = dk
            dv_ref[...] = dv

    row = pl.BlockSpec((hg, 1, t), lambda h, p, qi, kj: (h, 0, qi[p]))
    k_spec = pl.BlockSpec((t, hg * HEAD_PAD), lambda h, p, qi, kj: (kj[p], h))
    v_spec = pl.BlockSpec((t, hg * V_DIM), lambda h, p, qi, kj: (kj[p], h))
    in_specs = [pl.BlockSpec((t, hg * HEAD_PAD), lambda h, p, qi, kj: (qi[p], h)), k_spec, v_spec,
                pl.BlockSpec((t, hg * V_DIM), lambda h, p, qi, kj: (qi[p], h)), row, row]
    args = [q, k, v, do, lse_row, delta_row]
    if has_in:
        in_specs += [k_spec, v_spec]
        args += [dk_in, dv_in]
    return pl.pallas_call(
        body, name=name,
        grid_spec=pltpu.PrefetchScalarGridSpec(
            num_scalar_prefetch=2, grid=(N_HEADS // hg, int(qi_tab.shape[0])), in_specs=in_specs, out_specs=[k_spec, v_spec],
            scratch_shapes=[pltpu.VMEM((t, hg * HEAD_PAD), F32), pltpu.VMEM((t, hg * V_DIM), F32)]),
        out_shape=[jax.ShapeDtypeStruct((s, N_HEADS * HEAD_PAD), F32), jax.ShapeDtypeStruct((s, N_HEADS * V_DIM), F32)])(qi_tab, kj_tab, *args)


def _loss_head(y, target, *, name):
    s, d = y.shape
    tm = _tile(s, 512)

    def body(y_ref, t_ref, dy_ref, l_ref):
        @pl.when(pl.program_id(0) == 0)
        def _():
            l_ref[...] = jnp.zeros_like(l_ref)

        err = y_ref[...] - t_ref[...]
        dy_ref[...] = err * (1.0 / d)
        sq = _fold8(err * err)
        part = sq[:, 0:LANES]
        for cb in range(1, d // LANES):
            part = part + sq[:, cb * LANES:(cb + 1) * LANES]
        l_ref[...] += part * (0.5 / d)

    row_spec = pl.BlockSpec((tm, d), lambda i: (i, 0))
    return pl.pallas_call(
        body, name=name, grid=(s // tm,), in_specs=[row_spec, row_spec],
        out_specs=[row_spec, pl.BlockSpec((SUBLANES, LANES), lambda i: (0, 0))],
        out_shape=[jax.ShapeDtypeStruct((s, d), F32), jax.ShapeDtypeStruct((SUBLANES, LANES), F32)])(y, target)


ADAMW_ROWS = 512


def _adamw_math(w, m, v, g):
    mn = ADAM_B1 * m + (1.0 - ADAM_B1) * g
    vn = ADAM_B2 * v + (1.0 - ADAM_B2) * (g * g)
    m_hat = mn / (1.0 - ADAM_B1 ** ADAM_STEP)
    v_hat = vn / (1.0 - ADAM_B2 ** ADAM_STEP)
    return -ADAM_LR * (m_hat / (jnp.sqrt(v_hat) + ADAM_EPS) + ADAM_WD * w), mn, vn


def _adamw_vectors(ws, ms, vs, gs, *, name):
    n = len(ws)

    def body(*refs):
        ins, outs = refs[:4 * n], refs[4 * n:]
        for a in range(n):
            g = ins[3 * n + a][...]
            outs[a][...] = g
            outs[n + a][...], outs[2 * n + a][...], outs[3 * n + a][...] = _adamw_math(ins[a][...], ins[n + a][...], ins[2 * n + a][...], g)

    shapes = [jax.ShapeDtypeStruct(w.shape, F32) for w in ws]
    out = pl.pallas_call(body, name=name, in_specs=[VMEM_SPEC] * (4 * n), out_specs=[VMEM_SPEC] * (4 * n),
                         out_shape=shapes * 4)(*ws, *ms, *vs, *gs)
    return out[:n], out[n:2 * n], out[2 * n:3 * n], out[3 * n:]


def _adamw(w, m, v, g_parts, *, name):
    rows, cols = w.shape
    tm = _tile(rows, ADAMW_ROWS)
    n_parts = len(g_parts)

    def body(*refs):
        w_ref, m_ref, v_ref = refs[:3]
        g_refs = refs[3:3 + n_parts]
        g_out, d_out, m_out, v_out = refs[3 + n_parts:]
        g = g_refs[0][...]
        for r in g_refs[1:]:
            g = g + r[...]
        g_out[...] = g
        d_out[...], m_out[...], v_out[...] = _adamw_math(w_ref[...], m_ref[...], v_ref[...], g)

    spec = pl.BlockSpec((tm, cols), lambda i: (i, 0))
    out = jax.ShapeDtypeStruct((rows, cols), F32)
    return pl.pallas_call(
        body, name=name, grid=(rows // tm,), in_specs=[spec] * (3 + n_parts),
        out_specs=[spec] * 4, out_shape=[out] * 4)(w, m, v, *g_parts)


def _sum_slots(parts, *, name):
    _, rows, cols = parts.shape
    tm = _tile(rows, ROW_TILE)

    def body(p_ref, o_ref):
        acc = p_ref[0].astype(F32)
        for k in range(1, N_SHARD):
            acc = acc + p_ref[k].astype(F32)
        o_ref[...] = acc

    return pl.pallas_call(
        body, name=name, grid=(rows // tm,),
        in_specs=[pl.BlockSpec((N_SHARD, tm, cols), lambda i: (0, i, 0))],
        out_specs=pl.BlockSpec((tm, cols), lambda i: (i, 0)),
        out_shape=jax.ShapeDtypeStruct((rows, cols), F32))(parts)


def _mesh_pos():
    return lax.axis_index("x"), lax.axis_index("y"), lax.axis_index("c")


CHIP_FLIPS = ((1, 0), (0, 1), (1, 1))


class Exchange(NamedTuple):
    kind: str
    srcs: tuple
    lands: tuple
    layer: Any = None


HBM_SPEC = pl.BlockSpec(memory_space=pltpu.HBM)
SEM_SPEC = pl.BlockSpec(memory_space=pltpu.SEMAPHORE)
DATAFLOW = pltpu.SideEffectType.DATAFLOW_SIDE_EFFECTING


def _exchange_copies(ex, src_refs, land_refs, send_sems, recv_sems):
    x, y, c = _mesh_pos()
    mine = 2 * x + y

    def slot(ref, chip):
        return ref.at[chip] if ex.layer is None else ref.at[chip, ex.layer]

    pairs = []
    for a, (src, land) in enumerate(zip(src_refs, land_refs)):
        for k, (fx, fy) in enumerate(CHIP_FLIPS):
            px, py = x ^ fx, y ^ fy
            peer = 2 * px + py
            src_part = src if ex.kind == "gather" else src.at[peer]
            pair = a * len(CHIP_FLIPS) + k
            common = dict(src_ref=src_part, send_sem=send_sems.at[pair], recv_sem=recv_sems.at[pair],
                          device_id=(px, py, c), device_id_type=MESH)
            pairs.append((pltpu.make_async_remote_copy(dst_ref=slot(land, mine), **common),
                          pltpu.make_async_remote_copy(dst_ref=slot(land, peer), **common)))
    return pairs


def _exchange_start(exchanges, *, name):
    srcs = [s for ex in exchanges for s in ex.srcs]
    lands = [b for ex in exchanges for b in ex.lands]
    n_arr, n_ex = len(srcs) + len(lands), len(exchanges)

    def body(*refs):
        src_refs, land_refs = refs[:len(srcs)], refs[len(srcs):n_arr]
        sems, token = refs[n_arr:n_arr + 2 * n_ex], refs[-1]
        at = 0
        for e, ex in enumerate(exchanges):
            n = len(ex.srcs)
            for send, _ in _exchange_copies(ex, src_refs[at:at + n], land_refs[at:at + n], sems[2 * e], sems[2 * e + 1]):
                send.start()
            at += n
        token[...] = jnp.zeros_like(token)

    sem_shapes = [pltpu.SemaphoreType.DMA((len(ex.srcs) * len(CHIP_FLIPS),)) for ex in exchanges for _ in range(2)]
    out = pl.pallas_call(
        body, name=name,
        out_shape=sem_shapes + [pltpu.HBM(a.shape, a.dtype) for a in srcs + lands] + [jax.ShapeDtypeStruct((SUBLANES, LANES), F32)],
        in_specs=[HBM_SPEC] * n_arr, out_specs=[SEM_SPEC] * (2 * n_ex) + [HBM_SPEC] * n_arr + [VMEM_SPEC],
        input_output_aliases={i: 2 * n_ex + i for i in range(n_arr)},
        compiler_params=pltpu.CompilerParams(has_side_effects=DATAFLOW),
    )(*[pltpu.with_memory_space_constraint(a, pltpu.HBM) for a in srcs + lands])
    sems, thru = out[:2 * n_ex], out[2 * n_ex:-1]
    pending, at = [], 0
    for e, ex in enumerate(exchanges):
        n = len(ex.srcs)
        pending.append((ex._replace(srcs=tuple(thru[at:at + n]), lands=tuple(thru[len(srcs) + at:len(srcs) + at + n])),
                        sems[2 * e], sems[2 * e + 1]))
        at += n
    return pending, out[-1]


def _exchange_wait(pending, after, *, name):
    ex, send_sems, recv_sems = pending
    n = len(ex.srcs)
    after = list(after) if isinstance(after, (list, tuple)) else [after]

    def body(*refs):
        src_refs, land_refs = refs[:n], refs[n:2 * n]
        for send, arrive in _exchange_copies(ex, src_refs, land_refs, refs[2 * n], refs[2 * n + 1]):
            send.wait_send()
            arrive.wait_recv()

    arrays = list(ex.srcs) + list(ex.lands)
    out = pl.pallas_call(
        body, name=name, out_shape=[pltpu.HBM(a.shape, a.dtype) for a in arrays],
        in_specs=[HBM_SPEC] * (2 * n) + [SEM_SPEC, SEM_SPEC] + [ANY] * len(after), out_specs=[HBM_SPEC] * (2 * n),
        input_output_aliases={i: i for i in range(2 * n)},
        compiler_params=pltpu.CompilerParams(has_side_effects=DATAFLOW),
    )(*arrays, send_sems, recv_sems, *after)
    return out[n:]


def _swap_with_sibling(arrays, *, name):
    n = len(arrays)

    def body(*refs):
        ins, outs = refs[:n], refs[n:2 * n]
        send_sems, recv_sems = refs[2 * n:]
        x, y, c = _mesh_pos()
        copies = []
        for a in range(n):
            cp = pltpu.make_async_remote_copy(
                src_ref=ins[a], dst_ref=outs[a], send_sem=send_sems.at[a], recv_sem=recv_sems.at[a],
                device_id=(x, y, 1 - c), device_id_type=MESH)
            cp.start()
            copies.append(cp)
        for cp in copies:
            cp.wait()

    return pl.pallas_call(
        body, name=name, in_specs=[ANY] * n, out_specs=[ANY] * n,
        out_shape=[jax.ShapeDtypeStruct(a.shape, a.dtype) for a in arrays],
        scratch_shapes=[pltpu.SemaphoreType.DMA((n,)), pltpu.SemaphoreType.DMA((n,))])(*arrays)


N_DEV = 8


def _all_reduce_small(vec, *, after=(), name):
    rows = vec.shape[0]

    def body(v_ref, *refs):
        o_ref, land, send_sems, recv_sems = refs[len(after):]
        x, y, c = _mesh_pos()
        me = 4 * x + 2 * y + c
        land[me] = v_ref[...]
        copies = []
        for k in range(1, N_DEV):
            fx, fy, fc = (k >> 2) & 1, (k >> 1) & 1, k & 1
            px, py, pc = x ^ fx, y ^ fy, c ^ fc
            send = pltpu.make_async_remote_copy(
                src_ref=v_ref, dst_ref=land.at[me], send_sem=send_sems.at[k - 1], recv_sem=recv_sems.at[k - 1],
                device_id=(px, py, pc), device_id_type=MESH)
            send.start()
            arrive = pltpu.make_async_remote_copy(
                src_ref=v_ref, dst_ref=land.at[4 * px + 2 * py + pc], send_sem=send_sems.at[k - 1], recv_sem=recv_sems.at[k - 1],
                device_id=(px, py, pc), device_id_type=MESH)
            copies.append((send, arrive))
        for send, arrive in copies:
            send.wait_send()
            arrive.wait_recv()
        acc = land[0]
        for k in range(1, N_DEV):
            acc = acc + land[k]
        o_ref[...] = acc

    return pl.pallas_call(
        body, name=name, in_specs=[VMEM_SPEC] + [ANY] * len(after), out_specs=VMEM_SPEC,
        out_shape=jax.ShapeDtypeStruct(vec.shape, F32),
        scratch_shapes=[pltpu.VMEM((N_DEV, rows, LANES), F32), pltpu.SemaphoreType.DMA((N_DEV - 1,)),
                        pltpu.SemaphoreType.DMA((N_DEV - 1,))])(vec, *after)


PACK_UNIT = SUBLANES * LANES * 2


def _padded(n):
    return -(-n // PACK_UNIT) * PACK_UNIT


def _pack(arrays, dtype, lead=0):
    parts = []
    for a in arrays:
        lead_shape = a.shape[:lead]
        flat = a.astype(dtype).reshape(lead_shape + (-1,))
        n = flat.shape[-1]
        flat = jnp.pad(flat, [(0, 0)] * lead + [(0, _padded(n) - n)])
        parts.append(flat.reshape(lead_shape + (-1, LANES)))
    return jnp.concatenate(parts, axis=lead)


def _unpack(buf, shapes, lead=0):
    out, row = [], 0
    for shp in shapes:
        n = math.prod(shp)
        rows = _padded(n) // LANES
        part = lax.slice_in_dim(buf, row, row + rows, axis=lead)
        lead_shape = part.shape[:lead]
        part = part.reshape(lead_shape + (-1,))
        part = lax.slice_in_dim(part, 0, n, axis=lead)
        out.append(part.reshape(lead_shape + tuple(shp)))
        row += rows
    return out


def kernel(x, positions, ln_mix_a, w_pool, b_pool, pool_scale, ln_ffn, w_gate, w_up, w_down, ln_kv, w_dkv, g_kv_latent, w_uk, w_uv, g_k, ln_mix_b, w_dq, g_q_latent, w_uq, g_q, w_o, loss_target, m_ln_mix_a, m_w_pool, m_b_pool, m_pool_scale, m_ln_ffn, m_w_gate, m_w_up, m_w_down, m_ln_kv, m_w_dkv, m_g_kv_latent, m_w_uk, m_w_uv, m_g_k, m_ln_mix_b, m_w_dq, m_g_q_latent, m_w_uq, m_g_q, m_w_o, v_ln_mix_a, v_w_pool, v_b_pool, v_pool_scale, v_ln_ffn, v_w_gate, v_w_up, v_w_down, v_ln_kv, v_w_dkv, v_g_kv_latent, v_w_uk, v_w_uv, v_g_k, v_ln_mix_b, v_w_dq, v_g_q_latent, v_w_uq, v_g_q, v_w_o):
    weights = dict(ln_mix_a=ln_mix_a, w_pool=w_pool, b_pool=b_pool, pool_scale=pool_scale, ln_ffn=ln_ffn, w_gate=w_gate,
                   w_up=w_up, w_down=w_down, ln_kv=ln_kv, w_dkv=w_dkv, g_kv_latent=g_kv_latent, w_uk=w_uk, w_uv=w_uv, g_k=g_k,
                   ln_mix_b=ln_mix_b, w_dq=w_dq, g_q_latent=g_q_latent, w_uq=w_uq, g_q=g_q, w_o=w_o)
    mom_m = dict(ln_mix_a=m_ln_mix_a, w_pool=m_w_pool, b_pool=m_b_pool, pool_scale=m_pool_scale, ln_ffn=m_ln_ffn,
                 w_gate=m_w_gate, w_up=m_w_up, w_down=m_w_down, ln_kv=m_ln_kv, w_dkv=m_w_dkv, g_kv_latent=m_g_kv_latent,
                 w_uk=m_w_uk, w_uv=m_w_uv, g_k=m_g_k, ln_mix_b=m_ln_mix_b, w_dq=m_w_dq, g_q_latent=m_g_q_latent,
                 w_uq=m_w_uq, g_q=m_g_q, w_o=m_w_o)
    mom_v = dict(ln_mix_a=v_ln_mix_a, w_pool=v_w_pool, b_pool=v_b_pool, pool_scale=v_pool_scale, ln_ffn=v_ln_ffn,
                 w_gate=v_w_gate, w_up=v_w_up, w_down=v_w_down, ln_kv=v_ln_kv, w_dkv=v_w_dkv, g_kv_latent=v_g_kv_latent,
                 w_uk=v_w_uk, w_uv=v_w_uv, g_k=v_g_k, ln_mix_b=v_ln_mix_b, w_dq=v_w_dq, g_q_latent=v_g_q_latent,
                 w_uq=v_w_uq, g_q=v_g_q, w_o=v_w_o)
    order = list(weights)
    s = x.shape[1]
    d = D_MODEL
    xs = x.reshape(s, d)
    target = loss_target.reshape(s, d)
    my_chip = 2 * lax.axis_index("x") + lax.axis_index("y")

    mat_names = ("w_pool", "w_dkv", "w_uk", "w_uv", "w_dq", "w_uq", "w_o")
    vec_names = ("ln_mix_a", "b_pool", "pool_scale")
    mat_shapes = [weights[n].shape for n in mat_names]
    vec_shapes = [weights[n].shape for n in vec_names]

    def rows_of(a, lead=0):
        return a.reshape(a.shape[:lead] + (-1, a.shape[-1]))

    mats_local = tuple(rows_of(weights[n].astype(WIRE_DTYPE)) for n in mat_names)
    vecs_local = _pack([weights[n] for n in vec_names], F32)

    def landing(shard):
        return lax.dynamic_update_slice_in_dim(lax.empty((N_SHARD,) + shard.shape, shard.dtype), shard[None], my_chip, axis=0)

    def gather_of(shards):
        return Exchange("gather", tuple(shards), tuple(landing(sh) for sh in shards))

    def ffn_gathers(l):
        return [gather_of(tuple(w[l].astype(WIRE_DTYPE) for w in (w_gate, w_up))), gather_of((w_down[l].astype(WIRE_DTYPE),))]

    gathers = [gather_of(mats_local[:1] + (vecs_local,))]
    ffn_at = {}
    for l in range(DEPTH):
        if l == N_A:
            attn_at = len(gathers)
            gathers.append(gather_of(mats_local[1:]))
        ffn_at[l] = len(gathers)
        gathers += ffn_gathers(l)
    gathering, _ = _exchange_start(gathers, name="gather_start")

    inv = ROPE_THETA ** (-jnp.arange(ROPE // 2, dtype=F32) * 2.0 / ROPE)
    inv_lanes = jnp.concatenate([inv, inv, jnp.zeros((LANES - ROPE,), F32)]).reshape(1, LANES)
    cos_t, sin_t = _rope_tables(positions.reshape(s, 1).astype(F32), inv_lanes, name="rope_tables")

    g_pool, vecs_all = _exchange_wait(gathering[0], cos_t, name="gather_wait_small")
    g_lna, g_bp, g_ps = _unpack(vecs_all, vec_shapes, lead=1)
    wpool_f = g_pool.reshape((N_SHARD,) + mat_shapes[0]).transpose(1, 2, 0, 3, 4).reshape(N_A, N_GROUPS, GROUP_DIM, GROUP_DIM)
    bpool_f = g_bp.transpose(1, 2, 0, 3).reshape(N_A, 1, d)
    pscale_f = g_ps.transpose(1, 0, 2).reshape(N_A, 1, d)
    lna_f = g_lna.transpose(1, 0, 2).reshape(N_A, 1, d)

    def head_gain(g):
        return jnp.pad(g.reshape(1, QK_DIM), ((0, 0), (0, HEAD_PAD - QK_DIM)))

    ffn_w = [None] * DEPTH
    ffn_names = ("w_gate", "w_up", "w_down")
    ffn_2d = {nm: tuple(rows_of(src[nm]) for src in (weights, mom_m, mom_v)) for nm in ffn_names}
    moment_views = {0: ffn_2d["w_gate"][1:], 1: ffn_2d["w_up"][1:]}

    def ffn_fwd(xin, layer):
        hf = _rms_fwd(xin, ln_ffn[layer].reshape(1, d), n=d, name="ffn_norm")
        wg, wu = _exchange_wait(gathering[ffn_at[layer]], [hf, *moment_views.get(layer, ())], name=f"gather_wait_up_{layer}")
        a, b, u = _ffn_up(hf, wg, wu, name="ffn_up")
        (wd,) = _exchange_wait(gathering[ffn_at[layer] + 1], u, name=f"gather_wait_down_{layer}")
        ffn_w[layer] = wg, wu, wd
        return _ffn_down(u, wd, xin, name="ffn_down"), (xin, hf, a, b, u)

    saved_a, saved_b, saved_f = [], [], []
    cur = xs
    for l in range(N_A):
        dpool = _rms_pool_fwd(cur, lna_f[l], name="pool_fwd")
        x1 = _pool_mm_fwd(dpool, wpool_f[l], bpool_f[l], pscale_f[l], cur, name="pool_mm")
        saved_a.append((cur, dpool))
        cur, sf = ffn_fwd(x1, l)
        saved_f.append(sf)

    x_kv = cur
    hk = _rms_fwd(x_kv, ln_kv.reshape(1, d), n=d, name="kv_norm")
    g_dkv, g_uk, g_uv, g_dq, g_uq, g_o = (a.reshape((N_SHARD,) + shp) for a, shp in zip(
        _exchange_wait(gathering[attn_at], hk, name="gather_wait_attn"), mat_shapes[1:]))
    wdkv_f = jnp.pad(g_dkv.reshape(d, KV_LORA + ROPE), ((0, 0), (0, CKV_PAD - KV_LORA - ROPE)))
    wuk_f = g_uk.transpose(1, 0, 2).reshape(KV_LORA, N_HEADS * NOPE)
    wuv_f = g_uv.transpose(1, 0, 2).reshape(KV_LORA, N_HEADS * V_DIM)
    wdq_f = g_dq.transpose(1, 0, 2, 3).reshape(N_B, d, Q_LORA)
    wuq_f = jnp.pad(g_uq.transpose(1, 2, 0, 3).reshape(N_B, Q_LORA, N_HEADS, QK_DIM),
                    ((0, 0), (0, 0), (0, 0), (0, HEAD_PAD - QK_DIM))).reshape(N_B, Q_LORA, N_HEADS * HEAD_PAD)
    wo_f = g_o.transpose(1, 0, 2, 3).reshape(N_B, d, d)
    ckv = _mm(hk, wdkv_f, name="kv_down")
    c_lat = _rms_fwd(ckv, g_kv_latent.reshape(1, KV_LORA), n=KV_LORA, name="kv_latent_norm")
    kn_raw = _mm(c_lat, wuk_f, name="k_up")
    v_all = _mm(c_lat, wuv_f, out_dtype=MXU_DTYPE, name="v_up")
    vt_all = _mm(wuv_f.T, c_lat, tb=True, out_dtype=MXU_DTYPE, name="v_up_t")
    k_raw = _k_assemble(kn_raw, ckv, name="k_assemble")
    gk_pad = head_gain(g_k)
    k_cat = _head_norm_rope_fwd(k_raw, gk_pad, cos_t, sin_t, name="k_norm_rope")

    for j in range(N_B):
        l = N_A + j
        hq = _rms_fwd(cur, ln_mix_b[j].reshape(1, d), n=d, name="q_norm")
        cq_raw = _mm(hq, wdq_f[j], name="q_down")
        cq = _rms_fwd(cq_raw, g_q_latent[j].reshape(1, Q_LORA), n=Q_LORA, name="q_latent_norm")
        q_raw = _mm(cq, wuq_f[j], name="q_up")
        gq_pad = head_gain(g_q[j])
        q_cat = _head_norm_rope_fwd(q_raw, gq_pad, cos_t, sin_t, name="q_norm_rope")
        ot, lse = _attn_fwd(q_cat, k_cat, vt_all, name="attn_fwd")
        x1 = _mm(ot, wo_f[j], ta=True, resid=cur, name="attn_out")
        saved_b.append((cur, hq, cq_raw, cq, q_raw, gq_pad, q_cat, ot, lse))
        cur, sf = ffn_fwd(x1, l)
        saved_f.append(sf)

    dy, loss_part = _loss_head(cur, target, name="loss_head")

    ffn_landed = (lax.empty((N_SHARD, DEPTH, d, FF_SHARD), WIRE_DTYPE), lax.empty((N_SHARD, DEPTH, d, FF_SHARD), WIRE_DTYPE),
                  lax.empty((N_SHARD, DEPTH, FF_SHARD, d), WIRE_DTYPE))
    scattering = None
    grads = {}
    d_ln_ffn = [None] * DEPTH

    def own_part(full):
        return lax.dynamic_index_in_dim(full, my_chip, axis=0, keepdims=True)

    def ffn_bwd(dyv, layer):
        nonlocal ffn_landed, scattering
        xin, hf, a, b, u = saved_f[layer]
        wg, wu, wd = ffn_w[layer]
        da, db = _ffn_bwd_hidden(dyv, wd, a, b, name="ffn_bwd_hidden")
        dwd = _ffn_bwd_dwd(u, dyv, name="ffn_bwd_dwd")
        dwg, dwu = _ffn_bwd_dwgu(hf, da, db, name="ffn_bwd_dwgu")
        if scattering is not None:
            ffn_landed = _exchange_wait(scattering, dwg, name=f"scatter_wait_{layer + 1}")
        ffn_landed = tuple(lax.dynamic_update_slice(buf, own_part(g)[:, None], (my_chip, layer, 0, 0))
                           for buf, g in zip(ffn_landed, (dwg, dwu, dwd)))
        (scattering,), started = _exchange_start([Exchange("scatter", (dwg, dwu, dwd), ffn_landed, layer)],
                                                 name=f"scatter_start_{layer}")
        dhf = _ffn_bwd_dh(da, db, wg, wu, name="ffn_bwd_dh")
        dx, dg = _rms_bwd(xin, ln_ffn[layer].reshape(1, d), dhf, n=d, dx_in=dyv, after=started, name="ffn_norm_bwd")
        d_ln_ffn[layer] = dg.sum(axis=0)
        return dx

    dk_acc = dv_acc = None
    d_ln_mix_b, d_w_dq, d_g_q_latent, d_w_uq, d_g_q, d_w_o = ([None] * N_B for _ in range(6))
    dcur = dy
    for j in reversed(range(N_B)):
        l = N_A + j
        xin, hq, cq_raw, cq, q_raw, gq_pad, q_cat, ot, lse = saved_b[j]
        dx1 = ffn_bwd(dcur, l)
        do = _mm(dx1, wo_f[j], tb=True, out_dtype=MXU_DTYPE, name="attn_out_bwd")
        dot = _mm(wo_f[j], dx1, tb=True, name="attn_out_bwd_t")
        d_w_o[j] = _mm_tn(ot, dx1, at=True, name="attn_out_dw")
        delta = _attn_delta(ot, dot, name="attn_delta")
        lse_col, delta_col = lse.reshape(N_HEADS, s, 1), delta.reshape(N_HEADS, s, 1)
        dq_cat = _attn_bwd_dq(q_cat, k_cat, v_all, do, lse_col, delta_col, name="attn_bwd_dq")
        dk_acc, dv_acc = _attn_bwd_dkv(q_cat, k_cat, v_all, do, lse, delta, dk_acc, dv_acc, name="attn_bwd_dkv")
        dq_raw, dgq = _head_norm_rope_bwd(q_raw, gq_pad, cos_t, sin_t, dq_cat, name="q_norm_rope_bwd")
        d_g_q[j] = dgq.sum(axis=0)[:QK_DIM]
        dcq = _mm(dq_raw, wuq_f[j], tb=True, name="q_up_bwd")
        d_w_uq[j] = _mm_tn(cq, dq_raw, name="q_up_dw").reshape(Q_LORA, N_HEADS, HEAD_PAD)[:, :, :QK_DIM].reshape(Q_LORA, N_HEADS * QK_DIM)
        dcq_raw, dgl = _rms_bwd(cq_raw, g_q_latent[j].reshape(1, Q_LORA), dcq, n=Q_LORA, name="q_latent_norm_bwd")
        d_g_q_latent[j] = dgl.sum(axis=0)
        dhq = _mm(dcq_raw, wdq_f[j], tb=True, name="q_down_bwd")
        d_w_dq[j] = _mm_tn(hq, dcq_raw, name="q_down_dw")
        dcur, dgm = _rms_bwd(xin, ln_mix_b[j].reshape(1, d), dhq, n=d, dx_in=dx1, name="q_norm_bwd")
        d_ln_mix_b[j] = dgm.sum(axis=0)

    dk_raw, dgk = _head_norm_rope_bwd(k_raw, gk_pad, cos_t, sin_t, dk_acc, name="k_norm_rope_bwd")
    grads["g_k"] = dgk.sum(axis=0)[:QK_DIM]
    dc = _mm(dv_acc, wuv_f, tb=True, name="v_up_bwd")
    grads["w_uv"] = _mm_tn(c_lat, dv_acc, name="v_up_dw")
    dkn, dpe = _k_disassemble(dk_raw, name="k_disassemble")
    dc = _mm(dkn, wuk_f, tb=True, resid=dc, name="k_up_bwd")
    grads["w_uk"] = _mm_tn(c_lat, dkn, name="k_up_dw")
    dc_raw, dgl = _rms_bwd(ckv, g_kv_latent.reshape(1, KV_LORA), dc, n=KV_LORA, name="kv_latent_norm_bwd")
    grads["g_kv_latent"] = dgl.sum(axis=0)
    dckv = jnp.concatenate([dc_raw, dpe], axis=1)
    dhk = _mm(dckv, wdkv_f, tb=True, name="kv_down_bwd")
    grads["w_dkv"] = _mm_tn(hk, dckv, name="kv_down_dw")[:, :KV_LORA + ROPE]
    gm = {
        "w_dkv": grads["w_dkv"].reshape(N_SHARD, d // N_SHARD, KV_LORA + ROPE),
        "w_uk": grads["w_uk"].reshape(KV_LORA, N_SHARD, -1).transpose(1, 0, 2),
        "w_uv": grads["w_uv"].reshape(KV_LORA, N_SHARD, -1).transpose(1, 0, 2),
        "w_dq": jnp.stack(d_w_dq).reshape(N_B, N_SHARD, d // N_SHARD, Q_LORA).transpose(1, 0, 2, 3),
        "w_uq": jnp.stack(d_w_uq).reshape(N_B, Q_LORA, N_SHARD, -1).transpose(2, 0, 1, 3),
        "w_o": jnp.stack(d_w_o).reshape(N_B, N_SHARD, d // N_SHARD, d).transpose(1, 0, 2, 3),
    }

    def scatter_of(partials):
        zones = tuple(lax.dynamic_update_slice_in_dim(lax.empty(g.shape, WIRE_DTYPE), own_part(g), my_chip, axis=0) for g in partials)
        return Exchange("scatter", tuple(partials), zones)

    attn_partials = [rows_of(gm[n].astype(WIRE_DTYPE), lead=1) for n in mat_names[1:]]
    (attn_scatter,), started = _exchange_start([scatter_of(attn_partials)], name="scatter_start_attn")
    dcur, dg = _rms_bwd(x_kv, ln_kv.reshape(1, d), dhk, n=d, dx_in=dcur, after=started, name="kv_norm_bwd")
    grads["ln_kv"] = dg.sum(axis=0)

    d_ln_mix_a, d_w_pool, d_b_pool, d_pool_scale = ([None] * N_A for _ in range(4))
    for l in reversed(range(N_A)):
        xin, dpool = saved_a[l]
        dx1 = ffn_bwd(dcur, l)
        dd, dwp, dbp, dsp = _pool_mm_bwd(dpool, wpool_f[l], bpool_f[l], pscale_f[l], dx1, name="pool_mm_bwd")
        d_w_pool[l], d_b_pool[l], d_pool_scale[l] = dwp, dbp.sum(axis=0), dsp.sum(axis=0)
        dcur, dg = _rms_pool_bwd(xin, lna_f[l], dd, dx1, name="pool_bwd")
        d_ln_mix_a[l] = dg.sum(axis=0)
    grad_x = dcur.reshape(1, s, d)

    pool_partial = jnp.stack(d_w_pool).reshape(N_A, N_GROUPS, N_SHARD, GROUP_DIM // N_SHARD, GROUP_DIM).transpose(2, 0, 1, 3, 4)
    (pool_scatter,), _ = _exchange_start([scatter_of([rows_of(pool_partial.astype(WIRE_DTYPE), lead=1)])], name="scatter_start_small")
    out_g, out_d, out_m, out_v = {}, {}, {}, {}

    def reduce_and_update(names, views, landed, *, tag):
        chip_sums = [_sum_slots(p, name="sum_chips") for p in landed]
        sib_sums = _swap_with_sibling(chip_sums, name=f"swap_sibling_{tag}")
        deltas = []
        for nm, (w2, m2, v2), own, sib in zip(names, views, chip_sums, sib_sums):
            res = _adamw(w2, m2, v2, [own, sib], name=f"adamw_{tag}")
            shp = weights[nm].shape
            out_g[nm], out_d[nm], out_m[nm], out_v[nm] = (r.reshape(shp) for r in res)
            deltas.append(res[1])
        return deltas

    def mat_views(nm):
        return tuple(rows_of(src[nm]) for src in (weights, mom_m, mom_v))

    attn_landed = _exchange_wait(attn_scatter, dcur, name="scatter_wait_attn")
    updated = reduce_and_update(mat_names[1:], [mat_views(n) for n in mat_names[1:]], attn_landed, tag="attn")

    vec_full = {
        "ln_mix_a": jnp.stack(d_ln_mix_a), "b_pool": jnp.stack(d_b_pool).reshape(N_A, N_GROUPS, GROUP_DIM),
        "pool_scale": jnp.stack(d_pool_scale), "ln_ffn": jnp.stack(d_ln_ffn), "ln_kv": grads["ln_kv"],
        "g_kv_latent": grads["g_kv_latent"], "g_k": grads["g_k"], "ln_mix_b": jnp.stack(d_ln_mix_b),
        "g_q_latent": jnp.stack(d_g_q_latent), "g_q": jnp.stack(d_g_q),
    }
    small_names = list(vec_full)
    small_shapes = [vec_full[n].shape for n in small_names] + [(SUBLANES * LANES,)]
    small = _all_reduce_small(_pack([vec_full[n] for n in small_names] + [loss_part.reshape(-1)], F32), after=updated,
                              name="all_reduce_small")
    small_sum = _unpack(small, small_shapes)
    loss = jnp.sum(small_sum[-1])
    vec_grad = dict(zip(small_names, small_sum[:-1]))
    vec_grad["ln_mix_a"] = lax.dynamic_slice_in_dim(vec_grad["ln_mix_a"], my_chip * (d // N_SHARD), d // N_SHARD, axis=1)
    vec_grad["pool_scale"] = lax.dynamic_slice_in_dim(vec_grad["pool_scale"], my_chip * (d // N_SHARD), d // N_SHARD, axis=1)
    vec_grad["b_pool"] = lax.dynamic_slice_in_dim(vec_grad["b_pool"], my_chip * (GROUP_DIM // N_SHARD), GROUP_DIM // N_SHARD, axis=2)

    def as_rows(a):
        return a.reshape(1, -1) if a.ndim == 1 else rows_of(a)

    res = _adamw_vectors([as_rows(weights[n]) for n in small_names], [as_rows(mom_m[n]) for n in small_names],
                         [as_rows(mom_v[n]) for n in small_names],
                         [as_rows(vec_grad[n].reshape(weights[n].shape)) for n in small_names], name="adamw_vectors")
    for tgt, arrs in zip((out_g, out_d, out_m, out_v), res):
        for n, arr in zip(small_names, arrs):
            tgt[n] = arr.reshape(weights[n].shape)

    ffn_landed = _exchange_wait(scattering, updated + [res[1][0]], name="scatter_wait_0")
    pool_landed = _exchange_wait(pool_scatter, ffn_landed[0], name="scatter_wait_small")
    landed = [ffn_landed[0].reshape(N_SHARD, DEPTH * d, FF_SHARD), ffn_landed[1].reshape(N_SHARD, DEPTH * d, FF_SHARD),
              ffn_landed[2].reshape(N_SHARD, DEPTH * FF_SHARD, d), *pool_landed]
    reduce_and_update(ffn_names + mat_names[:1], [ffn_2d[n] for n in ffn_names] + [mat_views(mat_names[0])], landed, tag="ffn")

    return (loss, grad_x, *[out_g[n] for n in order], *[out_d[n] for n in order],
            *[out_m[n] for n in order], *[out_v[n] for n in order])
```
